```python
import jax, jax.numpy as jnp
from jax import lax
import numpy as np

D_MODEL = 1024
BATCH = 16
SEQ = 2048
DEPTH = 2

CHUNK = 64
N_A_LAYERS = DEPTH // 2
N_B_LAYERS = DEPTH - N_A_LAYERS
GMLP_BLOCK = 128
GMLP_WIDTH = D_MODEL
GMLP_GROUPS = 8
GMLP_GROUP_DIM = GMLP_WIDTH // GMLP_GROUPS
N_HEADS = 16
HEAD_DIM = D_MODEL // N_HEADS
LEFT_CHUNKS = 8
BAND = (LEFT_CHUNKS + 1) * CHUNK
PAD = LEFT_CHUNKS * CHUNK
REL_CLIP = 128
REL_SIZE = 2 * REL_CLIP + 1
D_FF = ((8 * D_MODEL // 3 + 127) // 128) * 128
CONV_WIDTH = 3
EPS = 1e-6
NEG_INF = -1e30

kernel_name = "hybrid_gmlp_chunkattn_yoco_convffn"


def rmsnorm(x, g):
    xf = x.astype(jnp.float32)
    y = xf * lax.rsqrt(jnp.mean(xf * xf, axis=-1, keepdims=True) + EPS)
    return (y * g.astype(jnp.float32)).astype(x.dtype)


def gmlp_mixer(h, w_in, v_norm_g, w_s, b_s, w_out):
    B, S, _ = h.shape
    z = jax.nn.gelu(h @ w_in)
    u, v = jnp.split(z, 2, axis=-1)
    v = rmsnorm(v, v_norm_g)
    pos_chunk = jnp.arange(GMLP_BLOCK) // CHUNK
    mask = pos_chunk[:, None] >= pos_chunk[None, :]
    w = jnp.where(mask[None], w_s, 0)
    v = v.reshape(B, S // GMLP_BLOCK, GMLP_BLOCK, GMLP_GROUPS, GMLP_GROUP_DIM)
    s = jnp.einsum('gij,bnjgc->bnigc', w, v) + b_s.T[None, None, :, :, None]
    out = u * s.reshape(B, S, GMLP_WIDTH)
    return out @ w_out


def chunk_attention(h, k, v, w_q, rel_bias, w_o):
    B, S, _ = h.shape
    nc = S // CHUNK
    scale = HEAD_DIM ** -0.5
    q = (h @ w_q).reshape(B, nc, CHUNK, N_HEADS, HEAD_DIM) * scale
    qc = jnp.moveaxis(q, 1, 0)
    kp = jnp.pad(k, ((0, 0), (PAD, 0), (0, 0), (0, 0)))
    vp = jnp.pad(v, ((0, 0), (PAD, 0), (0, 0), (0, 0)))
    qi = jnp.arange(CHUNK)[:, None]
    kj = jnp.arange(BAND)[None, :]
    rel_idx = jnp.clip(qi - kj + PAD, -REL_CLIP, REL_CLIP) + REL_CLIP
    bias = rel_bias[:, rel_idx].astype(jnp.float32)

    def one_chunk(args):
        c, qb = args
        start = c * CHUNK
        kb = lax.dynamic_slice_in_dim(kp, start, BAND, axis=1)
        vb = lax.dynamic_slice_in_dim(vp, start, BAND, axis=1)
        sc = jnp.einsum('bqhd,bkhd->bhqk', qb, kb).astype(jnp.float32) + bias
        valid = (start - PAD + jnp.arange(BAND)) >= 0
        sc = jnp.where(valid[None, None, None, :], sc, NEG_INF)
        p = jax.nn.softmax(sc, axis=-1).astype(vb.dtype)
        return jnp.einsum('bhqk,bkhd->bqhd', p, vb)

    o = lax.map(one_chunk, (jnp.arange(nc), qc))
    o = jnp.moveaxis(o, 0, 1).reshape(B, S, N_HEADS * HEAD_DIM)
    return o @ w_o


def conv_ffn(h, w_in, conv_w, conv_b, w_down):
    a = h @ w_in
    C = a.shape[-1]
    a = lax.conv_general_dilated(
        a, conv_w[:, None, :].astype(a.dtype), window_strides=(1,),
        padding=[(CONV_WIDTH - 1, 0)], dimension_numbers=('NWC', 'WIO', 'NWC'),
        feature_group_count=C) + conv_b
    up, gate = jnp.split(a, 2, axis=-1)
    return (jax.nn.silu(gate) * up) @ w_down


def _fwd_setup_inputs(seed: int = 0) -> dict:
    key = jax.random.key(seed)
    ks = jax.random.split(key, 20)
    nrm = lambda k, shape, s: jax.random.normal(k, shape, jnp.float32) * s
    gain = lambda k, shape: 1.0 + nrm(k, shape, 0.02)
    HD = N_HEADS * HEAD_DIM
    return {
        "x": nrm(ks[0], (BATCH, SEQ, D_MODEL), 1.0),
        "a_norm_g": gain(ks[1], (N_A_LAYERS, D_MODEL)),
        "a_w_in": nrm(ks[2], (N_A_LAYERS, D_MODEL, 2 * GMLP_WIDTH), D_MODEL ** -0.5),
        "a_v_norm_g": gain(ks[3], (N_A_LAYERS, GMLP_WIDTH)),
        "a_w_s": nrm(ks[4], (N_A_LAYERS, GMLP_GROUPS, GMLP_BLOCK, GMLP_BLOCK), GMLP_BLOCK ** -0.5),
        "a_b_s": 1.0 + nrm(ks[5], (N_A_LAYERS, GMLP_GROUPS, GMLP_BLOCK), 0.01),
        "a_w_out": nrm(ks[6], (N_A_LAYERS, GMLP_WIDTH, D_MODEL), GMLP_WIDTH ** -0.5),
        "kv_norm_g": gain(ks[7], (D_MODEL,)),
        "w_kv": nrm(ks[8], (D_MODEL, 2 * HD), D_MODEL ** -0.5),
        "b_norm_g": gain(ks[9], (N_B_LAYERS, D_MODEL)),
        "b_w_q": nrm(ks[10], (N_B_LAYERS, D_MODEL, HD), D_MODEL ** -0.5),
        "b_rel_bias": nrm(ks[11], (N_B_LAYERS, N_HEADS, REL_SIZE), 0.5),
        "b_w_o": nrm(ks[12], (N_B_LAYERS, HD, D_MODEL), HD ** -0.5),
        "f_norm_g": gain(ks[13], (DEPTH, D_MODEL)),
        "f_w_in": nrm(ks[14], (DEPTH, D_MODEL, 2 * D_FF), D_MODEL ** -0.5),
        "f_conv_w": nrm(ks[15], (DEPTH, CONV_WIDTH, 2 * D_FF), CONV_WIDTH ** -0.5),
        "f_conv_b": nrm(ks[16], (DEPTH, 2 * D_FF), 0.01),
        "f_w_down": nrm(ks[17], (DEPTH, D_FF, D_MODEL), D_FF ** -0.5),
        "final_norm_g": gain(ks[18], (D_MODEL,)),
    }


def _fwd_reference(x, a_norm_g, a_w_in, a_v_norm_g, a_w_s, a_b_s, a_w_out,
              kv_norm_g, w_kv, b_norm_g, b_w_q, b_rel_bias, b_w_o,
              f_norm_g, f_w_in, f_conv_w, f_conv_b, f_w_down, final_norm_g):
    B, S, _ = x.shape
    h = x
    k_shared = v_shared = None
    for l in range(DEPTH):
        if l < N_A_LAYERS:
            h = h + gmlp_mixer(rmsnorm(h, a_norm_g[l]), a_w_in[l], a_v_norm_g[l],
                               a_w_s[l], a_b_s[l], a_w_out[l])
        else:
            if l == N_A_LAYERS:
                kv = rmsnorm(h, kv_norm_g) @ w_kv
                k_shared, v_shared = jnp.split(kv, 2, axis=-1)
                k_shared = k_shared.reshape(B, S, N_HEADS, HEAD_DIM)
                v_shared = v_shared.reshape(B, S, N_HEADS, HEAD_DIM)
            j = l - N_A_LAYERS
            h = h + chunk_attention(rmsnorm(h, b_norm_g[j]), k_shared, v_shared,
                                    b_w_q[j], b_rel_bias[j], b_w_o[j])
        h = h + conv_ffn(rmsnorm(h, f_norm_g[l]), f_w_in[l], f_conv_w[l],
                         f_conv_b[l], f_w_down[l])
    return rmsnorm(h, final_norm_g)


import jax as _jax
import jax.numpy as _jnp

TWIN_FORMAT = 'train_step'
FWD_PARAMS = ['x', 'a_norm_g', 'a_w_in', 'a_v_norm_g', 'a_w_s', 'a_b_s', 'a_w_out', 'kv_norm_g', 'w_kv', 'b_norm_g', 'b_w_q', 'b_rel_bias', 'b_w_o', 'f_norm_g', 'f_w_in', 'f_conv_w', 'f_conv_b', 'f_w_down', 'final_norm_g']
TWIN_WEIGHTS = ['a_norm_g', 'a_w_in', 'a_v_norm_g', 'a_w_s', 'a_b_s', 'a_w_out', 'kv_norm_g', 'w_kv', 'b_norm_g', 'b_w_q', 'b_rel_bias', 'b_w_o', 'f_norm_g', 'f_w_in', 'f_conv_w', 'f_conv_b', 'f_w_down', 'final_norm_g']
TWIN_DIFF_INPUT = 'x'
TWIN_INPUTS = ['x', 'a_norm_g', 'a_w_in', 'a_v_norm_g', 'a_w_s', 'a_b_s', 'a_w_out', 'kv_norm_g', 'w_kv', 'b_norm_g', 'b_w_q', 'b_rel_bias', 'b_w_o', 'f_norm_g', 'f_w_in', 'f_conv_w', 'f_conv_b', 'f_w_down', 'final_norm_g', 'loss_target', 'm_a_norm_g', 'm_a_w_in', 'm_a_v_norm_g', 'm_a_w_s', 'm_a_b_s', 'm_a_w_out', 'm_kv_norm_g', 'm_w_kv', 'm_b_norm_g', 'm_b_w_q', 'm_b_rel_bias', 'm_b_w_o', 'm_f_norm_g', 'm_f_w_in', 'm_f_conv_w', 'm_f_conv_b', 'm_f_w_down', 'm_final_norm_g', 'v_a_norm_g', 'v_a_w_in', 'v_a_v_norm_g', 'v_a_w_s', 'v_a_b_s', 'v_a_w_out', 'v_kv_norm_g', 'v_w_kv', 'v_b_norm_g', 'v_b_w_q', 'v_b_rel_bias', 'v_b_w_o', 'v_f_norm_g', 'v_f_w_in', 'v_f_conv_w', 'v_f_conv_b', 'v_f_w_down', 'v_final_norm_g']
TWIN_OUTPUTS = ['loss', 'grad_x', 'grad_a_norm_g', 'grad_a_w_in', 'grad_a_v_norm_g', 'grad_a_w_s', 'grad_a_b_s', 'grad_a_w_out', 'grad_kv_norm_g', 'grad_w_kv', 'grad_b_norm_g', 'grad_b_w_q', 'grad_b_rel_bias', 'grad_b_w_o', 'grad_f_norm_g', 'grad_f_w_in', 'grad_f_conv_w', 'grad_f_conv_b', 'grad_f_w_down', 'grad_final_norm_g', 'delta_a_norm_g', 'delta_a_w_in', 'delta_a_v_norm_g', 'delta_a_w_s', 'delta_a_b_s', 'delta_a_w_out', 'delta_kv_norm_g', 'delta_w_kv', 'delta_b_norm_g', 'delta_b_w_q', 'delta_b_rel_bias', 'delta_b_w_o', 'delta_f_norm_g', 'delta_f_w_in', 'delta_f_conv_w', 'delta_f_conv_b', 'delta_f_w_down', 'delta_final_norm_g', 'new_m_a_norm_g', 'new_m_a_w_in', 'new_m_a_v_norm_g', 'new_m_a_w_s', 'new_m_a_b_s', 'new_m_a_w_out', 'new_m_kv_norm_g', 'new_m_w_kv', 'new_m_b_norm_g', 'new_m_b_w_q', 'new_m_b_rel_bias', 'new_m_b_w_o', 'new_m_f_norm_g', 'new_m_f_w_in', 'new_m_f_conv_w', 'new_m_f_conv_b', 'new_m_f_w_down', 'new_m_final_norm_g', 'new_v_a_norm_g', 'new_v_a_w_in', 'new_v_a_v_norm_g', 'new_v_a_w_s', 'new_v_a_b_s', 'new_v_a_w_out', 'new_v_kv_norm_g', 'new_v_w_kv', 'new_v_b_norm_g', 'new_v_b_w_q', 'new_v_b_rel_bias', 'new_v_b_w_o', 'new_v_f_norm_g', 'new_v_f_w_in', 'new_v_f_conv_w', 'new_v_f_conv_b', 'new_v_f_w_down', 'new_v_final_norm_g']
TWIN_LEAF_KINDS = {'loss': 'loss', 'grad_x': 'grad_x', 'grad_a_norm_g': 'grad_w', 'grad_a_w_in': 'grad_w', 'grad_a_v_norm_g': 'grad_w', 'grad_a_w_s': 'grad_w', 'grad_a_b_s': 'grad_w', 'grad_a_w_out': 'grad_w', 'grad_kv_norm_g': 'grad_w', 'grad_w_kv': 'grad_w', 'grad_b_norm_g': 'grad_w', 'grad_b_w_q': 'grad_w', 'grad_b_rel_bias': 'grad_w', 'grad_b_w_o': 'grad_w', 'grad_f_norm_g': 'grad_w', 'grad_f_w_in': 'grad_w', 'grad_f_conv_w': 'grad_w', 'grad_f_conv_b': 'grad_w', 'grad_f_w_down': 'grad_w', 'grad_final_norm_g': 'grad_w', 'delta_a_norm_g': 'delta_w', 'delta_a_w_in': 'delta_w', 'delta_a_v_norm_g': 'delta_w', 'delta_a_w_s': 'delta_w', 'delta_a_b_s': 'delta_w', 'delta_a_w_out': 'delta_w', 'delta_kv_norm_g': 'delta_w', 'delta_w_kv': 'delta_w', 'delta_b_norm_g': 'delta_w', 'delta_b_w_q': 'delta_w', 'delta_b_rel_bias': 'delta_w', 'delta_b_w_o': 'delta_w', 'delta_f_norm_g': 'delta_w', 'delta_f_w_in': 'delta_w', 'delta_f_conv_w': 'delta_w', 'delta_f_conv_b': 'delta_w', 'delta_f_w_down': 'delta_w', 'delta_final_norm_g': 'delta_w', 'new_m_a_norm_g': 'new_m', 'new_m_a_w_in': 'new_m', 'new_m_a_v_norm_g': 'new_m', 'new_m_a_w_s': 'new_m', 'new_m_a_b_s': 'new_m', 'new_m_a_w_out': 'new_m', 'new_m_kv_norm_g': 'new_m', 'new_m_w_kv': 'new_m', 'new_m_b_norm_g': 'new_m', 'new_m_b_w_q': 'new_m', 'new_m_b_rel_bias': 'new_m', 'new_m_b_w_o': 'new_m', 'new_m_f_norm_g': 'new_m', 'new_m_f_w_in': 'new_m', 'new_m_f_conv_w': 'new_m', 'new_m_f_conv_b': 'new_m', 'new_m_f_w_down': 'new_m', 'new_m_final_norm_g': 'new_m', 'new_v_a_norm_g': 'new_v', 'new_v_a_w_in': 'new_v', 'new_v_a_v_norm_g': 'new_v', 'new_v_a_w_s': 'new_v', 'new_v_a_b_s': 'new_v', 'new_v_a_w_out': 'new_v', 'new_v_kv_norm_g': 'new_v', 'new_v_w_kv': 'new_v', 'new_v_b_norm_g': 'new_v', 'new_v_b_w_q': 'new_v', 'new_v_b_rel_bias': 'new_v', 'new_v_b_w_o': 'new_v', 'new_v_f_norm_g': 'new_v', 'new_v_f_w_in': 'new_v', 'new_v_f_conv_w': 'new_v', 'new_v_f_conv_b': 'new_v', 'new_v_f_w_down': 'new_v', 'new_v_final_norm_g': 'new_v'}


def _forward(args):
    return _fwd_reference(*[args[k] for k in FWD_PARAMS])


def _output_shape():
    out = _jax.eval_shape(lambda: _forward(_fwd_setup_inputs(0)))
    return out.shape, out.dtype

N_MICROBATCH = 1
ADAM_LR = 0.001
ADAM_B1 = 0.9
ADAM_B2 = 0.999
ADAM_EPS = 1e-08
ADAM_WD = 0.01
ADAM_STEP = 10
PER_EXAMPLE_BATCH_AXIS = {'x': 0, 'loss_target': 0}
SHARED_INPUTS = []
_WEIGHT_DTYPES = {'a_norm_g': _jnp.float32, 'a_w_in': _jnp.float32, 'a_v_norm_g': _jnp.float32, 'a_w_s': _jnp.float32, 'a_b_s': _jnp.float32, 'a_w_out': _jnp.float32, 'kv_norm_g': _jnp.float32, 'w_kv': _jnp.float32, 'b_norm_g': _jnp.float32, 'b_w_q': _jnp.float32, 'b_rel_bias': _jnp.float32, 'b_w_o': _jnp.float32, 'f_norm_g': _jnp.float32, 'f_w_in': _jnp.float32, 'f_conv_w': _jnp.float32, 'f_conv_b': _jnp.float32, 'f_w_down': _jnp.float32, 'final_norm_g': _jnp.float32}
MOMENT_SCALE = {'a_norm_g': 1.865145e-01, 'a_w_in': 1.179092e-01, 'a_v_norm_g': 8.726923e-02, 'a_w_s': 8.729403e-02, 'a_b_s': 1.023345e-01, 'a_w_out': 1.436196e-01, 'kv_norm_g': 4.138902e-02, 'w_kv': 2.822940e-02, 'b_norm_g': 1.822273e-02, 'b_w_q': 1.853727e-02, 'b_rel_bias': 1.001365e-02, 'b_w_o': 3.637827e-02, 'f_norm_g': 1.106663e-01, 'f_w_in': 4.563683e-02, 'f_conv_w': 4.487603e-02, 'f_conv_b': 4.845484e-02, 'f_w_down': 7.465428e-02, 'final_norm_g': 3.208656e+01}


def _to_microbatches(a, axis):
    t = _jnp.moveaxis(a, axis, 0)
    t = t.reshape((N_MICROBATCH, t.shape[0] // N_MICROBATCH) + t.shape[1:])
    return _jnp.moveaxis(t, 1, axis + 1)


def setup_inputs(seed: int = 0) -> dict:
    inp = _fwd_setup_inputs(seed)
    key = _jax.random.fold_in(_jax.random.key(seed), 7919)
    shape, _ = _output_shape()
    out = dict(inp)
    out["loss_target"] = _jax.random.normal(_jax.random.fold_in(key, 0), shape, _jnp.float32)
    for i, name in enumerate(TWIN_WEIGHTS):
        w = inp[name].astype(_jnp.float32)
        if MOMENT_SCALE is None:
            s = _jnp.sqrt(_jnp.mean(_jnp.square(w)) + 1e-30)
        else:
            s = MOMENT_SCALE[name]
        km, kv = _jax.random.split(_jax.random.fold_in(key, i + 1))
        out[name] = w
        out["m_" + name] = s * _jax.random.normal(km, w.shape, _jnp.float32)
        out["v_" + name] = (s * s) * _jax.random.uniform(kv, w.shape, _jnp.float32, 0.5, 1.5)
    if N_MICROBATCH > 1:
        for name, axis in PER_EXAMPLE_BATCH_AXIS.items():
            out[name] = _to_microbatches(out[name], axis)
    return {'x': out['x'], 'a_norm_g': out['a_norm_g'], 'a_w_in': out['a_w_in'], 'a_v_norm_g': out['a_v_norm_g'], 'a_w_s': out['a_w_s'], 'a_b_s': out['a_b_s'], 'a_w_out': out['a_w_out'], 'kv_norm_g': out['kv_norm_g'], 'w_kv': out['w_kv'], 'b_norm_g': out['b_norm_g'], 'b_w_q': out['b_w_q'], 'b_rel_bias': out['b_rel_bias'], 'b_w_o': out['b_w_o'], 'f_norm_g': out['f_norm_g'], 'f_w_in': out['f_w_in'], 'f_conv_w': out['f_conv_w'], 'f_conv_b': out['f_conv_b'], 'f_w_down': out['f_w_down'], 'final_norm_g': out['final_norm_g'], 'loss_target': out['loss_target'], 'm_a_norm_g': out['m_a_norm_g'], 'm_a_w_in': out['m_a_w_in'], 'm_a_v_norm_g': out['m_a_v_norm_g'], 'm_a_w_s': out['m_a_w_s'], 'm_a_b_s': out['m_a_b_s'], 'm_a_w_out': out['m_a_w_out'], 'm_kv_norm_g': out['m_kv_norm_g'], 'm_w_kv': out['m_w_kv'], 'm_b_norm_g': out['m_b_norm_g'], 'm_b_w_q': out['m_b_w_q'], 'm_b_rel_bias': out['m_b_rel_bias'], 'm_b_w_o': out['m_b_w_o'], 'm_f_norm_g': out['m_f_norm_g'], 'm_f_w_in': out['m_f_w_in'], 'm_f_conv_w': out['m_f_conv_w'], 'm_f_conv_b': out['m_f_conv_b'], 'm_f_w_down': out['m_f_w_down'], 'm_final_norm_g': out['m_final_norm_g'], 'v_a_norm_g': out['v_a_norm_g'], 'v_a_w_in': out['v_a_w_in'], 'v_a_v_norm_g': out['v_a_v_norm_g'], 'v_a_w_s': out['v_a_w_s'], 'v_a_b_s': out['v_a_b_s'], 'v_a_w_out': out['v_a_w_out'], 'v_kv_norm_g': out['v_kv_norm_g'], 'v_w_kv': out['v_w_kv'], 'v_b_norm_g': out['v_b_norm_g'], 'v_b_w_q': out['v_b_w_q'], 'v_b_rel_bias': out['v_b_rel_bias'], 'v_b_w_o': out['v_b_w_o'], 'v_f_norm_g': out['v_f_norm_g'], 'v_f_w_in': out['v_f_w_in'], 'v_f_conv_w': out['v_f_conv_w'], 'v_f_conv_b': out['v_f_conv_b'], 'v_f_w_down': out['v_f_w_down'], 'v_final_norm_g': out['v_final_norm_g']}


def _loss(weights, diff, rest, loss_target):
    with _jax.named_scope("forward"):
        args = {**rest, TWIN_DIFF_INPUT: diff, **{k: w.astype(_WEIGHT_DTYPES[k]) for k, w in weights.items()}}
        y = _forward(args)
    with _jax.named_scope("loss_head"):
        err = _jnp.square(y.astype(_jnp.float32) - loss_target)
        return 0.5 * _jnp.sum(_jnp.mean(err, axis=-1)) if err.ndim else 0.5 * err


def _adamw(w, g, m, v):
    m = ADAM_B1 * m + (1.0 - ADAM_B1) * g
    v = ADAM_B2 * v + (1.0 - ADAM_B2) * _jnp.square(g)
    m_hat = m / (1.0 - ADAM_B1 ** ADAM_STEP)
    v_hat = v / (1.0 - ADAM_B2 ** ADAM_STEP)
    delta = -ADAM_LR * (m_hat / (_jnp.sqrt(v_hat) + ADAM_EPS) + ADAM_WD * w)
    return delta, m, v


def reference(x, a_norm_g, a_w_in, a_v_norm_g, a_w_s, a_b_s, a_w_out, kv_norm_g, w_kv, b_norm_g, b_w_q, b_rel_bias, b_w_o, f_norm_g, f_w_in, f_conv_w, f_conv_b, f_w_down, final_norm_g, loss_target, m_a_norm_g, m_a_w_in, m_a_v_norm_g, m_a_w_s, m_a_b_s, m_a_w_out, m_kv_norm_g, m_w_kv, m_b_norm_g, m_b_w_q, m_b_rel_bias, m_b_w_o, m_f_norm_g, m_f_w_in, m_f_conv_w, m_f_conv_b, m_f_w_down, m_final_norm_g, v_a_norm_g, v_a_w_in, v_a_v_norm_g, v_a_w_s, v_a_b_s, v_a_w_out, v_kv_norm_g, v_w_kv, v_b_norm_g, v_b_w_q, v_b_rel_bias, v_b_w_o, v_f_norm_g, v_f_w_in, v_f_conv_w, v_f_conv_b, v_f_w_down, v_final_norm_g):
    given = dict(x=x, a_norm_g=a_norm_g, a_w_in=a_w_in, a_v_norm_g=a_v_norm_g, a_w_s=a_w_s, a_b_s=a_b_s, a_w_out=a_w_out, kv_norm_g=kv_norm_g, w_kv=w_kv, b_norm_g=b_norm_g, b_w_q=b_w_q, b_rel_bias=b_rel_bias, b_w_o=b_w_o, f_norm_g=f_norm_g, f_w_in=f_w_in, f_conv_w=f_conv_w, f_conv_b=f_conv_b, f_w_down=f_w_down, final_norm_g=final_norm_g, loss_target=loss_target, m_a_norm_g=m_a_norm_g, m_a_w_in=m_a_w_in, m_a_v_norm_g=m_a_v_norm_g, m_a_w_s=m_a_w_s, m_a_b_s=m_a_b_s, m_a_w_out=m_a_w_out, m_kv_norm_g=m_kv_norm_g, m_w_kv=m_w_kv, m_b_norm_g=m_b_norm_g, m_b_w_q=m_b_w_q, m_b_rel_bias=m_b_rel_bias, m_b_w_o=m_b_w_o, m_f_norm_g=m_f_norm_g, m_f_w_in=m_f_w_in, m_f_conv_w=m_f_conv_w, m_f_conv_b=m_f_conv_b, m_f_w_down=m_f_w_down, m_final_norm_g=m_final_norm_g, v_a_norm_g=v_a_norm_g, v_a_w_in=v_a_w_in, v_a_v_norm_g=v_a_v_norm_g, v_a_w_s=v_a_w_s, v_a_b_s=v_a_b_s, v_a_w_out=v_a_w_out, v_kv_norm_g=v_kv_norm_g, v_w_kv=v_w_kv, v_b_norm_g=v_b_norm_g, v_b_w_q=v_b_w_q, v_b_rel_bias=v_b_rel_bias, v_b_w_o=v_b_w_o, v_f_norm_g=v_f_norm_g, v_f_w_in=v_f_w_in, v_f_conv_w=v_f_conv_w, v_f_conv_b=v_f_conv_b, v_f_w_down=v_f_w_down, v_final_norm_g=v_final_norm_g)
    weights = {n: given[n] for n in TWIN_WEIGHTS}
    shared = {n: given[n] for n in SHARED_INPUTS}
    per_example = {n: given[n] for n in ['x']}
    grad_fn = _jax.value_and_grad(_loss, argnums=(0, 1))

    def one_microbatch(ex, loss_target):
        ex = dict(ex)
        diff = ex.pop(TWIN_DIFF_INPUT)
        return grad_fn(weights, diff, {**shared, **ex}, loss_target)

    if N_MICROBATCH == 1:
        loss, (grad_w, grad_x) = one_microbatch(per_example, given["loss_target"])
    else:
        def body(carry, xs):
            loss_sum, grad_sum = carry
            l_k, (gw_k, gx_k) = one_microbatch(xs[0], xs[1])
            with _jax.named_scope("update"):
                return (loss_sum + l_k, _jax.tree.map(_jnp.add, grad_sum, gw_k)), gx_k

        init = (_jnp.zeros((), _jnp.float32), _jax.tree.map(_jnp.zeros_like, weights))
        (loss, grad_w), grad_x = _jax.lax.scan(body, init, (per_example, given["loss_target"]))
    with _jax.named_scope("update"):
        delta_w, new_m, new_v = {}, {}, {}
        for n in TWIN_WEIGHTS:
            delta_w[n], new_m[n], new_v[n] = _adamw(weights[n], grad_w[n], given["m_" + n], given["v_" + n])
    return (loss, grad_x, *[grad_w[n] for n in TWIN_WEIGHTS], *[delta_w[n] for n in TWIN_WEIGHTS],
            *[new_m[n] for n in TWIN_WEIGHTS], *[new_v[n] for n in TWIN_WEIGHTS])
```

```python
import functools

import jax
import jax.numpy as jnp
from jax import lax
from jax.experimental import pallas as pl
from jax.experimental.pallas import tpu as pltpu

F32 = jnp.float32
BF16 = jnp.bfloat16
MESH = pl.DeviceIdType.MESH

N_DEV = 8
EPS = 1e-6
NEG_INF = -1e30
CHUNK = 64
LEFT_CHUNKS = 8
REL_CLIP = 128
HEAD_DIM = 64
GMLP_BLOCK = 128
Q_TILE = 2 * CHUNK
PAD = LEFT_CHUNKS * CHUNK
WIN = PAD + Q_TILE
SKEW = WIN + Q_TILE
REL_PAD = 384
HEADS_PER_STEP = 4
ADAM_LR, ADAM_B1, ADAM_B2, ADAM_EPS, ADAM_WD, ADAM_STEP = 0.001, 0.9, 0.999, 1e-08, 0.01, 10
VMEM_LIMIT = 56 * 1024 * 1024
TOKEN_TILE = 512


def _params(sem=None):
    return pltpu.CompilerParams(dimension_semantics=sem, vmem_limit_bytes=VMEM_LIMIT)


def _tile(n, pref):
    t = min(n, pref)
    while n % t:
        t //= 2
    return t


def _gelu(x):
    return 0.5 * x * (1.0 + jnp.tanh(0.7978845608028654 * (x + 0.044715 * x * x * x)))


def _gelu_grad(x):
    t = jnp.tanh(0.7978845608028654 * (x + 0.044715 * x * x * x))
    return 0.5 * (1.0 + t) + 0.5 * x * (1.0 - t * t) * 0.7978845608028654 * (1.0 + 3 * 0.044715 * x * x)


def _sigmoid(x):
    return 1.0 / (1.0 + jnp.exp(-x))


def _dot(a, b):
    return jnp.dot(a, b, preferred_element_type=F32)


def _dot_nt(a, b):
    return lax.dot_general(a, b, (((1,), (1,)), ((), ())), preferred_element_type=F32)


def _dot_tn(a, b):
    return lax.dot_general(a, b, (((0,), (0,)), ((), ())), preferred_element_type=F32)


def _split3(x):
    hi = x.astype(BF16)
    r1 = x - hi.astype(F32)
    mid = r1.astype(BF16)
    lo = (r1 - mid.astype(F32)).astype(BF16)
    return hi, mid, lo


def _mesh_pos():
    return lax.axis_index("x"), lax.axis_index("y"), lax.axis_index("c")


def _all_gather(arrs, name):
    n = len(arrs)

    def body(*refs):
        ins, outs = refs[:n], refs[n:2 * n]
        send_sems, recv_sems, local_sems = refs[2 * n:]
        x, y, c = _mesh_pos()
        me, sibling = (x, y, c), (x, y, 1 - c)
        chips = [(1 - x, y), (x, 1 - y), (1 - x, 1 - y)]

        def slot(a, block):
            px, py, pc = block
            return outs[a].at[4 * px + 2 * py + pc]

        def copy(a, k, block, to, src=None):
            dst = slot(a, block)
            return pltpu.make_async_remote_copy(
                src_ref=dst if src is None else src, dst_ref=dst,
                send_sem=send_sems.at[a, k], recv_sem=recv_sems.at[a, k], device_id=to, device_id_type=MESH)

        mine = [pltpu.make_async_copy(ins[a], slot(a, me), local_sems.at[a]) for a in range(n)]
        for cp in mine:
            cp.start()
        first = []
        for a in range(n):
            first.append(copy(a, 0, me, sibling, src=ins[a]))
            first += [copy(a, 1 + j, me, (*chip, c), src=ins[a]) for j, chip in enumerate(chips)]
        for cp in first:
            cp.start()
        passed = []
        for j, chip in enumerate(chips):
            for a in range(n):
                copy(a, 1 + j, (*chip, c), me).wait_recv()
                fwd = copy(a, 4 + j, (*chip, c), sibling)
                fwd.start()
                passed.append(fwd)
        for a in range(n):
            copy(a, 0, sibling, me).wait_recv()
            for j, chip in enumerate(chips):
                copy(a, 4 + j, (*chip, 1 - c), me).wait_recv()
        for cp in first + passed:
            cp.wait_send()
        for cp in mine:
            cp.wait()

    any_spec = pl.BlockSpec(memory_space=pl.ANY)
    return pl.pallas_call(
        body, name=name,
        out_shape=[jax.ShapeDtypeStruct((N_DEV,) + a.shape, a.dtype) for a in arrs],
        in_specs=[any_spec] * n, out_specs=[any_spec] * n,
        scratch_shapes=[pltpu.SemaphoreType.DMA((n, 7)), pltpu.SemaphoreType.DMA((n, 7)), pltpu.SemaphoreType.DMA((n,))],
    )(*arrs)


def _sibling_exchange(arrs, name):
    n = len(arrs)

    def body(*refs):
        ins, outs = refs[:n], refs[n:2 * n]
        send_sems, recv_sems = refs[2 * n:]
        x, y, c = _mesh_pos()
        cps = [pltpu.make_async_remote_copy(
            src_ref=ins[a].at[:, pl.ds(1 - c, 1)], dst_ref=outs[a],
            send_sem=send_sems.at[a], recv_sem=recv_sems.at[a], device_id=(x, y, 1 - c), device_id_type=MESH)
            for a in range(n)]
        for cp in cps:
            cp.start()
        for cp in cps:
            cp.wait()

    any_spec = pl.BlockSpec(memory_space=pl.ANY)
    return pl.pallas_call(
        body, name=name,
        out_shape=[jax.ShapeDtypeStruct((4, 1) + a.shape[2:], a.dtype) for a in arrs],
        in_specs=[any_spec] * n, out_specs=[any_spec] * n,
        scratch_shapes=[pltpu.SemaphoreType.DMA((n,)), pltpu.SemaphoreType.DMA((n,))],
    )(*arrs)


def _chip_exchange(arrs, name):
    n = len(arrs)

    def body(*refs):
        ins, outs = refs[:n], refs[n:2 * n]
        send_sems, recv_sems = refs[2 * n:]
        x, y, c = _mesh_pos()
        cps = []
        for a in range(n):
            for k in (1, 2, 3):
                px = x if k < 2 else 1 - x
                py = y if k == 2 else 1 - y
                cps.append(pltpu.make_async_remote_copy(
                    src_ref=ins[a].at[2 * px + py], dst_ref=outs[a].at[k - 1],
                    send_sem=send_sems.at[a, k - 1], recv_sem=recv_sems.at[a, k - 1],
                    device_id=(px, py, c), device_id_type=MESH))
        for cp in cps:
            cp.start()
        for cp in cps:
            cp.wait()

    any_spec = pl.BlockSpec(memory_space=pl.ANY)
    return pl.pallas_call(
        body, name=name,
        out_shape=[jax.ShapeDtypeStruct((3,) + a.shape[1:], a.dtype) for a in arrs],
        in_specs=[any_spec] * n, out_specs=[any_spec] * n,
        scratch_shapes=[pltpu.SemaphoreType.DMA((n, 3)), pltpu.SemaphoreType.DMA((n, 3))],
    )(*arrs)


def _sibling_sum(part, landed, pos, name):
    _, _, rows, cols = part.shape
    tr = _tile(rows, 256)

    def body(pos_ref, p_ref, l_ref, own_ref, all_ref):
        s = p_ref[0, 0] + l_ref[0, 0]
        all_ref[0] = s.astype(BF16)

        @pl.when(pl.program_id(1) == pos_ref[1])
        def _():
            own_ref[...] = s

    return pl.pallas_call(
        body, name=name,
        grid_spec=pltpu.PrefetchScalarGridSpec(
            num_scalar_prefetch=1, grid=(rows // tr, 4),
            in_specs=[pl.BlockSpec((1, 1, tr, cols), lambda i, k, pos: (k, pos[0], i, 0)),
                      pl.BlockSpec((1, 1, tr, cols), lambda i, k, pos: (k, 0, i, 0))],
            out_specs=[pl.BlockSpec((tr, cols), lambda i, k, pos: (i, 0)),
                       pl.BlockSpec((1, tr, cols), lambda i, k, pos: (k, i, 0))]),
        out_shape=[jax.ShapeDtypeStruct((rows, cols), F32), jax.ShapeDtypeStruct((4, rows, cols), BF16)],
        compiler_params=_params(("arbitrary", "arbitrary")),
    )(pos, part, landed)


def _rms(x, g):
    r = lax.rsqrt(jnp.mean(x * x, axis=-1, keepdims=True) + EPS)
    return x * r, r


def _norm_matmul(h, g, w, *, flat, nbk, name):
    T, D = h.shape
    nb, _, bn = w.shape
    tm = _tile(T, TOKEN_TILE)

    def body(h_ref, g_ref, w_ref, o_ref, n_ref):
        @pl.when(pl.program_id(1) == 0)
        def _():
            xh, _ = _rms(h_ref[...], None)
            n_ref[...] = (xh * g_ref[...]).astype(BF16)

        n = n_ref[...]
        for k in range(nbk):
            r = _dot(n, w_ref[k]).astype(BF16)
            if flat:
                o_ref[:, k * bn:(k + 1) * bn] = r
            else:
                o_ref[k] = r

    if flat:
        out_shape = jax.ShapeDtypeStruct((T, nb * bn), BF16)
        out_spec = pl.BlockSpec((tm, nbk * bn), lambda i, j: (i, j))
    else:
        out_shape = jax.ShapeDtypeStruct((nb, T, bn), BF16)
        out_spec = pl.BlockSpec((nbk, tm, bn), lambda i, j: (j, i, 0))
    return pl.pallas_call(
        body, name=name, grid=(T // tm, nb // nbk),
        in_specs=[pl.BlockSpec((tm, D), lambda i, j: (i, 0)),
                  pl.BlockSpec((1, D), lambda i, j: (0, 0)),
                  pl.BlockSpec((nbk, D, bn), lambda i, j: (j, 0, 0))],
        out_specs=out_spec, out_shape=out_shape,
        scratch_shapes=[pltpu.VMEM((tm, D), BF16)],
        compiler_params=_params(("arbitrary", "arbitrary")),
    )(h, g, w)


def _matmul_nt(dy, w, *, flat, nbk, name, norm=None, out_dtype=BF16):
    nb, R, bn = w.shape
    T = dy.shape[0] if flat else dy.shape[1]
    tm = _tile(T, TOKEN_TILE)
    nj = nb // nbk

    def body(*refs):
        if norm is None:
            dy_ref, w_ref, o_ref, acc_ref = refs
        else:
            dy_ref, w_ref, h_ref, g_ref, dres_ref, o_ref, dg_ref, acc_ref = refs
        i, j = pl.program_id(0), pl.program_id(1)

        @pl.when(j == 0)
        def _():
            acc_ref[...] = jnp.zeros_like(acc_ref)

        acc = acc_ref[...]
        for k in range(nbk):
            d = dy_ref[:, k * bn:(k + 1) * bn] if flat else dy_ref[k]
            acc = acc + _dot_nt(d.astype(BF16), w_ref[k])
        acc_ref[...] = acc

        @pl.when(j == nj - 1)
        def _():
            if norm is None:
                o_ref[...] = acc.astype(out_dtype)
            else:
                xh, r = _rms(h_ref[...], None)

                @pl.when(i == 0)
                def _():
                    dg_ref[...] = jnp.zeros_like(dg_ref)

                dg_ref[0:1, :] += jnp.sum(acc * xh, axis=0, keepdims=True)
                dn = acc * g_ref[...]
                o_ref[...] = dres_ref[...] + r * (dn - xh * jnp.mean(dn * xh, axis=-1, keepdims=True))

    if flat:
        dy_spec = pl.BlockSpec((tm, nbk * bn), lambda i, j: (i, j))
    else:
        dy_spec = pl.BlockSpec((nbk, tm, bn), lambda i, j: (j, i, 0))
    w_spec = pl.BlockSpec((nbk, R, bn), lambda i, j: (j, 0, 0))
    row_spec = pl.BlockSpec((tm, R), lambda i, j: (i, 0))
    if norm is None:
        in_specs, args = [dy_spec, w_spec], (dy, w)
        out_specs = row_spec
        out_shape = jax.ShapeDtypeStruct((T, R), out_dtype)
    else:
        in_specs = [dy_spec, w_spec, row_spec, pl.BlockSpec((1, R), lambda i, j: (0, 0)), row_spec]
        args = (dy, w) + tuple(norm)
        out_specs = [row_spec, pl.BlockSpec((8, R), lambda i, j: (0, 0))]
        out_shape = [jax.ShapeDtypeStruct((T, R), F32), jax.ShapeDtypeStruct((8, R), F32)]
    return pl.pallas_call(
        body, name=name, grid=(T // tm, nj), in_specs=in_specs, out_specs=out_specs, out_shape=out_shape,
        scratch_shapes=[pltpu.VMEM((tm, R), F32)],
        compiler_params=_params(("arbitrary", "arbitrary")),
    )(*args)


def _wgrad_cols(h, g, dy, *, flat, nb, nbk, name):
    T, D = h.shape
    bn = dy.shape[1] // nb if flat else dy.shape[2]
    tt = _tile(T, TOKEN_TILE)
    nt = T // tt

    def body(h_ref, g_ref, dy_ref, o_ref, acc_ref):
        t = pl.program_id(1)

        @pl.when(t == 0)
        def _():
            acc_ref[...] = jnp.zeros_like(acc_ref)

        xh, _ = _rms(h_ref[...], None)
        n = (xh * g_ref[...]).astype(BF16)
        for k in range(nbk):
            d = dy_ref[:, k * bn:(k + 1) * bn] if flat else dy_ref[k]
            acc_ref[k] += _dot_tn(n, d)

        @pl.when(t == nt - 1)
        def _():
            o_ref[...] = acc_ref[...]

    if flat:
        dy_spec = pl.BlockSpec((tt, nbk * bn), lambda j, t: (t, j))
    else:
        dy_spec = pl.BlockSpec((nbk, tt, bn), lambda j, t: (j, t, 0))
    return pl.pallas_call(
        body, name=name, grid=(nb // nbk, nt),
        in_specs=[pl.BlockSpec((tt, D), lambda j, t: (t, 0)), pl.BlockSpec((1, D), lambda j, t: (0, 0)), dy_spec],
        out_specs=pl.BlockSpec((nbk, D, bn), lambda j, t: (j, 0, 0)),
        out_shape=jax.ShapeDtypeStruct((nb, D, bn), F32),
        scratch_shapes=[pltpu.VMEM((nbk, D, bn), F32)],
        compiler_params=_params(("arbitrary", "arbitrary")),
    )(h, g, dy)


def _wgrad_rows(xa, dh, *, flat, tk, name):
    T, D = dh.shape
    nk = xa.shape[1] // tk if flat else xa.shape[0]
    tt = _tile(T, TOKEN_TILE)
    nt = T // tt

    def body(x_ref, dh_ref, o_ref, acc_ref):
        t = pl.program_id(1)

        @pl.when(t == 0)
        def _():
            acc_ref[...] = jnp.zeros_like(acc_ref)

        xv = x_ref[...] if flat else x_ref[0]
        acc_ref[...] += _dot_tn(xv, dh_ref[...].astype(BF16))

        @pl.when(t == nt - 1)
        def _():
            o_ref[...] = acc_ref[...]

    x_spec = pl.BlockSpec((tt, tk), lambda j, t: (t, j)) if flat else pl.BlockSpec((1, tt, tk), lambda j, t: (j, t, 0))
    return pl.pallas_call(
        body, name=name, grid=(nk, nt),
        in_specs=[x_spec, pl.BlockSpec((tt, D), lambda j, t: (t, 0))],
        out_specs=pl.BlockSpec((tk, D), lambda j, t: (j, 0)),
        out_shape=jax.ShapeDtypeStruct((nk * tk, D), F32),
        scratch_shapes=[pltpu.VMEM((tk, D), F32)],
        compiler_params=_params(("arbitrary", "arbitrary")),
    )(xa, dh)


def _matmul_residual(xa, w, res, name):
    T, K = xa.shape
    D = w.shape[1]
    tm = _tile(T, TOKEN_TILE)

    def body(x_ref, w_ref, r_ref, o_ref):
        o_ref[...] = r_ref[...] + _dot(x_ref[...], w_ref[...])

    return pl.pallas_call(
        body, name=name, grid=(T // tm,),
        in_specs=[pl.BlockSpec((tm, K), lambda i: (i, 0)), pl.BlockSpec((K, D), lambda i: (0, 0)),
                  pl.BlockSpec((tm, D), lambda i: (i, 0))],
        out_specs=pl.BlockSpec((tm, D), lambda i: (i, 0)),
        out_shape=jax.ShapeDtypeStruct((T, D), F32),
        compiler_params=_params(("arbitrary",)),
    )(xa, w, res)


def _gmlp_gate(z, ws, bst, gv, G, gd):
    D = G * gd
    u = _gelu(z[:, :D].astype(F32))
    v = _gelu(z[:, D:].astype(F32))
    vh, r = _rms(v, None)
    vn = (vh * gv).astype(BF16)
    return u, v, vh, r, vn


def _gmlp_forward(z, ws, bst, gv, w_out, x, *, name):
    T, D2 = z.shape
    D = D2 // 2
    G = ws.shape[0]
    gd = D // G
    tb = _tile(T, 256)
    nblk = tb // GMLP_BLOCK

    def body(z_ref, ws_ref, b_ref, gv_ref, wo_ref, x_ref, gated_ref, h_ref):
        u, _, _, _, vn = _gmlp_gate(z_ref[...], None, None, gv_ref[...], G, gd)
        for n in range(nblk):
            rows = slice(n * GMLP_BLOCK, (n + 1) * GMLP_BLOCK)
            for gi in range(G):
                cols = slice(gi * gd, (gi + 1) * gd)
                s = _dot(ws_ref[gi], vn[rows, cols]) + b_ref[:, gi:gi + 1]
                gated_ref[rows, cols] = (u[rows, cols] * s).astype(BF16)
        h_ref[...] = x_ref[...] + _dot(gated_ref[...], wo_ref[...])

    return pl.pallas_call(
        body, name=name, grid=(T // tb,),
        in_specs=[pl.BlockSpec((tb, D2), lambda i: (i, 0)), pl.BlockSpec(ws.shape, lambda i: (0, 0, 0)),
                  pl.BlockSpec(bst.shape, lambda i: (0, 0)), pl.BlockSpec((1, D), lambda i: (0, 0)),
                  pl.BlockSpec((D, D), lambda i: (0, 0)), pl.BlockSpec((tb, D), lambda i: (i, 0))],
        out_specs=[pl.BlockSpec((tb, D), lambda i: (i, 0)), pl.BlockSpec((tb, D), lambda i: (i, 0))],
        out_shape=[jax.ShapeDtypeStruct((T, D), BF16), jax.ShapeDtypeStruct((T, D), F32)],
        compiler_params=_params(("arbitrary",)),
    )(z, ws, bst, gv, w_out, x)


def _gmlp_backward(z, dgated, ws, bst, gv, mask, *, name):
    T, D2 = z.shape
    D = D2 // 2
    G = ws.shape[0]
    gd = D // G
    tb = _tile(T, 256)
    nblk = tb // GMLP_BLOCK

    def body(z_ref, dg_ref, ws_ref, b_ref, gv_ref, mask_ref, dz_ref, dws_ref, db_ref, dgv_ref, dvn_ref):
        @pl.when(pl.program_id(0) == 0)
        def _():
            dws_ref[...] = jnp.zeros_like(dws_ref)
            db_ref[...] = jnp.zeros_like(db_ref)
            dgv_ref[...] = jnp.zeros_like(dgv_ref)

        zf = z_ref[...]
        u, v, vh, r, vn = _gmlp_gate(zf, None, None, gv_ref[...], G, gd)
        dg = dg_ref[...].astype(F32)
        for n in range(nblk):
            rows = slice(n * GMLP_BLOCK, (n + 1) * GMLP_BLOCK)
            for gi in range(G):
                cols = slice(gi * gd, (gi + 1) * gd)
                vblk = vn[rows, cols]
                s = _dot(ws_ref[gi], vblk) + b_ref[:, gi:gi + 1]
                dgb = dg[rows, cols]
                ds = dgb * u[rows, cols]
                dsb = ds.astype(BF16)
                dz_ref[rows, cols] = (dgb * s * _gelu_grad(zf[rows, cols].astype(F32))).astype(BF16)
                dvn_ref[rows, cols] = _dot_tn(ws_ref[gi], dsb)
                dws_ref[gi] += _dot_nt(dsb, vblk) * mask_ref[...]
                db_ref[:, gi:gi + 1] += jnp.sum(ds, axis=1, keepdims=True)
        dvn = dvn_ref[...]
        dgv_ref[0:1, :] += jnp.sum(dvn * vh, axis=0, keepdims=True)
        dn = dvn * gv_ref[...]
        dv = r * (dn - vh * jnp.mean(dn * vh, axis=-1, keepdims=True))
        dz_ref[:, D:] = (dv * _gelu_grad(zf[:, D:].astype(F32))).astype(BF16)

    return pl.pallas_call(
        body, name=name, grid=(T // tb,),
        in_specs=[pl.BlockSpec((tb, D2), lambda i: (i, 0)), pl.BlockSpec((tb, D), lambda i: (i, 0)),
                  pl.BlockSpec(ws.shape, lambda i: (0, 0, 0)), pl.BlockSpec(bst.shape, lambda i: (0, 0)),
                  pl.BlockSpec((1, D), lambda i: (0, 0)), pl.BlockSpec(mask.shape, lambda i: (0, 0))],
        out_specs=[pl.BlockSpec((tb, D2), lambda i: (i, 0)), pl.BlockSpec(ws.shape, lambda i: (0, 0, 0)),
                   pl.BlockSpec(bst.shape, lambda i: (0, 0)), pl.BlockSpec((8, D), lambda i: (0, 0))],
        out_shape=[jax.ShapeDtypeStruct((T, D2), BF16), jax.ShapeDtypeStruct(ws.shape, F32),
                   jax.ShapeDtypeStruct(bst.shape, F32), jax.ShapeDtypeStruct((8, D), F32)],
        scratch_shapes=[pltpu.VMEM((tb, D), F32)],
        compiler_params=_params(("arbitrary",)),
    )(z, dgated, ws, bst, gv, mask)


def _shift_rows(x, k):
    return pltpu.roll(x, k % x.shape[0], axis=0)


def _conv3(ext, cw):
    return (cw[0:1] * _shift_rows(ext, 2)[8:] + cw[1:2] * _shift_rows(ext, 1)[8:] + cw[2:3] * ext[8:])


def _ffn_forward(a, cw, cb, wd, h, seq, *, name):
    _, T, F = a.shape
    D = h.shape[1]
    tm = _tile(seq, TOKEN_TILE)
    hb = tm // 16

    def body(a_ref, ap_ref, cw_ref, cb_ref, wd_ref, h_ref, act_ref, o_ref, acc_ref):
        i, j = pl.program_id(0), pl.program_id(1)
        keep = ((i * tm) % seq != 0).astype(F32)

        def conv(b):
            ext = jnp.concatenate([ap_ref[b, 8:16].astype(F32) * keep, a_ref[b].astype(F32)], axis=0)
            return _conv3(ext, cw_ref[b]) + cb_ref[b]

        up, gate = conv(j), conv(j + 4)
        act = (gate * _sigmoid(gate) * up).astype(BF16)
        act_ref[0] = act

        @pl.when(j == 0)
        def _():
            acc_ref[...] = h_ref[...]

        acc_ref[...] += _dot(act, wd_ref[0])

        @pl.when(j == 3)
        def _():
            o_ref[...] = acc_ref[...]

    return pl.pallas_call(
        body, name=name, grid=(T // tm, 4),
        in_specs=[pl.BlockSpec((8, tm, F), lambda i, j: (0, i, 0)),
                  pl.BlockSpec((8, 16, F), lambda i, j: (0, jnp.maximum(i * hb - 1, 0), 0)),
                  pl.BlockSpec((8, 3, F), lambda i, j: (0, 0, 0)), pl.BlockSpec((8, 1, F), lambda i, j: (0, 0, 0)),
                  pl.BlockSpec((1, F, D), lambda i, j: (j, 0, 0)), pl.BlockSpec((tm, D), lambda i, j: (i, 0))],
        out_specs=[pl.BlockSpec((1, tm, F), lambda i, j: (j, i, 0)), pl.BlockSpec((tm, D), lambda i, j: (i, 0))],
        out_shape=[jax.ShapeDtypeStruct((4, T, F), BF16), jax.ShapeDtypeStruct((T, D), F32)],
        scratch_shapes=[pltpu.VMEM((tm, D), F32)],
        compiler_params=_params(("arbitrary", "arbitrary")),
    )(a, a, cw, cb, wd, h)


def _ffn_backward(dh, a, cw, cb, wd, seq, *, name):
    _, T, F = a.shape
    D = dh.shape[1]
    tm = _tile(seq, TOKEN_TILE)
    hb = tm // 16
    nt = T // tm

    def body(dh_ref, dhn_ref, a_ref, ap_ref, an_ref, cw_ref, cb_ref, wd_ref, da_ref, st_ref):
        i, j = pl.program_id(0), pl.program_id(1)
        keep_prev = ((i * tm) % seq != 0).astype(F32)
        keep_next = (((i + 1) * tm) % seq != 0).astype(F32)

        @pl.when((i == 0) & (j == 0))
        def _():
            st_ref[...] = jnp.zeros_like(st_ref)

        dhe = jnp.concatenate([dh_ref[...], dhn_ref[...] * keep_next], axis=0).astype(BF16)
        dact = _dot_nt(dhe, wd_ref[0])

        def conv(b):
            ext = jnp.concatenate([ap_ref[b, 8:16].astype(F32) * keep_prev, a_ref[b].astype(F32),
                                   an_ref[b, 0:8].astype(F32)], axis=0)
            return ext, _conv3(ext, cw_ref[b]) + cb_ref[b]

        ext_u, up = conv(j)
        ext_g, gate = conv(j + 4)
        sg = _sigmoid(gate)
        d_up = dact * (gate * sg)
        d_gate = dact * up * (sg * (1.0 + gate * (1.0 - sg)))

        def finish(b, ext, dc):
            w = cw_ref[b]
            da = (w[2:3] * dc + w[1:2] * _shift_rows(dc, -1) + w[0:1] * _shift_rows(dc, -2))[:tm]
            da_ref[b] = da.astype(BF16)
            dm = dc[:tm]
            st_ref[b, 0:1, :] += jnp.sum(dm * _shift_rows(ext, 2)[8:8 + tm], axis=0, keepdims=True)
            st_ref[b, 1:2, :] += jnp.sum(dm * _shift_rows(ext, 1)[8:8 + tm], axis=0, keepdims=True)
            st_ref[b, 2:3, :] += jnp.sum(dm * ext[8:8 + tm], axis=0, keepdims=True)
            st_ref[b, 3:4, :] += jnp.sum(dm, axis=0, keepdims=True)

        finish(j, ext_u, d_up)
        finish(j + 4, ext_g, d_gate)

    return pl.pallas_call(
        body, name=name, grid=(nt, 4),
        in_specs=[pl.BlockSpec((tm, D), lambda i, j: (i, 0)),
                  pl.BlockSpec((8, D), lambda i, j: (jnp.minimum((i + 1) * (tm // 8), T // 8 - 1), 0)),
                  pl.BlockSpec((8, tm, F), lambda i, j: (0, i, 0)),
                  pl.BlockSpec((8, 16, F), lambda i, j: (0, jnp.maximum(i * hb - 1, 0), 0)),
                  pl.BlockSpec((8, 16, F), lambda i, j: (0, jnp.minimum((i + 1) * hb, T // 16 - 1), 0)),
                  pl.BlockSpec((8, 3, F), lambda i, j: (0, 0, 0)), pl.BlockSpec((8, 1, F), lambda i, j: (0, 0, 0)),
                  pl.BlockSpec((1, F, D), lambda i, j: (j, 0, 0))],
        out_specs=[pl.BlockSpec((8, tm, F), lambda i, j: (0, i, 0)), pl.BlockSpec((8, 8, F), lambda i, j: (0, 0, 0))],
        out_shape=[jax.ShapeDtypeStruct((8, T, F), BF16), jax.ShapeDtypeStruct((8, 8, F), F32)],
        compiler_params=_params(("arbitrary", "arbitrary")),
    )(dh, dh, a, a, a, cw, cb, wd)


def _rel_onehot():
    r = lax.broadcasted_iota(jnp.int32, (REL_PAD, SKEW), 0)
    n = lax.broadcasted_iota(jnp.int32, (REL_PAD, SKEW), 1)
    off = jnp.where(n >= WIN, n - SKEW, n)
    idx = jnp.minimum(PAD - off, REL_CLIP) + REL_CLIP
    return (r == idx).astype(BF16)


def _skew(x, sign):
    row = lax.broadcasted_iota(jnp.int32, x.shape, 0)
    for b in range(7):
        x = jnp.where((row >> b) & 1 == 1, pltpu.roll(x, (sign * (1 << b)) % SKEW, axis=1), x)
    return x


def _bias_build(rel, name):
    H = rel.shape[0]

    def body(rel_ref, o_ref):
        oh = _rel_onehot()
        hi, mid, lo = _split3(rel_ref[...])
        base = _dot(hi, oh) + _dot(mid, oh) + _dot(lo, oh)
        q = lax.broadcasted_iota(jnp.int32, (Q_TILE, WIN), 0)
        k = lax.broadcasted_iota(jnp.int32, (Q_TILE, WIN), 1)
        ok = ((q < CHUNK) & (k < WIN - CHUNK)) | ((q >= CHUNK) & (k >= CHUNK))
        for hd in range(H):
            t = _skew(jnp.broadcast_to(base[hd:hd + 1, :], (Q_TILE, SKEW)), 1)
            o_ref[hd] = jnp.where(ok, t[:, :WIN], NEG_INF)

    return pl.pallas_call(
        body, name=name, out_shape=jax.ShapeDtypeStruct((H, Q_TILE, WIN), F32),
        in_specs=[pl.BlockSpec(memory_space=pltpu.VMEM)], out_specs=pl.BlockSpec(memory_space=pltpu.VMEM),
        compiler_params=_params(),
    )(rel)


def _bias_reduce(dbias, name):
    H = dbias.shape[0]

    def body(d_ref, o_ref, e_ref):
        oh = _rel_onehot()
        for hd in range(H):
            x = jnp.concatenate([d_ref[hd], jnp.zeros((Q_TILE, SKEW - WIN), F32)], axis=1)
            e_ref[hd:hd + 1, :] = jnp.sum(_skew(x, -1), axis=0, keepdims=True)
        hi, mid, lo = _split3(e_ref[...])
        o_ref[...] = _dot_nt(hi, oh) + _dot_nt(mid, oh) + _dot_nt(lo, oh)

    return pl.pallas_call(
        body, name=name, out_shape=jax.ShapeDtypeStruct((H, REL_PAD), F32),
        in_specs=[pl.BlockSpec(memory_space=pltpu.VMEM)], out_specs=pl.BlockSpec(memory_space=pltpu.VMEM),
        scratch_shapes=[pltpu.VMEM((H, SKEW), F32)],
        compiler_params=_params(),
    )(dbias)


def _attn_probs(qh, kwin, bias, first_valid):
    s = _dot_nt(qh, kwin) * (HEAD_DIM ** -0.5) + bias
    col = lax.broadcasted_iota(jnp.int32, s.shape, 1)
    s = jnp.where(col >= first_valid, s, NEG_INF)
    p = jnp.exp(s - jnp.max(s, axis=-1, keepdims=True))
    return p / jnp.sum(p, axis=-1, keepdims=True)


def _attn_specs(B, S, D, lanes):
    nt = S // Q_TILE
    q_spec = pl.BlockSpec((Q_TILE, lanes), lambda g, b, i: (b * nt + i, g))
    k_spec = pl.BlockSpec((1, S + PAD, lanes), lambda g, b, i: (b, 0, g))
    v_spec = pl.BlockSpec((1, S + PAD, lanes), lambda g, b, i: (b, 0, D // lanes + g))
    bias_spec = pl.BlockSpec((HEADS_PER_STEP, Q_TILE, WIN), lambda g, b, i: (g, 0, 0))
    return nt, q_spec, k_spec, v_spec, bias_spec


def _attn_forward(q, kvp, bias, *, name):
    T, D = q.shape
    B, SP, _ = kvp.shape
    S = SP - PAD
    lanes = HEADS_PER_STEP * HEAD_DIM
    nt, q_spec, k_spec, v_spec, bias_spec = _attn_specs(B, S, D, lanes)

    def body(q_ref, k_ref, v_ref, b_ref, o_ref):
        i = pl.program_id(2)
        start = pl.multiple_of(i * Q_TILE, Q_TILE)
        first_valid = PAD - i * Q_TILE
        lane = lax.broadcasted_iota(jnp.int32, (1, 2 * HEAD_DIM), 1)
        for pp in range(HEADS_PER_STEP // 2):
            cols = slice(pp * 2 * HEAD_DIM, (pp + 1) * 2 * HEAD_DIM)
            qp = q_ref[:, cols]
            kwin = k_ref[0, pl.ds(start, WIN), cols]
            vwin = v_ref[0, pl.ds(start, WIN), cols]
            outs = []
            for e in range(2):
                sel = (lane < HEAD_DIM) == (e == 0)
                p = _attn_probs(jnp.where(sel, qp, jnp.zeros_like(qp)), kwin, b_ref[2 * pp + e], first_valid)
                outs.append(_dot(p.astype(BF16), vwin))
            o_ref[:, cols] = jnp.where(lane < HEAD_DIM, outs[0], outs[1]).astype(BF16)

    return pl.pallas_call(
        body, name=name, grid=(D // lanes, B, nt),
        in_specs=[q_spec, k_spec, v_spec, bias_spec], out_specs=q_spec,
        out_shape=jax.ShapeDtypeStruct((T, D), BF16),
        compiler_params=_params(("arbitrary", "arbitrary", "arbitrary")),
    )(q, kvp, kvp, bias)


def _attn_backward(q, kvp, bias, do, *, name):
    T, D = q.shape
    B, SP, _ = kvp.shape
    S = SP - PAD
    H = D // HEAD_DIM
    lanes = HEADS_PER_STEP * HEAD_DIM
    nt, q_spec, k_spec, v_spec, bias_spec = _attn_specs(B, S, D, lanes)
    scale = HEAD_DIM ** -0.5

    def body(q_ref, k_ref, v_ref, b_ref, do_ref, dq_ref, dk_ref, dv_ref, db_ref, dka_ref, dva_ref):
        b, i = pl.program_id(1), pl.program_id(2)
        start = pl.multiple_of(i * Q_TILE, Q_TILE)
        first_valid = PAD - i * Q_TILE
        lane = lax.broadcasted_iota(jnp.int32, (1, 2 * HEAD_DIM), 1)

        @pl.when((b == 0) & (i == 0))
        def _():
            db_ref[...] = jnp.zeros_like(db_ref)

        @pl.when(i == 0)
        def _():
            dka_ref[...] = jnp.zeros_like(dka_ref)
            dva_ref[...] = jnp.zeros_like(dva_ref)

        for pp in range(HEADS_PER_STEP // 2):
            cols = slice(pp * 2 * HEAD_DIM, (pp + 1) * 2 * HEAD_DIM)
            qp = q_ref[:, cols]
            dop = do_ref[:, cols]
            kwin = k_ref[0, pl.ds(start, WIN), cols]
            vwin = v_ref[0, pl.ds(start, WIN), cols]
            dqs = []
            dk = jnp.zeros((WIN, 2 * HEAD_DIM), F32)
            dv = jnp.zeros((WIN, 2 * HEAD_DIM), F32)
            for e in range(2):
                sel = (lane < HEAD_DIM) == (e == 0)
                qh = jnp.where(sel, qp, jnp.zeros_like(qp))
                doh = jnp.where(sel, dop, jnp.zeros_like(dop))
                p = _attn_probs(qh, kwin, b_ref[2 * pp + e], first_valid)
                dp = _dot_nt(doh, vwin)
                ds = p * (dp - jnp.sum(p * dp, axis=-1, keepdims=True))
                db_ref[2 * pp + e] += ds
                dsb = ds.astype(BF16)
                dqs.append(_dot(dsb, kwin) * scale)
                dk = dk + _dot_tn(dsb, qh) * scale
                dv = dv + _dot_tn(p.astype(BF16), doh)
            dq_ref[:, cols] = jnp.where(lane < HEAD_DIM, dqs[0], dqs[1]).astype(BF16)
            dka_ref[pl.ds(start, WIN), cols] += dk
            dva_ref[pl.ds(start, WIN), cols] += dv

        @pl.when(i == nt - 1)
        def _():
            dk_ref[0] = dka_ref[...].astype(BF16)
            dv_ref[0] = dva_ref[...].astype(BF16)

    dkv_shape = jax.ShapeDtypeStruct((B, SP, D), BF16)
    half_spec = pl.BlockSpec((1, SP, lanes), lambda g, b, i: (b, 0, g))
    return pl.pallas_call(
        body, name=name, grid=(D // lanes, B, nt),
        in_specs=[q_spec, k_spec, v_spec, bias_spec, q_spec],
        out_specs=[q_spec, half_spec, half_spec, bias_spec],
        out_shape=[jax.ShapeDtypeStruct((T, D), BF16), dkv_shape, dkv_shape,
                   jax.ShapeDtypeStruct((H, Q_TILE, WIN), F32)],
        scratch_shapes=[pltpu.VMEM((SP, lanes), F32), pltpu.VMEM((SP, lanes), F32)],
        compiler_params=_params(("arbitrary", "arbitrary", "arbitrary")),
    )(q, kvp, kvp, bias, do)


def _loss_head(h, g, target, name):
    T, D = h.shape
    tm = _tile(T, TOKEN_TILE)

    def body(h_ref, g_ref, t_ref, dh_ref, st_ref):
        @pl.when(pl.program_id(0) == 0)
        def _():
            st_ref[...] = jnp.zeros_like(st_ref)

        xh, r = _rms(h_ref[...], None)
        err = xh * g_ref[...] - t_ref[...]
        st_ref[1:2, :] += 0.5 * jnp.sum(jnp.mean(err * err, axis=-1, keepdims=True), axis=0, keepdims=True)
        dy = err * (1.0 / D)
        st_ref[0:1, :] += jnp.sum(dy * xh, axis=0, keepdims=True)
        dn = dy * g_ref[...]
        dh_ref[...] = r * (dn - xh * jnp.mean(dn * xh, axis=-1, keepdims=True))

    row = pl.BlockSpec((tm, D), lambda i: (i, 0))
    return pl.pallas_call(
        body, name=name, grid=(T // tm,),
        in_specs=[row, pl.BlockSpec((1, D), lambda i: (0, 0)), row],
        out_specs=[row, pl.BlockSpec((8, D), lambda i: (0, 0))],
        out_shape=[jax.ShapeDtypeStruct((T, D), F32), jax.ShapeDtypeStruct((8, D), F32)],
        compiler_params=_params(("arbitrary",)),
    )(h, g, target)


def _sum_devices(arrs, name):
    n = len(arrs)

    def body(*refs):
        for a in range(n):
            s = refs[a][0]
            for k in range(1, N_DEV):
                s = s + refs[a][k]
            refs[n + a][...] = s

    vm = pl.BlockSpec(memory_space=pltpu.VMEM)
    return pl.pallas_call(
        body, name=name, out_shape=[jax.ShapeDtypeStruct(a.shape[1:], F32) for a in arrs],
        in_specs=[vm] * n, out_specs=[vm] * n, compiler_params=_params(),
    )(*arrs)


def _adamw_math(w, g, m, v):
    m = ADAM_B1 * m + (1.0 - ADAM_B1) * g
    v = ADAM_B2 * v + (1.0 - ADAM_B2) * (g * g)
    m_hat = m / (1.0 - ADAM_B1 ** ADAM_STEP)
    v_hat = v / (1.0 - ADAM_B2 ** ADAM_STEP)
    delta = -ADAM_LR * (m_hat / (jnp.sqrt(v_hat) + ADAM_EPS) + ADAM_WD * w)
    return delta, m, v


def _adamw_small(items, name):
    n = len(items)

    def body(*refs):
        for a in range(n):
            w, m, v, g = (refs[4 * a + k][...] for k in range(4))
            d, m, v = _adamw_math(w, g, m, v)
            refs[4 * n + 3 * a][...] = d
            refs[4 * n + 3 * a + 1][...] = m
            refs[4 * n + 3 * a + 2][...] = v

    vm = pl.BlockSpec(memory_space=pltpu.VMEM)
    flat = [t for it in items for t in it]
    outs = pl.pallas_call(
        body, name=name,
        out_shape=[jax.ShapeDtypeStruct(it[0].shape, F32) for it in items for _ in range(3)],
        in_specs=[vm] * (4 * n), out_specs=[vm] * (3 * n), compiler_params=_params(),
    )(*flat)
    return [tuple(outs[3 * a:3 * a + 3]) for a in range(n)]


def _adamw_big(w, m, v, own, landed, name):
    R, C = w.shape
    tr = _tile(R, 256)

    def body(w_ref, m_ref, v_ref, own_ref, l_ref, g_ref, d_ref, mo_ref, vo_ref):
        g = own_ref[...]
        for k in range(3):
            g = g + l_ref[k].astype(F32)
        d, mn, vn = _adamw_math(w_ref[...], g, m_ref[...], v_ref[...])
        g_ref[...] = g
        d_ref[...] = d
        mo_ref[...] = mn
        vo_ref[...] = vn

    row = pl.BlockSpec((tr, C), lambda i: (i, 0))
    return pl.pallas_call(
        body, name=name, grid=(R // tr,),
        in_specs=[row, row, row, row, pl.BlockSpec((3, tr, C), lambda i: (0, i, 0))],
        out_specs=[row] * 4, out_shape=[jax.ShapeDtypeStruct((R, C), F32)] * 4,
        compiler_params=_params(("arbitrary",)),
    )(w, m, v, own, landed)


def kernel(x, a_norm_g, a_w_in, a_v_norm_g, a_w_s, a_b_s, a_w_out, kv_norm_g, w_kv, b_norm_g, b_w_q, b_rel_bias, b_w_o, f_norm_g, f_w_in, f_conv_w, f_conv_b, f_w_down, final_norm_g, loss_target, m_a_norm_g, m_a_w_in, m_a_v_norm_g, m_a_w_s, m_a_b_s, m_a_w_out, m_kv_norm_g, m_w_kv, m_b_norm_g, m_b_w_q, m_b_rel_bias, m_b_w_o, m_f_norm_g, m_f_w_in, m_f_conv_w, m_f_conv_b, m_f_w_down, m_final_norm_g, v_a_norm_g, v_a_w_in, v_a_v_norm_g, v_a_w_s, v_a_b_s, v_a_w_out, v_kv_norm_g, v_w_kv, v_b_norm_g, v_b_w_q, v_b_rel_bias, v_b_w_o, v_f_norm_g, v_f_w_in, v_f_conv_w, v_f_conv_b, v_f_w_down, v_final_norm_g):
    B, S, D = x.shape
    T = B * S
    G = a_w_s.shape[1]
    H = D // HEAD_DIM
    F = f_w_in.shape[2]
    L = f_w_in.shape[0]
    dn = D // N_DEV
    xi, yi, ci = lax.axis_index("x"), lax.axis_index("y"), lax.axis_index("c")
    me = 4 * xi + 2 * yi + ci
    pos = jnp.stack([ci, 2 * xi + yi]).astype(jnp.int32)

    gathered = _all_gather(
        [a_w_in[0].astype(BF16), w_kv.astype(BF16), a_w_out[0].astype(BF16), b_w_q[0].astype(BF16),
         b_w_o[0].astype(BF16), f_w_in.astype(BF16), f_w_down.astype(BF16),
         jnp.concatenate([a_norm_g, a_v_norm_g], axis=0), f_conv_w], "gather_weights")
    wa_in, wkv, wa_out, wq, wo, wf_in, wf_down, norms_sh, conv_w = gathered
    wa_out, wq, wo = (t.reshape(D, D) for t in (wa_out, wq, wo))
    ga = jnp.transpose(norms_sh, (1, 0, 2)).reshape(2, D)
    g_a, g_av = ga[0:1], ga[1:2]

    x2 = x.reshape(T, D)
    tgt = loss_target.reshape(T, D)
    pc = jnp.arange(GMLP_BLOCK) // CHUNK
    mask = (pc[:, None] >= pc[None, :]).astype(F32)
    ws = (a_w_s[0] * mask[None]).astype(BF16)
    bst = jnp.transpose(a_b_s[0])
    rel = jnp.pad(b_rel_bias[0], ((0, 0), (0, REL_PAD - b_rel_bias.shape[2])))

    def ffn_weights(l):
        return (wf_in[:, l], conv_w[:, l], f_conv_b[l].reshape(8, 1, F), wf_down[:, l].reshape(4, F, D))

    bias = _bias_build(rel, "bias_build")
    z = _norm_matmul(x2, g_a, wa_in, flat=True, nbk=4, name="gmlp_in")
    gated, h1 = _gmlp_forward(z, ws, bst, g_av, wa_out, x2, name="gmlp_mix")
    w_in0, cw0, cb0, wd0 = ffn_weights(0)
    a0 = _norm_matmul(h1, f_norm_g[0:1], w_in0, flat=False, nbk=2, name="ffn0_in")
    act0, h2 = _ffn_forward(a0, cw0, cb0, wd0, h1, S, name="ffn0_out")
    kv = _norm_matmul(h2, kv_norm_g.reshape(1, D), wkv, flat=True, nbk=4, name="kv_proj")
    q = _norm_matmul(h2, b_norm_g, wq.reshape(1, D, D), flat=True, nbk=1, name="q_proj")
    kvp = jnp.pad(kv.reshape(B, S, 2 * D), ((0, 0), (PAD, 0), (0, 0)))
    o = _attn_forward(q, kvp, bias, name="attn")
    h3 = _matmul_residual(o, wo, h2, "attn_out")
    w_in1, cw1, cb1, wd1 = ffn_weights(1)
    a1 = _norm_matmul(h3, f_norm_g[1:2], w_in1, flat=False, nbk=2, name="ffn1_in")
    act1, h4 = _ffn_forward(a1, cw1, cb1, wd1, h3, S, name="ffn1_out")

    dh4, st_final = _loss_head(h4, final_norm_g.reshape(1, D), tgt, "loss_head")
    da1, st_conv1 = _ffn_backward(dh4, a1, cw1, cb1, wd1, S, name="ffn1_bwd")
    g_wd1 = _wgrad_rows(act1, dh4, flat=False, tk=F, name="ffn1_dwdown")
    g_win1 = _wgrad_cols(h3, f_norm_g[1:2], da1, flat=False, nb=8, nbk=2, name="ffn1_dwin")
    dh3, st_f1 = _matmul_nt(da1, w_in1, flat=False, nbk=2, name="ffn1_dx", norm=(h3, f_norm_g[1:2], dh4))
    d_o = _matmul_nt(dh3, wo.reshape(1, D, D), flat=True, nbk=1, name="attn_out_dx")
    g_wo = _wgrad_rows(o, dh3, flat=True, tk=_tile(D, 512), name="attn_out_dw")
    dq, dkp, dvp, dbias = _attn_backward(q, kvp, bias, d_o, name="attn_bwd")
    g_rel = _bias_reduce(dbias, "bias_reduce")
    g_wq = _wgrad_cols(h2, b_norm_g, dq, flat=True, nb=1, nbk=1, name="q_dw")
    dh2, st_b = _matmul_nt(dq, wq.reshape(1, D, D), flat=True, nbk=1, name="q_dx", norm=(h2, b_norm_g, dh3))
    dkv = jnp.concatenate([dkp[:, PAD:], dvp[:, PAD:]], axis=-1).reshape(T, 2 * D)
    g_wkv = _wgrad_cols(h2, kv_norm_g.reshape(1, D), dkv, flat=True, nb=8, nbk=4, name="kv_dw")
    dh2, st_kv = _matmul_nt(dkv, wkv, flat=True, nbk=4, name="kv_dx", norm=(h2, kv_norm_g.reshape(1, D), dh2))
    da0, st_conv0 = _ffn_backward(dh2, a0, cw0, cb0, wd0, S, name="ffn0_bwd")
    g_wd0 = _wgrad_rows(act0, dh2, flat=False, tk=F, name="ffn0_dwdown")
    g_win0 = _wgrad_cols(h1, f_norm_g[0:1], da0, flat=False, nb=8, nbk=2, name="ffn0_dwin")
    dh1, st_f0 = _matmul_nt(da0, w_in0, flat=False, nbk=2, name="ffn0_dx", norm=(h1, f_norm_g[0:1], dh2))
    dgated = _matmul_nt(dh1, wa_out.reshape(1, D, D), flat=True, nbk=1, name="gmlp_out_dx")
    g_wa_out = _wgrad_rows(gated, dh1, flat=True, tk=_tile(D, 512), name="gmlp_out_dw")
    dz, g_ws, g_bst, st_av = _gmlp_backward(z, dgated, ws, bst, g_av, mask, name="gmlp_bwd")
    g_wa_in = _wgrad_cols(x2, g_a, dz, flat=True, nb=8, nbk=4, name="gmlp_in_dw")
    grad_x, st_a = _matmul_nt(dz, wa_in, flat=True, nbk=4, name="gmlp_in_dx", norm=(x2, g_a, dh1))

    big = [g_wa_in, g_wkv, g_wa_out.reshape(8, dn, D), g_wq.reshape(8, dn, D), g_wo.reshape(8, dn, D),
           g_win0, g_win1, g_wd0.reshape(8, F // 2, D), g_wd1.reshape(8, F // 2, D)]
    big4 = [t.reshape((4, 2) + t.shape[1:]) for t in big]
    landed = _sibling_exchange(big4, "grad_sibling_exchange")
    sums = [_sibling_sum(p, l, pos, "grad_sibling_sum_%d" % k) for k, (p, l) in enumerate(zip(big4, landed))]
    from_chips = _chip_exchange([s[1] for s in sums], "grad_chip_exchange")

    def big_update(k, w, m, v):
        shape = w.shape
        r = lambda t: t.reshape(-1, shape[-1])
        outs = _adamw_big(r(w), r(m), r(v), sums[k][0], from_chips[k], "adamw_big_%d" % k)
        return [t.reshape(shape) for t in outs]

    u_a_w_in = big_update(0, a_w_in, m_a_w_in, v_a_w_in)
    u_w_kv = big_update(1, w_kv, m_w_kv, v_w_kv)
    u_a_w_out = big_update(2, a_w_out, m_a_w_out, v_a_w_out)
    u_b_w_q = big_update(3, b_w_q, m_b_w_q, v_b_w_q)
    u_b_w_o = big_update(4, b_w_o, m_b_w_o, v_b_w_o)
    u_f_w_in = [jnp.stack(ts) for ts in zip(*[big_update(5 + l, f_w_in[l], m_f_w_in[l], v_f_w_in[l]) for l in range(L)])]
    u_f_w_down = [jnp.stack(ts) for ts in zip(*[big_update(7 + l, f_w_down[l], m_f_w_down[l], v_f_w_down[l])
                                               for l in range(L)])]

    vec = jnp.concatenate([st_a[0:1], st_av[0:1], st_kv[0:1], st_b[0:1], st_f0[0:1], st_f1[0:1], st_final[0:2]], axis=0)
    small = _all_gather([vec, g_ws, g_bst, g_rel, st_conv0, st_conv1], "gather_small_grads")
    vec, g_ws, g_bst, g_rel, st_conv0, st_conv1 = _sum_devices(small, "sum_small_grads")
    loss = vec[7, 0]
    g_a_norm = lax.dynamic_slice_in_dim(vec[0:1], me * dn, dn, axis=1)
    g_av_norm = lax.dynamic_slice_in_dim(vec[1:2], me * dn, dn, axis=1)
    st_conv = jnp.stack([st_conv0, st_conv1])
    g_conv_w = lax.dynamic_index_in_dim(st_conv, me, axis=1, keepdims=False)[:, 0:3]
    g_conv_b = st_conv[:, :, 3, :].reshape(L, 8 * F)
    small_items = [
        (a_norm_g, m_a_norm_g, v_a_norm_g, g_a_norm),
        (a_v_norm_g, m_a_v_norm_g, v_a_v_norm_g, g_av_norm),
        (a_w_s, m_a_w_s, v_a_w_s, g_ws[None]),
        (a_b_s, m_a_b_s, v_a_b_s, jnp.transpose(g_bst)[None]),
        (kv_norm_g.reshape(1, D), m_kv_norm_g.reshape(1, D), v_kv_norm_g.reshape(1, D), vec[2:3]),
        (b_norm_g, m_b_norm_g, v_b_norm_g, vec[3:4]),
        (b_rel_bias, m_b_rel_bias, v_b_rel_bias, g_rel[None, :, :b_rel_bias.shape[2]]),
        (f_norm_g, m_f_norm_g, v_f_norm_g, vec[4:6]),
        (f_conv_w, m_f_conv_w, v_f_conv_w, g_conv_w),
        (f_conv_b, m_f_conv_b, v_f_conv_b, g_conv_b),
        (final_norm_g.reshape(1, D), m_final_norm_g.reshape(1, D), v_final_norm_g.reshape(1, D), vec[6:7]),
    ]
    small_out = _adamw_small(small_items, "adamw_small")
    (u_a_norm, u_av_norm, u_ws, u_bs, u_kvn, u_bn, u_rel, u_fn, u_cw, u_cb, u_fin) = [
        (it[3],) + so for it, so in zip(small_items, small_out)]
    vecD = lambda u: tuple(t.reshape(D) for t in u)
    u_kvn, u_fin = vecD(u_kvn), vecD(u_fin)

    order = [u_a_norm, u_a_w_in, u_av_norm, u_ws, u_bs, u_a_w_out, u_kvn, u_w_kv, u_bn, u_b_w_q, u_rel, u_b_w_o,
             u_fn, u_f_w_in, u_cw, u_cb, u_f_w_down, u_fin]
    outs = [loss, grad_x.reshape(B, S, D)]
    for k in range(4):
        outs += [u[k] for u in order]
    return tuple(outs)
```

```python
import functools

import jax
import jax.numpy as jnp
from jax import lax
from jax.experimental import pallas as pl
from jax.experimental.pallas import tpu as pltpu

F32 = jnp.float32
BF16 = jnp.bfloat16
MESH = pl.DeviceIdType.MESH

N_DEV = 8
EPS = 1e-6
NEG_INF = -1e30
CHUNK = 64
LEFT_CHUNKS = 8
REL_CLIP = 128
HEAD_DIM = 64
GMLP_BLOCK = 128
Q_TILE = 2 * CHUNK
PAD = LEFT_CHUNKS * CHUNK
WIN = PAD + Q_TILE
SKEW = WIN + Q_TILE
REL_PAD = 384
FWD_HEADS_PER_STEP = 8
BWD_HEADS_PER_STEP = 4
STRIP = 32
ADAM_LR, ADAM_B1, ADAM_B2, ADAM_EPS, ADAM_WD, ADAM_STEP = 0.001, 0.9, 0.999, 1e-08, 0.01, 10
VMEM_LIMIT = 56 * 1024 * 1024
TOKEN_TILE = 512


def _params(sem=None):
    return pltpu.CompilerParams(dimension_semantics=sem, vmem_limit_bytes=VMEM_LIMIT)


def _tile(n, pref):
    t = min(n, pref)
    while n % t:
        t //= 2
    return t


def _gelu(x):
    return 0.5 * x * (1.0 + jnp.tanh(0.7978845608028654 * (x + 0.044715 * x * x * x)))


def _gelu_grad(x):
    t = jnp.tanh(0.7978845608028654 * (x + 0.044715 * x * x * x))
    return 0.5 * (1.0 + t) + 0.5 * x * (1.0 - t * t) * 0.7978845608028654 * (1.0 + 3 * 0.044715 * x * x)


def _sigmoid(x):
    return 1.0 / (1.0 + jnp.exp(-x))


def _dot(a, b):
    return jnp.dot(a, b, preferred_element_type=F32)


def _dot_nt(a, b):
    return lax.dot_general(a, b, (((1,), (1,)), ((), ())), preferred_element_type=F32)


def _dot_tn(a, b):
    return lax.dot_general(a, b, (((0,), (0,)), ((), ())), preferred_element_type=F32)


def _split3(x):
    hi = x.astype(BF16)
    r1 = x - hi.astype(F32)
    mid = r1.astype(BF16)
    lo = (r1 - mid.astype(F32)).astype(BF16)
    return hi, mid, lo


def _mesh_pos():
    return lax.axis_index("x"), lax.axis_index("y"), lax.axis_index("c")


def _all_gather(arrs, name):
    n = len(arrs)

    def body(*refs):
        ins, outs = refs[:n], refs[n:2 * n]
        send_sems, recv_sems, local_sems = refs[2 * n:]
        x, y, c = _mesh_pos()
        me, sibling = (x, y, c), (x, y, 1 - c)
        chips = [(1 - x, y), (x, 1 - y), (1 - x, 1 - y)]

        def slot(a, block):
            px, py, pc = block
            return outs[a].at[4 * px + 2 * py + pc]

        def copy(a, k, block, to, src=None):
            dst = slot(a, block)
            return pltpu.make_async_remote_copy(
                src_ref=dst if src is None else src, dst_ref=dst,
                send_sem=send_sems.at[a, k], recv_sem=recv_sems.at[a, k], device_id=to, device_id_type=MESH)

        mine = [pltpu.make_async_copy(ins[a], slot(a, me), local_sems.at[a]) for a in range(n)]
        for cp in mine:
            cp.start()
        first = []
        for a in range(n):
            first.append(copy(a, 0, me, sibling, src=ins[a]))
            first += [copy(a, 1 + j, me, (*chip, c), src=ins[a]) for j, chip in enumerate(chips)]
        for cp in first:
            cp.start()
        passed = []
        for j, chip in enumerate(chips):
            for a in range(n):
                copy(a, 1 + j, (*chip, c), me).wait_recv()
                fwd = copy(a, 4 + j, (*chip, c), sibling)
                fwd.start()
                passed.append(fwd)
        for a in range(n):
            copy(a, 0, sibling, me).wait_recv()
            for j, chip in enumerate(chips):
                copy(a, 4 + j, (*chip, 1 - c), me).wait_recv()
        for cp in first + passed:
            cp.wait_send()
        for cp in mine:
            cp.wait()

    any_spec = pl.BlockSpec(memory_space=pl.ANY)
    return pl.pallas_call(
        body, name=name,
        out_shape=[jax.ShapeDtypeStruct((N_DEV,) + a.shape, a.dtype) for a in arrs],
        in_specs=[any_spec] * n, out_specs=[any_spec] * n,
        scratch_shapes=[pltpu.SemaphoreType.DMA((n, 7)), pltpu.SemaphoreType.DMA((n, 7)), pltpu.SemaphoreType.DMA((n,))],
    )(*arrs)


def _sibling_exchange(arrs, name):
    n = len(arrs)

    def body(*refs):
        ins, outs = refs[:n], refs[n:2 * n]
        send_sems, recv_sems = refs[2 * n:]
        x, y, c = _mesh_pos()
        cps = [pltpu.make_async_remote_copy(
            src_ref=ins[a].at[:, pl.ds(1 - c, 1)], dst_ref=outs[a],
            send_sem=send_sems.at[a], recv_sem=recv_sems.at[a], device_id=(x, y, 1 - c), device_id_type=MESH)
            for a in range(n)]
        for cp in cps:
            cp.start()
        for cp in cps:
            cp.wait()

    any_spec = pl.BlockSpec(memory_space=pl.ANY)
    return pl.pallas_call(
        body, name=name,
        out_shape=[jax.ShapeDtypeStruct((4, 1) + a.shape[2:], a.dtype) for a in arrs],
        in_specs=[any_spec] * n, out_specs=[any_spec] * n,
        scratch_shapes=[pltpu.SemaphoreType.DMA((n,)), pltpu.SemaphoreType.DMA((n,))],
    )(*arrs)


def _chip_exchange(arrs, name):
    n = len(arrs)

    def body(*refs):
        ins, outs = refs[:n], refs[n:2 * n]
        send_sems, recv_sems = refs[2 * n:]
        x, y, c = _mesh_pos()
        cps = []
        for a in range(n):
            for k in (1, 2, 3):
                px = x if k < 2 else 1 - x
                py = y if k == 2 else 1 - y
                cps.append(pltpu.make_async_remote_copy(
                    src_ref=ins[a].at[2 * px + py], dst_ref=outs[a].at[k - 1],
                    send_sem=send_sems.at[a, k - 1], recv_sem=recv_sems.at[a, k - 1],
                    device_id=(px, py, c), device_id_type=MESH))
        for cp in cps:
            cp.start()
        for cp in cps:
            cp.wait()

    any_spec = pl.BlockSpec(memory_space=pl.ANY)
    return pl.pallas_call(
        body, name=name,
        out_shape=[jax.ShapeDtypeStruct((3,) + a.shape[1:], a.dtype) for a in arrs],
        in_specs=[any_spec] * n, out_specs=[any_spec] * n,
        scratch_shapes=[pltpu.SemaphoreType.DMA((n, 3)), pltpu.SemaphoreType.DMA((n, 3))],
    )(*arrs)


def _sibling_sum(part, landed, pos, name):
    _, _, rows, cols = part.shape
    tr = _tile(rows, 256)

    def body(pos_ref, p_ref, l_ref, own_ref, all_ref):
        s = p_ref[0, 0] + l_ref[0, 0]
        all_ref[0] = s.astype(BF16)

        @pl.when(pl.program_id(1) == pos_ref[1])
        def _():
            own_ref[...] = s

    return pl.pallas_call(
        body, name=name,
        grid_spec=pltpu.PrefetchScalarGridSpec(
            num_scalar_prefetch=1, grid=(rows // tr, 4),
            in_specs=[pl.BlockSpec((1, 1, tr, cols), lambda i, k, pos: (k, pos[0], i, 0)),
                      pl.BlockSpec((1, 1, tr, cols), lambda i, k, pos: (k, 0, i, 0))],
            out_specs=[pl.BlockSpec((tr, cols), lambda i, k, pos: (i, 0)),
                       pl.BlockSpec((1, tr, cols), lambda i, k, pos: (k, i, 0))]),
        out_shape=[jax.ShapeDtypeStruct((rows, cols), F32), jax.ShapeDtypeStruct((4, rows, cols), BF16)],
        compiler_params=_params(("arbitrary", "arbitrary")),
    )(pos, part, landed)


def _rms(x, g):
    r = lax.rsqrt(jnp.mean(x * x, axis=-1, keepdims=True) + EPS)
    return x * r, r


def _norm_matmul(h, g, w, *, flat, nbk, name, scale=1.0):
    T, D = h.shape
    nb, _, bn = w.shape
    tm = _tile(T, TOKEN_TILE)

    def body(h_ref, g_ref, w_ref, o_ref, n_ref):
        @pl.when(pl.program_id(1) == 0)
        def _():
            xh, _ = _rms(h_ref[...], None)
            n_ref[...] = (xh * g_ref[...]).astype(BF16)

        n = n_ref[...]
        for k in range(nbk):
            r = _dot(n, w_ref[k])
            r = (r if scale == 1.0 else r * scale).astype(BF16)
            if flat:
                o_ref[:, k * bn:(k + 1) * bn] = r
            else:
                o_ref[k] = r

    if flat:
        out_shape = jax.ShapeDtypeStruct((T, nb * bn), BF16)
        out_spec = pl.BlockSpec((tm, nbk * bn), lambda i, j: (i, j))
    else:
        out_shape = jax.ShapeDtypeStruct((nb, T, bn), BF16)
        out_spec = pl.BlockSpec((nbk, tm, bn), lambda i, j: (j, i, 0))
    return pl.pallas_call(
        body, name=name, grid=(T // tm, nb // nbk),
        in_specs=[pl.BlockSpec((tm, D), lambda i, j: (i, 0)),
                  pl.BlockSpec((1, D), lambda i, j: (0, 0)),
                  pl.BlockSpec((nbk, D, bn), lambda i, j: (j, 0, 0))],
        out_specs=[out_spec, pl.BlockSpec((tm, D), lambda i, j: (i, 0))],
        out_shape=[out_shape, jax.ShapeDtypeStruct((T, D), BF16)],
        compiler_params=_params(("arbitrary", "arbitrary")),
    )(h, g, w)


def _matmul_nt(dy, w, *, flat, nbk, name, norm=None, out_dtype=BF16):
    nb, R, bn = w.shape
    T = dy.shape[0] if flat else dy.shape[1]
    tm = _tile(T, TOKEN_TILE)
    nj = nb // nbk

    def body(*refs):
        if norm is None:
            dy_ref, w_ref, o_ref, acc_ref = refs
        else:
            dy_ref, w_ref, h_ref, g_ref, dres_ref, o_ref, dg_ref, acc_ref = refs
        i, j = pl.program_id(0), pl.program_id(1)

        @pl.when(j == 0)
        def _():
            acc_ref[...] = jnp.zeros_like(acc_ref)

        acc = acc_ref[...]
        for k in range(nbk):
            d = dy_ref[:, k * bn:(k + 1) * bn] if flat else dy_ref[k]
            acc = acc + _dot_nt(d.astype(BF16), w_ref[k])
        acc_ref[...] = acc

        @pl.when(j == nj - 1)
        def _():
            if norm is None:
                o_ref[...] = acc.astype(out_dtype)
            else:
                xh, r = _rms(h_ref[...], None)

                @pl.when(i == 0)
                def _():
                    dg_ref[...] = jnp.zeros_like(dg_ref)

                dg_ref[0:1, :] += jnp.sum(acc * xh, axis=0, keepdims=True)
                dn = acc * g_ref[...]
                o_ref[...] = dres_ref[...] + r * (dn - xh * jnp.mean(dn * xh, axis=-1, keepdims=True))

    if flat:
        dy_spec = pl.BlockSpec((tm, nbk * bn), lambda i, j: (i, j))
    else:
        dy_spec = pl.BlockSpec((nbk, tm, bn), lambda i, j: (j, i, 0))
    w_spec = pl.BlockSpec((nbk, R, bn), lambda i, j: (j, 0, 0))
    row_spec = pl.BlockSpec((tm, R), lambda i, j: (i, 0))
    if norm is None:
        in_specs, args = [dy_spec, w_spec], (dy, w)
        out_specs = row_spec
        out_shape = jax.ShapeDtypeStruct((T, R), out_dtype)
    else:
        in_specs = [dy_spec, w_spec, row_spec, pl.BlockSpec((1, R), lambda i, j: (0, 0)), row_spec]
        args = (dy, w) + tuple(norm)
        out_specs = [row_spec, pl.BlockSpec((8, R), lambda i, j: (0, 0))]
        out_shape = [jax.ShapeDtypeStruct((T, R), F32), jax.ShapeDtypeStruct((8, R), F32)]
    return pl.pallas_call(
        body, name=name, grid=(T // tm, nj), in_specs=in_specs, out_specs=out_specs, out_shape=out_shape,
        scratch_shapes=[pltpu.VMEM((tm, R), F32)],
        compiler_params=_params(("arbitrary", "arbitrary")),
    )(*args)


def _wgrad_cols(n, dy, *, flat, nb, nbk, name):
    T, D = n.shape
    bn = dy.shape[1] // nb if flat else dy.shape[2]
    tt = _tile(T, TOKEN_TILE)
    nt = T // tt

    def body(n_ref, dy_ref, o_ref, acc_ref):
        t = pl.program_id(1)

        @pl.when(t == 0)
        def _():
            acc_ref[...] = jnp.zeros_like(acc_ref)

        nv = n_ref[...]
        for k in range(nbk):
            d = dy_ref[:, k * bn:(k + 1) * bn] if flat else dy_ref[k]
            acc_ref[k] += _dot_tn(nv, d)

        @pl.when(t == nt - 1)
        def _():
            o_ref[...] = acc_ref[...]

    if flat:
        dy_spec = pl.BlockSpec((tt, nbk * bn), lambda j, t: (t, j))
    else:
        dy_spec = pl.BlockSpec((nbk, tt, bn), lambda j, t: (j, t, 0))
    return pl.pallas_call(
        body, name=name, grid=(nb // nbk, nt),
        in_specs=[pl.BlockSpec((tt, D), lambda j, t: (t, 0)), dy_spec],
        out_specs=pl.BlockSpec((nbk, D, bn), lambda j, t: (j, 0, 0)),
        out_shape=jax.ShapeDtypeStruct((nb, D, bn), F32),
        scratch_shapes=[pltpu.VMEM((nbk, D, bn), F32)],
        compiler_params=_params(("arbitrary", "arbitrary")),
    )(n, dy)


def _wgrad_rows(xa, dh, *, flat, tk, name):
    T, D = dh.shape
    nk = xa.shape[1] // tk if flat else xa.shape[0]
    tt = _tile(T, TOKEN_TILE)
    nt = T // tt

    def body(x_ref, dh_ref, o_ref, acc_ref):
        t = pl.program_id(1)

        @pl.when(t == 0)
        def _():
            acc_ref[...] = jnp.zeros_like(acc_ref)

        xv = x_ref[...] if flat else x_ref[0]
        acc_ref[...] += _dot_tn(xv, dh_ref[...].astype(BF16))

        @pl.when(t == nt - 1)
        def _():
            o_ref[...] = acc_ref[...]

    x_spec = pl.BlockSpec((tt, tk), lambda j, t: (t, j)) if flat else pl.BlockSpec((1, tt, tk), lambda j, t: (j, t, 0))
    return pl.pallas_call(
        body, name=name, grid=(nk, nt),
        in_specs=[x_spec, pl.BlockSpec((tt, D), lambda j, t: (t, 0))],
        out_specs=pl.BlockSpec((tk, D), lambda j, t: (j, 0)),
        out_shape=jax.ShapeDtypeStruct((nk * tk, D), F32),
        scratch_shapes=[pltpu.VMEM((tk, D), F32)],
        compiler_params=_params(("arbitrary", "arbitrary")),
    )(xa, dh)


def _matmul_residual(xa, w, res, name):
    T, K = xa.shape
    D = w.shape[1]
    tm = _tile(T, TOKEN_TILE)

    def body(x_ref, w_ref, r_ref, o_ref):
        o_ref[...] = r_ref[...] + _dot(x_ref[...], w_ref[...])

    return pl.pallas_call(
        body, name=name, grid=(T // tm,),
        in_specs=[pl.BlockSpec((tm, K), lambda i: (i, 0)), pl.BlockSpec((K, D), lambda i: (0, 0)),
                  pl.BlockSpec((tm, D), lambda i: (i, 0))],
        out_specs=pl.BlockSpec((tm, D), lambda i: (i, 0)),
        out_shape=jax.ShapeDtypeStruct((T, D), F32),
        compiler_params=_params(("arbitrary",)),
    )(xa, w, res)


def _gmlp_gate(z, ws, bst, gv, G, gd):
    D = G * gd
    u = _gelu(z[:, :D].astype(F32))
    v = _gelu(z[:, D:].astype(F32))
    vh, r = _rms(v, None)
    vn = (vh * gv).astype(BF16)
    return u, v, vh, r, vn


def _gmlp_forward(z, ws, bst, gv, w_out, x, *, name):
    T, D2 = z.shape
    D = D2 // 2
    G = ws.shape[0]
    gd = D // G
    tb = _tile(T, 256)
    nblk = tb // GMLP_BLOCK

    def body(z_ref, ws_ref, b_ref, gv_ref, wo_ref, x_ref, gated_ref, h_ref):
        u, _, _, _, vn = _gmlp_gate(z_ref[...], None, None, gv_ref[...], G, gd)
        for n in range(nblk):
            rows = slice(n * GMLP_BLOCK, (n + 1) * GMLP_BLOCK)
            for gi in range(G):
                cols = slice(gi * gd, (gi + 1) * gd)
                s = _dot(ws_ref[gi], vn[rows, cols]) + b_ref[:, gi:gi + 1]
                gated_ref[rows, cols] = (u[rows, cols] * s).astype(BF16)
        h_ref[...] = x_ref[...] + _dot(gated_ref[...], wo_ref[...])

    return pl.pallas_call(
        body, name=name, grid=(T // tb,),
        in_specs=[pl.BlockSpec((tb, D2), lambda i: (i, 0)), pl.BlockSpec(ws.shape, lambda i: (0, 0, 0)),
                  pl.BlockSpec(bst.shape, lambda i: (0, 0)), pl.BlockSpec((1, D), lambda i: (0, 0)),
                  pl.BlockSpec((D, D), lambda i: (0, 0)), pl.BlockSpec((tb, D), lambda i: (i, 0))],
        out_specs=[pl.BlockSpec((tb, D), lambda i: (i, 0)), pl.BlockSpec((tb, D), lambda i: (i, 0))],
        out_shape=[jax.ShapeDtypeStruct((T, D), BF16), jax.ShapeDtypeStruct((T, D), F32)],
        compiler_params=_params(("arbitrary",)),
    )(z, ws, bst, gv, w_out, x)


def _gmlp_backward(z, dgated, ws, bst, gv, mask, *, name):
    T, D2 = z.shape
    D = D2 // 2
    G = ws.shape[0]
    gd = D // G
    tb = _tile(T, 256)
    nblk = tb // GMLP_BLOCK

    def body(z_ref, dg_ref, ws_ref, b_ref, gv_ref, mask_ref, dz_ref, dws_ref, db_ref, dgv_ref, dvn_ref):
        @pl.when(pl.program_id(0) == 0)
        def _():
            dws_ref[...] = jnp.zeros_like(dws_ref)
            db_ref[...] = jnp.zeros_like(db_ref)
            dgv_ref[...] = jnp.zeros_like(dgv_ref)

        zf = z_ref[...]
        u, v, vh, r, vn = _gmlp_gate(zf, None, None, gv_ref[...], G, gd)
        dg = dg_ref[...].astype(F32)
        for n in range(nblk):
            rows = slice(n * GMLP_BLOCK, (n + 1) * GMLP_BLOCK)
            for gi in range(G):
                cols = slice(gi * gd, (gi + 1) * gd)
                vblk = vn[rows, cols]
                s = _dot(ws_ref[gi], vblk) + b_ref[:, gi:gi + 1]
                dgb = dg[rows, cols]
                ds = dgb * u[rows, cols]
                dsb = ds.astype(BF16)
                dz_ref[rows, cols] = (dgb * s * _gelu_grad(zf[rows, cols].astype(F32))).astype(BF16)
                dvn_ref[rows, cols] = _dot_tn(ws_ref[gi], dsb)
                dws_ref[gi] += _dot_nt(dsb, vblk) * mask_ref[...]
                db_ref[:, gi:gi + 1] += jnp.sum(ds, axis=1, keepdims=True)
        dvn = dvn_ref[...]
        dgv_ref[0:1, :] += jnp.sum(dvn * vh, axis=0, keepdims=True)
        dn = dvn * gv_ref[...]
        dv = r * (dn - vh * jnp.mean(dn * vh, axis=-1, keepdims=True))
        dz_ref[:, D:] = (dv * _gelu_grad(zf[:, D:].astype(F32))).astype(BF16)

    return pl.pallas_call(
        body, name=name, grid=(T // tb,),
        in_specs=[pl.BlockSpec((tb, D2), lambda i: (i, 0)), pl.BlockSpec((tb, D), lambda i: (i, 0)),
                  pl.BlockSpec(ws.shape, lambda i: (0, 0, 0)), pl.BlockSpec(bst.shape, lambda i: (0, 0)),
                  pl.BlockSpec((1, D), lambda i: (0, 0)), pl.BlockSpec(mask.shape, lambda i: (0, 0))],
        out_specs=[pl.BlockSpec((tb, D2), lambda i: (i, 0)), pl.BlockSpec(ws.shape, lambda i: (0, 0, 0)),
                   pl.BlockSpec(bst.shape, lambda i: (0, 0)), pl.BlockSpec((8, D), lambda i: (0, 0))],
        out_shape=[jax.ShapeDtypeStruct((T, D2), BF16), jax.ShapeDtypeStruct(ws.shape, F32),
                   jax.ShapeDtypeStruct(bst.shape, F32), jax.ShapeDtypeStruct((8, D), F32)],
        scratch_shapes=[pltpu.VMEM((tb, D), F32)],
        compiler_params=_params(("arbitrary",)),
    )(z, dgated, ws, bst, gv, mask)


def _shift_rows(x, k):
    return pltpu.roll(x, k % x.shape[0], axis=0)


def _conv3(ext, cw):
    return (cw[0:1] * _shift_rows(ext, 2)[8:] + cw[1:2] * _shift_rows(ext, 1)[8:] + cw[2:3] * ext[8:])


def _ffn_forward(a, cw, cb, wd, h, seq, *, name):
    _, T, F = a.shape
    D = h.shape[1]
    tm = _tile(seq, TOKEN_TILE)
    hb = tm // 16

    def body(a_ref, ap_ref, cw_ref, cb_ref, wd_ref, h_ref, act_ref, c_ref, o_ref, acc_ref):
        i, j = pl.program_id(0), pl.program_id(1)
        keep = ((i * tm) % seq != 0).astype(F32)

        def conv(b):
            ext = jnp.concatenate([ap_ref[b, 8:16].astype(F32) * keep, a_ref[b].astype(F32)], axis=0)
            return _conv3(ext, cw_ref[b]) + cb_ref[b]

        up, gate = conv(j), conv(j + 4)
        c_ref[j] = up.astype(BF16)
        c_ref[j + 4] = gate.astype(BF16)
        act = (gate * _sigmoid(gate) * up).astype(BF16)
        act_ref[0] = act

        @pl.when(j == 0)
        def _():
            acc_ref[...] = h_ref[...]

        acc_ref[...] += _dot(act, wd_ref[0])

        @pl.when(j == 3)
        def _():
            o_ref[...] = acc_ref[...]

    return pl.pallas_call(
        body, name=name, grid=(T // tm, 4),
        in_specs=[pl.BlockSpec((8, tm, F), lambda i, j: (0, i, 0)),
                  pl.BlockSpec((8, 16, F), lambda i, j: (0, jnp.maximum(i * hb - 1, 0), 0)),
                  pl.BlockSpec((8, 3, F), lambda i, j: (0, 0, 0)), pl.BlockSpec((8, 1, F), lambda i, j: (0, 0, 0)),
                  pl.BlockSpec((1, F, D), lambda i, j: (j, 0, 0)), pl.BlockSpec((tm, D), lambda i, j: (i, 0))],
        out_specs=[pl.BlockSpec((1, tm, F), lambda i, j: (j, i, 0)), pl.BlockSpec((8, tm, F), lambda i, j: (0, i, 0)),
                   pl.BlockSpec((tm, D), lambda i, j: (i, 0))],
        out_shape=[jax.ShapeDtypeStruct((4, T, F), BF16), jax.ShapeDtypeStruct((8, T, F), BF16),
                   jax.ShapeDtypeStruct((T, D), F32)],
        scratch_shapes=[pltpu.VMEM((tm, D), F32)],
        compiler_params=_params(("arbitrary", "arbitrary")),
    )(a, a, cw, cb, wd, h)


def _ffn_backward(dh, c, a, cw, wd, seq, *, name):
    _, T, F = a.shape
    D = dh.shape[1]
    tm = _tile(seq, TOKEN_TILE)
    hb = tm // 16
    nt = T // tm

    def body(dh_ref, dhn_ref, cu_ref, cg_ref, cun_ref, cgn_ref, au_ref, ag_ref, cw_ref, wd_ref, da_ref, st_ref):
        i, j = pl.program_id(0), pl.program_id(1)
        keep_next = (((i + 1) * tm) % seq != 0).astype(F32)

        @pl.when((i == 0) & (j == 0))
        def _():
            st_ref[...] = jnp.zeros_like(st_ref)

        dhe = jnp.concatenate([dh_ref[...], dhn_ref[...] * keep_next], axis=0).astype(BF16)
        dact = _dot_nt(dhe, wd_ref[0])
        up = jnp.concatenate([cu_ref[0].astype(F32), cun_ref[0, 0:8].astype(F32)], axis=0)
        gate = jnp.concatenate([cg_ref[0].astype(F32), cgn_ref[0, 0:8].astype(F32)], axis=0)
        sg = _sigmoid(gate)
        gs = gate * sg
        d_up = dact * gs
        d_gate = dact * up * (sg + gs * (1.0 - sg))

        def finish(b, a_ref, dc):
            w = cw_ref[b]
            dm, u1, u2 = dc[:tm], _shift_rows(dc, -1)[:tm], _shift_rows(dc, -2)[:tm]
            da_ref[b] = (w[2:3] * dm + w[1:2] * u1 + w[0:1] * u2).astype(BF16)
            av = a_ref[0].astype(F32)
            st_ref[b, 0:1, :] += jnp.sum(u2 * av, axis=0, keepdims=True)
            st_ref[b, 1:2, :] += jnp.sum(u1 * av, axis=0, keepdims=True)
            st_ref[b, 2:3, :] += jnp.sum(dm * av, axis=0, keepdims=True)
            st_ref[b, 3:4, :] += jnp.sum(dm, axis=0, keepdims=True)

        finish(j, au_ref, d_up)
        finish(j + 4, ag_ref, d_gate)

    nxt = lambda i: jnp.minimum((i + 1) * hb, T // 16 - 1)
    return pl.pallas_call(
        body, name=name, grid=(nt, 4),
        in_specs=[pl.BlockSpec((tm, D), lambda i, j: (i, 0)),
                  pl.BlockSpec((8, D), lambda i, j: (jnp.minimum((i + 1) * (tm // 8), T // 8 - 1), 0)),
                  pl.BlockSpec((1, tm, F), lambda i, j: (j, i, 0)), pl.BlockSpec((1, tm, F), lambda i, j: (j + 4, i, 0)),
                  pl.BlockSpec((1, 16, F), lambda i, j: (j, nxt(i), 0)),
                  pl.BlockSpec((1, 16, F), lambda i, j: (j + 4, nxt(i), 0)),
                  pl.BlockSpec((1, tm, F), lambda i, j: (j, i, 0)), pl.BlockSpec((1, tm, F), lambda i, j: (j + 4, i, 0)),
                  pl.BlockSpec((8, 3, F), lambda i, j: (0, 0, 0)),
                  pl.BlockSpec((1, F, D), lambda i, j: (j, 0, 0))],
        out_specs=[pl.BlockSpec((8, tm, F), lambda i, j: (0, i, 0)), pl.BlockSpec((8, 8, F), lambda i, j: (0, 0, 0))],
        out_shape=[jax.ShapeDtypeStruct((8, T, F), BF16), jax.ShapeDtypeStruct((8, 8, F), F32)],
        compiler_params=_params(("arbitrary", "arbitrary")),
    )(dh, dh, c, c, c, c, a, a, cw, wd)


def _rel_onehot():
    r = lax.broadcasted_iota(jnp.int32, (REL_PAD, SKEW), 0)
    n = lax.broadcasted_iota(jnp.int32, (REL_PAD, SKEW), 1)
    off = jnp.where(n >= WIN, n - SKEW, n)
    idx = jnp.minimum(PAD - off, REL_CLIP) + REL_CLIP
    return (r == idx).astype(BF16)


def _skew(x, sign):
    row = lax.broadcasted_iota(jnp.int32, x.shape, 0)
    for b in range(7):
        x = jnp.where((row >> b) & 1 == 1, pltpu.roll(x, (sign * (1 << b)) % SKEW, axis=1), x)
    return x


def _bias_build(rel, name):
    H = rel.shape[0]

    def body(rel_ref, o_ref):
        oh = _rel_onehot()
        hi, mid, lo = _split3(rel_ref[...])
        base = _dot(hi, oh) + _dot(mid, oh) + _dot(lo, oh)
        q = lax.broadcasted_iota(jnp.int32, (Q_TILE, WIN), 0)
        k = lax.broadcasted_iota(jnp.int32, (Q_TILE, WIN), 1)
        ok = ((q < CHUNK) & (k < WIN - CHUNK)) | ((q >= CHUNK) & (k >= CHUNK))
        for hd in range(H):
            t = _skew(jnp.broadcast_to(base[hd:hd + 1, :], (Q_TILE, SKEW)), 1)
            o_ref[hd] = jnp.where(ok, t[:, :WIN], NEG_INF)

    return pl.pallas_call(
        body, name=name, out_shape=jax.ShapeDtypeStruct((H, Q_TILE, WIN), F32),
        in_specs=[pl.BlockSpec(memory_space=pltpu.VMEM)], out_specs=pl.BlockSpec(memory_space=pltpu.VMEM),
        compiler_params=_params(),
    )(rel)


def _bias_reduce(dbias, name):
    H = dbias.shape[0]

    def body(d_ref, o_ref, e_ref):
        oh = _rel_onehot()
        for hd in range(H):
            x = jnp.concatenate([d_ref[hd], jnp.zeros((Q_TILE, SKEW - WIN), F32)], axis=1)
            e_ref[hd:hd + 1, :] = jnp.sum(_skew(x, -1), axis=0, keepdims=True)
        hi, mid, lo = _split3(e_ref[...])
        o_ref[...] = _dot_nt(hi, oh) + _dot_nt(mid, oh) + _dot_nt(lo, oh)

    return pl.pallas_call(
        body, name=name, out_shape=jax.ShapeDtypeStruct((H, REL_PAD), F32),
        in_specs=[pl.BlockSpec(memory_space=pltpu.VMEM)], out_specs=pl.BlockSpec(memory_space=pltpu.VMEM),
        scratch_shapes=[pltpu.VMEM((H, SKEW), F32)],
        compiler_params=_params(),
    )(dbias)


def _pair_stack(xp, even):
    z = jnp.zeros_like(xp)
    return jnp.concatenate([jnp.where(even, xp, z), jnp.where(even, z, xp)], axis=0)


def _pair_merge(y, even):
    return jnp.where(even, y[:Q_TILE], y[Q_TILE:])


def _strip_probs(s_ref, b_ref, pp, r, valid):
    hb, hr = divmod(r, Q_TILE)
    s = s_ref[pp, r:r + STRIP, :] + b_ref[2 * pp + hb, hr:hr + STRIP, :]
    s = jnp.where(valid, s, NEG_INF)
    e = jnp.exp(s - jnp.max(s, axis=-1, keepdims=True))
    return e * (1.0 / jnp.sum(e, axis=-1, keepdims=True))


def _fill_padded(dst_ref, src_ref):
    dst_ref[0:PAD, :] = jnp.zeros((PAD, dst_ref.shape[1]), dst_ref.dtype)
    dst_ref[PAD:, :] = src_ref[...]


def _attn_specs(B, S, D, lanes):
    nt = S // Q_TILE
    q_spec = pl.BlockSpec((Q_TILE, lanes), lambda g, b, i: (b * nt + i, g))
    k_spec = pl.BlockSpec((S, lanes), lambda g, b, i: (b, g))
    v_spec = pl.BlockSpec((S, lanes), lambda g, b, i: (b, D // lanes + g))
    bias_spec = pl.BlockSpec((lanes // HEAD_DIM, Q_TILE, WIN), lambda g, b, i: (g, 0, 0))
    return nt, q_spec, k_spec, v_spec, bias_spec


def _attn_forward(q, kv, bias, S, *, name):
    T, D = q.shape
    B = T // S
    lanes = min(FWD_HEADS_PER_STEP * HEAD_DIM, D)
    nt, q_spec, k_spec, v_spec, bias_spec = _attn_specs(B, S, D, lanes)

    npairs = lanes // (2 * HEAD_DIM)

    def body(q_ref, k_ref, v_ref, b_ref, o_ref, kp_ref, vp_ref, s_ref, p_ref):
        i = pl.program_id(2)

        @pl.when(i == 0)
        def _():
            _fill_padded(kp_ref, k_ref)
            _fill_padded(vp_ref, v_ref)

        start = pl.multiple_of(i * Q_TILE, Q_TILE)
        even = lax.broadcasted_iota(jnp.int32, (1, 2 * HEAD_DIM), 1) < HEAD_DIM
        valid = lax.broadcasted_iota(jnp.int32, (STRIP, WIN), 1) >= PAD - i * Q_TILE
        pair_cols = [slice(pp * 2 * HEAD_DIM, (pp + 1) * 2 * HEAD_DIM) for pp in range(npairs)]
        for pp, cols in enumerate(pair_cols):
            s_ref[pp] = _dot_nt(_pair_stack(q_ref[:, cols], even), kp_ref[pl.ds(start, WIN), cols])
        for pp in range(npairs):
            for r in range(0, 2 * Q_TILE, STRIP):
                p = _strip_probs(s_ref, b_ref, pp, r, valid)
                p_ref[pp, r:r + STRIP, :] = p.astype(BF16)
        for pp, cols in enumerate(pair_cols):
            o_ref[:, cols] = _pair_merge(_dot(p_ref[pp], vp_ref[pl.ds(start, WIN), cols]), even).astype(BF16)

    return pl.pallas_call(
        body, name=name, grid=(D // lanes, B, nt),
        in_specs=[q_spec, k_spec, v_spec, bias_spec], out_specs=q_spec,
        out_shape=jax.ShapeDtypeStruct((T, D), BF16),
        scratch_shapes=[pltpu.VMEM((S + PAD, lanes), BF16), pltpu.VMEM((S + PAD, lanes), BF16),
                        pltpu.VMEM((npairs, 2 * Q_TILE, WIN), F32), pltpu.VMEM((npairs, 2 * Q_TILE, WIN), BF16)],
        compiler_params=_params(("arbitrary", "arbitrary", "arbitrary")),
    )(q, kv, kv, bias)


def _attn_backward(q, kv, bias, do, S, *, name):
    T, D = q.shape
    B = T // S
    H = D // HEAD_DIM
    lanes = min(BWD_HEADS_PER_STEP * HEAD_DIM, D)
    nt, q_spec, k_spec, v_spec, bias_spec = _attn_specs(B, S, D, lanes)
    scale = HEAD_DIM ** -0.5

    npairs = lanes // (2 * HEAD_DIM)

    def body(q_ref, k_ref, v_ref, b_ref, do_ref, dq_ref, dk_ref, dv_ref, db_ref, kp_ref, vp_ref, dka_ref, dva_ref,
             s_ref, dp_ref, p_ref, ds_ref):
        b, i = pl.program_id(1), pl.program_id(2)

        @pl.when((b == 0) & (i == 0))
        def _():
            db_ref[...] = jnp.zeros_like(db_ref)

        @pl.when(i == 0)
        def _():
            _fill_padded(kp_ref, k_ref)
            _fill_padded(vp_ref, v_ref)
            dka_ref[...] = jnp.zeros_like(dka_ref)
            dva_ref[...] = jnp.zeros_like(dva_ref)

        start = pl.multiple_of(i * Q_TILE, Q_TILE)
        even = lax.broadcasted_iota(jnp.int32, (1, 2 * HEAD_DIM), 1) < HEAD_DIM
        valid = lax.broadcasted_iota(jnp.int32, (STRIP, WIN), 1) >= PAD - i * Q_TILE
        pair_cols = [slice(pp * 2 * HEAD_DIM, (pp + 1) * 2 * HEAD_DIM) for pp in range(npairs)]
        for pp, cols in enumerate(pair_cols):
            s_ref[pp] = _dot_nt(_pair_stack(q_ref[:, cols], even), kp_ref[pl.ds(start, WIN), cols])
            dp_ref[pp] = _dot_nt(_pair_stack(do_ref[:, cols], even), vp_ref[pl.ds(start, WIN), cols])
        for pp in range(npairs):
            for r in range(0, 2 * Q_TILE, STRIP):
                hb, hr = divmod(r, Q_TILE)
                p = _strip_probs(s_ref, b_ref, pp, r, valid)
                dp = dp_ref[pp, r:r + STRIP, :]
                ds = p * (dp - jnp.sum(p * dp, axis=-1, keepdims=True))
                db_ref[2 * pp + hb, hr:hr + STRIP, :] += ds
                p_ref[pp, r:r + STRIP, :] = p.astype(BF16)
                ds_ref[pp, r:r + STRIP, :] = ds.astype(BF16)
        for pp, cols in enumerate(pair_cols):
            dsb = ds_ref[pp]
            dq_ref[:, cols] = (_pair_merge(_dot(dsb, kp_ref[pl.ds(start, WIN), cols]), even) * scale).astype(BF16)
            dka_ref[pl.ds(start, WIN), cols] += _dot_tn(dsb, _pair_stack(q_ref[:, cols], even))
            dva_ref[pl.ds(start, WIN), cols] += _dot_tn(p_ref[pp], _pair_stack(do_ref[:, cols], even))

        @pl.when(i == nt - 1)
        def _():
            dk_ref[...] = dka_ref[PAD:, :].astype(BF16)
            dv_ref[...] = dva_ref[PAD:, :].astype(BF16)

    dkv_shape = jax.ShapeDtypeStruct((T, D), BF16)
    return pl.pallas_call(
        body, name=name, grid=(D // lanes, B, nt),
        in_specs=[q_spec, k_spec, v_spec, bias_spec, q_spec],
        out_specs=[q_spec, k_spec, k_spec, bias_spec],
        out_shape=[jax.ShapeDtypeStruct((T, D), BF16), dkv_shape, dkv_shape,
                   jax.ShapeDtypeStruct((H, Q_TILE, WIN), F32)],
        scratch_shapes=[pltpu.VMEM((S + PAD, lanes), BF16), pltpu.VMEM((S + PAD, lanes), BF16),
                        pltpu.VMEM((S + PAD, lanes), F32), pltpu.VMEM((S + PAD, lanes), F32),
                        pltpu.VMEM((npairs, 2 * Q_TILE, WIN), F32), pltpu.VMEM((npairs, 2 * Q_TILE, WIN), F32),
                        pltpu.VMEM((npairs, 2 * Q_TILE, WIN), BF16), pltpu.VMEM((npairs, 2 * Q_TILE, WIN), BF16)],
        compiler_params=_params(("arbitrary", "arbitrary", "arbitrary")),
    )(q, kv, kv, bias, do)


def _loss_head(h, g, target, name):
    T, D = h.shape
    tm = _tile(T, TOKEN_TILE)

    def body(h_ref, g_ref, t_ref, dh_ref, st_ref):
        @pl.when(pl.program_id(0) == 0)
        def _():
            st_ref[...] = jnp.zeros_like(st_ref)

        xh, r = _rms(h_ref[...], None)
        err = xh * g_ref[...] - t_ref[...]
        st_ref[1:2, :] += 0.5 * jnp.sum(jnp.mean(err * err, axis=-1, keepdims=True), axis=0, keepdims=True)
        dy = err * (1.0 / D)
        st_ref[0:1, :] += jnp.sum(dy * xh, axis=0, keepdims=True)
        dn = dy * g_ref[...]
        dh_ref[...] = r * (dn - xh * jnp.mean(dn * xh, axis=-1, keepdims=True))

    row = pl.BlockSpec((tm, D), lambda i: (i, 0))
    return pl.pallas_call(
        body, name=name, grid=(T // tm,),
        in_specs=[row, pl.BlockSpec((1, D), lambda i: (0, 0)), row],
        out_specs=[row, pl.BlockSpec((8, D), lambda i: (0, 0))],
        out_shape=[jax.ShapeDtypeStruct((T, D), F32), jax.ShapeDtypeStruct((8, D), F32)],
        compiler_params=_params(("arbitrary",)),
    )(h, g, target)


def _sum_devices(arrs, name):
    n = len(arrs)

    def body(*refs):
        for a in range(n):
            s = refs[a][0]
            for k in range(1, N_DEV):
                s = s + refs[a][k]
            refs[n + a][...] = s

    vm = pl.BlockSpec(memory_space=pltpu.VMEM)
    return pl.pallas_call(
        body, name=name, out_shape=[jax.ShapeDtypeStruct(a.shape[1:], F32) for a in arrs],
        in_specs=[vm] * n, out_specs=[vm] * n, compiler_params=_params(),
    )(*arrs)


def _adamw_math(w, g, m, v):
    m = ADAM_B1 * m + (1.0 - ADAM_B1) * g
    v = ADAM_B2 * v + (1.0 - ADAM_B2) * (g * g)
    m_hat = m / (1.0 - ADAM_B1 ** ADAM_STEP)
    v_hat = v / (1.0 - ADAM_B2 ** ADAM_STEP)
    delta = -ADAM_LR * (m_hat / (jnp.sqrt(v_hat) + ADAM_EPS) + ADAM_WD * w)
    return delta, m, v


def _adamw_small(items, name):
    n = len(items)

    def body(*refs):
        for a in range(n):
            w, m, v, g = (refs[4 * a + k][...] for k in range(4))
            d, m, v = _adamw_math(w, g, m, v)
            refs[4 * n + 3 * a][...] = d
            refs[4 * n + 3 * a + 1][...] = m
            refs[4 * n + 3 * a + 2][...] = v

    vm = pl.BlockSpec(memory_space=pltpu.VMEM)
    flat = [t for it in items for t in it]
    outs = pl.pallas_call(
        body, name=name,
        out_shape=[jax.ShapeDtypeStruct(it[0].shape, F32) for it in items for _ in range(3)],
        in_specs=[vm] * (4 * n), out_specs=[vm] * (3 * n), compiler_params=_params(),
    )(*flat)
    return [tuple(outs[3 * a:3 * a + 3]) for a in range(n)]


def _adamw_big(w, m, v, own, landed, name):
    R, C = w.shape
    tr = _tile(R, 256)

    def body(w_ref, m_ref, v_ref, own_ref, l_ref, g_ref, d_ref, mo_ref, vo_ref):
        g = own_ref[...]
        for k in range(3):
            g = g + l_ref[k].astype(F32)
        d, mn, vn = _adamw_math(w_ref[...], g, m_ref[...], v_ref[...])
        g_ref[...] = g
        d_ref[...] = d
        mo_ref[...] = mn
        vo_ref[...] = vn

    row = pl.BlockSpec((tr, C), lambda i: (i, 0))
    return pl.pallas_call(
        body, name=name, grid=(R // tr,),
        in_specs=[row, row, row, row, pl.BlockSpec((3, tr, C), lambda i: (0, i, 0))],
        out_specs=[row] * 4, out_shape=[jax.ShapeDtypeStruct((R, C), F32)] * 4,
        compiler_params=_params(("arbitrary",)),
    )(w, m, v, own, landed)


def kernel(x, a_norm_g, a_w_in, a_v_norm_g, a_w_s, a_b_s, a_w_out, kv_norm_g, w_kv, b_norm_g, b_w_q, b_rel_bias, b_w_o, f_norm_g, f_w_in, f_conv_w, f_conv_b, f_w_down, final_norm_g, loss_target, m_a_norm_g, m_a_w_in, m_a_v_norm_g, m_a_w_s, m_a_b_s, m_a_w_out, m_kv_norm_g, m_w_kv, m_b_norm_g, m_b_w_q, m_b_rel_bias, m_b_w_o, m_f_norm_g, m_f_w_in, m_f_conv_w, m_f_conv_b, m_f_w_down, m_final_norm_g, v_a_norm_g, v_a_w_in, v_a_v_norm_g, v_a_w_s, v_a_b_s, v_a_w_out, v_kv_norm_g, v_w_kv, v_b_norm_g, v_b_w_q, v_b_rel_bias, v_b_w_o, v_f_norm_g, v_f_w_in, v_f_conv_w, v_f_conv_b, v_f_w_down, v_final_norm_g):
    B, S, D = x.shape
    T = B * S
    G = a_w_s.shape[1]
    H = D // HEAD_DIM
    F = f_w_in.shape[2]
    L = f_w_in.shape[0]
    dn = D // N_DEV
    xi, yi, ci = lax.axis_index("x"), lax.axis_index("y"), lax.axis_index("c")
    me = 4 * xi + 2 * yi + ci
    pos = jnp.stack([ci, 2 * xi + yi]).astype(jnp.int32)

    gathered = _all_gather(
        [a_w_in[0].astype(BF16), w_kv.astype(BF16), a_w_out[0].astype(BF16), b_w_q[0].astype(BF16),
         b_w_o[0].astype(BF16), f_w_in[0].astype(BF16), f_w_in[1].astype(BF16), f_w_down[0].astype(BF16),
         f_w_down[1].astype(BF16), jnp.concatenate([a_norm_g, a_v_norm_g], axis=0), f_conv_w[0], f_conv_w[1]],
        "gather_weights")
    wa_in, wkv, wa_out, wq, wo, wf_in0, wf_in1, wf_down0, wf_down1, norms_sh, conv_w0, conv_w1 = gathered
    wf_in, wf_down, conv_w = (wf_in0, wf_in1), (wf_down0, wf_down1), (conv_w0, conv_w1)
    wa_out, wq, wo = (t.reshape(D, D) for t in (wa_out, wq, wo))
    ga = jnp.transpose(norms_sh, (1, 0, 2)).reshape(2, D)
    g_a, g_av = ga[0:1], ga[1:2]

    x2 = x.reshape(T, D)
    tgt = loss_target.reshape(T, D)
    pc = jnp.arange(GMLP_BLOCK) // CHUNK
    mask = (pc[:, None] >= pc[None, :]).astype(F32)
    ws = (a_w_s[0] * mask[None]).astype(BF16)
    bst = jnp.transpose(a_b_s[0])
    rel = jnp.pad(b_rel_bias[0], ((0, 0), (0, REL_PAD - b_rel_bias.shape[2])))

    def ffn_weights(l):
        return (wf_in[l], conv_w[l], f_conv_b[l].reshape(8, 1, F), wf_down[l].reshape(4, F, D))

    bias = _bias_build(rel, "bias_build")
    z, n_a = _norm_matmul(x2, g_a, wa_in, flat=True, nbk=4, name="gmlp_in")
    gated, h1 = _gmlp_forward(z, ws, bst, g_av, wa_out, x2, name="gmlp_mix")
    w_in0, cw0, cb0, wd0 = ffn_weights(0)
    a0, n_f0 = _norm_matmul(h1, f_norm_g[0:1], w_in0, flat=False, nbk=2, name="ffn0_in")
    act0, c0, h2 = _ffn_forward(a0, cw0, cb0, wd0, h1, S, name="ffn0_out")
    kv, n_kv = _norm_matmul(h2, kv_norm_g.reshape(1, D), wkv, flat=True, nbk=4, name="kv_proj")
    q, n_q = _norm_matmul(h2, b_norm_g, wq.reshape(1, D, D), flat=True, nbk=1, name="q_proj", scale=HEAD_DIM ** -0.5)
    o = _attn_forward(q, kv, bias, S, name="attn")
    h3 = _matmul_residual(o, wo, h2, "attn_out")
    w_in1, cw1, cb1, wd1 = ffn_weights(1)
    a1, n_f1 = _norm_matmul(h3, f_norm_g[1:2], w_in1, flat=False, nbk=2, name="ffn1_in")
    act1, c1, h4 = _ffn_forward(a1, cw1, cb1, wd1, h3, S, name="ffn1_out")

    dh4, st_final = _loss_head(h4, final_norm_g.reshape(1, D), tgt, "loss_head")
    da1, st_conv1 = _ffn_backward(dh4, c1, a1, cw1, wd1, S, name="ffn1_bwd")
    g_wd1 = _wgrad_rows(act1, dh4, flat=False, tk=F, name="ffn1_dwdown")
    g_win1 = _wgrad_cols(n_f1, da1, flat=False, nb=8, nbk=2, name="ffn1_dwin")
    dh3, st_f1 = _matmul_nt(da1, w_in1, flat=False, nbk=4, name="ffn1_dx", norm=(h3, f_norm_g[1:2], dh4))
    d_o = _matmul_nt(dh3, wo.reshape(1, D, D), flat=True, nbk=1, name="attn_out_dx")
    g_wo = _wgrad_rows(o, dh3, flat=True, tk=_tile(D, 512), name="attn_out_dw")
    dq, dk, dv, dbias = _attn_backward(q, kv, bias, d_o, S, name="attn_bwd")
    g_rel = _bias_reduce(dbias, "bias_reduce")
    g_wq = _wgrad_cols(n_q, dq, flat=True, nb=1, nbk=1, name="q_dw")
    dh2, st_b = _matmul_nt(dq, wq.reshape(1, D, D), flat=True, nbk=1, name="q_dx", norm=(h2, b_norm_g, dh3))
    dkv = jnp.concatenate([dk, dv], axis=-1)
    g_wkv = _wgrad_cols(n_kv, dkv, flat=True, nb=8, nbk=4, name="kv_dw")
    dh2, st_kv = _matmul_nt(dkv, wkv, flat=True, nbk=4, name="kv_dx", norm=(h2, kv_norm_g.reshape(1, D), dh2))
    da0, st_conv0 = _ffn_backward(dh2, c0, a0, cw0, wd0, S, name="ffn0_bwd")
    g_wd0 = _wgrad_rows(act0, dh2, flat=False, tk=F, name="ffn0_dwdown")
    g_win0 = _wgrad_cols(n_f0, da0, flat=False, nb=8, nbk=2, name="ffn0_dwin")
    dh1, st_f0 = _matmul_nt(da0, w_in0, flat=False, nbk=4, name="ffn0_dx", norm=(h1, f_norm_g[0:1], dh2))
    dgated = _matmul_nt(dh1, wa_out.reshape(1, D, D), flat=True, nbk=1, name="gmlp_out_dx")
    g_wa_out = _wgrad_rows(gated, dh1, flat=True, tk=_tile(D, 512), name="gmlp_out_dw")
    dz, g_ws, g_bst, st_av = _gmlp_backward(z, dgated, ws, bst, g_av, mask, name="gmlp_bwd")
    g_wa_in = _wgrad_cols(n_a, dz, flat=True, nb=8, nbk=4, name="gmlp_in_dw")
    grad_x, st_a = _matmul_nt(dz, wa_in, flat=True, nbk=4, name="gmlp_in_dx", norm=(x2, g_a, dh1))

    big = [g_wa_in, g_wkv, g_wa_out.reshape(8, dn, D), g_wq.reshape(8, dn, D), g_wo.reshape(8, dn, D),
           g_win0, g_win1, g_wd0.reshape(8, F // 2, D), g_wd1.reshape(8, F // 2, D)]
    big4 = [t.reshape((4, 2) + t.shape[1:]) for t in big]
    landed = _sibling_exchange(big4, "grad_sibling_exchange")
    sums = [_sibling_sum(p, l, pos, "grad_sibling_sum_%d" % k) for k, (p, l) in enumerate(zip(big4, landed))]
    from_chips = _chip_exchange([s[1] for s in sums], "grad_chip_exchange")

    def big_update(k, w, m, v):
        shape = w.shape
        r = lambda t: t.reshape(-1, shape[-1])
        outs = _adamw_big(r(w), r(m), r(v), sums[k][0], from_chips[k], "adamw_big_%d" % k)
        return [t.reshape(shape) for t in outs]

    u_a_w_in = big_update(0, a_w_in, m_a_w_in, v_a_w_in)
    u_w_kv = big_update(1, w_kv, m_w_kv, v_w_kv)
    u_a_w_out = big_update(2, a_w_out, m_a_w_out, v_a_w_out)
    u_b_w_q = big_update(3, b_w_q, m_b_w_q, v_b_w_q)
    u_b_w_o = big_update(4, b_w_o, m_b_w_o, v_b_w_o)
    u_f_w_in = [jnp.stack(ts) for ts in zip(*[big_update(5 + l, f_w_in[l], m_f_w_in[l], v_f_w_in[l]) for l in range(L)])]
    u_f_w_down = [jnp.stack(ts) for ts in zip(*[big_update(7 + l, f_w_down[l], m_f_w_down[l], v_f_w_down[l])
                                               for l in range(L)])]

    vec = jnp.concatenate([st_a[0:1], st_av[0:1], st_kv[0:1], st_b[0:1], st_f0[0:1], st_f1[0:1], st_final[0:2]], axis=0)
    small = _all_gather([vec, g_ws, g_bst, g_rel, st_conv0, st_conv1], "gather_small_grads")
    vec, g_ws, g_bst, g_rel, st_conv0, st_conv1 = _sum_devices(small, "sum_small_grads")
    loss = vec[7, 0]
    g_a_norm = lax.dynamic_slice_in_dim(vec[0:1], me * dn, dn, axis=1)
    g_av_norm = lax.dynamic_slice_in_dim(vec[1:2], me * dn, dn, axis=1)
    st_conv = jnp.stack([st_conv0, st_conv1])
    g_conv_w = lax.dynamic_index_in_dim(st_conv, me, axis=1, keepdims=False)[:, 0:3]
    g_conv_b = st_conv[:, :, 3, :].reshape(L, 8 * F)
    small_items = [
        (a_norm_g, m_a_norm_g, v_a_norm_g, g_a_norm),
        (a_v_norm_g, m_a_v_norm_g, v_a_v_norm_g, g_av_norm),
        (a_w_s, m_a_w_s, v_a_w_s, g_ws[None]),
        (a_b_s, m_a_b_s, v_a_b_s, jnp.transpose(g_bst)[None]),
        (kv_norm_g.reshape(1, D), m_kv_norm_g.reshape(1, D), v_kv_norm_g.reshape(1, D), vec[2:3]),
        (b_norm_g, m_b_norm_g, v_b_norm_g, vec[3:4]),
        (b_rel_bias, m_b_rel_bias, v_b_rel_bias, g_rel[None, :, :b_rel_bias.shape[2]]),
        (f_norm_g, m_f_norm_g, v_f_norm_g, vec[4:6]),
        (f_conv_w, m_f_conv_w, v_f_conv_w, g_conv_w),
        (f_conv_b, m_f_conv_b, v_f_conv_b, g_conv_b),
        (final_norm_g.reshape(1, D), m_final_norm_g.reshape(1, D), v_final_norm_g.reshape(1, D), vec[6:7]),
    ]
    small_out = _adamw_small(small_items, "adamw_small")
    (u_a_norm, u_av_norm, u_ws, u_bs, u_kvn, u_bn, u_rel, u_fn, u_cw, u_cb, u_fin) = [
        (it[3],) + so for it, so in zip(small_items, small_out)]
    vecD = lambda u: tuple(t.reshape(D) for t in u)
    u_kvn, u_fin = vecD(u_kvn), vecD(u_fin)

    order = [u_a_norm, u_a_w_in, u_av_norm, u_ws, u_bs, u_a_w_out, u_kvn, u_w_kv, u_bn, u_b_w_q, u_rel, u_b_w_o,
             u_fn, u_f_w_in, u_cw, u_cb, u_f_w_down, u_fin]
    outs = [loss, grad_x.reshape(B, S, D)]
    for k in range(4):
        outs += [u[k] for u in order]
    return tuple(outs)
```

```python
import functools

import jax
import jax.numpy as jnp
from jax import lax
from jax.experimental import pallas as pl
from jax.experimental.pallas import tpu as pltpu

F32 = jnp.float32
BF16 = jnp.bfloat16
MESH = pl.DeviceIdType.MESH

N_DEV = 8
EPS = 1e-6
NEG_INF = -1e30
CHUNK = 64
LEFT_CHUNKS = 8
REL_CLIP = 128
HEAD_DIM = 64
GMLP_BLOCK = 128
Q_TILE = 2 * CHUNK
PAD = LEFT_CHUNKS * CHUNK
WIN = PAD + Q_TILE
SKEW = WIN + Q_TILE
REL_PAD = 384
FWD_HEADS_PER_STEP = 8
BWD_HEADS_PER_STEP = 4
STRIP = 32
ADAM_LR, ADAM_B1, ADAM_B2, ADAM_EPS, ADAM_WD, ADAM_STEP = 0.001, 0.9, 0.999, 1e-08, 0.01, 10
VMEM_LIMIT = 56 * 1024 * 1024
TOKEN_TILE = 512


def _params(sem=None):
    return pltpu.CompilerParams(dimension_semantics=sem, vmem_limit_bytes=VMEM_LIMIT)


def _tile(n, pref):
    t = min(n, pref)
    while n % t:
        t //= 2
    return t


def _gelu(x):
    return 0.5 * x * (1.0 + jnp.tanh(0.7978845608028654 * (x + 0.044715 * x * x * x)))


def _gelu_grad(x):
    t = jnp.tanh(0.7978845608028654 * (x + 0.044715 * x * x * x))
    return 0.5 * (1.0 + t) + 0.5 * x * (1.0 - t * t) * 0.7978845608028654 * (1.0 + 3 * 0.044715 * x * x)


def _sigmoid(x):
    return 1.0 / (1.0 + jnp.exp(-x))


def _dot(a, b):
    return jnp.dot(a, b, preferred_element_type=F32)


def _dot_nt(a, b):
    return lax.dot_general(a, b, (((1,), (1,)), ((), ())), preferred_element_type=F32)


def _dot_tn(a, b):
    return lax.dot_general(a, b, (((0,), (0,)), ((), ())), preferred_element_type=F32)


def _split3(x):
    hi = x.astype(BF16)
    r1 = x - hi.astype(F32)
    mid = r1.astype(BF16)
    lo = (r1 - mid.astype(F32)).astype(BF16)
    return hi, mid, lo


def _mesh_pos():
    return lax.axis_index("x"), lax.axis_index("y"), lax.axis_index("c")


class _Rider:
    def __init__(self, arrs, out_shapes, sems, start, finish):
        self.arrs, self.out_shapes, self.sems, self.start, self.finish = arrs, out_shapes, sems, start, finish


def _gather_rider(arrs):
    n = len(arrs)

    def tools(ins, outs, sems):
        send_sems, recv_sems, local_sems = sems
        x, y, c = _mesh_pos()
        me, sibling = (x, y, c), (x, y, 1 - c)
        chips = [(1 - x, y), (x, 1 - y), (1 - x, 1 - y)]

        def slot(a, block):
            px, py, pc = block
            return outs[a].at[4 * px + 2 * py + pc]

        def copy(a, k, block, to, src=None):
            dst = slot(a, block)
            return pltpu.make_async_remote_copy(
                src_ref=dst if src is None else src, dst_ref=dst,
                send_sem=send_sems.at[a, k], recv_sem=recv_sems.at[a, k], device_id=to, device_id_type=MESH)

        def first(a):
            cps = [copy(a, 0, me, sibling, src=ins[a])]
            return cps + [copy(a, 1 + j, me, (*chip, c), src=ins[a]) for j, chip in enumerate(chips)]

        def mine(a):
            return pltpu.make_async_copy(ins[a], slot(a, me), local_sems.at[a])

        return me, sibling, chips, c, copy, first, mine

    def start(ins, outs, sems):
        _, _, _, _, _, first, mine = tools(ins, outs, sems)
        for a in range(n):
            mine(a).start()
            for cp in first(a):
                cp.start()

    def finish(ins, outs, sems):
        me, sibling, chips, c, copy, first, mine = tools(ins, outs, sems)
        passed = []
        for j, chip in enumerate(chips):
            for a in range(n):
                copy(a, 1 + j, (*chip, c), me).wait_recv()
                fwd = copy(a, 4 + j, (*chip, c), sibling)
                fwd.start()
                passed.append(fwd)
        for a in range(n):
            copy(a, 0, sibling, me).wait_recv()
            for j, chip in enumerate(chips):
                copy(a, 4 + j, (*chip, 1 - c), me).wait_recv()
        for a in range(n):
            for cp in first(a):
                cp.wait_send()
        for cp in passed:
            cp.wait_send()
        for a in range(n):
            mine(a).wait()

    return _Rider(list(arrs), [jax.ShapeDtypeStruct((N_DEV,) + a.shape, a.dtype) for a in arrs],
                  [pltpu.SemaphoreType.DMA((n, 7)), pltpu.SemaphoreType.DMA((n, 7)), pltpu.SemaphoreType.DMA((n,))],
                  start, finish)


def _sibling_rider(arrs):
    n = len(arrs)

    def copies(ins, outs, sems):
        x, y, c = _mesh_pos()
        return [pltpu.make_async_remote_copy(
            src_ref=ins[a].at[:, pl.ds(1 - c, 1)], dst_ref=outs[a],
            send_sem=sems[0].at[a], recv_sem=sems[1].at[a], device_id=(x, y, 1 - c), device_id_type=MESH)
            for a in range(n)]

    def start(ins, outs, sems):
        for cp in copies(ins, outs, sems):
            cp.start()

    def finish(ins, outs, sems):
        for cp in copies(ins, outs, sems):
            cp.wait()

    return _Rider(list(arrs), [jax.ShapeDtypeStruct((4, 1) + a.shape[2:], a.dtype) for a in arrs],
                  [pltpu.SemaphoreType.DMA((n,)), pltpu.SemaphoreType.DMA((n,))], start, finish)


def _chip_rider(arrs):
    n = len(arrs)

    def copies(ins, outs, sems):
        x, y, c = _mesh_pos()
        cps = []
        for a in range(n):
            for k in (1, 2, 3):
                px = x if k < 2 else 1 - x
                py = y if k == 2 else 1 - y
                cps.append(pltpu.make_async_remote_copy(
                    src_ref=ins[a].at[2 * px + py], dst_ref=outs[a].at[k - 1],
                    send_sem=sems[0].at[a, k - 1], recv_sem=sems[1].at[a, k - 1],
                    device_id=(px, py, c), device_id_type=MESH))
        return cps

    def start(ins, outs, sems):
        for cp in copies(ins, outs, sems):
            cp.start()

    def finish(ins, outs, sems):
        for cp in copies(ins, outs, sems):
            cp.wait()

    return _Rider(list(arrs), [jax.ShapeDtypeStruct((3,) + a.shape[1:], a.dtype) for a in arrs],
                  [pltpu.SemaphoreType.DMA((n, 3)), pltpu.SemaphoreType.DMA((n, 3))], start, finish)


def _join_riders(riders):
    def split(seq, counts):
        out, at = [], 0
        for k in counts:
            out.append(seq[at:at + k])
            at += k
        return out

    n_in = [len(r.arrs) for r in riders]
    n_out = [len(r.out_shapes) for r in riders]
    n_sem = [len(r.sems) for r in riders]

    def run(which):
        def fn(ins, outs, sems):
            for r, i, o, s in zip(riders, split(ins, n_in), split(outs, n_out), split(sems, n_sem)):
                getattr(r, which)(i, o, s)
        return fn

    return _Rider([a for r in riders for a in r.arrs], [o for r in riders for o in r.out_shapes],
                  [s for r in riders for s in r.sems], run("start"), run("finish"))


def _run_rider(rider, name):
    n_in, n_out = len(rider.arrs), len(rider.out_shapes)

    def body(*refs):
        ins, outs, sems = refs[:n_in], refs[n_in:n_in + n_out], refs[n_in + n_out:]
        rider.start(ins, outs, sems)
        rider.finish(ins, outs, sems)

    any_spec = pl.BlockSpec(memory_space=pl.ANY)
    return pl.pallas_call(
        body, name=name, out_shape=list(rider.out_shapes), in_specs=[any_spec] * n_in, out_specs=[any_spec] * n_out,
        scratch_shapes=list(rider.sems),
    )(*rider.arrs)


def _call(body, *, name, grid, in_specs, out_specs, out_shape, args, scratch_shapes=(), rider=None):
    params = _params(("arbitrary",) * len(grid))
    if rider is None:
        return pl.pallas_call(body, name=name, grid=grid, in_specs=in_specs, out_specs=out_specs, out_shape=out_shape,
                              scratch_shapes=list(scratch_shapes), compiler_params=params)(*args)
    single = not isinstance(out_shape, (list, tuple))
    outs = [out_shape] if single else list(out_shape)
    ospecs = [out_specs] if single else list(out_specs)
    n_in, n_out, n_scr = len(in_specs), len(outs), len(scratch_shapes)
    r_in, r_out = len(rider.arrs), len(rider.out_shapes)

    def hosted(*refs):
        refs = list(refs)
        ins, rins = refs[:n_in], refs[n_in:n_in + r_in]
        refs = refs[n_in + r_in:]
        houts, routs = refs[:n_out], refs[n_out:n_out + r_out]
        refs = refs[n_out + r_out:]
        scr, rsems = refs[:n_scr], refs[n_scr:]
        ids = [pl.program_id(a) for a in range(len(grid))]
        first = functools.reduce(lambda p, q: p & q, [i == 0 for i in ids])
        last = functools.reduce(lambda p, q: p & q, [i == g - 1 for i, g in zip(ids, grid)])

        @pl.when(first)
        def _():
            rider.start(rins, routs, rsems)

        body(*ins, *houts, *scr)

        @pl.when(last)
        def _():
            rider.finish(rins, routs, rsems)

    any_spec = pl.BlockSpec(memory_space=pl.ANY)
    res = pl.pallas_call(
        hosted, name=name, grid=grid, in_specs=list(in_specs) + [any_spec] * r_in,
        out_specs=ospecs + [any_spec] * r_out, out_shape=outs + list(rider.out_shapes),
        scratch_shapes=list(scratch_shapes) + list(rider.sems), compiler_params=params,
    )(*args, *rider.arrs)
    return (res[0] if single else list(res[:n_out])), list(res[n_out:])


def _sibling_sum(part, landed, pos, name):
    _, _, rows, cols = part.shape
    tr = _tile(rows, 256)

    def body(pos_ref, p_ref, l_ref, own_ref, all_ref):
        s = p_ref[0, 0] + l_ref[0, 0]
        all_ref[0] = s.astype(BF16)

        @pl.when(pl.program_id(1) == pos_ref[1])
        def _():
            own_ref[...] = s

    return pl.pallas_call(
        body, name=name,
        grid_spec=pltpu.PrefetchScalarGridSpec(
            num_scalar_prefetch=1, grid=(rows // tr, 4),
            in_specs=[pl.BlockSpec((1, 1, tr, cols), lambda i, k, pos: (k, pos[0], i, 0)),
                      pl.BlockSpec((1, 1, tr, cols), lambda i, k, pos: (k, 0, i, 0))],
            out_specs=[pl.BlockSpec((tr, cols), lambda i, k, pos: (i, 0)),
                       pl.BlockSpec((1, tr, cols), lambda i, k, pos: (k, i, 0))]),
        out_shape=[jax.ShapeDtypeStruct((rows, cols), F32), jax.ShapeDtypeStruct((4, rows, cols), BF16)],
        compiler_params=_params(("arbitrary", "arbitrary")),
    )(pos, part, landed)


def _rms(x, g):
    r = lax.rsqrt(jnp.mean(x * x, axis=-1, keepdims=True) + EPS)
    return x * r, r


def _norm_matmul(h, g, w, *, flat, nbk, name, scale=1.0, rider=None):
    T, D = h.shape
    nb, _, bn = w.shape
    tm = _tile(T, TOKEN_TILE)

    def body(h_ref, g_ref, w_ref, o_ref, n_ref):
        @pl.when(pl.program_id(1) == 0)
        def _():
            xh, _ = _rms(h_ref[...], None)
            n_ref[...] = (xh * g_ref[...]).astype(BF16)

        n = n_ref[...]
        for k in range(nbk):
            r = _dot(n, w_ref[k])
            r = (r if scale == 1.0 else r * scale).astype(BF16)
            if flat:
                o_ref[:, k * bn:(k + 1) * bn] = r
            else:
                o_ref[k] = r

    if flat:
        out_shape = jax.ShapeDtypeStruct((T, nb * bn), BF16)
        out_spec = pl.BlockSpec((tm, nbk * bn), lambda i, j: (i, j))
    else:
        out_shape = jax.ShapeDtypeStruct((nb, T, bn), BF16)
        out_spec = pl.BlockSpec((nbk, tm, bn), lambda i, j: (j, i, 0))
    return _call(
        body, name=name, grid=(T // tm, nb // nbk),
        in_specs=[pl.BlockSpec((tm, D), lambda i, j: (i, 0)),
                  pl.BlockSpec((1, D), lambda i, j: (0, 0)),
                  pl.BlockSpec((nbk, D, bn), lambda i, j: (j, 0, 0))],
        out_specs=[out_spec, pl.BlockSpec((tm, D), lambda i, j: (i, 0))],
        out_shape=[out_shape, jax.ShapeDtypeStruct((T, D), BF16)],
        args=(h, g, w), rider=rider)


def _matmul_nt(dy, w, *, flat, nbk, name, norm=None, out_dtype=BF16, rider=None):
    nb, R, bn = w.shape
    T = dy.shape[0] if flat else dy.shape[1]
    tm = _tile(T, TOKEN_TILE)
    nj = nb // nbk

    def body(*refs):
        if norm is None:
            dy_ref, w_ref, o_ref, acc_ref = refs
        else:
            dy_ref, w_ref, h_ref, g_ref, dres_ref, o_ref, dg_ref, acc_ref = refs
        i, j = pl.program_id(0), pl.program_id(1)

        @pl.when(j == 0)
        def _():
            acc_ref[...] = jnp.zeros_like(acc_ref)

        acc = acc_ref[...]
        for k in range(nbk):
            d = dy_ref[:, k * bn:(k + 1) * bn] if flat else dy_ref[k]
            acc = acc + _dot_nt(d.astype(BF16), w_ref[k])
        acc_ref[...] = acc

        @pl.when(j == nj - 1)
        def _():
            if norm is None:
                o_ref[...] = acc.astype(out_dtype)
            else:
                xh, r = _rms(h_ref[...], None)

                @pl.when(i == 0)
                def _():
                    dg_ref[...] = jnp.zeros_like(dg_ref)

                dg_ref[0:1, :] += jnp.sum(acc * xh, axis=0, keepdims=True)
                dn = acc * g_ref[...]
                o_ref[...] = dres_ref[...] + r * (dn - xh * jnp.mean(dn * xh, axis=-1, keepdims=True))

    if flat:
        dy_spec = pl.BlockSpec((tm, nbk * bn), lambda i, j: (i, j))
    else:
        dy_spec = pl.BlockSpec((nbk, tm, bn), lambda i, j: (j, i, 0))
    w_spec = pl.BlockSpec((nbk, R, bn), lambda i, j: (j, 0, 0))
    row_spec = pl.BlockSpec((tm, R), lambda i, j: (i, 0))
    if norm is None:
        in_specs, args = [dy_spec, w_spec], (dy, w)
        out_specs = row_spec
        out_shape = jax.ShapeDtypeStruct((T, R), out_dtype)
    else:
        in_specs = [dy_spec, w_spec, row_spec, pl.BlockSpec((1, R), lambda i, j: (0, 0)), row_spec]
        args = (dy, w) + tuple(norm)
        out_specs = [row_spec, pl.BlockSpec((8, R), lambda i, j: (0, 0))]
        out_shape = [jax.ShapeDtypeStruct((T, R), F32), jax.ShapeDtypeStruct((8, R), F32)]
    return _call(
        body, name=name, grid=(T // tm, nj), in_specs=in_specs, out_specs=out_specs, out_shape=out_shape,
        scratch_shapes=[pltpu.VMEM((tm, R), F32)], args=args, rider=rider)


def _wgrad_cols(n, dy, *, flat, nb, nbk, name, rider=None):
    T, D = n.shape
    bn = dy.shape[1] // nb if flat else dy.shape[2]
    tt = _tile(T, TOKEN_TILE)
    nt = T // tt

    def body(n_ref, dy_ref, o_ref, acc_ref):
        t = pl.program_id(1)

        @pl.when(t == 0)
        def _():
            acc_ref[...] = jnp.zeros_like(acc_ref)

        nv = n_ref[...]
        for k in range(nbk):
            d = dy_ref[:, k * bn:(k + 1) * bn] if flat else dy_ref[k]
            acc_ref[k] += _dot_tn(nv, d)

        @pl.when(t == nt - 1)
        def _():
            o_ref[...] = acc_ref[...]

    if flat:
        dy_spec = pl.BlockSpec((tt, nbk * bn), lambda j, t: (t, j))
    else:
        dy_spec = pl.BlockSpec((nbk, tt, bn), lambda j, t: (j, t, 0))
    return _call(
        body, name=name, grid=(nb // nbk, nt),
        in_specs=[pl.BlockSpec((tt, D), lambda j, t: (t, 0)), dy_spec],
        out_specs=pl.BlockSpec((nbk, D, bn), lambda j, t: (j, 0, 0)),
        out_shape=jax.ShapeDtypeStruct((nb, D, bn), F32),
        scratch_shapes=[pltpu.VMEM((nbk, D, bn), F32)], args=(n, dy), rider=rider)


def _wgrad_rows(xa, dh, *, flat, tk, name, rider=None):
    T, D = dh.shape
    nk = xa.shape[1] // tk if flat else xa.shape[0]
    tt = _tile(T, TOKEN_TILE)
    nt = T // tt

    def body(x_ref, dh_ref, o_ref, acc_ref):
        t = pl.program_id(1)

        @pl.when(t == 0)
        def _():
            acc_ref[...] = jnp.zeros_like(acc_ref)

        xv = x_ref[...] if flat else x_ref[0]
        acc_ref[...] += _dot_tn(xv, dh_ref[...].astype(BF16))

        @pl.when(t == nt - 1)
        def _():
            o_ref[...] = acc_ref[...]

    x_spec = pl.BlockSpec((tt, tk), lambda j, t: (t, j)) if flat else pl.BlockSpec((1, tt, tk), lambda j, t: (j, t, 0))
    return _call(
        body, name=name, grid=(nk, nt),
        in_specs=[x_spec, pl.BlockSpec((tt, D), lambda j, t: (t, 0))],
        out_specs=pl.BlockSpec((tk, D), lambda j, t: (j, 0)),
        out_shape=jax.ShapeDtypeStruct((nk * tk, D), F32),
        scratch_shapes=[pltpu.VMEM((tk, D), F32)], args=(xa, dh), rider=rider)


def _matmul_residual(xa, w, res, name):
    T, K = xa.shape
    D = w.shape[1]
    tm = _tile(T, TOKEN_TILE)

    def body(x_ref, w_ref, r_ref, o_ref):
        o_ref[...] = r_ref[...] + _dot(x_ref[...], w_ref[...])

    return pl.pallas_call(
        body, name=name, grid=(T // tm,),
        in_specs=[pl.BlockSpec((tm, K), lambda i: (i, 0)), pl.BlockSpec((K, D), lambda i: (0, 0)),
                  pl.BlockSpec((tm, D), lambda i: (i, 0))],
        out_specs=pl.BlockSpec((tm, D), lambda i: (i, 0)),
        out_shape=jax.ShapeDtypeStruct((T, D), F32),
        compiler_params=_params(("arbitrary",)),
    )(xa, w, res)


def _gmlp_gate(z, ws, bst, gv, G, gd):
    D = G * gd
    u = _gelu(z[:, :D].astype(F32))
    v = _gelu(z[:, D:].astype(F32))
    vh, r = _rms(v, None)
    vn = (vh * gv).astype(BF16)
    return u, v, vh, r, vn


def _gmlp_forward(z, ws, bst, gv, w_out, x, *, name, rider=None):
    T, D2 = z.shape
    D = D2 // 2
    G = ws.shape[0]
    gd = D // G
    tb = _tile(T, 256)
    nblk = tb // GMLP_BLOCK

    def body(z_ref, ws_ref, b_ref, gv_ref, wo_ref, x_ref, gated_ref, h_ref):
        u, _, _, _, vn = _gmlp_gate(z_ref[...], None, None, gv_ref[...], G, gd)
        for n in range(nblk):
            rows = slice(n * GMLP_BLOCK, (n + 1) * GMLP_BLOCK)
            for gi in range(G):
                cols = slice(gi * gd, (gi + 1) * gd)
                s = _dot(ws_ref[gi], vn[rows, cols]) + b_ref[:, gi:gi + 1]
                gated_ref[rows, cols] = (u[rows, cols] * s).astype(BF16)
        h_ref[...] = x_ref[...] + _dot(gated_ref[...], wo_ref[...])

    return _call(
        body, name=name, grid=(T // tb,),
        in_specs=[pl.BlockSpec((tb, D2), lambda i: (i, 0)), pl.BlockSpec(ws.shape, lambda i: (0, 0, 0)),
                  pl.BlockSpec(bst.shape, lambda i: (0, 0)), pl.BlockSpec((1, D), lambda i: (0, 0)),
                  pl.BlockSpec((D, D), lambda i: (0, 0)), pl.BlockSpec((tb, D), lambda i: (i, 0))],
        out_specs=[pl.BlockSpec((tb, D), lambda i: (i, 0)), pl.BlockSpec((tb, D), lambda i: (i, 0))],
        out_shape=[jax.ShapeDtypeStruct((T, D), BF16), jax.ShapeDtypeStruct((T, D), F32)],
        args=(z, ws, bst, gv, w_out, x), rider=rider)


def _gmlp_backward(z, dgated, ws, bst, gv, mask, *, name, rider=None):
    T, D2 = z.shape
    D = D2 // 2
    G = ws.shape[0]
    gd = D // G
    tb = _tile(T, 256)
    nblk = tb // GMLP_BLOCK

    def body(z_ref, dg_ref, ws_ref, b_ref, gv_ref, mask_ref, dz_ref, dws_ref, db_ref, dgv_ref, dvn_ref):
        @pl.when(pl.program_id(0) == 0)
        def _():
            dws_ref[...] = jnp.zeros_like(dws_ref)
            db_ref[...] = jnp.zeros_like(db_ref)
            dgv_ref[...] = jnp.zeros_like(dgv_ref)

        zf = z_ref[...]
        u, v, vh, r, vn = _gmlp_gate(zf, None, None, gv_ref[...], G, gd)
        dg = dg_ref[...].astype(F32)
        for n in range(nblk):
            rows = slice(n * GMLP_BLOCK, (n + 1) * GMLP_BLOCK)
            for gi in range(G):
                cols = slice(gi * gd, (gi + 1) * gd)
                vblk = vn[rows, cols]
                s = _dot(ws_ref[gi], vblk) + b_ref[:, gi:gi + 1]
                dgb = dg[rows, cols]
                ds = dgb * u[rows, cols]
                dsb = ds.astype(BF16)
                dz_ref[rows, cols] = (dgb * s * _gelu_grad(zf[rows, cols].astype(F32))).astype(BF16)
                dvn_ref[rows, cols] = _dot_tn(ws_ref[gi], dsb)
                dws_ref[gi] += _dot_nt(dsb, vblk) * mask_ref[...]
                db_ref[:, gi:gi + 1] += jnp.sum(ds, axis=1, keepdims=True)
        dvn = dvn_ref[...]
        dgv_ref[0:1, :] += jnp.sum(dvn * vh, axis=0, keepdims=True)
        dn = dvn * gv_ref[...]
        dv = r * (dn - vh * jnp.mean(dn * vh, axis=-1, keepdims=True))
        dz_ref[:, D:] = (dv * _gelu_grad(zf[:, D:].astype(F32))).astype(BF16)

    return _call(
        body, name=name, grid=(T // tb,),
        in_specs=[pl.BlockSpec((tb, D2), lambda i: (i, 0)), pl.BlockSpec((tb, D), lambda i: (i, 0)),
                  pl.BlockSpec(ws.shape, lambda i: (0, 0, 0)), pl.BlockSpec(bst.shape, lambda i: (0, 0)),
                  pl.BlockSpec((1, D), lambda i: (0, 0)), pl.BlockSpec(mask.shape, lambda i: (0, 0))],
        out_specs=[pl.BlockSpec((tb, D2), lambda i: (i, 0)), pl.BlockSpec(ws.shape, lambda i: (0, 0, 0)),
                   pl.BlockSpec(bst.shape, lambda i: (0, 0)), pl.BlockSpec((8, D), lambda i: (0, 0))],
        out_shape=[jax.ShapeDtypeStruct((T, D2), BF16), jax.ShapeDtypeStruct(ws.shape, F32),
                   jax.ShapeDtypeStruct(bst.shape, F32), jax.ShapeDtypeStruct((8, D), F32)],
        scratch_shapes=[pltpu.VMEM((tb, D), F32)], args=(z, dgated, ws, bst, gv, mask), rider=rider)


def _shift_rows(x, k):
    return pltpu.roll(x, k % x.shape[0], axis=0)


def _conv3(ext, cw):
    return (cw[0:1] * _shift_rows(ext, 2)[8:] + cw[1:2] * _shift_rows(ext, 1)[8:] + cw[2:3] * ext[8:])


def _ffn_forward(a, cw, cb, wd, h, seq, *, name, rider=None):
    _, T, F = a.shape
    D = h.shape[1]
    tm = _tile(seq, TOKEN_TILE)
    hb = tm // 16

    def body(a_ref, ap_ref, cw_ref, cb_ref, wd_ref, h_ref, act_ref, c_ref, o_ref, acc_ref):
        i, j = pl.program_id(0), pl.program_id(1)
        keep = ((i * tm) % seq != 0).astype(F32)

        def conv(b):
            ext = jnp.concatenate([ap_ref[b, 8:16].astype(F32) * keep, a_ref[b].astype(F32)], axis=0)
            return _conv3(ext, cw_ref[b]) + cb_ref[b]

        up, gate = conv(j), conv(j + 4)
        c_ref[j] = up.astype(BF16)
        c_ref[j + 4] = gate.astype(BF16)
        act = (gate * _sigmoid(gate) * up).astype(BF16)
        act_ref[0] = act

        @pl.when(j == 0)
        def _():
            acc_ref[...] = h_ref[...]

        acc_ref[...] += _dot(act, wd_ref[0])

        @pl.when(j == 3)
        def _():
            o_ref[...] = acc_ref[...]

    return _call(
        body, name=name, grid=(T // tm, 4),
        in_specs=[pl.BlockSpec((8, tm, F), lambda i, j: (0, i, 0)),
                  pl.BlockSpec((8, 16, F), lambda i, j: (0, jnp.maximum(i * hb - 1, 0), 0)),
                  pl.BlockSpec((8, 3, F), lambda i, j: (0, 0, 0)), pl.BlockSpec((8, 1, F), lambda i, j: (0, 0, 0)),
                  pl.BlockSpec((1, F, D), lambda i, j: (j, 0, 0)), pl.BlockSpec((tm, D), lambda i, j: (i, 0))],
        out_specs=[pl.BlockSpec((1, tm, F), lambda i, j: (j, i, 0)), pl.BlockSpec((8, tm, F), lambda i, j: (0, i, 0)),
                   pl.BlockSpec((tm, D), lambda i, j: (i, 0))],
        out_shape=[jax.ShapeDtypeStruct((4, T, F), BF16), jax.ShapeDtypeStruct((8, T, F), BF16),
                   jax.ShapeDtypeStruct((T, D), F32)],
        scratch_shapes=[pltpu.VMEM((tm, D), F32)], args=(a, a, cw, cb, wd, h), rider=rider)


def _ffn_backward(dh, c, a, cw, wd, seq, *, name, rider=None):
    _, T, F = a.shape
    D = dh.shape[1]
    tm = _tile(seq, TOKEN_TILE)
    hb = tm // 16
    nt = T // tm

    def body(dh_ref, dhn_ref, cu_ref, cg_ref, cun_ref, cgn_ref, au_ref, ag_ref, cw_ref, wd_ref, da_ref, st_ref):
        i, j = pl.program_id(0), pl.program_id(1)
        keep_next = (((i + 1) * tm) % seq != 0).astype(F32)

        @pl.when((i == 0) & (j == 0))
        def _():
            st_ref[...] = jnp.zeros_like(st_ref)

        dhe = jnp.concatenate([dh_ref[...], dhn_ref[...] * keep_next], axis=0).astype(BF16)
        dact = _dot_nt(dhe, wd_ref[0])
        up = jnp.concatenate([cu_ref[0].astype(F32), cun_ref[0, 0:8].astype(F32)], axis=0)
        gate = jnp.concatenate([cg_ref[0].astype(F32), cgn_ref[0, 0:8].astype(F32)], axis=0)
        sg = _sigmoid(gate)
        gs = gate * sg
        d_up = dact * gs
        d_gate = dact * up * (sg + gs * (1.0 - sg))

        def finish(b, a_ref, dc):
            w = cw_ref[b]
            dm, u1, u2 = dc[:tm], _shift_rows(dc, -1)[:tm], _shift_rows(dc, -2)[:tm]
            da_ref[b] = (w[2:3] * dm + w[1:2] * u1 + w[0:1] * u2).astype(BF16)
            av = a_ref[0].astype(F32)
            st_ref[b, 0:1, :] += jnp.sum(u2 * av, axis=0, keepdims=True)
            st_ref[b, 1:2, :] += jnp.sum(u1 * av, axis=0, keepdims=True)
            st_ref[b, 2:3, :] += jnp.sum(dm * av, axis=0, keepdims=True)
            st_ref[b, 3:4, :] += jnp.sum(dm, axis=0, keepdims=True)

        finish(j, au_ref, d_up)
        finish(j + 4, ag_ref, d_gate)

    nxt = lambda i: jnp.minimum((i + 1) * hb, T // 16 - 1)
    return _call(
        body, name=name, grid=(nt, 4),
        in_specs=[pl.BlockSpec((tm, D), lambda i, j: (i, 0)),
                  pl.BlockSpec((8, D), lambda i, j: (jnp.minimum((i + 1) * (tm // 8), T // 8 - 1), 0)),
                  pl.BlockSpec((1, tm, F), lambda i, j: (j, i, 0)), pl.BlockSpec((1, tm, F), lambda i, j: (j + 4, i, 0)),
                  pl.BlockSpec((1, 16, F), lambda i, j: (j, nxt(i), 0)),
                  pl.BlockSpec((1, 16, F), lambda i, j: (j + 4, nxt(i), 0)),
                  pl.BlockSpec((1, tm, F), lambda i, j: (j, i, 0)), pl.BlockSpec((1, tm, F), lambda i, j: (j + 4, i, 0)),
                  pl.BlockSpec((8, 3, F), lambda i, j: (0, 0, 0)),
                  pl.BlockSpec((1, F, D), lambda i, j: (j, 0, 0))],
        out_specs=[pl.BlockSpec((8, tm, F), lambda i, j: (0, i, 0)), pl.BlockSpec((8, 8, F), lambda i, j: (0, 0, 0))],
        out_shape=[jax.ShapeDtypeStruct((8, T, F), BF16), jax.ShapeDtypeStruct((8, 8, F), F32)],
        args=(dh, dh, c, c, c, c, a, a, cw, wd), rider=rider)


def _rel_onehot():
    r = lax.broadcasted_iota(jnp.int32, (REL_PAD, SKEW), 0)
    n = lax.broadcasted_iota(jnp.int32, (REL_PAD, SKEW), 1)
    off = jnp.where(n >= WIN, n - SKEW, n)
    idx = jnp.minimum(PAD - off, REL_CLIP) + REL_CLIP
    return (r == idx).astype(BF16)


def _skew(x, sign):
    row = lax.broadcasted_iota(jnp.int32, x.shape, 0)
    for b in range(7):
        x = jnp.where((row >> b) & 1 == 1, pltpu.roll(x, (sign * (1 << b)) % SKEW, axis=1), x)
    return x


def _bias_build(rel, name):
    H = rel.shape[0]

    def body(rel_ref, o_ref):
        oh = _rel_onehot()
        hi, mid, lo = _split3(rel_ref[...])
        base = _dot(hi, oh) + _dot(mid, oh) + _dot(lo, oh)
        q = lax.broadcasted_iota(jnp.int32, (Q_TILE, WIN), 0)
        k = lax.broadcasted_iota(jnp.int32, (Q_TILE, WIN), 1)
        ok = ((q < CHUNK) & (k < WIN - CHUNK)) | ((q >= CHUNK) & (k >= CHUNK))
        for hd in range(H):
            t = _skew(jnp.broadcast_to(base[hd:hd + 1, :], (Q_TILE, SKEW)), 1)
            o_ref[hd] = jnp.where(ok, t[:, :WIN], NEG_INF)

    return pl.pallas_call(
        body, name=name, out_shape=jax.ShapeDtypeStruct((H, Q_TILE, WIN), F32),
        in_specs=[pl.BlockSpec(memory_space=pltpu.VMEM)], out_specs=pl.BlockSpec(memory_space=pltpu.VMEM),
        compiler_params=_params(),
    )(rel)


def _bias_reduce(dbias, name):
    H = dbias.shape[0]

    def body(d_ref, o_ref, e_ref):
        oh = _rel_onehot()
        for hd in range(H):
            x = jnp.concatenate([d_ref[hd], jnp.zeros((Q_TILE, SKEW - WIN), F32)], axis=1)
            e_ref[hd:hd + 1, :] = jnp.sum(_skew(x, -1), axis=0, keepdims=True)
        hi, mid, lo = _split3(e_ref[...])
        o_ref[...] = _dot_nt(hi, oh) + _dot_nt(mid, oh) + _dot_nt(lo, oh)

    return pl.pallas_call(
        body, name=name, out_shape=jax.ShapeDtypeStruct((H, REL_PAD), F32),
        in_specs=[pl.BlockSpec(memory_space=pltpu.VMEM)], out_specs=pl.BlockSpec(memory_space=pltpu.VMEM),
        scratch_shapes=[pltpu.VMEM((H, SKEW), F32)],
        compiler_params=_params(),
    )(dbias)


def _pair_stack(xp, even):
    z = jnp.zeros_like(xp)
    return jnp.concatenate([jnp.where(even, xp, z), jnp.where(even, z, xp)], axis=0)


def _pair_merge(y, even):
    return jnp.where(even, y[:Q_TILE], y[Q_TILE:])


def _strip_probs(s_ref, b_ref, pp, r, valid):
    hb, hr = divmod(r, Q_TILE)
    s = s_ref[pp, r:r + STRIP, :] + b_ref[2 * pp + hb, hr:hr + STRIP, :]
    s = jnp.where(valid, s, NEG_INF)
    e = jnp.exp(s - jnp.max(s, axis=-1, keepdims=True))
    return e * (1.0 / jnp.sum(e, axis=-1, keepdims=True))


def _fill_padded(dst_ref, src_ref):
    dst_ref[0:PAD, :] = jnp.zeros((PAD, dst_ref.shape[1]), dst_ref.dtype)
    dst_ref[PAD:, :] = src_ref[...]


def _attn_specs(B, S, D, lanes):
    nt = S // Q_TILE
    q_spec = pl.BlockSpec((Q_TILE, lanes), lambda g, b, i: (b * nt + i, g))
    k_spec = pl.BlockSpec((S, lanes), lambda g, b, i: (b, g))
    v_spec = pl.BlockSpec((S, lanes), lambda g, b, i: (b, D // lanes + g))
    bias_spec = pl.BlockSpec((lanes // HEAD_DIM, Q_TILE, WIN), lambda g, b, i: (g, 0, 0))
    return nt, q_spec, k_spec, v_spec, bias_spec


def _attn_forward(q, kv, bias, S, *, name):
    T, D = q.shape
    B = T // S
    lanes = min(FWD_HEADS_PER_STEP * HEAD_DIM, D)
    nt, q_spec, k_spec, v_spec, bias_spec = _attn_specs(B, S, D, lanes)

    npairs = lanes // (2 * HEAD_DIM)

    def body(q_ref, k_ref, v_ref, b_ref, o_ref, kp_ref, vp_ref, s_ref, p_ref):
        i = pl.program_id(2)

        @pl.when(i == 0)
        def _():
            _fill_padded(kp_ref, k_ref)
            _fill_padded(vp_ref, v_ref)

        start = pl.multiple_of(i * Q_TILE, Q_TILE)
        even = lax.broadcasted_iota(jnp.int32, (1, 2 * HEAD_DIM), 1) < HEAD_DIM
        valid = lax.broadcasted_iota(jnp.int32, (STRIP, WIN), 1) >= PAD - i * Q_TILE
        pair_cols = [slice(pp * 2 * HEAD_DIM, (pp + 1) * 2 * HEAD_DIM) for pp in range(npairs)]
        for pp, cols in enumerate(pair_cols):
            s_ref[pp] = _dot_nt(_pair_stack(q_ref[:, cols], even), kp_ref[pl.ds(start, WIN), cols])
        for pp in range(npairs):
            for r in range(0, 2 * Q_TILE, STRIP):
                p = _strip_probs(s_ref, b_ref, pp, r, valid)
                p_ref[pp, r:r + STRIP, :] = p.astype(BF16)
        for pp, cols in enumerate(pair_cols):
            o_ref[:, cols] = _pair_merge(_dot(p_ref[pp], vp_ref[pl.ds(start, WIN), cols]), even).astype(BF16)

    return pl.pallas_call(
        body, name=name, grid=(D // lanes, B, nt),
        in_specs=[q_spec, k_spec, v_spec, bias_spec], out_specs=q_spec,
        out_shape=jax.ShapeDtypeStruct((T, D), BF16),
        scratch_shapes=[pltpu.VMEM((S + PAD, lanes), BF16), pltpu.VMEM((S + PAD, lanes), BF16),
                        pltpu.VMEM((npairs, 2 * Q_TILE, WIN), F32), pltpu.VMEM((npairs, 2 * Q_TILE, WIN), BF16)],
        compiler_params=_params(("arbitrary", "arbitrary", "arbitrary")),
    )(q, kv, kv, bias)


def _attn_backward(q, kv, bias, do, S, *, name, rider=None):
    T, D = q.shape
    B = T // S
    H = D // HEAD_DIM
    lanes = min(BWD_HEADS_PER_STEP * HEAD_DIM, D)
    nt, q_spec, k_spec, v_spec, bias_spec = _attn_specs(B, S, D, lanes)
    scale = HEAD_DIM ** -0.5

    npairs = lanes // (2 * HEAD_DIM)

    def body(q_ref, k_ref, v_ref, b_ref, do_ref, dq_ref, dk_ref, dv_ref, db_ref, kp_ref, vp_ref, dka_ref, dva_ref,
             s_ref, dp_ref, p_ref, ds_ref):
        b, i = pl.program_id(1), pl.program_id(2)

        @pl.when((b == 0) & (i == 0))
        def _():
            db_ref[...] = jnp.zeros_like(db_ref)

        @pl.when(i == 0)
        def _():
            _fill_padded(kp_ref, k_ref)
            _fill_padded(vp_ref, v_ref)
            dka_ref[...] = jnp.zeros_like(dka_ref)
            dva_ref[...] = jnp.zeros_like(dva_ref)

        start = pl.multiple_of(i * Q_TILE, Q_TILE)
        even = lax.broadcasted_iota(jnp.int32, (1, 2 * HEAD_DIM), 1) < HEAD_DIM
        valid = lax.broadcasted_iota(jnp.int32, (STRIP, WIN), 1) >= PAD - i * Q_TILE
        pair_cols = [slice(pp * 2 * HEAD_DIM, (pp + 1) * 2 * HEAD_DIM) for pp in range(npairs)]
        for pp, cols in enumerate(pair_cols):
            s_ref[pp] = _dot_nt(_pair_stack(q_ref[:, cols], even), kp_ref[pl.ds(start, WIN), cols])
            dp_ref[pp] = _dot_nt(_pair_stack(do_ref[:, cols], even), vp_ref[pl.ds(start, WIN), cols])
        for pp in range(npairs):
            for r in range(0, 2 * Q_TILE, STRIP):
                hb, hr = divmod(r, Q_TILE)
                p = _strip_probs(s_ref, b_ref, pp, r, valid)
                dp = dp_ref[pp, r:r + STRIP, :]
                ds = p * (dp - jnp.sum(p * dp, axis=-1, keepdims=True))
                db_ref[2 * pp + hb, hr:hr + STRIP, :] += ds
                p_ref[pp, r:r + STRIP, :] = p.astype(BF16)
                ds_ref[pp, r:r + STRIP, :] = ds.astype(BF16)
        for pp, cols in enumerate(pair_cols):
            dsb = ds_ref[pp]
            dq_ref[:, cols] = (_pair_merge(_dot(dsb, kp_ref[pl.ds(start, WIN), cols]), even) * scale).astype(BF16)
            dka_ref[pl.ds(start, WIN), cols] += _dot_tn(dsb, _pair_stack(q_ref[:, cols], even))
            dva_ref[pl.ds(start, WIN), cols] += _dot_tn(p_ref[pp], _pair_stack(do_ref[:, cols], even))

        @pl.when(i == nt - 1)
        def _():
            dk_ref[...] = dka_ref[PAD:, :].astype(BF16)
            dv_ref[...] = dva_ref[PAD:, :].astype(BF16)

    dkv_shape = jax.ShapeDtypeStruct((T, D), BF16)
    return _call(
        body, name=name, grid=(D // lanes, B, nt),
        in_specs=[q_spec, k_spec, v_spec, bias_spec, q_spec],
        out_specs=[q_spec, k_spec, k_spec, bias_spec],
        out_shape=[jax.ShapeDtypeStruct((T, D), BF16), dkv_shape, dkv_shape,
                   jax.ShapeDtypeStruct((H, Q_TILE, WIN), F32)],
        scratch_shapes=[pltpu.VMEM((S + PAD, lanes), BF16), pltpu.VMEM((S + PAD, lanes), BF16),
                        pltpu.VMEM((S + PAD, lanes), F32), pltpu.VMEM((S + PAD, lanes), F32),
                        pltpu.VMEM((npairs, 2 * Q_TILE, WIN), F32), pltpu.VMEM((npairs, 2 * Q_TILE, WIN), F32),
                        pltpu.VMEM((npairs, 2 * Q_TILE, WIN), BF16), pltpu.VMEM((npairs, 2 * Q_TILE, WIN), BF16)],
        args=(q, kv, kv, bias, do), rider=rider)


def _loss_head(h, g, target, name):
    T, D = h.shape
    tm = _tile(T, TOKEN_TILE)

    def body(h_ref, g_ref, t_ref, dh_ref, st_ref):
        @pl.when(pl.program_id(0) == 0)
        def _():
            st_ref[...] = jnp.zeros_like(st_ref)

        xh, r = _rms(h_ref[...], None)
        err = xh * g_ref[...] - t_ref[...]
        st_ref[1:2, :] += 0.5 * jnp.sum(jnp.mean(err * err, axis=-1, keepdims=True), axis=0, keepdims=True)
        dy = err * (1.0 / D)
        st_ref[0:1, :] += jnp.sum(dy * xh, axis=0, keepdims=True)
        dn = dy * g_ref[...]
        dh_ref[...] = r * (dn - xh * jnp.mean(dn * xh, axis=-1, keepdims=True))

    row = pl.BlockSpec((tm, D), lambda i: (i, 0))
    return pl.pallas_call(
        body, name=name, grid=(T // tm,),
        in_specs=[row, pl.BlockSpec((1, D), lambda i: (0, 0)), row],
        out_specs=[row, pl.BlockSpec((8, D), lambda i: (0, 0))],
        out_shape=[jax.ShapeDtypeStruct((T, D), F32), jax.ShapeDtypeStruct((8, D), F32)],
        compiler_params=_params(("arbitrary",)),
    )(h, g, target)


def _sum_devices(arrs, name):
    n = len(arrs)

    def body(*refs):
        for a in range(n):
            s = refs[a][0]
            for k in range(1, N_DEV):
                s = s + refs[a][k]
            refs[n + a][...] = s

    vm = pl.BlockSpec(memory_space=pltpu.VMEM)
    return pl.pallas_call(
        body, name=name, out_shape=[jax.ShapeDtypeStruct(a.shape[1:], F32) for a in arrs],
        in_specs=[vm] * n, out_specs=[vm] * n, compiler_params=_params(),
    )(*arrs)


def _adamw_math(w, g, m, v):
    m = ADAM_B1 * m + (1.0 - ADAM_B1) * g
    v = ADAM_B2 * v + (1.0 - ADAM_B2) * (g * g)
    m_hat = m / (1.0 - ADAM_B1 ** ADAM_STEP)
    v_hat = v / (1.0 - ADAM_B2 ** ADAM_STEP)
    delta = -ADAM_LR * (m_hat / (jnp.sqrt(v_hat) + ADAM_EPS) + ADAM_WD * w)
    return delta, m, v


def _adamw_small(items, name):
    n = len(items)

    def body(*refs):
        for a in range(n):
            w, m, v, g = (refs[4 * a + k][...] for k in range(4))
            d, m, v = _adamw_math(w, g, m, v)
            refs[4 * n + 3 * a][...] = d
            refs[4 * n + 3 * a + 1][...] = m
            refs[4 * n + 3 * a + 2][...] = v

    vm = pl.BlockSpec(memory_space=pltpu.VMEM)
    flat = [t for it in items for t in it]
    outs = pl.pallas_call(
        body, name=name,
        out_shape=[jax.ShapeDtypeStruct(it[0].shape, F32) for it in items for _ in range(3)],
        in_specs=[vm] * (4 * n), out_specs=[vm] * (3 * n), compiler_params=_params(),
    )(*flat)
    return [tuple(outs[3 * a:3 * a + 3]) for a in range(n)]


def _adamw_big(w, m, v, own, landed, name):
    R, C = w.shape
    tr = _tile(R, 256)

    def body(w_ref, m_ref, v_ref, own_ref, l_ref, g_ref, d_ref, mo_ref, vo_ref):
        g = own_ref[...]
        for k in range(3):
            g = g + l_ref[k].astype(F32)
        d, mn, vn = _adamw_math(w_ref[...], g, m_ref[...], v_ref[...])
        g_ref[...] = g
        d_ref[...] = d
        mo_ref[...] = mn
        vo_ref[...] = vn

    row = pl.BlockSpec((tr, C), lambda i: (i, 0))
    return pl.pallas_call(
        body, name=name, grid=(R // tr,),
        in_specs=[row, row, row, row, pl.BlockSpec((3, tr, C), lambda i: (0, i, 0))],
        out_specs=[row] * 4, out_shape=[jax.ShapeDtypeStruct((R, C), F32)] * 4,
        compiler_params=_params(("arbitrary",)),
    )(w, m, v, own, landed)


def kernel(x, a_norm_g, a_w_in, a_v_norm_g, a_w_s, a_b_s, a_w_out, kv_norm_g, w_kv, b_norm_g, b_w_q, b_rel_bias, b_w_o, f_norm_g, f_w_in, f_conv_w, f_conv_b, f_w_down, final_norm_g, loss_target, m_a_norm_g, m_a_w_in, m_a_v_norm_g, m_a_w_s, m_a_b_s, m_a_w_out, m_kv_norm_g, m_w_kv, m_b_norm_g, m_b_w_q, m_b_rel_bias, m_b_w_o, m_f_norm_g, m_f_w_in, m_f_conv_w, m_f_conv_b, m_f_w_down, m_final_norm_g, v_a_norm_g, v_a_w_in, v_a_v_norm_g, v_a_w_s, v_a_b_s, v_a_w_out, v_kv_norm_g, v_w_kv, v_b_norm_g, v_b_w_q, v_b_rel_bias, v_b_w_o, v_f_norm_g, v_f_w_in, v_f_conv_w, v_f_conv_b, v_f_w_down, v_final_norm_g):
    B, S, D = x.shape
    T = B * S
    G = a_w_s.shape[1]
    H = D // HEAD_DIM
    F = f_w_in.shape[2]
    L = f_w_in.shape[0]
    dn = D // N_DEV
    xi, yi, ci = lax.axis_index("x"), lax.axis_index("y"), lax.axis_index("c")
    me = 4 * xi + 2 * yi + ci
    pos = jnp.stack([ci, 2 * xi + yi]).astype(jnp.int32)

    cast = lambda t: t.astype(BF16)
    gather = lambda *ts: _gather_rider(list(ts))
    wa_in, norms_sh, conv_w0, conv_w1 = _run_rider(
        gather(cast(a_w_in[0]), jnp.concatenate([a_norm_g, a_v_norm_g], axis=0), f_conv_w[0], f_conv_w[1]),
        "gather_first")
    ga = jnp.transpose(norms_sh, (1, 0, 2)).reshape(2, D)
    g_a, g_av = ga[0:1], ga[1:2]

    x2 = x.reshape(T, D)
    tgt = loss_target.reshape(T, D)
    pc = jnp.arange(GMLP_BLOCK) // CHUNK
    mask = (pc[:, None] >= pc[None, :]).astype(F32)
    ws = (a_w_s[0] * mask[None]).astype(BF16)
    bst = jnp.transpose(a_b_s[0])
    rel = jnp.pad(b_rel_bias[0], ((0, 0), (0, REL_PAD - b_rel_bias.shape[2])))
    four = lambda t: t.reshape((4, 2) + t.shape[1:])

    bias = _bias_build(rel, "bias_build")
    (z, n_a), (wa_out, w_in0) = _norm_matmul(x2, g_a, wa_in, flat=True, nbk=4, name="gmlp_in",
                                             rider=gather(cast(a_w_out[0]), cast(f_w_in[0])))
    wa_out = wa_out.reshape(D, D)
    (gated, h1), (wf_down0,) = _gmlp_forward(z, ws, bst, g_av, wa_out, x2, name="gmlp_mix",
                                             rider=gather(cast(f_w_down[0])))
    cw0, cb0, wd0 = conv_w0, f_conv_b[0].reshape(8, 1, F), wf_down0.reshape(4, F, D)
    (a0, n_f0), (wkv, wq, wo) = _norm_matmul(h1, f_norm_g[0:1], w_in0, flat=False, nbk=2, name="ffn0_in",
                                             rider=gather(cast(w_kv), cast(b_w_q[0]), cast(b_w_o[0])))
    wq, wo = wq.reshape(D, D), wo.reshape(D, D)
    (act0, c0, h2), (w_in1, wf_down1) = _ffn_forward(a0, cw0, cb0, wd0, h1, S, name="ffn0_out",
                                                     rider=gather(cast(f_w_in[1]), cast(f_w_down[1])))
    cw1, cb1, wd1 = conv_w1, f_conv_b[1].reshape(8, 1, F), wf_down1.reshape(4, F, D)
    kv, n_kv = _norm_matmul(h2, kv_norm_g.reshape(1, D), wkv, flat=True, nbk=4, name="kv_proj")
    q, n_q = _norm_matmul(h2, b_norm_g, wq.reshape(1, D, D), flat=True, nbk=1, name="q_proj", scale=HEAD_DIM ** -0.5)
    o = _attn_forward(q, kv, bias, S, name="attn")
    h3 = _matmul_residual(o, wo, h2, "attn_out")
    a1, n_f1 = _norm_matmul(h3, f_norm_g[1:2], w_in1, flat=False, nbk=2, name="ffn1_in")
    act1, c1, h4 = _ffn_forward(a1, cw1, cb1, wd1, h3, S, name="ffn1_out")

    sums, from_chips = {}, {}

    def sibling_sums(names, parts, landed):
        for nm, p, l in zip(names, parts, landed):
            sums[nm] = _sibling_sum(p, l, pos, "grad_sibling_sum_" + nm)

    def chip_rider(*names):
        return _chip_rider([sums[nm][1] for nm in names])

    dh4, st_final = _loss_head(h4, final_norm_g.reshape(1, D), tgt, "loss_head")
    da1, st_conv1 = _ffn_backward(dh4, c1, a1, cw1, wd1, S, name="ffn1_bwd")
    g_wd1 = _wgrad_rows(act1, dh4, flat=False, tk=F, name="ffn1_dwdown")
    g_win1 = _wgrad_cols(n_f1, da1, flat=False, nb=8, nbk=2, name="ffn1_dwin")
    parts = [four(g_wd1.reshape(8, F // 2, D)), four(g_win1)]
    (dh3, st_f1), landed = _matmul_nt(da1, w_in1, flat=False, nbk=4, name="ffn1_dx", norm=(h3, f_norm_g[1:2], dh4),
                                      rider=_sibling_rider(parts))
    sibling_sums(["wd1", "win1"], parts, landed)
    d_o = _matmul_nt(dh3, wo.reshape(1, D, D), flat=True, nbk=1, name="attn_out_dx")
    g_wo = _wgrad_rows(o, dh3, flat=True, tk=_tile(D, 512), name="attn_out_dw")
    (dq, dk, dv, dbias), (from_chips["wd1"], from_chips["win1"]) = _attn_backward(
        q, kv, bias, d_o, S, name="attn_bwd", rider=chip_rider("wd1", "win1"))
    g_rel = _bias_reduce(dbias, "bias_reduce")
    g_wq = _wgrad_cols(n_q, dq, flat=True, nb=1, nbk=1, name="q_dw")
    dh2, st_b = _matmul_nt(dq, wq.reshape(1, D, D), flat=True, nbk=1, name="q_dx", norm=(h2, b_norm_g, dh3))
    dkv = jnp.concatenate([dk, dv], axis=-1)
    g_wkv = _wgrad_cols(n_kv, dkv, flat=True, nb=8, nbk=4, name="kv_dw")
    parts = [four(g_wo.reshape(8, dn, D)), four(g_wq.reshape(8, dn, D)), four(g_wkv)]
    (dh2, st_kv), landed = _matmul_nt(dkv, wkv, flat=True, nbk=4, name="kv_dx",
                                      norm=(h2, kv_norm_g.reshape(1, D), dh2), rider=_sibling_rider(parts))
    sibling_sums(["wo", "wq", "wkv"], parts, landed)
    (da0, st_conv0), (from_chips["wo"], from_chips["wq"], from_chips["wkv"]) = _ffn_backward(
        dh2, c0, a0, cw0, wd0, S, name="ffn0_bwd", rider=chip_rider("wo", "wq", "wkv"))
    g_wd0 = _wgrad_rows(act0, dh2, flat=False, tk=F, name="ffn0_dwdown")
    g_win0 = _wgrad_cols(n_f0, da0, flat=False, nb=8, nbk=2, name="ffn0_dwin")
    parts = [four(g_wd0.reshape(8, F // 2, D)), four(g_win0)]
    (dh1, st_f0), landed = _matmul_nt(da0, w_in0, flat=False, nbk=4, name="ffn0_dx", norm=(h1, f_norm_g[0:1], dh2),
                                      rider=_sibling_rider(parts))
    sibling_sums(["wd0", "win0"], parts, landed)
    dgated = _matmul_nt(dh1, wa_out.reshape(1, D, D), flat=True, nbk=1, name="gmlp_out_dx")
    g_wa_out, (from_chips["wd0"],) = _wgrad_rows(gated, dh1, flat=True, tk=_tile(D, 512), name="gmlp_out_dw",
                                                 rider=chip_rider("wd0"))
    parts = [four(g_wa_out.reshape(8, dn, D))]
    (dz, g_ws, g_bst, st_av), (from_chips["win0"], landed) = _gmlp_backward(
        z, dgated, ws, bst, g_av, mask, name="gmlp_bwd",
        rider=_join_riders([chip_rider("win0"), _sibling_rider(parts)]))
    sibling_sums(["wa_out"], parts, [landed])
    vec = jnp.concatenate([st_av[0:1], st_kv[0:1], st_b[0:1], st_f0[0:1], st_f1[0:1], st_final[0:3]], axis=0)
    g_wa_in, got = _wgrad_cols(n_a, dz, flat=True, nb=8, nbk=4, name="gmlp_in_dw", rider=_join_riders(
        [chip_rider("wa_out"), gather(vec, g_ws, g_bst, g_rel, st_conv0, st_conv1)]))
    from_chips["wa_out"], small = got[0], got[1:]
    parts = [four(g_wa_in)]
    (grad_x, st_a), landed = _matmul_nt(dz, wa_in, flat=True, nbk=4, name="gmlp_in_dx", norm=(x2, g_a, dh1),
                                        rider=_sibling_rider(parts))
    sibling_sums(["wa_in"], parts, landed)
    from_chips["wa_in"], st_a = _run_rider(_join_riders([chip_rider("wa_in"), gather(st_a)]), "last_exchange")

    def big_update(nm, w, m, v):
        shape = w.shape
        r = lambda t: t.reshape(-1, shape[-1])
        outs = _adamw_big(r(w), r(m), r(v), sums[nm][0], from_chips[nm], "adamw_" + nm)
        return [t.reshape(shape) for t in outs]

    u_a_w_in = big_update("wa_in", a_w_in, m_a_w_in, v_a_w_in)
    u_w_kv = big_update("wkv", w_kv, m_w_kv, v_w_kv)
    u_a_w_out = big_update("wa_out", a_w_out, m_a_w_out, v_a_w_out)
    u_b_w_q = big_update("wq", b_w_q, m_b_w_q, v_b_w_q)
    u_b_w_o = big_update("wo", b_w_o, m_b_w_o, v_b_w_o)
    u_f_w_in = [jnp.stack(ts) for ts in zip(*[big_update("win%d" % l, f_w_in[l], m_f_w_in[l], v_f_w_in[l])
                                             for l in range(L)])]
    u_f_w_down = [jnp.stack(ts) for ts in zip(*[big_update("wd%d" % l, f_w_down[l], m_f_w_down[l], v_f_w_down[l])
                                               for l in range(L)])]

    vec, g_ws, g_bst, g_rel, st_conv0, st_conv1, st_a = _sum_devices(list(small) + [st_a], "sum_small_grads")
    vec = jnp.concatenate([st_a[0:1], vec[0:7]], axis=0)
    loss = vec[7, 0]
    g_a_norm = lax.dynamic_slice_in_dim(vec[0:1], me * dn, dn, axis=1)
    g_av_norm = lax.dynamic_slice_in_dim(vec[1:2], me * dn, dn, axis=1)
    st_conv = jnp.stack([st_conv0, st_conv1])
    g_conv_w = lax.dynamic_index_in_dim(st_conv, me, axis=1, keepdims=False)[:, 0:3]
    g_conv_b = st_conv[:, :, 3, :].reshape(L, 8 * F)
    small_items = [
        (a_norm_g, m_a_norm_g, v_a_norm_g, g_a_norm),
        (a_v_norm_g, m_a_v_norm_g, v_a_v_norm_g, g_av_norm),
        (a_w_s, m_a_w_s, v_a_w_s, g_ws[None]),
        (a_b_s, m_a_b_s, v_a_b_s, jnp.transpose(g_bst)[None]),
        (kv_norm_g.reshape(1, D), m_kv_norm_g.reshape(1, D), v_kv_norm_g.reshape(1, D), vec[2:3]),
        (b_norm_g, m_b_norm_g, v_b_norm_g, vec[3:4]),
        (b_rel_bias, m_b_rel_bias, v_b_rel_bias, g_rel[None, :, :b_rel_bias.shape[2]]),
        (f_norm_g, m_f_norm_g, v_f_norm_g, vec[4:6]),
        (f_conv_w, m_f_conv_w, v_f_conv_w, g_conv_w),
        (f_conv_b, m_f_conv_b, v_f_conv_b, g_conv_b),
        (final_norm_g.reshape(1, D), m_final_norm_g.reshape(1, D), v_final_norm_g.reshape(1, D), vec[6:7]),
    ]
    small_out = _adamw_small(small_items, "adamw_small")
    (u_a_norm, u_av_norm, u_ws, u_bs, u_kvn, u_bn, u_rel, u_fn, u_cw, u_cb, u_fin) = [
        (it[3],) + so for it, so in zip(small_items, small_out)]
    vecD = lambda u: tuple(t.reshape(D) for t in u)
    u_kvn, u_fin = vecD(u_kvn), vecD(u_fin)

    order = [u_a_norm, u_a_w_in, u_av_norm, u_ws, u_bs, u_a_w_out, u_kvn, u_w_kv, u_bn, u_b_w_q, u_rel, u_b_w_o,
             u_fn, u_f_w_in, u_cw, u_cb, u_f_w_down, u_fin]
    outs = [loss, grad_x.reshape(B, S, D)]
    for k in range(4):
        outs += [u[k] for u in order]
    return tuple(outs)
```

```python
import functools

import jax
import jax.numpy as jnp
from jax import lax
from jax.experimental import pallas as pl
from jax.experimental.pallas import tpu as pltpu

F32 = jnp.float32
BF16 = jnp.bfloat16
MESH = pl.DeviceIdType.MESH

N_DEV = 8
EPS = 1e-6
NEG_INF = -1e30
CHUNK = 64
LEFT_CHUNKS = 8
REL_CLIP = 128
HEAD_DIM = 64
GMLP_BLOCK = 128
Q_TILE = 2 * CHUNK
PAD = LEFT_CHUNKS * CHUNK
WIN = PAD + Q_TILE
SKEW = WIN + Q_TILE
REL_PAD = 384
FWD_HEADS_PER_STEP = 8
BWD_HEADS_PER_STEP = 4
STRIP = 32
ADAM_LR, ADAM_B1, ADAM_B2, ADAM_EPS, ADAM_WD, ADAM_STEP = 0.001, 0.9, 0.999, 1e-08, 0.01, 10
VMEM_LIMIT = 56 * 1024 * 1024
TOKEN_TILE = 512


def _params(sem=None):
    return pltpu.CompilerParams(dimension_semantics=sem, vmem_limit_bytes=VMEM_LIMIT)


def _tile(n, pref):
    if n <= pref:
        return n
    for t in range(pref - pref % 8, 7, -8):
        if n % t == 0:
            return t
    return n


def _gelu(x):
    return 0.5 * x * (1.0 + jnp.tanh(0.7978845608028654 * (x + 0.044715 * x * x * x)))


def _gelu_grad(x):
    t = jnp.tanh(0.7978845608028654 * (x + 0.044715 * x * x * x))
    return 0.5 * (1.0 + t) + 0.5 * x * (1.0 - t * t) * 0.7978845608028654 * (1.0 + 3 * 0.044715 * x * x)


def _sigmoid(x):
    return 1.0 / (1.0 + jnp.exp(-x))


def _dot(a, b):
    return jnp.dot(a, b, preferred_element_type=F32)


def _dot_nt(a, b):
    return lax.dot_general(a, b, (((1,), (1,)), ((), ())), preferred_element_type=F32)


def _dot_tn(a, b):
    return lax.dot_general(a, b, (((0,), (0,)), ((), ())), preferred_element_type=F32)


def _split3(x):
    hi = x.astype(BF16)
    r1 = x - hi.astype(F32)
    mid = r1.astype(BF16)
    lo = (r1 - mid.astype(F32)).astype(BF16)
    return hi, mid, lo


def _mesh_pos():
    return lax.axis_index("x"), lax.axis_index("y"), lax.axis_index("c")


class _Rider:
    def __init__(self, arrs, out_shapes, sems, start, finish):
        self.arrs, self.out_shapes, self.sems, self.start, self.finish = arrs, out_shapes, sems, start, finish


def _gather_rider(arrs):
    n = len(arrs)

    def tools(ins, outs, sems):
        send_sems, recv_sems, local_sems = sems
        x, y, c = _mesh_pos()
        me, sibling = (x, y, c), (x, y, 1 - c)
        chips = [(1 - x, y), (x, 1 - y), (1 - x, 1 - y)]

        def slot(a, block):
            px, py, pc = block
            return outs[a].at[4 * px + 2 * py + pc]

        def copy(a, k, block, to, src=None):
            dst = slot(a, block)
            return pltpu.make_async_remote_copy(
                src_ref=dst if src is None else src, dst_ref=dst,
                send_sem=send_sems.at[a, k], recv_sem=recv_sems.at[a, k], device_id=to, device_id_type=MESH)

        def first(a):
            cps = [copy(a, 0, me, sibling, src=ins[a])]
            return cps + [copy(a, 1 + j, me, (*chip, c), src=ins[a]) for j, chip in enumerate(chips)]

        def mine(a):
            return pltpu.make_async_copy(ins[a], slot(a, me), local_sems.at[a])

        return me, sibling, chips, c, copy, first, mine

    def start(ins, outs, sems):
        _, _, _, _, _, first, mine = tools(ins, outs, sems)
        for a in range(n):
            mine(a).start()
            for cp in first(a):
                cp.start()

    def finish(ins, outs, sems):
        me, sibling, chips, c, copy, first, mine = tools(ins, outs, sems)
        passed = []
        for j, chip in enumerate(chips):
            for a in range(n):
                copy(a, 1 + j, (*chip, c), me).wait_recv()
                fwd = copy(a, 4 + j, (*chip, c), sibling)
                fwd.start()
                passed.append(fwd)
        for a in range(n):
            copy(a, 0, sibling, me).wait_recv()
            for j, chip in enumerate(chips):
                copy(a, 4 + j, (*chip, 1 - c), me).wait_recv()
        for a in range(n):
            for cp in first(a):
                cp.wait_send()
        for cp in passed:
            cp.wait_send()
        for a in range(n):
            mine(a).wait()

    return _Rider(list(arrs), [jax.ShapeDtypeStruct((N_DEV,) + a.shape, a.dtype) for a in arrs],
                  [pltpu.SemaphoreType.DMA((n, 7)), pltpu.SemaphoreType.DMA((n, 7)), pltpu.SemaphoreType.DMA((n,))],
                  start, finish)


def _sibling_rider(arrs):
    n = len(arrs)

    def copies(ins, outs, sems):
        x, y, c = _mesh_pos()
        return [pltpu.make_async_remote_copy(
            src_ref=ins[a].at[:, pl.ds(1 - c, 1)], dst_ref=outs[a],
            send_sem=sems[0].at[a], recv_sem=sems[1].at[a], device_id=(x, y, 1 - c), device_id_type=MESH)
            for a in range(n)]

    def start(ins, outs, sems):
        for cp in copies(ins, outs, sems):
            cp.start()

    def finish(ins, outs, sems):
        for cp in copies(ins, outs, sems):
            cp.wait()

    return _Rider(list(arrs), [jax.ShapeDtypeStruct((4, 1) + a.shape[2:], a.dtype) for a in arrs],
                  [pltpu.SemaphoreType.DMA((n,)), pltpu.SemaphoreType.DMA((n,))], start, finish)


def _chip_rider(arrs, ks=(1, 2, 3)):
    n = len(arrs)

    def copies(ins, outs, sems):
        x, y, c = _mesh_pos()
        cps = []
        for a in range(n):
            for s, k in enumerate(ks):
                px = x if k < 2 else 1 - x
                py = y if k == 2 else 1 - y
                cps.append(pltpu.make_async_remote_copy(
                    src_ref=ins[a].at[2 * px + py], dst_ref=outs[a].at[s],
                    send_sem=sems[0].at[a, s], recv_sem=sems[1].at[a, s],
                    device_id=(px, py, c), device_id_type=MESH))
        return cps

    def start(ins, outs, sems):
        for cp in copies(ins, outs, sems):
            cp.start()

    def finish(ins, outs, sems):
        for cp in copies(ins, outs, sems):
            cp.wait()

    return _Rider(list(arrs), [jax.ShapeDtypeStruct((len(ks),) + a.shape[1:], a.dtype) for a in arrs],
                  [pltpu.SemaphoreType.DMA((n, len(ks))), pltpu.SemaphoreType.DMA((n, len(ks)))], start, finish)


def _join_riders(riders):
    def split(seq, counts):
        out, at = [], 0
        for k in counts:
            out.append(seq[at:at + k])
            at += k
        return out

    n_in = [len(r.arrs) for r in riders]
    n_out = [len(r.out_shapes) for r in riders]
    n_sem = [len(r.sems) for r in riders]

    def run(which):
        def fn(ins, outs, sems):
            for r, i, o, s in zip(riders, split(ins, n_in), split(outs, n_out), split(sems, n_sem)):
                getattr(r, which)(i, o, s)
        return fn

    return _Rider([a for r in riders for a in r.arrs], [o for r in riders for o in r.out_shapes],
                  [s for r in riders for s in r.sems], run("start"), run("finish"))


def _run_rider(rider, name):
    n_in, n_out = len(rider.arrs), len(rider.out_shapes)

    def body(*refs):
        ins, outs, sems = refs[:n_in], refs[n_in:n_in + n_out], refs[n_in + n_out:]
        rider.start(ins, outs, sems)
        rider.finish(ins, outs, sems)

    any_spec = pl.BlockSpec(memory_space=pl.ANY)
    return pl.pallas_call(
        body, name=name, out_shape=list(rider.out_shapes), in_specs=[any_spec] * n_in, out_specs=[any_spec] * n_out,
        scratch_shapes=list(rider.sems),
    )(*rider.arrs)


def _call(body, *, name, grid, in_specs, out_specs, out_shape, args, scratch_shapes=(), rider=None):
    params = _params(("arbitrary",) * len(grid))
    if rider is None:
        return pl.pallas_call(body, name=name, grid=grid, in_specs=in_specs, out_specs=out_specs, out_shape=out_shape,
                              scratch_shapes=list(scratch_shapes), compiler_params=params)(*args)
    single = not isinstance(out_shape, (list, tuple))
    outs = [out_shape] if single else list(out_shape)
    ospecs = [out_specs] if single else list(out_specs)
    n_in, n_out, n_scr = len(in_specs), len(outs), len(scratch_shapes)
    r_in, r_out = len(rider.arrs), len(rider.out_shapes)

    def hosted(*refs):
        refs = list(refs)
        ins, rins = refs[:n_in], refs[n_in:n_in + r_in]
        refs = refs[n_in + r_in:]
        houts, routs = refs[:n_out], refs[n_out:n_out + r_out]
        refs = refs[n_out + r_out:]
        scr, rsems = refs[:n_scr], refs[n_scr:]
        ids = [pl.program_id(a) for a in range(len(grid))]
        first = functools.reduce(lambda p, q: p & q, [i == 0 for i in ids])
        last = functools.reduce(lambda p, q: p & q, [i == g - 1 for i, g in zip(ids, grid)])

        @pl.when(first)
        def _():
            rider.start(rins, routs, rsems)

        body(*ins, *houts, *scr)

        @pl.when(last)
        def _():
            rider.finish(rins, routs, rsems)

    any_spec = pl.BlockSpec(memory_space=pl.ANY)
    res = pl.pallas_call(
        hosted, name=name, grid=grid, in_specs=list(in_specs) + [any_spec] * r_in,
        out_specs=ospecs + [any_spec] * r_out, out_shape=outs + list(rider.out_shapes),
        scratch_shapes=list(scratch_shapes) + list(rider.sems), compiler_params=params,
    )(*args, *rider.arrs)
    return (res[0] if single else list(res[:n_out])), list(res[n_out:])


def _sibling_sum(part, landed, pos, name):
    _, _, rows, cols = part.shape
    tr = _tile(rows, 512)

    def body(pos_ref, p_ref, l_ref, own_ref, all_ref):
        s = p_ref[0, 0] + l_ref[0, 0]
        all_ref[0] = s.astype(BF16)

        @pl.when(pl.program_id(1) == pos_ref[1])
        def _():
            own_ref[...] = s

    return pl.pallas_call(
        body, name=name,
        grid_spec=pltpu.PrefetchScalarGridSpec(
            num_scalar_prefetch=1, grid=(rows // tr, 4),
            in_specs=[pl.BlockSpec((1, 1, tr, cols), lambda i, k, pos: (k, pos[0], i, 0)),
                      pl.BlockSpec((1, 1, tr, cols), lambda i, k, pos: (k, 0, i, 0))],
            out_specs=[pl.BlockSpec((tr, cols), lambda i, k, pos: (i, 0)),
                       pl.BlockSpec((1, tr, cols), lambda i, k, pos: (k, i, 0))]),
        out_shape=[jax.ShapeDtypeStruct((rows, cols), F32), jax.ShapeDtypeStruct((4, rows, cols), BF16)],
        compiler_params=_params(("arbitrary", "arbitrary")),
    )(pos, part, landed)


def _rms(x, g):
    r = lax.rsqrt(jnp.mean(x * x, axis=-1, keepdims=True) + EPS)
    return x * r, r


def _norm_matmul(h, g, w, *, flat, nbk, name, scale=1.0, rider=None):
    T, D = h.shape
    nb, _, bn = w.shape
    tm = _tile(T, TOKEN_TILE)

    def body(h_ref, g_ref, w_ref, o_ref, n_ref):
        @pl.when(pl.program_id(1) == 0)
        def _():
            xh, _ = _rms(h_ref[...], None)
            n_ref[...] = (xh * g_ref[...]).astype(BF16)

        n = n_ref[...]
        for k in range(nbk):
            r = _dot(n, w_ref[k])
            r = (r if scale == 1.0 else r * scale).astype(BF16)
            if flat:
                o_ref[:, k * bn:(k + 1) * bn] = r
            else:
                o_ref[k] = r

    if flat:
        out_shape = jax.ShapeDtypeStruct((T, nb * bn), BF16)
        out_spec = pl.BlockSpec((tm, nbk * bn), lambda i, j: (i, j))
    else:
        out_shape = jax.ShapeDtypeStruct((nb, T, bn), BF16)
        out_spec = pl.BlockSpec((nbk, tm, bn), lambda i, j: (j, i, 0))
    return _call(
        body, name=name, grid=(T // tm, nb // nbk),
        in_specs=[pl.BlockSpec((tm, D), lambda i, j: (i, 0)),
                  pl.BlockSpec((1, D), lambda i, j: (0, 0)),
                  pl.BlockSpec((nbk, D, bn), lambda i, j: (j, 0, 0))],
        out_specs=[out_spec, pl.BlockSpec((tm, D), lambda i, j: (i, 0))],
        out_shape=[out_shape, jax.ShapeDtypeStruct((T, D), BF16)],
        args=(h, g, w), rider=rider)


def _matmul_nt(dy, w, *, flat, nbk, name, norm=None, out_dtype=BF16, rider=None):
    nb, R, bn = w.shape
    T = dy.shape[0] if flat else dy.shape[1]
    tm = _tile(T, TOKEN_TILE)
    nj = nb // nbk

    def body(*refs):
        if norm is None:
            dy_ref, w_ref, o_ref, acc_ref = refs
        else:
            dy_ref, w_ref, h_ref, g_ref, dres_ref, o_ref, dg_ref, acc_ref = refs
        i, j = pl.program_id(0), pl.program_id(1)

        @pl.when(j == 0)
        def _():
            acc_ref[...] = jnp.zeros_like(acc_ref)

        acc = acc_ref[...]
        for k in range(nbk):
            d = dy_ref[:, k * bn:(k + 1) * bn] if flat else dy_ref[k]
            acc = acc + _dot_nt(d.astype(BF16), w_ref[k])
        acc_ref[...] = acc

        @pl.when(j == nj - 1)
        def _():
            if norm is None:
                o_ref[...] = acc.astype(out_dtype)
            else:
                xh, r = _rms(h_ref[...], None)

                @pl.when(i == 0)
                def _():
                    dg_ref[...] = jnp.zeros_like(dg_ref)

                dg_ref[0:1, :] += jnp.sum(acc * xh, axis=0, keepdims=True)
                dn = acc * g_ref[...]
                o_ref[...] = dres_ref[...] + r * (dn - xh * jnp.mean(dn * xh, axis=-1, keepdims=True))

    if flat:
        dy_spec = pl.BlockSpec((tm, nbk * bn), lambda i, j: (i, j))
    else:
        dy_spec = pl.BlockSpec((nbk, tm, bn), lambda i, j: (j, i, 0))
    w_spec = pl.BlockSpec((nbk, R, bn), lambda i, j: (j, 0, 0))
    row_spec = pl.BlockSpec((tm, R), lambda i, j: (i, 0))
    if norm is None:
        in_specs, args = [dy_spec, w_spec], (dy, w)
        out_specs = row_spec
        out_shape = jax.ShapeDtypeStruct((T, R), out_dtype)
    else:
        in_specs = [dy_spec, w_spec, row_spec, pl.BlockSpec((1, R), lambda i, j: (0, 0)), row_spec]
        args = (dy, w) + tuple(norm)
        out_specs = [row_spec, pl.BlockSpec((8, R), lambda i, j: (0, 0))]
        out_shape = [jax.ShapeDtypeStruct((T, R), F32), jax.ShapeDtypeStruct((8, R), F32)]
    return _call(
        body, name=name, grid=(T // tm, nj), in_specs=in_specs, out_specs=out_specs, out_shape=out_shape,
        scratch_shapes=[pltpu.VMEM((tm, R), F32)], args=args, rider=rider)


def _wgrad_cols(n, dy, *, flat, nb, nbk, name, rider=None):
    T, D = n.shape
    bn = dy.shape[1] // nb if flat else dy.shape[2]
    tt = _tile(T, TOKEN_TILE)
    nt = T // tt

    def body(n_ref, dy_ref, o_ref, acc_ref):
        t = pl.program_id(1)

        @pl.when(t == 0)
        def _():
            acc_ref[...] = jnp.zeros_like(acc_ref)

        nv = n_ref[...]
        for k in range(nbk):
            d = dy_ref[:, k * bn:(k + 1) * bn] if flat else dy_ref[k]
            acc_ref[k] += _dot_tn(nv, d)

        @pl.when(t == nt - 1)
        def _():
            o_ref[...] = acc_ref[...]

    if flat:
        dy_spec = pl.BlockSpec((tt, nbk * bn), lambda j, t: (t, j))
    else:
        dy_spec = pl.BlockSpec((nbk, tt, bn), lambda j, t: (j, t, 0))
    return _call(
        body, name=name, grid=(nb // nbk, nt),
        in_specs=[pl.BlockSpec((tt, D), lambda j, t: (t, 0)), dy_spec],
        out_specs=pl.BlockSpec((nbk, D, bn), lambda j, t: (j, 0, 0)),
        out_shape=jax.ShapeDtypeStruct((nb, D, bn), F32),
        scratch_shapes=[pltpu.VMEM((nbk, D, bn), F32)], args=(n, dy), rider=rider)


def _wgrad_rows(xa, dh, *, flat, tk, name, rider=None):
    T, D = dh.shape
    nk = xa.shape[1] // tk if flat else xa.shape[0]
    tt = _tile(T, TOKEN_TILE)
    nt = T // tt

    def body(x_ref, dh_ref, o_ref, acc_ref):
        t = pl.program_id(1)

        @pl.when(t == 0)
        def _():
            acc_ref[...] = jnp.zeros_like(acc_ref)

        xv = x_ref[...] if flat else x_ref[0]
        acc_ref[...] += _dot_tn(xv, dh_ref[...].astype(BF16))

        @pl.when(t == nt - 1)
        def _():
            o_ref[...] = acc_ref[...]

    x_spec = pl.BlockSpec((tt, tk), lambda j, t: (t, j)) if flat else pl.BlockSpec((1, tt, tk), lambda j, t: (j, t, 0))
    return _call(
        body, name=name, grid=(nk, nt),
        in_specs=[x_spec, pl.BlockSpec((tt, D), lambda j, t: (t, 0))],
        out_specs=pl.BlockSpec((tk, D), lambda j, t: (j, 0)),
        out_shape=jax.ShapeDtypeStruct((nk * tk, D), F32),
        scratch_shapes=[pltpu.VMEM((tk, D), F32)], args=(xa, dh), rider=rider)


def _matmul_residual(xa, w, res, name):
    T, K = xa.shape
    D = w.shape[1]
    tm = _tile(T, TOKEN_TILE)

    def body(x_ref, w_ref, r_ref, o_ref):
        o_ref[...] = r_ref[...] + _dot(x_ref[...], w_ref[...])

    return pl.pallas_call(
        body, name=name, grid=(T // tm,),
        in_specs=[pl.BlockSpec((tm, K), lambda i: (i, 0)), pl.BlockSpec((K, D), lambda i: (0, 0)),
                  pl.BlockSpec((tm, D), lambda i: (i, 0))],
        out_specs=pl.BlockSpec((tm, D), lambda i: (i, 0)),
        out_shape=jax.ShapeDtypeStruct((T, D), F32),
        compiler_params=_params(("arbitrary",)),
    )(xa, w, res)


def _gmlp_gate(z, ws, bst, gv, G, gd):
    D = G * gd
    u = _gelu(z[:, :D].astype(F32))
    v = _gelu(z[:, D:].astype(F32))
    vh, r = _rms(v, None)
    vn = (vh * gv).astype(BF16)
    return u, v, vh, r, vn


def _gmlp_forward(z, ws, bst, gv, w_out, x, *, name, rider=None):
    T, D2 = z.shape
    D = D2 // 2
    G = ws.shape[0]
    gd = D // G
    tb = _tile(T, 256)
    nblk = tb // GMLP_BLOCK

    def body(z_ref, ws_ref, b_ref, gv_ref, wo_ref, x_ref, gated_ref, h_ref):
        u, _, _, _, vn = _gmlp_gate(z_ref[...], None, None, gv_ref[...], G, gd)
        for n in range(nblk):
            rows = slice(n * GMLP_BLOCK, (n + 1) * GMLP_BLOCK)
            for gi in range(G):
                cols = slice(gi * gd, (gi + 1) * gd)
                s = _dot(ws_ref[gi], vn[rows, cols]) + b_ref[:, gi:gi + 1]
                gated_ref[rows, cols] = (u[rows, cols] * s).astype(BF16)
        h_ref[...] = x_ref[...] + _dot(gated_ref[...], wo_ref[...])

    return _call(
        body, name=name, grid=(T // tb,),
        in_specs=[pl.BlockSpec((tb, D2), lambda i: (i, 0)), pl.BlockSpec(ws.shape, lambda i: (0, 0, 0)),
                  pl.BlockSpec(bst.shape, lambda i: (0, 0)), pl.BlockSpec((1, D), lambda i: (0, 0)),
                  pl.BlockSpec((D, D), lambda i: (0, 0)), pl.BlockSpec((tb, D), lambda i: (i, 0))],
        out_specs=[pl.BlockSpec((tb, D), lambda i: (i, 0)), pl.BlockSpec((tb, D), lambda i: (i, 0))],
        out_shape=[jax.ShapeDtypeStruct((T, D), BF16), jax.ShapeDtypeStruct((T, D), F32)],
        args=(z, ws, bst, gv, w_out, x), rider=rider)


def _gmlp_backward(z, dgated, ws, bst, gv, mask, *, name, rider=None):
    T, D2 = z.shape
    D = D2 // 2
    G = ws.shape[0]
    gd = D // G
    tb = _tile(T, 256)
    nblk = tb // GMLP_BLOCK

    def body(z_ref, dg_ref, ws_ref, b_ref, gv_ref, mask_ref, dz_ref, dws_ref, db_ref, dgv_ref, dvn_ref):
        @pl.when(pl.program_id(0) == 0)
        def _():
            dws_ref[...] = jnp.zeros_like(dws_ref)
            db_ref[...] = jnp.zeros_like(db_ref)
            dgv_ref[...] = jnp.zeros_like(dgv_ref)

        zf = z_ref[...]
        u, v, vh, r, vn = _gmlp_gate(zf, None, None, gv_ref[...], G, gd)
        dg = dg_ref[...].astype(F32)
        for n in range(nblk):
            rows = slice(n * GMLP_BLOCK, (n + 1) * GMLP_BLOCK)
            for gi in range(G):
                cols = slice(gi * gd, (gi + 1) * gd)
                vblk = vn[rows, cols]
                s = _dot(ws_ref[gi], vblk) + b_ref[:, gi:gi + 1]
                dgb = dg[rows, cols]
                ds = dgb * u[rows, cols]
                dsb = ds.astype(BF16)
                dz_ref[rows, cols] = (dgb * s * _gelu_grad(zf[rows, cols].astype(F32))).astype(BF16)
                dvn_ref[rows, cols] = _dot_tn(ws_ref[gi], dsb)
                dws_ref[gi] += _dot_nt(dsb, vblk) * mask_ref[...]
                db_ref[:, gi:gi + 1] += jnp.sum(ds, axis=1, keepdims=True)
        dvn = dvn_ref[...]
        dgv_ref[0:1, :] += jnp.sum(dvn * vh, axis=0, keepdims=True)
        dn = dvn * gv_ref[...]
        dv = r * (dn - vh * jnp.mean(dn * vh, axis=-1, keepdims=True))
        dz_ref[:, D:] = (dv * _gelu_grad(zf[:, D:].astype(F32))).astype(BF16)

    return _call(
        body, name=name, grid=(T // tb,),
        in_specs=[pl.BlockSpec((tb, D2), lambda i: (i, 0)), pl.BlockSpec((tb, D), lambda i: (i, 0)),
                  pl.BlockSpec(ws.shape, lambda i: (0, 0, 0)), pl.BlockSpec(bst.shape, lambda i: (0, 0)),
                  pl.BlockSpec((1, D), lambda i: (0, 0)), pl.BlockSpec(mask.shape, lambda i: (0, 0))],
        out_specs=[pl.BlockSpec((tb, D2), lambda i: (i, 0)), pl.BlockSpec(ws.shape, lambda i: (0, 0, 0)),
                   pl.BlockSpec(bst.shape, lambda i: (0, 0)), pl.BlockSpec((8, D), lambda i: (0, 0))],
        out_shape=[jax.ShapeDtypeStruct((T, D2), BF16), jax.ShapeDtypeStruct(ws.shape, F32),
                   jax.ShapeDtypeStruct(bst.shape, F32), jax.ShapeDtypeStruct((8, D), F32)],
        scratch_shapes=[pltpu.VMEM((tb, D), F32)], args=(z, dgated, ws, bst, gv, mask), rider=rider)


def _shift_rows(x, k):
    return pltpu.roll(x, k % x.shape[0], axis=0)


def _conv3(ext, cw):
    return (cw[0:1] * _shift_rows(ext, 2)[8:] + cw[1:2] * _shift_rows(ext, 1)[8:] + cw[2:3] * ext[8:])


def _ffn_forward(a, cw, cb, wd, h, seq, *, name, rider=None):
    _, T, F = a.shape
    D = h.shape[1]
    tm = _tile(seq, TOKEN_TILE)
    hb = tm // 16

    def body(a_ref, ap_ref, cw_ref, cb_ref, wd_ref, h_ref, act_ref, c_ref, o_ref, acc_ref):
        i, j = pl.program_id(0), pl.program_id(1)
        keep = ((i * tm) % seq != 0).astype(F32)

        def conv(b):
            ext = jnp.concatenate([ap_ref[b, 8:16].astype(F32) * keep, a_ref[b].astype(F32)], axis=0)
            return _conv3(ext, cw_ref[b]) + cb_ref[b]

        up, gate = conv(j), conv(j + 4)
        c_ref[j] = up.astype(BF16)
        c_ref[j + 4] = gate.astype(BF16)
        act = (gate * _sigmoid(gate) * up).astype(BF16)
        act_ref[0] = act

        @pl.when(j == 0)
        def _():
            acc_ref[...] = h_ref[...]

        acc_ref[...] += _dot(act, wd_ref[0])

        @pl.when(j == 3)
        def _():
            o_ref[...] = acc_ref[...]

    return _call(
        body, name=name, grid=(T // tm, 4),
        in_specs=[pl.BlockSpec((8, tm, F), lambda i, j: (0, i, 0)),
                  pl.BlockSpec((8, 16, F), lambda i, j: (0, jnp.maximum(i * hb - 1, 0), 0)),
                  pl.BlockSpec((8, 3, F), lambda i, j: (0, 0, 0)), pl.BlockSpec((8, 1, F), lambda i, j: (0, 0, 0)),
                  pl.BlockSpec((1, F, D), lambda i, j: (j, 0, 0)), pl.BlockSpec((tm, D), lambda i, j: (i, 0))],
        out_specs=[pl.BlockSpec((1, tm, F), lambda i, j: (j, i, 0)), pl.BlockSpec((8, tm, F), lambda i, j: (0, i, 0)),
                   pl.BlockSpec((tm, D), lambda i, j: (i, 0))],
        out_shape=[jax.ShapeDtypeStruct((4, T, F), BF16), jax.ShapeDtypeStruct((8, T, F), BF16),
                   jax.ShapeDtypeStruct((T, D), F32)],
        scratch_shapes=[pltpu.VMEM((tm, D), F32)], args=(a, a, cw, cb, wd, h), rider=rider)


def _ffn_backward(dh, c, a, cw, wd, seq, *, name, rider=None):
    _, T, F = a.shape
    D = dh.shape[1]
    tm = _tile(seq, TOKEN_TILE)
    hb = tm // 16
    nt = T // tm

    def body(dh_ref, dhn_ref, cu_ref, cg_ref, cun_ref, cgn_ref, au_ref, ag_ref, cw_ref, wd_ref, da_ref, st_ref):
        i, j = pl.program_id(0), pl.program_id(1)
        keep_next = (((i + 1) * tm) % seq != 0).astype(F32)

        @pl.when((i == 0) & (j == 0))
        def _():
            st_ref[...] = jnp.zeros_like(st_ref)

        dhe = jnp.concatenate([dh_ref[...], dhn_ref[...] * keep_next], axis=0).astype(BF16)
        dact = _dot_nt(dhe, wd_ref[0])
        up = jnp.concatenate([cu_ref[0].astype(F32), cun_ref[0, 0:8].astype(F32)], axis=0)
        gate = jnp.concatenate([cg_ref[0].astype(F32), cgn_ref[0, 0:8].astype(F32)], axis=0)
        sg = _sigmoid(gate)
        gs = gate * sg
        d_up = dact * gs
        d_gate = dact * up * (sg + gs * (1.0 - sg))

        def finish(b, a_ref, dc):
            w = cw_ref[b]
            dm, u1, u2 = dc[:tm], _shift_rows(dc, -1)[:tm], _shift_rows(dc, -2)[:tm]
            da_ref[b] = (w[2:3] * dm + w[1:2] * u1 + w[0:1] * u2).astype(BF16)
            av = a_ref[0].astype(F32)
            st_ref[b, 0:1, :] += jnp.sum(u2 * av, axis=0, keepdims=True)
            st_ref[b, 1:2, :] += jnp.sum(u1 * av, axis=0, keepdims=True)
            st_ref[b, 2:3, :] += jnp.sum(dm * av, axis=0, keepdims=True)
            st_ref[b, 3:4, :] += jnp.sum(dm, axis=0, keepdims=True)

        finish(j, au_ref, d_up)
        finish(j + 4, ag_ref, d_gate)

    nxt = lambda i: jnp.minimum((i + 1) * hb, T // 16 - 1)
    return _call(
        body, name=name, grid=(nt, 4),
        in_specs=[pl.BlockSpec((tm, D), lambda i, j: (i, 0)),
                  pl.BlockSpec((8, D), lambda i, j: (jnp.minimum((i + 1) * (tm // 8), T // 8 - 1), 0)),
                  pl.BlockSpec((1, tm, F), lambda i, j: (j, i, 0)), pl.BlockSpec((1, tm, F), lambda i, j: (j + 4, i, 0)),
                  pl.BlockSpec((1, 16, F), lambda i, j: (j, nxt(i), 0)),
                  pl.BlockSpec((1, 16, F), lambda i, j: (j + 4, nxt(i), 0)),
                  pl.BlockSpec((1, tm, F), lambda i, j: (j, i, 0)), pl.BlockSpec((1, tm, F), lambda i, j: (j + 4, i, 0)),
                  pl.BlockSpec((8, 3, F), lambda i, j: (0, 0, 0)),
                  pl.BlockSpec((1, F, D), lambda i, j: (j, 0, 0))],
        out_specs=[pl.BlockSpec((8, tm, F), lambda i, j: (0, i, 0)), pl.BlockSpec((8, 8, F), lambda i, j: (0, 0, 0))],
        out_shape=[jax.ShapeDtypeStruct((8, T, F), BF16), jax.ShapeDtypeStruct((8, 8, F), F32)],
        args=(dh, dh, c, c, c, c, a, a, cw, wd), rider=rider)


def _rel_onehot():
    r = lax.broadcasted_iota(jnp.int32, (REL_PAD, SKEW), 0)
    n = lax.broadcasted_iota(jnp.int32, (REL_PAD, SKEW), 1)
    off = jnp.where(n >= WIN, n - SKEW, n)
    idx = jnp.minimum(PAD - off, REL_CLIP) + REL_CLIP
    return (r == idx).astype(BF16)


def _skew(x, sign):
    row = lax.broadcasted_iota(jnp.int32, x.shape, 0)
    for b in range(7):
        x = jnp.where((row >> b) & 1 == 1, pltpu.roll(x, (sign * (1 << b)) % SKEW, axis=1), x)
    return x


def _bias_build(rel, name, rider=None):
    H = rel.shape[0]

    def body(rel_ref, o_ref):
        oh = _rel_onehot()
        hi, mid, lo = _split3(rel_ref[...])
        base = _dot(hi, oh) + _dot(mid, oh) + _dot(lo, oh)
        mine = lax.broadcasted_iota(jnp.int32, (H, 1), 0) == pl.program_id(0)
        row = jnp.sum(jnp.where(mine, base, 0.0), axis=0, keepdims=True)
        q = lax.broadcasted_iota(jnp.int32, (Q_TILE, WIN), 0)
        k = lax.broadcasted_iota(jnp.int32, (Q_TILE, WIN), 1)
        ok = ((q < CHUNK) & (k < WIN - CHUNK)) | ((q >= CHUNK) & (k >= CHUNK))
        t = _skew(jnp.broadcast_to(row, (Q_TILE, SKEW)), 1)
        o_ref[0] = jnp.where(ok, t[:, :WIN], NEG_INF)

    return _call(
        body, name=name, grid=(H,), in_specs=[pl.BlockSpec((H, REL_PAD), lambda h: (0, 0))],
        out_specs=pl.BlockSpec((1, Q_TILE, WIN), lambda h: (h, 0, 0)),
        out_shape=jax.ShapeDtypeStruct((H, Q_TILE, WIN), F32), args=(rel,), rider=rider)


def _bias_reduce(dbias, name):
    H = dbias.shape[0]

    def body(d_ref, o_ref, e_ref):
        oh = _rel_onehot()
        for hd in range(H):
            x = jnp.concatenate([d_ref[hd], jnp.zeros((Q_TILE, SKEW - WIN), F32)], axis=1)
            e_ref[hd:hd + 1, :] = jnp.sum(_skew(x, -1), axis=0, keepdims=True)
        hi, mid, lo = _split3(e_ref[...])
        o_ref[...] = _dot_nt(hi, oh) + _dot_nt(mid, oh) + _dot_nt(lo, oh)

    return pl.pallas_call(
        body, name=name, out_shape=jax.ShapeDtypeStruct((H, REL_PAD), F32),
        in_specs=[pl.BlockSpec(memory_space=pltpu.VMEM)], out_specs=pl.BlockSpec(memory_space=pltpu.VMEM),
        scratch_shapes=[pltpu.VMEM((H, SKEW), F32)],
        compiler_params=_params(),
    )(dbias)


def _pair_stack(xp, even):
    z = jnp.zeros_like(xp)
    return jnp.concatenate([jnp.where(even, xp, z), jnp.where(even, z, xp)], axis=0)


def _pair_merge(y, even):
    return jnp.where(even, y[:Q_TILE], y[Q_TILE:])


def _strip_probs(s_ref, b_ref, pp, r, valid):
    hb, hr = divmod(r, Q_TILE)
    s = s_ref[pp, r:r + STRIP, :] + b_ref[2 * pp + hb, hr:hr + STRIP, :]
    s = jnp.where(valid, s, NEG_INF)
    e = jnp.exp(s - jnp.max(s, axis=-1, keepdims=True))
    return e * (1.0 / jnp.sum(e, axis=-1, keepdims=True))


def _fill_padded(dst_ref, src_ref):
    dst_ref[0:PAD, :] = jnp.zeros((PAD, dst_ref.shape[1]), dst_ref.dtype)
    dst_ref[PAD:, :] = src_ref[...]


def _attn_specs(B, S, D, lanes):
    nt = S // Q_TILE
    q_spec = pl.BlockSpec((Q_TILE, lanes), lambda g, b, i: (b * nt + i, g))
    k_spec = pl.BlockSpec((S, lanes), lambda g, b, i: (b, g))
    v_spec = pl.BlockSpec((S, lanes), lambda g, b, i: (b, D // lanes + g))
    bias_spec = pl.BlockSpec((lanes // HEAD_DIM, Q_TILE, WIN), lambda g, b, i: (g, 0, 0))
    return nt, q_spec, k_spec, v_spec, bias_spec


def _attn_forward(q, kv, bias, S, *, name):
    T, D = q.shape
    B = T // S
    lanes = min(FWD_HEADS_PER_STEP * HEAD_DIM, D)
    nt, q_spec, k_spec, v_spec, bias_spec = _attn_specs(B, S, D, lanes)

    npairs = lanes // (2 * HEAD_DIM)

    def body(q_ref, k_ref, v_ref, b_ref, o_ref, kp_ref, vp_ref, s_ref, p_ref):
        i = pl.program_id(2)

        @pl.when(i == 0)
        def _():
            _fill_padded(kp_ref, k_ref)
            _fill_padded(vp_ref, v_ref)

        start = pl.multiple_of(i * Q_TILE, Q_TILE)
        even = lax.broadcasted_iota(jnp.int32, (1, 2 * HEAD_DIM), 1) < HEAD_DIM
        valid = lax.broadcasted_iota(jnp.int32, (STRIP, WIN), 1) >= PAD - i * Q_TILE
        pair_cols = [slice(pp * 2 * HEAD_DIM, (pp + 1) * 2 * HEAD_DIM) for pp in range(npairs)]
        for pp, cols in enumerate(pair_cols):
            s_ref[pp] = _dot_nt(_pair_stack(q_ref[:, cols], even), kp_ref[pl.ds(start, WIN), cols])
        for pp in range(npairs):
            for r in range(0, 2 * Q_TILE, STRIP):
                p = _strip_probs(s_ref, b_ref, pp, r, valid)
                p_ref[pp, r:r + STRIP, :] = p.astype(BF16)
        for pp, cols in enumerate(pair_cols):
            o_ref[:, cols] = _pair_merge(_dot(p_ref[pp], vp_ref[pl.ds(start, WIN), cols]), even).astype(BF16)

    return pl.pallas_call(
        body, name=name, grid=(D // lanes, B, nt),
        in_specs=[q_spec, k_spec, v_spec, bias_spec], out_specs=q_spec,
        out_shape=jax.ShapeDtypeStruct((T, D), BF16),
        scratch_shapes=[pltpu.VMEM((S + PAD, lanes), BF16), pltpu.VMEM((S + PAD, lanes), BF16),
                        pltpu.VMEM((npairs, 2 * Q_TILE, WIN), F32), pltpu.VMEM((npairs, 2 * Q_TILE, WIN), BF16)],
        compiler_params=_params(("arbitrary", "arbitrary", "arbitrary")),
    )(q, kv, kv, bias)


def _attn_backward(q, kv, bias, do, S, *, name, rider=None):
    T, D = q.shape
    B = T // S
    H = D // HEAD_DIM
    lanes = min(BWD_HEADS_PER_STEP * HEAD_DIM, D)
    nt, q_spec, k_spec, v_spec, bias_spec = _attn_specs(B, S, D, lanes)
    scale = HEAD_DIM ** -0.5

    npairs = lanes // (2 * HEAD_DIM)

    def body(q_ref, k_ref, v_ref, b_ref, do_ref, dq_ref, dk_ref, dv_ref, db_ref, kp_ref, vp_ref, dka_ref, dva_ref,
             s_ref, dp_ref, p_ref, ds_ref):
        b, i = pl.program_id(1), pl.program_id(2)

        @pl.when((b == 0) & (i == 0))
        def _():
            db_ref[...] = jnp.zeros_like(db_ref)

        @pl.when(i == 0)
        def _():
            _fill_padded(kp_ref, k_ref)
            _fill_padded(vp_ref, v_ref)
            dka_ref[...] = jnp.zeros_like(dka_ref)
            dva_ref[...] = jnp.zeros_like(dva_ref)

        start = pl.multiple_of(i * Q_TILE, Q_TILE)
        even = lax.broadcasted_iota(jnp.int32, (1, 2 * HEAD_DIM), 1) < HEAD_DIM
        valid = lax.broadcasted_iota(jnp.int32, (STRIP, WIN), 1) >= PAD - i * Q_TILE
        pair_cols = [slice(pp * 2 * HEAD_DIM, (pp + 1) * 2 * HEAD_DIM) for pp in range(npairs)]
        for pp, cols in enumerate(pair_cols):
            s_ref[pp] = _dot_nt(_pair_stack(q_ref[:, cols], even), kp_ref[pl.ds(start, WIN), cols])
            dp_ref[pp] = _dot_nt(_pair_stack(do_ref[:, cols], even), vp_ref[pl.ds(start, WIN), cols])
        for pp in range(npairs):
            for r in range(0, 2 * Q_TILE, STRIP):
                hb, hr = divmod(r, Q_TILE)
                p = _strip_probs(s_ref, b_ref, pp, r, valid)
                dp = dp_ref[pp, r:r + STRIP, :]
                ds = p * (dp - jnp.sum(p * dp, axis=-1, keepdims=True))
                db_ref[2 * pp + hb, hr:hr + STRIP, :] += ds
                p_ref[pp, r:r + STRIP, :] = p.astype(BF16)
                ds_ref[pp, r:r + STRIP, :] = ds.astype(BF16)
        for pp, cols in enumerate(pair_cols):
            dsb = ds_ref[pp]
            dq_ref[:, cols] = (_pair_merge(_dot(dsb, kp_ref[pl.ds(start, WIN), cols]), even) * scale).astype(BF16)
            dka_ref[pl.ds(start, WIN), cols] += _dot_tn(dsb, _pair_stack(q_ref[:, cols], even))
            dva_ref[pl.ds(start, WIN), cols] += _dot_tn(p_ref[pp], _pair_stack(do_ref[:, cols], even))

        @pl.when(i == nt - 1)
        def _():
            dk_ref[...] = dka_ref[PAD:, :].astype(BF16)
            dv_ref[...] = dva_ref[PAD:, :].astype(BF16)

    dkv_shape = jax.ShapeDtypeStruct((T, D), BF16)
    return _call(
        body, name=name, grid=(D // lanes, B, nt),
        in_specs=[q_spec, k_spec, v_spec, bias_spec, q_spec],
        out_specs=[q_spec, k_spec, k_spec, bias_spec],
        out_shape=[jax.ShapeDtypeStruct((T, D), BF16), dkv_shape, dkv_shape,
                   jax.ShapeDtypeStruct((H, Q_TILE, WIN), F32)],
        scratch_shapes=[pltpu.VMEM((S + PAD, lanes), BF16), pltpu.VMEM((S + PAD, lanes), BF16),
                        pltpu.VMEM((S + PAD, lanes), F32), pltpu.VMEM((S + PAD, lanes), F32),
                        pltpu.VMEM((npairs, 2 * Q_TILE, WIN), F32), pltpu.VMEM((npairs, 2 * Q_TILE, WIN), F32),
                        pltpu.VMEM((npairs, 2 * Q_TILE, WIN), BF16), pltpu.VMEM((npairs, 2 * Q_TILE, WIN), BF16)],
        args=(q, kv, kv, bias, do), rider=rider)


def _loss_head(h, g, target, name):
    T, D = h.shape
    tm = _tile(T, TOKEN_TILE)

    def body(h_ref, g_ref, t_ref, dh_ref, st_ref):
        @pl.when(pl.program_id(0) == 0)
        def _():
            st_ref[...] = jnp.zeros_like(st_ref)

        xh, r = _rms(h_ref[...], None)
        err = xh * g_ref[...] - t_ref[...]
        st_ref[1:2, :] += 0.5 * jnp.sum(jnp.mean(err * err, axis=-1, keepdims=True), axis=0, keepdims=True)
        dy = err * (1.0 / D)
        st_ref[0:1, :] += jnp.sum(dy * xh, axis=0, keepdims=True)
        dn = dy * g_ref[...]
        dh_ref[...] = r * (dn - xh * jnp.mean(dn * xh, axis=-1, keepdims=True))

    row = pl.BlockSpec((tm, D), lambda i: (i, 0))
    return pl.pallas_call(
        body, name=name, grid=(T // tm,),
        in_specs=[row, pl.BlockSpec((1, D), lambda i: (0, 0)), row],
        out_specs=[row, pl.BlockSpec((8, D), lambda i: (0, 0))],
        out_shape=[jax.ShapeDtypeStruct((T, D), F32), jax.ShapeDtypeStruct((8, D), F32)],
        compiler_params=_params(("arbitrary",)),
    )(h, g, target)


def _sum_devices(arrs, name):
    n = len(arrs)

    def body(*refs):
        for a in range(n):
            s = refs[a][0]
            for k in range(1, N_DEV):
                s = s + refs[a][k]
            refs[n + a][...] = s

    vm = pl.BlockSpec(memory_space=pltpu.VMEM)
    return pl.pallas_call(
        body, name=name, out_shape=[jax.ShapeDtypeStruct(a.shape[1:], F32) for a in arrs],
        in_specs=[vm] * n, out_specs=[vm] * n, compiler_params=_params(),
    )(*arrs)


def _adamw_math(w, g, m, v):
    m = ADAM_B1 * m + (1.0 - ADAM_B1) * g
    v = ADAM_B2 * v + (1.0 - ADAM_B2) * (g * g)
    m_hat = m / (1.0 - ADAM_B1 ** ADAM_STEP)
    v_hat = v / (1.0 - ADAM_B2 ** ADAM_STEP)
    delta = -ADAM_LR * (m_hat / (jnp.sqrt(v_hat) + ADAM_EPS) + ADAM_WD * w)
    return delta, m, v


def _adamw_small(items, name):
    n = len(items)

    def body(*refs):
        for a in range(n):
            w, m, v, g = (refs[4 * a + k][...] for k in range(4))
            d, m, v = _adamw_math(w, g, m, v)
            refs[4 * n + 3 * a][...] = d
            refs[4 * n + 3 * a + 1][...] = m
            refs[4 * n + 3 * a + 2][...] = v

    vm = pl.BlockSpec(memory_space=pltpu.VMEM)
    flat = [t for it in items for t in it]
    outs = pl.pallas_call(
        body, name=name,
        out_shape=[jax.ShapeDtypeStruct(it[0].shape, F32) for it in items for _ in range(3)],
        in_specs=[vm] * (4 * n), out_specs=[vm] * (3 * n), compiler_params=_params(),
    )(*flat)
    return [tuple(outs[3 * a:3 * a + 3]) for a in range(n)]


def _adamw_big(w, m, v, owns, landeds, name, rider=None):
    L, R, C = w.shape
    tr = _tile(R, 512)
    nr = R // tr
    counts = [len(ls) for ls in landeds]

    def body(*refs):
        w_ref, m_ref, v_ref = refs[:3]
        g_ref, d_ref, mo_ref, vo_ref = refs[-4:]
        layer = pl.program_id(0)
        at = 3
        for j in range(L):
            own_ref, l_refs = refs[at], refs[at + 1:at + 1 + counts[j]]
            at += 1 + counts[j]

            @pl.when(layer == j)
            def _(own_ref=own_ref, l_refs=l_refs):
                g = own_ref[...]
                for l_ref in l_refs:
                    for k in range(l_ref.shape[0]):
                        g = g + l_ref[k].astype(F32)
                d, mn, vn = _adamw_math(w_ref[0], g, m_ref[0], v_ref[0])
                g_ref[0] = g
                d_ref[0] = d
                mo_ref[0] = mn
                vo_ref[0] = vn

    def pinned(j):
        return lambda l, i: jnp.where(l == j, i, jnp.where(l < j, 0, nr - 1))

    row = pl.BlockSpec((1, tr, C), lambda l, i: (l, i, 0))
    in_specs, args = [row, row, row], [w, m, v]
    for j in range(L):
        in_specs.append(pl.BlockSpec((tr, C), lambda l, i, p=pinned(j): (p(l, i), 0)))
        args.append(owns[j])
        for arr in landeds[j]:
            in_specs.append(pl.BlockSpec((arr.shape[0], tr, C), lambda l, i, p=pinned(j): (0, p(l, i), 0)))
            args.append(arr)
    return _call(body, name=name, grid=(L, nr), in_specs=in_specs, out_specs=[row] * 4,
                 out_shape=[jax.ShapeDtypeStruct((L, R, C), F32)] * 4, args=args, rider=rider)


def kernel(x, a_norm_g, a_w_in, a_v_norm_g, a_w_s, a_b_s, a_w_out, kv_norm_g, w_kv, b_norm_g, b_w_q, b_rel_bias, b_w_o, f_norm_g, f_w_in, f_conv_w, f_conv_b, f_w_down, final_norm_g, loss_target, m_a_norm_g, m_a_w_in, m_a_v_norm_g, m_a_w_s, m_a_b_s, m_a_w_out, m_kv_norm_g, m_w_kv, m_b_norm_g, m_b_w_q, m_b_rel_bias, m_b_w_o, m_f_norm_g, m_f_w_in, m_f_conv_w, m_f_conv_b, m_f_w_down, m_final_norm_g, v_a_norm_g, v_a_w_in, v_a_v_norm_g, v_a_w_s, v_a_b_s, v_a_w_out, v_kv_norm_g, v_w_kv, v_b_norm_g, v_b_w_q, v_b_rel_bias, v_b_w_o, v_f_norm_g, v_f_w_in, v_f_conv_w, v_f_conv_b, v_f_w_down, v_final_norm_g):
    B, S, D = x.shape
    T = B * S
    G = a_w_s.shape[1]
    H = D // HEAD_DIM
    F = f_w_in.shape[2]
    L = f_w_in.shape[0]
    dn = D // N_DEV
    xi, yi, ci = lax.axis_index("x"), lax.axis_index("y"), lax.axis_index("c")
    me = 4 * xi + 2 * yi + ci
    pos = jnp.stack([ci, 2 * xi + yi]).astype(jnp.int32)

    cast = lambda t: t.astype(BF16)
    gather = lambda *ts: _gather_rider(list(ts))
    rel = jnp.pad(b_rel_bias[0], ((0, 0), (0, REL_PAD - b_rel_bias.shape[2])))
    bias, (wa_in, norms_sh, conv_w0, conv_w1) = _bias_build(rel, "bias_build", rider=gather(
        cast(a_w_in[0]), jnp.concatenate([a_norm_g, a_v_norm_g], axis=0), f_conv_w[0], f_conv_w[1]))
    ga = jnp.transpose(norms_sh, (1, 0, 2)).reshape(2, D)
    g_a, g_av = ga[0:1], ga[1:2]

    x2 = x.reshape(T, D)
    tgt = loss_target.reshape(T, D)
    pc = jnp.arange(GMLP_BLOCK) // CHUNK
    mask = (pc[:, None] >= pc[None, :]).astype(F32)
    ws = (a_w_s[0] * mask[None]).astype(BF16)
    bst = jnp.transpose(a_b_s[0])
    four = lambda t: t.reshape((4, 2) + t.shape[1:])

    w_in0_sh = cast(f_w_in[0])
    (z, n_a), (wa_out, w_in0_top) = _norm_matmul(x2, g_a, wa_in, flat=True, nbk=4, name="gmlp_in",
                                                 rider=gather(cast(a_w_out[0]), w_in0_sh[:D // 2]))
    wa_out = wa_out.reshape(D, D)
    (gated, h1), (w_in0_bottom,) = _gmlp_forward(z, ws, bst, g_av, wa_out, x2, name="gmlp_mix",
                                                 rider=gather(w_in0_sh[D // 2:]))
    w_in0 = jnp.concatenate([w_in0_top, w_in0_bottom], axis=1)
    (a0, n_f0), (wf_down0, wkv, wq) = _norm_matmul(h1, f_norm_g[0:1], w_in0, flat=False, nbk=2, name="ffn0_in",
                                                   rider=gather(cast(f_w_down[0]), cast(w_kv), cast(b_w_q[0])))
    cw0, cb0, wd0 = conv_w0, f_conv_b[0].reshape(8, 1, F), wf_down0.reshape(4, F, D)
    wq = wq.reshape(D, D)
    (act0, c0, h2), (w_in1, wo) = _ffn_forward(a0, cw0, cb0, wd0, h1, S, name="ffn0_out",
                                               rider=gather(cast(f_w_in[1]), cast(b_w_o[0])))
    wo = wo.reshape(D, D)
    (kv, n_kv), (wf_down1,) = _norm_matmul(h2, kv_norm_g.reshape(1, D), wkv, flat=True, nbk=4, name="kv_proj",
                                           rider=gather(cast(f_w_down[1])))
    cw1, cb1, wd1 = conv_w1, f_conv_b[1].reshape(8, 1, F), wf_down1.reshape(4, F, D)
    q, n_q = _norm_matmul(h2, b_norm_g, wq.reshape(1, D, D), flat=True, nbk=1, name="q_proj", scale=HEAD_DIM ** -0.5)
    o = _attn_forward(q, kv, bias, S, name="attn")
    h3 = _matmul_residual(o, wo, h2, "attn_out")
    a1, n_f1 = _norm_matmul(h3, f_norm_g[1:2], w_in1, flat=False, nbk=2, name="ffn1_in")
    act1, c1, h4 = _ffn_forward(a1, cw1, cb1, wd1, h3, S, name="ffn1_out")

    sums, from_chips = {}, {}

    def sibling_sums(names, parts, landed):
        for nm, p, l in zip(names, parts, landed):
            sums[nm] = _sibling_sum(p, l, pos, "grad_sibling_sum_" + nm)

    def chip_rider(*names):
        return _chip_rider([sums[nm][1] for nm in names])

    dh4, st_final = _loss_head(h4, final_norm_g.reshape(1, D), tgt, "loss_head")
    da1, st_conv1 = _ffn_backward(dh4, c1, a1, cw1, wd1, S, name="ffn1_bwd")
    g_wd1 = _wgrad_rows(act1, dh4, flat=False, tk=F, name="ffn1_dwdown")
    g_win1 = _wgrad_cols(n_f1, da1, flat=False, nb=8, nbk=2, name="ffn1_dwin")
    parts = [four(g_wd1.reshape(8, F // 2, D)), four(g_win1)]
    (dh3, st_f1), landed = _matmul_nt(da1, w_in1, flat=False, nbk=4, name="ffn1_dx", norm=(h3, f_norm_g[1:2], dh4),
                                      rider=_sibling_rider(parts))
    sibling_sums(["wd1", "win1"], parts, landed)
    d_o = _matmul_nt(dh3, wo.reshape(1, D, D), flat=True, nbk=1, name="attn_out_dx")
    g_wo = _wgrad_rows(o, dh3, flat=True, tk=_tile(D, 512), name="attn_out_dw")
    (dq, dk, dv, dbias), (from_chips["wd1"], from_chips["win1"]) = _attn_backward(
        q, kv, bias, d_o, S, name="attn_bwd", rider=chip_rider("wd1", "win1"))
    g_rel = _bias_reduce(dbias, "bias_reduce")
    g_wq = _wgrad_cols(n_q, dq, flat=True, nb=1, nbk=1, name="q_dw")
    dh2, st_b = _matmul_nt(dq, wq.reshape(1, D, D), flat=True, nbk=1, name="q_dx", norm=(h2, b_norm_g, dh3))
    dkv = jnp.concatenate([dk, dv], axis=-1)
    g_wkv = _wgrad_cols(n_kv, dkv, flat=True, nb=8, nbk=4, name="kv_dw")
    parts = [four(g_wo.reshape(8, dn, D)), four(g_wq.reshape(8, dn, D)), four(g_wkv)]
    (dh2, st_kv), landed = _matmul_nt(dkv, wkv, flat=True, nbk=4, name="kv_dx",
                                      norm=(h2, kv_norm_g.reshape(1, D), dh2), rider=_sibling_rider(parts))
    sibling_sums(["wo", "wq", "wkv"], parts, landed)
    (da0, st_conv0), (from_chips["wo"], from_chips["wq"], from_chips["wkv"]) = _ffn_backward(
        dh2, c0, a0, cw0, wd0, S, name="ffn0_bwd", rider=chip_rider("wo", "wq", "wkv"))
    g_wd0 = _wgrad_rows(act0, dh2, flat=False, tk=F, name="ffn0_dwdown")
    g_win0 = _wgrad_cols(n_f0, da0, flat=False, nb=8, nbk=2, name="ffn0_dwin")
    parts = [four(g_wd0.reshape(8, F // 2, D)), four(g_win0)]
    (dh1, st_f0), landed = _matmul_nt(da0, w_in0, flat=False, nbk=4, name="ffn0_dx", norm=(h1, f_norm_g[0:1], dh2),
                                      rider=_sibling_rider(parts))
    sibling_sums(["wd0", "win0"], parts, landed)
    dgated = _matmul_nt(dh1, wa_out.reshape(1, D, D), flat=True, nbk=1, name="gmlp_out_dx")
    g_wa_out, (ce,) = _wgrad_rows(gated, dh1, flat=True, tk=_tile(D, 512), name="gmlp_out_dw",
                                  rider=chip_rider("wd0"))
    from_chips["wd0"] = [ce]
    parts = [four(g_wa_out.reshape(8, dn, D))]
    (dz, g_ws, g_bst, st_av), (ce_win0_a, landed) = _gmlp_backward(
        z, dgated, ws, bst, g_av, mask, name="gmlp_bwd",
        rider=_join_riders([_chip_rider([sums["win0"][1]], ks=(1, 2)), _sibling_rider(parts)]))
    sibling_sums(["wa_out"], parts, [landed])
    g_wa_in, (ce_win0_b, ce) = _wgrad_cols(n_a, dz, flat=True, nb=8, nbk=4, name="gmlp_in_dw", rider=_join_riders(
        [_chip_rider([sums["win0"][1]], ks=(3,)), chip_rider("wa_out")]))
    from_chips["win0"], from_chips["wa_out"] = [ce_win0_a, ce_win0_b], [ce]
    vec = jnp.concatenate([st_av[0:1], st_kv[0:1], st_b[0:1], st_f0[0:1], st_f1[0:1], st_final[0:3]], axis=0)
    parts = [four(g_wa_in)]
    (grad_x, st_a), got = _matmul_nt(dz, wa_in, flat=True, nbk=4, name="gmlp_in_dx", norm=(x2, g_a, dh1),
                                     rider=_join_riders([_sibling_rider(parts),
                                                         gather(vec, g_ws, g_bst, g_rel, st_conv0, st_conv1)]))
    sibling_sums(["wa_in"], parts, got[0:1])
    small = got[1:]

    def big_update(names, w, m, v, rider=None):
        shape = w.shape
        r = lambda t: t.reshape((len(names), -1, shape[-1]))
        as_list = lambda t: t if isinstance(t, list) else [t]
        outs = _adamw_big(r(w), r(m), r(v), [sums[nm][0] for nm in names],
                          [as_list(from_chips[nm]) for nm in names], "adamw_" + names[0], rider=rider)
        outs, got = (outs, None) if rider is None else outs
        return [t.reshape(shape) for t in outs], got

    u_f_w_in, (ce, st_a) = big_update(["win0", "win1"], f_w_in, m_f_w_in, v_f_w_in,
                                      rider=_join_riders([chip_rider("wa_in"), gather(st_a)]))
    from_chips["wa_in"] = [ce]
    u_f_w_down, _ = big_update(["wd0", "wd1"], f_w_down, m_f_w_down, v_f_w_down)
    u_a_w_in, _ = big_update(["wa_in"], a_w_in, m_a_w_in, v_a_w_in)
    u_w_kv, _ = big_update(["wkv"], w_kv, m_w_kv, v_w_kv)
    u_a_w_out, _ = big_update(["wa_out"], a_w_out, m_a_w_out, v_a_w_out)
    u_b_w_q, _ = big_update(["wq"], b_w_q, m_b_w_q, v_b_w_q)
    u_b_w_o, _ = big_update(["wo"], b_w_o, m_b_w_o, v_b_w_o)

    vec, g_ws, g_bst, g_rel, st_conv0, st_conv1, st_a = _sum_devices(list(small) + [st_a], "sum_small_grads")
    vec = jnp.concatenate([st_a[0:1], vec[0:7]], axis=0)
    loss = vec[7, 0]
    g_a_norm = lax.dynamic_slice_in_dim(vec[0:1], me * dn, dn, axis=1)
    g_av_norm = lax.dynamic_slice_in_dim(vec[1:2], me * dn, dn, axis=1)
    st_conv = jnp.stack([st_conv0, st_conv1])
    g_conv_w = lax.dynamic_index_in_dim(st_conv, me, axis=1, keepdims=False)[:, 0:3]
    g_conv_b = st_conv[:, :, 3, :].reshape(L, 8 * F)
    small_items = [
        (a_norm_g, m_a_norm_g, v_a_norm_g, g_a_norm),
        (a_v_norm_g, m_a_v_norm_g, v_a_v_norm_g, g_av_norm),
        (a_w_s, m_a_w_s, v_a_w_s, g_ws[None]),
        (a_b_s, m_a_b_s, v_a_b_s, jnp.transpose(g_bst)[None]),
        (kv_norm_g.reshape(1, D), m_kv_norm_g.reshape(1, D), v_kv_norm_g.reshape(1, D), vec[2:3]),
        (b_norm_g, m_b_norm_g, v_b_norm_g, vec[3:4]),
        (b_rel_bias, m_b_rel_bias, v_b_rel_bias, g_rel[None, :, :b_rel_bias.shape[2]]),
        (f_norm_g, m_f_norm_g, v_f_norm_g, vec[4:6]),
        (f_conv_w, m_f_conv_w, v_f_conv_w, g_conv_w),
        (f_conv_b, m_f_conv_b, v_f_conv_b, g_conv_b),
        (final_norm_g.reshape(1, D), m_final_norm_g.reshape(1, D), v_final_norm_g.reshape(1, D), vec[6:7]),
    ]
    small_out = _adamw_small(small_items, "adamw_small")
    (u_a_norm, u_av_norm, u_ws, u_bs, u_kvn, u_bn, u_rel, u_fn, u_cw, u_cb, u_fin) = [
        (it[3],) + so for it, so in zip(small_items, small_out)]
    vecD = lambda u: tuple(t.reshape(D) for t in u)
    u_kvn, u_fin = vecD(u_kvn), vecD(u_fin)

    order = [u_a_norm, u_a_w_in, u_av_norm, u_ws, u_bs, u_a_w_out, u_kvn, u_w_kv, u_bn, u_b_w_q, u_rel, u_b_w_o,
             u_fn, u_f_w_in, u_cw, u_cb, u_f_w_down, u_fin]
    outs = [loss, grad_x.reshape(B, S, D)]
    for k in range(4):
        outs += [u[k] for u in order]
    return tuple(outs)
```

```python
import functools

import jax
import jax.numpy as jnp
from jax import lax
from jax.experimental import pallas as pl
from jax.experimental.pallas import tpu as pltpu

F32 = jnp.float32
BF16 = jnp.bfloat16
MESH = pl.DeviceIdType.MESH

N_DEV = 8
EPS = 1e-6
NEG_INF = -1e30
CHUNK = 64
LEFT_CHUNKS = 8
REL_CLIP = 128
HEAD_DIM = 64
GMLP_BLOCK = 128
Q_TILE = 2 * CHUNK
PAD = LEFT_CHUNKS * CHUNK
WIN = PAD + Q_TILE
SKEW = WIN + Q_TILE
REL_PAD = 384
FWD_HEADS_PER_STEP = 8
BWD_HEADS_PER_STEP = 4
STRIP = 32
ADAM_LR, ADAM_B1, ADAM_B2, ADAM_EPS, ADAM_WD, ADAM_STEP = 0.001, 0.9, 0.999, 1e-08, 0.01, 10
VMEM_LIMIT = 56 * 1024 * 1024
TOKEN_TILE = 512


def _params(sem=None):
    return pltpu.CompilerParams(dimension_semantics=sem, vmem_limit_bytes=VMEM_LIMIT)


def _tile(n, pref):
    if n <= pref:
        return n
    for t in range(pref - pref % 8, 7, -8):
        if n % t == 0:
            return t
    return n


def _gelu(x):
    return 0.5 * x * (1.0 + jnp.tanh(0.7978845608028654 * (x + 0.044715 * x * x * x)))


def _gelu_grad(x):
    t = jnp.tanh(0.7978845608028654 * (x + 0.044715 * x * x * x))
    return 0.5 * (1.0 + t) + 0.5 * x * (1.0 - t * t) * 0.7978845608028654 * (1.0 + 3 * 0.044715 * x * x)


def _sigmoid(x):
    return 1.0 / (1.0 + jnp.exp(-x))


def _dot(a, b):
    return jnp.dot(a, b, preferred_element_type=F32)


def _dot_nt(a, b):
    return lax.dot_general(a, b, (((1,), (1,)), ((), ())), preferred_element_type=F32)


def _dot_tn(a, b):
    return lax.dot_general(a, b, (((0,), (0,)), ((), ())), preferred_element_type=F32)


def _split3(x):
    hi = x.astype(BF16)
    r1 = x - hi.astype(F32)
    mid = r1.astype(BF16)
    lo = (r1 - mid.astype(F32)).astype(BF16)
    return hi, mid, lo


def _mesh_pos():
    return lax.axis_index("x"), lax.axis_index("y"), lax.axis_index("c")


class _Rider:
    def __init__(self, arrs, out_shapes, sems, start, finish):
        self.arrs, self.out_shapes, self.sems, self.start, self.finish = arrs, out_shapes, sems, start, finish


def _gather_rider(arrs):
    n = len(arrs)

    def tools(ins, outs, sems):
        send_sems, recv_sems, local_sems = sems
        x, y, c = _mesh_pos()
        me, sibling = (x, y, c), (x, y, 1 - c)
        chips = [(1 - x, y), (x, 1 - y), (1 - x, 1 - y)]

        def slot(a, block):
            px, py, pc = block
            return outs[a].at[4 * px + 2 * py + pc]

        def copy(a, k, block, to, src=None):
            dst = slot(a, block)
            return pltpu.make_async_remote_copy(
                src_ref=dst if src is None else src, dst_ref=dst,
                send_sem=send_sems.at[a, k], recv_sem=recv_sems.at[a, k], device_id=to, device_id_type=MESH)

        def first(a):
            cps = [copy(a, 0, me, sibling, src=ins[a])]
            return cps + [copy(a, 1 + j, me, (*chip, c), src=ins[a]) for j, chip in enumerate(chips)]

        def mine(a):
            return pltpu.make_async_copy(ins[a], slot(a, me), local_sems.at[a])

        return me, sibling, chips, c, copy, first, mine

    def start(ins, outs, sems):
        _, _, _, _, _, first, mine = tools(ins, outs, sems)
        for a in range(n):
            mine(a).start()
            for cp in first(a):
                cp.start()

    def finish(ins, outs, sems):
        me, sibling, chips, c, copy, first, mine = tools(ins, outs, sems)
        passed = []
        for j, chip in enumerate(chips):
            for a in range(n):
                copy(a, 1 + j, (*chip, c), me).wait_recv()
                fwd = copy(a, 4 + j, (*chip, c), sibling)
                fwd.start()
                passed.append(fwd)
        for a in range(n):
            copy(a, 0, sibling, me).wait_recv()
            for j, chip in enumerate(chips):
                copy(a, 4 + j, (*chip, 1 - c), me).wait_recv()
        for a in range(n):
            for cp in first(a):
                cp.wait_send()
        for cp in passed:
            cp.wait_send()
        for a in range(n):
            mine(a).wait()

    return _Rider(list(arrs), [jax.ShapeDtypeStruct((N_DEV,) + a.shape, a.dtype) for a in arrs],
                  [pltpu.SemaphoreType.DMA((n, 7)), pltpu.SemaphoreType.DMA((n, 7)), pltpu.SemaphoreType.DMA((n,))],
                  start, finish)


def _sibling_rider(arrs):
    n = len(arrs)

    def copies(ins, outs, sems):
        x, y, c = _mesh_pos()
        return [pltpu.make_async_remote_copy(
            src_ref=ins[a].at[:, pl.ds(1 - c, 1)], dst_ref=outs[a],
            send_sem=sems[0].at[a], recv_sem=sems[1].at[a], device_id=(x, y, 1 - c), device_id_type=MESH)
            for a in range(n)]

    def start(ins, outs, sems):
        for cp in copies(ins, outs, sems):
            cp.start()

    def finish(ins, outs, sems):
        for cp in copies(ins, outs, sems):
            cp.wait()

    return _Rider(list(arrs), [jax.ShapeDtypeStruct((4, 1) + a.shape[2:], a.dtype) for a in arrs],
                  [pltpu.SemaphoreType.DMA((n,)), pltpu.SemaphoreType.DMA((n,))], start, finish)


def _chip_rider(arrs, ks=(1, 2, 3)):
    n = len(arrs)

    def copies(ins, outs, sems):
        x, y, c = _mesh_pos()
        cps = []
        for a in range(n):
            for s, k in enumerate(ks):
                px = x if k < 2 else 1 - x
                py = y if k == 2 else 1 - y
                cps.append(pltpu.make_async_remote_copy(
                    src_ref=ins[a].at[2 * px + py], dst_ref=outs[a].at[s],
                    send_sem=sems[0].at[a, s], recv_sem=sems[1].at[a, s],
                    device_id=(px, py, c), device_id_type=MESH))
        return cps

    def start(ins, outs, sems):
        for cp in copies(ins, outs, sems):
            cp.start()

    def finish(ins, outs, sems):
        for cp in copies(ins, outs, sems):
            cp.wait()

    return _Rider(list(arrs), [jax.ShapeDtypeStruct((len(ks),) + a.shape[1:], a.dtype) for a in arrs],
                  [pltpu.SemaphoreType.DMA((n, len(ks))), pltpu.SemaphoreType.DMA((n, len(ks)))], start, finish)


def _join_riders(riders):
    def split(seq, counts):
        out, at = [], 0
        for k in counts:
            out.append(seq[at:at + k])
            at += k
        return out

    n_in = [len(r.arrs) for r in riders]
    n_out = [len(r.out_shapes) for r in riders]
    n_sem = [len(r.sems) for r in riders]

    def run(which):
        def fn(ins, outs, sems):
            for r, i, o, s in zip(riders, split(ins, n_in), split(outs, n_out), split(sems, n_sem)):
                getattr(r, which)(i, o, s)
        return fn

    return _Rider([a for r in riders for a in r.arrs], [o for r in riders for o in r.out_shapes],
                  [s for r in riders for s in r.sems], run("start"), run("finish"))


def _run_rider(rider, name):
    n_in, n_out = len(rider.arrs), len(rider.out_shapes)

    def body(*refs):
        ins, outs, sems = refs[:n_in], refs[n_in:n_in + n_out], refs[n_in + n_out:]
        rider.start(ins, outs, sems)
        rider.finish(ins, outs, sems)

    any_spec = pl.BlockSpec(memory_space=pl.ANY)
    return pl.pallas_call(
        body, name=name, out_shape=list(rider.out_shapes), in_specs=[any_spec] * n_in, out_specs=[any_spec] * n_out,
        scratch_shapes=list(rider.sems),
    )(*rider.arrs)


def _call(body, *, name, grid, in_specs, out_specs, out_shape, args, scratch_shapes=(), rider=None):
    params = _params(("arbitrary",) * len(grid))
    if rider is None:
        return pl.pallas_call(body, name=name, grid=grid, in_specs=in_specs, out_specs=out_specs, out_shape=out_shape,
                              scratch_shapes=list(scratch_shapes), compiler_params=params)(*args)
    single = not isinstance(out_shape, (list, tuple))
    outs = [out_shape] if single else list(out_shape)
    ospecs = [out_specs] if single else list(out_specs)
    n_in, n_out, n_scr = len(in_specs), len(outs), len(scratch_shapes)
    r_in, r_out = len(rider.arrs), len(rider.out_shapes)

    def hosted(*refs):
        refs = list(refs)
        ins, rins = refs[:n_in], refs[n_in:n_in + r_in]
        refs = refs[n_in + r_in:]
        houts, routs = refs[:n_out], refs[n_out:n_out + r_out]
        refs = refs[n_out + r_out:]
        scr, rsems = refs[:n_scr], refs[n_scr:]
        ids = [pl.program_id(a) for a in range(len(grid))]
        first = functools.reduce(lambda p, q: p & q, [i == 0 for i in ids])
        last = functools.reduce(lambda p, q: p & q, [i == g - 1 for i, g in zip(ids, grid)])

        @pl.when(first)
        def _():
            rider.start(rins, routs, rsems)

        body(*ins, *houts, *scr)

        @pl.when(last)
        def _():
            rider.finish(rins, routs, rsems)

    any_spec = pl.BlockSpec(memory_space=pl.ANY)
    res = pl.pallas_call(
        hosted, name=name, grid=grid, in_specs=list(in_specs) + [any_spec] * r_in,
        out_specs=ospecs + [any_spec] * r_out, out_shape=outs + list(rider.out_shapes),
        scratch_shapes=list(scratch_shapes) + list(rider.sems), compiler_params=params,
    )(*args, *rider.arrs)
    return (res[0] if single else list(res[:n_out])), list(res[n_out:])


def _sibling_sum(part, landed, pos, name):
    _, _, rows, cols = part.shape
    tr = _tile(rows, 512)

    def body(pos_ref, p_ref, l_ref, own_ref, all_ref):
        s = p_ref[0, 0] + l_ref[0, 0]
        all_ref[0] = s.astype(BF16)

        @pl.when(pl.program_id(1) == pos_ref[1])
        def _():
            own_ref[...] = s

    return pl.pallas_call(
        body, name=name,
        grid_spec=pltpu.PrefetchScalarGridSpec(
            num_scalar_prefetch=1, grid=(rows // tr, 4),
            in_specs=[pl.BlockSpec((1, 1, tr, cols), lambda i, k, pos: (k, pos[0], i, 0)),
                      pl.BlockSpec((1, 1, tr, cols), lambda i, k, pos: (k, 0, i, 0))],
            out_specs=[pl.BlockSpec((tr, cols), lambda i, k, pos: (i, 0)),
                       pl.BlockSpec((1, tr, cols), lambda i, k, pos: (k, i, 0))]),
        out_shape=[jax.ShapeDtypeStruct((rows, cols), F32), jax.ShapeDtypeStruct((4, rows, cols), BF16)],
        compiler_params=_params(("arbitrary", "arbitrary")),
    )(pos, part, landed)


def _rms(x, g):
    r = lax.rsqrt(jnp.mean(x * x, axis=-1, keepdims=True) + EPS)
    return x * r, r


def _norm_matmul(h, g, w, *, flat, nbk, name, scale=1.0, rider=None):
    T, D = h.shape
    nb, _, bn = w.shape
    tm = _tile(T, TOKEN_TILE)

    def body(h_ref, g_ref, w_ref, o_ref, n_ref):
        @pl.when(pl.program_id(1) == 0)
        def _():
            xh, _ = _rms(h_ref[...], None)
            n_ref[...] = (xh * g_ref[...]).astype(BF16)

        n = n_ref[...]
        for k in range(nbk):
            r = _dot(n, w_ref[k])
            r = (r if scale == 1.0 else r * scale).astype(BF16)
            if flat:
                o_ref[:, k * bn:(k + 1) * bn] = r
            else:
                o_ref[k] = r

    if flat:
        out_shape = jax.ShapeDtypeStruct((T, nb * bn), BF16)
        out_spec = pl.BlockSpec((tm, nbk * bn), lambda i, j: (i, j))
    else:
        out_shape = jax.ShapeDtypeStruct((nb, T, bn), BF16)
        out_spec = pl.BlockSpec((nbk, tm, bn), lambda i, j: (j, i, 0))
    return _call(
        body, name=name, grid=(T // tm, nb // nbk),
        in_specs=[pl.BlockSpec((tm, D), lambda i, j: (i, 0)),
                  pl.BlockSpec((1, D), lambda i, j: (0, 0)),
                  pl.BlockSpec((nbk, D, bn), lambda i, j: (j, 0, 0))],
        out_specs=[out_spec, pl.BlockSpec((tm, D), lambda i, j: (i, 0))],
        out_shape=[out_shape, jax.ShapeDtypeStruct((T, D), BF16)],
        args=(h, g, w), rider=rider)


def _matmul_nt(dy, w, *, flat, nbk, name, norm=None, out_dtype=BF16, rider=None):
    nb, R, bn = w.shape
    T = dy.shape[0] if flat else dy.shape[1]
    tm = _tile(T, TOKEN_TILE)
    nj = nb // nbk

    def body(*refs):
        if norm is None:
            dy_ref, w_ref, o_ref, acc_ref = refs
        else:
            dy_ref, w_ref, h_ref, g_ref, dres_ref, o_ref, dg_ref, acc_ref = refs
        i, j = pl.program_id(0), pl.program_id(1)

        @pl.when(j == 0)
        def _():
            acc_ref[...] = jnp.zeros_like(acc_ref)

        acc = acc_ref[...]
        for k in range(nbk):
            d = dy_ref[:, k * bn:(k + 1) * bn] if flat else dy_ref[k]
            acc = acc + _dot_nt(d.astype(BF16), w_ref[k])
        acc_ref[...] = acc

        @pl.when(j == nj - 1)
        def _():
            if norm is None:
                o_ref[...] = acc.astype(out_dtype)
            else:
                xh, r = _rms(h_ref[...], None)

                @pl.when(i == 0)
                def _():
                    dg_ref[...] = jnp.zeros_like(dg_ref)

                dg_ref[0:1, :] += jnp.sum(acc * xh, axis=0, keepdims=True)
                dn = acc * g_ref[...]
                o_ref[...] = dres_ref[...] + r * (dn - xh * jnp.mean(dn * xh, axis=-1, keepdims=True))

    if flat:
        dy_spec = pl.BlockSpec((tm, nbk * bn), lambda i, j: (i, j))
    else:
        dy_spec = pl.BlockSpec((nbk, tm, bn), lambda i, j: (j, i, 0))
    w_spec = pl.BlockSpec((nbk, R, bn), lambda i, j: (j, 0, 0))
    row_spec = pl.BlockSpec((tm, R), lambda i, j: (i, 0))
    if norm is None:
        in_specs, args = [dy_spec, w_spec], (dy, w)
        out_specs = row_spec
        out_shape = jax.ShapeDtypeStruct((T, R), out_dtype)
    else:
        in_specs = [dy_spec, w_spec, row_spec, pl.BlockSpec((1, R), lambda i, j: (0, 0)), row_spec]
        args = (dy, w) + tuple(norm)
        out_specs = [row_spec, pl.BlockSpec((8, R), lambda i, j: (0, 0))]
        out_shape = [jax.ShapeDtypeStruct((T, R), F32), jax.ShapeDtypeStruct((8, R), F32)]
    return _call(
        body, name=name, grid=(T // tm, nj), in_specs=in_specs, out_specs=out_specs, out_shape=out_shape,
        scratch_shapes=[pltpu.VMEM((tm, R), F32)], args=args, rider=rider)


def _wgrad_cols(n, dy, *, flat, nb, nbk, name, rider=None):
    T, D = n.shape
    bn = dy.shape[1] // nb if flat else dy.shape[2]
    tt = _tile(T, TOKEN_TILE)
    nt = T // tt

    def body(n_ref, dy_ref, o_ref, acc_ref):
        t = pl.program_id(1)

        @pl.when(t == 0)
        def _():
            acc_ref[...] = jnp.zeros_like(acc_ref)

        nv = n_ref[...]
        for k in range(nbk):
            d = dy_ref[:, k * bn:(k + 1) * bn] if flat else dy_ref[k]
            acc_ref[k] += _dot_tn(nv, d)

        @pl.when(t == nt - 1)
        def _():
            o_ref[...] = acc_ref[...]

    if flat:
        dy_spec = pl.BlockSpec((tt, nbk * bn), lambda j, t: (t, j))
    else:
        dy_spec = pl.BlockSpec((nbk, tt, bn), lambda j, t: (j, t, 0))
    return _call(
        body, name=name, grid=(nb // nbk, nt),
        in_specs=[pl.BlockSpec((tt, D), lambda j, t: (t, 0)), dy_spec],
        out_specs=pl.BlockSpec((nbk, D, bn), lambda j, t: (j, 0, 0)),
        out_shape=jax.ShapeDtypeStruct((nb, D, bn), F32),
        scratch_shapes=[pltpu.VMEM((nbk, D, bn), F32)], args=(n, dy), rider=rider)


def _wgrad_rows(xa, dh, *, flat, tk, name, rider=None):
    T, D = dh.shape
    nk = xa.shape[1] // tk if flat else xa.shape[0]
    tt = _tile(T, TOKEN_TILE)
    nt = T // tt

    def body(x_ref, dh_ref, o_ref, acc_ref):
        t = pl.program_id(1)

        @pl.when(t == 0)
        def _():
            acc_ref[...] = jnp.zeros_like(acc_ref)

        xv = x_ref[...] if flat else x_ref[0]
        acc_ref[...] += _dot_tn(xv, dh_ref[...].astype(BF16))

        @pl.when(t == nt - 1)
        def _():
            o_ref[...] = acc_ref[...]

    x_spec = pl.BlockSpec((tt, tk), lambda j, t: (t, j)) if flat else pl.BlockSpec((1, tt, tk), lambda j, t: (j, t, 0))
    return _call(
        body, name=name, grid=(nk, nt),
        in_specs=[x_spec, pl.BlockSpec((tt, D), lambda j, t: (t, 0))],
        out_specs=pl.BlockSpec((tk, D), lambda j, t: (j, 0)),
        out_shape=jax.ShapeDtypeStruct((nk * tk, D), F32),
        scratch_shapes=[pltpu.VMEM((tk, D), F32)], args=(xa, dh), rider=rider)


def _matmul_residual(xa, w, res, name):
    T, K = xa.shape
    D = w.shape[1]
    tm = _tile(T, TOKEN_TILE)

    def body(x_ref, w_ref, r_ref, o_ref):
        o_ref[...] = r_ref[...] + _dot(x_ref[...], w_ref[...])

    return pl.pallas_call(
        body, name=name, grid=(T // tm,),
        in_specs=[pl.BlockSpec((tm, K), lambda i: (i, 0)), pl.BlockSpec((K, D), lambda i: (0, 0)),
                  pl.BlockSpec((tm, D), lambda i: (i, 0))],
        out_specs=pl.BlockSpec((tm, D), lambda i: (i, 0)),
        out_shape=jax.ShapeDtypeStruct((T, D), F32),
        compiler_params=_params(("arbitrary",)),
    )(xa, w, res)


def _gmlp_gate(z, ws, bst, gv, G, gd):
    D = G * gd
    u = _gelu(z[:, :D].astype(F32))
    v = _gelu(z[:, D:].astype(F32))
    vh, r = _rms(v, None)
    vn = (vh * gv).astype(BF16)
    return u, v, vh, r, vn


def _gmlp_forward(z, ws, bst, gv, w_out, x, *, name, rider=None):
    T, D2 = z.shape
    D = D2 // 2
    G = ws.shape[0]
    gd = D // G
    tb = _tile(T, 256)
    nblk = tb // GMLP_BLOCK

    def body(z_ref, ws_ref, b_ref, gv_ref, wo_ref, x_ref, gated_ref, h_ref):
        u, _, _, _, vn = _gmlp_gate(z_ref[...], None, None, gv_ref[...], G, gd)
        for n in range(nblk):
            rows = slice(n * GMLP_BLOCK, (n + 1) * GMLP_BLOCK)
            for gi in range(G):
                cols = slice(gi * gd, (gi + 1) * gd)
                s = _dot(ws_ref[gi], vn[rows, cols]) + b_ref[:, gi:gi + 1]
                gated_ref[rows, cols] = (u[rows, cols] * s).astype(BF16)
        h_ref[...] = x_ref[...] + _dot(gated_ref[...], wo_ref[...])

    return _call(
        body, name=name, grid=(T // tb,),
        in_specs=[pl.BlockSpec((tb, D2), lambda i: (i, 0)), pl.BlockSpec(ws.shape, lambda i: (0, 0, 0)),
                  pl.BlockSpec(bst.shape, lambda i: (0, 0)), pl.BlockSpec((1, D), lambda i: (0, 0)),
                  pl.BlockSpec((D, D), lambda i: (0, 0)), pl.BlockSpec((tb, D), lambda i: (i, 0))],
        out_specs=[pl.BlockSpec((tb, D), lambda i: (i, 0)), pl.BlockSpec((tb, D), lambda i: (i, 0))],
        out_shape=[jax.ShapeDtypeStruct((T, D), BF16), jax.ShapeDtypeStruct((T, D), F32)],
        args=(z, ws, bst, gv, w_out, x), rider=rider)


def _gmlp_backward(z, dgated, ws, bst, gv, mask, *, name, rider=None):
    T, D2 = z.shape
    D = D2 // 2
    G = ws.shape[0]
    gd = D // G
    tb = _tile(T, 256)
    nblk = tb // GMLP_BLOCK

    def body(z_ref, dg_ref, ws_ref, b_ref, gv_ref, mask_ref, dz_ref, dws_ref, db_ref, dgv_ref, dvn_ref):
        @pl.when(pl.program_id(0) == 0)
        def _():
            dws_ref[...] = jnp.zeros_like(dws_ref)
            db_ref[...] = jnp.zeros_like(db_ref)
            dgv_ref[...] = jnp.zeros_like(dgv_ref)

        zf = z_ref[...]
        u, v, vh, r, vn = _gmlp_gate(zf, None, None, gv_ref[...], G, gd)
        dg = dg_ref[...].astype(F32)
        for n in range(nblk):
            rows = slice(n * GMLP_BLOCK, (n + 1) * GMLP_BLOCK)
            for gi in range(G):
                cols = slice(gi * gd, (gi + 1) * gd)
                vblk = vn[rows, cols]
                s = _dot(ws_ref[gi], vblk) + b_ref[:, gi:gi + 1]
                dgb = dg[rows, cols]
                ds = dgb * u[rows, cols]
                dsb = ds.astype(BF16)
                dz_ref[rows, cols] = (dgb * s * _gelu_grad(zf[rows, cols].astype(F32))).astype(BF16)
                dvn_ref[rows, cols] = _dot_tn(ws_ref[gi], dsb)
                dws_ref[gi] += _dot_nt(dsb, vblk) * mask_ref[...]
                db_ref[:, gi:gi + 1] += jnp.sum(ds, axis=1, keepdims=True)
        dvn = dvn_ref[...]
        dgv_ref[0:1, :] += jnp.sum(dvn * vh, axis=0, keepdims=True)
        dn = dvn * gv_ref[...]
        dv = r * (dn - vh * jnp.mean(dn * vh, axis=-1, keepdims=True))
        dz_ref[:, D:] = (dv * _gelu_grad(zf[:, D:].astype(F32))).astype(BF16)

    return _call(
        body, name=name, grid=(T // tb,),
        in_specs=[pl.BlockSpec((tb, D2), lambda i: (i, 0)), pl.BlockSpec((tb, D), lambda i: (i, 0)),
                  pl.BlockSpec(ws.shape, lambda i: (0, 0, 0)), pl.BlockSpec(bst.shape, lambda i: (0, 0)),
                  pl.BlockSpec((1, D), lambda i: (0, 0)), pl.BlockSpec(mask.shape, lambda i: (0, 0))],
        out_specs=[pl.BlockSpec((tb, D2), lambda i: (i, 0)), pl.BlockSpec(ws.shape, lambda i: (0, 0, 0)),
                   pl.BlockSpec(bst.shape, lambda i: (0, 0)), pl.BlockSpec((8, D), lambda i: (0, 0))],
        out_shape=[jax.ShapeDtypeStruct((T, D2), BF16), jax.ShapeDtypeStruct(ws.shape, F32),
                   jax.ShapeDtypeStruct(bst.shape, F32), jax.ShapeDtypeStruct((8, D), F32)],
        scratch_shapes=[pltpu.VMEM((tb, D), F32)], args=(z, dgated, ws, bst, gv, mask), rider=rider)


def _shift_rows(x, k):
    return pltpu.roll(x, k % x.shape[0], axis=0)


def _conv3(ext, cw):
    return (cw[0:1] * _shift_rows(ext, 2)[8:] + cw[1:2] * _shift_rows(ext, 1)[8:] + cw[2:3] * ext[8:])


def _ffn_forward(a, cw, cb, wd, h, seq, *, name, rider=None):
    _, T, F = a.shape
    D = h.shape[1]
    tm = _tile(seq, TOKEN_TILE)
    hb = tm // 16

    def body(a_ref, ap_ref, cw_ref, cb_ref, wd_ref, h_ref, act_ref, c_ref, o_ref, acc_ref):
        i, j = pl.program_id(0), pl.program_id(1)
        keep = ((i * tm) % seq != 0).astype(F32)

        def conv(b):
            ext = jnp.concatenate([ap_ref[b, 8:16].astype(F32) * keep, a_ref[b].astype(F32)], axis=0)
            return _conv3(ext, cw_ref[b]) + cb_ref[b]

        up, gate = conv(j), conv(j + 4)
        c_ref[j] = up.astype(BF16)
        c_ref[j + 4] = gate.astype(BF16)
        act = (gate * _sigmoid(gate) * up).astype(BF16)
        act_ref[0] = act

        @pl.when(j == 0)
        def _():
            acc_ref[...] = h_ref[...]

        acc_ref[...] += _dot(act, wd_ref[0])

        @pl.when(j == 3)
        def _():
            o_ref[...] = acc_ref[...]

    return _call(
        body, name=name, grid=(T // tm, 4),
        in_specs=[pl.BlockSpec((8, tm, F), lambda i, j: (0, i, 0)),
                  pl.BlockSpec((8, 16, F), lambda i, j: (0, jnp.maximum(i * hb - 1, 0), 0)),
                  pl.BlockSpec((8, 3, F), lambda i, j: (0, 0, 0)), pl.BlockSpec((8, 1, F), lambda i, j: (0, 0, 0)),
                  pl.BlockSpec((1, F, D), lambda i, j: (j, 0, 0)), pl.BlockSpec((tm, D), lambda i, j: (i, 0))],
        out_specs=[pl.BlockSpec((1, tm, F), lambda i, j: (j, i, 0)), pl.BlockSpec((8, tm, F), lambda i, j: (0, i, 0)),
                   pl.BlockSpec((tm, D), lambda i, j: (i, 0))],
        out_shape=[jax.ShapeDtypeStruct((4, T, F), BF16), jax.ShapeDtypeStruct((8, T, F), BF16),
                   jax.ShapeDtypeStruct((T, D), F32)],
        scratch_shapes=[pltpu.VMEM((tm, D), F32)], args=(a, a, cw, cb, wd, h), rider=rider)


def _ffn_backward(dh, c, a, cw, wd, seq, *, name, rider=None):
    _, T, F = a.shape
    D = dh.shape[1]
    tm = _tile(seq, TOKEN_TILE)
    hb = tm // 16
    nt = T // tm

    def body(dh_ref, dhn_ref, cu_ref, cg_ref, cun_ref, cgn_ref, au_ref, ag_ref, cw_ref, wd_ref, da_ref, st_ref):
        i, j = pl.program_id(0), pl.program_id(1)
        keep_next = (((i + 1) * tm) % seq != 0).astype(F32)

        @pl.when((i == 0) & (j == 0))
        def _():
            st_ref[...] = jnp.zeros_like(st_ref)

        dhe = jnp.concatenate([dh_ref[...], dhn_ref[...] * keep_next], axis=0).astype(BF16)
        dact = _dot_nt(dhe, wd_ref[0])
        up = jnp.concatenate([cu_ref[0].astype(F32), cun_ref[0, 0:8].astype(F32)], axis=0)
        gate = jnp.concatenate([cg_ref[0].astype(F32), cgn_ref[0, 0:8].astype(F32)], axis=0)
        sg = _sigmoid(gate)
        gs = gate * sg
        d_up = dact * gs
        d_gate = dact * up * (sg + gs * (1.0 - sg))

        def finish(b, a_ref, dc):
            w = cw_ref[b]
            dm, u1, u2 = dc[:tm], _shift_rows(dc, -1)[:tm], _shift_rows(dc, -2)[:tm]
            da_ref[b] = (w[2:3] * dm + w[1:2] * u1 + w[0:1] * u2).astype(BF16)
            av = a_ref[0].astype(F32)
            st_ref[b, 0:1, :] += jnp.sum(u2 * av, axis=0, keepdims=True)
            st_ref[b, 1:2, :] += jnp.sum(u1 * av, axis=0, keepdims=True)
            st_ref[b, 2:3, :] += jnp.sum(dm * av, axis=0, keepdims=True)
            st_ref[b, 3:4, :] += jnp.sum(dm, axis=0, keepdims=True)

        finish(j, au_ref, d_up)
        finish(j + 4, ag_ref, d_gate)

    nxt = lambda i: jnp.minimum((i + 1) * hb, T // 16 - 1)
    return _call(
        body, name=name, grid=(nt, 4),
        in_specs=[pl.BlockSpec((tm, D), lambda i, j: (i, 0)),
                  pl.BlockSpec((8, D), lambda i, j: (jnp.minimum((i + 1) * (tm // 8), T // 8 - 1), 0)),
                  pl.BlockSpec((1, tm, F), lambda i, j: (j, i, 0)), pl.BlockSpec((1, tm, F), lambda i, j: (j + 4, i, 0)),
                  pl.BlockSpec((1, 16, F), lambda i, j: (j, nxt(i), 0)),
                  pl.BlockSpec((1, 16, F), lambda i, j: (j + 4, nxt(i), 0)),
                  pl.BlockSpec((1, tm, F), lambda i, j: (j, i, 0)), pl.BlockSpec((1, tm, F), lambda i, j: (j + 4, i, 0)),
                  pl.BlockSpec((8, 3, F), lambda i, j: (0, 0, 0)),
                  pl.BlockSpec((1, F, D), lambda i, j: (j, 0, 0))],
        out_specs=[pl.BlockSpec((8, tm, F), lambda i, j: (0, i, 0)), pl.BlockSpec((8, 8, F), lambda i, j: (0, 0, 0))],
        out_shape=[jax.ShapeDtypeStruct((8, T, F), BF16), jax.ShapeDtypeStruct((8, 8, F), F32)],
        args=(dh, dh, c, c, c, c, a, a, cw, wd), rider=rider)


def _ffn_fused_forward(h, g, w_in, cw, cb, wd, seq, *, name, rider=None):
    T, D = h.shape
    F = w_in.shape[2]
    tm = _tile(seq, TOKEN_TILE)

    def body(h_ref, g_ref, wu_ref, wg_ref, cw_ref, cb_ref, wd_ref,
             au_ref, ag_ref, cu_ref, cg_ref, act_ref, n_ref, o_ref, acc_ref, carry_ref):
        i, j = pl.program_id(0), pl.program_id(1)
        keep = ((i * tm) % seq != 0).astype(F32)

        @pl.when((i == 0) & (j == 0))
        def _():
            carry_ref[...] = jnp.zeros_like(carry_ref)

        @pl.when(j == 0)
        def _():
            xh, _ = _rms(h_ref[...], None)
            n_ref[...] = (xh * g_ref[...]).astype(BF16)
            acc_ref[...] = h_ref[...]

        n = n_ref[...]

        def branch(b, w_ref, a_ref, c_ref):
            a = _dot(n, w_ref[0]).astype(BF16)
            a_ref[0] = a
            ext = jnp.concatenate([carry_ref[b] * keep, a.astype(F32)], axis=0)
            carry_ref[b] = ext[tm:tm + 8]
            c = _conv3(ext, cw_ref[b]) + cb_ref[b]
            c_ref[0] = c.astype(BF16)
            return c

        up = branch(j, wu_ref, au_ref, cu_ref)
        gate = branch(j + 4, wg_ref, ag_ref, cg_ref)
        act = (gate * _sigmoid(gate) * up).astype(BF16)
        act_ref[0] = act
        acc_ref[...] += _dot(act, wd_ref[0])

        @pl.when(j == 3)
        def _():
            o_ref[...] = acc_ref[...]

    blk = pl.BlockSpec((1, tm, F), lambda i, j: (j, i, 0))
    row = pl.BlockSpec((tm, D), lambda i, j: (i, 0))
    half = jax.ShapeDtypeStruct((4, T, F), BF16)
    return _call(
        body, name=name, grid=(T // tm, 4),
        in_specs=[row, pl.BlockSpec((1, D), lambda i, j: (0, 0)),
                  pl.BlockSpec((1, D, F), lambda i, j: (j, 0, 0)), pl.BlockSpec((1, D, F), lambda i, j: (j + 4, 0, 0)),
                  pl.BlockSpec((8, 3, F), lambda i, j: (0, 0, 0)), pl.BlockSpec((8, 1, F), lambda i, j: (0, 0, 0)),
                  pl.BlockSpec((1, F, D), lambda i, j: (j, 0, 0))],
        out_specs=[blk, blk, blk, blk, blk, row, row],
        out_shape=[half, half, half, half, half, jax.ShapeDtypeStruct((T, D), BF16), jax.ShapeDtypeStruct((T, D), F32)],
        scratch_shapes=[pltpu.VMEM((tm, D), F32), pltpu.VMEM((8, 8, F), F32)],
        args=(h, g, w_in, w_in, cw, cb, wd, ), rider=rider)


def _ffn_fused_backward(dh, cu, cg, au, ag, cw, wd, w_in, h, g, seq, *, name, rider=None):
    T, D = dh.shape
    F = wd.shape[1]
    tm = _tile(seq, TOKEN_TILE // 2)
    hb = tm // 16
    nt = T // tm

    def body(dh_ref, dhn_ref, cu_ref, cg_ref, cun_ref, cgn_ref, au_ref, ag_ref, cw_ref, wd_ref, wu_ref, wg_ref,
             h_ref, g_ref, da_ref, st_ref, o_ref, dg_ref, acc_ref):
        i, j = pl.program_id(0), pl.program_id(1)
        keep_next = (((i + 1) * tm) % seq != 0).astype(F32)

        @pl.when((i == 0) & (j == 0))
        def _():
            st_ref[...] = jnp.zeros_like(st_ref)
            dg_ref[...] = jnp.zeros_like(dg_ref)

        @pl.when(j == 0)
        def _():
            acc_ref[...] = jnp.zeros_like(acc_ref)

        dhe = jnp.concatenate([dh_ref[...], dhn_ref[...] * keep_next], axis=0).astype(BF16)
        dact = _dot_nt(dhe, wd_ref[0])
        up = jnp.concatenate([cu_ref[0].astype(F32), cun_ref[0, 0:8].astype(F32)], axis=0)
        gate = jnp.concatenate([cg_ref[0].astype(F32), cgn_ref[0, 0:8].astype(F32)], axis=0)
        sg = _sigmoid(gate)
        gs = gate * sg
        d_up = dact * gs
        d_gate = dact * up * (sg + gs * (1.0 - sg))

        def finish(b, a_ref, w_ref, dc):
            w = cw_ref[b]
            dm, u1, u2 = dc[:tm], _shift_rows(dc, -1)[:tm], _shift_rows(dc, -2)[:tm]
            da = (w[2:3] * dm + w[1:2] * u1 + w[0:1] * u2).astype(BF16)
            da_ref[b] = da
            av = a_ref[0].astype(F32)
            st_ref[b, 0:1, :] += jnp.sum(u2 * av, axis=0, keepdims=True)
            st_ref[b, 1:2, :] += jnp.sum(u1 * av, axis=0, keepdims=True)
            st_ref[b, 2:3, :] += jnp.sum(dm * av, axis=0, keepdims=True)
            st_ref[b, 3:4, :] += jnp.sum(dm, axis=0, keepdims=True)
            acc_ref[...] += _dot_nt(da, w_ref[0])

        finish(j, au_ref, wu_ref, d_up)
        finish(j + 4, ag_ref, wg_ref, d_gate)

        @pl.when(j == 3)
        def _():
            acc = acc_ref[...]
            xh, r = _rms(h_ref[...], None)
            dg_ref[0:1, :] += jnp.sum(acc * xh, axis=0, keepdims=True)
            dn = acc * g_ref[...]
            o_ref[...] = dh_ref[...] + r * (dn - xh * jnp.mean(dn * xh, axis=-1, keepdims=True))

    nxt = lambda i: jnp.minimum((i + 1) * hb, T // 16 - 1)
    blk = pl.BlockSpec((1, tm, F), lambda i, j: (j, i, 0))
    halo = pl.BlockSpec((1, 16, F), lambda i, j: (j, nxt(i), 0))
    row = pl.BlockSpec((tm, D), lambda i, j: (i, 0))
    return _call(
        body, name=name, grid=(nt, 4),
        in_specs=[row, pl.BlockSpec((8, D), lambda i, j: (jnp.minimum((i + 1) * (tm // 8), T // 8 - 1), 0)),
                  blk, blk, halo, halo, blk, blk,
                  pl.BlockSpec((8, 3, F), lambda i, j: (0, 0, 0)), pl.BlockSpec((1, F, D), lambda i, j: (j, 0, 0)),
                  pl.BlockSpec((1, D, F), lambda i, j: (j, 0, 0)), pl.BlockSpec((1, D, F), lambda i, j: (j + 4, 0, 0)),
                  row, pl.BlockSpec((1, D), lambda i, j: (0, 0))],
        out_specs=[pl.BlockSpec((8, tm, F), lambda i, j: (0, i, 0)), pl.BlockSpec((8, 8, F), lambda i, j: (0, 0, 0)),
                   row, pl.BlockSpec((8, D), lambda i, j: (0, 0))],
        out_shape=[jax.ShapeDtypeStruct((8, T, F), BF16), jax.ShapeDtypeStruct((8, 8, F), F32),
                   jax.ShapeDtypeStruct((T, D), F32), jax.ShapeDtypeStruct((8, D), F32)],
        scratch_shapes=[pltpu.VMEM((tm, D), F32)],
        args=(dh, dh, cu, cg, cu, cg, au, ag, cw, wd, w_in, w_in, h, g), rider=rider)


def _rel_onehot():
    r = lax.broadcasted_iota(jnp.int32, (REL_PAD, SKEW), 0)
    n = lax.broadcasted_iota(jnp.int32, (REL_PAD, SKEW), 1)
    off = jnp.where(n >= WIN, n - SKEW, n)
    idx = jnp.minimum(PAD - off, REL_CLIP) + REL_CLIP
    return (r == idx).astype(BF16)


def _skew(x, sign):
    row = lax.broadcasted_iota(jnp.int32, x.shape, 0)
    for b in range(7):
        x = jnp.where((row >> b) & 1 == 1, pltpu.roll(x, (sign * (1 << b)) % SKEW, axis=1), x)
    return x


def _bias_build(rel, name, rider=None):
    H = rel.shape[0]

    def body(rel_ref, o_ref):
        oh = _rel_onehot()
        hi, mid, lo = _split3(rel_ref[...])
        base = _dot(hi, oh) + _dot(mid, oh) + _dot(lo, oh)
        mine = lax.broadcasted_iota(jnp.int32, (H, 1), 0) == pl.program_id(0)
        row = jnp.sum(jnp.where(mine, base, 0.0), axis=0, keepdims=True)
        q = lax.broadcasted_iota(jnp.int32, (Q_TILE, WIN), 0)
        k = lax.broadcasted_iota(jnp.int32, (Q_TILE, WIN), 1)
        ok = ((q < CHUNK) & (k < WIN - CHUNK)) | ((q >= CHUNK) & (k >= CHUNK))
        t = _skew(jnp.broadcast_to(row, (Q_TILE, SKEW)), 1)
        o_ref[0] = jnp.where(ok, t[:, :WIN], NEG_INF)

    return _call(
        body, name=name, grid=(H,), in_specs=[pl.BlockSpec((H, REL_PAD), lambda h: (0, 0))],
        out_specs=pl.BlockSpec((1, Q_TILE, WIN), lambda h: (h, 0, 0)),
        out_shape=jax.ShapeDtypeStruct((H, Q_TILE, WIN), F32), args=(rel,), rider=rider)


def _bias_reduce(dbias, name):
    H = dbias.shape[0]

    def body(d_ref, o_ref, e_ref):
        oh = _rel_onehot()
        for hd in range(H):
            x = jnp.concatenate([d_ref[hd], jnp.zeros((Q_TILE, SKEW - WIN), F32)], axis=1)
            e_ref[hd:hd + 1, :] = jnp.sum(_skew(x, -1), axis=0, keepdims=True)
        hi, mid, lo = _split3(e_ref[...])
        o_ref[...] = _dot_nt(hi, oh) + _dot_nt(mid, oh) + _dot_nt(lo, oh)

    return pl.pallas_call(
        body, name=name, out_shape=jax.ShapeDtypeStruct((H, REL_PAD), F32),
        in_specs=[pl.BlockSpec(memory_space=pltpu.VMEM)], out_specs=pl.BlockSpec(memory_space=pltpu.VMEM),
        scratch_shapes=[pltpu.VMEM((H, SKEW), F32)],
        compiler_params=_params(),
    )(dbias)


def _pair_stack(xp, even):
    z = jnp.zeros_like(xp)
    return jnp.concatenate([jnp.where(even, xp, z), jnp.where(even, z, xp)], axis=0)


def _pair_merge(y, even):
    return jnp.where(even, y[:Q_TILE], y[Q_TILE:])


def _strip_probs(s_ref, b_ref, pp, r, valid):
    hb, hr = divmod(r, Q_TILE)
    s = s_ref[pp, r:r + STRIP, :] + b_ref[2 * pp + hb, hr:hr + STRIP, :]
    s = jnp.where(valid, s, NEG_INF)
    e = jnp.exp(s - jnp.max(s, axis=-1, keepdims=True))
    return e * (1.0 / jnp.sum(e, axis=-1, keepdims=True))


def _fill_padded(dst_ref, src_ref):
    dst_ref[0:PAD, :] = jnp.zeros((PAD, dst_ref.shape[1]), dst_ref.dtype)
    dst_ref[PAD:, :] = src_ref[...]


def _attn_specs(B, S, D, lanes):
    nt = S // Q_TILE
    q_spec = pl.BlockSpec((Q_TILE, lanes), lambda g, b, i: (b * nt + i, g))
    k_spec = pl.BlockSpec((S, lanes), lambda g, b, i: (b, g))
    v_spec = pl.BlockSpec((S, lanes), lambda g, b, i: (b, D // lanes + g))
    bias_spec = pl.BlockSpec((lanes // HEAD_DIM, Q_TILE, WIN), lambda g, b, i: (g, 0, 0))
    return nt, q_spec, k_spec, v_spec, bias_spec


def _attn_forward(q, kv, bias, S, *, name, rider=None):
    T, D = q.shape
    B = T // S
    lanes = min(FWD_HEADS_PER_STEP * HEAD_DIM, D)
    nt, q_spec, k_spec, v_spec, bias_spec = _attn_specs(B, S, D, lanes)

    npairs = lanes // (2 * HEAD_DIM)

    def body(q_ref, k_ref, v_ref, b_ref, o_ref, kp_ref, vp_ref, s_ref, p_ref):
        i = pl.program_id(2)

        @pl.when(i == 0)
        def _():
            _fill_padded(kp_ref, k_ref)
            _fill_padded(vp_ref, v_ref)

        start = pl.multiple_of(i * Q_TILE, Q_TILE)
        even = lax.broadcasted_iota(jnp.int32, (1, 2 * HEAD_DIM), 1) < HEAD_DIM
        valid = lax.broadcasted_iota(jnp.int32, (STRIP, WIN), 1) >= PAD - i * Q_TILE
        pair_cols = [slice(pp * 2 * HEAD_DIM, (pp + 1) * 2 * HEAD_DIM) for pp in range(npairs)]
        for pp, cols in enumerate(pair_cols):
            s_ref[pp] = _dot_nt(_pair_stack(q_ref[:, cols], even), kp_ref[pl.ds(start, WIN), cols])
        for pp in range(npairs):
            for r in range(0, 2 * Q_TILE, STRIP):
                p = _strip_probs(s_ref, b_ref, pp, r, valid)
                p_ref[pp, r:r + STRIP, :] = p.astype(BF16)
        for pp, cols in enumerate(pair_cols):
            o_ref[:, cols] = _pair_merge(_dot(p_ref[pp], vp_ref[pl.ds(start, WIN), cols]), even).astype(BF16)

    return _call(
        body, name=name, grid=(D // lanes, B, nt),
        in_specs=[q_spec, k_spec, v_spec, bias_spec], out_specs=q_spec,
        out_shape=jax.ShapeDtypeStruct((T, D), BF16),
        scratch_shapes=[pltpu.VMEM((S + PAD, lanes), BF16), pltpu.VMEM((S + PAD, lanes), BF16),
                        pltpu.VMEM((npairs, 2 * Q_TILE, WIN), F32), pltpu.VMEM((npairs, 2 * Q_TILE, WIN), BF16)],
        args=(q, kv, kv, bias), rider=rider)


def _attn_backward(q, kv, bias, do, S, *, name, rider=None):
    T, D = q.shape
    B = T // S
    H = D // HEAD_DIM
    lanes = min(BWD_HEADS_PER_STEP * HEAD_DIM, D)
    nt, q_spec, k_spec, v_spec, bias_spec = _attn_specs(B, S, D, lanes)
    scale = HEAD_DIM ** -0.5

    npairs = lanes // (2 * HEAD_DIM)

    def body(q_ref, k_ref, v_ref, b_ref, do_ref, dq_ref, dk_ref, dv_ref, db_ref, kp_ref, vp_ref, dka_ref, dva_ref,
             s_ref, dp_ref, p_ref, ds_ref):
        b, i = pl.program_id(1), pl.program_id(2)

        @pl.when((b == 0) & (i == 0))
        def _():
            db_ref[...] = jnp.zeros_like(db_ref)

        @pl.when(i == 0)
        def _():
            _fill_padded(kp_ref, k_ref)
            _fill_padded(vp_ref, v_ref)
            dka_ref[...] = jnp.zeros_like(dka_ref)
            dva_ref[...] = jnp.zeros_like(dva_ref)

        start = pl.multiple_of(i * Q_TILE, Q_TILE)
        even = lax.broadcasted_iota(jnp.int32, (1, 2 * HEAD_DIM), 1) < HEAD_DIM
        valid = lax.broadcasted_iota(jnp.int32, (STRIP, WIN), 1) >= PAD - i * Q_TILE
        pair_cols = [slice(pp * 2 * HEAD_DIM, (pp + 1) * 2 * HEAD_DIM) for pp in range(npairs)]
        for pp, cols in enumerate(pair_cols):
            s_ref[pp] = _dot_nt(_pair_stack(q_ref[:, cols], even), kp_ref[pl.ds(start, WIN), cols])
            dp_ref[pp] = _dot_nt(_pair_stack(do_ref[:, cols], even), vp_ref[pl.ds(start, WIN), cols])
        for pp in range(npairs):
            for r in range(0, 2 * Q_TILE, STRIP):
                hb, hr = divmod(r, Q_TILE)
                p = _strip_probs(s_ref, b_ref, pp, r, valid)
                dp = dp_ref[pp, r:r + STRIP, :]
                ds = p * (dp - jnp.sum(p * dp, axis=-1, keepdims=True))
                db_ref[2 * pp + hb, hr:hr + STRIP, :] += ds
                p_ref[pp, r:r + STRIP, :] = p.astype(BF16)
                ds_ref[pp, r:r + STRIP, :] = ds.astype(BF16)
        for pp, cols in enumerate(pair_cols):
            dsb = ds_ref[pp]
            dq_ref[:, cols] = (_pair_merge(_dot(dsb, kp_ref[pl.ds(start, WIN), cols]), even) * scale).astype(BF16)
            dka_ref[pl.ds(start, WIN), cols] += _dot_tn(dsb, _pair_stack(q_ref[:, cols], even))
            dva_ref[pl.ds(start, WIN), cols] += _dot_tn(p_ref[pp], _pair_stack(do_ref[:, cols], even))

        @pl.when(i == nt - 1)
        def _():
            dk_ref[...] = dka_ref[PAD:, :].astype(BF16)
            dv_ref[...] = dva_ref[PAD:, :].astype(BF16)

    dkv_shape = jax.ShapeDtypeStruct((T, D), BF16)
    return _call(
        body, name=name, grid=(D // lanes, B, nt),
        in_specs=[q_spec, k_spec, v_spec, bias_spec, q_spec],
        out_specs=[q_spec, k_spec, k_spec, bias_spec],
        out_shape=[jax.ShapeDtypeStruct((T, D), BF16), dkv_shape, dkv_shape,
                   jax.ShapeDtypeStruct((H, Q_TILE, WIN), F32)],
        scratch_shapes=[pltpu.VMEM((S + PAD, lanes), BF16), pltpu.VMEM((S + PAD, lanes), BF16),
                        pltpu.VMEM((S + PAD, lanes), F32), pltpu.VMEM((S + PAD, lanes), F32),
                        pltpu.VMEM((npairs, 2 * Q_TILE, WIN), F32), pltpu.VMEM((npairs, 2 * Q_TILE, WIN), F32),
                        pltpu.VMEM((npairs, 2 * Q_TILE, WIN), BF16), pltpu.VMEM((npairs, 2 * Q_TILE, WIN), BF16)],
        args=(q, kv, kv, bias, do), rider=rider)


def _loss_head(h, g, target, name):
    T, D = h.shape
    tm = _tile(T, TOKEN_TILE)

    def body(h_ref, g_ref, t_ref, dh_ref, st_ref):
        @pl.when(pl.program_id(0) == 0)
        def _():
            st_ref[...] = jnp.zeros_like(st_ref)

        xh, r = _rms(h_ref[...], None)
        err = xh * g_ref[...] - t_ref[...]
        st_ref[1:2, :] += 0.5 * jnp.sum(jnp.mean(err * err, axis=-1, keepdims=True), axis=0, keepdims=True)
        dy = err * (1.0 / D)
        st_ref[0:1, :] += jnp.sum(dy * xh, axis=0, keepdims=True)
        dn = dy * g_ref[...]
        dh_ref[...] = r * (dn - xh * jnp.mean(dn * xh, axis=-1, keepdims=True))

    row = pl.BlockSpec((tm, D), lambda i: (i, 0))
    return pl.pallas_call(
        body, name=name, grid=(T // tm,),
        in_specs=[row, pl.BlockSpec((1, D), lambda i: (0, 0)), row],
        out_specs=[row, pl.BlockSpec((8, D), lambda i: (0, 0))],
        out_shape=[jax.ShapeDtypeStruct((T, D), F32), jax.ShapeDtypeStruct((8, D), F32)],
        compiler_params=_params(("arbitrary",)),
    )(h, g, target)


def _sum_devices(arrs, name):
    n = len(arrs)

    def body(*refs):
        for a in range(n):
            s = refs[a][0]
            for k in range(1, N_DEV):
                s = s + refs[a][k]
            refs[n + a][...] = s

    vm = pl.BlockSpec(memory_space=pltpu.VMEM)
    return pl.pallas_call(
        body, name=name, out_shape=[jax.ShapeDtypeStruct(a.shape[1:], F32) for a in arrs],
        in_specs=[vm] * n, out_specs=[vm] * n, compiler_params=_params(),
    )(*arrs)


def _adamw_math(w, g, m, v):
    m = ADAM_B1 * m + (1.0 - ADAM_B1) * g
    v = ADAM_B2 * v + (1.0 - ADAM_B2) * (g * g)
    m_hat = m / (1.0 - ADAM_B1 ** ADAM_STEP)
    v_hat = v / (1.0 - ADAM_B2 ** ADAM_STEP)
    delta = -ADAM_LR * (m_hat / (jnp.sqrt(v_hat) + ADAM_EPS) + ADAM_WD * w)
    return delta, m, v


def _adamw_small(items, name):
    n = len(items)

    def body(*refs):
        for a in range(n):
            w, m, v, g = (refs[4 * a + k][...] for k in range(4))
            d, m, v = _adamw_math(w, g, m, v)
            refs[4 * n + 3 * a][...] = d
            refs[4 * n + 3 * a + 1][...] = m
            refs[4 * n + 3 * a + 2][...] = v

    vm = pl.BlockSpec(memory_space=pltpu.VMEM)
    flat = [t for it in items for t in it]
    outs = pl.pallas_call(
        body, name=name,
        out_shape=[jax.ShapeDtypeStruct(it[0].shape, F32) for it in items for _ in range(3)],
        in_specs=[vm] * (4 * n), out_specs=[vm] * (3 * n), compiler_params=_params(),
    )(*flat)
    return [tuple(outs[3 * a:3 * a + 3]) for a in range(n)]


def _adamw_big(w, m, v, owns, landeds, name, rider=None):
    L, R, C = w.shape
    tr = _tile(R, 512)
    nr = R // tr
    counts = [len(ls) for ls in landeds]

    def body(*refs):
        w_ref, m_ref, v_ref = refs[:3]
        g_ref, d_ref, mo_ref, vo_ref = refs[-4:]
        layer = pl.program_id(0)
        at = 3
        for j in range(L):
            own_ref, l_refs = refs[at], refs[at + 1:at + 1 + counts[j]]
            at += 1 + counts[j]

            @pl.when(layer == j)
            def _(own_ref=own_ref, l_refs=l_refs):
                g = own_ref[...]
                for l_ref in l_refs:
                    for k in range(l_ref.shape[0]):
                        g = g + l_ref[k].astype(F32)
                d, mn, vn = _adamw_math(w_ref[0], g, m_ref[0], v_ref[0])
                g_ref[0] = g
                d_ref[0] = d
                mo_ref[0] = mn
                vo_ref[0] = vn

    def pinned(j):
        return lambda l, i: jnp.where(l == j, i, jnp.where(l < j, 0, nr - 1))

    row = pl.BlockSpec((1, tr, C), lambda l, i: (l, i, 0))
    in_specs, args = [row, row, row], [w, m, v]
    for j in range(L):
        in_specs.append(pl.BlockSpec((tr, C), lambda l, i, p=pinned(j): (p(l, i), 0)))
        args.append(owns[j])
        for arr in landeds[j]:
            in_specs.append(pl.BlockSpec((arr.shape[0], tr, C), lambda l, i, p=pinned(j): (0, p(l, i), 0)))
            args.append(arr)
    return _call(body, name=name, grid=(L, nr), in_specs=in_specs, out_specs=[row] * 4,
                 out_shape=[jax.ShapeDtypeStruct((L, R, C), F32)] * 4, args=args, rider=rider)


def kernel(x, a_norm_g, a_w_in, a_v_norm_g, a_w_s, a_b_s, a_w_out, kv_norm_g, w_kv, b_norm_g, b_w_q, b_rel_bias, b_w_o, f_norm_g, f_w_in, f_conv_w, f_conv_b, f_w_down, final_norm_g, loss_target, m_a_norm_g, m_a_w_in, m_a_v_norm_g, m_a_w_s, m_a_b_s, m_a_w_out, m_kv_norm_g, m_w_kv, m_b_norm_g, m_b_w_q, m_b_rel_bias, m_b_w_o, m_f_norm_g, m_f_w_in, m_f_conv_w, m_f_conv_b, m_f_w_down, m_final_norm_g, v_a_norm_g, v_a_w_in, v_a_v_norm_g, v_a_w_s, v_a_b_s, v_a_w_out, v_kv_norm_g, v_w_kv, v_b_norm_g, v_b_w_q, v_b_rel_bias, v_b_w_o, v_f_norm_g, v_f_w_in, v_f_conv_w, v_f_conv_b, v_f_w_down, v_final_norm_g):
    B, S, D = x.shape
    T = B * S
    G = a_w_s.shape[1]
    H = D // HEAD_DIM
    F = f_w_in.shape[2]
    L = f_w_in.shape[0]
    dn = D // N_DEV
    xi, yi, ci = lax.axis_index("x"), lax.axis_index("y"), lax.axis_index("c")
    me = 4 * xi + 2 * yi + ci
    pos = jnp.stack([ci, 2 * xi + yi]).astype(jnp.int32)

    cast = lambda t: t.astype(BF16)
    gather = lambda *ts: _gather_rider(list(ts))
    rel = jnp.pad(b_rel_bias[0], ((0, 0), (0, REL_PAD - b_rel_bias.shape[2])))
    bias, (wa_in, norms_sh, conv_w0, conv_w1) = _bias_build(rel, "bias_build", rider=gather(
        cast(a_w_in[0]), jnp.concatenate([a_norm_g, a_v_norm_g], axis=0), f_conv_w[0], f_conv_w[1]))
    ga = jnp.transpose(norms_sh, (1, 0, 2)).reshape(2, D)
    g_a, g_av = ga[0:1], ga[1:2]

    x2 = x.reshape(T, D)
    tgt = loss_target.reshape(T, D)
    pc = jnp.arange(GMLP_BLOCK) // CHUNK
    mask = (pc[:, None] >= pc[None, :]).astype(F32)
    ws = (a_w_s[0] * mask[None]).astype(BF16)
    bst = jnp.transpose(a_b_s[0])
    four = lambda t: t.reshape((4, 2) + t.shape[1:])

    w_in0_sh = cast(f_w_in[0])
    (z, n_a), (wa_out, w_in0_top) = _norm_matmul(x2, g_a, wa_in, flat=True, nbk=4, name="gmlp_in",
                                                 rider=gather(cast(a_w_out[0]), w_in0_sh[:D // 2]))
    wa_out = wa_out.reshape(D, D)
    (gated, h1), (w_in0_bottom, wf_down0) = _gmlp_forward(z, ws, bst, g_av, wa_out, x2, name="gmlp_mix",
                                                          rider=gather(w_in0_sh[D // 2:], cast(f_w_down[0])))
    w_in0 = jnp.concatenate([w_in0_top, w_in0_bottom], axis=1)
    cw0, cb0, wd0 = conv_w0, f_conv_b[0].reshape(8, 1, F), wf_down0.reshape(4, F, D)
    (au0, ag0, cu0, cg0, act0, n_f0, h2), (wkv, wq, w_in1) = _ffn_fused_forward(
        h1, f_norm_g[0:1], w_in0, cw0, cb0, wd0, S, name="ffn0_fwd",
        rider=gather(cast(w_kv), cast(b_w_q[0]), cast(f_w_in[1])))
    wq = wq.reshape(D, D)
    kv, n_kv = _norm_matmul(h2, kv_norm_g.reshape(1, D), wkv, flat=True, nbk=4, name="kv_proj")
    q, n_q = _norm_matmul(h2, b_norm_g, wq.reshape(1, D, D), flat=True, nbk=1, name="q_proj", scale=HEAD_DIM ** -0.5)
    o, (wo, wf_down1) = _attn_forward(q, kv, bias, S, name="attn", rider=gather(cast(b_w_o[0]), cast(f_w_down[1])))
    wo = wo.reshape(D, D)
    cw1, cb1, wd1 = conv_w1, f_conv_b[1].reshape(8, 1, F), wf_down1.reshape(4, F, D)
    h3 = _matmul_residual(o, wo, h2, "attn_out")
    au1, ag1, cu1, cg1, act1, n_f1, h4 = _ffn_fused_forward(h3, f_norm_g[1:2], w_in1, cw1, cb1, wd1, S, name="ffn1_fwd")

    sums, from_chips = {}, {}

    def sibling_sums(names, parts, landed):
        for nm, p, l in zip(names, parts, landed):
            sums[nm] = _sibling_sum(p, l, pos, "grad_sibling_sum_" + nm)

    def chip_rider(*names):
        return _chip_rider([sums[nm][1] for nm in names])

    dh4, st_final = _loss_head(h4, final_norm_g.reshape(1, D), tgt, "loss_head")
    g_wd1 = _wgrad_rows(act1, dh4, flat=False, tk=F, name="ffn1_dwdown")
    parts = [four(g_wd1.reshape(8, F // 2, D))]
    (da1, st_conv1, dh3, st_f1), landed = _ffn_fused_backward(
        dh4, cu1, cg1, au1, ag1, cw1, wd1, w_in1, h3, f_norm_g[1:2], S, name="ffn1_bwd", rider=_sibling_rider(parts))
    sibling_sums(["wd1"], parts, landed)
    g_win1, (from_chips["wd1"],) = _wgrad_cols(n_f1, da1, flat=False, nb=8, nbk=2, name="ffn1_dwin",
                                               rider=chip_rider("wd1"))
    d_o = _matmul_nt(dh3, wo.reshape(1, D, D), flat=True, nbk=1, name="attn_out_dx")
    parts = [four(g_win1)]
    g_wo, landed = _wgrad_rows(o, dh3, flat=True, tk=_tile(D, 512), name="attn_out_dw", rider=_sibling_rider(parts))
    sibling_sums(["win1"], parts, landed)
    (dq, dk, dv, dbias), (from_chips["win1"],) = _attn_backward(
        q, kv, bias, d_o, S, name="attn_bwd", rider=chip_rider("win1"))
    g_rel = _bias_reduce(dbias, "bias_reduce")
    g_wq = _wgrad_cols(n_q, dq, flat=True, nb=1, nbk=1, name="q_dw")
    dh2, st_b = _matmul_nt(dq, wq.reshape(1, D, D), flat=True, nbk=1, name="q_dx", norm=(h2, b_norm_g, dh3))
    dkv = jnp.concatenate([dk, dv], axis=-1)
    g_wkv = _wgrad_cols(n_kv, dkv, flat=True, nb=8, nbk=4, name="kv_dw")
    parts = [four(g_wo.reshape(8, dn, D)), four(g_wq.reshape(8, dn, D)), four(g_wkv)]
    (dh2, st_kv), landed = _matmul_nt(dkv, wkv, flat=True, nbk=4, name="kv_dx",
                                      norm=(h2, kv_norm_g.reshape(1, D), dh2), rider=_sibling_rider(parts))
    sibling_sums(["wo", "wq", "wkv"], parts, landed)
    g_wd0, (from_chips["wo"], from_chips["wq"], from_chips["wkv"]) = _wgrad_rows(
        act0, dh2, flat=False, tk=F, name="ffn0_dwdown", rider=chip_rider("wo", "wq", "wkv"))
    parts = [four(g_wd0.reshape(8, F // 2, D))]
    (da0, st_conv0, dh1, st_f0), landed = _ffn_fused_backward(
        dh2, cu0, cg0, au0, ag0, cw0, wd0, w_in0, h1, f_norm_g[0:1], S, name="ffn0_bwd", rider=_sibling_rider(parts))
    sibling_sums(["wd0"], parts, landed)
    g_win0, (ce,) = _wgrad_cols(n_f0, da0, flat=False, nb=8, nbk=2, name="ffn0_dwin", rider=chip_rider("wd0"))
    from_chips["wd0"] = [ce]
    dgated = _matmul_nt(dh1, wa_out.reshape(1, D, D), flat=True, nbk=1, name="gmlp_out_dx")
    parts = [four(g_win0)]
    g_wa_out, landed = _wgrad_rows(gated, dh1, flat=True, tk=_tile(D, 512), name="gmlp_out_dw",
                                   rider=_sibling_rider(parts))
    sibling_sums(["win0"], parts, landed)
    parts = [four(g_wa_out.reshape(8, dn, D))]
    (dz, g_ws, g_bst, st_av), (ce_win0_a, landed) = _gmlp_backward(
        z, dgated, ws, bst, g_av, mask, name="gmlp_bwd",
        rider=_join_riders([_chip_rider([sums["win0"][1]], ks=(1, 2)), _sibling_rider(parts)]))
    sibling_sums(["wa_out"], parts, [landed])
    g_wa_in, (ce_win0_b, ce) = _wgrad_cols(n_a, dz, flat=True, nb=8, nbk=4, name="gmlp_in_dw", rider=_join_riders(
        [_chip_rider([sums["win0"][1]], ks=(3,)), chip_rider("wa_out")]))
    from_chips["win0"], from_chips["wa_out"] = [ce_win0_a, ce_win0_b], [ce]
    vec = jnp.concatenate([st_av[0:1], st_kv[0:1], st_b[0:1], st_f0[0:1], st_f1[0:1], st_final[0:3]], axis=0)
    parts = [four(g_wa_in)]
    (grad_x, st_a), got = _matmul_nt(dz, wa_in, flat=True, nbk=4, name="gmlp_in_dx", norm=(x2, g_a, dh1),
                                     rider=_join_riders([_sibling_rider(parts),
                                                         gather(vec, g_ws, g_bst, g_rel, st_conv0, st_conv1)]))
    sibling_sums(["wa_in"], parts, got[0:1])
    small = got[1:]

    def big_update(names, w, m, v, rider=None):
        shape = w.shape
        r = lambda t: t.reshape((len(names), -1, shape[-1]))
        as_list = lambda t: t if isinstance(t, list) else [t]
        outs = _adamw_big(r(w), r(m), r(v), [sums[nm][0] for nm in names],
                          [as_list(from_chips[nm]) for nm in names], "adamw_" + names[0], rider=rider)
        outs, got = (outs, None) if rider is None else outs
        return [t.reshape(shape) for t in outs], got

    u_f_w_in, (ce, st_a) = big_update(["win0", "win1"], f_w_in, m_f_w_in, v_f_w_in,
                                      rider=_join_riders([chip_rider("wa_in"), gather(st_a)]))
    from_chips["wa_in"] = [ce]
    u_f_w_down, _ = big_update(["wd0", "wd1"], f_w_down, m_f_w_down, v_f_w_down)
    u_a_w_in, _ = big_update(["wa_in"], a_w_in, m_a_w_in, v_a_w_in)
    u_w_kv, _ = big_update(["wkv"], w_kv, m_w_kv, v_w_kv)
    u_a_w_out, _ = big_update(["wa_out"], a_w_out, m_a_w_out, v_a_w_out)
    u_b_w_q, _ = big_update(["wq"], b_w_q, m_b_w_q, v_b_w_q)
    u_b_w_o, _ = big_update(["wo"], b_w_o, m_b_w_o, v_b_w_o)

    vec, g_ws, g_bst, g_rel, st_conv0, st_conv1, st_a = _sum_devices(list(small) + [st_a], "sum_small_grads")
    vec = jnp.concatenate([st_a[0:1], vec[0:7]], axis=0)
    loss = vec[7, 0]
    g_a_norm = lax.dynamic_slice_in_dim(vec[0:1], me * dn, dn, axis=1)
    g_av_norm = lax.dynamic_slice_in_dim(vec[1:2], me * dn, dn, axis=1)
    st_conv = jnp.stack([st_conv0, st_conv1])
    g_conv_w = lax.dynamic_index_in_dim(st_conv, me, axis=1, keepdims=False)[:, 0:3]
    g_conv_b = st_conv[:, :, 3, :].reshape(L, 8 * F)
    small_items = [
        (a_norm_g, m_a_norm_g, v_a_norm_g, g_a_norm),
        (a_v_norm_g, m_a_v_norm_g, v_a_v_norm_g, g_av_norm),
        (a_w_s, m_a_w_s, v_a_w_s, g_ws[None]),
        (a_b_s, m_a_b_s, v_a_b_s, jnp.transpose(g_bst)[None]),
        (kv_norm_g.reshape(1, D), m_kv_norm_g.reshape(1, D), v_kv_norm_g.reshape(1, D), vec[2:3]),
        (b_norm_g, m_b_norm_g, v_b_norm_g, vec[3:4]),
        (b_rel_bias, m_b_rel_bias, v_b_rel_bias, g_rel[None, :, :b_rel_bias.shape[2]]),
        (f_norm_g, m_f_norm_g, v_f_norm_g, vec[4:6]),
        (f_conv_w, m_f_conv_w, v_f_conv_w, g_conv_w),
        (f_conv_b, m_f_conv_b, v_f_conv_b, g_conv_b),
        (final_norm_g.reshape(1, D), m_final_norm_g.reshape(1, D), v_final_norm_g.reshape(1, D), vec[6:7]),
    ]
    small_out = _adamw_small(small_items, "adamw_small")
    (u_a_norm, u_av_norm, u_ws, u_bs, u_kvn, u_bn, u_rel, u_fn, u_cw, u_cb, u_fin) = [
        (it[3],) + so for it, so in zip(small_items, small_out)]
    vecD = lambda u: tuple(t.reshape(D) for t in u)
    u_kvn, u_fin = vecD(u_kvn), vecD(u_fin)

    order = [u_a_norm, u_a_w_in, u_av_norm, u_ws, u_bs, u_a_w_out, u_kvn, u_w_kv, u_bn, u_b_w_q, u_rel, u_b_w_o,
             u_fn, u_f_w_in, u_cw, u_cb, u_f_w_down, u_fin]
    outs = [loss, grad_x.reshape(B, S, D)]
    for k in range(4):
        outs += [u[k] for u in order]
    return tuple(outs)
```

```python
import functools

import jax
import jax.numpy as jnp
from jax import lax
from jax.experimental import pallas as pl
from jax.experimental.pallas import tpu as pltpu

F32 = jnp.float32
BF16 = jnp.bfloat16
MESH = pl.DeviceIdType.MESH

N_DEV = 8
EPS = 1e-6
NEG_INF = -1e30
CHUNK = 64
LEFT_CHUNKS = 8
REL_CLIP = 128
HEAD_DIM = 64
GMLP_BLOCK = 128
Q_TILE = 2 * CHUNK
PAD = LEFT_CHUNKS * CHUNK
WIN = PAD + Q_TILE
SKEW = WIN + Q_TILE
REL_PAD = 384
FWD_HEADS_PER_STEP = 8
BWD_HEADS_PER_STEP = 4
STRIP = 32
ROWS = 32
ADAM_LR, ADAM_B1, ADAM_B2, ADAM_EPS, ADAM_WD, ADAM_STEP = 0.001, 0.9, 0.999, 1e-08, 0.01, 10
VMEM_LIMIT = 56 * 1024 * 1024
TOKEN_TILE = 512


def _params(sem=None):
    return pltpu.CompilerParams(dimension_semantics=sem, vmem_limit_bytes=VMEM_LIMIT)


def _tile(n, pref):
    if n <= pref:
        return n
    for t in range(pref - pref % 8, 7, -8):
        if n % t == 0:
            return t
    return n


def _gelu(x):
    return 0.5 * x * (1.0 + jnp.tanh(0.7978845608028654 * (x + 0.044715 * x * x * x)))


def _gelu_grad(x):
    t = jnp.tanh(0.7978845608028654 * (x + 0.044715 * x * x * x))
    return 0.5 * (1.0 + t) + 0.5 * x * (1.0 - t * t) * 0.7978845608028654 * (1.0 + 3 * 0.044715 * x * x)


def _sigmoid(x):
    return 1.0 / (1.0 + jnp.exp(-x))


def _dot(a, b):
    return jnp.dot(a, b, preferred_element_type=F32)


def _dot_nt(a, b):
    return lax.dot_general(a, b, (((1,), (1,)), ((), ())), preferred_element_type=F32)


def _dot_tn(a, b):
    return lax.dot_general(a, b, (((0,), (0,)), ((), ())), preferred_element_type=F32)


def _split3(x):
    hi = x.astype(BF16)
    r1 = x - hi.astype(F32)
    mid = r1.astype(BF16)
    lo = (r1 - mid.astype(F32)).astype(BF16)
    return hi, mid, lo


def _mesh_pos():
    return lax.axis_index("x"), lax.axis_index("y"), lax.axis_index("c")


class _Rider:
    def __init__(self, arrs, out_shapes, sems, start, finish):
        self.arrs, self.out_shapes, self.sems, self.start, self.finish = arrs, out_shapes, sems, start, finish


def _gather_rider(arrs):
    n = len(arrs)

    def tools(ins, outs, sems):
        send_sems, recv_sems, local_sems = sems
        x, y, c = _mesh_pos()
        me, sibling = (x, y, c), (x, y, 1 - c)
        chips = [(1 - x, y), (x, 1 - y), (1 - x, 1 - y)]

        def slot(a, block):
            px, py, pc = block
            return outs[a].at[4 * px + 2 * py + pc]

        def copy(a, k, block, to, src=None):
            dst = slot(a, block)
            return pltpu.make_async_remote_copy(
                src_ref=dst if src is None else src, dst_ref=dst,
                send_sem=send_sems.at[a, k], recv_sem=recv_sems.at[a, k], device_id=to, device_id_type=MESH)

        def first(a):
            cps = [copy(a, 0, me, sibling, src=ins[a])]
            return cps + [copy(a, 1 + j, me, (*chip, c), src=ins[a]) for j, chip in enumerate(chips)]

        def mine(a):
            return pltpu.make_async_copy(ins[a], slot(a, me), local_sems.at[a])

        return me, sibling, chips, c, copy, first, mine

    def start(ins, outs, sems):
        _, _, _, _, _, first, mine = tools(ins, outs, sems)
        for a in range(n):
            mine(a).start()
            for cp in first(a):
                cp.start()

    def finish(ins, outs, sems):
        me, sibling, chips, c, copy, first, mine = tools(ins, outs, sems)
        passed = []
        for j, chip in enumerate(chips):
            for a in range(n):
                copy(a, 1 + j, (*chip, c), me).wait_recv()
                fwd = copy(a, 4 + j, (*chip, c), sibling)
                fwd.start()
                passed.append(fwd)
        for a in range(n):
            copy(a, 0, sibling, me).wait_recv()
            for j, chip in enumerate(chips):
                copy(a, 4 + j, (*chip, 1 - c), me).wait_recv()
        for a in range(n):
            for cp in first(a):
                cp.wait_send()
        for cp in passed:
            cp.wait_send()
        for a in range(n):
            mine(a).wait()

    return _Rider(list(arrs), [jax.ShapeDtypeStruct((N_DEV,) + a.shape, a.dtype) for a in arrs],
                  [pltpu.SemaphoreType.DMA((n, 7)), pltpu.SemaphoreType.DMA((n, 7)), pltpu.SemaphoreType.DMA((n,))],
                  start, finish)


def _sibling_rider(arrs):
    n = len(arrs)

    def copies(ins, outs, sems):
        x, y, c = _mesh_pos()
        return [pltpu.make_async_remote_copy(
            src_ref=ins[a].at[:, pl.ds(1 - c, 1)], dst_ref=outs[a],
            send_sem=sems[0].at[a], recv_sem=sems[1].at[a], device_id=(x, y, 1 - c), device_id_type=MESH)
            for a in range(n)]

    def start(ins, outs, sems):
        for cp in copies(ins, outs, sems):
            cp.start()

    def finish(ins, outs, sems):
        for cp in copies(ins, outs, sems):
            cp.wait()

    return _Rider(list(arrs), [jax.ShapeDtypeStruct((4, 1) + a.shape[2:], a.dtype) for a in arrs],
                  [pltpu.SemaphoreType.DMA((n,)), pltpu.SemaphoreType.DMA((n,))], start, finish)


def _chip_rider(arrs, ks=(1, 2, 3)):
    n = len(arrs)

    def copies(ins, outs, sems):
        x, y, c = _mesh_pos()
        cps = []
        for a in range(n):
            for s, k in enumerate(ks):
                px = x if k < 2 else 1 - x
                py = y if k == 2 else 1 - y
                cps.append(pltpu.make_async_remote_copy(
                    src_ref=ins[a].at[2 * px + py], dst_ref=outs[a].at[s],
                    send_sem=sems[0].at[a, s], recv_sem=sems[1].at[a, s],
                    device_id=(px, py, c), device_id_type=MESH))
        return cps

    def start(ins, outs, sems):
        for cp in copies(ins, outs, sems):
            cp.start()

    def finish(ins, outs, sems):
        for cp in copies(ins, outs, sems):
            cp.wait()

    return _Rider(list(arrs), [jax.ShapeDtypeStruct((len(ks),) + a.shape[1:], a.dtype) for a in arrs],
                  [pltpu.SemaphoreType.DMA((n, len(ks))), pltpu.SemaphoreType.DMA((n, len(ks)))], start, finish)


def _join_riders(riders):
    def split(seq, counts):
        out, at = [], 0
        for k in counts:
            out.append(seq[at:at + k])
            at += k
        return out

    n_in = [len(r.arrs) for r in riders]
    n_out = [len(r.out_shapes) for r in riders]
    n_sem = [len(r.sems) for r in riders]

    def run(which):
        def fn(ins, outs, sems):
            for r, i, o, s in zip(riders, split(ins, n_in), split(outs, n_out), split(sems, n_sem)):
                getattr(r, which)(i, o, s)
        return fn

    return _Rider([a for r in riders for a in r.arrs], [o for r in riders for o in r.out_shapes],
                  [s for r in riders for s in r.sems], run("start"), run("finish"))


def _run_rider(rider, name):
    n_in, n_out = len(rider.arrs), len(rider.out_shapes)

    def body(*refs):
        ins, outs, sems = refs[:n_in], refs[n_in:n_in + n_out], refs[n_in + n_out:]
        rider.start(ins, outs, sems)
        rider.finish(ins, outs, sems)

    any_spec = pl.BlockSpec(memory_space=pl.ANY)
    return pl.pallas_call(
        body, name=name, out_shape=list(rider.out_shapes), in_specs=[any_spec] * n_in, out_specs=[any_spec] * n_out,
        scratch_shapes=list(rider.sems),
    )(*rider.arrs)


def _call(body, *, name, grid, in_specs, out_specs, out_shape, args, scratch_shapes=(), rider=None):
    params = _params(("arbitrary",) * len(grid))
    if rider is None:
        return pl.pallas_call(body, name=name, grid=grid, in_specs=in_specs, out_specs=out_specs, out_shape=out_shape,
                              scratch_shapes=list(scratch_shapes), compiler_params=params)(*args)
    single = not isinstance(out_shape, (list, tuple))
    outs = [out_shape] if single else list(out_shape)
    ospecs = [out_specs] if single else list(out_specs)
    n_in, n_out, n_scr = len(in_specs), len(outs), len(scratch_shapes)
    r_in, r_out = len(rider.arrs), len(rider.out_shapes)

    def hosted(*refs):
        refs = list(refs)
        ins, rins = refs[:n_in], refs[n_in:n_in + r_in]
        refs = refs[n_in + r_in:]
        houts, routs = refs[:n_out], refs[n_out:n_out + r_out]
        refs = refs[n_out + r_out:]
        scr, rsems = refs[:n_scr], refs[n_scr:]
        ids = [pl.program_id(a) for a in range(len(grid))]
        first = functools.reduce(lambda p, q: p & q, [i == 0 for i in ids])
        last = functools.reduce(lambda p, q: p & q, [i == g - 1 for i, g in zip(ids, grid)])

        @pl.when(first)
        def _():
            rider.start(rins, routs, rsems)

        body(*ins, *houts, *scr)

        @pl.when(last)
        def _():
            rider.finish(rins, routs, rsems)

    any_spec = pl.BlockSpec(memory_space=pl.ANY)
    res = pl.pallas_call(
        hosted, name=name, grid=grid, in_specs=list(in_specs) + [any_spec] * r_in,
        out_specs=ospecs + [any_spec] * r_out, out_shape=outs + list(rider.out_shapes),
        scratch_shapes=list(scratch_shapes) + list(rider.sems), compiler_params=params,
    )(*args, *rider.arrs)
    return (res[0] if single else list(res[:n_out])), list(res[n_out:])


def _sibling_sum(part, landed, pos, name):
    _, _, rows, cols = part.shape
    tr = _tile(rows, 512)

    def body(pos_ref, p_ref, l_ref, own_ref, all_ref):
        s = p_ref[0, 0] + l_ref[0, 0]
        all_ref[0] = s.astype(BF16)

        @pl.when(pl.program_id(1) == pos_ref[1])
        def _():
            own_ref[...] = s

    return pl.pallas_call(
        body, name=name,
        grid_spec=pltpu.PrefetchScalarGridSpec(
            num_scalar_prefetch=1, grid=(rows // tr, 4),
            in_specs=[pl.BlockSpec((1, 1, tr, cols), lambda i, k, pos: (k, pos[0], i, 0)),
                      pl.BlockSpec((1, 1, tr, cols), lambda i, k, pos: (k, 0, i, 0))],
            out_specs=[pl.BlockSpec((tr, cols), lambda i, k, pos: (i, 0)),
                       pl.BlockSpec((1, tr, cols), lambda i, k, pos: (k, i, 0))]),
        out_shape=[jax.ShapeDtypeStruct((rows, cols), F32), jax.ShapeDtypeStruct((4, rows, cols), BF16)],
        compiler_params=_params(("arbitrary", "arbitrary")),
    )(pos, part, landed)


def _rms(x, g):
    r = lax.rsqrt(jnp.mean(x * x, axis=-1, keepdims=True) + EPS)
    return x * r, r


def _norm_matmul(h, g, w, *, flat, nbk, name, scale=1.0, rider=None):
    T, D = h.shape
    nb, _, bn = w.shape
    tm = _tile(T, TOKEN_TILE)

    def body(h_ref, g_ref, w_ref, o_ref, n_ref):
        @pl.when(pl.program_id(1) == 0)
        def _():
            xh, _ = _rms(h_ref[...], None)
            n_ref[...] = (xh * g_ref[...]).astype(BF16)

        n = n_ref[...]
        for k in range(nbk):
            r = _dot(n, w_ref[k])
            r = (r if scale == 1.0 else r * scale).astype(BF16)
            if flat:
                o_ref[:, k * bn:(k + 1) * bn] = r
            else:
                o_ref[k] = r

    if flat:
        out_shape = jax.ShapeDtypeStruct((T, nb * bn), BF16)
        out_spec = pl.BlockSpec((tm, nbk * bn), lambda i, j: (i, j))
    else:
        out_shape = jax.ShapeDtypeStruct((nb, T, bn), BF16)
        out_spec = pl.BlockSpec((nbk, tm, bn), lambda i, j: (j, i, 0))
    return _call(
        body, name=name, grid=(T // tm, nb // nbk),
        in_specs=[pl.BlockSpec((tm, D), lambda i, j: (i, 0)),
                  pl.BlockSpec((1, D), lambda i, j: (0, 0)),
                  pl.BlockSpec((nbk, D, bn), lambda i, j: (j, 0, 0))],
        out_specs=[out_spec, pl.BlockSpec((tm, D), lambda i, j: (i, 0))],
        out_shape=[out_shape, jax.ShapeDtypeStruct((T, D), BF16)],
        args=(h, g, w), rider=rider)


def _matmul_nt(dy, w, *, flat, nbk, name, norm=None, out_dtype=BF16, rider=None):
    nb, R, bn = w.shape
    T = dy.shape[0] if flat else dy.shape[1]
    tm = _tile(T, TOKEN_TILE)
    nj = nb // nbk

    def body(*refs):
        if norm is None:
            dy_ref, w_ref, o_ref, acc_ref = refs
        else:
            dy_ref, w_ref, h_ref, g_ref, dres_ref, o_ref, dg_ref, acc_ref = refs
        i, j = pl.program_id(0), pl.program_id(1)

        @pl.when(j == 0)
        def _():
            acc_ref[...] = jnp.zeros_like(acc_ref)

        acc = acc_ref[...]
        for k in range(nbk):
            d = dy_ref[:, k * bn:(k + 1) * bn] if flat else dy_ref[k]
            acc = acc + _dot_nt(d.astype(BF16), w_ref[k])
        acc_ref[...] = acc

        @pl.when(j == nj - 1)
        def _():
            if norm is None:
                o_ref[...] = acc.astype(out_dtype)
            else:
                xh, r = _rms(h_ref[...], None)

                @pl.when(i == 0)
                def _():
                    dg_ref[...] = jnp.zeros_like(dg_ref)

                dg_ref[0:1, :] += jnp.sum(acc * xh, axis=0, keepdims=True)
                dn = acc * g_ref[...]
                o_ref[...] = dres_ref[...] + r * (dn - xh * jnp.mean(dn * xh, axis=-1, keepdims=True))

    if flat:
        dy_spec = pl.BlockSpec((tm, nbk * bn), lambda i, j: (i, j))
    else:
        dy_spec = pl.BlockSpec((nbk, tm, bn), lambda i, j: (j, i, 0))
    w_spec = pl.BlockSpec((nbk, R, bn), lambda i, j: (j, 0, 0))
    row_spec = pl.BlockSpec((tm, R), lambda i, j: (i, 0))
    if norm is None:
        in_specs, args = [dy_spec, w_spec], (dy, w)
        out_specs = row_spec
        out_shape = jax.ShapeDtypeStruct((T, R), out_dtype)
    else:
        in_specs = [dy_spec, w_spec, row_spec, pl.BlockSpec((1, R), lambda i, j: (0, 0)), row_spec]
        args = (dy, w) + tuple(norm)
        out_specs = [row_spec, pl.BlockSpec((8, R), lambda i, j: (0, 0))]
        out_shape = [jax.ShapeDtypeStruct((T, R), F32), jax.ShapeDtypeStruct((8, R), F32)]
    return _call(
        body, name=name, grid=(T // tm, nj), in_specs=in_specs, out_specs=out_specs, out_shape=out_shape,
        scratch_shapes=[pltpu.VMEM((tm, R), F32)], args=args, rider=rider)


def _wgrad_cols(n, dy, *, flat, nb, nbk, name, rider=None):
    T, D = n.shape
    bn = dy.shape[1] // nb if flat else dy.shape[2]
    tt = _tile(T, TOKEN_TILE)
    nt = T // tt

    def body(n_ref, dy_ref, o_ref, acc_ref):
        t = pl.program_id(1)

        @pl.when(t == 0)
        def _():
            acc_ref[...] = jnp.zeros_like(acc_ref)

        nv = n_ref[...]
        for k in range(nbk):
            d = dy_ref[:, k * bn:(k + 1) * bn] if flat else dy_ref[k]
            acc_ref[k] += _dot_tn(nv, d)

        @pl.when(t == nt - 1)
        def _():
            o_ref[...] = acc_ref[...]

    if flat:
        dy_spec = pl.BlockSpec((tt, nbk * bn), lambda j, t: (t, j))
    else:
        dy_spec = pl.BlockSpec((nbk, tt, bn), lambda j, t: (j, t, 0))
    return _call(
        body, name=name, grid=(nb // nbk, nt),
        in_specs=[pl.BlockSpec((tt, D), lambda j, t: (t, 0)), dy_spec],
        out_specs=pl.BlockSpec((nbk, D, bn), lambda j, t: (j, 0, 0)),
        out_shape=jax.ShapeDtypeStruct((nb, D, bn), F32),
        scratch_shapes=[pltpu.VMEM((nbk, D, bn), F32)], args=(n, dy), rider=rider)


def _wgrad_rows(xa, dh, *, flat, tk, name, rider=None):
    T, D = dh.shape
    nk = xa.shape[1] // tk if flat else xa.shape[0]
    tt = _tile(T, TOKEN_TILE)
    nt = T // tt

    def body(x_ref, dh_ref, o_ref, acc_ref):
        t = pl.program_id(1)

        @pl.when(t == 0)
        def _():
            acc_ref[...] = jnp.zeros_like(acc_ref)

        xv = x_ref[...] if flat else x_ref[0]
        acc_ref[...] += _dot_tn(xv, dh_ref[...].astype(BF16))

        @pl.when(t == nt - 1)
        def _():
            o_ref[...] = acc_ref[...]

    x_spec = pl.BlockSpec((tt, tk), lambda j, t: (t, j)) if flat else pl.BlockSpec((1, tt, tk), lambda j, t: (j, t, 0))
    return _call(
        body, name=name, grid=(nk, nt),
        in_specs=[x_spec, pl.BlockSpec((tt, D), lambda j, t: (t, 0))],
        out_specs=pl.BlockSpec((tk, D), lambda j, t: (j, 0)),
        out_shape=jax.ShapeDtypeStruct((nk * tk, D), F32),
        scratch_shapes=[pltpu.VMEM((tk, D), F32)], args=(xa, dh), rider=rider)


def _matmul_residual(xa, w, res, name):
    T, K = xa.shape
    D = w.shape[1]
    tm = _tile(T, TOKEN_TILE)

    def body(x_ref, w_ref, r_ref, o_ref):
        o_ref[...] = r_ref[...] + _dot(x_ref[...], w_ref[...])

    return pl.pallas_call(
        body, name=name, grid=(T // tm,),
        in_specs=[pl.BlockSpec((tm, K), lambda i: (i, 0)), pl.BlockSpec((K, D), lambda i: (0, 0)),
                  pl.BlockSpec((tm, D), lambda i: (i, 0))],
        out_specs=pl.BlockSpec((tm, D), lambda i: (i, 0)),
        out_shape=jax.ShapeDtypeStruct((T, D), F32),
        compiler_params=_params(("arbitrary",)),
    )(xa, w, res)


def _gmlp_gate(z, ws, bst, gv, G, gd):
    D = G * gd
    u = _gelu(z[:, :D].astype(F32))
    v = _gelu(z[:, D:].astype(F32))
    vh, r = _rms(v, None)
    vn = (vh * gv).astype(BF16)
    return u, v, vh, r, vn


def _gmlp_forward(z, ws, bst, gv, w_out, x, *, name, rider=None):
    T, D2 = z.shape
    D = D2 // 2
    G = ws.shape[0]
    gd = D // G
    tb = _tile(T, 256)
    nblk = tb // GMLP_BLOCK

    def body(z_ref, ws_ref, b_ref, gv_ref, wo_ref, x_ref, gated_ref, h_ref):
        u, _, _, _, vn = _gmlp_gate(z_ref[...], None, None, gv_ref[...], G, gd)
        for n in range(nblk):
            rows = slice(n * GMLP_BLOCK, (n + 1) * GMLP_BLOCK)
            for gi in range(G):
                cols = slice(gi * gd, (gi + 1) * gd)
                s = _dot(ws_ref[gi], vn[rows, cols]) + b_ref[:, gi:gi + 1]
                gated_ref[rows, cols] = (u[rows, cols] * s).astype(BF16)
        h_ref[...] = x_ref[...] + _dot(gated_ref[...], wo_ref[...])

    return _call(
        body, name=name, grid=(T // tb,),
        in_specs=[pl.BlockSpec((tb, D2), lambda i: (i, 0)), pl.BlockSpec(ws.shape, lambda i: (0, 0, 0)),
                  pl.BlockSpec(bst.shape, lambda i: (0, 0)), pl.BlockSpec((1, D), lambda i: (0, 0)),
                  pl.BlockSpec((D, D), lambda i: (0, 0)), pl.BlockSpec((tb, D), lambda i: (i, 0))],
        out_specs=[pl.BlockSpec((tb, D), lambda i: (i, 0)), pl.BlockSpec((tb, D), lambda i: (i, 0))],
        out_shape=[jax.ShapeDtypeStruct((T, D), BF16), jax.ShapeDtypeStruct((T, D), F32)],
        args=(z, ws, bst, gv, w_out, x), rider=rider)


def _gmlp_backward(z, dgated, ws, bst, gv, mask, *, name, rider=None):
    T, D2 = z.shape
    D = D2 // 2
    G = ws.shape[0]
    gd = D // G
    tb = _tile(T, 256)
    nblk = tb // GMLP_BLOCK

    def body(z_ref, dg_ref, ws_ref, b_ref, gv_ref, mask_ref, dz_ref, dws_ref, db_ref, dgv_ref, dvn_ref):
        @pl.when(pl.program_id(0) == 0)
        def _():
            dws_ref[...] = jnp.zeros_like(dws_ref)
            db_ref[...] = jnp.zeros_like(db_ref)
            dgv_ref[...] = jnp.zeros_like(dgv_ref)

        zf = z_ref[...]
        u, v, vh, r, vn = _gmlp_gate(zf, None, None, gv_ref[...], G, gd)
        dg = dg_ref[...].astype(F32)
        for n in range(nblk):
            rows = slice(n * GMLP_BLOCK, (n + 1) * GMLP_BLOCK)
            for gi in range(G):
                cols = slice(gi * gd, (gi + 1) * gd)
                vblk = vn[rows, cols]
                s = _dot(ws_ref[gi], vblk) + b_ref[:, gi:gi + 1]
                dgb = dg[rows, cols]
                ds = dgb * u[rows, cols]
                dsb = ds.astype(BF16)
                dz_ref[rows, cols] = (dgb * s * _gelu_grad(zf[rows, cols].astype(F32))).astype(BF16)
                dvn_ref[rows, cols] = _dot_tn(ws_ref[gi], dsb)
                dws_ref[gi] += _dot_nt(dsb, vblk) * mask_ref[...]
                db_ref[:, gi:gi + 1] += jnp.sum(ds, axis=1, keepdims=True)
        dvn = dvn_ref[...]
        dgv_ref[0:1, :] += jnp.sum(dvn * vh, axis=0, keepdims=True)
        dn = dvn * gv_ref[...]
        dv = r * (dn - vh * jnp.mean(dn * vh, axis=-1, keepdims=True))
        dz_ref[:, D:] = (dv * _gelu_grad(zf[:, D:].astype(F32))).astype(BF16)

    return _call(
        body, name=name, grid=(T // tb,),
        in_specs=[pl.BlockSpec((tb, D2), lambda i: (i, 0)), pl.BlockSpec((tb, D), lambda i: (i, 0)),
                  pl.BlockSpec(ws.shape, lambda i: (0, 0, 0)), pl.BlockSpec(bst.shape, lambda i: (0, 0)),
                  pl.BlockSpec((1, D), lambda i: (0, 0)), pl.BlockSpec(mask.shape, lambda i: (0, 0))],
        out_specs=[pl.BlockSpec((tb, D2), lambda i: (i, 0)), pl.BlockSpec(ws.shape, lambda i: (0, 0, 0)),
                   pl.BlockSpec(bst.shape, lambda i: (0, 0)), pl.BlockSpec((8, D), lambda i: (0, 0))],
        out_shape=[jax.ShapeDtypeStruct((T, D2), BF16), jax.ShapeDtypeStruct(ws.shape, F32),
                   jax.ShapeDtypeStruct(bst.shape, F32), jax.ShapeDtypeStruct((8, D), F32)],
        scratch_shapes=[pltpu.VMEM((tb, D), F32)], args=(z, dgated, ws, bst, gv, mask), rider=rider)


def _shift_rows(x, k):
    return pltpu.roll(x, k % x.shape[0], axis=0)


def _conv3(ext, cw):
    return (cw[0:1] * _shift_rows(ext, 2)[8:] + cw[1:2] * _shift_rows(ext, 1)[8:] + cw[2:3] * ext[8:])


def _ffn_forward(a, cw, cb, wd, h, seq, *, name, rider=None):
    _, T, F = a.shape
    D = h.shape[1]
    tm = _tile(seq, TOKEN_TILE)
    hb = tm // 16

    def body(a_ref, ap_ref, cw_ref, cb_ref, wd_ref, h_ref, act_ref, c_ref, o_ref, acc_ref):
        i, j = pl.program_id(0), pl.program_id(1)
        keep = ((i * tm) % seq != 0).astype(F32)

        def conv(b):
            ext = jnp.concatenate([ap_ref[b, 8:16].astype(F32) * keep, a_ref[b].astype(F32)], axis=0)
            return _conv3(ext, cw_ref[b]) + cb_ref[b]

        up, gate = conv(j), conv(j + 4)
        c_ref[j] = up.astype(BF16)
        c_ref[j + 4] = gate.astype(BF16)
        act = (gate * _sigmoid(gate) * up).astype(BF16)
        act_ref[0] = act

        @pl.when(j == 0)
        def _():
            acc_ref[...] = h_ref[...]

        acc_ref[...] += _dot(act, wd_ref[0])

        @pl.when(j == 3)
        def _():
            o_ref[...] = acc_ref[...]

    return _call(
        body, name=name, grid=(T // tm, 4),
        in_specs=[pl.BlockSpec((8, tm, F), lambda i, j: (0, i, 0)),
                  pl.BlockSpec((8, 16, F), lambda i, j: (0, jnp.maximum(i * hb - 1, 0), 0)),
                  pl.BlockSpec((8, 3, F), lambda i, j: (0, 0, 0)), pl.BlockSpec((8, 1, F), lambda i, j: (0, 0, 0)),
                  pl.BlockSpec((1, F, D), lambda i, j: (j, 0, 0)), pl.BlockSpec((tm, D), lambda i, j: (i, 0))],
        out_specs=[pl.BlockSpec((1, tm, F), lambda i, j: (j, i, 0)), pl.BlockSpec((8, tm, F), lambda i, j: (0, i, 0)),
                   pl.BlockSpec((tm, D), lambda i, j: (i, 0))],
        out_shape=[jax.ShapeDtypeStruct((4, T, F), BF16), jax.ShapeDtypeStruct((8, T, F), BF16),
                   jax.ShapeDtypeStruct((T, D), F32)],
        scratch_shapes=[pltpu.VMEM((tm, D), F32)], args=(a, a, cw, cb, wd, h), rider=rider)


def _ffn_backward(dh, c, a, cw, wd, seq, *, name, rider=None):
    _, T, F = a.shape
    D = dh.shape[1]
    tm = _tile(seq, TOKEN_TILE)
    hb = tm // 16
    nt = T // tm

    def body(dh_ref, dhn_ref, cu_ref, cg_ref, cun_ref, cgn_ref, au_ref, ag_ref, cw_ref, wd_ref, da_ref, st_ref):
        i, j = pl.program_id(0), pl.program_id(1)
        keep_next = (((i + 1) * tm) % seq != 0).astype(F32)

        @pl.when((i == 0) & (j == 0))
        def _():
            st_ref[...] = jnp.zeros_like(st_ref)

        dhe = jnp.concatenate([dh_ref[...], dhn_ref[...] * keep_next], axis=0).astype(BF16)
        dact = _dot_nt(dhe, wd_ref[0])
        up = jnp.concatenate([cu_ref[0].astype(F32), cun_ref[0, 0:8].astype(F32)], axis=0)
        gate = jnp.concatenate([cg_ref[0].astype(F32), cgn_ref[0, 0:8].astype(F32)], axis=0)
        sg = _sigmoid(gate)
        gs = gate * sg
        d_up = dact * gs
        d_gate = dact * up * (sg + gs * (1.0 - sg))

        def finish(b, a_ref, dc):
            w = cw_ref[b]
            dm, u1, u2 = dc[:tm], _shift_rows(dc, -1)[:tm], _shift_rows(dc, -2)[:tm]
            da_ref[b] = (w[2:3] * dm + w[1:2] * u1 + w[0:1] * u2).astype(BF16)
            av = a_ref[0].astype(F32)
            st_ref[b, 0:1, :] += jnp.sum(u2 * av, axis=0, keepdims=True)
            st_ref[b, 1:2, :] += jnp.sum(u1 * av, axis=0, keepdims=True)
            st_ref[b, 2:3, :] += jnp.sum(dm * av, axis=0, keepdims=True)
            st_ref[b, 3:4, :] += jnp.sum(dm, axis=0, keepdims=True)

        finish(j, au_ref, d_up)
        finish(j + 4, ag_ref, d_gate)

    nxt = lambda i: jnp.minimum((i + 1) * hb, T // 16 - 1)
    return _call(
        body, name=name, grid=(nt, 4),
        in_specs=[pl.BlockSpec((tm, D), lambda i, j: (i, 0)),
                  pl.BlockSpec((8, D), lambda i, j: (jnp.minimum((i + 1) * (tm // 8), T // 8 - 1), 0)),
                  pl.BlockSpec((1, tm, F), lambda i, j: (j, i, 0)), pl.BlockSpec((1, tm, F), lambda i, j: (j + 4, i, 0)),
                  pl.BlockSpec((1, 16, F), lambda i, j: (j, nxt(i), 0)),
                  pl.BlockSpec((1, 16, F), lambda i, j: (j + 4, nxt(i), 0)),
                  pl.BlockSpec((1, tm, F), lambda i, j: (j, i, 0)), pl.BlockSpec((1, tm, F), lambda i, j: (j + 4, i, 0)),
                  pl.BlockSpec((8, 3, F), lambda i, j: (0, 0, 0)),
                  pl.BlockSpec((1, F, D), lambda i, j: (j, 0, 0))],
        out_specs=[pl.BlockSpec((8, tm, F), lambda i, j: (0, i, 0)), pl.BlockSpec((8, 8, F), lambda i, j: (0, 0, 0))],
        out_shape=[jax.ShapeDtypeStruct((8, T, F), BF16), jax.ShapeDtypeStruct((8, 8, F), F32)],
        args=(dh, dh, c, c, c, c, a, a, cw, wd), rider=rider)


def _ffn_fused_forward(h, g, w_in, cw, cb, wd, seq, *, name, rider=None):
    T, D = h.shape
    F = w_in.shape[2]
    tm = _tile(seq, TOKEN_TILE)

    def body(h_ref, g_ref, wu_ref, wg_ref, cw_ref, cb_ref, wd_ref,
             au_ref, ag_ref, cu_ref, cg_ref, act_ref, n_ref, o_ref, acc_ref, carry_ref, eu_ref, eg_ref):
        i, j = pl.program_id(0), pl.program_id(1)
        keep = ((i * tm) % seq != 0).astype(F32)

        @pl.when((i == 0) & (j == 0))
        def _():
            carry_ref[...] = jnp.zeros_like(carry_ref)

        @pl.when(j == 0)
        def _():
            xh, _ = _rms(h_ref[...], None)
            n_ref[...] = (xh * g_ref[...]).astype(BF16)
            acc_ref[...] = h_ref[...]

        n = n_ref[...]

        def project(b, w_ref, a_ref, ext_ref):
            a = _dot(n, w_ref[0]).astype(BF16)
            a_ref[0] = a
            ext_ref[0:8, :] = carry_ref[b] * keep
            ext_ref[8:, :] = a.astype(F32)
            carry_ref[b] = ext_ref[tm:tm + 8, :]

        project(j, wu_ref, au_ref, eu_ref)
        project(j + 4, wg_ref, ag_ref, eg_ref)

        def conv(b, ext_ref, r):
            x, w = ext_ref[r:r + ROWS + 8, :], cw_ref[b]
            return (w[0:1] * _shift_rows(x, 2) + w[1:2] * _shift_rows(x, 1) + w[2:3] * x)[8:] + cb_ref[b]

        for r in range(0, tm, ROWS):
            up, gate = conv(j, eu_ref, r), conv(j + 4, eg_ref, r)
            cu_ref[0, r:r + ROWS, :] = up.astype(BF16)
            cg_ref[0, r:r + ROWS, :] = gate.astype(BF16)
            act_ref[0, r:r + ROWS, :] = (gate * _sigmoid(gate) * up).astype(BF16)
        acc_ref[...] += _dot(act_ref[0], wd_ref[0])

        @pl.when(j == 3)
        def _():
            o_ref[...] = acc_ref[...]

    blk = pl.BlockSpec((1, tm, F), lambda i, j: (j, i, 0))
    row = pl.BlockSpec((tm, D), lambda i, j: (i, 0))
    half = jax.ShapeDtypeStruct((4, T, F), BF16)
    return _call(
        body, name=name, grid=(T // tm, 4),
        in_specs=[row, pl.BlockSpec((1, D), lambda i, j: (0, 0)),
                  pl.BlockSpec((1, D, F), lambda i, j: (j, 0, 0)), pl.BlockSpec((1, D, F), lambda i, j: (j + 4, 0, 0)),
                  pl.BlockSpec((8, 3, F), lambda i, j: (0, 0, 0)), pl.BlockSpec((8, 1, F), lambda i, j: (0, 0, 0)),
                  pl.BlockSpec((1, F, D), lambda i, j: (j, 0, 0))],
        out_specs=[blk, blk, blk, blk, blk, row, row],
        out_shape=[half, half, half, half, half, jax.ShapeDtypeStruct((T, D), BF16), jax.ShapeDtypeStruct((T, D), F32)],
        scratch_shapes=[pltpu.VMEM((tm, D), F32), pltpu.VMEM((8, 8, F), F32),
                        pltpu.VMEM((tm + 8, F), F32), pltpu.VMEM((tm + 8, F), F32)],
        args=(h, g, w_in, w_in, cw, cb, wd), rider=rider)


def _ffn_fused_backward(dh, cu, cg, au, ag, cw, wd, w_in, h, g, seq, *, name, rider=None):
    T, D = dh.shape
    F = wd.shape[1]
    tm = _tile(seq, TOKEN_TILE)
    hb = tm // 16
    nt = T // tm

    def body(dh_ref, dhn_ref, cu_ref, cg_ref, cun_ref, cgn_ref, au_ref, ag_ref, cw_ref, wd_ref, wu_ref, wg_ref,
             h_ref, g_ref, da_ref, st_ref, o_ref, dg_ref, acc_ref, dact_ref, du_ref, dgt_ref):
        i, j = pl.program_id(0), pl.program_id(1)
        keep_next = (((i + 1) * tm) % seq != 0).astype(F32)

        @pl.when((i == 0) & (j == 0))
        def _():
            st_ref[...] = jnp.zeros_like(st_ref)
            dg_ref[...] = jnp.zeros_like(dg_ref)

        @pl.when(j == 0)
        def _():
            acc_ref[...] = jnp.zeros_like(acc_ref)

        dhe = jnp.concatenate([dh_ref[...], dhn_ref[...] * keep_next], axis=0).astype(BF16)
        dact_ref[...] = _dot_nt(dhe, wd_ref[0])

        for r in range(0, tm + 8, ROWS):
            if r < tm:
                rows = slice(r, r + ROWS)
                up, gate = cu_ref[0, rows, :].astype(F32), cg_ref[0, rows, :].astype(F32)
            else:
                rows = slice(tm, tm + 8)
                up, gate = cun_ref[0, 0:8, :].astype(F32), cgn_ref[0, 0:8, :].astype(F32)
            dact = dact_ref[rows, :]
            sg = _sigmoid(gate)
            gs = gate * sg
            du_ref[rows, :] = dact * gs
            dgt_ref[rows, :] = dact * up * (sg + gs * (1.0 - sg))

        def finish(b, a_ref, w_ref, dc_ref):
            w = cw_ref[b]
            sums = [jnp.zeros((8, F), F32) for _ in range(4)]
            fold = lambda t: jnp.sum(t.reshape(ROWS // 8, 8, F), axis=0)
            for r in range(0, tm, ROWS):
                dc = dc_ref[r:r + ROWS + 8, :]
                dm, u1, u2 = dc[:ROWS], _shift_rows(dc, -1)[:ROWS], _shift_rows(dc, -2)[:ROWS]
                da_ref[b, r:r + ROWS, :] = (w[2:3] * dm + w[1:2] * u1 + w[0:1] * u2).astype(BF16)
                av = a_ref[0, r:r + ROWS, :].astype(F32)
                for k, t in enumerate((u2 * av, u1 * av, dm * av, dm)):
                    sums[k] = sums[k] + fold(t)
            for k in range(4):
                st_ref[b, k:k + 1, :] += jnp.sum(sums[k], axis=0, keepdims=True)
            acc_ref[...] += _dot_nt(da_ref[b], w_ref[0])

        finish(j, au_ref, wu_ref, du_ref)
        finish(j + 4, ag_ref, wg_ref, dgt_ref)

        @pl.when(j == 3)
        def _():
            acc = acc_ref[...]
            xh, r = _rms(h_ref[...], None)
            dg_ref[0:1, :] += jnp.sum(acc * xh, axis=0, keepdims=True)
            dn = acc * g_ref[...]
            o_ref[...] = dh_ref[...] + r * (dn - xh * jnp.mean(dn * xh, axis=-1, keepdims=True))

    nxt = lambda i: jnp.minimum((i + 1) * hb, T // 16 - 1)
    blk = pl.BlockSpec((1, tm, F), lambda i, j: (j, i, 0))
    halo = pl.BlockSpec((1, 16, F), lambda i, j: (j, nxt(i), 0))
    row = pl.BlockSpec((tm, D), lambda i, j: (i, 0))
    return _call(
        body, name=name, grid=(nt, 4),
        in_specs=[row, pl.BlockSpec((8, D), lambda i, j: (jnp.minimum((i + 1) * (tm // 8), T // 8 - 1), 0)),
                  blk, blk, halo, halo, blk, blk,
                  pl.BlockSpec((8, 3, F), lambda i, j: (0, 0, 0)), pl.BlockSpec((1, F, D), lambda i, j: (j, 0, 0)),
                  pl.BlockSpec((1, D, F), lambda i, j: (j, 0, 0)), pl.BlockSpec((1, D, F), lambda i, j: (j + 4, 0, 0)),
                  row, pl.BlockSpec((1, D), lambda i, j: (0, 0))],
        out_specs=[pl.BlockSpec((8, tm, F), lambda i, j: (0, i, 0)), pl.BlockSpec((8, 8, F), lambda i, j: (0, 0, 0)),
                   row, pl.BlockSpec((8, D), lambda i, j: (0, 0))],
        out_shape=[jax.ShapeDtypeStruct((8, T, F), BF16), jax.ShapeDtypeStruct((8, 8, F), F32),
                   jax.ShapeDtypeStruct((T, D), F32), jax.ShapeDtypeStruct((8, D), F32)],
        scratch_shapes=[pltpu.VMEM((tm, D), F32), pltpu.VMEM((tm + 8, F), F32), pltpu.VMEM((tm + 8, F), F32),
                        pltpu.VMEM((tm + 8, F), F32)],
        args=(dh, dh, cu, cg, cu, cg, au, ag, cw, wd, w_in, w_in, h, g), rider=rider)


def _rel_onehot():
    r = lax.broadcasted_iota(jnp.int32, (REL_PAD, SKEW), 0)
    n = lax.broadcasted_iota(jnp.int32, (REL_PAD, SKEW), 1)
    off = jnp.where(n >= WIN, n - SKEW, n)
    idx = jnp.minimum(PAD - off, REL_CLIP) + REL_CLIP
    return (r == idx).astype(BF16)


def _skew(x, sign):
    row = lax.broadcasted_iota(jnp.int32, x.shape, 0)
    for b in range(7):
        x = jnp.where((row >> b) & 1 == 1, pltpu.roll(x, (sign * (1 << b)) % SKEW, axis=1), x)
    return x


def _bias_build(rel, name, rider=None):
    H = rel.shape[0]

    def body(rel_ref, o_ref):
        oh = _rel_onehot()
        hi, mid, lo = _split3(rel_ref[...])
        base = _dot(hi, oh) + _dot(mid, oh) + _dot(lo, oh)
        mine = lax.broadcasted_iota(jnp.int32, (H, 1), 0) == pl.program_id(0)
        row = jnp.sum(jnp.where(mine, base, 0.0), axis=0, keepdims=True)
        q = lax.broadcasted_iota(jnp.int32, (Q_TILE, WIN), 0)
        k = lax.broadcasted_iota(jnp.int32, (Q_TILE, WIN), 1)
        ok = ((q < CHUNK) & (k < WIN - CHUNK)) | ((q >= CHUNK) & (k >= CHUNK))
        t = _skew(jnp.broadcast_to(row, (Q_TILE, SKEW)), 1)
        o_ref[0] = jnp.where(ok, t[:, :WIN], NEG_INF)

    return _call(
        body, name=name, grid=(H,), in_specs=[pl.BlockSpec((H, REL_PAD), lambda h: (0, 0))],
        out_specs=pl.BlockSpec((1, Q_TILE, WIN), lambda h: (h, 0, 0)),
        out_shape=jax.ShapeDtypeStruct((H, Q_TILE, WIN), F32), args=(rel,), rider=rider)


def _bias_reduce(dbias, name):
    H = dbias.shape[0]

    def body(d_ref, o_ref, e_ref):
        oh = _rel_onehot()
        for hd in range(H):
            x = jnp.concatenate([d_ref[hd], jnp.zeros((Q_TILE, SKEW - WIN), F32)], axis=1)
            e_ref[hd:hd + 1, :] = jnp.sum(_skew(x, -1), axis=0, keepdims=True)
        hi, mid, lo = _split3(e_ref[...])
        o_ref[...] = _dot_nt(hi, oh) + _dot_nt(mid, oh) + _dot_nt(lo, oh)

    return pl.pallas_call(
        body, name=name, out_shape=jax.ShapeDtypeStruct((H, REL_PAD), F32),
        in_specs=[pl.BlockSpec(memory_space=pltpu.VMEM)], out_specs=pl.BlockSpec(memory_space=pltpu.VMEM),
        scratch_shapes=[pltpu.VMEM((H, SKEW), F32)],
        compiler_params=_params(),
    )(dbias)


def _pair_stack(xp, even):
    z = jnp.zeros_like(xp)
    return jnp.concatenate([jnp.where(even, xp, z), jnp.where(even, z, xp)], axis=0)


def _pair_merge(y, even):
    return jnp.where(even, y[:Q_TILE], y[Q_TILE:])


def _strip_probs(s_ref, b_ref, pp, r, valid):
    hb, hr = divmod(r, Q_TILE)
    s = s_ref[pp, r:r + STRIP, :] + b_ref[2 * pp + hb, hr:hr + STRIP, :]
    s = jnp.where(valid, s, NEG_INF)
    e = jnp.exp(s - jnp.max(s, axis=-1, keepdims=True))
    return e * (1.0 / jnp.sum(e, axis=-1, keepdims=True))


def _fill_padded(dst_ref, src_ref):
    dst_ref[0:PAD, :] = jnp.zeros((PAD, dst_ref.shape[1]), dst_ref.dtype)
    dst_ref[PAD:, :] = src_ref[...]


def _attn_specs(B, S, D, lanes):
    nt = S // Q_TILE
    q_spec = pl.BlockSpec((Q_TILE, lanes), lambda g, b, i: (b * nt + i, g))
    k_spec = pl.BlockSpec((S, lanes), lambda g, b, i: (b, g))
    v_spec = pl.BlockSpec((S, lanes), lambda g, b, i: (b, D // lanes + g))
    bias_spec = pl.BlockSpec((lanes // HEAD_DIM, Q_TILE, WIN), lambda g, b, i: (g, 0, 0))
    return nt, q_spec, k_spec, v_spec, bias_spec


def _attn_forward(q, kv, bias, S, *, name, rider=None):
    T, D = q.shape
    B = T // S
    lanes = min(FWD_HEADS_PER_STEP * HEAD_DIM, D)
    nt, q_spec, k_spec, v_spec, bias_spec = _attn_specs(B, S, D, lanes)

    npairs = lanes // (2 * HEAD_DIM)

    def body(q_ref, k_ref, v_ref, b_ref, o_ref, kp_ref, vp_ref, s_ref, p_ref):
        i = pl.program_id(2)

        @pl.when(i == 0)
        def _():
            _fill_padded(kp_ref, k_ref)
            _fill_padded(vp_ref, v_ref)

        start = pl.multiple_of(i * Q_TILE, Q_TILE)
        even = lax.broadcasted_iota(jnp.int32, (1, 2 * HEAD_DIM), 1) < HEAD_DIM
        valid = lax.broadcasted_iota(jnp.int32, (STRIP, WIN), 1) >= PAD - i * Q_TILE
        pair_cols = [slice(pp * 2 * HEAD_DIM, (pp + 1) * 2 * HEAD_DIM) for pp in range(npairs)]
        for pp, cols in enumerate(pair_cols):
            s_ref[pp] = _dot_nt(_pair_stack(q_ref[:, cols], even), kp_ref[pl.ds(start, WIN), cols])
        for pp in range(npairs):
            for r in range(0, 2 * Q_TILE, STRIP):
                p = _strip_probs(s_ref, b_ref, pp, r, valid)
                p_ref[pp, r:r + STRIP, :] = p.astype(BF16)
        for pp, cols in enumerate(pair_cols):
            o_ref[:, cols] = _pair_merge(_dot(p_ref[pp], vp_ref[pl.ds(start, WIN), cols]), even).astype(BF16)

    return _call(
        body, name=name, grid=(D // lanes, B, nt),
        in_specs=[q_spec, k_spec, v_spec, bias_spec], out_specs=q_spec,
        out_shape=jax.ShapeDtypeStruct((T, D), BF16),
        scratch_shapes=[pltpu.VMEM((S + PAD, lanes), BF16), pltpu.VMEM((S + PAD, lanes), BF16),
                        pltpu.VMEM((npairs, 2 * Q_TILE, WIN), F32), pltpu.VMEM((npairs, 2 * Q_TILE, WIN), BF16)],
        args=(q, kv, kv, bias), rider=rider)


def _attn_backward(q, kv, bias, do, S, *, name, rider=None):
    T, D = q.shape
    B = T // S
    H = D // HEAD_DIM
    lanes = min(BWD_HEADS_PER_STEP * HEAD_DIM, D)
    nt, q_spec, k_spec, v_spec, bias_spec = _attn_specs(B, S, D, lanes)
    scale = HEAD_DIM ** -0.5

    npairs = lanes // (2 * HEAD_DIM)

    def body(q_ref, k_ref, v_ref, b_ref, do_ref, dq_ref, dk_ref, dv_ref, db_ref, kp_ref, vp_ref, dka_ref, dva_ref,
             s_ref, dp_ref, p_ref, ds_ref):
        b, i = pl.program_id(1), pl.program_id(2)

        @pl.when((b == 0) & (i == 0))
        def _():
            db_ref[...] = jnp.zeros_like(db_ref)

        @pl.when(i == 0)
        def _():
            _fill_padded(kp_ref, k_ref)
            _fill_padded(vp_ref, v_ref)
            dka_ref[...] = jnp.zeros_like(dka_ref)
            dva_ref[...] = jnp.zeros_like(dva_ref)

        start = pl.multiple_of(i * Q_TILE, Q_TILE)
        even = lax.broadcasted_iota(jnp.int32, (1, 2 * HEAD_DIM), 1) < HEAD_DIM
        valid = lax.broadcasted_iota(jnp.int32, (STRIP, WIN), 1) >= PAD - i * Q_TILE
        pair_cols = [slice(pp * 2 * HEAD_DIM, (pp + 1) * 2 * HEAD_DIM) for pp in range(npairs)]
        for pp, cols in enumerate(pair_cols):
            s_ref[pp] = _dot_nt(_pair_stack(q_ref[:, cols], even), kp_ref[pl.ds(start, WIN), cols])
            dp_ref[pp] = _dot_nt(_pair_stack(do_ref[:, cols], even), vp_ref[pl.ds(start, WIN), cols])
        for pp in range(npairs):
            for r in range(0, 2 * Q_TILE, STRIP):
                hb, hr = divmod(r, Q_TILE)
                p = _strip_probs(s_ref, b_ref, pp, r, valid)
                dp = dp_ref[pp, r:r + STRIP, :]
                ds = p * (dp - jnp.sum(p * dp, axis=-1, keepdims=True))
                db_ref[2 * pp + hb, hr:hr + STRIP, :] += ds
                p_ref[pp, r:r + STRIP, :] = p.astype(BF16)
                ds_ref[pp, r:r + STRIP, :] = ds.astype(BF16)
        for pp, cols in enumerate(pair_cols):
            dsb = ds_ref[pp]
            dq_ref[:, cols] = (_pair_merge(_dot(dsb, kp_ref[pl.ds(start, WIN), cols]), even) * scale).astype(BF16)
            dka_ref[pl.ds(start, WIN), cols] += _dot_tn(dsb, _pair_stack(q_ref[:, cols], even))
            dva_ref[pl.ds(start, WIN), cols] += _dot_tn(p_ref[pp], _pair_stack(do_ref[:, cols], even))

        @pl.when(i == nt - 1)
        def _():
            dk_ref[...] = dka_ref[PAD:, :].astype(BF16)
            dv_ref[...] = dva_ref[PAD:, :].astype(BF16)

    dkv_shape = jax.ShapeDtypeStruct((T, D), BF16)
    return _call(
        body, name=name, grid=(D // lanes, B, nt),
        in_specs=[q_spec, k_spec, v_spec, bias_spec, q_spec],
        out_specs=[q_spec, k_spec, k_spec, bias_spec],
        out_shape=[jax.ShapeDtypeStruct((T, D), BF16), dkv_shape, dkv_shape,
                   jax.ShapeDtypeStruct((H, Q_TILE, WIN), F32)],
        scratch_shapes=[pltpu.VMEM((S + PAD, lanes), BF16), pltpu.VMEM((S + PAD, lanes), BF16),
                        pltpu.VMEM((S + PAD, lanes), F32), pltpu.VMEM((S + PAD, lanes), F32),
                        pltpu.VMEM((npairs, 2 * Q_TILE, WIN), F32), pltpu.VMEM((npairs, 2 * Q_TILE, WIN), F32),
                        pltpu.VMEM((npairs, 2 * Q_TILE, WIN), BF16), pltpu.VMEM((npairs, 2 * Q_TILE, WIN), BF16)],
        args=(q, kv, kv, bias, do), rider=rider)


def _loss_head(h, g, target, name):
    T, D = h.shape
    tm = _tile(T, TOKEN_TILE)

    def body(h_ref, g_ref, t_ref, dh_ref, st_ref):
        @pl.when(pl.program_id(0) == 0)
        def _():
            st_ref[...] = jnp.zeros_like(st_ref)

        xh, r = _rms(h_ref[...], None)
        err = xh * g_ref[...] - t_ref[...]
        st_ref[1:2, :] += 0.5 * jnp.sum(jnp.mean(err * err, axis=-1, keepdims=True), axis=0, keepdims=True)
        dy = err * (1.0 / D)
        st_ref[0:1, :] += jnp.sum(dy * xh, axis=0, keepdims=True)
        dn = dy * g_ref[...]
        dh_ref[...] = r * (dn - xh * jnp.mean(dn * xh, axis=-1, keepdims=True))

    row = pl.BlockSpec((tm, D), lambda i: (i, 0))
    return pl.pallas_call(
        body, name=name, grid=(T // tm,),
        in_specs=[row, pl.BlockSpec((1, D), lambda i: (0, 0)), row],
        out_specs=[row, pl.BlockSpec((8, D), lambda i: (0, 0))],
        out_shape=[jax.ShapeDtypeStruct((T, D), F32), jax.ShapeDtypeStruct((8, D), F32)],
        compiler_params=_params(("arbitrary",)),
    )(h, g, target)


def _sum_devices(arrs, name):
    n = len(arrs)

    def body(*refs):
        for a in range(n):
            s = refs[a][0]
            for k in range(1, N_DEV):
                s = s + refs[a][k]
            refs[n + a][...] = s

    vm = pl.BlockSpec(memory_space=pltpu.VMEM)
    return pl.pallas_call(
        body, name=name, out_shape=[jax.ShapeDtypeStruct(a.shape[1:], F32) for a in arrs],
        in_specs=[vm] * n, out_specs=[vm] * n, compiler_params=_params(),
    )(*arrs)


def _adamw_math(w, g, m, v):
    m = ADAM_B1 * m + (1.0 - ADAM_B1) * g
    v = ADAM_B2 * v + (1.0 - ADAM_B2) * (g * g)
    m_hat = m / (1.0 - ADAM_B1 ** ADAM_STEP)
    v_hat = v / (1.0 - ADAM_B2 ** ADAM_STEP)
    delta = -ADAM_LR * (m_hat / (jnp.sqrt(v_hat) + ADAM_EPS) + ADAM_WD * w)
    return delta, m, v


def _adamw_small(items, name):
    n = len(items)

    def body(*refs):
        for a in range(n):
            w, m, v, g = (refs[4 * a + k][...] for k in range(4))
            d, m, v = _adamw_math(w, g, m, v)
            refs[4 * n + 3 * a][...] = d
            refs[4 * n + 3 * a + 1][...] = m
            refs[4 * n + 3 * a + 2][...] = v

    vm = pl.BlockSpec(memory_space=pltpu.VMEM)
    flat = [t for it in items for t in it]
    outs = pl.pallas_call(
        body, name=name,
        out_shape=[jax.ShapeDtypeStruct(it[0].shape, F32) for it in items for _ in range(3)],
        in_specs=[vm] * (4 * n), out_specs=[vm] * (3 * n), compiler_params=_params(),
    )(*flat)
    return [tuple(outs[3 * a:3 * a + 3]) for a in range(n)]


def _adamw_big(w, m, v, owns, landeds, name, rider=None):
    L, R, C = w.shape
    tr = _tile(R, 512)
    nr = R // tr
    counts = [len(ls) for ls in landeds]

    def body(*refs):
        w_ref, m_ref, v_ref = refs[:3]
        g_ref, d_ref, mo_ref, vo_ref = refs[-4:]
        layer = pl.program_id(0)
        at = 3
        for j in range(L):
            own_ref, l_refs = refs[at], refs[at + 1:at + 1 + counts[j]]
            at += 1 + counts[j]

            @pl.when(layer == j)
            def _(own_ref=own_ref, l_refs=l_refs):
                g = own_ref[...]
                for l_ref in l_refs:
                    for k in range(l_ref.shape[0]):
                        g = g + l_ref[k].astype(F32)
                d, mn, vn = _adamw_math(w_ref[0], g, m_ref[0], v_ref[0])
                g_ref[0] = g
                d_ref[0] = d
                mo_ref[0] = mn
                vo_ref[0] = vn

    def pinned(j):
        return lambda l, i: jnp.where(l == j, i, jnp.where(l < j, 0, nr - 1))

    row = pl.BlockSpec((1, tr, C), lambda l, i: (l, i, 0))
    in_specs, args = [row, row, row], [w, m, v]
    for j in range(L):
        in_specs.append(pl.BlockSpec((tr, C), lambda l, i, p=pinned(j): (p(l, i), 0)))
        args.append(owns[j])
        for arr in landeds[j]:
            in_specs.append(pl.BlockSpec((arr.shape[0], tr, C), lambda l, i, p=pinned(j): (0, p(l, i), 0)))
            args.append(arr)
    return _call(body, name=name, grid=(L, nr), in_specs=in_specs, out_specs=[row] * 4,
                 out_shape=[jax.ShapeDtypeStruct((L, R, C), F32)] * 4, args=args, rider=rider)


def kernel(x, a_norm_g, a_w_in, a_v_norm_g, a_w_s, a_b_s, a_w_out, kv_norm_g, w_kv, b_norm_g, b_w_q, b_rel_bias, b_w_o, f_norm_g, f_w_in, f_conv_w, f_conv_b, f_w_down, final_norm_g, loss_target, m_a_norm_g, m_a_w_in, m_a_v_norm_g, m_a_w_s, m_a_b_s, m_a_w_out, m_kv_norm_g, m_w_kv, m_b_norm_g, m_b_w_q, m_b_rel_bias, m_b_w_o, m_f_norm_g, m_f_w_in, m_f_conv_w, m_f_conv_b, m_f_w_down, m_final_norm_g, v_a_norm_g, v_a_w_in, v_a_v_norm_g, v_a_w_s, v_a_b_s, v_a_w_out, v_kv_norm_g, v_w_kv, v_b_norm_g, v_b_w_q, v_b_rel_bias, v_b_w_o, v_f_norm_g, v_f_w_in, v_f_conv_w, v_f_conv_b, v_f_w_down, v_final_norm_g):
    B, S, D = x.shape
    T = B * S
    G = a_w_s.shape[1]
    H = D // HEAD_DIM
    F = f_w_in.shape[2]
    L = f_w_in.shape[0]
    dn = D // N_DEV
    xi, yi, ci = lax.axis_index("x"), lax.axis_index("y"), lax.axis_index("c")
    me = 4 * xi + 2 * yi + ci
    pos = jnp.stack([ci, 2 * xi + yi]).astype(jnp.int32)

    cast = lambda t: t.astype(BF16)
    gather = lambda *ts: _gather_rider(list(ts))
    rel = jnp.pad(b_rel_bias[0], ((0, 0), (0, REL_PAD - b_rel_bias.shape[2])))
    bias, (wa_in, norms_sh, conv_w0, conv_w1) = _bias_build(rel, "bias_build", rider=gather(
        cast(a_w_in[0]), jnp.concatenate([a_norm_g, a_v_norm_g], axis=0), f_conv_w[0], f_conv_w[1]))
    ga = jnp.transpose(norms_sh, (1, 0, 2)).reshape(2, D)
    g_a, g_av = ga[0:1], ga[1:2]

    x2 = x.reshape(T, D)
    tgt = loss_target.reshape(T, D)
    pc = jnp.arange(GMLP_BLOCK) // CHUNK
    mask = (pc[:, None] >= pc[None, :]).astype(F32)
    ws = (a_w_s[0] * mask[None]).astype(BF16)
    bst = jnp.transpose(a_b_s[0])
    four = lambda t: t.reshape((4, 2) + t.shape[1:])

    w_in0_sh = cast(f_w_in[0])
    (z, n_a), (wa_out, w_in0_top) = _norm_matmul(x2, g_a, wa_in, flat=True, nbk=4, name="gmlp_in",
                                                 rider=gather(cast(a_w_out[0]), w_in0_sh[:D // 2]))
    wa_out = wa_out.reshape(D, D)
    (gated, h1), (w_in0_bottom, wf_down0) = _gmlp_forward(z, ws, bst, g_av, wa_out, x2, name="gmlp_mix",
                                                          rider=gather(w_in0_sh[D // 2:], cast(f_w_down[0])))
    w_in0 = jnp.concatenate([w_in0_top, w_in0_bottom], axis=1)
    cw0, cb0, wd0 = conv_w0, f_conv_b[0].reshape(8, 1, F), wf_down0.reshape(4, F, D)
    (au0, ag0, cu0, cg0, act0, n_f0, h2), (wkv, wq, w_in1) = _ffn_fused_forward(
        h1, f_norm_g[0:1], w_in0, cw0, cb0, wd0, S, name="ffn0_fwd",
        rider=gather(cast(w_kv), cast(b_w_q[0]), cast(f_w_in[1])))
    wq = wq.reshape(D, D)
    kv, n_kv = _norm_matmul(h2, kv_norm_g.reshape(1, D), wkv, flat=True, nbk=4, name="kv_proj")
    q, n_q = _norm_matmul(h2, b_norm_g, wq.reshape(1, D, D), flat=True, nbk=1, name="q_proj", scale=HEAD_DIM ** -0.5)
    o, (wo, wf_down1) = _attn_forward(q, kv, bias, S, name="attn", rider=gather(cast(b_w_o[0]), cast(f_w_down[1])))
    wo = wo.reshape(D, D)
    cw1, cb1, wd1 = conv_w1, f_conv_b[1].reshape(8, 1, F), wf_down1.reshape(4, F, D)
    h3 = _matmul_residual(o, wo, h2, "attn_out")
    au1, ag1, cu1, cg1, act1, n_f1, h4 = _ffn_fused_forward(h3, f_norm_g[1:2], w_in1, cw1, cb1, wd1, S, name="ffn1_fwd")

    sums, from_chips = {}, {}

    def sibling_sums(names, parts, landed):
        for nm, p, l in zip(names, parts, landed):
            sums[nm] = _sibling_sum(p, l, pos, "grad_sibling_sum_" + nm)

    def chip_rider(*names):
        return _chip_rider([sums[nm][1] for nm in names])

    dh4, st_final = _loss_head(h4, final_norm_g.reshape(1, D), tgt, "loss_head")
    g_wd1 = _wgrad_rows(act1, dh4, flat=False, tk=F, name="ffn1_dwdown")
    parts = [four(g_wd1.reshape(8, F // 2, D))]
    (da1, st_conv1, dh3, st_f1), landed = _ffn_fused_backward(
        dh4, cu1, cg1, au1, ag1, cw1, wd1, w_in1, h3, f_norm_g[1:2], S, name="ffn1_bwd", rider=_sibling_rider(parts))
    sibling_sums(["wd1"], parts, landed)
    g_win1, (from_chips["wd1"],) = _wgrad_cols(n_f1, da1, flat=False, nb=8, nbk=2, name="ffn1_dwin",
                                               rider=chip_rider("wd1"))
    d_o = _matmul_nt(dh3, wo.reshape(1, D, D), flat=True, nbk=1, name="attn_out_dx")
    parts = [four(g_win1)]
    g_wo, landed = _wgrad_rows(o, dh3, flat=True, tk=_tile(D, 512), name="attn_out_dw", rider=_sibling_rider(parts))
    sibling_sums(["win1"], parts, landed)
    (dq, dk, dv, dbias), (from_chips["win1"],) = _attn_backward(
        q, kv, bias, d_o, S, name="attn_bwd", rider=chip_rider("win1"))
    g_rel = _bias_reduce(dbias, "bias_reduce")
    g_wq = _wgrad_cols(n_q, dq, flat=True, nb=1, nbk=1, name="q_dw")
    dh2, st_b = _matmul_nt(dq, wq.reshape(1, D, D), flat=True, nbk=1, name="q_dx", norm=(h2, b_norm_g, dh3))
    dkv = jnp.concatenate([dk, dv], axis=-1)
    g_wkv = _wgrad_cols(n_kv, dkv, flat=True, nb=8, nbk=4, name="kv_dw")
    parts = [four(g_wo.reshape(8, dn, D)), four(g_wq.reshape(8, dn, D)), four(g_wkv)]
    (dh2, st_kv), landed = _matmul_nt(dkv, wkv, flat=True, nbk=4, name="kv_dx",
                                      norm=(h2, kv_norm_g.reshape(1, D), dh2), rider=_sibling_rider(parts))
    sibling_sums(["wo", "wq", "wkv"], parts, landed)
    g_wd0, (from_chips["wo"], from_chips["wq"], from_chips["wkv"]) = _wgrad_rows(
        act0, dh2, flat=False, tk=F, name="ffn0_dwdown", rider=chip_rider("wo", "wq", "wkv"))
    parts = [four(g_wd0.reshape(8, F // 2, D))]
    (da0, st_conv0, dh1, st_f0), landed = _ffn_fused_backward(
        dh2, cu0, cg0, au0, ag0, cw0, wd0, w_in0, h1, f_norm_g[0:1], S, name="ffn0_bwd", rider=_sibling_rider(parts))
    sibling_sums(["wd0"], parts, landed)
    g_win0, (ce,) = _wgrad_cols(n_f0, da0, flat=False, nb=8, nbk=2, name="ffn0_dwin", rider=chip_rider("wd0"))
    from_chips["wd0"] = [ce]
    dgated = _matmul_nt(dh1, wa_out.reshape(1, D, D), flat=True, nbk=1, name="gmlp_out_dx")
    parts = [four(g_win0)]
    g_wa_out, landed = _wgrad_rows(gated, dh1, flat=True, tk=_tile(D, 512), name="gmlp_out_dw",
                                   rider=_sibling_rider(parts))
    sibling_sums(["win0"], parts, landed)
    parts = [four(g_wa_out.reshape(8, dn, D))]
    (dz, g_ws, g_bst, st_av), (ce_win0_a, landed) = _gmlp_backward(
        z, dgated, ws, bst, g_av, mask, name="gmlp_bwd",
        rider=_join_riders([_chip_rider([sums["win0"][1]], ks=(1, 2)), _sibling_rider(parts)]))
    sibling_sums(["wa_out"], parts, [landed])
    g_wa_in, (ce_win0_b, ce) = _wgrad_cols(n_a, dz, flat=True, nb=8, nbk=4, name="gmlp_in_dw", rider=_join_riders(
        [_chip_rider([sums["win0"][1]], ks=(3,)), chip_rider("wa_out")]))
    from_chips["win0"], from_chips["wa_out"] = [ce_win0_a, ce_win0_b], [ce]
    vec = jnp.concatenate([st_av[0:1], st_kv[0:1], st_b[0:1], st_f0[0:1], st_f1[0:1], st_final[0:3]], axis=0)
    parts = [four(g_wa_in)]
    (grad_x, st_a), got = _matmul_nt(dz, wa_in, flat=True, nbk=4, name="gmlp_in_dx", norm=(x2, g_a, dh1),
                                     rider=_join_riders([_sibling_rider(parts),
                                                         gather(vec, g_ws, g_bst, g_rel, st_conv0, st_conv1)]))
    sibling_sums(["wa_in"], parts, got[0:1])
    small = got[1:]

    def big_update(names, w, m, v, rider=None):
        shape = w.shape
        r = lambda t: t.reshape((len(names), -1, shape[-1]))
        as_list = lambda t: t if isinstance(t, list) else [t]
        outs = _adamw_big(r(w), r(m), r(v), [sums[nm][0] for nm in names],
                          [as_list(from_chips[nm]) for nm in names], "adamw_" + names[0], rider=rider)
        outs, got = (outs, None) if rider is None else outs
        return [t.reshape(shape) for t in outs], got

    u_f_w_in, (ce, st_a) = big_update(["win0", "win1"], f_w_in, m_f_w_in, v_f_w_in,
                                      rider=_join_riders([chip_rider("wa_in"), gather(st_a)]))
    from_chips["wa_in"] = [ce]
    u_f_w_down, _ = big_update(["wd0", "wd1"], f_w_down, m_f_w_down, v_f_w_down)
    u_a_w_in, _ = big_update(["wa_in"], a_w_in, m_a_w_in, v_a_w_in)
    u_w_kv, _ = big_update(["wkv"], w_kv, m_w_kv, v_w_kv)
    u_a_w_out, _ = big_update(["wa_out"], a_w_out, m_a_w_out, v_a_w_out)
    u_b_w_q, _ = big_update(["wq"], b_w_q, m_b_w_q, v_b_w_q)
    u_b_w_o, _ = big_update(["wo"], b_w_o, m_b_w_o, v_b_w_o)

    vec, g_ws, g_bst, g_rel, st_conv0, st_conv1, st_a = _sum_devices(list(small) + [st_a], "sum_small_grads")
    vec = jnp.concatenate([st_a[0:1], vec[0:7]], axis=0)
    loss = vec[7, 0]
    g_a_norm = lax.dynamic_slice_in_dim(vec[0:1], me * dn, dn, axis=1)
    g_av_norm = lax.dynamic_slice_in_dim(vec[1:2], me * dn, dn, axis=1)
    st_conv = jnp.stack([st_conv0, st_conv1])
    g_conv_w = lax.dynamic_index_in_dim(st_conv, me, axis=1, keepdims=False)[:, 0:3]
    g_conv_b = st_conv[:, :, 3, :].reshape(L, 8 * F)
    small_items = [
        (a_norm_g, m_a_norm_g, v_a_norm_g, g_a_norm),
        (a_v_norm_g, m_a_v_norm_g, v_a_v_norm_g, g_av_norm),
        (a_w_s, m_a_w_s, v_a_w_s, g_ws[None]),
        (a_b_s, m_a_b_s, v_a_b_s, jnp.transpose(g_bst)[None]),
        (kv_norm_g.reshape(1, D), m_kv_norm_g.reshape(1, D), v_kv_norm_g.reshape(1, D), vec[2:3]),
        (b_norm_g, m_b_norm_g, v_b_norm_g, vec[3:4]),
        (b_rel_bias, m_b_rel_bias, v_b_rel_bias, g_rel[None, :, :b_rel_bias.shape[2]]),
        (f_norm_g, m_f_norm_g, v_f_norm_g, vec[4:6]),
        (f_conv_w, m_f_conv_w, v_f_conv_w, g_conv_w),
        (f_conv_b, m_f_conv_b, v_f_conv_b, g_conv_b),
        (final_norm_g.reshape(1, D), m_final_norm_g.reshape(1, D), v_final_norm_g.reshape(1, D), vec[6:7]),
    ]
    small_out = _adamw_small(small_items, "adamw_small")
    (u_a_norm, u_av_norm, u_ws, u_bs, u_kvn, u_bn, u_rel, u_fn, u_cw, u_cb, u_fin) = [
        (it[3],) + so for it, so in zip(small_items, small_out)]
    vecD = lambda u: tuple(t.reshape(D) for t in u)
    u_kvn, u_fin = vecD(u_kvn), vecD(u_fin)

    order = [u_a_norm, u_a_w_in, u_av_norm, u_ws, u_bs, u_a_w_out, u_kvn, u_w_kv, u_bn, u_b_w_q, u_rel, u_b_w_o,
             u_fn, u_f_w_in, u_cw, u_cb, u_f_w_down, u_fin]
    outs = [loss, grad_x.reshape(B, S, D)]
    for k in range(4):
        outs += [u[k] for u in order]
    return tuple(outs)
```

```python
import functools

import jax
import jax.numpy as jnp
from jax import lax
from jax.experimental import pallas as pl
from jax.experimental.pallas import tpu as pltpu

F32 = jnp.float32
BF16 = jnp.bfloat16
MESH = pl.DeviceIdType.MESH

N_DEV = 8
EPS = 1e-6
NEG_INF = -1e30
CHUNK = 64
LEFT_CHUNKS = 8
REL_CLIP = 128
HEAD_DIM = 64
GMLP_BLOCK = 128
Q_TILE = 2 * CHUNK
PAD = LEFT_CHUNKS * CHUNK
WIN = PAD + Q_TILE
SKEW = WIN + Q_TILE
REL_PAD = 384
FWD_HEADS_PER_STEP = 8
BWD_HEADS_PER_STEP = 4
TILES_PER_STEP = 2
STRIP = 32
ROWS = 32
ADAM_LR, ADAM_B1, ADAM_B2, ADAM_EPS, ADAM_WD, ADAM_STEP = 0.001, 0.9, 0.999, 1e-08, 0.01, 10
VMEM_LIMIT = 56 * 1024 * 1024
TOKEN_TILE = 512
MATMUL_TILE = 1024


def _params(sem=None):
    return pltpu.CompilerParams(dimension_semantics=sem, vmem_limit_bytes=VMEM_LIMIT)


def _tile(n, pref):
    if n <= pref:
        return n
    for t in range(pref - pref % 8, 7, -8):
        if n % t == 0:
            return t
    return n


def _gelu(x):
    return 0.5 * x * (1.0 + jnp.tanh(0.7978845608028654 * (x + 0.044715 * x * x * x)))


def _gelu_grad(x):
    t = jnp.tanh(0.7978845608028654 * (x + 0.044715 * x * x * x))
    return 0.5 * (1.0 + t) + 0.5 * x * (1.0 - t * t) * 0.7978845608028654 * (1.0 + 3 * 0.044715 * x * x)


def _sigmoid(x):
    return 1.0 / (1.0 + jnp.exp(-x))


def _dot(a, b):
    return jnp.dot(a, b, preferred_element_type=F32)


def _dot_nt(a, b):
    return lax.dot_general(a, b, (((1,), (1,)), ((), ())), preferred_element_type=F32)


def _dot_tn(a, b):
    return lax.dot_general(a, b, (((0,), (0,)), ((), ())), preferred_element_type=F32)


def _split3(x):
    hi = x.astype(BF16)
    r1 = x - hi.astype(F32)
    mid = r1.astype(BF16)
    lo = (r1 - mid.astype(F32)).astype(BF16)
    return hi, mid, lo


def _mesh_pos():
    return lax.axis_index("x"), lax.axis_index("y"), lax.axis_index("c")


class _Rider:
    def __init__(self, arrs, out_shapes, sems, start, finish):
        self.arrs, self.out_shapes, self.sems, self.start, self.finish = arrs, out_shapes, sems, start, finish


def _gather_rider(arrs):
    n = len(arrs)

    def tools(ins, outs, sems):
        send_sems, recv_sems, local_sems = sems
        x, y, c = _mesh_pos()
        me, sibling = (x, y, c), (x, y, 1 - c)
        chips = [(1 - x, y), (x, 1 - y), (1 - x, 1 - y)]

        def slot(a, block):
            px, py, pc = block
            return outs[a].at[4 * px + 2 * py + pc]

        def copy(a, k, block, to, src=None):
            dst = slot(a, block)
            return pltpu.make_async_remote_copy(
                src_ref=dst if src is None else src, dst_ref=dst,
                send_sem=send_sems.at[a, k], recv_sem=recv_sems.at[a, k], device_id=to, device_id_type=MESH)

        def first(a):
            cps = [copy(a, 0, me, sibling, src=ins[a])]
            return cps + [copy(a, 1 + j, me, (*chip, c), src=ins[a]) for j, chip in enumerate(chips)]

        def mine(a):
            return pltpu.make_async_copy(ins[a], slot(a, me), local_sems.at[a])

        return me, sibling, chips, c, copy, first, mine

    def start(ins, outs, sems):
        _, _, _, _, _, first, mine = tools(ins, outs, sems)
        for a in range(n):
            mine(a).start()
            for cp in first(a):
                cp.start()

    def finish(ins, outs, sems):
        me, sibling, chips, c, copy, first, mine = tools(ins, outs, sems)
        passed = []
        for j, chip in enumerate(chips):
            for a in range(n):
                copy(a, 1 + j, (*chip, c), me).wait_recv()
                fwd = copy(a, 4 + j, (*chip, c), sibling)
                fwd.start()
                passed.append(fwd)
        for a in range(n):
            copy(a, 0, sibling, me).wait_recv()
            for j, chip in enumerate(chips):
                copy(a, 4 + j, (*chip, 1 - c), me).wait_recv()
        for a in range(n):
            for cp in first(a):
                cp.wait_send()
        for cp in passed:
            cp.wait_send()
        for a in range(n):
            mine(a).wait()

    return _Rider(list(arrs), [jax.ShapeDtypeStruct((N_DEV,) + a.shape, a.dtype) for a in arrs],
                  [pltpu.SemaphoreType.DMA((n, 7)), pltpu.SemaphoreType.DMA((n, 7)), pltpu.SemaphoreType.DMA((n,))],
                  start, finish)


def _sibling_rider(arrs):
    n = len(arrs)

    def copies(ins, outs, sems):
        x, y, c = _mesh_pos()
        return [pltpu.make_async_remote_copy(
            src_ref=ins[a].at[:, pl.ds(1 - c, 1)], dst_ref=outs[a],
            send_sem=sems[0].at[a], recv_sem=sems[1].at[a], device_id=(x, y, 1 - c), device_id_type=MESH)
            for a in range(n)]

    def start(ins, outs, sems):
        for cp in copies(ins, outs, sems):
            cp.start()

    def finish(ins, outs, sems):
        for cp in copies(ins, outs, sems):
            cp.wait()

    return _Rider(list(arrs), [jax.ShapeDtypeStruct((4, 1) + a.shape[2:], a.dtype) for a in arrs],
                  [pltpu.SemaphoreType.DMA((n,)), pltpu.SemaphoreType.DMA((n,))], start, finish)


def _chip_rider(arrs, ks=(1, 2, 3)):
    n = len(arrs)

    def copies(ins, outs, sems):
        x, y, c = _mesh_pos()
        cps = []
        for a in range(n):
            for s, k in enumerate(ks):
                px = x if k < 2 else 1 - x
                py = y if k == 2 else 1 - y
                cps.append(pltpu.make_async_remote_copy(
                    src_ref=ins[a].at[2 * px + py], dst_ref=outs[a].at[s],
                    send_sem=sems[0].at[a, s], recv_sem=sems[1].at[a, s],
                    device_id=(px, py, c), device_id_type=MESH))
        return cps

    def start(ins, outs, sems):
        for cp in copies(ins, outs, sems):
            cp.start()

    def finish(ins, outs, sems):
        for cp in copies(ins, outs, sems):
            cp.wait()

    return _Rider(list(arrs), [jax.ShapeDtypeStruct((len(ks),) + a.shape[1:], a.dtype) for a in arrs],
                  [pltpu.SemaphoreType.DMA((n, len(ks))), pltpu.SemaphoreType.DMA((n, len(ks)))], start, finish)


def _join_riders(riders):
    def split(seq, counts):
        out, at = [], 0
        for k in counts:
            out.append(seq[at:at + k])
            at += k
        return out

    n_in = [len(r.arrs) for r in riders]
    n_out = [len(r.out_shapes) for r in riders]
    n_sem = [len(r.sems) for r in riders]

    def run(which):
        def fn(ins, outs, sems):
            for r, i, o, s in zip(riders, split(ins, n_in), split(outs, n_out), split(sems, n_sem)):
                getattr(r, which)(i, o, s)
        return fn

    return _Rider([a for r in riders for a in r.arrs], [o for r in riders for o in r.out_shapes],
                  [s for r in riders for s in r.sems], run("start"), run("finish"))


def _run_rider(rider, name):
    n_in, n_out = len(rider.arrs), len(rider.out_shapes)

    def body(*refs):
        ins, outs, sems = refs[:n_in], refs[n_in:n_in + n_out], refs[n_in + n_out:]
        rider.start(ins, outs, sems)
        rider.finish(ins, outs, sems)

    any_spec = pl.BlockSpec(memory_space=pl.ANY)
    return pl.pallas_call(
        body, name=name, out_shape=list(rider.out_shapes), in_specs=[any_spec] * n_in, out_specs=[any_spec] * n_out,
        scratch_shapes=list(rider.sems),
    )(*rider.arrs)


def _call(body, *, name, grid, in_specs, out_specs, out_shape, args, scratch_shapes=(), rider=None):
    params = _params(("arbitrary",) * len(grid))
    if rider is None:
        return pl.pallas_call(body, name=name, grid=grid, in_specs=in_specs, out_specs=out_specs, out_shape=out_shape,
                              scratch_shapes=list(scratch_shapes), compiler_params=params)(*args)
    single = not isinstance(out_shape, (list, tuple))
    outs = [out_shape] if single else list(out_shape)
    ospecs = [out_specs] if single else list(out_specs)
    n_in, n_out, n_scr = len(in_specs), len(outs), len(scratch_shapes)
    r_in, r_out = len(rider.arrs), len(rider.out_shapes)

    def hosted(*refs):
        refs = list(refs)
        ins, rins = refs[:n_in], refs[n_in:n_in + r_in]
        refs = refs[n_in + r_in:]
        houts, routs = refs[:n_out], refs[n_out:n_out + r_out]
        refs = refs[n_out + r_out:]
        scr, rsems = refs[:n_scr], refs[n_scr:]
        ids = [pl.program_id(a) for a in range(len(grid))]
        first = functools.reduce(lambda p, q: p & q, [i == 0 for i in ids])
        last = functools.reduce(lambda p, q: p & q, [i == g - 1 for i, g in zip(ids, grid)])

        @pl.when(first)
        def _():
            rider.start(rins, routs, rsems)

        body(*ins, *houts, *scr)

        @pl.when(last)
        def _():
            rider.finish(rins, routs, rsems)

    any_spec = pl.BlockSpec(memory_space=pl.ANY)
    res = pl.pallas_call(
        hosted, name=name, grid=grid, in_specs=list(in_specs) + [any_spec] * r_in,
        out_specs=ospecs + [any_spec] * r_out, out_shape=outs + list(rider.out_shapes),
        scratch_shapes=list(scratch_shapes) + list(rider.sems), compiler_params=params,
    )(*args, *rider.arrs)
    return (res[0] if single else list(res[:n_out])), list(res[n_out:])


def _sibling_sum(part, landed, pos, name):
    _, _, rows, cols = part.shape
    tr = _tile(rows, 512)

    def body(pos_ref, p_ref, l_ref, own_ref, all_ref):
        s = p_ref[0, 0] + l_ref[0, 0]
        all_ref[0] = s.astype(BF16)

        @pl.when(pl.program_id(1) == pos_ref[1])
        def _():
            own_ref[...] = s

    return pl.pallas_call(
        body, name=name,
        grid_spec=pltpu.PrefetchScalarGridSpec(
            num_scalar_prefetch=1, grid=(rows // tr, 4),
            in_specs=[pl.BlockSpec((1, 1, tr, cols), lambda i, k, pos: (k, pos[0], i, 0)),
                      pl.BlockSpec((1, 1, tr, cols), lambda i, k, pos: (k, 0, i, 0))],
            out_specs=[pl.BlockSpec((tr, cols), lambda i, k, pos: (i, 0)),
                       pl.BlockSpec((1, tr, cols), lambda i, k, pos: (k, i, 0))]),
        out_shape=[jax.ShapeDtypeStruct((rows, cols), F32), jax.ShapeDtypeStruct((4, rows, cols), BF16)],
        compiler_params=_params(("arbitrary", "arbitrary")),
    )(pos, part, landed)


def _rms(x, g):
    r = lax.rsqrt(jnp.mean(x * x, axis=-1, keepdims=True) + EPS)
    return x * r, r


def _norm_matmul(h, g, w, *, flat, nbk, name, scale=1.0, rider=None):
    T, D = h.shape
    nb, _, bn = w.shape
    tm = _tile(T, MATMUL_TILE)

    def body(h_ref, g_ref, w_ref, o_ref, n_ref):
        @pl.when(pl.program_id(1) == 0)
        def _():
            xh, _ = _rms(h_ref[...], None)
            n_ref[...] = (xh * g_ref[...]).astype(BF16)

        n = n_ref[...]
        for k in range(nbk):
            r = _dot(n, w_ref[k])
            r = (r if scale == 1.0 else r * scale).astype(BF16)
            if flat:
                o_ref[:, k * bn:(k + 1) * bn] = r
            else:
                o_ref[k] = r

    if flat:
        out_shape = jax.ShapeDtypeStruct((T, nb * bn), BF16)
        out_spec = pl.BlockSpec((tm, nbk * bn), lambda i, j: (i, j))
    else:
        out_shape = jax.ShapeDtypeStruct((nb, T, bn), BF16)
        out_spec = pl.BlockSpec((nbk, tm, bn), lambda i, j: (j, i, 0))
    return _call(
        body, name=name, grid=(T // tm, nb // nbk),
        in_specs=[pl.BlockSpec((tm, D), lambda i, j: (i, 0)),
                  pl.BlockSpec((1, D), lambda i, j: (0, 0)),
                  pl.BlockSpec((nbk, D, bn), lambda i, j: (j, 0, 0))],
        out_specs=[out_spec, pl.BlockSpec((tm, D), lambda i, j: (i, 0))],
        out_shape=[out_shape, jax.ShapeDtypeStruct((T, D), BF16)],
        args=(h, g, w), rider=rider)


def _matmul_nt(dy, w, *, flat, nbk, name, norm=None, out_dtype=BF16, rider=None):
    nb, R, bn = w.shape
    T = dy.shape[0] if flat else dy.shape[1]
    tm = _tile(T, MATMUL_TILE)
    nj = nb // nbk

    def body(*refs):
        if norm is None:
            dy_ref, w_ref, o_ref, acc_ref = refs
        else:
            dy_ref, w_ref, h_ref, g_ref, dres_ref, o_ref, dg_ref, acc_ref = refs
        i, j = pl.program_id(0), pl.program_id(1)

        @pl.when(j == 0)
        def _():
            acc_ref[...] = jnp.zeros_like(acc_ref)

        acc = acc_ref[...]
        for k in range(nbk):
            d = dy_ref[:, k * bn:(k + 1) * bn] if flat else dy_ref[k]
            acc = acc + _dot_nt(d.astype(BF16), w_ref[k])
        acc_ref[...] = acc

        @pl.when(j == nj - 1)
        def _():
            if norm is None:
                o_ref[...] = acc.astype(out_dtype)
            else:
                xh, r = _rms(h_ref[...], None)

                @pl.when(i == 0)
                def _():
                    dg_ref[...] = jnp.zeros_like(dg_ref)

                dg_ref[0:1, :] += jnp.sum(acc * xh, axis=0, keepdims=True)
                dn = acc * g_ref[...]
                o_ref[...] = dres_ref[...] + r * (dn - xh * jnp.mean(dn * xh, axis=-1, keepdims=True))

    if flat:
        dy_spec = pl.BlockSpec((tm, nbk * bn), lambda i, j: (i, j))
    else:
        dy_spec = pl.BlockSpec((nbk, tm, bn), lambda i, j: (j, i, 0))
    w_spec = pl.BlockSpec((nbk, R, bn), lambda i, j: (j, 0, 0))
    row_spec = pl.BlockSpec((tm, R), lambda i, j: (i, 0))
    if norm is None:
        in_specs, args = [dy_spec, w_spec], (dy, w)
        out_specs = row_spec
        out_shape = jax.ShapeDtypeStruct((T, R), out_dtype)
    else:
        in_specs = [dy_spec, w_spec, row_spec, pl.BlockSpec((1, R), lambda i, j: (0, 0)), row_spec]
        args = (dy, w) + tuple(norm)
        out_specs = [row_spec, pl.BlockSpec((8, R), lambda i, j: (0, 0))]
        out_shape = [jax.ShapeDtypeStruct((T, R), F32), jax.ShapeDtypeStruct((8, R), F32)]
    return _call(
        body, name=name, grid=(T // tm, nj), in_specs=in_specs, out_specs=out_specs, out_shape=out_shape,
        scratch_shapes=[pltpu.VMEM((tm, R), F32)], args=args, rider=rider)


def _wgrad_cols(n, dy, *, flat, nb, nbk, name, rider=None):
    T, D = n.shape
    bn = dy.shape[1] // nb if flat else dy.shape[2]
    tt = _tile(T, MATMUL_TILE)
    nt = T // tt

    def body(n_ref, dy_ref, o_ref, acc_ref):
        t = pl.program_id(1)

        @pl.when(t == 0)
        def _():
            acc_ref[...] = jnp.zeros_like(acc_ref)

        nv = n_ref[...]
        for k in range(nbk):
            d = dy_ref[:, k * bn:(k + 1) * bn] if flat else dy_ref[k]
            acc_ref[k] += _dot_tn(nv, d)

        @pl.when(t == nt - 1)
        def _():
            o_ref[...] = acc_ref[...]

    if flat:
        dy_spec = pl.BlockSpec((tt, nbk * bn), lambda j, t: (t, j))
    else:
        dy_spec = pl.BlockSpec((nbk, tt, bn), lambda j, t: (j, t, 0))
    return _call(
        body, name=name, grid=(nb // nbk, nt),
        in_specs=[pl.BlockSpec((tt, D), lambda j, t: (t, 0)), dy_spec],
        out_specs=pl.BlockSpec((nbk, D, bn), lambda j, t: (j, 0, 0)),
        out_shape=jax.ShapeDtypeStruct((nb, D, bn), F32),
        scratch_shapes=[pltpu.VMEM((nbk, D, bn), F32)], args=(n, dy), rider=rider)


def _wgrad_rows(xa, dh, *, flat, tk, name, rider=None):
    T, D = dh.shape
    nk = xa.shape[1] // tk if flat else xa.shape[0]
    tt = _tile(T, MATMUL_TILE)
    nt = T // tt

    def body(x_ref, dh_ref, o_ref, acc_ref):
        t = pl.program_id(1)

        @pl.when(t == 0)
        def _():
            acc_ref[...] = jnp.zeros_like(acc_ref)

        xv = x_ref[...] if flat else x_ref[0]
        acc_ref[...] += _dot_tn(xv, dh_ref[...].astype(BF16))

        @pl.when(t == nt - 1)
        def _():
            o_ref[...] = acc_ref[...]

    x_spec = pl.BlockSpec((tt, tk), lambda j, t: (t, j)) if flat else pl.BlockSpec((1, tt, tk), lambda j, t: (j, t, 0))
    return _call(
        body, name=name, grid=(nk, nt),
        in_specs=[x_spec, pl.BlockSpec((tt, D), lambda j, t: (t, 0))],
        out_specs=pl.BlockSpec((tk, D), lambda j, t: (j, 0)),
        out_shape=jax.ShapeDtypeStruct((nk * tk, D), F32),
        scratch_shapes=[pltpu.VMEM((tk, D), F32)], args=(xa, dh), rider=rider)


def _matmul_residual(xa, w, res, name):
    T, K = xa.shape
    D = w.shape[1]
    tm = _tile(T, MATMUL_TILE)

    def body(x_ref, w_ref, r_ref, o_ref):
        o_ref[...] = r_ref[...] + _dot(x_ref[...], w_ref[...])

    return pl.pallas_call(
        body, name=name, grid=(T // tm,),
        in_specs=[pl.BlockSpec((tm, K), lambda i: (i, 0)), pl.BlockSpec((K, D), lambda i: (0, 0)),
                  pl.BlockSpec((tm, D), lambda i: (i, 0))],
        out_specs=pl.BlockSpec((tm, D), lambda i: (i, 0)),
        out_shape=jax.ShapeDtypeStruct((T, D), F32),
        compiler_params=_params(("arbitrary",)),
    )(xa, w, res)


def _gmlp_gate(z, ws, bst, gv, G, gd):
    D = G * gd
    u = _gelu(z[:, :D].astype(F32))
    v = _gelu(z[:, D:].astype(F32))
    vh, r = _rms(v, None)
    vn = (vh * gv).astype(BF16)
    return u, v, vh, r, vn


def _gmlp_forward(z, ws, bst, gv, w_out, x, *, name, rider=None):
    T, D2 = z.shape
    D = D2 // 2
    G = ws.shape[0]
    gd = D // G
    tb = _tile(T, 256)
    nblk = tb // GMLP_BLOCK

    def body(z_ref, ws_ref, b_ref, gv_ref, wo_ref, x_ref, gated_ref, h_ref):
        u, _, _, _, vn = _gmlp_gate(z_ref[...], None, None, gv_ref[...], G, gd)
        for n in range(nblk):
            rows = slice(n * GMLP_BLOCK, (n + 1) * GMLP_BLOCK)
            for gi in range(G):
                cols = slice(gi * gd, (gi + 1) * gd)
                s = _dot(ws_ref[gi], vn[rows, cols]) + b_ref[:, gi:gi + 1]
                gated_ref[rows, cols] = (u[rows, cols] * s).astype(BF16)
        h_ref[...] = x_ref[...] + _dot(gated_ref[...], wo_ref[...])

    return _call(
        body, name=name, grid=(T // tb,),
        in_specs=[pl.BlockSpec((tb, D2), lambda i: (i, 0)), pl.BlockSpec(ws.shape, lambda i: (0, 0, 0)),
                  pl.BlockSpec(bst.shape, lambda i: (0, 0)), pl.BlockSpec((1, D), lambda i: (0, 0)),
                  pl.BlockSpec((D, D), lambda i: (0, 0)), pl.BlockSpec((tb, D), lambda i: (i, 0))],
        out_specs=[pl.BlockSpec((tb, D), lambda i: (i, 0)), pl.BlockSpec((tb, D), lambda i: (i, 0))],
        out_shape=[jax.ShapeDtypeStruct((T, D), BF16), jax.ShapeDtypeStruct((T, D), F32)],
        args=(z, ws, bst, gv, w_out, x), rider=rider)


def _gmlp_backward(z, dgated, ws, bst, gv, mask, *, name, rider=None):
    T, D2 = z.shape
    D = D2 // 2
    G = ws.shape[0]
    gd = D // G
    tb = _tile(T, 256)
    nblk = tb // GMLP_BLOCK

    def body(z_ref, dg_ref, ws_ref, b_ref, gv_ref, mask_ref, dz_ref, dws_ref, db_ref, dgv_ref, dvn_ref):
        @pl.when(pl.program_id(0) == 0)
        def _():
            dws_ref[...] = jnp.zeros_like(dws_ref)
            db_ref[...] = jnp.zeros_like(db_ref)
            dgv_ref[...] = jnp.zeros_like(dgv_ref)

        zf = z_ref[...]
        u, v, vh, r, vn = _gmlp_gate(zf, None, None, gv_ref[...], G, gd)
        dg = dg_ref[...].astype(F32)
        for n in range(nblk):
            rows = slice(n * GMLP_BLOCK, (n + 1) * GMLP_BLOCK)
            for gi in range(G):
                cols = slice(gi * gd, (gi + 1) * gd)
                vblk = vn[rows, cols]
                s = _dot(ws_ref[gi], vblk) + b_ref[:, gi:gi + 1]
                dgb = dg[rows, cols]
                ds = dgb * u[rows, cols]
                dsb = ds.astype(BF16)
                dz_ref[rows, cols] = (dgb * s * _gelu_grad(zf[rows, cols].astype(F32))).astype(BF16)
                dvn_ref[rows, cols] = _dot_tn(ws_ref[gi], dsb)
                dws_ref[gi] += _dot_nt(dsb, vblk) * mask_ref[...]
                db_ref[:, gi:gi + 1] += jnp.sum(ds, axis=1, keepdims=True)
        dvn = dvn_ref[...]
        dgv_ref[0:1, :] += jnp.sum(dvn * vh, axis=0, keepdims=True)
        dn = dvn * gv_ref[...]
        dv = r * (dn - vh * jnp.mean(dn * vh, axis=-1, keepdims=True))
        dz_ref[:, D:] = (dv * _gelu_grad(zf[:, D:].astype(F32))).astype(BF16)

    return _call(
        body, name=name, grid=(T // tb,),
        in_specs=[pl.BlockSpec((tb, D2), lambda i: (i, 0)), pl.BlockSpec((tb, D), lambda i: (i, 0)),
                  pl.BlockSpec(ws.shape, lambda i: (0, 0, 0)), pl.BlockSpec(bst.shape, lambda i: (0, 0)),
                  pl.BlockSpec((1, D), lambda i: (0, 0)), pl.BlockSpec(mask.shape, lambda i: (0, 0))],
        out_specs=[pl.BlockSpec((tb, D2), lambda i: (i, 0)), pl.BlockSpec(ws.shape, lambda i: (0, 0, 0)),
                   pl.BlockSpec(bst.shape, lambda i: (0, 0)), pl.BlockSpec((8, D), lambda i: (0, 0))],
        out_shape=[jax.ShapeDtypeStruct((T, D2), BF16), jax.ShapeDtypeStruct(ws.shape, F32),
                   jax.ShapeDtypeStruct(bst.shape, F32), jax.ShapeDtypeStruct((8, D), F32)],
        scratch_shapes=[pltpu.VMEM((tb, D), F32)], args=(z, dgated, ws, bst, gv, mask), rider=rider)


def _shift_rows(x, k):
    return pltpu.roll(x, k % x.shape[0], axis=0)


def _conv3(ext, cw):
    return (cw[0:1] * _shift_rows(ext, 2)[8:] + cw[1:2] * _shift_rows(ext, 1)[8:] + cw[2:3] * ext[8:])


def _ffn_forward(a, cw, cb, wd, h, seq, *, name, rider=None):
    _, T, F = a.shape
    D = h.shape[1]
    tm = _tile(seq, TOKEN_TILE)
    hb = tm // 16

    def body(a_ref, ap_ref, cw_ref, cb_ref, wd_ref, h_ref, act_ref, c_ref, o_ref, acc_ref):
        i, j = pl.program_id(0), pl.program_id(1)
        keep = ((i * tm) % seq != 0).astype(F32)

        def conv(b):
            ext = jnp.concatenate([ap_ref[b, 8:16].astype(F32) * keep, a_ref[b].astype(F32)], axis=0)
            return _conv3(ext, cw_ref[b]) + cb_ref[b]

        up, gate = conv(j), conv(j + 4)
        c_ref[j] = up.astype(BF16)
        c_ref[j + 4] = gate.astype(BF16)
        act = (gate * _sigmoid(gate) * up).astype(BF16)
        act_ref[0] = act

        @pl.when(j == 0)
        def _():
            acc_ref[...] = h_ref[...]

        acc_ref[...] += _dot(act, wd_ref[0])

        @pl.when(j == 3)
        def _():
            o_ref[...] = acc_ref[...]

    return _call(
        body, name=name, grid=(T // tm, 4),
        in_specs=[pl.BlockSpec((8, tm, F), lambda i, j: (0, i, 0)),
                  pl.BlockSpec((8, 16, F), lambda i, j: (0, jnp.maximum(i * hb - 1, 0), 0)),
                  pl.BlockSpec((8, 3, F), lambda i, j: (0, 0, 0)), pl.BlockSpec((8, 1, F), lambda i, j: (0, 0, 0)),
                  pl.BlockSpec((1, F, D), lambda i, j: (j, 0, 0)), pl.BlockSpec((tm, D), lambda i, j: (i, 0))],
        out_specs=[pl.BlockSpec((1, tm, F), lambda i, j: (j, i, 0)), pl.BlockSpec((8, tm, F), lambda i, j: (0, i, 0)),
                   pl.BlockSpec((tm, D), lambda i, j: (i, 0))],
        out_shape=[jax.ShapeDtypeStruct((4, T, F), BF16), jax.ShapeDtypeStruct((8, T, F), BF16),
                   jax.ShapeDtypeStruct((T, D), F32)],
        scratch_shapes=[pltpu.VMEM((tm, D), F32)], args=(a, a, cw, cb, wd, h), rider=rider)


def _ffn_backward(dh, c, a, cw, wd, seq, *, name, rider=None):
    _, T, F = a.shape
    D = dh.shape[1]
    tm = _tile(seq, TOKEN_TILE)
    hb = tm // 16
    nt = T // tm

    def body(dh_ref, dhn_ref, cu_ref, cg_ref, cun_ref, cgn_ref, au_ref, ag_ref, cw_ref, wd_ref, da_ref, st_ref):
        i, j = pl.program_id(0), pl.program_id(1)
        keep_next = (((i + 1) * tm) % seq != 0).astype(F32)

        @pl.when((i == 0) & (j == 0))
        def _():
            st_ref[...] = jnp.zeros_like(st_ref)

        dhe = jnp.concatenate([dh_ref[...], dhn_ref[...] * keep_next], axis=0).astype(BF16)
        dact = _dot_nt(dhe, wd_ref[0])
        up = jnp.concatenate([cu_ref[0].astype(F32), cun_ref[0, 0:8].astype(F32)], axis=0)
        gate = jnp.concatenate([cg_ref[0].astype(F32), cgn_ref[0, 0:8].astype(F32)], axis=0)
        sg = _sigmoid(gate)
        gs = gate * sg
        d_up = dact * gs
        d_gate = dact * up * (sg + gs * (1.0 - sg))

        def finish(b, a_ref, dc):
            w = cw_ref[b]
            dm, u1, u2 = dc[:tm], _shift_rows(dc, -1)[:tm], _shift_rows(dc, -2)[:tm]
            da_ref[b] = (w[2:3] * dm + w[1:2] * u1 + w[0:1] * u2).astype(BF16)
            av = a_ref[0].astype(F32)
            st_ref[b, 0:1, :] += jnp.sum(u2 * av, axis=0, keepdims=True)
            st_ref[b, 1:2, :] += jnp.sum(u1 * av, axis=0, keepdims=True)
            st_ref[b, 2:3, :] += jnp.sum(dm * av, axis=0, keepdims=True)
            st_ref[b, 3:4, :] += jnp.sum(dm, axis=0, keepdims=True)

        finish(j, au_ref, d_up)
        finish(j + 4, ag_ref, d_gate)

    nxt = lambda i: jnp.minimum((i + 1) * hb, T // 16 - 1)
    return _call(
        body, name=name, grid=(nt, 4),
        in_specs=[pl.BlockSpec((tm, D), lambda i, j: (i, 0)),
                  pl.BlockSpec((8, D), lambda i, j: (jnp.minimum((i + 1) * (tm // 8), T // 8 - 1), 0)),
                  pl.BlockSpec((1, tm, F), lambda i, j: (j, i, 0)), pl.BlockSpec((1, tm, F), lambda i, j: (j + 4, i, 0)),
                  pl.BlockSpec((1, 16, F), lambda i, j: (j, nxt(i), 0)),
                  pl.BlockSpec((1, 16, F), lambda i, j: (j + 4, nxt(i), 0)),
                  pl.BlockSpec((1, tm, F), lambda i, j: (j, i, 0)), pl.BlockSpec((1, tm, F), lambda i, j: (j + 4, i, 0)),
                  pl.BlockSpec((8, 3, F), lambda i, j: (0, 0, 0)),
                  pl.BlockSpec((1, F, D), lambda i, j: (j, 0, 0))],
        out_specs=[pl.BlockSpec((8, tm, F), lambda i, j: (0, i, 0)), pl.BlockSpec((8, 8, F), lambda i, j: (0, 0, 0))],
        out_shape=[jax.ShapeDtypeStruct((8, T, F), BF16), jax.ShapeDtypeStruct((8, 8, F), F32)],
        args=(dh, dh, c, c, c, c, a, a, cw, wd), rider=rider)


def _ffn_fused_forward(h, g, w_in, cw, cb, wd, seq, *, name, rider=None):
    T, D = h.shape
    F = w_in.shape[2]
    tm = _tile(seq, TOKEN_TILE)

    def body(h_ref, g_ref, wu_ref, wg_ref, cw_ref, cb_ref, wd_ref,
             au_ref, ag_ref, cu_ref, cg_ref, act_ref, n_ref, o_ref, acc_ref, carry_ref, eu_ref, eg_ref):
        i, j = pl.program_id(0), pl.program_id(1)
        keep = ((i * tm) % seq != 0).astype(F32)

        @pl.when((i == 0) & (j == 0))
        def _():
            carry_ref[...] = jnp.zeros_like(carry_ref)

        @pl.when(j == 0)
        def _():
            xh, _ = _rms(h_ref[...], None)
            n_ref[...] = (xh * g_ref[...]).astype(BF16)
            acc_ref[...] = h_ref[...]

        n = n_ref[...]

        def project(b, w_ref, a_ref, ext_ref):
            a = _dot(n, w_ref[0]).astype(BF16)
            a_ref[0] = a
            ext_ref[0:8, :] = carry_ref[b] * keep
            ext_ref[8:, :] = a.astype(F32)
            carry_ref[b] = ext_ref[tm:tm + 8, :]

        project(j, wu_ref, au_ref, eu_ref)
        project(j + 4, wg_ref, ag_ref, eg_ref)

        def conv(b, ext_ref, r):
            x, w = ext_ref[r:r + ROWS + 8, :], cw_ref[b]
            return (w[0:1] * _shift_rows(x, 2) + w[1:2] * _shift_rows(x, 1) + w[2:3] * x)[8:] + cb_ref[b]

        for r in range(0, tm, ROWS):
            up, gate = conv(j, eu_ref, r), conv(j + 4, eg_ref, r)
            cu_ref[0, r:r + ROWS, :] = up.astype(BF16)
            cg_ref[0, r:r + ROWS, :] = gate.astype(BF16)
            act_ref[0, r:r + ROWS, :] = (gate * _sigmoid(gate) * up).astype(BF16)
        acc_ref[...] += _dot(act_ref[0], wd_ref[0])

        @pl.when(j == 3)
        def _():
            o_ref[...] = acc_ref[...]

    blk = pl.BlockSpec((1, tm, F), lambda i, j: (j, i, 0))
    row = pl.BlockSpec((tm, D), lambda i, j: (i, 0))
    half = jax.ShapeDtypeStruct((4, T, F), BF16)
    return _call(
        body, name=name, grid=(T // tm, 4),
        in_specs=[row, pl.BlockSpec((1, D), lambda i, j: (0, 0)),
                  pl.BlockSpec((1, D, F), lambda i, j: (j, 0, 0)), pl.BlockSpec((1, D, F), lambda i, j: (j + 4, 0, 0)),
                  pl.BlockSpec((8, 3, F), lambda i, j: (0, 0, 0)), pl.BlockSpec((8, 1, F), lambda i, j: (0, 0, 0)),
                  pl.BlockSpec((1, F, D), lambda i, j: (j, 0, 0))],
        out_specs=[blk, blk, blk, blk, blk, row, row],
        out_shape=[half, half, half, half, half, jax.ShapeDtypeStruct((T, D), BF16), jax.ShapeDtypeStruct((T, D), F32)],
        scratch_shapes=[pltpu.VMEM((tm, D), F32), pltpu.VMEM((8, 8, F), F32),
                        pltpu.VMEM((tm + 8, F), F32), pltpu.VMEM((tm + 8, F), F32)],
        args=(h, g, w_in, w_in, cw, cb, wd), rider=rider)


def _ffn_fused_backward(dh, cu, cg, au, ag, cw, wd, w_in, h, g, seq, *, name, rider=None):
    T, D = dh.shape
    F = wd.shape[1]
    tm = _tile(seq, TOKEN_TILE)
    hb = tm // 16
    nt = T // tm

    def body(dh_ref, dhn_ref, cu_ref, cg_ref, cun_ref, cgn_ref, au_ref, ag_ref, cw_ref, wd_ref, wu_ref, wg_ref,
             h_ref, g_ref, da_ref, st_ref, o_ref, dg_ref, acc_ref, dact_ref, du_ref, dgt_ref):
        i, j = pl.program_id(0), pl.program_id(1)
        keep_next = (((i + 1) * tm) % seq != 0).astype(F32)

        @pl.when((i == 0) & (j == 0))
        def _():
            st_ref[...] = jnp.zeros_like(st_ref)
            dg_ref[...] = jnp.zeros_like(dg_ref)

        @pl.when(j == 0)
        def _():
            acc_ref[...] = jnp.zeros_like(acc_ref)

        dhe = jnp.concatenate([dh_ref[...], dhn_ref[...] * keep_next], axis=0).astype(BF16)
        dact_ref[...] = _dot_nt(dhe, wd_ref[0])

        for r in range(0, tm + 8, ROWS):
            if r < tm:
                rows = slice(r, r + ROWS)
                up, gate = cu_ref[0, rows, :].astype(F32), cg_ref[0, rows, :].astype(F32)
            else:
                rows = slice(tm, tm + 8)
                up, gate = cun_ref[0, 0:8, :].astype(F32), cgn_ref[0, 0:8, :].astype(F32)
            dact = dact_ref[rows, :]
            sg = _sigmoid(gate)
            gs = gate * sg
            du_ref[rows, :] = dact * gs
            dgt_ref[rows, :] = dact * up * (sg + gs * (1.0 - sg))

        def finish(b, a_ref, w_ref, dc_ref):
            w = cw_ref[b]
            sums = [jnp.zeros((8, F), F32) for _ in range(4)]
            fold = lambda t: jnp.sum(t.reshape(ROWS // 8, 8, F), axis=0)
            for r in range(0, tm, ROWS):
                dc = dc_ref[r:r + ROWS + 8, :]
                dm, u1, u2 = dc[:ROWS], _shift_rows(dc, -1)[:ROWS], _shift_rows(dc, -2)[:ROWS]
                da_ref[b, r:r + ROWS, :] = (w[2:3] * dm + w[1:2] * u1 + w[0:1] * u2).astype(BF16)
                av = a_ref[0, r:r + ROWS, :].astype(F32)
                for k, t in enumerate((u2 * av, u1 * av, dm * av, dm)):
                    sums[k] = sums[k] + fold(t)
            for k in range(4):
                st_ref[b, k:k + 1, :] += jnp.sum(sums[k], axis=0, keepdims=True)
            acc_ref[...] += _dot_nt(da_ref[b], w_ref[0])

        finish(j, au_ref, wu_ref, du_ref)
        finish(j + 4, ag_ref, wg_ref, dgt_ref)

        @pl.when(j == 3)
        def _():
            acc = acc_ref[...]
            xh, r = _rms(h_ref[...], None)
            dg_ref[0:1, :] += jnp.sum(acc * xh, axis=0, keepdims=True)
            dn = acc * g_ref[...]
            o_ref[...] = dh_ref[...] + r * (dn - xh * jnp.mean(dn * xh, axis=-1, keepdims=True))

    nxt = lambda i: jnp.minimum((i + 1) * hb, T // 16 - 1)
    blk = pl.BlockSpec((1, tm, F), lambda i, j: (j, i, 0))
    halo = pl.BlockSpec((1, 16, F), lambda i, j: (j, nxt(i), 0))
    row = pl.BlockSpec((tm, D), lambda i, j: (i, 0))
    return _call(
        body, name=name, grid=(nt, 4),
        in_specs=[row, pl.BlockSpec((8, D), lambda i, j: (jnp.minimum((i + 1) * (tm // 8), T // 8 - 1), 0)),
                  blk, blk, halo, halo, blk, blk,
                  pl.BlockSpec((8, 3, F), lambda i, j: (0, 0, 0)), pl.BlockSpec((1, F, D), lambda i, j: (j, 0, 0)),
                  pl.BlockSpec((1, D, F), lambda i, j: (j, 0, 0)), pl.BlockSpec((1, D, F), lambda i, j: (j + 4, 0, 0)),
                  row, pl.BlockSpec((1, D), lambda i, j: (0, 0))],
        out_specs=[pl.BlockSpec((8, tm, F), lambda i, j: (0, i, 0)), pl.BlockSpec((8, 8, F), lambda i, j: (0, 0, 0)),
                   row, pl.BlockSpec((8, D), lambda i, j: (0, 0))],
        out_shape=[jax.ShapeDtypeStruct((8, T, F), BF16), jax.ShapeDtypeStruct((8, 8, F), F32),
                   jax.ShapeDtypeStruct((T, D), F32), jax.ShapeDtypeStruct((8, D), F32)],
        scratch_shapes=[pltpu.VMEM((tm, D), F32), pltpu.VMEM((tm + 8, F), F32), pltpu.VMEM((tm + 8, F), F32),
                        pltpu.VMEM((tm + 8, F), F32)],
        args=(dh, dh, cu, cg, cu, cg, au, ag, cw, wd, w_in, w_in, h, g), rider=rider)


def _rel_onehot():
    r = lax.broadcasted_iota(jnp.int32, (REL_PAD, SKEW), 0)
    n = lax.broadcasted_iota(jnp.int32, (REL_PAD, SKEW), 1)
    off = jnp.where(n >= WIN, n - SKEW, n)
    idx = jnp.minimum(PAD - off, REL_CLIP) + REL_CLIP
    return (r == idx).astype(BF16)


def _skew(x, sign):
    row = lax.broadcasted_iota(jnp.int32, x.shape, 0)
    for b in range(7):
        x = jnp.where((row >> b) & 1 == 1, pltpu.roll(x, (sign * (1 << b)) % SKEW, axis=1), x)
    return x


def _bias_build(rel, name, rider=None):
    H = rel.shape[0]

    def body(rel_ref, o_ref):
        oh = _rel_onehot()
        hi, mid, lo = _split3(rel_ref[...])
        base = _dot(hi, oh) + _dot(mid, oh) + _dot(lo, oh)
        mine = lax.broadcasted_iota(jnp.int32, (H, 1), 0) == pl.program_id(0)
        row = jnp.sum(jnp.where(mine, base, 0.0), axis=0, keepdims=True)
        q = lax.broadcasted_iota(jnp.int32, (Q_TILE, WIN), 0)
        k = lax.broadcasted_iota(jnp.int32, (Q_TILE, WIN), 1)
        ok = ((q < CHUNK) & (k < WIN - CHUNK)) | ((q >= CHUNK) & (k >= CHUNK))
        t = _skew(jnp.broadcast_to(row, (Q_TILE, SKEW)), 1)
        o_ref[0] = jnp.where(ok, t[:, :WIN], NEG_INF)

    return _call(
        body, name=name, grid=(H,), in_specs=[pl.BlockSpec((H, REL_PAD), lambda h: (0, 0))],
        out_specs=pl.BlockSpec((1, Q_TILE, WIN), lambda h: (h, 0, 0)),
        out_shape=jax.ShapeDtypeStruct((H, Q_TILE, WIN), F32), args=(rel,), rider=rider)


def _bias_reduce(dbias, name):
    H = dbias.shape[0]

    def body(d_ref, o_ref, e_ref):
        oh = _rel_onehot()
        for hd in range(H):
            x = jnp.concatenate([d_ref[hd], jnp.zeros((Q_TILE, SKEW - WIN), F32)], axis=1)
            e_ref[hd:hd + 1, :] = jnp.sum(_skew(x, -1), axis=0, keepdims=True)
        hi, mid, lo = _split3(e_ref[...])
        o_ref[...] = _dot_nt(hi, oh) + _dot_nt(mid, oh) + _dot_nt(lo, oh)

    return pl.pallas_call(
        body, name=name, out_shape=jax.ShapeDtypeStruct((H, REL_PAD), F32),
        in_specs=[pl.BlockSpec(memory_space=pltpu.VMEM)], out_specs=pl.BlockSpec(memory_space=pltpu.VMEM),
        scratch_shapes=[pltpu.VMEM((H, SKEW), F32)],
        compiler_params=_params(),
    )(dbias)


def _pair_stack(xp, even):
    z = jnp.zeros_like(xp)
    return jnp.concatenate([jnp.where(even, xp, z), jnp.where(even, z, xp)], axis=0)


def _pair_merge(y, even):
    return jnp.where(even, y[:Q_TILE], y[Q_TILE:])


def _strip_probs(s_ref, b_ref, pp, r, valid, base=0):
    hb, hr = divmod(r, Q_TILE)
    s = s_ref[base + pp, r:r + STRIP, :] + b_ref[2 * pp + hb, hr:hr + STRIP, :]
    s = jnp.where(valid, s, NEG_INF)
    e = jnp.exp(s - jnp.max(s, axis=-1, keepdims=True))
    return e * (1.0 / jnp.sum(e, axis=-1, keepdims=True))


def _fill_padded(dst_ref, src_ref):
    dst_ref[0:PAD, :] = jnp.zeros((PAD, dst_ref.shape[1]), dst_ref.dtype)
    dst_ref[PAD:, :] = src_ref[...]


def _attn_specs(B, S, D, lanes):
    nt = S // (TILES_PER_STEP * Q_TILE)
    q_spec = pl.BlockSpec((TILES_PER_STEP * Q_TILE, lanes), lambda g, b, i: (b * nt + i, g))
    k_spec = pl.BlockSpec((S, lanes), lambda g, b, i: (b, g))
    v_spec = pl.BlockSpec((S, lanes), lambda g, b, i: (b, D // lanes + g))
    bias_spec = pl.BlockSpec((lanes // HEAD_DIM, Q_TILE, WIN), lambda g, b, i: (g, 0, 0))
    return nt, q_spec, k_spec, v_spec, bias_spec


def _attn_forward(q, kv, bias, S, *, name, rider=None):
    T, D = q.shape
    B = T // S
    lanes = min(FWD_HEADS_PER_STEP * HEAD_DIM, D)
    nt, q_spec, k_spec, v_spec, bias_spec = _attn_specs(B, S, D, lanes)

    npairs = lanes // (2 * HEAD_DIM)

    def body(q_ref, k_ref, v_ref, b_ref, o_ref, kp_ref, vp_ref, s_ref, p_ref):
        i = pl.program_id(2)

        @pl.when(i == 0)
        def _():
            _fill_padded(kp_ref, k_ref)
            _fill_padded(vp_ref, v_ref)

        even = lax.broadcasted_iota(jnp.int32, (1, 2 * HEAD_DIM), 1) < HEAD_DIM
        pair_cols = [slice(pp * 2 * HEAD_DIM, (pp + 1) * 2 * HEAD_DIM) for pp in range(npairs)]
        for t in range(TILES_PER_STEP):
            tile = i * TILES_PER_STEP + t
            start = pl.multiple_of(tile * Q_TILE, Q_TILE)
            rows = slice(t * Q_TILE, (t + 1) * Q_TILE)
            valid = lax.broadcasted_iota(jnp.int32, (STRIP, WIN), 1) >= PAD - tile * Q_TILE
            for pp, cols in enumerate(pair_cols):
                s_ref[t * npairs + pp] = _dot_nt(_pair_stack(q_ref[rows, cols], even), kp_ref[pl.ds(start, WIN), cols])
            for pp in range(npairs):
                for r in range(0, 2 * Q_TILE, STRIP):
                    p = _strip_probs(s_ref, b_ref, pp, r, valid, base=t * npairs)
                    p_ref[t * npairs + pp, r:r + STRIP, :] = p.astype(BF16)
            for pp, cols in enumerate(pair_cols):
                o = _dot(p_ref[t * npairs + pp], vp_ref[pl.ds(start, WIN), cols])
                o_ref[rows, cols] = _pair_merge(o, even).astype(BF16)

    nbuf = TILES_PER_STEP * npairs
    return _call(
        body, name=name, grid=(D // lanes, B, nt),
        in_specs=[q_spec, k_spec, v_spec, bias_spec], out_specs=q_spec,
        out_shape=jax.ShapeDtypeStruct((T, D), BF16),
        scratch_shapes=[pltpu.VMEM((S + PAD, lanes), BF16), pltpu.VMEM((S + PAD, lanes), BF16),
                        pltpu.VMEM((nbuf, 2 * Q_TILE, WIN), F32), pltpu.VMEM((nbuf, 2 * Q_TILE, WIN), BF16)],
        args=(q, kv, kv, bias), rider=rider)


def _attn_backward(q, kv, bias, do, S, *, name, rider=None):
    T, D = q.shape
    B = T // S
    H = D // HEAD_DIM
    lanes = min(BWD_HEADS_PER_STEP * HEAD_DIM, D)
    nt, q_spec, k_spec, v_spec, bias_spec = _attn_specs(B, S, D, lanes)
    scale = HEAD_DIM ** -0.5

    npairs = lanes // (2 * HEAD_DIM)

    def body(q_ref, k_ref, v_ref, b_ref, do_ref, dq_ref, dk_ref, dv_ref, db_ref, kp_ref, vp_ref, dka_ref, dva_ref,
             s_ref, dp_ref, p_ref, ds_ref):
        b, i = pl.program_id(1), pl.program_id(2)

        @pl.when((b == 0) & (i == 0))
        def _():
            db_ref[...] = jnp.zeros_like(db_ref)

        @pl.when(i == 0)
        def _():
            _fill_padded(kp_ref, k_ref)
            _fill_padded(vp_ref, v_ref)
            dka_ref[...] = jnp.zeros_like(dka_ref)
            dva_ref[...] = jnp.zeros_like(dva_ref)

        even = lax.broadcasted_iota(jnp.int32, (1, 2 * HEAD_DIM), 1) < HEAD_DIM
        pair_cols = [slice(pp * 2 * HEAD_DIM, (pp + 1) * 2 * HEAD_DIM) for pp in range(npairs)]
        for t in range(TILES_PER_STEP):
            tile = i * TILES_PER_STEP + t
            start = pl.multiple_of(tile * Q_TILE, Q_TILE)
            rows = slice(t * Q_TILE, (t + 1) * Q_TILE)
            valid = lax.broadcasted_iota(jnp.int32, (STRIP, WIN), 1) >= PAD - tile * Q_TILE
            base = t * npairs
            for pp, cols in enumerate(pair_cols):
                s_ref[base + pp] = _dot_nt(_pair_stack(q_ref[rows, cols], even), kp_ref[pl.ds(start, WIN), cols])
                dp_ref[base + pp] = _dot_nt(_pair_stack(do_ref[rows, cols], even), vp_ref[pl.ds(start, WIN), cols])
            for pp in range(npairs):
                for r in range(0, 2 * Q_TILE, STRIP):
                    hb, hr = divmod(r, Q_TILE)
                    p = _strip_probs(s_ref, b_ref, pp, r, valid, base=base)
                    dp = dp_ref[base + pp, r:r + STRIP, :]
                    ds = p * (dp - jnp.sum(p * dp, axis=-1, keepdims=True))
                    db_ref[2 * pp + hb, hr:hr + STRIP, :] += ds
                    p_ref[base + pp, r:r + STRIP, :] = p.astype(BF16)
                    ds_ref[base + pp, r:r + STRIP, :] = ds.astype(BF16)
            for pp, cols in enumerate(pair_cols):
                dsb = ds_ref[base + pp]
                dq = _pair_merge(_dot(dsb, kp_ref[pl.ds(start, WIN), cols]), even) * scale
                dq_ref[rows, cols] = dq.astype(BF16)
                dka_ref[pl.ds(start, WIN), cols] += _dot_tn(dsb, _pair_stack(q_ref[rows, cols], even))
                dva_ref[pl.ds(start, WIN), cols] += _dot_tn(p_ref[base + pp], _pair_stack(do_ref[rows, cols], even))

        @pl.when(i == nt - 1)
        def _():
            dk_ref[...] = dka_ref[PAD:, :].astype(BF16)
            dv_ref[...] = dva_ref[PAD:, :].astype(BF16)

    dkv_shape = jax.ShapeDtypeStruct((T, D), BF16)
    nbuf = TILES_PER_STEP * npairs
    return _call(
        body, name=name, grid=(D // lanes, B, nt),
        in_specs=[q_spec, k_spec, v_spec, bias_spec, q_spec],
        out_specs=[q_spec, k_spec, k_spec, bias_spec],
        out_shape=[jax.ShapeDtypeStruct((T, D), BF16), dkv_shape, dkv_shape,
                   jax.ShapeDtypeStruct((H, Q_TILE, WIN), F32)],
        scratch_shapes=[pltpu.VMEM((S + PAD, lanes), BF16), pltpu.VMEM((S + PAD, lanes), BF16),
                        pltpu.VMEM((S + PAD, lanes), F32), pltpu.VMEM((S + PAD, lanes), F32),
                        pltpu.VMEM((nbuf, 2 * Q_TILE, WIN), F32), pltpu.VMEM((nbuf, 2 * Q_TILE, WIN), F32),
                        pltpu.VMEM((nbuf, 2 * Q_TILE, WIN), BF16), pltpu.VMEM((nbuf, 2 * Q_TILE, WIN), BF16)],
        args=(q, kv, kv, bias, do), rider=rider)


def _loss_head(h, g, target, name):
    T, D = h.shape
    tm = _tile(T, MATMUL_TILE)

    def body(h_ref, g_ref, t_ref, dh_ref, st_ref):
        @pl.when(pl.program_id(0) == 0)
        def _():
            st_ref[...] = jnp.zeros_like(st_ref)

        xh, r = _rms(h_ref[...], None)
        err = xh * g_ref[...] - t_ref[...]
        st_ref[1:2, :] += 0.5 * jnp.sum(jnp.mean(err * err, axis=-1, keepdims=True), axis=0, keepdims=True)
        dy = err * (1.0 / D)
        st_ref[0:1, :] += jnp.sum(dy * xh, axis=0, keepdims=True)
        dn = dy * g_ref[...]
        dh_ref[...] = r * (dn - xh * jnp.mean(dn * xh, axis=-1, keepdims=True))

    row = pl.BlockSpec((tm, D), lambda i: (i, 0))
    return pl.pallas_call(
        body, name=name, grid=(T // tm,),
        in_specs=[row, pl.BlockSpec((1, D), lambda i: (0, 0)), row],
        out_specs=[row, pl.BlockSpec((8, D), lambda i: (0, 0))],
        out_shape=[jax.ShapeDtypeStruct((T, D), F32), jax.ShapeDtypeStruct((8, D), F32)],
        compiler_params=_params(("arbitrary",)),
    )(h, g, target)


def _sum_devices(arrs, name):
    n = len(arrs)

    def body(*refs):
        for a in range(n):
            s = refs[a][0].astype(F32)
            for k in range(1, N_DEV):
                s = s + refs[a][k].astype(F32)
            refs[n + a][...] = s

    vm = pl.BlockSpec(memory_space=pltpu.VMEM)
    return pl.pallas_call(
        body, name=name, out_shape=[jax.ShapeDtypeStruct(a.shape[1:], F32) for a in arrs],
        in_specs=[vm] * n, out_specs=[vm] * n, compiler_params=_params(),
    )(*arrs)


def _adamw_math(w, g, m, v):
    m = ADAM_B1 * m + (1.0 - ADAM_B1) * g
    v = ADAM_B2 * v + (1.0 - ADAM_B2) * (g * g)
    m_hat = m / (1.0 - ADAM_B1 ** ADAM_STEP)
    v_hat = v / (1.0 - ADAM_B2 ** ADAM_STEP)
    delta = -ADAM_LR * (m_hat / (jnp.sqrt(v_hat) + ADAM_EPS) + ADAM_WD * w)
    return delta, m, v


def _adamw_small(items, name):
    n = len(items)

    def body(*refs):
        for a in range(n):
            w, m, v, g = (refs[4 * a + k][...] for k in range(4))
            d, m, v = _adamw_math(w, g, m, v)
            refs[4 * n + 3 * a][...] = d
            refs[4 * n + 3 * a + 1][...] = m
            refs[4 * n + 3 * a + 2][...] = v

    vm = pl.BlockSpec(memory_space=pltpu.VMEM)
    flat = [t for it in items for t in it]
    outs = pl.pallas_call(
        body, name=name,
        out_shape=[jax.ShapeDtypeStruct(it[0].shape, F32) for it in items for _ in range(3)],
        in_specs=[vm] * (4 * n), out_specs=[vm] * (3 * n), compiler_params=_params(),
    )(*flat)
    return [tuple(outs[3 * a:3 * a + 3]) for a in range(n)]


def _adamw_big(w, m, v, owns, landeds, name, rider=None):
    L, R, C = w.shape
    tr = _tile(R, 512)
    nr = R // tr
    counts = [len(ls) for ls in landeds]

    def body(*refs):
        w_ref, m_ref, v_ref = refs[:3]
        g_ref, d_ref, mo_ref, vo_ref = refs[-4:]
        layer = pl.program_id(0)
        at = 3
        for j in range(L):
            own_ref, l_refs = refs[at], refs[at + 1:at + 1 + counts[j]]
            at += 1 + counts[j]

            @pl.when(layer == j)
            def _(own_ref=own_ref, l_refs=l_refs):
                g = own_ref[...]
                for l_ref in l_refs:
                    for k in range(l_ref.shape[0]):
                        g = g + l_ref[k].astype(F32)
                d, mn, vn = _adamw_math(w_ref[0], g, m_ref[0], v_ref[0])
                g_ref[0] = g
                d_ref[0] = d
                mo_ref[0] = mn
                vo_ref[0] = vn

    def pinned(j):
        return lambda l, i: jnp.where(l == j, i, jnp.where(l < j, 0, nr - 1))

    row = pl.BlockSpec((1, tr, C), lambda l, i: (l, i, 0))
    in_specs, args = [row, row, row], [w, m, v]
    for j in range(L):
        in_specs.append(pl.BlockSpec((tr, C), lambda l, i, p=pinned(j): (p(l, i), 0)))
        args.append(owns[j])
        for arr in landeds[j]:
            in_specs.append(pl.BlockSpec((arr.shape[0], tr, C), lambda l, i, p=pinned(j): (0, p(l, i), 0)))
            args.append(arr)
    return _call(body, name=name, grid=(L, nr), in_specs=in_specs, out_specs=[row] * 4,
                 out_shape=[jax.ShapeDtypeStruct((L, R, C), F32)] * 4, args=args, rider=rider)


def kernel(x, a_norm_g, a_w_in, a_v_norm_g, a_w_s, a_b_s, a_w_out, kv_norm_g, w_kv, b_norm_g, b_w_q, b_rel_bias, b_w_o, f_norm_g, f_w_in, f_conv_w, f_conv_b, f_w_down, final_norm_g, loss_target, m_a_norm_g, m_a_w_in, m_a_v_norm_g, m_a_w_s, m_a_b_s, m_a_w_out, m_kv_norm_g, m_w_kv, m_b_norm_g, m_b_w_q, m_b_rel_bias, m_b_w_o, m_f_norm_g, m_f_w_in, m_f_conv_w, m_f_conv_b, m_f_w_down, m_final_norm_g, v_a_norm_g, v_a_w_in, v_a_v_norm_g, v_a_w_s, v_a_b_s, v_a_w_out, v_kv_norm_g, v_w_kv, v_b_norm_g, v_b_w_q, v_b_rel_bias, v_b_w_o, v_f_norm_g, v_f_w_in, v_f_conv_w, v_f_conv_b, v_f_w_down, v_final_norm_g):
    B, S, D = x.shape
    T = B * S
    G = a_w_s.shape[1]
    H = D // HEAD_DIM
    F = f_w_in.shape[2]
    L = f_w_in.shape[0]
    dn = D // N_DEV
    xi, yi, ci = lax.axis_index("x"), lax.axis_index("y"), lax.axis_index("c")
    me = 4 * xi + 2 * yi + ci
    pos = jnp.stack([ci, 2 * xi + yi]).astype(jnp.int32)

    cast = lambda t: t.astype(BF16)
    gather = lambda *ts: _gather_rider(list(ts))
    rel = jnp.pad(b_rel_bias[0], ((0, 0), (0, REL_PAD - b_rel_bias.shape[2])))
    bias, (wa_in, norms_sh, conv_w0, conv_w1) = _bias_build(rel, "bias_build", rider=gather(
        cast(a_w_in[0]), jnp.concatenate([a_norm_g, a_v_norm_g], axis=0), f_conv_w[0], f_conv_w[1]))
    ga = jnp.transpose(norms_sh, (1, 0, 2)).reshape(2, D)
    g_a, g_av = ga[0:1], ga[1:2]

    x2 = x.reshape(T, D)
    tgt = loss_target.reshape(T, D)
    pc = jnp.arange(GMLP_BLOCK) // CHUNK
    mask = (pc[:, None] >= pc[None, :]).astype(F32)
    ws = (a_w_s[0] * mask[None]).astype(BF16)
    bst = jnp.transpose(a_b_s[0])
    four = lambda t: t.reshape((4, 2) + t.shape[1:])

    w_in0_sh = cast(f_w_in[0])
    (z, n_a), (wa_out, w_in0_top) = _norm_matmul(x2, g_a, wa_in, flat=True, nbk=4, name="gmlp_in",
                                                 rider=gather(cast(a_w_out[0]), w_in0_sh[:D // 2]))
    wa_out = wa_out.reshape(D, D)
    (gated, h1), (w_in0_bottom, wf_down0) = _gmlp_forward(z, ws, bst, g_av, wa_out, x2, name="gmlp_mix",
                                                          rider=gather(w_in0_sh[D // 2:], cast(f_w_down[0])))
    w_in0 = jnp.concatenate([w_in0_top, w_in0_bottom], axis=1)
    cw0, cb0, wd0 = conv_w0, f_conv_b[0].reshape(8, 1, F), wf_down0.reshape(4, F, D)
    (au0, ag0, cu0, cg0, act0, n_f0, h2), (wkv, wq, w_in1) = _ffn_fused_forward(
        h1, f_norm_g[0:1], w_in0, cw0, cb0, wd0, S, name="ffn0_fwd",
        rider=gather(cast(w_kv), cast(b_w_q[0]), cast(f_w_in[1])))
    wq = wq.reshape(D, D)
    kv, n_kv = _norm_matmul(h2, kv_norm_g.reshape(1, D), wkv, flat=True, nbk=4, name="kv_proj")
    q, n_q = _norm_matmul(h2, b_norm_g, wq.reshape(1, D, D), flat=True, nbk=1, name="q_proj", scale=HEAD_DIM ** -0.5)
    o, (wo, wf_down1) = _attn_forward(q, kv, bias, S, name="attn", rider=gather(cast(b_w_o[0]), cast(f_w_down[1])))
    wo = wo.reshape(D, D)
    cw1, cb1, wd1 = conv_w1, f_conv_b[1].reshape(8, 1, F), wf_down1.reshape(4, F, D)
    h3 = _matmul_residual(o, wo, h2, "attn_out")
    au1, ag1, cu1, cg1, act1, n_f1, h4 = _ffn_fused_forward(h3, f_norm_g[1:2], w_in1, cw1, cb1, wd1, S, name="ffn1_fwd")

    sums, from_chips = {}, {}

    def sibling_sums(names, parts, landed):
        for nm, p, l in zip(names, parts, landed):
            sums[nm] = _sibling_sum(p, l, pos, "grad_sibling_sum_" + nm)

    def chip_rider(*names):
        return _chip_rider([sums[nm][1] for nm in names])

    dh4, st_final = _loss_head(h4, final_norm_g.reshape(1, D), tgt, "loss_head")
    g_wd1 = _wgrad_rows(act1, dh4, flat=False, tk=F, name="ffn1_dwdown")
    parts = [four(g_wd1.reshape(8, F // 2, D))]
    (da1, st_conv1, dh3, st_f1), landed = _ffn_fused_backward(
        dh4, cu1, cg1, au1, ag1, cw1, wd1, w_in1, h3, f_norm_g[1:2], S, name="ffn1_bwd", rider=_sibling_rider(parts))
    sibling_sums(["wd1"], parts, landed)
    g_win1, (from_chips["wd1"],) = _wgrad_cols(n_f1, da1, flat=False, nb=8, nbk=2, name="ffn1_dwin",
                                               rider=chip_rider("wd1"))
    d_o = _matmul_nt(dh3, wo.reshape(1, D, D), flat=True, nbk=1, name="attn_out_dx")
    parts = [four(g_win1)]
    g_wo, landed = _wgrad_rows(o, dh3, flat=True, tk=_tile(D, 512), name="attn_out_dw", rider=_sibling_rider(parts))
    sibling_sums(["win1"], parts, landed)
    (dq, dk, dv, dbias), (from_chips["win1"],) = _attn_backward(
        q, kv, bias, d_o, S, name="attn_bwd", rider=chip_rider("win1"))
    g_rel = _bias_reduce(dbias, "bias_reduce")
    g_wq = _wgrad_cols(n_q, dq, flat=True, nb=1, nbk=1, name="q_dw")
    dh2, st_b = _matmul_nt(dq, wq.reshape(1, D, D), flat=True, nbk=1, name="q_dx", norm=(h2, b_norm_g, dh3))
    dkv = jnp.concatenate([dk, dv], axis=-1)
    g_wkv = _wgrad_cols(n_kv, dkv, flat=True, nb=8, nbk=4, name="kv_dw")
    parts = [four(g_wo.reshape(8, dn, D)), four(g_wq.reshape(8, dn, D)), four(g_wkv)]
    (dh2, st_kv), landed = _matmul_nt(dkv, wkv, flat=True, nbk=4, name="kv_dx",
                                      norm=(h2, kv_norm_g.reshape(1, D), dh2), rider=_sibling_rider(parts))
    sibling_sums(["wo", "wq", "wkv"], parts, landed)
    g_wd0, (from_chips["wo"], from_chips["wq"], from_chips["wkv"]) = _wgrad_rows(
        act0, dh2, flat=False, tk=F, name="ffn0_dwdown", rider=chip_rider("wo", "wq", "wkv"))
    parts = [four(g_wd0.reshape(8, F // 2, D))]
    (da0, st_conv0, dh1, st_f0), landed = _ffn_fused_backward(
        dh2, cu0, cg0, au0, ag0, cw0, wd0, w_in0, h1, f_norm_g[0:1], S, name="ffn0_bwd", rider=_sibling_rider(parts))
    sibling_sums(["wd0"], parts, landed)
    g_win0, (ce,) = _wgrad_cols(n_f0, da0, flat=False, nb=8, nbk=2, name="ffn0_dwin", rider=chip_rider("wd0"))
    from_chips["wd0"] = [ce]
    dgated = _matmul_nt(dh1, wa_out.reshape(1, D, D), flat=True, nbk=1, name="gmlp_out_dx")
    parts = [four(g_win0)]
    g_wa_out, landed = _wgrad_rows(gated, dh1, flat=True, tk=_tile(D, 512), name="gmlp_out_dw",
                                   rider=_sibling_rider(parts))
    sibling_sums(["win0"], parts, landed)
    parts = [four(g_wa_out.reshape(8, dn, D))]
    (dz, g_ws, g_bst, st_av), (ce_win0_a, landed) = _gmlp_backward(
        z, dgated, ws, bst, g_av, mask, name="gmlp_bwd",
        rider=_join_riders([_chip_rider([sums["win0"][1]], ks=(1, 2)), _sibling_rider(parts)]))
    sibling_sums(["wa_out"], parts, [landed])
    g_wa_in, (ce_win0_b, ce) = _wgrad_cols(n_a, dz, flat=True, nb=8, nbk=4, name="gmlp_in_dw", rider=_join_riders(
        [_chip_rider([sums["win0"][1]], ks=(3,)), chip_rider("wa_out")]))
    from_chips["win0"], from_chips["wa_out"] = [ce_win0_a, ce_win0_b], [ce]
    vec = jnp.concatenate([st_av[0:1], st_kv[0:1], st_b[0:1], st_f0[0:1], st_f1[0:1], st_final[0:3]], axis=0)
    parts = [four(g_wa_in)]
    (grad_x, st_a), got = _matmul_nt(dz, wa_in, flat=True, nbk=4, name="gmlp_in_dx", norm=(x2, g_a, dh1),
                                     rider=_join_riders([_sibling_rider(parts), gather(
                                         vec, cast(g_ws), cast(g_bst), cast(g_rel), cast(st_conv0), cast(st_conv1))]))
    sibling_sums(["wa_in"], parts, got[0:1])
    small = got[1:]

    def big_update(names, w, m, v, rider=None):
        shape = w.shape
        r = lambda t: t.reshape((len(names), -1, shape[-1]))
        as_list = lambda t: t if isinstance(t, list) else [t]
        outs = _adamw_big(r(w), r(m), r(v), [sums[nm][0] for nm in names],
                          [as_list(from_chips[nm]) for nm in names], "adamw_" + names[0], rider=rider)
        outs, got = (outs, None) if rider is None else outs
        return [t.reshape(shape) for t in outs], got

    u_f_w_in, (ce, st_a) = big_update(["win0", "win1"], f_w_in, m_f_w_in, v_f_w_in,
                                      rider=_join_riders([chip_rider("wa_in"), gather(st_a)]))
    from_chips["wa_in"] = [ce]
    u_f_w_down, _ = big_update(["wd0", "wd1"], f_w_down, m_f_w_down, v_f_w_down)
    u_a_w_in, _ = big_update(["wa_in"], a_w_in, m_a_w_in, v_a_w_in)
    u_w_kv, _ = big_update(["wkv"], w_kv, m_w_kv, v_w_kv)
    u_a_w_out, _ = big_update(["wa_out"], a_w_out, m_a_w_out, v_a_w_out)
    u_b_w_q, _ = big_update(["wq"], b_w_q, m_b_w_q, v_b_w_q)
    u_b_w_o, _ = big_update(["wo"], b_w_o, m_b_w_o, v_b_w_o)

    vec, g_ws, g_bst, g_rel, st_conv0, st_conv1, st_a = _sum_devices(list(small) + [st_a], "sum_small_grads")
    vec = jnp.concatenate([st_a[0:1], vec[0:7]], axis=0)
    loss = vec[7, 0]
    g_a_norm = lax.dynamic_slice_in_dim(vec[0:1], me * dn, dn, axis=1)
    g_av_norm = lax.dynamic_slice_in_dim(vec[1:2], me * dn, dn, axis=1)
    st_conv = jnp.stack([st_conv0, st_conv1])
    g_conv_w = lax.dynamic_index_in_dim(st_conv, me, axis=1, keepdims=False)[:, 0:3]
    g_conv_b = st_conv[:, :, 3, :].reshape(L, 8 * F)
    small_items = [
        (a_norm_g, m_a_norm_g, v_a_norm_g, g_a_norm),
        (a_v_norm_g, m_a_v_norm_g, v_a_v_norm_g, g_av_norm),
        (a_w_s, m_a_w_s, v_a_w_s, g_ws[None]),
        (a_b_s, m_a_b_s, v_a_b_s, jnp.transpose(g_bst)[None]),
        (kv_norm_g.reshape(1, D), m_kv_norm_g.reshape(1, D), v_kv_norm_g.reshape(1, D), vec[2:3]),
        (b_norm_g, m_b_norm_g, v_b_norm_g, vec[3:4]),
        (b_rel_bias, m_b_rel_bias, v_b_rel_bias, g_rel[None, :, :b_rel_bias.shape[2]]),
        (f_norm_g, m_f_norm_g, v_f_norm_g, vec[4:6]),
        (f_conv_w, m_f_conv_w, v_f_conv_w, g_conv_w),
        (f_conv_b, m_f_conv_b, v_f_conv_b, g_conv_b),
        (final_norm_g.reshape(1, D), m_final_norm_g.reshape(1, D), v_final_norm_g.reshape(1, D), vec[6:7]),
    ]
    small_out = _adamw_small(small_items, "adamw_small")
    (u_a_norm, u_av_norm, u_ws, u_bs, u_kvn, u_bn, u_rel, u_fn, u_cw, u_cb, u_fin) = [
        (it[3],) + so for it, so in zip(small_items, small_out)]
    vecD = lambda u: tuple(t.reshape(D) for t in u)
    u_kvn, u_fin = vecD(u_kvn), vecD(u_fin)

    order = [u_a_norm, u_a_w_in, u_av_norm, u_ws, u_bs, u_a_w_out, u_kvn, u_w_kv, u_bn, u_b_w_q, u_rel, u_b_w_o,
             u_fn, u_f_w_in, u_cw, u_cb, u_f_w_down, u_fin]
    outs = [loss, grad_x.reshape(B, S, D)]
    for k in range(4):
        outs += [u[k] for u in order]
    return tuple(outs)
```

```python
import functools

import jax
import jax.numpy as jnp
from jax import lax
from jax.experimental import pallas as pl
from jax.experimental.pallas import tpu as pltpu

F32 = jnp.float32
BF16 = jnp.bfloat16
MESH = pl.DeviceIdType.MESH

N_DEV = 8
EPS = 1e-6
NEG_INF = -1e30
CHUNK = 64
LEFT_CHUNKS = 8
REL_CLIP = 128
HEAD_DIM = 64
GMLP_BLOCK = 128
Q_TILE = 2 * CHUNK
PAD = LEFT_CHUNKS * CHUNK
WIN = PAD + Q_TILE
SKEW = WIN + Q_TILE
REL_PAD = 384
FWD_HEADS_PER_STEP = 8
BWD_HEADS_PER_STEP = 4
TILES_PER_STEP = 2
STRIP = 32
ROWS = 32
ADAM_LR, ADAM_B1, ADAM_B2, ADAM_EPS, ADAM_WD, ADAM_STEP = 0.001, 0.9, 0.999, 1e-08, 0.01, 10
VMEM_LIMIT = 56 * 1024 * 1024
TOKEN_TILE = 512
MATMUL_TILE = 1024
FFN_BLOCKS_PER_STEP = 2


def _params(sem=None):
    return pltpu.CompilerParams(dimension_semantics=sem, vmem_limit_bytes=VMEM_LIMIT)


def _tile(n, pref):
    if n <= pref:
        return n
    for t in range(pref - pref % 8, 7, -8):
        if n % t == 0:
            return t
    return n


def _gelu(x):
    return 0.5 * x * (1.0 + jnp.tanh(0.7978845608028654 * (x + 0.044715 * x * x * x)))


def _gelu_grad(x):
    t = jnp.tanh(0.7978845608028654 * (x + 0.044715 * x * x * x))
    return 0.5 * (1.0 + t) + 0.5 * x * (1.0 - t * t) * 0.7978845608028654 * (1.0 + 3 * 0.044715 * x * x)


def _sigmoid(x):
    return 1.0 / (1.0 + jnp.exp(-x))


def _dot(a, b):
    return jnp.dot(a, b, preferred_element_type=F32)


def _dot_nt(a, b):
    return lax.dot_general(a, b, (((1,), (1,)), ((), ())), preferred_element_type=F32)


def _dot_tn(a, b):
    return lax.dot_general(a, b, (((0,), (0,)), ((), ())), preferred_element_type=F32)


def _split3(x):
    hi = x.astype(BF16)
    r1 = x - hi.astype(F32)
    mid = r1.astype(BF16)
    lo = (r1 - mid.astype(F32)).astype(BF16)
    return hi, mid, lo


def _mesh_pos():
    return lax.axis_index("x"), lax.axis_index("y"), lax.axis_index("c")


class _Rider:
    def __init__(self, arrs, out_shapes, sems, start, finish):
        self.arrs, self.out_shapes, self.sems, self.start, self.finish = arrs, out_shapes, sems, start, finish


def _gather_rider(arrs):
    n = len(arrs)

    def tools(ins, outs, sems):
        send_sems, recv_sems, local_sems = sems
        x, y, c = _mesh_pos()
        me, sibling = (x, y, c), (x, y, 1 - c)
        chips = [(1 - x, y), (x, 1 - y), (1 - x, 1 - y)]

        def slot(a, block):
            px, py, pc = block
            return outs[a].at[4 * px + 2 * py + pc]

        def copy(a, k, block, to, src=None):
            dst = slot(a, block)
            return pltpu.make_async_remote_copy(
                src_ref=dst if src is None else src, dst_ref=dst,
                send_sem=send_sems.at[a, k], recv_sem=recv_sems.at[a, k], device_id=to, device_id_type=MESH)

        def first(a):
            cps = [copy(a, 0, me, sibling, src=ins[a])]
            return cps + [copy(a, 1 + j, me, (*chip, c), src=ins[a]) for j, chip in enumerate(chips)]

        def mine(a):
            return pltpu.make_async_copy(ins[a], slot(a, me), local_sems.at[a])

        return me, sibling, chips, c, copy, first, mine

    def start(ins, outs, sems):
        _, _, _, _, _, first, mine = tools(ins, outs, sems)
        for a in range(n):
            mine(a).start()
            for cp in first(a):
                cp.start()

    def finish(ins, outs, sems):
        me, sibling, chips, c, copy, first, mine = tools(ins, outs, sems)
        passed = []
        for j, chip in enumerate(chips):
            for a in range(n):
                copy(a, 1 + j, (*chip, c), me).wait_recv()
                fwd = copy(a, 4 + j, (*chip, c), sibling)
                fwd.start()
                passed.append(fwd)
        for a in range(n):
            copy(a, 0, sibling, me).wait_recv()
            for j, chip in enumerate(chips):
                copy(a, 4 + j, (*chip, 1 - c), me).wait_recv()
        for a in range(n):
            for cp in first(a):
                cp.wait_send()
        for cp in passed:
            cp.wait_send()
        for a in range(n):
            mine(a).wait()

    return _Rider(list(arrs), [jax.ShapeDtypeStruct((N_DEV,) + a.shape, a.dtype) for a in arrs],
                  [pltpu.SemaphoreType.DMA((n, 7)), pltpu.SemaphoreType.DMA((n, 7)), pltpu.SemaphoreType.DMA((n,))],
                  start, finish)


def _sibling_rider(arrs):
    n = len(arrs)

    def copies(ins, outs, sems):
        x, y, c = _mesh_pos()
        return [pltpu.make_async_remote_copy(
            src_ref=ins[a].at[:, pl.ds(1 - c, 1)], dst_ref=outs[a],
            send_sem=sems[0].at[a], recv_sem=sems[1].at[a], device_id=(x, y, 1 - c), device_id_type=MESH)
            for a in range(n)]

    def start(ins, outs, sems):
        for cp in copies(ins, outs, sems):
            cp.start()

    def finish(ins, outs, sems):
        for cp in copies(ins, outs, sems):
            cp.wait()

    return _Rider(list(arrs), [jax.ShapeDtypeStruct((4, 1) + a.shape[2:], a.dtype) for a in arrs],
                  [pltpu.SemaphoreType.DMA((n,)), pltpu.SemaphoreType.DMA((n,))], start, finish)


def _chip_rider(arrs, ks=(1, 2, 3)):
    n = len(arrs)

    def copies(ins, outs, sems):
        x, y, c = _mesh_pos()
        cps = []
        for a in range(n):
            for s, k in enumerate(ks):
                px = x if k < 2 else 1 - x
                py = y if k == 2 else 1 - y
                cps.append(pltpu.make_async_remote_copy(
                    src_ref=ins[a].at[2 * px + py], dst_ref=outs[a].at[s],
                    send_sem=sems[0].at[a, s], recv_sem=sems[1].at[a, s],
                    device_id=(px, py, c), device_id_type=MESH))
        return cps

    def start(ins, outs, sems):
        for cp in copies(ins, outs, sems):
            cp.start()

    def finish(ins, outs, sems):
        for cp in copies(ins, outs, sems):
            cp.wait()

    return _Rider(list(arrs), [jax.ShapeDtypeStruct((len(ks),) + a.shape[1:], a.dtype) for a in arrs],
                  [pltpu.SemaphoreType.DMA((n, len(ks))), pltpu.SemaphoreType.DMA((n, len(ks)))], start, finish)


def _join_riders(riders):
    def split(seq, counts):
        out, at = [], 0
        for k in counts:
            out.append(seq[at:at + k])
            at += k
        return out

    n_in = [len(r.arrs) for r in riders]
    n_out = [len(r.out_shapes) for r in riders]
    n_sem = [len(r.sems) for r in riders]

    def run(which):
        def fn(ins, outs, sems):
            for r, i, o, s in zip(riders, split(ins, n_in), split(outs, n_out), split(sems, n_sem)):
                getattr(r, which)(i, o, s)
        return fn

    return _Rider([a for r in riders for a in r.arrs], [o for r in riders for o in r.out_shapes],
                  [s for r in riders for s in r.sems], run("start"), run("finish"))


def _run_rider(rider, name):
    n_in, n_out = len(rider.arrs), len(rider.out_shapes)

    def body(*refs):
        ins, outs, sems = refs[:n_in], refs[n_in:n_in + n_out], refs[n_in + n_out:]
        rider.start(ins, outs, sems)
        rider.finish(ins, outs, sems)

    any_spec = pl.BlockSpec(memory_space=pl.ANY)
    return pl.pallas_call(
        body, name=name, out_shape=list(rider.out_shapes), in_specs=[any_spec] * n_in, out_specs=[any_spec] * n_out,
        scratch_shapes=list(rider.sems),
    )(*rider.arrs)


def _call(body, *, name, grid, in_specs, out_specs, out_shape, args, scratch_shapes=(), rider=None):
    params = _params(("arbitrary",) * len(grid))
    if rider is None:
        return pl.pallas_call(body, name=name, grid=grid, in_specs=in_specs, out_specs=out_specs, out_shape=out_shape,
                              scratch_shapes=list(scratch_shapes), compiler_params=params)(*args)
    single = not isinstance(out_shape, (list, tuple))
    outs = [out_shape] if single else list(out_shape)
    ospecs = [out_specs] if single else list(out_specs)
    n_in, n_out, n_scr = len(in_specs), len(outs), len(scratch_shapes)
    r_in, r_out = len(rider.arrs), len(rider.out_shapes)

    def hosted(*refs):
        refs = list(refs)
        ins, rins = refs[:n_in], refs[n_in:n_in + r_in]
        refs = refs[n_in + r_in:]
        houts, routs = refs[:n_out], refs[n_out:n_out + r_out]
        refs = refs[n_out + r_out:]
        scr, rsems = refs[:n_scr], refs[n_scr:]
        ids = [pl.program_id(a) for a in range(len(grid))]
        first = functools.reduce(lambda p, q: p & q, [i == 0 for i in ids])
        last = functools.reduce(lambda p, q: p & q, [i == g - 1 for i, g in zip(ids, grid)])

        @pl.when(first)
        def _():
            rider.start(rins, routs, rsems)

        body(*ins, *houts, *scr)

        @pl.when(last)
        def _():
            rider.finish(rins, routs, rsems)

    any_spec = pl.BlockSpec(memory_space=pl.ANY)
    res = pl.pallas_call(
        hosted, name=name, grid=grid, in_specs=list(in_specs) + [any_spec] * r_in,
        out_specs=ospecs + [any_spec] * r_out, out_shape=outs + list(rider.out_shapes),
        scratch_shapes=list(scratch_shapes) + list(rider.sems), compiler_params=params,
    )(*args, *rider.arrs)
    return (res[0] if single else list(res[:n_out])), list(res[n_out:])


def _sibling_sum(part, landed, pos, name):
    _, _, rows, cols = part.shape
    tr = _tile(rows, 512)

    def body(pos_ref, p_ref, l_ref, own_ref, all_ref):
        s = p_ref[0, 0] + l_ref[0, 0]
        all_ref[0] = s.astype(BF16)

        @pl.when(pl.program_id(1) == pos_ref[1])
        def _():
            own_ref[...] = s

    return pl.pallas_call(
        body, name=name,
        grid_spec=pltpu.PrefetchScalarGridSpec(
            num_scalar_prefetch=1, grid=(rows // tr, 4),
            in_specs=[pl.BlockSpec((1, 1, tr, cols), lambda i, k, pos: (k, pos[0], i, 0)),
                      pl.BlockSpec((1, 1, tr, cols), lambda i, k, pos: (k, 0, i, 0))],
            out_specs=[pl.BlockSpec((tr, cols), lambda i, k, pos: (i, 0)),
                       pl.BlockSpec((1, tr, cols), lambda i, k, pos: (k, i, 0))]),
        out_shape=[jax.ShapeDtypeStruct((rows, cols), F32), jax.ShapeDtypeStruct((4, rows, cols), BF16)],
        compiler_params=_params(("arbitrary", "arbitrary")),
    )(pos, part, landed)


def _rms(x, g):
    r = lax.rsqrt(jnp.mean(x * x, axis=-1, keepdims=True) + EPS)
    return x * r, r


def _norm_matmul(h, g, w, *, flat, nbk, name, scale=1.0, rider=None):
    T, D = h.shape
    nb, _, bn = w.shape
    tm = _tile(T, MATMUL_TILE)

    def body(h_ref, g_ref, w_ref, o_ref, n_ref):
        @pl.when(pl.program_id(1) == 0)
        def _():
            xh, _ = _rms(h_ref[...], None)
            n_ref[...] = (xh * g_ref[...]).astype(BF16)

        n = n_ref[...]
        for k in range(nbk):
            r = _dot(n, w_ref[k])
            r = (r if scale == 1.0 else r * scale).astype(BF16)
            if flat:
                o_ref[:, k * bn:(k + 1) * bn] = r
            else:
                o_ref[k] = r

    if flat:
        out_shape = jax.ShapeDtypeStruct((T, nb * bn), BF16)
        out_spec = pl.BlockSpec((tm, nbk * bn), lambda i, j: (i, j))
    else:
        out_shape = jax.ShapeDtypeStruct((nb, T, bn), BF16)
        out_spec = pl.BlockSpec((nbk, tm, bn), lambda i, j: (j, i, 0))
    return _call(
        body, name=name, grid=(T // tm, nb // nbk),
        in_specs=[pl.BlockSpec((tm, D), lambda i, j: (i, 0)),
                  pl.BlockSpec((1, D), lambda i, j: (0, 0)),
                  pl.BlockSpec((nbk, D, bn), lambda i, j: (j, 0, 0))],
        out_specs=[out_spec, pl.BlockSpec((tm, D), lambda i, j: (i, 0))],
        out_shape=[out_shape, jax.ShapeDtypeStruct((T, D), BF16)],
        args=(h, g, w), rider=rider)


def _matmul_nt(dy, w, *, flat, nbk, name, norm=None, out_dtype=BF16, rider=None):
    nb, R, bn = w.shape
    T = dy.shape[0] if flat else dy.shape[1]
    tm = _tile(T, MATMUL_TILE)
    nj = nb // nbk

    def body(*refs):
        if norm is None:
            dy_ref, w_ref, o_ref, acc_ref = refs
        else:
            dy_ref, w_ref, h_ref, g_ref, dres_ref, o_ref, dg_ref, acc_ref = refs
        i, j = pl.program_id(0), pl.program_id(1)

        @pl.when(j == 0)
        def _():
            acc_ref[...] = jnp.zeros_like(acc_ref)

        acc = acc_ref[...]
        for k in range(nbk):
            d = dy_ref[:, k * bn:(k + 1) * bn] if flat else dy_ref[k]
            acc = acc + _dot_nt(d.astype(BF16), w_ref[k])
        acc_ref[...] = acc

        @pl.when(j == nj - 1)
        def _():
            if norm is None:
                o_ref[...] = acc.astype(out_dtype)
            else:
                xh, r = _rms(h_ref[...], None)

                @pl.when(i == 0)
                def _():
                    dg_ref[...] = jnp.zeros_like(dg_ref)

                dg_ref[0:1, :] += jnp.sum(acc * xh, axis=0, keepdims=True)
                dn = acc * g_ref[...]
                o_ref[...] = dres_ref[...] + r * (dn - xh * jnp.mean(dn * xh, axis=-1, keepdims=True))

    if flat:
        dy_spec = pl.BlockSpec((tm, nbk * bn), lambda i, j: (i, j))
    else:
        dy_spec = pl.BlockSpec((nbk, tm, bn), lambda i, j: (j, i, 0))
    w_spec = pl.BlockSpec((nbk, R, bn), lambda i, j: (j, 0, 0))
    row_spec = pl.BlockSpec((tm, R), lambda i, j: (i, 0))
    if norm is None:
        in_specs, args = [dy_spec, w_spec], (dy, w)
        out_specs = row_spec
        out_shape = jax.ShapeDtypeStruct((T, R), out_dtype)
    else:
        in_specs = [dy_spec, w_spec, row_spec, pl.BlockSpec((1, R), lambda i, j: (0, 0)), row_spec]
        args = (dy, w) + tuple(norm)
        out_specs = [row_spec, pl.BlockSpec((8, R), lambda i, j: (0, 0))]
        out_shape = [jax.ShapeDtypeStruct((T, R), F32), jax.ShapeDtypeStruct((8, R), F32)]
    return _call(
        body, name=name, grid=(T // tm, nj), in_specs=in_specs, out_specs=out_specs, out_shape=out_shape,
        scratch_shapes=[pltpu.VMEM((tm, R), F32)], args=args, rider=rider)


def _wgrad_cols(n, dy, *, flat, nb, nbk, name, rider=None):
    T, D = n.shape
    bn = dy.shape[1] // nb if flat else dy.shape[2]
    tt = _tile(T, MATMUL_TILE)
    nt = T // tt

    def body(n_ref, dy_ref, o_ref, acc_ref):
        t = pl.program_id(1)

        @pl.when(t == 0)
        def _():
            acc_ref[...] = jnp.zeros_like(acc_ref)

        nv = n_ref[...]
        for k in range(nbk):
            d = dy_ref[:, k * bn:(k + 1) * bn] if flat else dy_ref[k]
            acc_ref[k] += _dot_tn(nv, d)

        @pl.when(t == nt - 1)
        def _():
            o_ref[...] = acc_ref[...]

    if flat:
        dy_spec = pl.BlockSpec((tt, nbk * bn), lambda j, t: (t, j))
    else:
        dy_spec = pl.BlockSpec((nbk, tt, bn), lambda j, t: (j, t, 0))
    return _call(
        body, name=name, grid=(nb // nbk, nt),
        in_specs=[pl.BlockSpec((tt, D), lambda j, t: (t, 0)), dy_spec],
        out_specs=pl.BlockSpec((nbk, D, bn), lambda j, t: (j, 0, 0)),
        out_shape=jax.ShapeDtypeStruct((nb, D, bn), F32),
        scratch_shapes=[pltpu.VMEM((nbk, D, bn), F32)], args=(n, dy), rider=rider)


def _wgrad_rows(xa, dh, *, flat, tk, name, rider=None):
    T, D = dh.shape
    nk = xa.shape[1] // tk if flat else xa.shape[0]
    tt = _tile(T, MATMUL_TILE)
    nt = T // tt

    def body(x_ref, dh_ref, o_ref, acc_ref):
        t = pl.program_id(1)

        @pl.when(t == 0)
        def _():
            acc_ref[...] = jnp.zeros_like(acc_ref)

        xv = x_ref[...] if flat else x_ref[0]
        acc_ref[...] += _dot_tn(xv, dh_ref[...].astype(BF16))

        @pl.when(t == nt - 1)
        def _():
            o_ref[...] = acc_ref[...]

    x_spec = pl.BlockSpec((tt, tk), lambda j, t: (t, j)) if flat else pl.BlockSpec((1, tt, tk), lambda j, t: (j, t, 0))
    return _call(
        body, name=name, grid=(nk, nt),
        in_specs=[x_spec, pl.BlockSpec((tt, D), lambda j, t: (t, 0))],
        out_specs=pl.BlockSpec((tk, D), lambda j, t: (j, 0)),
        out_shape=jax.ShapeDtypeStruct((nk * tk, D), F32),
        scratch_shapes=[pltpu.VMEM((tk, D), F32)], args=(xa, dh), rider=rider)


def _matmul_residual(xa, w, res, name):
    T, K = xa.shape
    D = w.shape[1]
    tm = _tile(T, MATMUL_TILE)

    def body(x_ref, w_ref, r_ref, o_ref):
        o_ref[...] = r_ref[...] + _dot(x_ref[...], w_ref[...])

    return pl.pallas_call(
        body, name=name, grid=(T // tm,),
        in_specs=[pl.BlockSpec((tm, K), lambda i: (i, 0)), pl.BlockSpec((K, D), lambda i: (0, 0)),
                  pl.BlockSpec((tm, D), lambda i: (i, 0))],
        out_specs=pl.BlockSpec((tm, D), lambda i: (i, 0)),
        out_shape=jax.ShapeDtypeStruct((T, D), F32),
        compiler_params=_params(("arbitrary",)),
    )(xa, w, res)


def _gmlp_gate(z, ws, bst, gv, G, gd):
    D = G * gd
    u = _gelu(z[:, :D].astype(F32))
    v = _gelu(z[:, D:].astype(F32))
    vh, r = _rms(v, None)
    vn = (vh * gv).astype(BF16)
    return u, v, vh, r, vn


def _gmlp_forward(z, ws, bst, gv, w_out, x, *, name, rider=None):
    T, D2 = z.shape
    D = D2 // 2
    G = ws.shape[0]
    gd = D // G
    tb = _tile(T, 256)
    nblk = tb // GMLP_BLOCK

    def body(z_ref, ws_ref, b_ref, gv_ref, wo_ref, x_ref, gated_ref, h_ref):
        u, _, _, _, vn = _gmlp_gate(z_ref[...], None, None, gv_ref[...], G, gd)
        for n in range(nblk):
            rows = slice(n * GMLP_BLOCK, (n + 1) * GMLP_BLOCK)
            for gi in range(G):
                cols = slice(gi * gd, (gi + 1) * gd)
                s = _dot(ws_ref[gi], vn[rows, cols]) + b_ref[:, gi:gi + 1]
                gated_ref[rows, cols] = (u[rows, cols] * s).astype(BF16)
        h_ref[...] = x_ref[...] + _dot(gated_ref[...], wo_ref[...])

    return _call(
        body, name=name, grid=(T // tb,),
        in_specs=[pl.BlockSpec((tb, D2), lambda i: (i, 0)), pl.BlockSpec(ws.shape, lambda i: (0, 0, 0)),
                  pl.BlockSpec(bst.shape, lambda i: (0, 0)), pl.BlockSpec((1, D), lambda i: (0, 0)),
                  pl.BlockSpec((D, D), lambda i: (0, 0)), pl.BlockSpec((tb, D), lambda i: (i, 0))],
        out_specs=[pl.BlockSpec((tb, D), lambda i: (i, 0)), pl.BlockSpec((tb, D), lambda i: (i, 0))],
        out_shape=[jax.ShapeDtypeStruct((T, D), BF16), jax.ShapeDtypeStruct((T, D), F32)],
        args=(z, ws, bst, gv, w_out, x), rider=rider)


def _gmlp_backward(z, dgated, ws, bst, gv, mask, *, name, rider=None):
    T, D2 = z.shape
    D = D2 // 2
    G = ws.shape[0]
    gd = D // G
    tb = _tile(T, 256)
    nblk = tb // GMLP_BLOCK

    def body(z_ref, dg_ref, ws_ref, b_ref, gv_ref, mask_ref, dz_ref, dws_ref, db_ref, dgv_ref, dvn_ref):
        @pl.when(pl.program_id(0) == 0)
        def _():
            dws_ref[...] = jnp.zeros_like(dws_ref)
            db_ref[...] = jnp.zeros_like(db_ref)
            dgv_ref[...] = jnp.zeros_like(dgv_ref)

        zf = z_ref[...]
        u, v, vh, r, vn = _gmlp_gate(zf, None, None, gv_ref[...], G, gd)
        dg = dg_ref[...].astype(F32)
        for n in range(nblk):
            rows = slice(n * GMLP_BLOCK, (n + 1) * GMLP_BLOCK)
            for gi in range(G):
                cols = slice(gi * gd, (gi + 1) * gd)
                vblk = vn[rows, cols]
                s = _dot(ws_ref[gi], vblk) + b_ref[:, gi:gi + 1]
                dgb = dg[rows, cols]
                ds = dgb * u[rows, cols]
                dsb = ds.astype(BF16)
                dz_ref[rows, cols] = (dgb * s * _gelu_grad(zf[rows, cols].astype(F32))).astype(BF16)
                dvn_ref[rows, cols] = _dot_tn(ws_ref[gi], dsb)
                dws_ref[gi] += _dot_nt(dsb, vblk) * mask_ref[...]
                db_ref[:, gi:gi + 1] += jnp.sum(ds, axis=1, keepdims=True)
        dvn = dvn_ref[...]
        dgv_ref[0:1, :] += jnp.sum(dvn * vh, axis=0, keepdims=True)
        dn = dvn * gv_ref[...]
        dv = r * (dn - vh * jnp.mean(dn * vh, axis=-1, keepdims=True))
        dz_ref[:, D:] = (dv * _gelu_grad(zf[:, D:].astype(F32))).astype(BF16)

    return _call(
        body, name=name, grid=(T // tb,),
        in_specs=[pl.BlockSpec((tb, D2), lambda i: (i, 0)), pl.BlockSpec((tb, D), lambda i: (i, 0)),
                  pl.BlockSpec(ws.shape, lambda i: (0, 0, 0)), pl.BlockSpec(bst.shape, lambda i: (0, 0)),
                  pl.BlockSpec((1, D), lambda i: (0, 0)), pl.BlockSpec(mask.shape, lambda i: (0, 0))],
        out_specs=[pl.BlockSpec((tb, D2), lambda i: (i, 0)), pl.BlockSpec(ws.shape, lambda i: (0, 0, 0)),
                   pl.BlockSpec(bst.shape, lambda i: (0, 0)), pl.BlockSpec((8, D), lambda i: (0, 0))],
        out_shape=[jax.ShapeDtypeStruct((T, D2), BF16), jax.ShapeDtypeStruct(ws.shape, F32),
                   jax.ShapeDtypeStruct(bst.shape, F32), jax.ShapeDtypeStruct((8, D), F32)],
        scratch_shapes=[pltpu.VMEM((tb, D), F32)], args=(z, dgated, ws, bst, gv, mask), rider=rider)


def _shift_rows(x, k):
    return pltpu.roll(x, k % x.shape[0], axis=0)


def _conv3(ext, cw):
    return (cw[0:1] * _shift_rows(ext, 2)[8:] + cw[1:2] * _shift_rows(ext, 1)[8:] + cw[2:3] * ext[8:])


def _ffn_forward(a, cw, cb, wd, h, seq, *, name, rider=None):
    _, T, F = a.shape
    D = h.shape[1]
    tm = _tile(seq, TOKEN_TILE)
    hb = tm // 16

    def body(a_ref, ap_ref, cw_ref, cb_ref, wd_ref, h_ref, act_ref, c_ref, o_ref, acc_ref):
        i, j = pl.program_id(0), pl.program_id(1)
        keep = ((i * tm) % seq != 0).astype(F32)

        def conv(b):
            ext = jnp.concatenate([ap_ref[b, 8:16].astype(F32) * keep, a_ref[b].astype(F32)], axis=0)
            return _conv3(ext, cw_ref[b]) + cb_ref[b]

        up, gate = conv(j), conv(j + 4)
        c_ref[j] = up.astype(BF16)
        c_ref[j + 4] = gate.astype(BF16)
        act = (gate * _sigmoid(gate) * up).astype(BF16)
        act_ref[0] = act

        @pl.when(j == 0)
        def _():
            acc_ref[...] = h_ref[...]

        acc_ref[...] += _dot(act, wd_ref[0])

        @pl.when(j == 3)
        def _():
            o_ref[...] = acc_ref[...]

    return _call(
        body, name=name, grid=(T // tm, 4),
        in_specs=[pl.BlockSpec((8, tm, F), lambda i, j: (0, i, 0)),
                  pl.BlockSpec((8, 16, F), lambda i, j: (0, jnp.maximum(i * hb - 1, 0), 0)),
                  pl.BlockSpec((8, 3, F), lambda i, j: (0, 0, 0)), pl.BlockSpec((8, 1, F), lambda i, j: (0, 0, 0)),
                  pl.BlockSpec((1, F, D), lambda i, j: (j, 0, 0)), pl.BlockSpec((tm, D), lambda i, j: (i, 0))],
        out_specs=[pl.BlockSpec((1, tm, F), lambda i, j: (j, i, 0)), pl.BlockSpec((8, tm, F), lambda i, j: (0, i, 0)),
                   pl.BlockSpec((tm, D), lambda i, j: (i, 0))],
        out_shape=[jax.ShapeDtypeStruct((4, T, F), BF16), jax.ShapeDtypeStruct((8, T, F), BF16),
                   jax.ShapeDtypeStruct((T, D), F32)],
        scratch_shapes=[pltpu.VMEM((tm, D), F32)], args=(a, a, cw, cb, wd, h), rider=rider)


def _ffn_backward(dh, c, a, cw, wd, seq, *, name, rider=None):
    _, T, F = a.shape
    D = dh.shape[1]
    tm = _tile(seq, TOKEN_TILE)
    hb = tm // 16
    nt = T // tm

    def body(dh_ref, dhn_ref, cu_ref, cg_ref, cun_ref, cgn_ref, au_ref, ag_ref, cw_ref, wd_ref, da_ref, st_ref):
        i, j = pl.program_id(0), pl.program_id(1)
        keep_next = (((i + 1) * tm) % seq != 0).astype(F32)

        @pl.when((i == 0) & (j == 0))
        def _():
            st_ref[...] = jnp.zeros_like(st_ref)

        dhe = jnp.concatenate([dh_ref[...], dhn_ref[...] * keep_next], axis=0).astype(BF16)
        dact = _dot_nt(dhe, wd_ref[0])
        up = jnp.concatenate([cu_ref[0].astype(F32), cun_ref[0, 0:8].astype(F32)], axis=0)
        gate = jnp.concatenate([cg_ref[0].astype(F32), cgn_ref[0, 0:8].astype(F32)], axis=0)
        sg = _sigmoid(gate)
        gs = gate * sg
        d_up = dact * gs
        d_gate = dact * up * (sg + gs * (1.0 - sg))

        def finish(b, a_ref, dc):
            w = cw_ref[b]
            dm, u1, u2 = dc[:tm], _shift_rows(dc, -1)[:tm], _shift_rows(dc, -2)[:tm]
            da_ref[b] = (w[2:3] * dm + w[1:2] * u1 + w[0:1] * u2).astype(BF16)
            av = a_ref[0].astype(F32)
            st_ref[b, 0:1, :] += jnp.sum(u2 * av, axis=0, keepdims=True)
            st_ref[b, 1:2, :] += jnp.sum(u1 * av, axis=0, keepdims=True)
            st_ref[b, 2:3, :] += jnp.sum(dm * av, axis=0, keepdims=True)
            st_ref[b, 3:4, :] += jnp.sum(dm, axis=0, keepdims=True)

        finish(j, au_ref, d_up)
        finish(j + 4, ag_ref, d_gate)

    nxt = lambda i: jnp.minimum((i + 1) * hb, T // 16 - 1)
    return _call(
        body, name=name, grid=(nt, 4),
        in_specs=[pl.BlockSpec((tm, D), lambda i, j: (i, 0)),
                  pl.BlockSpec((8, D), lambda i, j: (jnp.minimum((i + 1) * (tm // 8), T // 8 - 1), 0)),
                  pl.BlockSpec((1, tm, F), lambda i, j: (j, i, 0)), pl.BlockSpec((1, tm, F), lambda i, j: (j + 4, i, 0)),
                  pl.BlockSpec((1, 16, F), lambda i, j: (j, nxt(i), 0)),
                  pl.BlockSpec((1, 16, F), lambda i, j: (j + 4, nxt(i), 0)),
                  pl.BlockSpec((1, tm, F), lambda i, j: (j, i, 0)), pl.BlockSpec((1, tm, F), lambda i, j: (j + 4, i, 0)),
                  pl.BlockSpec((8, 3, F), lambda i, j: (0, 0, 0)),
                  pl.BlockSpec((1, F, D), lambda i, j: (j, 0, 0))],
        out_specs=[pl.BlockSpec((8, tm, F), lambda i, j: (0, i, 0)), pl.BlockSpec((8, 8, F), lambda i, j: (0, 0, 0))],
        out_shape=[jax.ShapeDtypeStruct((8, T, F), BF16), jax.ShapeDtypeStruct((8, 8, F), F32)],
        args=(dh, dh, c, c, c, c, a, a, cw, wd), rider=rider)


def _ffn_fused_forward(h, g, w_in, cw, cb, wd, seq, *, name, rider=None):
    T, D = h.shape
    F = w_in.shape[2]
    tm = _tile(seq, TOKEN_TILE // 2)
    bps = FFN_BLOCKS_PER_STEP
    nj = 4 // bps

    def body(h_ref, g_ref, wu_ref, wg_ref, cw_ref, cb_ref, wd_ref,
             au_ref, ag_ref, cu_ref, cg_ref, act_ref, n_ref, o_ref, acc_ref, carry_ref, *work_refs):
        eu_refs, eg_refs, stage_refs = work_refs[:bps], work_refs[bps:2 * bps], work_refs[2 * bps:]
        i, j = pl.program_id(0), pl.program_id(1)
        keep = ((i * tm) % seq != 0).astype(F32)

        @pl.when((i == 0) & (j == 0))
        def _():
            carry_ref[...] = jnp.zeros_like(carry_ref)

        @pl.when(j == 0)
        def _():
            xh, _ = _rms(h_ref[...], None)
            n_ref[...] = (xh * g_ref[...]).astype(BF16)
            acc_ref[...] = h_ref[...]

        n = n_ref[...]

        def project(b, k, w_ref, a_ref, ext_ref):
            a = _dot(n, w_ref[k]).astype(BF16)
            a_ref[k] = a
            ext_ref[0:8, :] = carry_ref[b] * keep
            ext_ref[8:, :] = a.astype(F32)
            carry_ref[b] = ext_ref[tm:tm + 8, :]

        def conv(b, ext_ref, r):
            x, w = ext_ref[r:r + ROWS + 8, :], cw_ref[b]
            return (w[0:1] * _shift_rows(x, 2) + w[1:2] * _shift_rows(x, 1) + w[2:3] * x)[8:] + cb_ref[b]

        for k in range(bps):
            project(j * bps + k, k, wu_ref, au_ref, eu_refs[k])
            project(j * bps + k + 4, k, wg_ref, ag_ref, eg_refs[k])
        outs = []
        for k in range(bps):
            for r in range(0, tm, ROWS):
                up, gate = conv(j * bps + k, eu_refs[k], r), conv(j * bps + k + 4, eg_refs[k], r)
                cu_ref[k, r:r + ROWS, :] = up.astype(BF16)
                cg_ref[k, r:r + ROWS, :] = gate.astype(BF16)
                stage_refs[k][r:r + ROWS, :] = (gate * _sigmoid(gate) * up).astype(BF16)
            act = stage_refs[k][...]
            act_ref[k] = act
            outs.append(_dot(act, wd_ref[k]))
        acc_ref[...] += functools.reduce(lambda p, q: p + q, outs)

        @pl.when(j == nj - 1)
        def _():
            o_ref[...] = acc_ref[...]

    blk = pl.BlockSpec((bps, tm, F), lambda i, j: (j, i, 0))
    row = pl.BlockSpec((tm, D), lambda i, j: (i, 0))
    half = jax.ShapeDtypeStruct((4, T, F), BF16)
    work = [pltpu.VMEM((tm + 8, F), F32)] * (2 * bps) + [pltpu.VMEM((tm, F), BF16)] * bps
    return _call(
        body, name=name, grid=(T // tm, nj),
        in_specs=[row, pl.BlockSpec((1, D), lambda i, j: (0, 0)),
                  pl.BlockSpec((bps, D, F), lambda i, j: (j, 0, 0)), pl.BlockSpec((bps, D, F), lambda i, j: (j + nj, 0, 0)),
                  pl.BlockSpec((8, 3, F), lambda i, j: (0, 0, 0)), pl.BlockSpec((8, 1, F), lambda i, j: (0, 0, 0)),
                  pl.BlockSpec((bps, F, D), lambda i, j: (j, 0, 0))],
        out_specs=[blk, blk, blk, blk, blk, row, row],
        out_shape=[half, half, half, half, half, jax.ShapeDtypeStruct((T, D), BF16), jax.ShapeDtypeStruct((T, D), F32)],
        scratch_shapes=[pltpu.VMEM((tm, D), F32), pltpu.VMEM((8, 8, F), F32)] + work,
        args=(h, g, w_in, w_in, cw, cb, wd), rider=rider)


def _ffn_fused_backward(dh, cu, cg, au, ag, cw, wd, w_in, h, g, seq, *, name, rider=None):
    T, D = dh.shape
    F = wd.shape[1]
    tm = _tile(seq, TOKEN_TILE // 2)
    hb = tm // 16
    nt = T // tm
    bps = FFN_BLOCKS_PER_STEP
    nj = 4 // bps

    def body(dh_ref, dhn_ref, cu_ref, cg_ref, cun_ref, cgn_ref, au_ref, ag_ref, cw_ref, wd_ref, wu_ref, wg_ref,
             h_ref, g_ref, da_ref, st_ref, o_ref, dg_ref, acc_ref, dact_ref, du_ref, dgt_ref):
        i, j = pl.program_id(0), pl.program_id(1)
        keep_next = (((i + 1) * tm) % seq != 0).astype(F32)

        @pl.when((i == 0) & (j == 0))
        def _():
            st_ref[...] = jnp.zeros_like(st_ref)
            dg_ref[...] = jnp.zeros_like(dg_ref)

        @pl.when(j == 0)
        def _():
            acc_ref[...] = jnp.zeros_like(acc_ref)

        dhe = jnp.concatenate([dh_ref[...], dhn_ref[...] * keep_next], axis=0).astype(BF16)
        for k in range(bps):
            dact_ref[k] = _dot_nt(dhe, wd_ref[k])

        def conv_grads(k):
            for r in range(0, tm + 8, ROWS):
                if r < tm:
                    rows = slice(r, r + ROWS)
                    up, gate = cu_ref[k, rows, :].astype(F32), cg_ref[k, rows, :].astype(F32)
                else:
                    rows = slice(tm, tm + 8)
                    up, gate = cun_ref[k, 0:8, :].astype(F32), cgn_ref[k, 0:8, :].astype(F32)
                dact = dact_ref[k, rows, :]
                sg = _sigmoid(gate)
                gs = gate * sg
                du_ref[k, rows, :] = dact * gs
                dgt_ref[k, rows, :] = dact * up * (sg + gs * (1.0 - sg))

        def finish(b, k, a_ref, w_ref, dc_ref):
            w = cw_ref[b]
            sums = [jnp.zeros((8, F), F32) for _ in range(4)]
            fold = lambda t: jnp.sum(t.reshape(ROWS // 8, 8, F), axis=0)
            for r in range(0, tm, ROWS):
                dc = dc_ref[k, r:r + ROWS + 8, :]
                dm, u1, u2 = dc[:ROWS], _shift_rows(dc, -1)[:ROWS], _shift_rows(dc, -2)[:ROWS]
                da_ref[b, r:r + ROWS, :] = (w[2:3] * dm + w[1:2] * u1 + w[0:1] * u2).astype(BF16)
                av = a_ref[k, r:r + ROWS, :].astype(F32)
                for s, t in enumerate((u2 * av, u1 * av, dm * av, dm)):
                    sums[s] = sums[s] + fold(t)
            for s in range(4):
                st_ref[b, s:s + 1, :] += jnp.sum(sums[s], axis=0, keepdims=True)
            return _dot_nt(da_ref[b], w_ref[k])

        dn_parts = []
        for k in range(bps):
            conv_grads(k)
            dn_parts.append(finish(j * bps + k, k, au_ref, wu_ref, du_ref))
            dn_parts.append(finish(j * bps + k + 4, k, ag_ref, wg_ref, dgt_ref))
        acc_ref[...] += functools.reduce(lambda p, q: p + q, dn_parts)

        @pl.when(j == nj - 1)
        def _():
            acc = acc_ref[...]
            xh, r = _rms(h_ref[...], None)
            dg_ref[0:1, :] += jnp.sum(acc * xh, axis=0, keepdims=True)
            dn = acc * g_ref[...]
            o_ref[...] = dh_ref[...] + r * (dn - xh * jnp.mean(dn * xh, axis=-1, keepdims=True))

    nxt = lambda i: jnp.minimum((i + 1) * hb, T // 16 - 1)
    blk = pl.BlockSpec((bps, tm, F), lambda i, j: (j, i, 0))
    halo = pl.BlockSpec((bps, 16, F), lambda i, j: (j, nxt(i), 0))
    row = pl.BlockSpec((tm, D), lambda i, j: (i, 0))
    work = pltpu.VMEM((bps, tm + 8, F), F32)
    return _call(
        body, name=name, grid=(nt, nj),
        in_specs=[row, pl.BlockSpec((8, D), lambda i, j: (jnp.minimum((i + 1) * (tm // 8), T // 8 - 1), 0)),
                  blk, blk, halo, halo, blk, blk,
                  pl.BlockSpec((8, 3, F), lambda i, j: (0, 0, 0)), pl.BlockSpec((bps, F, D), lambda i, j: (j, 0, 0)),
                  pl.BlockSpec((bps, D, F), lambda i, j: (j, 0, 0)),
                  pl.BlockSpec((bps, D, F), lambda i, j: (j + nj, 0, 0)),
                  row, pl.BlockSpec((1, D), lambda i, j: (0, 0))],
        out_specs=[pl.BlockSpec((8, tm, F), lambda i, j: (0, i, 0)), pl.BlockSpec((8, 8, F), lambda i, j: (0, 0, 0)),
                   row, pl.BlockSpec((8, D), lambda i, j: (0, 0))],
        out_shape=[jax.ShapeDtypeStruct((8, T, F), BF16), jax.ShapeDtypeStruct((8, 8, F), F32),
                   jax.ShapeDtypeStruct((T, D), F32), jax.ShapeDtypeStruct((8, D), F32)],
        scratch_shapes=[pltpu.VMEM((tm, D), F32), work, work, work],
        args=(dh, dh, cu, cg, cu, cg, au, ag, cw, wd, w_in, w_in, h, g), rider=rider)


def _rel_onehot():
    r = lax.broadcasted_iota(jnp.int32, (REL_PAD, SKEW), 0)
    n = lax.broadcasted_iota(jnp.int32, (REL_PAD, SKEW), 1)
    off = jnp.where(n >= WIN, n - SKEW, n)
    idx = jnp.minimum(PAD - off, REL_CLIP) + REL_CLIP
    return (r == idx).astype(BF16)


def _skew(x, sign):
    row = lax.broadcasted_iota(jnp.int32, x.shape, 0)
    for b in range(7):
        x = jnp.where((row >> b) & 1 == 1, pltpu.roll(x, (sign * (1 << b)) % SKEW, axis=1), x)
    return x


def _bias_build(rel, name, rider=None):
    H = rel.shape[0]

    def body(rel_ref, o_ref):
        oh = _rel_onehot()
        hi, mid, lo = _split3(rel_ref[...])
        base = _dot(hi, oh) + _dot(mid, oh) + _dot(lo, oh)
        mine = lax.broadcasted_iota(jnp.int32, (H, 1), 0) == pl.program_id(0)
        row = jnp.sum(jnp.where(mine, base, 0.0), axis=0, keepdims=True)
        q = lax.broadcasted_iota(jnp.int32, (Q_TILE, WIN), 0)
        k = lax.broadcasted_iota(jnp.int32, (Q_TILE, WIN), 1)
        ok = ((q < CHUNK) & (k < WIN - CHUNK)) | ((q >= CHUNK) & (k >= CHUNK))
        t = _skew(jnp.broadcast_to(row, (Q_TILE, SKEW)), 1)
        o_ref[0] = jnp.where(ok, t[:, :WIN], NEG_INF)

    return _call(
        body, name=name, grid=(H,), in_specs=[pl.BlockSpec((H, REL_PAD), lambda h: (0, 0))],
        out_specs=pl.BlockSpec((1, Q_TILE, WIN), lambda h: (h, 0, 0)),
        out_shape=jax.ShapeDtypeStruct((H, Q_TILE, WIN), F32), args=(rel,), rider=rider)


def _bias_reduce(dbias, name):
    H = dbias.shape[0]

    def body(d_ref, o_ref, e_ref):
        oh = _rel_onehot()
        for hd in range(H):
            x = jnp.concatenate([d_ref[hd], jnp.zeros((Q_TILE, SKEW - WIN), F32)], axis=1)
            e_ref[hd:hd + 1, :] = jnp.sum(_skew(x, -1), axis=0, keepdims=True)
        hi, mid, lo = _split3(e_ref[...])
        o_ref[...] = _dot_nt(hi, oh) + _dot_nt(mid, oh) + _dot_nt(lo, oh)

    return pl.pallas_call(
        body, name=name, out_shape=jax.ShapeDtypeStruct((H, REL_PAD), F32),
        in_specs=[pl.BlockSpec(memory_space=pltpu.VMEM)], out_specs=pl.BlockSpec(memory_space=pltpu.VMEM),
        scratch_shapes=[pltpu.VMEM((H, SKEW), F32)],
        compiler_params=_params(),
    )(dbias)


def _pair_stack(xp, even):
    z = jnp.zeros_like(xp)
    return jnp.concatenate([jnp.where(even, xp, z), jnp.where(even, z, xp)], axis=0)


def _pair_merge(y, even):
    return jnp.where(even, y[:Q_TILE], y[Q_TILE:])


def _strip_probs(s_ref, b_ref, pp, r, valid, base=0):
    hb, hr = divmod(r, Q_TILE)
    s = s_ref[base + pp, r:r + STRIP, :] + b_ref[2 * pp + hb, hr:hr + STRIP, :]
    s = jnp.where(valid, s, NEG_INF)
    e = jnp.exp(s - jnp.max(s, axis=-1, keepdims=True))
    return e * (1.0 / jnp.sum(e, axis=-1, keepdims=True))


def _fill_padded(dst_ref, src_ref):
    dst_ref[0:PAD, :] = jnp.zeros((PAD, dst_ref.shape[1]), dst_ref.dtype)
    dst_ref[PAD:, :] = src_ref[...]


def _attn_specs(B, S, D, lanes):
    nt = S // (TILES_PER_STEP * Q_TILE)
    q_spec = pl.BlockSpec((TILES_PER_STEP * Q_TILE, lanes), lambda g, b, i: (b * nt + i, g))
    k_spec = pl.BlockSpec((S, lanes), lambda g, b, i: (b, g))
    v_spec = pl.BlockSpec((S, lanes), lambda g, b, i: (b, D // lanes + g))
    bias_spec = pl.BlockSpec((lanes // HEAD_DIM, Q_TILE, WIN), lambda g, b, i: (g, 0, 0))
    return nt, q_spec, k_spec, v_spec, bias_spec


def _attn_forward(q, kv, bias, S, *, name, rider=None):
    T, D = q.shape
    B = T // S
    lanes = min(FWD_HEADS_PER_STEP * HEAD_DIM, D)
    nt, q_spec, k_spec, v_spec, bias_spec = _attn_specs(B, S, D, lanes)

    npairs = lanes // (2 * HEAD_DIM)

    def body(q_ref, k_ref, v_ref, b_ref, o_ref, kp_ref, vp_ref, s_ref, p_ref):
        i = pl.program_id(2)

        @pl.when(i == 0)
        def _():
            _fill_padded(kp_ref, k_ref)
            _fill_padded(vp_ref, v_ref)

        even = lax.broadcasted_iota(jnp.int32, (1, 2 * HEAD_DIM), 1) < HEAD_DIM
        pair_cols = [slice(pp * 2 * HEAD_DIM, (pp + 1) * 2 * HEAD_DIM) for pp in range(npairs)]
        for t in range(TILES_PER_STEP):
            tile = i * TILES_PER_STEP + t
            start = pl.multiple_of(tile * Q_TILE, Q_TILE)
            rows = slice(t * Q_TILE, (t + 1) * Q_TILE)
            valid = lax.broadcasted_iota(jnp.int32, (STRIP, WIN), 1) >= PAD - tile * Q_TILE
            for pp, cols in enumerate(pair_cols):
                s_ref[t * npairs + pp] = _dot_nt(_pair_stack(q_ref[rows, cols], even), kp_ref[pl.ds(start, WIN), cols])
            for pp in range(npairs):
                for r in range(0, 2 * Q_TILE, STRIP):
                    p = _strip_probs(s_ref, b_ref, pp, r, valid, base=t * npairs)
                    p_ref[t * npairs + pp, r:r + STRIP, :] = p.astype(BF16)
            for pp, cols in enumerate(pair_cols):
                o = _dot(p_ref[t * npairs + pp], vp_ref[pl.ds(start, WIN), cols])
                o_ref[rows, cols] = _pair_merge(o, even).astype(BF16)

    nbuf = TILES_PER_STEP * npairs
    return _call(
        body, name=name, grid=(D // lanes, B, nt),
        in_specs=[q_spec, k_spec, v_spec, bias_spec], out_specs=q_spec,
        out_shape=jax.ShapeDtypeStruct((T, D), BF16),
        scratch_shapes=[pltpu.VMEM((S + PAD, lanes), BF16), pltpu.VMEM((S + PAD, lanes), BF16),
                        pltpu.VMEM((nbuf, 2 * Q_TILE, WIN), F32), pltpu.VMEM((nbuf, 2 * Q_TILE, WIN), BF16)],
        args=(q, kv, kv, bias), rider=rider)


def _attn_backward(q, kv, bias, do, S, *, name, rider=None):
    T, D = q.shape
    B = T // S
    H = D // HEAD_DIM
    lanes = min(BWD_HEADS_PER_STEP * HEAD_DIM, D)
    nt, q_spec, k_spec, v_spec, bias_spec = _attn_specs(B, S, D, lanes)
    scale = HEAD_DIM ** -0.5

    npairs = lanes // (2 * HEAD_DIM)

    def body(q_ref, k_ref, v_ref, b_ref, do_ref, dq_ref, dk_ref, dv_ref, db_ref, kp_ref, vp_ref, dka_ref, dva_ref,
             s_ref, dp_ref, p_ref, ds_ref):
        b, i = pl.program_id(1), pl.program_id(2)

        @pl.when((b == 0) & (i == 0))
        def _():
            db_ref[...] = jnp.zeros_like(db_ref)

        @pl.when(i == 0)
        def _():
            _fill_padded(kp_ref, k_ref)
            _fill_padded(vp_ref, v_ref)
            dka_ref[...] = jnp.zeros_like(dka_ref)
            dva_ref[...] = jnp.zeros_like(dva_ref)

        even = lax.broadcasted_iota(jnp.int32, (1, 2 * HEAD_DIM), 1) < HEAD_DIM
        pair_cols = [slice(pp * 2 * HEAD_DIM, (pp + 1) * 2 * HEAD_DIM) for pp in range(npairs)]
        for t in range(TILES_PER_STEP):
            tile = i * TILES_PER_STEP + t
            start = pl.multiple_of(tile * Q_TILE, Q_TILE)
            rows = slice(t * Q_TILE, (t + 1) * Q_TILE)
            valid = lax.broadcasted_iota(jnp.int32, (STRIP, WIN), 1) >= PAD - tile * Q_TILE
            base = t * npairs
            for pp, cols in enumerate(pair_cols):
                s_ref[base + pp] = _dot_nt(_pair_stack(q_ref[rows, cols], even), kp_ref[pl.ds(start, WIN), cols])
                dp_ref[base + pp] = _dot_nt(_pair_stack(do_ref[rows, cols], even), vp_ref[pl.ds(start, WIN), cols])
            for pp in range(npairs):
                for r in range(0, 2 * Q_TILE, STRIP):
                    hb, hr = divmod(r, Q_TILE)
                    p = _strip_probs(s_ref, b_ref, pp, r, valid, base=base)
                    dp = dp_ref[base + pp, r:r + STRIP, :]
                    ds = p * (dp - jnp.sum(p * dp, axis=-1, keepdims=True))
                    db_ref[2 * pp + hb, hr:hr + STRIP, :] += ds
                    p_ref[base + pp, r:r + STRIP, :] = p.astype(BF16)
                    ds_ref[base + pp, r:r + STRIP, :] = ds.astype(BF16)
            for pp, cols in enumerate(pair_cols):
                dsb = ds_ref[base + pp]
                dq = _pair_merge(_dot(dsb, kp_ref[pl.ds(start, WIN), cols]), even) * scale
                dq_ref[rows, cols] = dq.astype(BF16)
                dka_ref[pl.ds(start, WIN), cols] += _dot_tn(dsb, _pair_stack(q_ref[rows, cols], even))
                dva_ref[pl.ds(start, WIN), cols] += _dot_tn(p_ref[base + pp], _pair_stack(do_ref[rows, cols], even))

        @pl.when(i == nt - 1)
        def _():
            dk_ref[...] = dka_ref[PAD:, :].astype(BF16)
            dv_ref[...] = dva_ref[PAD:, :].astype(BF16)

    dkv_shape = jax.ShapeDtypeStruct((T, D), BF16)
    nbuf = TILES_PER_STEP * npairs
    return _call(
        body, name=name, grid=(D // lanes, B, nt),
        in_specs=[q_spec, k_spec, v_spec, bias_spec, q_spec],
        out_specs=[q_spec, k_spec, k_spec, bias_spec],
        out_shape=[jax.ShapeDtypeStruct((T, D), BF16), dkv_shape, dkv_shape,
                   jax.ShapeDtypeStruct((H, Q_TILE, WIN), F32)],
        scratch_shapes=[pltpu.VMEM((S + PAD, lanes), BF16), pltpu.VMEM((S + PAD, lanes), BF16),
                        pltpu.VMEM((S + PAD, lanes), F32), pltpu.VMEM((S + PAD, lanes), F32),
                        pltpu.VMEM((nbuf, 2 * Q_TILE, WIN), F32), pltpu.VMEM((nbuf, 2 * Q_TILE, WIN), F32),
                        pltpu.VMEM((nbuf, 2 * Q_TILE, WIN), BF16), pltpu.VMEM((nbuf, 2 * Q_TILE, WIN), BF16)],
        args=(q, kv, kv, bias, do), rider=rider)


def _loss_head(h, g, target, name):
    T, D = h.shape
    tm = _tile(T, MATMUL_TILE)

    def body(h_ref, g_ref, t_ref, dh_ref, st_ref):
        @pl.when(pl.program_id(0) == 0)
        def _():
            st_ref[...] = jnp.zeros_like(st_ref)

        xh, r = _rms(h_ref[...], None)
        err = xh * g_ref[...] - t_ref[...]
        st_ref[1:2, :] += 0.5 * jnp.sum(jnp.mean(err * err, axis=-1, keepdims=True), axis=0, keepdims=True)
        dy = err * (1.0 / D)
        st_ref[0:1, :] += jnp.sum(dy * xh, axis=0, keepdims=True)
        dn = dy * g_ref[...]
        dh_ref[...] = r * (dn - xh * jnp.mean(dn * xh, axis=-1, keepdims=True))

    row = pl.BlockSpec((tm, D), lambda i: (i, 0))
    return pl.pallas_call(
        body, name=name, grid=(T // tm,),
        in_specs=[row, pl.BlockSpec((1, D), lambda i: (0, 0)), row],
        out_specs=[row, pl.BlockSpec((8, D), lambda i: (0, 0))],
        out_shape=[jax.ShapeDtypeStruct((T, D), F32), jax.ShapeDtypeStruct((8, D), F32)],
        compiler_params=_params(("arbitrary",)),
    )(h, g, target)


def _sum_devices(arrs, name):
    n = len(arrs)

    def body(*refs):
        for a in range(n):
            s = refs[a][0].astype(F32)
            for k in range(1, N_DEV):
                s = s + refs[a][k].astype(F32)
            refs[n + a][...] = s

    vm = pl.BlockSpec(memory_space=pltpu.VMEM)
    return pl.pallas_call(
        body, name=name, out_shape=[jax.ShapeDtypeStruct(a.shape[1:], F32) for a in arrs],
        in_specs=[vm] * n, out_specs=[vm] * n, compiler_params=_params(),
    )(*arrs)


def _adamw_math(w, g, m, v):
    m = ADAM_B1 * m + (1.0 - ADAM_B1) * g
    v = ADAM_B2 * v + (1.0 - ADAM_B2) * (g * g)
    m_hat = m / (1.0 - ADAM_B1 ** ADAM_STEP)
    v_hat = v / (1.0 - ADAM_B2 ** ADAM_STEP)
    delta = -ADAM_LR * (m_hat / (jnp.sqrt(v_hat) + ADAM_EPS) + ADAM_WD * w)
    return delta, m, v


def _adamw_small(items, name):
    n = len(items)

    def body(*refs):
        for a in range(n):
            w, m, v, g = (refs[4 * a + k][...] for k in range(4))
            d, m, v = _adamw_math(w, g, m, v)
            refs[4 * n + 3 * a][...] = d
            refs[4 * n + 3 * a + 1][...] = m
            refs[4 * n + 3 * a + 2][...] = v

    vm = pl.BlockSpec(memory_space=pltpu.VMEM)
    flat = [t for it in items for t in it]
    outs = pl.pallas_call(
        body, name=name,
        out_shape=[jax.ShapeDtypeStruct(it[0].shape, F32) for it in items for _ in range(3)],
        in_specs=[vm] * (4 * n), out_specs=[vm] * (3 * n), compiler_params=_params(),
    )(*flat)
    return [tuple(outs[3 * a:3 * a + 3]) for a in range(n)]


def _adamw_big(w, m, v, owns, landeds, name, rider=None):
    L, R, C = w.shape
    tr = _tile(R, 512)
    nr = R // tr
    counts = [len(ls) for ls in landeds]

    def body(*refs):
        w_ref, m_ref, v_ref = refs[:3]
        g_ref, d_ref, mo_ref, vo_ref = refs[-4:]
        layer = pl.program_id(0)
        at = 3
        for j in range(L):
            own_ref, l_refs = refs[at], refs[at + 1:at + 1 + counts[j]]
            at += 1 + counts[j]

            @pl.when(layer == j)
            def _(own_ref=own_ref, l_refs=l_refs):
                g = own_ref[...]
                for l_ref in l_refs:
                    for k in range(l_ref.shape[0]):
                        g = g + l_ref[k].astype(F32)
                d, mn, vn = _adamw_math(w_ref[0], g, m_ref[0], v_ref[0])
                g_ref[0] = g
                d_ref[0] = d
                mo_ref[0] = mn
                vo_ref[0] = vn

    def pinned(j):
        return lambda l, i: jnp.where(l == j, i, jnp.where(l < j, 0, nr - 1))

    row = pl.BlockSpec((1, tr, C), lambda l, i: (l, i, 0))
    in_specs, args = [row, row, row], [w, m, v]
    for j in range(L):
        in_specs.append(pl.BlockSpec((tr, C), lambda l, i, p=pinned(j): (p(l, i), 0)))
        args.append(owns[j])
        for arr in landeds[j]:
            in_specs.append(pl.BlockSpec((arr.shape[0], tr, C), lambda l, i, p=pinned(j): (0, p(l, i), 0)))
            args.append(arr)
    return _call(body, name=name, grid=(L, nr), in_specs=in_specs, out_specs=[row] * 4,
                 out_shape=[jax.ShapeDtypeStruct((L, R, C), F32)] * 4, args=args, rider=rider)


def kernel(x, a_norm_g, a_w_in, a_v_norm_g, a_w_s, a_b_s, a_w_out, kv_norm_g, w_kv, b_norm_g, b_w_q, b_rel_bias, b_w_o, f_norm_g, f_w_in, f_conv_w, f_conv_b, f_w_down, final_norm_g, loss_target, m_a_norm_g, m_a_w_in, m_a_v_norm_g, m_a_w_s, m_a_b_s, m_a_w_out, m_kv_norm_g, m_w_kv, m_b_norm_g, m_b_w_q, m_b_rel_bias, m_b_w_o, m_f_norm_g, m_f_w_in, m_f_conv_w, m_f_conv_b, m_f_w_down, m_final_norm_g, v_a_norm_g, v_a_w_in, v_a_v_norm_g, v_a_w_s, v_a_b_s, v_a_w_out, v_kv_norm_g, v_w_kv, v_b_norm_g, v_b_w_q, v_b_rel_bias, v_b_w_o, v_f_norm_g, v_f_w_in, v_f_conv_w, v_f_conv_b, v_f_w_down, v_final_norm_g):
    B, S, D = x.shape
    T = B * S
    G = a_w_s.shape[1]
    H = D // HEAD_DIM
    F = f_w_in.shape[2]
    L = f_w_in.shape[0]
    dn = D // N_DEV
    xi, yi, ci = lax.axis_index("x"), lax.axis_index("y"), lax.axis_index("c")
    me = 4 * xi + 2 * yi + ci
    pos = jnp.stack([ci, 2 * xi + yi]).astype(jnp.int32)

    cast = lambda t: t.astype(BF16)
    gather = lambda *ts: _gather_rider(list(ts))
    rel = jnp.pad(b_rel_bias[0], ((0, 0), (0, REL_PAD - b_rel_bias.shape[2])))
    bias, (wa_in, norms_sh, conv_w0, conv_w1) = _bias_build(rel, "bias_build", rider=gather(
        cast(a_w_in[0]), jnp.concatenate([a_norm_g, a_v_norm_g], axis=0), f_conv_w[0], f_conv_w[1]))
    ga = jnp.transpose(norms_sh, (1, 0, 2)).reshape(2, D)
    g_a, g_av = ga[0:1], ga[1:2]

    x2 = x.reshape(T, D)
    tgt = loss_target.reshape(T, D)
    pc = jnp.arange(GMLP_BLOCK) // CHUNK
    mask = (pc[:, None] >= pc[None, :]).astype(F32)
    ws = (a_w_s[0] * mask[None]).astype(BF16)
    bst = jnp.transpose(a_b_s[0])
    four = lambda t: t.reshape((4, 2) + t.shape[1:])

    w_in0_sh = cast(f_w_in[0])
    (z, n_a), (wa_out, w_in0_top) = _norm_matmul(x2, g_a, wa_in, flat=True, nbk=4, name="gmlp_in",
                                                 rider=gather(cast(a_w_out[0]), w_in0_sh[:D // 2]))
    wa_out = wa_out.reshape(D, D)
    (gated, h1), (w_in0_bottom, wf_down0) = _gmlp_forward(z, ws, bst, g_av, wa_out, x2, name="gmlp_mix",
                                                          rider=gather(w_in0_sh[D // 2:], cast(f_w_down[0])))
    w_in0 = jnp.concatenate([w_in0_top, w_in0_bottom], axis=1)
    cw0, cb0, wd0 = conv_w0, f_conv_b[0].reshape(8, 1, F), wf_down0.reshape(4, F, D)
    (au0, ag0, cu0, cg0, act0, n_f0, h2), (wkv, wq, w_in1) = _ffn_fused_forward(
        h1, f_norm_g[0:1], w_in0, cw0, cb0, wd0, S, name="ffn0_fwd",
        rider=gather(cast(w_kv), cast(b_w_q[0]), cast(f_w_in[1])))
    wq = wq.reshape(D, D)
    kv, n_kv = _norm_matmul(h2, kv_norm_g.reshape(1, D), wkv, flat=True, nbk=4, name="kv_proj")
    q, n_q = _norm_matmul(h2, b_norm_g, wq.reshape(1, D, D), flat=True, nbk=1, name="q_proj", scale=HEAD_DIM ** -0.5)
    o, (wo, wf_down1) = _attn_forward(q, kv, bias, S, name="attn", rider=gather(cast(b_w_o[0]), cast(f_w_down[1])))
    wo = wo.reshape(D, D)
    cw1, cb1, wd1 = conv_w1, f_conv_b[1].reshape(8, 1, F), wf_down1.reshape(4, F, D)
    h3 = _matmul_residual(o, wo, h2, "attn_out")
    au1, ag1, cu1, cg1, act1, n_f1, h4 = _ffn_fused_forward(h3, f_norm_g[1:2], w_in1, cw1, cb1, wd1, S, name="ffn1_fwd")

    sums, from_chips = {}, {}

    def sibling_sums(names, parts, landed):
        for nm, p, l in zip(names, parts, landed):
            sums[nm] = _sibling_sum(p, l, pos, "grad_sibling_sum_" + nm)

    def chip_rider(*names):
        return _chip_rider([sums[nm][1] for nm in names])

    dh4, st_final = _loss_head(h4, final_norm_g.reshape(1, D), tgt, "loss_head")
    g_wd1 = _wgrad_rows(act1, dh4, flat=False, tk=F, name="ffn1_dwdown")
    parts = [four(g_wd1.reshape(8, F // 2, D))]
    (da1, st_conv1, dh3, st_f1), landed = _ffn_fused_backward(
        dh4, cu1, cg1, au1, ag1, cw1, wd1, w_in1, h3, f_norm_g[1:2], S, name="ffn1_bwd", rider=_sibling_rider(parts))
    sibling_sums(["wd1"], parts, landed)
    g_win1, (from_chips["wd1"],) = _wgrad_cols(n_f1, da1, flat=False, nb=8, nbk=2, name="ffn1_dwin",
                                               rider=chip_rider("wd1"))
    d_o = _matmul_nt(dh3, wo.reshape(1, D, D), flat=True, nbk=1, name="attn_out_dx")
    parts = [four(g_win1)]
    g_wo, landed = _wgrad_rows(o, dh3, flat=True, tk=_tile(D, 512), name="attn_out_dw", rider=_sibling_rider(parts))
    sibling_sums(["win1"], parts, landed)
    (dq, dk, dv, dbias), (from_chips["win1"],) = _attn_backward(
        q, kv, bias, d_o, S, name="attn_bwd", rider=chip_rider("win1"))
    g_rel = _bias_reduce(dbias, "bias_reduce")
    g_wq = _wgrad_cols(n_q, dq, flat=True, nb=1, nbk=1, name="q_dw")
    dh2, st_b = _matmul_nt(dq, wq.reshape(1, D, D), flat=True, nbk=1, name="q_dx", norm=(h2, b_norm_g, dh3))
    dkv = jnp.concatenate([dk, dv], axis=-1)
    g_wkv = _wgrad_cols(n_kv, dkv, flat=True, nb=8, nbk=4, name="kv_dw")
    parts = [four(g_wo.reshape(8, dn, D)), four(g_wq.reshape(8, dn, D)), four(g_wkv)]
    (dh2, st_kv), landed = _matmul_nt(dkv, wkv, flat=True, nbk=4, name="kv_dx",
                                      norm=(h2, kv_norm_g.reshape(1, D), dh2), rider=_sibling_rider(parts))
    sibling_sums(["wo", "wq", "wkv"], parts, landed)
    g_wd0, (from_chips["wo"], from_chips["wq"], from_chips["wkv"]) = _wgrad_rows(
        act0, dh2, flat=False, tk=F, name="ffn0_dwdown", rider=chip_rider("wo", "wq", "wkv"))
    parts = [four(g_wd0.reshape(8, F // 2, D))]
    (da0, st_conv0, dh1, st_f0), landed = _ffn_fused_backward(
        dh2, cu0, cg0, au0, ag0, cw0, wd0, w_in0, h1, f_norm_g[0:1], S, name="ffn0_bwd", rider=_sibling_rider(parts))
    sibling_sums(["wd0"], parts, landed)
    g_win0, (ce,) = _wgrad_cols(n_f0, da0, flat=False, nb=8, nbk=2, name="ffn0_dwin", rider=chip_rider("wd0"))
    from_chips["wd0"] = [ce]
    dgated = _matmul_nt(dh1, wa_out.reshape(1, D, D), flat=True, nbk=1, name="gmlp_out_dx")
    parts = [four(g_win0)]
    g_wa_out, landed = _wgrad_rows(gated, dh1, flat=True, tk=_tile(D, 512), name="gmlp_out_dw",
                                   rider=_sibling_rider(parts))
    sibling_sums(["win0"], parts, landed)
    parts = [four(g_wa_out.reshape(8, dn, D))]
    (dz, g_ws, g_bst, st_av), (ce_win0_a, landed) = _gmlp_backward(
        z, dgated, ws, bst, g_av, mask, name="gmlp_bwd",
        rider=_join_riders([_chip_rider([sums["win0"][1]], ks=(1, 2)), _sibling_rider(parts)]))
    sibling_sums(["wa_out"], parts, [landed])
    g_wa_in, (ce_win0_b, ce) = _wgrad_cols(n_a, dz, flat=True, nb=8, nbk=4, name="gmlp_in_dw", rider=_join_riders(
        [_chip_rider([sums["win0"][1]], ks=(3,)), chip_rider("wa_out")]))
    from_chips["win0"], from_chips["wa_out"] = [ce_win0_a, ce_win0_b], [ce]
    vec = jnp.concatenate([st_av[0:1], st_kv[0:1], st_b[0:1], st_f0[0:1], st_f1[0:1], st_final[0:3]], axis=0)
    parts = [four(g_wa_in)]
    (grad_x, st_a), got = _matmul_nt(dz, wa_in, flat=True, nbk=4, name="gmlp_in_dx", norm=(x2, g_a, dh1),
                                     rider=_join_riders([_sibling_rider(parts), gather(
                                         vec, cast(g_ws), cast(g_bst), cast(g_rel), cast(st_conv0), cast(st_conv1))]))
    sibling_sums(["wa_in"], parts, got[0:1])
    small = got[1:]

    def big_update(names, w, m, v, rider=None):
        shape = w.shape
        r = lambda t: t.reshape((len(names), -1, shape[-1]))
        as_list = lambda t: t if isinstance(t, list) else [t]
        outs = _adamw_big(r(w), r(m), r(v), [sums[nm][0] for nm in names],
                          [as_list(from_chips[nm]) for nm in names], "adamw_" + names[0], rider=rider)
        outs, got = (outs, None) if rider is None else outs
        return [t.reshape(shape) for t in outs], got

    u_f_w_in, (ce, st_a) = big_update(["win0", "win1"], f_w_in, m_f_w_in, v_f_w_in,
                                      rider=_join_riders([chip_rider("wa_in"), gather(st_a)]))
    from_chips["wa_in"] = [ce]
    u_f_w_down, _ = big_update(["wd0", "wd1"], f_w_down, m_f_w_down, v_f_w_down)
    u_a_w_in, _ = big_update(["wa_in"], a_w_in, m_a_w_in, v_a_w_in)
    u_w_kv, _ = big_update(["wkv"], w_kv, m_w_kv, v_w_kv)
    u_a_w_out, _ = big_update(["wa_out"], a_w_out, m_a_w_out, v_a_w_out)
    u_b_w_q, _ = big_update(["wq"], b_w_q, m_b_w_q, v_b_w_q)
    u_b_w_o, _ = big_update(["wo"], b_w_o, m_b_w_o, v_b_w_o)

    vec, g_ws, g_bst, g_rel, st_conv0, st_conv1, st_a = _sum_devices(list(small) + [st_a], "sum_small_grads")
    vec = jnp.concatenate([st_a[0:1], vec[0:7]], axis=0)
    loss = vec[7, 0]
    g_a_norm = lax.dynamic_slice_in_dim(vec[0:1], me * dn, dn, axis=1)
    g_av_norm = lax.dynamic_slice_in_dim(vec[1:2], me * dn, dn, axis=1)
    st_conv = jnp.stack([st_conv0, st_conv1])
    g_conv_w = lax.dynamic_index_in_dim(st_conv, me, axis=1, keepdims=False)[:, 0:3]
    g_conv_b = st_conv[:, :, 3, :].reshape(L, 8 * F)
    small_items = [
        (a_norm_g, m_a_norm_g, v_a_norm_g, g_a_norm),
        (a_v_norm_g, m_a_v_norm_g, v_a_v_norm_g, g_av_norm),
        (a_w_s, m_a_w_s, v_a_w_s, g_ws[None]),
        (a_b_s, m_a_b_s, v_a_b_s, jnp.transpose(g_bst)[None]),
        (kv_norm_g.reshape(1, D), m_kv_norm_g.reshape(1, D), v_kv_norm_g.reshape(1, D), vec[2:3]),
        (b_norm_g, m_b_norm_g, v_b_norm_g, vec[3:4]),
        (b_rel_bias, m_b_rel_bias, v_b_rel_bias, g_rel[None, :, :b_rel_bias.shape[2]]),
        (f_norm_g, m_f_norm_g, v_f_norm_g, vec[4:6]),
        (f_conv_w, m_f_conv_w, v_f_conv_w, g_conv_w),
        (f_conv_b, m_f_conv_b, v_f_conv_b, g_conv_b),
        (final_norm_g.reshape(1, D), m_final_norm_g.reshape(1, D), v_final_norm_g.reshape(1, D), vec[6:7]),
    ]
    small_out = _adamw_small(small_items, "adamw_small")
    (u_a_norm, u_av_norm, u_ws, u_bs, u_kvn, u_bn, u_rel, u_fn, u_cw, u_cb, u_fin) = [
        (it[3],) + so for it, so in zip(small_items, small_out)]
    vecD = lambda u: tuple(t.reshape(D) for t in u)
    u_kvn, u_fin = vecD(u_kvn), vecD(u_fin)

    order = [u_a_norm, u_a_w_in, u_av_norm, u_ws, u_bs, u_a_w_out, u_kvn, u_w_kv, u_bn, u_b_w_q, u_rel, u_b_w_o,
             u_fn, u_f_w_in, u_cw, u_cb, u_f_w_down, u_fin]
    outs = [loss, grad_x.reshape(B, S, D)]
    for k in range(4):
        outs += [u[k] for u in order]
    return tuple(outs)
```

```python
import functools

import jax
import jax.numpy as jnp
from jax import lax
from jax.experimental import pallas as pl
from jax.experimental.pallas import tpu as pltpu

F32 = jnp.float32
BF16 = jnp.bfloat16
MESH = pl.DeviceIdType.MESH

N_DEV = 8
EPS = 1e-6
NEG_INF = -1e30
CHUNK = 64
LEFT_CHUNKS = 8
REL_CLIP = 128
HEAD_DIM = 64
GMLP_BLOCK = 128
Q_TILE = 2 * CHUNK
PAD = LEFT_CHUNKS * CHUNK
WIN = PAD + Q_TILE
SKEW = WIN + Q_TILE
REL_PAD = 384
FWD_HEADS_PER_STEP = 8
BWD_HEADS_PER_STEP = 4
TILES_PER_STEP = 2
STRIP = 32
ROWS = 32
ADAM_LR, ADAM_B1, ADAM_B2, ADAM_EPS, ADAM_WD, ADAM_STEP = 0.001, 0.9, 0.999, 1e-08, 0.01, 10
VMEM_LIMIT = 60 * 1024 * 1024
TOKEN_TILE = 512
MATMUL_TILE = 1024
FFN_BLOCKS_PER_STEP = 2


def _params(sem=None):
    return pltpu.CompilerParams(dimension_semantics=sem, vmem_limit_bytes=VMEM_LIMIT)


def _tile(n, pref):
    if n <= pref:
        return n
    for t in range(pref - pref % 8, 7, -8):
        if n % t == 0:
            return t
    return n


def _gelu(x):
    return 0.5 * x * (1.0 + jnp.tanh(0.7978845608028654 * (x + 0.044715 * x * x * x)))


def _gelu_grad(x):
    t = jnp.tanh(0.7978845608028654 * (x + 0.044715 * x * x * x))
    return 0.5 * (1.0 + t) + 0.5 * x * (1.0 - t * t) * 0.7978845608028654 * (1.0 + 3 * 0.044715 * x * x)


def _sigmoid(x):
    return 1.0 / (1.0 + jnp.exp(-x))


def _dot(a, b):
    return jnp.dot(a, b, preferred_element_type=F32)


def _dot_nt(a, b):
    return lax.dot_general(a, b, (((1,), (1,)), ((), ())), preferred_element_type=F32)


def _dot_tn(a, b):
    return lax.dot_general(a, b, (((0,), (0,)), ((), ())), preferred_element_type=F32)


def _split3(x):
    hi = x.astype(BF16)
    r1 = x - hi.astype(F32)
    mid = r1.astype(BF16)
    lo = (r1 - mid.astype(F32)).astype(BF16)
    return hi, mid, lo


def _mesh_pos():
    return lax.axis_index("x"), lax.axis_index("y"), lax.axis_index("c")


class _Rider:
    def __init__(self, arrs, out_shapes, sems, start, finish):
        self.arrs, self.out_shapes, self.sems, self.start, self.finish = arrs, out_shapes, sems, start, finish


def _gather_rider(arrs):
    n = len(arrs)

    def tools(ins, outs, sems):
        send_sems, recv_sems, local_sems = sems
        x, y, c = _mesh_pos()
        me, sibling = (x, y, c), (x, y, 1 - c)
        chips = [(1 - x, y), (x, 1 - y), (1 - x, 1 - y)]

        def slot(a, block):
            px, py, pc = block
            return outs[a].at[4 * px + 2 * py + pc]

        def copy(a, k, block, to, src=None):
            dst = slot(a, block)
            return pltpu.make_async_remote_copy(
                src_ref=dst if src is None else src, dst_ref=dst,
                send_sem=send_sems.at[a, k], recv_sem=recv_sems.at[a, k], device_id=to, device_id_type=MESH)

        def first(a):
            cps = [copy(a, 0, me, sibling, src=ins[a])]
            return cps + [copy(a, 1 + j, me, (*chip, c), src=ins[a]) for j, chip in enumerate(chips)]

        def mine(a):
            return pltpu.make_async_copy(ins[a], slot(a, me), local_sems.at[a])

        return me, sibling, chips, c, copy, first, mine

    def start(ins, outs, sems):
        _, _, _, _, _, first, mine = tools(ins, outs, sems)
        for a in range(n):
            mine(a).start()
            for cp in first(a):
                cp.start()

    def finish(ins, outs, sems):
        me, sibling, chips, c, copy, first, mine = tools(ins, outs, sems)
        passed = []
        for j, chip in enumerate(chips):
            for a in range(n):
                copy(a, 1 + j, (*chip, c), me).wait_recv()
                fwd = copy(a, 4 + j, (*chip, c), sibling)
                fwd.start()
                passed.append(fwd)
        for a in range(n):
            copy(a, 0, sibling, me).wait_recv()
            for j, chip in enumerate(chips):
                copy(a, 4 + j, (*chip, 1 - c), me).wait_recv()
        for a in range(n):
            for cp in first(a):
                cp.wait_send()
        for cp in passed:
            cp.wait_send()
        for a in range(n):
            mine(a).wait()

    return _Rider(list(arrs), [jax.ShapeDtypeStruct((N_DEV,) + a.shape, a.dtype) for a in arrs],
                  [pltpu.SemaphoreType.DMA((n, 7)), pltpu.SemaphoreType.DMA((n, 7)), pltpu.SemaphoreType.DMA((n,))],
                  start, finish)


def _sibling_rider(arrs):
    n = len(arrs)

    def copies(ins, outs, sems):
        x, y, c = _mesh_pos()
        return [pltpu.make_async_remote_copy(
            src_ref=ins[a].at[:, pl.ds(1 - c, 1)], dst_ref=outs[a],
            send_sem=sems[0].at[a], recv_sem=sems[1].at[a], device_id=(x, y, 1 - c), device_id_type=MESH)
            for a in range(n)]

    def start(ins, outs, sems):
        for cp in copies(ins, outs, sems):
            cp.start()

    def finish(ins, outs, sems):
        for cp in copies(ins, outs, sems):
            cp.wait()

    return _Rider(list(arrs), [jax.ShapeDtypeStruct((4, 1) + a.shape[2:], a.dtype) for a in arrs],
                  [pltpu.SemaphoreType.DMA((n,)), pltpu.SemaphoreType.DMA((n,))], start, finish)


def _chip_rider(arrs, ks=(1, 2, 3)):
    n = len(arrs)

    def copies(ins, outs, sems):
        x, y, c = _mesh_pos()
        cps = []
        for a in range(n):
            for s, k in enumerate(ks):
                px = x if k < 2 else 1 - x
                py = y if k == 2 else 1 - y
                cps.append(pltpu.make_async_remote_copy(
                    src_ref=ins[a].at[2 * px + py], dst_ref=outs[a].at[s],
                    send_sem=sems[0].at[a, s], recv_sem=sems[1].at[a, s],
                    device_id=(px, py, c), device_id_type=MESH))
        return cps

    def start(ins, outs, sems):
        for cp in copies(ins, outs, sems):
            cp.start()

    def finish(ins, outs, sems):
        for cp in copies(ins, outs, sems):
            cp.wait()

    return _Rider(list(arrs), [jax.ShapeDtypeStruct((len(ks),) + a.shape[1:], a.dtype) for a in arrs],
                  [pltpu.SemaphoreType.DMA((n, len(ks))), pltpu.SemaphoreType.DMA((n, len(ks)))], start, finish)


def _join_riders(riders):
    def split(seq, counts):
        out, at = [], 0
        for k in counts:
            out.append(seq[at:at + k])
            at += k
        return out

    n_in = [len(r.arrs) for r in riders]
    n_out = [len(r.out_shapes) for r in riders]
    n_sem = [len(r.sems) for r in riders]

    def run(which):
        def fn(ins, outs, sems):
            for r, i, o, s in zip(riders, split(ins, n_in), split(outs, n_out), split(sems, n_sem)):
                getattr(r, which)(i, o, s)
        return fn

    return _Rider([a for r in riders for a in r.arrs], [o for r in riders for o in r.out_shapes],
                  [s for r in riders for s in r.sems], run("start"), run("finish"))


def _run_rider(rider, name):
    n_in, n_out = len(rider.arrs), len(rider.out_shapes)

    def body(*refs):
        ins, outs, sems = refs[:n_in], refs[n_in:n_in + n_out], refs[n_in + n_out:]
        rider.start(ins, outs, sems)
        rider.finish(ins, outs, sems)

    any_spec = pl.BlockSpec(memory_space=pl.ANY)
    return pl.pallas_call(
        body, name=name, out_shape=list(rider.out_shapes), in_specs=[any_spec] * n_in, out_specs=[any_spec] * n_out,
        scratch_shapes=list(rider.sems),
    )(*rider.arrs)


def _call(body, *, name, grid, in_specs, out_specs, out_shape, args, scratch_shapes=(), rider=None):
    params = _params(("arbitrary",) * len(grid))
    if rider is None:
        return pl.pallas_call(body, name=name, grid=grid, in_specs=in_specs, out_specs=out_specs, out_shape=out_shape,
                              scratch_shapes=list(scratch_shapes), compiler_params=params)(*args)
    single = not isinstance(out_shape, (list, tuple))
    outs = [out_shape] if single else list(out_shape)
    ospecs = [out_specs] if single else list(out_specs)
    n_in, n_out, n_scr = len(in_specs), len(outs), len(scratch_shapes)
    r_in, r_out = len(rider.arrs), len(rider.out_shapes)

    def hosted(*refs):
        refs = list(refs)
        ins, rins = refs[:n_in], refs[n_in:n_in + r_in]
        refs = refs[n_in + r_in:]
        houts, routs = refs[:n_out], refs[n_out:n_out + r_out]
        refs = refs[n_out + r_out:]
        scr, rsems = refs[:n_scr], refs[n_scr:]
        ids = [pl.program_id(a) for a in range(len(grid))]
        first = functools.reduce(lambda p, q: p & q, [i == 0 for i in ids])
        last = functools.reduce(lambda p, q: p & q, [i == g - 1 for i, g in zip(ids, grid)])

        @pl.when(first)
        def _():
            rider.start(rins, routs, rsems)

        body(*ins, *houts, *scr)

        @pl.when(last)
        def _():
            rider.finish(rins, routs, rsems)

    any_spec = pl.BlockSpec(memory_space=pl.ANY)
    res = pl.pallas_call(
        hosted, name=name, grid=grid, in_specs=list(in_specs) + [any_spec] * r_in,
        out_specs=ospecs + [any_spec] * r_out, out_shape=outs + list(rider.out_shapes),
        scratch_shapes=list(scratch_shapes) + list(rider.sems), compiler_params=params,
    )(*args, *rider.arrs)
    return (res[0] if single else list(res[:n_out])), list(res[n_out:])


def _sibling_sum(part, landed, pos, name):
    _, _, rows, cols = part.shape
    tr = _tile(rows, 512)

    def body(pos_ref, p_ref, l_ref, own_ref, all_ref):
        s = p_ref[0, 0] + l_ref[0, 0]
        all_ref[0] = s.astype(BF16)

        @pl.when(pl.program_id(1) == pos_ref[1])
        def _():
            own_ref[...] = s

    return pl.pallas_call(
        body, name=name,
        grid_spec=pltpu.PrefetchScalarGridSpec(
            num_scalar_prefetch=1, grid=(rows // tr, 4),
            in_specs=[pl.BlockSpec((1, 1, tr, cols), lambda i, k, pos: (k, pos[0], i, 0)),
                      pl.BlockSpec((1, 1, tr, cols), lambda i, k, pos: (k, 0, i, 0))],
            out_specs=[pl.BlockSpec((tr, cols), lambda i, k, pos: (i, 0)),
                       pl.BlockSpec((1, tr, cols), lambda i, k, pos: (k, i, 0))]),
        out_shape=[jax.ShapeDtypeStruct((rows, cols), F32), jax.ShapeDtypeStruct((4, rows, cols), BF16)],
        compiler_params=_params(("arbitrary", "arbitrary")),
    )(pos, part, landed)


def _rms(x, g):
    r = lax.rsqrt(jnp.mean(x * x, axis=-1, keepdims=True) + EPS)
    return x * r, r


def _norm_matmul(h, g, w, *, flat, nbk, name, scale=1.0, rider=None):
    T, D = h.shape
    nb, _, bn = w.shape
    tm = _tile(T, MATMUL_TILE)

    def body(h_ref, g_ref, w_ref, o_ref, n_ref):
        @pl.when(pl.program_id(1) == 0)
        def _():
            xh, _ = _rms(h_ref[...], None)
            n_ref[...] = (xh * g_ref[...]).astype(BF16)

        n = n_ref[...]
        for k in range(nbk):
            r = _dot(n, w_ref[k])
            r = (r if scale == 1.0 else r * scale).astype(BF16)
            if flat:
                o_ref[:, k * bn:(k + 1) * bn] = r
            else:
                o_ref[k] = r

    if flat:
        out_shape = jax.ShapeDtypeStruct((T, nb * bn), BF16)
        out_spec = pl.BlockSpec((tm, nbk * bn), lambda i, j: (i, j))
    else:
        out_shape = jax.ShapeDtypeStruct((nb, T, bn), BF16)
        out_spec = pl.BlockSpec((nbk, tm, bn), lambda i, j: (j, i, 0))
    return _call(
        body, name=name, grid=(T // tm, nb // nbk),
        in_specs=[pl.BlockSpec((tm, D), lambda i, j: (i, 0)),
                  pl.BlockSpec((1, D), lambda i, j: (0, 0)),
                  pl.BlockSpec((nbk, D, bn), lambda i, j: (j, 0, 0))],
        out_specs=[out_spec, pl.BlockSpec((tm, D), lambda i, j: (i, 0))],
        out_shape=[out_shape, jax.ShapeDtypeStruct((T, D), BF16)],
        args=(h, g, w), rider=rider)


def _matmul_nt(dy, w, *, flat, nbk, name, norm=None, out_dtype=BF16, rider=None):
    nb, R, bn = w.shape
    T = dy.shape[0] if flat else dy.shape[1]
    tm = _tile(T, MATMUL_TILE)
    nj = nb // nbk

    def body(*refs):
        if norm is None:
            dy_ref, w_ref, o_ref, acc_ref = refs
        else:
            dy_ref, w_ref, h_ref, g_ref, dres_ref, o_ref, dg_ref, acc_ref = refs
        i, j = pl.program_id(0), pl.program_id(1)

        @pl.when(j == 0)
        def _():
            acc_ref[...] = jnp.zeros_like(acc_ref)

        acc = acc_ref[...]
        for k in range(nbk):
            d = dy_ref[:, k * bn:(k + 1) * bn] if flat else dy_ref[k]
            acc = acc + _dot_nt(d.astype(BF16), w_ref[k])
        acc_ref[...] = acc

        @pl.when(j == nj - 1)
        def _():
            if norm is None:
                o_ref[...] = acc.astype(out_dtype)
            else:
                xh, r = _rms(h_ref[...], None)

                @pl.when(i == 0)
                def _():
                    dg_ref[...] = jnp.zeros_like(dg_ref)

                dg_ref[0:1, :] += jnp.sum(acc * xh, axis=0, keepdims=True)
                dn = acc * g_ref[...]
                o_ref[...] = dres_ref[...] + r * (dn - xh * jnp.mean(dn * xh, axis=-1, keepdims=True))

    if flat:
        dy_spec = pl.BlockSpec((tm, nbk * bn), lambda i, j: (i, j))
    else:
        dy_spec = pl.BlockSpec((nbk, tm, bn), lambda i, j: (j, i, 0))
    w_spec = pl.BlockSpec((nbk, R, bn), lambda i, j: (j, 0, 0))
    row_spec = pl.BlockSpec((tm, R), lambda i, j: (i, 0))
    if norm is None:
        in_specs, args = [dy_spec, w_spec], (dy, w)
        out_specs = row_spec
        out_shape = jax.ShapeDtypeStruct((T, R), out_dtype)
    else:
        in_specs = [dy_spec, w_spec, row_spec, pl.BlockSpec((1, R), lambda i, j: (0, 0)), row_spec]
        args = (dy, w) + tuple(norm)
        out_specs = [row_spec, pl.BlockSpec((8, R), lambda i, j: (0, 0))]
        out_shape = [jax.ShapeDtypeStruct((T, R), F32), jax.ShapeDtypeStruct((8, R), F32)]
    return _call(
        body, name=name, grid=(T // tm, nj), in_specs=in_specs, out_specs=out_specs, out_shape=out_shape,
        scratch_shapes=[pltpu.VMEM((tm, R), F32)], args=args, rider=rider)


def _wgrad_cols(n, dy, *, flat, nb, nbk, name, rider=None):
    T, D = n.shape
    halves = isinstance(dy, tuple)
    bn = dy.shape[1] // nb if flat else (dy[0] if halves else dy).shape[2]
    tt = _tile(T, MATMUL_TILE)
    nt = T // tt
    nj = nb // nbk

    def body(*refs):
        n_ref, dy_refs, (o_ref, acc_ref) = refs[0], refs[1:-2], refs[-2:]
        j, t = pl.program_id(0), pl.program_id(1)

        @pl.when(t == 0)
        def _():
            acc_ref[...] = jnp.zeros_like(acc_ref)

        def accumulate(dy_ref):
            nv = n_ref[...]
            for k in range(nbk):
                d = dy_ref[:, k * bn:(k + 1) * bn] if flat else dy_ref[k]
                acc_ref[k] += _dot_tn(nv, d)

        if halves:
            pl.when(j < nj // 2)(lambda: accumulate(dy_refs[0]))
            pl.when(j >= nj // 2)(lambda: accumulate(dy_refs[1]))
        else:
            accumulate(dy_refs[0])

        @pl.when(t == nt - 1)
        def _():
            o_ref[...] = acc_ref[...]

    if flat:
        dy_specs, dys = [pl.BlockSpec((tt, nbk * bn), lambda j, t: (t, j))], [dy]
    elif halves:
        first = pl.BlockSpec((nbk, tt, bn), lambda j, t: (jnp.minimum(j, nj // 2 - 1),
                                                          jnp.where(j < nj // 2, t, nt - 1), 0))
        second = pl.BlockSpec((nbk, tt, bn), lambda j, t: (jnp.maximum(j - nj // 2, 0),
                                                           jnp.where(j >= nj // 2, t, 0), 0))
        dy_specs, dys = [first, second], list(dy)
    else:
        dy_specs, dys = [pl.BlockSpec((nbk, tt, bn), lambda j, t: (j, t, 0))], [dy]
    return _call(
        body, name=name, grid=(nj, nt),
        in_specs=[pl.BlockSpec((tt, D), lambda j, t: (t, 0))] + dy_specs,
        out_specs=pl.BlockSpec((nbk, D, bn), lambda j, t: (j, 0, 0)),
        out_shape=jax.ShapeDtypeStruct((nb, D, bn), F32),
        scratch_shapes=[pltpu.VMEM((nbk, D, bn), F32)], args=[n] + dys, rider=rider)


def _wgrad_rows(xa, dh, *, flat, tk, name, rider=None):
    T, D = dh.shape
    nk = xa.shape[1] // tk if flat else xa.shape[0]
    tt = _tile(T, MATMUL_TILE)
    nt = T // tt

    def body(x_ref, dh_ref, o_ref, acc_ref):
        t = pl.program_id(1)

        @pl.when(t == 0)
        def _():
            acc_ref[...] = jnp.zeros_like(acc_ref)

        xv = x_ref[...] if flat else x_ref[0]
        acc_ref[...] += _dot_tn(xv, dh_ref[...].astype(BF16))

        @pl.when(t == nt - 1)
        def _():
            o_ref[...] = acc_ref[...]

    x_spec = pl.BlockSpec((tt, tk), lambda j, t: (t, j)) if flat else pl.BlockSpec((1, tt, tk), lambda j, t: (j, t, 0))
    return _call(
        body, name=name, grid=(nk, nt),
        in_specs=[x_spec, pl.BlockSpec((tt, D), lambda j, t: (t, 0))],
        out_specs=pl.BlockSpec((tk, D), lambda j, t: (j, 0)),
        out_shape=jax.ShapeDtypeStruct((nk * tk, D), F32),
        scratch_shapes=[pltpu.VMEM((tk, D), F32)], args=(xa, dh), rider=rider)


def _matmul_residual(xa, w, res, name):
    T, K = xa.shape
    D = w.shape[1]
    tm = _tile(T, MATMUL_TILE)

    def body(x_ref, w_ref, r_ref, o_ref):
        o_ref[...] = r_ref[...] + _dot(x_ref[...], w_ref[...])

    return pl.pallas_call(
        body, name=name, grid=(T // tm,),
        in_specs=[pl.BlockSpec((tm, K), lambda i: (i, 0)), pl.BlockSpec((K, D), lambda i: (0, 0)),
                  pl.BlockSpec((tm, D), lambda i: (i, 0))],
        out_specs=pl.BlockSpec((tm, D), lambda i: (i, 0)),
        out_shape=jax.ShapeDtypeStruct((T, D), F32),
        compiler_params=_params(("arbitrary",)),
    )(xa, w, res)


def _gmlp_gate(z, ws, bst, gv, G, gd):
    D = G * gd
    u = _gelu(z[:, :D].astype(F32))
    v = _gelu(z[:, D:].astype(F32))
    vh, r = _rms(v, None)
    vn = (vh * gv).astype(BF16)
    return u, v, vh, r, vn


def _gmlp_forward(z, ws, bst, gv, w_out, x, *, name, rider=None):
    T, D2 = z.shape
    D = D2 // 2
    G = ws.shape[0]
    gd = D // G
    tb = _tile(T, 256)
    nblk = tb // GMLP_BLOCK

    def body(z_ref, ws_ref, b_ref, gv_ref, wo_ref, x_ref, gated_ref, h_ref):
        u, _, _, _, vn = _gmlp_gate(z_ref[...], None, None, gv_ref[...], G, gd)
        for n in range(nblk):
            rows = slice(n * GMLP_BLOCK, (n + 1) * GMLP_BLOCK)
            for gi in range(G):
                cols = slice(gi * gd, (gi + 1) * gd)
                s = _dot(ws_ref[gi], vn[rows, cols]) + b_ref[:, gi:gi + 1]
                gated_ref[rows, cols] = (u[rows, cols] * s).astype(BF16)
        h_ref[...] = x_ref[...] + _dot(gated_ref[...], wo_ref[...])

    return _call(
        body, name=name, grid=(T // tb,),
        in_specs=[pl.BlockSpec((tb, D2), lambda i: (i, 0)), pl.BlockSpec(ws.shape, lambda i: (0, 0, 0)),
                  pl.BlockSpec(bst.shape, lambda i: (0, 0)), pl.BlockSpec((1, D), lambda i: (0, 0)),
                  pl.BlockSpec((D, D), lambda i: (0, 0)), pl.BlockSpec((tb, D), lambda i: (i, 0))],
        out_specs=[pl.BlockSpec((tb, D), lambda i: (i, 0)), pl.BlockSpec((tb, D), lambda i: (i, 0))],
        out_shape=[jax.ShapeDtypeStruct((T, D), BF16), jax.ShapeDtypeStruct((T, D), F32)],
        args=(z, ws, bst, gv, w_out, x), rider=rider)


def _gmlp_backward(z, dgated, ws, bst, gv, mask, *, name, rider=None):
    T, D2 = z.shape
    D = D2 // 2
    G = ws.shape[0]
    gd = D // G
    tb = _tile(T, 256)
    nblk = tb // GMLP_BLOCK

    def body(z_ref, dg_ref, ws_ref, b_ref, gv_ref, mask_ref, dz_ref, dws_ref, db_ref, dgv_ref, dvn_ref):
        @pl.when(pl.program_id(0) == 0)
        def _():
            dws_ref[...] = jnp.zeros_like(dws_ref)
            db_ref[...] = jnp.zeros_like(db_ref)
            dgv_ref[...] = jnp.zeros_like(dgv_ref)

        zf = z_ref[...]
        u, v, vh, r, vn = _gmlp_gate(zf, None, None, gv_ref[...], G, gd)
        dg = dg_ref[...].astype(F32)
        for n in range(nblk):
            rows = slice(n * GMLP_BLOCK, (n + 1) * GMLP_BLOCK)
            for gi in range(G):
                cols = slice(gi * gd, (gi + 1) * gd)
                vblk = vn[rows, cols]
                s = _dot(ws_ref[gi], vblk) + b_ref[:, gi:gi + 1]
                dgb = dg[rows, cols]
                ds = dgb * u[rows, cols]
                dsb = ds.astype(BF16)
                dz_ref[rows, cols] = (dgb * s * _gelu_grad(zf[rows, cols].astype(F32))).astype(BF16)
                dvn_ref[rows, cols] = _dot_tn(ws_ref[gi], dsb)
                dws_ref[gi] += _dot_nt(dsb, vblk) * mask_ref[...]
                db_ref[:, gi:gi + 1] += jnp.sum(ds, axis=1, keepdims=True)
        dvn = dvn_ref[...]
        dgv_ref[0:1, :] += jnp.sum(dvn * vh, axis=0, keepdims=True)
        dn = dvn * gv_ref[...]
        dv = r * (dn - vh * jnp.mean(dn * vh, axis=-1, keepdims=True))
        dz_ref[:, D:] = (dv * _gelu_grad(zf[:, D:].astype(F32))).astype(BF16)

    return _call(
        body, name=name, grid=(T // tb,),
        in_specs=[pl.BlockSpec((tb, D2), lambda i: (i, 0)), pl.BlockSpec((tb, D), lambda i: (i, 0)),
                  pl.BlockSpec(ws.shape, lambda i: (0, 0, 0)), pl.BlockSpec(bst.shape, lambda i: (0, 0)),
                  pl.BlockSpec((1, D), lambda i: (0, 0)), pl.BlockSpec(mask.shape, lambda i: (0, 0))],
        out_specs=[pl.BlockSpec((tb, D2), lambda i: (i, 0)), pl.BlockSpec(ws.shape, lambda i: (0, 0, 0)),
                   pl.BlockSpec(bst.shape, lambda i: (0, 0)), pl.BlockSpec((8, D), lambda i: (0, 0))],
        out_shape=[jax.ShapeDtypeStruct((T, D2), BF16), jax.ShapeDtypeStruct(ws.shape, F32),
                   jax.ShapeDtypeStruct(bst.shape, F32), jax.ShapeDtypeStruct((8, D), F32)],
        scratch_shapes=[pltpu.VMEM((tb, D), F32)], args=(z, dgated, ws, bst, gv, mask), rider=rider)


def _shift_rows(x, k):
    return pltpu.roll(x, k % x.shape[0], axis=0)


def _conv3(ext, cw):
    return (cw[0:1] * _shift_rows(ext, 2)[8:] + cw[1:2] * _shift_rows(ext, 1)[8:] + cw[2:3] * ext[8:])


def _ffn_forward(a, cw, cb, wd, h, seq, *, name, rider=None):
    _, T, F = a.shape
    D = h.shape[1]
    tm = _tile(seq, TOKEN_TILE)
    hb = tm // 16

    def body(a_ref, ap_ref, cw_ref, cb_ref, wd_ref, h_ref, act_ref, c_ref, o_ref, acc_ref):
        i, j = pl.program_id(0), pl.program_id(1)
        keep = ((i * tm) % seq != 0).astype(F32)

        def conv(b):
            ext = jnp.concatenate([ap_ref[b, 8:16].astype(F32) * keep, a_ref[b].astype(F32)], axis=0)
            return _conv3(ext, cw_ref[b]) + cb_ref[b]

        up, gate = conv(j), conv(j + 4)
        c_ref[j] = up.astype(BF16)
        c_ref[j + 4] = gate.astype(BF16)
        act = (gate * _sigmoid(gate) * up).astype(BF16)
        act_ref[0] = act

        @pl.when(j == 0)
        def _():
            acc_ref[...] = h_ref[...]

        acc_ref[...] += _dot(act, wd_ref[0])

        @pl.when(j == 3)
        def _():
            o_ref[...] = acc_ref[...]

    return _call(
        body, name=name, grid=(T // tm, 4),
        in_specs=[pl.BlockSpec((8, tm, F), lambda i, j: (0, i, 0)),
                  pl.BlockSpec((8, 16, F), lambda i, j: (0, jnp.maximum(i * hb - 1, 0), 0)),
                  pl.BlockSpec((8, 3, F), lambda i, j: (0, 0, 0)), pl.BlockSpec((8, 1, F), lambda i, j: (0, 0, 0)),
                  pl.BlockSpec((1, F, D), lambda i, j: (j, 0, 0)), pl.BlockSpec((tm, D), lambda i, j: (i, 0))],
        out_specs=[pl.BlockSpec((1, tm, F), lambda i, j: (j, i, 0)), pl.BlockSpec((8, tm, F), lambda i, j: (0, i, 0)),
                   pl.BlockSpec((tm, D), lambda i, j: (i, 0))],
        out_shape=[jax.ShapeDtypeStruct((4, T, F), BF16), jax.ShapeDtypeStruct((8, T, F), BF16),
                   jax.ShapeDtypeStruct((T, D), F32)],
        scratch_shapes=[pltpu.VMEM((tm, D), F32)], args=(a, a, cw, cb, wd, h), rider=rider)


def _ffn_backward(dh, c, a, cw, wd, seq, *, name, rider=None):
    _, T, F = a.shape
    D = dh.shape[1]
    tm = _tile(seq, TOKEN_TILE)
    hb = tm // 16
    nt = T // tm

    def body(dh_ref, dhn_ref, cu_ref, cg_ref, cun_ref, cgn_ref, au_ref, ag_ref, cw_ref, wd_ref, da_ref, st_ref):
        i, j = pl.program_id(0), pl.program_id(1)
        keep_next = (((i + 1) * tm) % seq != 0).astype(F32)

        @pl.when((i == 0) & (j == 0))
        def _():
            st_ref[...] = jnp.zeros_like(st_ref)

        dhe = jnp.concatenate([dh_ref[...], dhn_ref[...] * keep_next], axis=0).astype(BF16)
        dact = _dot_nt(dhe, wd_ref[0])
        up = jnp.concatenate([cu_ref[0].astype(F32), cun_ref[0, 0:8].astype(F32)], axis=0)
        gate = jnp.concatenate([cg_ref[0].astype(F32), cgn_ref[0, 0:8].astype(F32)], axis=0)
        sg = _sigmoid(gate)
        gs = gate * sg
        d_up = dact * gs
        d_gate = dact * up * (sg + gs * (1.0 - sg))

        def finish(b, a_ref, dc):
            w = cw_ref[b]
            dm, u1, u2 = dc[:tm], _shift_rows(dc, -1)[:tm], _shift_rows(dc, -2)[:tm]
            da_ref[b] = (w[2:3] * dm + w[1:2] * u1 + w[0:1] * u2).astype(BF16)
            av = a_ref[0].astype(F32)
            st_ref[b, 0:1, :] += jnp.sum(u2 * av, axis=0, keepdims=True)
            st_ref[b, 1:2, :] += jnp.sum(u1 * av, axis=0, keepdims=True)
            st_ref[b, 2:3, :] += jnp.sum(dm * av, axis=0, keepdims=True)
            st_ref[b, 3:4, :] += jnp.sum(dm, axis=0, keepdims=True)

        finish(j, au_ref, d_up)
        finish(j + 4, ag_ref, d_gate)

    nxt = lambda i: jnp.minimum((i + 1) * hb, T // 16 - 1)
    return _call(
        body, name=name, grid=(nt, 4),
        in_specs=[pl.BlockSpec((tm, D), lambda i, j: (i, 0)),
                  pl.BlockSpec((8, D), lambda i, j: (jnp.minimum((i + 1) * (tm // 8), T // 8 - 1), 0)),
                  pl.BlockSpec((1, tm, F), lambda i, j: (j, i, 0)), pl.BlockSpec((1, tm, F), lambda i, j: (j + 4, i, 0)),
                  pl.BlockSpec((1, 16, F), lambda i, j: (j, nxt(i), 0)),
                  pl.BlockSpec((1, 16, F), lambda i, j: (j + 4, nxt(i), 0)),
                  pl.BlockSpec((1, tm, F), lambda i, j: (j, i, 0)), pl.BlockSpec((1, tm, F), lambda i, j: (j + 4, i, 0)),
                  pl.BlockSpec((8, 3, F), lambda i, j: (0, 0, 0)),
                  pl.BlockSpec((1, F, D), lambda i, j: (j, 0, 0))],
        out_specs=[pl.BlockSpec((8, tm, F), lambda i, j: (0, i, 0)), pl.BlockSpec((8, 8, F), lambda i, j: (0, 0, 0))],
        out_shape=[jax.ShapeDtypeStruct((8, T, F), BF16), jax.ShapeDtypeStruct((8, 8, F), F32)],
        args=(dh, dh, c, c, c, c, a, a, cw, wd), rider=rider)


def _ffn_fused_forward(h, g, w_in, cw, cb, wd, seq, *, name, rider=None):
    T, D = h.shape
    F = w_in.shape[2]
    tm = _tile(seq, TOKEN_TILE)
    bps = 1
    nj = 4 // bps

    def body(h_ref, g_ref, wu_ref, wg_ref, cw_ref, cb_ref, wd_ref,
             au_ref, ag_ref, cu_ref, cg_ref, act_ref, n_ref, o_ref, acc_ref, carry_ref, *work_refs):
        eu_refs, eg_refs, stage_refs = work_refs[:bps], work_refs[bps:2 * bps], work_refs[2 * bps:]
        i, j = pl.program_id(0), pl.program_id(1)
        keep = ((i * tm) % seq != 0).astype(F32)

        @pl.when((i == 0) & (j == 0))
        def _():
            carry_ref[...] = jnp.zeros_like(carry_ref)

        @pl.when(j == 0)
        def _():
            xh, _ = _rms(h_ref[...], None)
            n_ref[...] = (xh * g_ref[...]).astype(BF16)
            acc_ref[...] = h_ref[...]

        n = n_ref[...]

        def project(b, k, w_ref, a_ref, ext_ref):
            a = _dot(n, w_ref[k]).astype(BF16)
            a_ref[k] = a
            ext_ref[0:8, :] = carry_ref[b] * keep
            ext_ref[8:, :] = a.astype(F32)
            carry_ref[b] = ext_ref[tm:tm + 8, :]

        def conv(b, ext_ref, r):
            x, w = ext_ref[r:r + ROWS + 8, :], cw_ref[b]
            return (w[0:1] * _shift_rows(x, 2) + w[1:2] * _shift_rows(x, 1) + w[2:3] * x)[8:] + cb_ref[b]

        for k in range(bps):
            project(j * bps + k, k, wu_ref, au_ref, eu_refs[k])
            project(j * bps + k + 4, k, wg_ref, ag_ref, eg_refs[k])
        outs = []
        for k in range(bps):
            for r in range(0, tm, ROWS):
                up, gate = conv(j * bps + k, eu_refs[k], r), conv(j * bps + k + 4, eg_refs[k], r)
                cu_ref[k, r:r + ROWS, :] = up.astype(BF16)
                cg_ref[k, r:r + ROWS, :] = gate.astype(BF16)
                stage_refs[k][r:r + ROWS, :] = (gate * _sigmoid(gate) * up).astype(BF16)
            act = stage_refs[k][...]
            act_ref[k] = act
            outs.append(_dot(act, wd_ref[k]))
        acc_ref[...] += functools.reduce(lambda p, q: p + q, outs)

        @pl.when(j == nj - 1)
        def _():
            o_ref[...] = acc_ref[...]

    blk = pl.BlockSpec((bps, tm, F), lambda i, j: (j, i, 0))
    row = pl.BlockSpec((tm, D), lambda i, j: (i, 0))
    half = jax.ShapeDtypeStruct((4, T, F), BF16)
    work = [pltpu.VMEM((tm + 8, F), F32)] * (2 * bps) + [pltpu.VMEM((tm, F), BF16)] * bps
    return _call(
        body, name=name, grid=(T // tm, nj),
        in_specs=[row, pl.BlockSpec((1, D), lambda i, j: (0, 0)),
                  pl.BlockSpec((bps, D, F), lambda i, j: (j, 0, 0)), pl.BlockSpec((bps, D, F), lambda i, j: (j + nj, 0, 0)),
                  pl.BlockSpec((8, 3, F), lambda i, j: (0, 0, 0)), pl.BlockSpec((8, 1, F), lambda i, j: (0, 0, 0)),
                  pl.BlockSpec((bps, F, D), lambda i, j: (j, 0, 0))],
        out_specs=[blk, blk, blk, blk, blk, row, row],
        out_shape=[half, half, half, half, half, jax.ShapeDtypeStruct((T, D), BF16), jax.ShapeDtypeStruct((T, D), F32)],
        scratch_shapes=[pltpu.VMEM((tm, D), F32), pltpu.VMEM((8, 8, F), F32)] + work,
        args=(h, g, w_in, w_in, cw, cb, wd), rider=rider)


def _ffn_fused_backward(dh, cu, cg, au, ag, cw, wd, w_in, h, g, seq, *, name, rider=None):
    T, D = dh.shape
    F = wd.shape[1]
    tm = _tile(seq, TOKEN_TILE // 2)
    hb = tm // 16
    nt = T // tm
    bps = FFN_BLOCKS_PER_STEP
    nj = 4 // bps

    def body(dh_ref, dhn_ref, cu_ref, cg_ref, cun_ref, cgn_ref, au_ref, ag_ref, cw_ref, wd_ref, wu_ref, wg_ref,
             h_ref, g_ref, dau_ref, dag_ref, st_ref, o_ref, dg_ref, acc_ref, dact_ref, du_ref, dgt_ref):
        j, i = pl.program_id(0), pl.program_id(1)
        keep_next = (((i + 1) * tm) % seq != 0).astype(F32)
        tile_rows = pl.ds(pl.multiple_of(i * tm, tm), tm)

        @pl.when((i == 0) & (j == 0))
        def _():
            st_ref[...] = jnp.zeros_like(st_ref)
            dg_ref[...] = jnp.zeros_like(dg_ref)

        dhe = jnp.concatenate([dh_ref[...], dhn_ref[...] * keep_next], axis=0).astype(BF16)
        for k in range(bps):
            dact_ref[k] = _dot_nt(dhe, wd_ref[k])

        def conv_grads(k):
            for r in range(0, tm + 8, ROWS):
                if r < tm:
                    rows = slice(r, r + ROWS)
                    up, gate = cu_ref[k, rows, :].astype(F32), cg_ref[k, rows, :].astype(F32)
                else:
                    rows = slice(tm, tm + 8)
                    up, gate = cun_ref[k, 0:8, :].astype(F32), cgn_ref[k, 0:8, :].astype(F32)
                dact = dact_ref[k, rows, :]
                sg = _sigmoid(gate)
                gs = gate * sg
                du_ref[k, rows, :] = dact * gs
                dgt_ref[k, rows, :] = dact * up * (sg + gs * (1.0 - sg))

        def finish(b, k, a_ref, w_ref, dc_ref, da_ref):
            w = cw_ref[b]
            sums = [jnp.zeros((8, F), F32) for _ in range(4)]
            fold = lambda t: jnp.sum(t.reshape(ROWS // 8, 8, F), axis=0)
            for r in range(0, tm, ROWS):
                dc = dc_ref[k, r:r + ROWS + 8, :]
                dm, u1, u2 = dc[:ROWS], _shift_rows(dc, -1)[:ROWS], _shift_rows(dc, -2)[:ROWS]
                da_ref[k, r:r + ROWS, :] = (w[2:3] * dm + w[1:2] * u1 + w[0:1] * u2).astype(BF16)
                av = a_ref[k, r:r + ROWS, :].astype(F32)
                for s, t in enumerate((u2 * av, u1 * av, dm * av, dm)):
                    sums[s] = sums[s] + fold(t)
            for s in range(4):
                st_ref[b, s:s + 1, :] += jnp.sum(sums[s], axis=0, keepdims=True)
            return _dot_nt(da_ref[k], w_ref[k])

        dn_parts = []
        for k in range(bps):
            conv_grads(k)
            dn_parts.append(finish(j * bps + k, k, au_ref, wu_ref, du_ref, dau_ref))
            dn_parts.append(finish(j * bps + k + 4, k, ag_ref, wg_ref, dgt_ref, dag_ref))
        dn_part = functools.reduce(lambda p, q: p + q, dn_parts)

        @pl.when(j == 0)
        def _():
            acc_ref[tile_rows, :] = dn_part

        @pl.when(j > 0)
        def _():
            acc_ref[tile_rows, :] += dn_part

        @pl.when(j == nj - 1)
        def _():
            acc = acc_ref[tile_rows, :]
            xh, r = _rms(h_ref[...], None)
            dg_ref[0:1, :] += jnp.sum(acc * xh, axis=0, keepdims=True)
            dn = acc * g_ref[...]
            o_ref[...] = dh_ref[...] + r * (dn - xh * jnp.mean(dn * xh, axis=-1, keepdims=True))

    last = lambda j, i: jnp.where(j == nj - 1, i, 0)
    nxt = lambda i: jnp.minimum((i + 1) * hb, T // 16 - 1)
    blk = pl.BlockSpec((bps, tm, F), lambda j, i: (j, i, 0))
    halo = pl.BlockSpec((bps, 16, F), lambda j, i: (j, nxt(i), 0))
    row = pl.BlockSpec((tm, D), lambda j, i: (i, 0))
    work = pltpu.VMEM((bps, tm + 8, F), F32)
    half = jax.ShapeDtypeStruct((4, T, F), BF16)
    return _call(
        body, name=name, grid=(nj, nt),
        in_specs=[row, pl.BlockSpec((8, D), lambda j, i: (jnp.minimum((i + 1) * (tm // 8), T // 8 - 1), 0)),
                  blk, blk, halo, halo, blk, blk,
                  pl.BlockSpec((8, 3, F), lambda j, i: (0, 0, 0)), pl.BlockSpec((bps, F, D), lambda j, i: (j, 0, 0)),
                  pl.BlockSpec((bps, D, F), lambda j, i: (j, 0, 0)),
                  pl.BlockSpec((bps, D, F), lambda j, i: (j + nj, 0, 0)),
                  pl.BlockSpec((tm, D), lambda j, i: (last(j, i), 0)), pl.BlockSpec((1, D), lambda j, i: (0, 0))],
        out_specs=[blk, blk, pl.BlockSpec((8, 8, F), lambda j, i: (0, 0, 0)),
                   pl.BlockSpec((tm, D), lambda j, i: (last(j, i), 0)), pl.BlockSpec((8, D), lambda j, i: (0, 0))],
        out_shape=[half, half, jax.ShapeDtypeStruct((8, 8, F), F32),
                   jax.ShapeDtypeStruct((T, D), F32), jax.ShapeDtypeStruct((8, D), F32)],
        scratch_shapes=[pltpu.VMEM((T, D), F32), work, work, work],
        args=(dh, dh, cu, cg, cu, cg, au, ag, cw, wd, w_in, w_in, h, g), rider=rider)


def _rel_onehot():
    r = lax.broadcasted_iota(jnp.int32, (REL_PAD, SKEW), 0)
    n = lax.broadcasted_iota(jnp.int32, (REL_PAD, SKEW), 1)
    off = jnp.where(n >= WIN, n - SKEW, n)
    idx = jnp.minimum(PAD - off, REL_CLIP) + REL_CLIP
    return (r == idx).astype(BF16)


def _skew(x, sign):
    row = lax.broadcasted_iota(jnp.int32, x.shape, 0)
    for b in range(7):
        x = jnp.where((row >> b) & 1 == 1, pltpu.roll(x, (sign * (1 << b)) % SKEW, axis=1), x)
    return x


def _bias_build(rel, name, rider=None):
    H = rel.shape[0]

    def body(rel_ref, o_ref):
        oh = _rel_onehot()
        hi, mid, lo = _split3(rel_ref[...])
        base = _dot(hi, oh) + _dot(mid, oh) + _dot(lo, oh)
        mine = lax.broadcasted_iota(jnp.int32, (H, 1), 0) == pl.program_id(0)
        row = jnp.sum(jnp.where(mine, base, 0.0), axis=0, keepdims=True)
        q = lax.broadcasted_iota(jnp.int32, (Q_TILE, WIN), 0)
        k = lax.broadcasted_iota(jnp.int32, (Q_TILE, WIN), 1)
        ok = ((q < CHUNK) & (k < WIN - CHUNK)) | ((q >= CHUNK) & (k >= CHUNK))
        t = _skew(jnp.broadcast_to(row, (Q_TILE, SKEW)), 1)
        o_ref[0] = jnp.where(ok, t[:, :WIN], NEG_INF)

    return _call(
        body, name=name, grid=(H,), in_specs=[pl.BlockSpec((H, REL_PAD), lambda h: (0, 0))],
        out_specs=pl.BlockSpec((1, Q_TILE, WIN), lambda h: (h, 0, 0)),
        out_shape=jax.ShapeDtypeStruct((H, Q_TILE, WIN), F32), args=(rel,), rider=rider)


def _bias_reduce(dbias, name):
    H = dbias.shape[0]

    def body(d_ref, o_ref, e_ref):
        oh = _rel_onehot()
        for hd in range(H):
            x = jnp.concatenate([d_ref[hd], jnp.zeros((Q_TILE, SKEW - WIN), F32)], axis=1)
            e_ref[hd:hd + 1, :] = jnp.sum(_skew(x, -1), axis=0, keepdims=True)
        hi, mid, lo = _split3(e_ref[...])
        o_ref[...] = _dot_nt(hi, oh) + _dot_nt(mid, oh) + _dot_nt(lo, oh)

    return pl.pallas_call(
        body, name=name, out_shape=jax.ShapeDtypeStruct((H, REL_PAD), F32),
        in_specs=[pl.BlockSpec(memory_space=pltpu.VMEM)], out_specs=pl.BlockSpec(memory_space=pltpu.VMEM),
        scratch_shapes=[pltpu.VMEM((H, SKEW), F32)],
        compiler_params=_params(),
    )(dbias)


def _pair_stack(xp, even):
    z = jnp.zeros_like(xp)
    return jnp.concatenate([jnp.where(even, xp, z), jnp.where(even, z, xp)], axis=0)


def _pair_merge(y, even):
    return jnp.where(even, y[:Q_TILE], y[Q_TILE:])


def _strip_probs(s_ref, b_ref, pp, r, valid, base=0):
    hb, hr = divmod(r, Q_TILE)
    s = s_ref[base + pp, r:r + STRIP, :] + b_ref[2 * pp + hb, hr:hr + STRIP, :]
    s = jnp.where(valid, s, NEG_INF)
    e = jnp.exp(s - jnp.max(s, axis=-1, keepdims=True))
    return e * (1.0 / jnp.sum(e, axis=-1, keepdims=True))


def _fill_padded(dst_ref, src_ref):
    dst_ref[0:PAD, :] = jnp.zeros((PAD, dst_ref.shape[1]), dst_ref.dtype)
    dst_ref[PAD:, :] = src_ref[...]


def _attn_specs(B, S, D, lanes):
    nt = S // (TILES_PER_STEP * Q_TILE)
    q_spec = pl.BlockSpec((TILES_PER_STEP * Q_TILE, lanes), lambda g, b, i: (b * nt + i, g))
    k_spec = pl.BlockSpec((S, lanes), lambda g, b, i: (b, g))
    v_spec = pl.BlockSpec((S, lanes), lambda g, b, i: (b, D // lanes + g))
    bias_spec = pl.BlockSpec((lanes // HEAD_DIM, Q_TILE, WIN), lambda g, b, i: (g, 0, 0))
    return nt, q_spec, k_spec, v_spec, bias_spec


def _attn_forward(q, kv, bias, S, *, name, rider=None):
    T, D = q.shape
    B = T // S
    lanes = min(FWD_HEADS_PER_STEP * HEAD_DIM, D)
    nt, q_spec, k_spec, v_spec, bias_spec = _attn_specs(B, S, D, lanes)

    npairs = lanes // (2 * HEAD_DIM)

    def body(q_ref, k_ref, v_ref, b_ref, o_ref, kp_ref, vp_ref, s_ref, p_ref):
        i = pl.program_id(2)

        @pl.when(i == 0)
        def _():
            _fill_padded(kp_ref, k_ref)
            _fill_padded(vp_ref, v_ref)

        even = lax.broadcasted_iota(jnp.int32, (1, 2 * HEAD_DIM), 1) < HEAD_DIM
        pair_cols = [slice(pp * 2 * HEAD_DIM, (pp + 1) * 2 * HEAD_DIM) for pp in range(npairs)]
        for t in range(TILES_PER_STEP):
            tile = i * TILES_PER_STEP + t
            start = pl.multiple_of(tile * Q_TILE, Q_TILE)
            rows = slice(t * Q_TILE, (t + 1) * Q_TILE)
            valid = lax.broadcasted_iota(jnp.int32, (STRIP, WIN), 1) >= PAD - tile * Q_TILE
            for pp, cols in enumerate(pair_cols):
                s_ref[t * npairs + pp] = _dot_nt(_pair_stack(q_ref[rows, cols], even), kp_ref[pl.ds(start, WIN), cols])
            for pp in range(npairs):
                for r in range(0, 2 * Q_TILE, STRIP):
                    p = _strip_probs(s_ref, b_ref, pp, r, valid, base=t * npairs)
                    p_ref[t * npairs + pp, r:r + STRIP, :] = p.astype(BF16)
            for pp, cols in enumerate(pair_cols):
                o = _dot(p_ref[t * npairs + pp], vp_ref[pl.ds(start, WIN), cols])
                o_ref[rows, cols] = _pair_merge(o, even).astype(BF16)

    nbuf = TILES_PER_STEP * npairs
    return _call(
        body, name=name, grid=(D // lanes, B, nt),
        in_specs=[q_spec, k_spec, v_spec, bias_spec], out_specs=q_spec,
        out_shape=jax.ShapeDtypeStruct((T, D), BF16),
        scratch_shapes=[pltpu.VMEM((S + PAD, lanes), BF16), pltpu.VMEM((S + PAD, lanes), BF16),
                        pltpu.VMEM((nbuf, 2 * Q_TILE, WIN), F32), pltpu.VMEM((nbuf, 2 * Q_TILE, WIN), BF16)],
        args=(q, kv, kv, bias), rider=rider)


def _attn_backward(q, kv, bias, do, S, *, name, rider=None):
    T, D = q.shape
    B = T // S
    H = D // HEAD_DIM
    lanes = min(BWD_HEADS_PER_STEP * HEAD_DIM, D)
    nt, q_spec, k_spec, v_spec, bias_spec = _attn_specs(B, S, D, lanes)
    scale = HEAD_DIM ** -0.5

    npairs = lanes // (2 * HEAD_DIM)

    def body(q_ref, k_ref, v_ref, b_ref, do_ref, dq_ref, dk_ref, dv_ref, db_ref, kp_ref, vp_ref, dka_ref, dva_ref,
             s_ref, dp_ref, p_ref, ds_ref):
        b, i = pl.program_id(1), pl.program_id(2)

        @pl.when((b == 0) & (i == 0))
        def _():
            db_ref[...] = jnp.zeros_like(db_ref)

        @pl.when(i == 0)
        def _():
            _fill_padded(kp_ref, k_ref)
            _fill_padded(vp_ref, v_ref)
            dka_ref[...] = jnp.zeros_like(dka_ref)
            dva_ref[...] = jnp.zeros_like(dva_ref)

        even = lax.broadcasted_iota(jnp.int32, (1, 2 * HEAD_DIM), 1) < HEAD_DIM
        pair_cols = [slice(pp * 2 * HEAD_DIM, (pp + 1) * 2 * HEAD_DIM) for pp in range(npairs)]
        for t in range(TILES_PER_STEP):
            tile = i * TILES_PER_STEP + t
            start = pl.multiple_of(tile * Q_TILE, Q_TILE)
            rows = slice(t * Q_TILE, (t + 1) * Q_TILE)
            valid = lax.broadcasted_iota(jnp.int32, (STRIP, WIN), 1) >= PAD - tile * Q_TILE
            base = t * npairs
            for pp, cols in enumerate(pair_cols):
                s_ref[base + pp] = _dot_nt(_pair_stack(q_ref[rows, cols], even), kp_ref[pl.ds(start, WIN), cols])
                dp_ref[base + pp] = _dot_nt(_pair_stack(do_ref[rows, cols], even), vp_ref[pl.ds(start, WIN), cols])
            for pp in range(npairs):
                for r in range(0, 2 * Q_TILE, STRIP):
                    hb, hr = divmod(r, Q_TILE)
                    p = _strip_probs(s_ref, b_ref, pp, r, valid, base=base)
                    dp = dp_ref[base + pp, r:r + STRIP, :]
                    ds = p * (dp - jnp.sum(p * dp, axis=-1, keepdims=True))
                    db_ref[2 * pp + hb, hr:hr + STRIP, :] += ds
                    p_ref[base + pp, r:r + STRIP, :] = p.astype(BF16)
                    ds_ref[base + pp, r:r + STRIP, :] = ds.astype(BF16)
            for pp, cols in enumerate(pair_cols):
                dsb = ds_ref[base + pp]
                dq = _pair_merge(_dot(dsb, kp_ref[pl.ds(start, WIN), cols]), even) * scale
                dq_ref[rows, cols] = dq.astype(BF16)
                dka_ref[pl.ds(start, WIN), cols] += _dot_tn(dsb, _pair_stack(q_ref[rows, cols], even))
                dva_ref[pl.ds(start, WIN), cols] += _dot_tn(p_ref[base + pp], _pair_stack(do_ref[rows, cols], even))

        @pl.when(i == nt - 1)
        def _():
            dk_ref[...] = dka_ref[PAD:, :].astype(BF16)
            dv_ref[...] = dva_ref[PAD:, :].astype(BF16)

    dkv_shape = jax.ShapeDtypeStruct((T, D), BF16)
    nbuf = TILES_PER_STEP * npairs
    return _call(
        body, name=name, grid=(D // lanes, B, nt),
        in_specs=[q_spec, k_spec, v_spec, bias_spec, q_spec],
        out_specs=[q_spec, k_spec, k_spec, bias_spec],
        out_shape=[jax.ShapeDtypeStruct((T, D), BF16), dkv_shape, dkv_shape,
                   jax.ShapeDtypeStruct((H, Q_TILE, WIN), F32)],
        scratch_shapes=[pltpu.VMEM((S + PAD, lanes), BF16), pltpu.VMEM((S + PAD, lanes), BF16),
                        pltpu.VMEM((S + PAD, lanes), F32), pltpu.VMEM((S + PAD, lanes), F32),
                        pltpu.VMEM((nbuf, 2 * Q_TILE, WIN), F32), pltpu.VMEM((nbuf, 2 * Q_TILE, WIN), F32),
                        pltpu.VMEM((nbuf, 2 * Q_TILE, WIN), BF16), pltpu.VMEM((nbuf, 2 * Q_TILE, WIN), BF16)],
        args=(q, kv, kv, bias, do), rider=rider)


def _loss_head(h, g, target, name):
    T, D = h.shape
    tm = _tile(T, MATMUL_TILE)

    def body(h_ref, g_ref, t_ref, dh_ref, st_ref):
        @pl.when(pl.program_id(0) == 0)
        def _():
            st_ref[...] = jnp.zeros_like(st_ref)

        xh, r = _rms(h_ref[...], None)
        err = xh * g_ref[...] - t_ref[...]
        st_ref[1:2, :] += 0.5 * jnp.sum(jnp.mean(err * err, axis=-1, keepdims=True), axis=0, keepdims=True)
        dy = err * (1.0 / D)
        st_ref[0:1, :] += jnp.sum(dy * xh, axis=0, keepdims=True)
        dn = dy * g_ref[...]
        dh_ref[...] = r * (dn - xh * jnp.mean(dn * xh, axis=-1, keepdims=True))

    row = pl.BlockSpec((tm, D), lambda i: (i, 0))
    return pl.pallas_call(
        body, name=name, grid=(T // tm,),
        in_specs=[row, pl.BlockSpec((1, D), lambda i: (0, 0)), row],
        out_specs=[row, pl.BlockSpec((8, D), lambda i: (0, 0))],
        out_shape=[jax.ShapeDtypeStruct((T, D), F32), jax.ShapeDtypeStruct((8, D), F32)],
        compiler_params=_params(("arbitrary",)),
    )(h, g, target)


def _sum_devices(arrs, name):
    n = len(arrs)

    def body(*refs):
        for a in range(n):
            s = refs[a][0].astype(F32)
            for k in range(1, N_DEV):
                s = s + refs[a][k].astype(F32)
            refs[n + a][...] = s

    vm = pl.BlockSpec(memory_space=pltpu.VMEM)
    return pl.pallas_call(
        body, name=name, out_shape=[jax.ShapeDtypeStruct(a.shape[1:], F32) for a in arrs],
        in_specs=[vm] * n, out_specs=[vm] * n, compiler_params=_params(),
    )(*arrs)


def _adamw_math(w, g, m, v):
    m = ADAM_B1 * m + (1.0 - ADAM_B1) * g
    v = ADAM_B2 * v + (1.0 - ADAM_B2) * (g * g)
    m_hat = m / (1.0 - ADAM_B1 ** ADAM_STEP)
    v_hat = v / (1.0 - ADAM_B2 ** ADAM_STEP)
    delta = -ADAM_LR * (m_hat / (jnp.sqrt(v_hat) + ADAM_EPS) + ADAM_WD * w)
    return delta, m, v


def _adamw_small(items, name):
    n = len(items)

    def body(*refs):
        for a in range(n):
            w, m, v, g = (refs[4 * a + k][...] for k in range(4))
            d, m, v = _adamw_math(w, g, m, v)
            refs[4 * n + 3 * a][...] = d
            refs[4 * n + 3 * a + 1][...] = m
            refs[4 * n + 3 * a + 2][...] = v

    vm = pl.BlockSpec(memory_space=pltpu.VMEM)
    flat = [t for it in items for t in it]
    outs = pl.pallas_call(
        body, name=name,
        out_shape=[jax.ShapeDtypeStruct(it[0].shape, F32) for it in items for _ in range(3)],
        in_specs=[vm] * (4 * n), out_specs=[vm] * (3 * n), compiler_params=_params(),
    )(*flat)
    return [tuple(outs[3 * a:3 * a + 3]) for a in range(n)]


def _adamw_big(w, m, v, owns, landeds, name, rider=None):
    L, R, C = w.shape
    tr = _tile(R, 512)
    nr = R // tr
    counts = [len(ls) for ls in landeds]

    def body(*refs):
        w_ref, m_ref, v_ref = refs[:3]
        g_ref, d_ref, mo_ref, vo_ref = refs[-4:]
        layer = pl.program_id(0)
        at = 3
        for j in range(L):
            own_ref, l_refs = refs[at], refs[at + 1:at + 1 + counts[j]]
            at += 1 + counts[j]

            @pl.when(layer == j)
            def _(own_ref=own_ref, l_refs=l_refs):
                g = own_ref[...]
                for l_ref in l_refs:
                    for k in range(l_ref.shape[0]):
                        g = g + l_ref[k].astype(F32)
                d, mn, vn = _adamw_math(w_ref[0], g, m_ref[0], v_ref[0])
                g_ref[0] = g
                d_ref[0] = d
                mo_ref[0] = mn
                vo_ref[0] = vn

    def pinned(j):
        return lambda l, i: jnp.where(l == j, i, jnp.where(l < j, 0, nr - 1))

    row = pl.BlockSpec((1, tr, C), lambda l, i: (l, i, 0))
    in_specs, args = [row, row, row], [w, m, v]
    for j in range(L):
        in_specs.append(pl.BlockSpec((tr, C), lambda l, i, p=pinned(j): (p(l, i), 0)))
        args.append(owns[j])
        for arr in landeds[j]:
            in_specs.append(pl.BlockSpec((arr.shape[0], tr, C), lambda l, i, p=pinned(j): (0, p(l, i), 0)))
            args.append(arr)
    return _call(body, name=name, grid=(L, nr), in_specs=in_specs, out_specs=[row] * 4,
                 out_shape=[jax.ShapeDtypeStruct((L, R, C), F32)] * 4, args=args, rider=rider)


def kernel(x, a_norm_g, a_w_in, a_v_norm_g, a_w_s, a_b_s, a_w_out, kv_norm_g, w_kv, b_norm_g, b_w_q, b_rel_bias, b_w_o, f_norm_g, f_w_in, f_conv_w, f_conv_b, f_w_down, final_norm_g, loss_target, m_a_norm_g, m_a_w_in, m_a_v_norm_g, m_a_w_s, m_a_b_s, m_a_w_out, m_kv_norm_g, m_w_kv, m_b_norm_g, m_b_w_q, m_b_rel_bias, m_b_w_o, m_f_norm_g, m_f_w_in, m_f_conv_w, m_f_conv_b, m_f_w_down, m_final_norm_g, v_a_norm_g, v_a_w_in, v_a_v_norm_g, v_a_w_s, v_a_b_s, v_a_w_out, v_kv_norm_g, v_w_kv, v_b_norm_g, v_b_w_q, v_b_rel_bias, v_b_w_o, v_f_norm_g, v_f_w_in, v_f_conv_w, v_f_conv_b, v_f_w_down, v_final_norm_g):
    B, S, D = x.shape
    T = B * S
    G = a_w_s.shape[1]
    H = D // HEAD_DIM
    F = f_w_in.shape[2]
    L = f_w_in.shape[0]
    dn = D // N_DEV
    xi, yi, ci = lax.axis_index("x"), lax.axis_index("y"), lax.axis_index("c")
    me = 4 * xi + 2 * yi + ci
    pos = jnp.stack([ci, 2 * xi + yi]).astype(jnp.int32)

    cast = lambda t: t.astype(BF16)
    gather = lambda *ts: _gather_rider(list(ts))
    rel = jnp.pad(b_rel_bias[0], ((0, 0), (0, REL_PAD - b_rel_bias.shape[2])))
    bias, (wa_in, norms_sh, conv_w0, conv_w1) = _bias_build(rel, "bias_build", rider=gather(
        cast(a_w_in[0]), jnp.concatenate([a_norm_g, a_v_norm_g], axis=0), f_conv_w[0], f_conv_w[1]))
    ga = jnp.transpose(norms_sh, (1, 0, 2)).reshape(2, D)
    g_a, g_av = ga[0:1], ga[1:2]

    x2 = x.reshape(T, D)
    tgt = loss_target.reshape(T, D)
    pc = jnp.arange(GMLP_BLOCK) // CHUNK
    mask = (pc[:, None] >= pc[None, :]).astype(F32)
    ws = (a_w_s[0] * mask[None]).astype(BF16)
    bst = jnp.transpose(a_b_s[0])
    four = lambda t: t.reshape((4, 2) + t.shape[1:])

    w_in0_sh = cast(f_w_in[0])
    (z, n_a), (wa_out, w_in0_top) = _norm_matmul(x2, g_a, wa_in, flat=True, nbk=4, name="gmlp_in",
                                                 rider=gather(cast(a_w_out[0]), w_in0_sh[:D // 2]))
    wa_out = wa_out.reshape(D, D)
    (gated, h1), (w_in0_bottom, wf_down0) = _gmlp_forward(z, ws, bst, g_av, wa_out, x2, name="gmlp_mix",
                                                          rider=gather(w_in0_sh[D // 2:], cast(f_w_down[0])))
    w_in0 = jnp.concatenate([w_in0_top, w_in0_bottom], axis=1)
    cw0, cb0, wd0 = conv_w0, f_conv_b[0].reshape(8, 1, F), wf_down0.reshape(4, F, D)
    (au0, ag0, cu0, cg0, act0, n_f0, h2), (wkv, wq, w_in1) = _ffn_fused_forward(
        h1, f_norm_g[0:1], w_in0, cw0, cb0, wd0, S, name="ffn0_fwd",
        rider=gather(cast(w_kv), cast(b_w_q[0]), cast(f_w_in[1])))
    wq = wq.reshape(D, D)
    kv, n_kv = _norm_matmul(h2, kv_norm_g.reshape(1, D), wkv, flat=True, nbk=4, name="kv_proj")
    q, n_q = _norm_matmul(h2, b_norm_g, wq.reshape(1, D, D), flat=True, nbk=1, name="q_proj", scale=HEAD_DIM ** -0.5)
    o, (wo, wf_down1) = _attn_forward(q, kv, bias, S, name="attn", rider=gather(cast(b_w_o[0]), cast(f_w_down[1])))
    wo = wo.reshape(D, D)
    cw1, cb1, wd1 = conv_w1, f_conv_b[1].reshape(8, 1, F), wf_down1.reshape(4, F, D)
    h3 = _matmul_residual(o, wo, h2, "attn_out")
    au1, ag1, cu1, cg1, act1, n_f1, h4 = _ffn_fused_forward(h3, f_norm_g[1:2], w_in1, cw1, cb1, wd1, S, name="ffn1_fwd")

    sums, from_chips = {}, {}

    def sibling_sums(names, parts, landed):
        for nm, p, l in zip(names, parts, landed):
            sums[nm] = _sibling_sum(p, l, pos, "grad_sibling_sum_" + nm)

    def chip_rider(*names):
        return _chip_rider([sums[nm][1] for nm in names])

    dh4, st_final = _loss_head(h4, final_norm_g.reshape(1, D), tgt, "loss_head")
    g_wd1 = _wgrad_rows(act1, dh4, flat=False, tk=F, name="ffn1_dwdown")
    parts = [four(g_wd1.reshape(8, F // 2, D))]
    (dau1, dag1, st_conv1, dh3, st_f1), landed = _ffn_fused_backward(
        dh4, cu1, cg1, au1, ag1, cw1, wd1, w_in1, h3, f_norm_g[1:2], S, name="ffn1_bwd", rider=_sibling_rider(parts))
    sibling_sums(["wd1"], parts, landed)
    g_win1, (from_chips["wd1"],) = _wgrad_cols(n_f1, (dau1, dag1), flat=False, nb=8, nbk=2, name="ffn1_dwin",
                                               rider=chip_rider("wd1"))
    d_o = _matmul_nt(dh3, wo.reshape(1, D, D), flat=True, nbk=1, name="attn_out_dx")
    parts = [four(g_win1)]
    g_wo, landed = _wgrad_rows(o, dh3, flat=True, tk=_tile(D, 512), name="attn_out_dw", rider=_sibling_rider(parts))
    sibling_sums(["win1"], parts, landed)
    (dq, dk, dv, dbias), (from_chips["win1"],) = _attn_backward(
        q, kv, bias, d_o, S, name="attn_bwd", rider=chip_rider("win1"))
    g_rel = _bias_reduce(dbias, "bias_reduce")
    g_wq = _wgrad_cols(n_q, dq, flat=True, nb=1, nbk=1, name="q_dw")
    dh2, st_b = _matmul_nt(dq, wq.reshape(1, D, D), flat=True, nbk=1, name="q_dx", norm=(h2, b_norm_g, dh3))
    dkv = jnp.concatenate([dk, dv], axis=-1)
    g_wkv = _wgrad_cols(n_kv, dkv, flat=True, nb=8, nbk=4, name="kv_dw")
    parts = [four(g_wo.reshape(8, dn, D)), four(g_wq.reshape(8, dn, D)), four(g_wkv)]
    (dh2, st_kv), landed = _matmul_nt(dkv, wkv, flat=True, nbk=4, name="kv_dx",
                                      norm=(h2, kv_norm_g.reshape(1, D), dh2), rider=_sibling_rider(parts))
    sibling_sums(["wo", "wq", "wkv"], parts, landed)
    g_wd0, (from_chips["wo"], from_chips["wq"], from_chips["wkv"]) = _wgrad_rows(
        act0, dh2, flat=False, tk=F, name="ffn0_dwdown", rider=chip_rider("wo", "wq", "wkv"))
    parts = [four(g_wd0.reshape(8, F // 2, D))]
    (dau0, dag0, st_conv0, dh1, st_f0), landed = _ffn_fused_backward(
        dh2, cu0, cg0, au0, ag0, cw0, wd0, w_in0, h1, f_norm_g[0:1], S, name="ffn0_bwd", rider=_sibling_rider(parts))
    sibling_sums(["wd0"], parts, landed)
    g_win0, (ce,) = _wgrad_cols(n_f0, (dau0, dag0), flat=False, nb=8, nbk=2, name="ffn0_dwin",
                                rider=chip_rider("wd0"))
    from_chips["wd0"] = [ce]
    dgated = _matmul_nt(dh1, wa_out.reshape(1, D, D), flat=True, nbk=1, name="gmlp_out_dx")
    parts = [four(g_win0)]
    g_wa_out, landed = _wgrad_rows(gated, dh1, flat=True, tk=_tile(D, 512), name="gmlp_out_dw",
                                   rider=_sibling_rider(parts))
    sibling_sums(["win0"], parts, landed)
    parts = [four(g_wa_out.reshape(8, dn, D))]
    (dz, g_ws, g_bst, st_av), (ce_win0_a, landed) = _gmlp_backward(
        z, dgated, ws, bst, g_av, mask, name="gmlp_bwd",
        rider=_join_riders([_chip_rider([sums["win0"][1]], ks=(1, 2)), _sibling_rider(parts)]))
    sibling_sums(["wa_out"], parts, [landed])
    g_wa_in, (ce_win0_b, ce) = _wgrad_cols(n_a, dz, flat=True, nb=8, nbk=4, name="gmlp_in_dw", rider=_join_riders(
        [_chip_rider([sums["win0"][1]], ks=(3,)), chip_rider("wa_out")]))
    from_chips["win0"], from_chips["wa_out"] = [ce_win0_a, ce_win0_b], [ce]
    vec = jnp.concatenate([st_av[0:1], st_kv[0:1], st_b[0:1], st_f0[0:1], st_f1[0:1], st_final[0:3]], axis=0)
    parts = [four(g_wa_in)]
    (grad_x, st_a), got = _matmul_nt(dz, wa_in, flat=True, nbk=4, name="gmlp_in_dx", norm=(x2, g_a, dh1),
                                     rider=_join_riders([_sibling_rider(parts), gather(
                                         vec, cast(g_ws), cast(g_bst), cast(g_rel), cast(st_conv0), cast(st_conv1))]))
    sibling_sums(["wa_in"], parts, got[0:1])
    small = got[1:]

    def big_update(names, w, m, v, rider=None):
        shape = w.shape
        r = lambda t: t.reshape((len(names), -1, shape[-1]))
        as_list = lambda t: t if isinstance(t, list) else [t]
        outs = _adamw_big(r(w), r(m), r(v), [sums[nm][0] for nm in names],
                          [as_list(from_chips[nm]) for nm in names], "adamw_" + names[0], rider=rider)
        outs, got = (outs, None) if rider is None else outs
        return [t.reshape(shape) for t in outs], got

    u_f_w_in, (ce, st_a) = big_update(["win0", "win1"], f_w_in, m_f_w_in, v_f_w_in,
                                      rider=_join_riders([chip_rider("wa_in"), gather(st_a)]))
    from_chips["wa_in"] = [ce]
    u_f_w_down, _ = big_update(["wd0", "wd1"], f_w_down, m_f_w_down, v_f_w_down)
    u_a_w_in, _ = big_update(["wa_in"], a_w_in, m_a_w_in, v_a_w_in)
    u_w_kv, _ = big_update(["wkv"], w_kv, m_w_kv, v_w_kv)
    u_a_w_out, _ = big_update(["wa_out"], a_w_out, m_a_w_out, v_a_w_out)
    u_b_w_q, _ = big_update(["wq"], b_w_q, m_b_w_q, v_b_w_q)
    u_b_w_o, _ = big_update(["wo"], b_w_o, m_b_w_o, v_b_w_o)

    vec, g_ws, g_bst, g_rel, st_conv0, st_conv1, st_a = _sum_devices(list(small) + [st_a], "sum_small_grads")
    vec = jnp.concatenate([st_a[0:1], vec[0:7]], axis=0)
    loss = vec[7, 0]
    g_a_norm = lax.dynamic_slice_in_dim(vec[0:1], me * dn, dn, axis=1)
    g_av_norm = lax.dynamic_slice_in_dim(vec[1:2], me * dn, dn, axis=1)
    st_conv = jnp.stack([st_conv0, st_conv1])
    g_conv_w = lax.dynamic_index_in_dim(st_conv, me, axis=1, keepdims=False)[:, 0:3]
    g_conv_b = st_conv[:, :, 3, :].reshape(L, 8 * F)
    small_items = [
        (a_norm_g, m_a_norm_g, v_a_norm_g, g_a_norm),
        (a_v_norm_g, m_a_v_norm_g, v_a_v_norm_g, g_av_norm),
        (a_w_s, m_a_w_s, v_a_w_s, g_ws[None]),
        (a_b_s, m_a_b_s, v_a_b_s, jnp.transpose(g_bst)[None]),
        (kv_norm_g.reshape(1, D), m_kv_norm_g.reshape(1, D), v_kv_norm_g.reshape(1, D), vec[2:3]),
        (b_norm_g, m_b_norm_g, v_b_norm_g, vec[3:4]),
        (b_rel_bias, m_b_rel_bias, v_b_rel_bias, g_rel[None, :, :b_rel_bias.shape[2]]),
        (f_norm_g, m_f_norm_g, v_f_norm_g, vec[4:6]),
        (f_conv_w, m_f_conv_w, v_f_conv_w, g_conv_w),
        (f_conv_b, m_f_conv_b, v_f_conv_b, g_conv_b),
        (final_norm_g.reshape(1, D), m_final_norm_g.reshape(1, D), v_final_norm_g.reshape(1, D), vec[6:7]),
    ]
    small_out = _adamw_small(small_items, "adamw_small")
    (u_a_norm, u_av_norm, u_ws, u_bs, u_kvn, u_bn, u_rel, u_fn, u_cw, u_cb, u_fin) = [
        (it[3],) + so for it, so in zip(small_items, small_out)]
    vecD = lambda u: tuple(t.reshape(D) for t in u)
    u_kvn, u_fin = vecD(u_kvn), vecD(u_fin)

    order = [u_a_norm, u_a_w_in, u_av_norm, u_ws, u_bs, u_a_w_out, u_kvn, u_w_kv, u_bn, u_b_w_q, u_rel, u_b_w_o,
             u_fn, u_f_w_in, u_cw, u_cb, u_f_w_down, u_fin]
    outs = [loss, grad_x.reshape(B, S, D)]
    for k in range(4):
        outs += [u[k] for u in order]
    return tuple(outs)
```

```python
import functools

import jax
import jax.numpy as jnp
from jax import lax
from jax.experimental import pallas as pl
from jax.experimental.pallas import tpu as pltpu

F32 = jnp.float32
BF16 = jnp.bfloat16
MESH = pl.DeviceIdType.MESH

N_DEV = 8
EPS = 1e-6
NEG_INF = -1e30
CHUNK = 64
LEFT_CHUNKS = 8
REL_CLIP = 128
HEAD_DIM = 64
GMLP_BLOCK = 128
Q_TILE = 2 * CHUNK
PAD = LEFT_CHUNKS * CHUNK
WIN = PAD + Q_TILE
SKEW = WIN + Q_TILE
REL_PAD = 384
FWD_HEADS_PER_STEP = 8
BWD_HEADS_PER_STEP = 4
TILES_PER_STEP = 4
STRIP = 32
ROWS = 32
ADAM_LR, ADAM_B1, ADAM_B2, ADAM_EPS, ADAM_WD, ADAM_STEP = 0.001, 0.9, 0.999, 1e-08, 0.01, 10
VMEM_LIMIT = 60 * 1024 * 1024
TOKEN_TILE = 512
MATMUL_TILE = 1024
FFN_BLOCKS_PER_STEP = 2


def _params(sem=None):
    return pltpu.CompilerParams(dimension_semantics=sem, vmem_limit_bytes=VMEM_LIMIT)


def _tile(n, pref):
    if n <= pref:
        return n
    for t in range(pref - pref % 8, 7, -8):
        if n % t == 0:
            return t
    return n


def _gelu(x):
    return 0.5 * x * (1.0 + jnp.tanh(0.7978845608028654 * (x + 0.044715 * x * x * x)))


def _gelu_grad(x):
    t = jnp.tanh(0.7978845608028654 * (x + 0.044715 * x * x * x))
    return 0.5 * (1.0 + t) + 0.5 * x * (1.0 - t * t) * 0.7978845608028654 * (1.0 + 3 * 0.044715 * x * x)


def _sigmoid(x):
    return 1.0 / (1.0 + jnp.exp(-x))


def _dot(a, b):
    return jnp.dot(a, b, preferred_element_type=F32)


def _dot_nt(a, b):
    return lax.dot_general(a, b, (((1,), (1,)), ((), ())), preferred_element_type=F32)


def _dot_tn(a, b):
    return lax.dot_general(a, b, (((0,), (0,)), ((), ())), preferred_element_type=F32)


def _split3(x):
    hi = x.astype(BF16)
    r1 = x - hi.astype(F32)
    mid = r1.astype(BF16)
    lo = (r1 - mid.astype(F32)).astype(BF16)
    return hi, mid, lo


def _mesh_pos():
    return lax.axis_index("x"), lax.axis_index("y"), lax.axis_index("c")


class _Rider:
    def __init__(self, arrs, out_shapes, sems, start, finish):
        self.arrs, self.out_shapes, self.sems, self.start, self.finish = arrs, out_shapes, sems, start, finish


def _gather_rider(arrs):
    n = len(arrs)

    def tools(ins, outs, sems):
        send_sems, recv_sems, local_sems = sems
        x, y, c = _mesh_pos()
        me, sibling = (x, y, c), (x, y, 1 - c)
        chips = [(1 - x, y), (x, 1 - y), (1 - x, 1 - y)]

        def slot(a, block):
            px, py, pc = block
            return outs[a].at[4 * px + 2 * py + pc]

        def copy(a, k, block, to, src=None):
            dst = slot(a, block)
            return pltpu.make_async_remote_copy(
                src_ref=dst if src is None else src, dst_ref=dst,
                send_sem=send_sems.at[a, k], recv_sem=recv_sems.at[a, k], device_id=to, device_id_type=MESH)

        def first(a):
            cps = [copy(a, 0, me, sibling, src=ins[a])]
            return cps + [copy(a, 1 + j, me, (*chip, c), src=ins[a]) for j, chip in enumerate(chips)]

        def mine(a):
            return pltpu.make_async_copy(ins[a], slot(a, me), local_sems.at[a])

        return me, sibling, chips, c, copy, first, mine

    def start(ins, outs, sems):
        _, _, _, _, _, first, mine = tools(ins, outs, sems)
        for a in range(n):
            mine(a).start()
            for cp in first(a):
                cp.start()

    def finish(ins, outs, sems):
        me, sibling, chips, c, copy, first, mine = tools(ins, outs, sems)
        passed = []
        for j, chip in enumerate(chips):
            for a in range(n):
                copy(a, 1 + j, (*chip, c), me).wait_recv()
                fwd = copy(a, 4 + j, (*chip, c), sibling)
                fwd.start()
                passed.append(fwd)
        for a in range(n):
            copy(a, 0, sibling, me).wait_recv()
            for j, chip in enumerate(chips):
                copy(a, 4 + j, (*chip, 1 - c), me).wait_recv()
        for a in range(n):
            for cp in first(a):
                cp.wait_send()
        for cp in passed:
            cp.wait_send()
        for a in range(n):
            mine(a).wait()

    return _Rider(list(arrs), [jax.ShapeDtypeStruct((N_DEV,) + a.shape, a.dtype) for a in arrs],
                  [pltpu.SemaphoreType.DMA((n, 7)), pltpu.SemaphoreType.DMA((n, 7)), pltpu.SemaphoreType.DMA((n,))],
                  start, finish)


def _sibling_rider(arrs):
    n = len(arrs)

    def copies(ins, outs, sems):
        x, y, c = _mesh_pos()
        return [pltpu.make_async_remote_copy(
            src_ref=ins[a].at[:, pl.ds(1 - c, 1)], dst_ref=outs[a],
            send_sem=sems[0].at[a], recv_sem=sems[1].at[a], device_id=(x, y, 1 - c), device_id_type=MESH)
            for a in range(n)]

    def start(ins, outs, sems):
        for cp in copies(ins, outs, sems):
            cp.start()

    def finish(ins, outs, sems):
        for cp in copies(ins, outs, sems):
            cp.wait()

    return _Rider(list(arrs), [jax.ShapeDtypeStruct((4, 1) + a.shape[2:], a.dtype) for a in arrs],
                  [pltpu.SemaphoreType.DMA((n,)), pltpu.SemaphoreType.DMA((n,))], start, finish)


def _chip_rider(arrs, ks=(1, 2, 3)):
    n = len(arrs)

    def copies(ins, outs, sems):
        x, y, c = _mesh_pos()
        cps = []
        for a in range(n):
            for s, k in enumerate(ks):
                px = x if k < 2 else 1 - x
                py = y if k == 2 else 1 - y
                cps.append(pltpu.make_async_remote_copy(
                    src_ref=ins[a].at[2 * px + py], dst_ref=outs[a].at[s],
                    send_sem=sems[0].at[a, s], recv_sem=sems[1].at[a, s],
                    device_id=(px, py, c), device_id_type=MESH))
        return cps

    def start(ins, outs, sems):
        for cp in copies(ins, outs, sems):
            cp.start()

    def finish(ins, outs, sems):
        for cp in copies(ins, outs, sems):
            cp.wait()

    return _Rider(list(arrs), [jax.ShapeDtypeStruct((len(ks),) + a.shape[1:], a.dtype) for a in arrs],
                  [pltpu.SemaphoreType.DMA((n, len(ks))), pltpu.SemaphoreType.DMA((n, len(ks)))], start, finish)


def _join_riders(riders):
    def split(seq, counts):
        out, at = [], 0
        for k in counts:
            out.append(seq[at:at + k])
            at += k
        return out

    n_in = [len(r.arrs) for r in riders]
    n_out = [len(r.out_shapes) for r in riders]
    n_sem = [len(r.sems) for r in riders]

    def run(which):
        def fn(ins, outs, sems):
            for r, i, o, s in zip(riders, split(ins, n_in), split(outs, n_out), split(sems, n_sem)):
                getattr(r, which)(i, o, s)
        return fn

    return _Rider([a for r in riders for a in r.arrs], [o for r in riders for o in r.out_shapes],
                  [s for r in riders for s in r.sems], run("start"), run("finish"))


def _run_rider(rider, name):
    n_in, n_out = len(rider.arrs), len(rider.out_shapes)

    def body(*refs):
        ins, outs, sems = refs[:n_in], refs[n_in:n_in + n_out], refs[n_in + n_out:]
        rider.start(ins, outs, sems)
        rider.finish(ins, outs, sems)

    any_spec = pl.BlockSpec(memory_space=pl.ANY)
    return pl.pallas_call(
        body, name=name, out_shape=list(rider.out_shapes), in_specs=[any_spec] * n_in, out_specs=[any_spec] * n_out,
        scratch_shapes=list(rider.sems),
    )(*rider.arrs)


def _call(body, *, name, grid, in_specs, out_specs, out_shape, args, scratch_shapes=(), rider=None):
    params = _params(("arbitrary",) * len(grid))
    if rider is None:
        return pl.pallas_call(body, name=name, grid=grid, in_specs=in_specs, out_specs=out_specs, out_shape=out_shape,
                              scratch_shapes=list(scratch_shapes), compiler_params=params)(*args)
    single = not isinstance(out_shape, (list, tuple))
    outs = [out_shape] if single else list(out_shape)
    ospecs = [out_specs] if single else list(out_specs)
    n_in, n_out, n_scr = len(in_specs), len(outs), len(scratch_shapes)
    r_in, r_out = len(rider.arrs), len(rider.out_shapes)

    def hosted(*refs):
        refs = list(refs)
        ins, rins = refs[:n_in], refs[n_in:n_in + r_in]
        refs = refs[n_in + r_in:]
        houts, routs = refs[:n_out], refs[n_out:n_out + r_out]
        refs = refs[n_out + r_out:]
        scr, rsems = refs[:n_scr], refs[n_scr:]
        ids = [pl.program_id(a) for a in range(len(grid))]
        first = functools.reduce(lambda p, q: p & q, [i == 0 for i in ids])
        last = functools.reduce(lambda p, q: p & q, [i == g - 1 for i, g in zip(ids, grid)])

        @pl.when(first)
        def _():
            rider.start(rins, routs, rsems)

        body(*ins, *houts, *scr)

        @pl.when(last)
        def _():
            rider.finish(rins, routs, rsems)

    any_spec = pl.BlockSpec(memory_space=pl.ANY)
    res = pl.pallas_call(
        hosted, name=name, grid=grid, in_specs=list(in_specs) + [any_spec] * r_in,
        out_specs=ospecs + [any_spec] * r_out, out_shape=outs + list(rider.out_shapes),
        scratch_shapes=list(scratch_shapes) + list(rider.sems), compiler_params=params,
    )(*args, *rider.arrs)
    return (res[0] if single else list(res[:n_out])), list(res[n_out:])


def _sibling_sum(part, landed, pos, name):
    _, _, rows, cols = part.shape
    tr = _tile(rows, 512)

    def body(pos_ref, p_ref, l_ref, own_ref, all_ref):
        s = p_ref[0, 0] + l_ref[0, 0]
        all_ref[0] = s.astype(BF16)

        @pl.when(pl.program_id(1) == pos_ref[1])
        def _():
            own_ref[...] = s

    return pl.pallas_call(
        body, name=name,
        grid_spec=pltpu.PrefetchScalarGridSpec(
            num_scalar_prefetch=1, grid=(rows // tr, 4),
            in_specs=[pl.BlockSpec((1, 1, tr, cols), lambda i, k, pos: (k, pos[0], i, 0)),
                      pl.BlockSpec((1, 1, tr, cols), lambda i, k, pos: (k, 0, i, 0))],
            out_specs=[pl.BlockSpec((tr, cols), lambda i, k, pos: (i, 0)),
                       pl.BlockSpec((1, tr, cols), lambda i, k, pos: (k, i, 0))]),
        out_shape=[jax.ShapeDtypeStruct((rows, cols), F32), jax.ShapeDtypeStruct((4, rows, cols), BF16)],
        compiler_params=_params(("arbitrary", "arbitrary")),
    )(pos, part, landed)


def _rms(x, g):
    r = lax.rsqrt(jnp.mean(x * x, axis=-1, keepdims=True) + EPS)
    return x * r, r


def _norm_matmul(h, g, w, *, flat, nbk, name, scale=1.0, rider=None):
    T, D = h.shape
    nb, _, bn = w.shape
    tm = _tile(T, MATMUL_TILE)

    def body(h_ref, g_ref, w_ref, o_ref, n_ref):
        @pl.when(pl.program_id(1) == 0)
        def _():
            xh, _ = _rms(h_ref[...], None)
            n_ref[...] = (xh * g_ref[...]).astype(BF16)

        n = n_ref[...]
        for k in range(nbk):
            r = _dot(n, w_ref[k])
            r = (r if scale == 1.0 else r * scale).astype(BF16)
            if flat:
                o_ref[:, k * bn:(k + 1) * bn] = r
            else:
                o_ref[k] = r

    if flat:
        out_shape = jax.ShapeDtypeStruct((T, nb * bn), BF16)
        out_spec = pl.BlockSpec((tm, nbk * bn), lambda i, j: (i, j))
    else:
        out_shape = jax.ShapeDtypeStruct((nb, T, bn), BF16)
        out_spec = pl.BlockSpec((nbk, tm, bn), lambda i, j: (j, i, 0))
    return _call(
        body, name=name, grid=(T // tm, nb // nbk),
        in_specs=[pl.BlockSpec((tm, D), lambda i, j: (i, 0)),
                  pl.BlockSpec((1, D), lambda i, j: (0, 0)),
                  pl.BlockSpec((nbk, D, bn), lambda i, j: (j, 0, 0))],
        out_specs=[out_spec, pl.BlockSpec((tm, D), lambda i, j: (i, 0))],
        out_shape=[out_shape, jax.ShapeDtypeStruct((T, D), BF16)],
        args=(h, g, w), rider=rider)


def _matmul_nt(dy, w, *, flat, nbk, name, norm=None, out_dtype=BF16, rider=None):
    nb, R, bn = w.shape
    T = dy.shape[0] if flat else dy.shape[1]
    tm = _tile(T, MATMUL_TILE)
    nj = nb // nbk

    def body(*refs):
        if norm is None:
            dy_ref, w_ref, o_ref, acc_ref = refs
        else:
            dy_ref, w_ref, h_ref, g_ref, dres_ref, o_ref, dg_ref, acc_ref = refs
        i, j = pl.program_id(0), pl.program_id(1)

        @pl.when(j == 0)
        def _():
            acc_ref[...] = jnp.zeros_like(acc_ref)

        acc = acc_ref[...]
        for k in range(nbk):
            d = dy_ref[:, k * bn:(k + 1) * bn] if flat else dy_ref[k]
            acc = acc + _dot_nt(d.astype(BF16), w_ref[k])
        acc_ref[...] = acc

        @pl.when(j == nj - 1)
        def _():
            if norm is None:
                o_ref[...] = acc.astype(out_dtype)
            else:
                xh, r = _rms(h_ref[...], None)

                @pl.when(i == 0)
                def _():
                    dg_ref[...] = jnp.zeros_like(dg_ref)

                dg_ref[0:1, :] += jnp.sum(acc * xh, axis=0, keepdims=True)
                dn = acc * g_ref[...]
                o_ref[...] = dres_ref[...] + r * (dn - xh * jnp.mean(dn * xh, axis=-1, keepdims=True))

    if flat:
        dy_spec = pl.BlockSpec((tm, nbk * bn), lambda i, j: (i, j))
    else:
        dy_spec = pl.BlockSpec((nbk, tm, bn), lambda i, j: (j, i, 0))
    w_spec = pl.BlockSpec((nbk, R, bn), lambda i, j: (j, 0, 0))
    row_spec = pl.BlockSpec((tm, R), lambda i, j: (i, 0))
    if norm is None:
        in_specs, args = [dy_spec, w_spec], (dy, w)
        out_specs = row_spec
        out_shape = jax.ShapeDtypeStruct((T, R), out_dtype)
    else:
        in_specs = [dy_spec, w_spec, row_spec, pl.BlockSpec((1, R), lambda i, j: (0, 0)), row_spec]
        args = (dy, w) + tuple(norm)
        out_specs = [row_spec, pl.BlockSpec((8, R), lambda i, j: (0, 0))]
        out_shape = [jax.ShapeDtypeStruct((T, R), F32), jax.ShapeDtypeStruct((8, R), F32)]
    return _call(
        body, name=name, grid=(T // tm, nj), in_specs=in_specs, out_specs=out_specs, out_shape=out_shape,
        scratch_shapes=[pltpu.VMEM((tm, R), F32)], args=args, rider=rider)


def _wgrad_cols(n, dy, *, flat, nb, nbk, name, rider=None):
    T, D = n.shape
    halves = isinstance(dy, tuple)
    bn = dy.shape[1] // nb if flat else (dy[0] if halves else dy).shape[2]
    tt = _tile(T, MATMUL_TILE)
    nt = T // tt
    nj = nb // nbk

    def body(*refs):
        n_ref, dy_refs, (o_ref, acc_ref) = refs[0], refs[1:-2], refs[-2:]
        j, t = pl.program_id(0), pl.program_id(1)

        @pl.when(t == 0)
        def _():
            acc_ref[...] = jnp.zeros_like(acc_ref)

        def accumulate(dy_ref):
            nv = n_ref[...]
            for k in range(nbk):
                d = dy_ref[:, k * bn:(k + 1) * bn] if flat else dy_ref[k]
                acc_ref[k] += _dot_tn(nv, d)

        if halves:
            pl.when(j < nj // 2)(lambda: accumulate(dy_refs[0]))
            pl.when(j >= nj // 2)(lambda: accumulate(dy_refs[1]))
        else:
            accumulate(dy_refs[0])

        @pl.when(t == nt - 1)
        def _():
            o_ref[...] = acc_ref[...]

    if flat:
        dy_specs, dys = [pl.BlockSpec((tt, nbk * bn), lambda j, t: (t, j))], [dy]
    elif halves:
        first = pl.BlockSpec((nbk, tt, bn), lambda j, t: (jnp.minimum(j, nj // 2 - 1),
                                                          jnp.where(j < nj // 2, t, nt - 1), 0))
        second = pl.BlockSpec((nbk, tt, bn), lambda j, t: (jnp.maximum(j - nj // 2, 0),
                                                           jnp.where(j >= nj // 2, t, 0), 0))
        dy_specs, dys = [first, second], list(dy)
    else:
        dy_specs, dys = [pl.BlockSpec((nbk, tt, bn), lambda j, t: (j, t, 0))], [dy]
    return _call(
        body, name=name, grid=(nj, nt),
        in_specs=[pl.BlockSpec((tt, D), lambda j, t: (t, 0))] + dy_specs,
        out_specs=pl.BlockSpec((nbk, D, bn), lambda j, t: (j, 0, 0)),
        out_shape=jax.ShapeDtypeStruct((nb, D, bn), F32),
        scratch_shapes=[pltpu.VMEM((nbk, D, bn), F32)], args=[n] + dys, rider=rider)


def _wgrad_rows(xa, dh, *, flat, tk, name, rider=None):
    T, D = dh.shape
    nk = xa.shape[1] // tk if flat else xa.shape[0]
    tt = _tile(T, MATMUL_TILE)
    nt = T // tt

    def body(x_ref, dh_ref, o_ref, acc_ref):
        t = pl.program_id(1)

        @pl.when(t == 0)
        def _():
            acc_ref[...] = jnp.zeros_like(acc_ref)

        xv = x_ref[...] if flat else x_ref[0]
        acc_ref[...] += _dot_tn(xv, dh_ref[...].astype(BF16))

        @pl.when(t == nt - 1)
        def _():
            o_ref[...] = acc_ref[...]

    x_spec = pl.BlockSpec((tt, tk), lambda j, t: (t, j)) if flat else pl.BlockSpec((1, tt, tk), lambda j, t: (j, t, 0))
    return _call(
        body, name=name, grid=(nk, nt),
        in_specs=[x_spec, pl.BlockSpec((tt, D), lambda j, t: (t, 0))],
        out_specs=pl.BlockSpec((tk, D), lambda j, t: (j, 0)),
        out_shape=jax.ShapeDtypeStruct((nk * tk, D), F32),
        scratch_shapes=[pltpu.VMEM((tk, D), F32)], args=(xa, dh), rider=rider)


def _matmul_residual(xa, w, res, name):
    T, K = xa.shape
    D = w.shape[1]
    tm = _tile(T, MATMUL_TILE)

    def body(x_ref, w_ref, r_ref, o_ref):
        o_ref[...] = r_ref[...] + _dot(x_ref[...], w_ref[...])

    return pl.pallas_call(
        body, name=name, grid=(T // tm,),
        in_specs=[pl.BlockSpec((tm, K), lambda i: (i, 0)), pl.BlockSpec((K, D), lambda i: (0, 0)),
                  pl.BlockSpec((tm, D), lambda i: (i, 0))],
        out_specs=pl.BlockSpec((tm, D), lambda i: (i, 0)),
        out_shape=jax.ShapeDtypeStruct((T, D), F32),
        compiler_params=_params(("arbitrary",)),
    )(xa, w, res)


def _gmlp_gate(z, ws, bst, gv, G, gd):
    D = G * gd
    u = _gelu(z[:, :D].astype(F32))
    v = _gelu(z[:, D:].astype(F32))
    vh, r = _rms(v, None)
    vn = (vh * gv).astype(BF16)
    return u, v, vh, r, vn


def _gmlp_forward(z, ws, bst, gv, w_out, x, *, name, rider=None):
    T, D2 = z.shape
    D = D2 // 2
    G = ws.shape[0]
    gd = D // G
    tb = _tile(T, 256)
    nblk = tb // GMLP_BLOCK

    def body(z_ref, ws_ref, b_ref, gv_ref, wo_ref, x_ref, gated_ref, h_ref):
        u, _, _, _, vn = _gmlp_gate(z_ref[...], None, None, gv_ref[...], G, gd)
        for n in range(nblk):
            rows = slice(n * GMLP_BLOCK, (n + 1) * GMLP_BLOCK)
            for gi in range(G):
                cols = slice(gi * gd, (gi + 1) * gd)
                s = _dot(ws_ref[gi], vn[rows, cols]) + b_ref[:, gi:gi + 1]
                gated_ref[rows, cols] = (u[rows, cols] * s).astype(BF16)
        h_ref[...] = x_ref[...] + _dot(gated_ref[...], wo_ref[...])

    return _call(
        body, name=name, grid=(T // tb,),
        in_specs=[pl.BlockSpec((tb, D2), lambda i: (i, 0)), pl.BlockSpec(ws.shape, lambda i: (0, 0, 0)),
                  pl.BlockSpec(bst.shape, lambda i: (0, 0)), pl.BlockSpec((1, D), lambda i: (0, 0)),
                  pl.BlockSpec((D, D), lambda i: (0, 0)), pl.BlockSpec((tb, D), lambda i: (i, 0))],
        out_specs=[pl.BlockSpec((tb, D), lambda i: (i, 0)), pl.BlockSpec((tb, D), lambda i: (i, 0))],
        out_shape=[jax.ShapeDtypeStruct((T, D), BF16), jax.ShapeDtypeStruct((T, D), F32)],
        args=(z, ws, bst, gv, w_out, x), rider=rider)


def _gmlp_backward(z, dgated, ws, bst, gv, mask, *, name, rider=None):
    T, D2 = z.shape
    D = D2 // 2
    G = ws.shape[0]
    gd = D // G
    tb = _tile(T, 256)
    nblk = tb // GMLP_BLOCK

    def body(z_ref, dg_ref, ws_ref, b_ref, gv_ref, mask_ref, dz_ref, dws_ref, db_ref, dgv_ref, dvn_ref):
        @pl.when(pl.program_id(0) == 0)
        def _():
            dws_ref[...] = jnp.zeros_like(dws_ref)
            db_ref[...] = jnp.zeros_like(db_ref)
            dgv_ref[...] = jnp.zeros_like(dgv_ref)

        zf = z_ref[...]
        u, v, vh, r, vn = _gmlp_gate(zf, None, None, gv_ref[...], G, gd)
        dg = dg_ref[...].astype(F32)
        for n in range(nblk):
            rows = slice(n * GMLP_BLOCK, (n + 1) * GMLP_BLOCK)
            for gi in range(G):
                cols = slice(gi * gd, (gi + 1) * gd)
                vblk = vn[rows, cols]
                s = _dot(ws_ref[gi], vblk) + b_ref[:, gi:gi + 1]
                dgb = dg[rows, cols]
                ds = dgb * u[rows, cols]
                dsb = ds.astype(BF16)
                dz_ref[rows, cols] = (dgb * s * _gelu_grad(zf[rows, cols].astype(F32))).astype(BF16)
                dvn_ref[rows, cols] = _dot_tn(ws_ref[gi], dsb)
                dws_ref[gi] += _dot_nt(dsb, vblk) * mask_ref[...]
                db_ref[:, gi:gi + 1] += jnp.sum(ds, axis=1, keepdims=True)
        dvn = dvn_ref[...]
        dgv_ref[0:1, :] += jnp.sum(dvn * vh, axis=0, keepdims=True)
        dn = dvn * gv_ref[...]
        dv = r * (dn - vh * jnp.mean(dn * vh, axis=-1, keepdims=True))
        dz_ref[:, D:] = (dv * _gelu_grad(zf[:, D:].astype(F32))).astype(BF16)

    return _call(
        body, name=name, grid=(T // tb,),
        in_specs=[pl.BlockSpec((tb, D2), lambda i: (i, 0)), pl.BlockSpec((tb, D), lambda i: (i, 0)),
                  pl.BlockSpec(ws.shape, lambda i: (0, 0, 0)), pl.BlockSpec(bst.shape, lambda i: (0, 0)),
                  pl.BlockSpec((1, D), lambda i: (0, 0)), pl.BlockSpec(mask.shape, lambda i: (0, 0))],
        out_specs=[pl.BlockSpec((tb, D2), lambda i: (i, 0)), pl.BlockSpec(ws.shape, lambda i: (0, 0, 0)),
                   pl.BlockSpec(bst.shape, lambda i: (0, 0)), pl.BlockSpec((8, D), lambda i: (0, 0))],
        out_shape=[jax.ShapeDtypeStruct((T, D2), BF16), jax.ShapeDtypeStruct(ws.shape, F32),
                   jax.ShapeDtypeStruct(bst.shape, F32), jax.ShapeDtypeStruct((8, D), F32)],
        scratch_shapes=[pltpu.VMEM((tb, D), F32)], args=(z, dgated, ws, bst, gv, mask), rider=rider)


def _shift_rows(x, k):
    return pltpu.roll(x, k % x.shape[0], axis=0)


def _conv3(ext, cw):
    return (cw[0:1] * _shift_rows(ext, 2)[8:] + cw[1:2] * _shift_rows(ext, 1)[8:] + cw[2:3] * ext[8:])


def _ffn_forward(a, cw, cb, wd, h, seq, *, name, rider=None):
    _, T, F = a.shape
    D = h.shape[1]
    tm = _tile(seq, TOKEN_TILE)
    hb = tm // 16

    def body(a_ref, ap_ref, cw_ref, cb_ref, wd_ref, h_ref, act_ref, c_ref, o_ref, acc_ref):
        i, j = pl.program_id(0), pl.program_id(1)
        keep = ((i * tm) % seq != 0).astype(F32)

        def conv(b):
            ext = jnp.concatenate([ap_ref[b, 8:16].astype(F32) * keep, a_ref[b].astype(F32)], axis=0)
            return _conv3(ext, cw_ref[b]) + cb_ref[b]

        up, gate = conv(j), conv(j + 4)
        c_ref[j] = up.astype(BF16)
        c_ref[j + 4] = gate.astype(BF16)
        act = (gate * _sigmoid(gate) * up).astype(BF16)
        act_ref[0] = act

        @pl.when(j == 0)
        def _():
            acc_ref[...] = h_ref[...]

        acc_ref[...] += _dot(act, wd_ref[0])

        @pl.when(j == 3)
        def _():
            o_ref[...] = acc_ref[...]

    return _call(
        body, name=name, grid=(T // tm, 4),
        in_specs=[pl.BlockSpec((8, tm, F), lambda i, j: (0, i, 0)),
                  pl.BlockSpec((8, 16, F), lambda i, j: (0, jnp.maximum(i * hb - 1, 0), 0)),
                  pl.BlockSpec((8, 3, F), lambda i, j: (0, 0, 0)), pl.BlockSpec((8, 1, F), lambda i, j: (0, 0, 0)),
                  pl.BlockSpec((1, F, D), lambda i, j: (j, 0, 0)), pl.BlockSpec((tm, D), lambda i, j: (i, 0))],
        out_specs=[pl.BlockSpec((1, tm, F), lambda i, j: (j, i, 0)), pl.BlockSpec((8, tm, F), lambda i, j: (0, i, 0)),
                   pl.BlockSpec((tm, D), lambda i, j: (i, 0))],
        out_shape=[jax.ShapeDtypeStruct((4, T, F), BF16), jax.ShapeDtypeStruct((8, T, F), BF16),
                   jax.ShapeDtypeStruct((T, D), F32)],
        scratch_shapes=[pltpu.VMEM((tm, D), F32)], args=(a, a, cw, cb, wd, h), rider=rider)


def _ffn_backward(dh, c, a, cw, wd, seq, *, name, rider=None):
    _, T, F = a.shape
    D = dh.shape[1]
    tm = _tile(seq, TOKEN_TILE)
    hb = tm // 16
    nt = T // tm

    def body(dh_ref, dhn_ref, cu_ref, cg_ref, cun_ref, cgn_ref, au_ref, ag_ref, cw_ref, wd_ref, da_ref, st_ref):
        i, j = pl.program_id(0), pl.program_id(1)
        keep_next = (((i + 1) * tm) % seq != 0).astype(F32)

        @pl.when((i == 0) & (j == 0))
        def _():
            st_ref[...] = jnp.zeros_like(st_ref)

        dhe = jnp.concatenate([dh_ref[...], dhn_ref[...] * keep_next], axis=0).astype(BF16)
        dact = _dot_nt(dhe, wd_ref[0])
        up = jnp.concatenate([cu_ref[0].astype(F32), cun_ref[0, 0:8].astype(F32)], axis=0)
        gate = jnp.concatenate([cg_ref[0].astype(F32), cgn_ref[0, 0:8].astype(F32)], axis=0)
        sg = _sigmoid(gate)
        gs = gate * sg
        d_up = dact * gs
        d_gate = dact * up * (sg + gs * (1.0 - sg))

        def finish(b, a_ref, dc):
            w = cw_ref[b]
            dm, u1, u2 = dc[:tm], _shift_rows(dc, -1)[:tm], _shift_rows(dc, -2)[:tm]
            da_ref[b] = (w[2:3] * dm + w[1:2] * u1 + w[0:1] * u2).astype(BF16)
            av = a_ref[0].astype(F32)
            st_ref[b, 0:1, :] += jnp.sum(u2 * av, axis=0, keepdims=True)
            st_ref[b, 1:2, :] += jnp.sum(u1 * av, axis=0, keepdims=True)
            st_ref[b, 2:3, :] += jnp.sum(dm * av, axis=0, keepdims=True)
            st_ref[b, 3:4, :] += jnp.sum(dm, axis=0, keepdims=True)

        finish(j, au_ref, d_up)
        finish(j + 4, ag_ref, d_gate)

    nxt = lambda i: jnp.minimum((i + 1) * hb, T // 16 - 1)
    return _call(
        body, name=name, grid=(nt, 4),
        in_specs=[pl.BlockSpec((tm, D), lambda i, j: (i, 0)),
                  pl.BlockSpec((8, D), lambda i, j: (jnp.minimum((i + 1) * (tm // 8), T // 8 - 1), 0)),
                  pl.BlockSpec((1, tm, F), lambda i, j: (j, i, 0)), pl.BlockSpec((1, tm, F), lambda i, j: (j + 4, i, 0)),
                  pl.BlockSpec((1, 16, F), lambda i, j: (j, nxt(i), 0)),
                  pl.BlockSpec((1, 16, F), lambda i, j: (j + 4, nxt(i), 0)),
                  pl.BlockSpec((1, tm, F), lambda i, j: (j, i, 0)), pl.BlockSpec((1, tm, F), lambda i, j: (j + 4, i, 0)),
                  pl.BlockSpec((8, 3, F), lambda i, j: (0, 0, 0)),
                  pl.BlockSpec((1, F, D), lambda i, j: (j, 0, 0))],
        out_specs=[pl.BlockSpec((8, tm, F), lambda i, j: (0, i, 0)), pl.BlockSpec((8, 8, F), lambda i, j: (0, 0, 0))],
        out_shape=[jax.ShapeDtypeStruct((8, T, F), BF16), jax.ShapeDtypeStruct((8, 8, F), F32)],
        args=(dh, dh, c, c, c, c, a, a, cw, wd), rider=rider)


def _ffn_fused_forward(h, g, w_in, cw, cb, wd, seq, *, name, rider=None):
    T, D = h.shape
    F = w_in.shape[2]
    tm = _tile(seq, TOKEN_TILE)
    bps = 1
    nj = 4 // bps

    def body(h_ref, g_ref, wu_ref, wg_ref, cw_ref, cb_ref, wd_ref,
             au_ref, ag_ref, cu_ref, cg_ref, act_ref, n_ref, o_ref, acc_ref, carry_ref, *work_refs):
        eu_refs, eg_refs, stage_refs = work_refs[:bps], work_refs[bps:2 * bps], work_refs[2 * bps:]
        i, j = pl.program_id(0), pl.program_id(1)
        keep = ((i * tm) % seq != 0).astype(F32)

        @pl.when((i == 0) & (j == 0))
        def _():
            carry_ref[...] = jnp.zeros_like(carry_ref)

        @pl.when(j == 0)
        def _():
            xh, _ = _rms(h_ref[...], None)
            n_ref[...] = (xh * g_ref[...]).astype(BF16)
            acc_ref[...] = h_ref[...]

        n = n_ref[...]

        def project(b, k, w_ref, a_ref, ext_ref):
            a = _dot(n, w_ref[k]).astype(BF16)
            a_ref[k] = a
            ext_ref[0:8, :] = carry_ref[b] * keep
            ext_ref[8:, :] = a.astype(F32)
            carry_ref[b] = ext_ref[tm:tm + 8, :]

        def conv(b, ext_ref, r):
            x, w = ext_ref[r:r + ROWS + 8, :], cw_ref[b]
            return (w[0:1] * _shift_rows(x, 2) + w[1:2] * _shift_rows(x, 1) + w[2:3] * x)[8:] + cb_ref[b]

        for k in range(bps):
            project(j * bps + k, k, wu_ref, au_ref, eu_refs[k])
            project(j * bps + k + 4, k, wg_ref, ag_ref, eg_refs[k])
        outs = []
        for k in range(bps):
            for r in range(0, tm, ROWS):
                up, gate = conv(j * bps + k, eu_refs[k], r), conv(j * bps + k + 4, eg_refs[k], r)
                cu_ref[k, r:r + ROWS, :] = up.astype(BF16)
                cg_ref[k, r:r + ROWS, :] = gate.astype(BF16)
                stage_refs[k][r:r + ROWS, :] = (gate * _sigmoid(gate) * up).astype(BF16)
            act = stage_refs[k][...]
            act_ref[k] = act
            outs.append(_dot(act, wd_ref[k]))
        acc_ref[...] += functools.reduce(lambda p, q: p + q, outs)

        @pl.when(j == nj - 1)
        def _():
            o_ref[...] = acc_ref[...]

    blk = pl.BlockSpec((bps, tm, F), lambda i, j: (j, i, 0))
    row = pl.BlockSpec((tm, D), lambda i, j: (i, 0))
    half = jax.ShapeDtypeStruct((4, T, F), BF16)
    work = [pltpu.VMEM((tm + 8, F), F32)] * (2 * bps) + [pltpu.VMEM((tm, F), BF16)] * bps
    return _call(
        body, name=name, grid=(T // tm, nj),
        in_specs=[row, pl.BlockSpec((1, D), lambda i, j: (0, 0)),
                  pl.BlockSpec((bps, D, F), lambda i, j: (j, 0, 0)), pl.BlockSpec((bps, D, F), lambda i, j: (j + nj, 0, 0)),
                  pl.BlockSpec((8, 3, F), lambda i, j: (0, 0, 0)), pl.BlockSpec((8, 1, F), lambda i, j: (0, 0, 0)),
                  pl.BlockSpec((bps, F, D), lambda i, j: (j, 0, 0))],
        out_specs=[blk, blk, blk, blk, blk, row, row],
        out_shape=[half, half, half, half, half, jax.ShapeDtypeStruct((T, D), BF16), jax.ShapeDtypeStruct((T, D), F32)],
        scratch_shapes=[pltpu.VMEM((tm, D), F32), pltpu.VMEM((8, 8, F), F32)] + work,
        args=(h, g, w_in, w_in, cw, cb, wd), rider=rider)


def _ffn_fused_backward(dh, cu, cg, au, ag, cw, wd, w_in, h, g, seq, *, name, rider=None):
    T, D = dh.shape
    F = wd.shape[1]
    tm = _tile(seq, TOKEN_TILE // 2)
    hb = tm // 16
    nt = T // tm
    bps = FFN_BLOCKS_PER_STEP
    nj = 4 // bps

    def body(dh_ref, dhn_ref, cu_ref, cg_ref, cun_ref, cgn_ref, au_ref, ag_ref, cw_ref, wd_ref, wu_ref, wg_ref,
             h_ref, g_ref, dau_ref, dag_ref, st_ref, o_ref, dg_ref, acc_ref, dact_ref, du_ref, dgt_ref):
        j, i = pl.program_id(0), pl.program_id(1)
        keep_next = (((i + 1) * tm) % seq != 0).astype(F32)
        tile_rows = pl.ds(pl.multiple_of(i * tm, tm), tm)

        @pl.when((i == 0) & (j == 0))
        def _():
            st_ref[...] = jnp.zeros_like(st_ref)
            dg_ref[...] = jnp.zeros_like(dg_ref)

        dhe = jnp.concatenate([dh_ref[...], dhn_ref[...] * keep_next], axis=0).astype(BF16)
        for k in range(bps):
            dact_ref[k] = _dot_nt(dhe, wd_ref[k])

        def conv_grads(k):
            for r in range(0, tm + 8, ROWS):
                if r < tm:
                    rows = slice(r, r + ROWS)
                    up, gate = cu_ref[k, rows, :].astype(F32), cg_ref[k, rows, :].astype(F32)
                else:
                    rows = slice(tm, tm + 8)
                    up, gate = cun_ref[k, 0:8, :].astype(F32), cgn_ref[k, 0:8, :].astype(F32)
                dact = dact_ref[k, rows, :]
                sg = _sigmoid(gate)
                gs = gate * sg
                du_ref[k, rows, :] = dact * gs
                dgt_ref[k, rows, :] = dact * up * (sg + gs * (1.0 - sg))

        def finish(b, k, a_ref, w_ref, dc_ref, da_ref):
            w = cw_ref[b]
            sums = [jnp.zeros((8, F), F32) for _ in range(4)]
            fold = lambda t: jnp.sum(t.reshape(ROWS // 8, 8, F), axis=0)
            for r in range(0, tm, ROWS):
                dc = dc_ref[k, r:r + ROWS + 8, :]
                dm, u1, u2 = dc[:ROWS], _shift_rows(dc, -1)[:ROWS], _shift_rows(dc, -2)[:ROWS]
                da_ref[k, r:r + ROWS, :] = (w[2:3] * dm + w[1:2] * u1 + w[0:1] * u2).astype(BF16)
                av = a_ref[k, r:r + ROWS, :].astype(F32)
                for s, t in enumerate((u2 * av, u1 * av, dm * av, dm)):
                    sums[s] = sums[s] + fold(t)
            for s in range(4):
                st_ref[b, s:s + 1, :] += jnp.sum(sums[s], axis=0, keepdims=True)
            return _dot_nt(da_ref[k], w_ref[k])

        dn_parts = []
        for k in range(bps):
            conv_grads(k)
            dn_parts.append(finish(j * bps + k, k, au_ref, wu_ref, du_ref, dau_ref))
            dn_parts.append(finish(j * bps + k + 4, k, ag_ref, wg_ref, dgt_ref, dag_ref))
        dn_part = functools.reduce(lambda p, q: p + q, dn_parts)

        @pl.when(j == 0)
        def _():
            acc_ref[tile_rows, :] = dn_part

        @pl.when(j > 0)
        def _():
            acc_ref[tile_rows, :] += dn_part

        @pl.when(j == nj - 1)
        def _():
            acc = acc_ref[tile_rows, :]
            xh, r = _rms(h_ref[...], None)
            dg_ref[0:1, :] += jnp.sum(acc * xh, axis=0, keepdims=True)
            dn = acc * g_ref[...]
            o_ref[...] = dh_ref[...] + r * (dn - xh * jnp.mean(dn * xh, axis=-1, keepdims=True))

    last = lambda j, i: jnp.where(j == nj - 1, i, 0)
    nxt = lambda i: jnp.minimum((i + 1) * hb, T // 16 - 1)
    blk = pl.BlockSpec((bps, tm, F), lambda j, i: (j, i, 0))
    halo = pl.BlockSpec((bps, 16, F), lambda j, i: (j, nxt(i), 0))
    row = pl.BlockSpec((tm, D), lambda j, i: (i, 0))
    work = pltpu.VMEM((bps, tm + 8, F), F32)
    half = jax.ShapeDtypeStruct((4, T, F), BF16)
    return _call(
        body, name=name, grid=(nj, nt),
        in_specs=[row, pl.BlockSpec((8, D), lambda j, i: (jnp.minimum((i + 1) * (tm // 8), T // 8 - 1), 0)),
                  blk, blk, halo, halo, blk, blk,
                  pl.BlockSpec((8, 3, F), lambda j, i: (0, 0, 0)), pl.BlockSpec((bps, F, D), lambda j, i: (j, 0, 0)),
                  pl.BlockSpec((bps, D, F), lambda j, i: (j, 0, 0)),
                  pl.BlockSpec((bps, D, F), lambda j, i: (j + nj, 0, 0)),
                  pl.BlockSpec((tm, D), lambda j, i: (last(j, i), 0)), pl.BlockSpec((1, D), lambda j, i: (0, 0))],
        out_specs=[blk, blk, pl.BlockSpec((8, 8, F), lambda j, i: (0, 0, 0)),
                   pl.BlockSpec((tm, D), lambda j, i: (last(j, i), 0)), pl.BlockSpec((8, D), lambda j, i: (0, 0))],
        out_shape=[half, half, jax.ShapeDtypeStruct((8, 8, F), F32),
                   jax.ShapeDtypeStruct((T, D), F32), jax.ShapeDtypeStruct((8, D), F32)],
        scratch_shapes=[pltpu.VMEM((T, D), F32), work, work, work],
        args=(dh, dh, cu, cg, cu, cg, au, ag, cw, wd, w_in, w_in, h, g), rider=rider)


def _rel_onehot():
    r = lax.broadcasted_iota(jnp.int32, (REL_PAD, SKEW), 0)
    n = lax.broadcasted_iota(jnp.int32, (REL_PAD, SKEW), 1)
    off = jnp.where(n >= WIN, n - SKEW, n)
    idx = jnp.minimum(PAD - off, REL_CLIP) + REL_CLIP
    return (r == idx).astype(BF16)


def _skew(x, sign):
    row = lax.broadcasted_iota(jnp.int32, x.shape, 0)
    for b in range(7):
        x = jnp.where((row >> b) & 1 == 1, pltpu.roll(x, (sign * (1 << b)) % SKEW, axis=1), x)
    return x


def _bias_build(rel, name, rider=None):
    H = rel.shape[0]

    def body(rel_ref, o_ref):
        oh = _rel_onehot()
        hi, mid, lo = _split3(rel_ref[...])
        base = _dot(hi, oh) + _dot(mid, oh) + _dot(lo, oh)
        mine = lax.broadcasted_iota(jnp.int32, (H, 1), 0) == pl.program_id(0)
        row = jnp.sum(jnp.where(mine, base, 0.0), axis=0, keepdims=True)
        q = lax.broadcasted_iota(jnp.int32, (Q_TILE, WIN), 0)
        k = lax.broadcasted_iota(jnp.int32, (Q_TILE, WIN), 1)
        ok = ((q < CHUNK) & (k < WIN - CHUNK)) | ((q >= CHUNK) & (k >= CHUNK))
        t = _skew(jnp.broadcast_to(row, (Q_TILE, SKEW)), 1)
        o_ref[0] = jnp.where(ok, t[:, :WIN], NEG_INF)

    return _call(
        body, name=name, grid=(H,), in_specs=[pl.BlockSpec((H, REL_PAD), lambda h: (0, 0))],
        out_specs=pl.BlockSpec((1, Q_TILE, WIN), lambda h: (h, 0, 0)),
        out_shape=jax.ShapeDtypeStruct((H, Q_TILE, WIN), F32), args=(rel,), rider=rider)


def _bias_reduce(dbias, name):
    H = dbias.shape[0]

    def body(d_ref, o_ref, e_ref):
        oh = _rel_onehot()
        for hd in range(H):
            x = jnp.concatenate([d_ref[hd], jnp.zeros((Q_TILE, SKEW - WIN), F32)], axis=1)
            e_ref[hd:hd + 1, :] = jnp.sum(_skew(x, -1), axis=0, keepdims=True)
        hi, mid, lo = _split3(e_ref[...])
        o_ref[...] = _dot_nt(hi, oh) + _dot_nt(mid, oh) + _dot_nt(lo, oh)

    return pl.pallas_call(
        body, name=name, out_shape=jax.ShapeDtypeStruct((H, REL_PAD), F32),
        in_specs=[pl.BlockSpec(memory_space=pltpu.VMEM)], out_specs=pl.BlockSpec(memory_space=pltpu.VMEM),
        scratch_shapes=[pltpu.VMEM((H, SKEW), F32)],
        compiler_params=_params(),
    )(dbias)


def _pair_stack(xp, even):
    z = jnp.zeros_like(xp)
    return jnp.concatenate([jnp.where(even, xp, z), jnp.where(even, z, xp)], axis=0)


def _pair_merge(y, even):
    return jnp.where(even, y[:Q_TILE], y[Q_TILE:])


def _strip_probs(s_ref, b_ref, pp, r, valid, base=0):
    hb, hr = divmod(r, Q_TILE)
    s = s_ref[base + pp, r:r + STRIP, :] + b_ref[2 * pp + hb, hr:hr + STRIP, :]
    s = jnp.where(valid, s, NEG_INF)
    e = jnp.exp(s - jnp.max(s, axis=-1, keepdims=True))
    return e * (1.0 / jnp.sum(e, axis=-1, keepdims=True))


def _fill_padded(dst_ref, src_ref):
    dst_ref[0:PAD, :] = jnp.zeros((PAD, dst_ref.shape[1]), dst_ref.dtype)
    dst_ref[PAD:, :] = src_ref[...]


def _attn_specs(B, S, D, lanes):
    nt = S // (TILES_PER_STEP * Q_TILE)
    q_spec = pl.BlockSpec((TILES_PER_STEP * Q_TILE, lanes), lambda g, b, i: (b * nt + i, g))
    k_spec = pl.BlockSpec((S, lanes), lambda g, b, i: (b, g))
    v_spec = pl.BlockSpec((S, lanes), lambda g, b, i: (b, D // lanes + g))
    bias_spec = pl.BlockSpec((lanes // HEAD_DIM, Q_TILE, WIN), lambda g, b, i: (g, 0, 0))
    return nt, q_spec, k_spec, v_spec, bias_spec


def _attn_forward(q, kv, bias, S, *, name, rider=None):
    T, D = q.shape
    B = T // S
    lanes = min(FWD_HEADS_PER_STEP * HEAD_DIM, D)
    nt, q_spec, k_spec, v_spec, bias_spec = _attn_specs(B, S, D, lanes)

    npairs = lanes // (2 * HEAD_DIM)

    def body(q_ref, k_ref, v_ref, b_ref, o_ref, kp_ref, vp_ref, s_ref, p_ref):
        i = pl.program_id(2)

        @pl.when(i == 0)
        def _():
            _fill_padded(kp_ref, k_ref)
            _fill_padded(vp_ref, v_ref)

        even = lax.broadcasted_iota(jnp.int32, (1, 2 * HEAD_DIM), 1) < HEAD_DIM
        pair_cols = [slice(pp * 2 * HEAD_DIM, (pp + 1) * 2 * HEAD_DIM) for pp in range(npairs)]
        for t in range(TILES_PER_STEP):
            tile = i * TILES_PER_STEP + t
            start = pl.multiple_of(tile * Q_TILE, Q_TILE)
            rows = slice(t * Q_TILE, (t + 1) * Q_TILE)
            valid = lax.broadcasted_iota(jnp.int32, (STRIP, WIN), 1) >= PAD - tile * Q_TILE
            for pp, cols in enumerate(pair_cols):
                s_ref[t * npairs + pp] = _dot_nt(_pair_stack(q_ref[rows, cols], even), kp_ref[pl.ds(start, WIN), cols])
            for pp in range(npairs):
                for r in range(0, 2 * Q_TILE, STRIP):
                    p = _strip_probs(s_ref, b_ref, pp, r, valid, base=t * npairs)
                    p_ref[t * npairs + pp, r:r + STRIP, :] = p.astype(BF16)
            for pp, cols in enumerate(pair_cols):
                o = _dot(p_ref[t * npairs + pp], vp_ref[pl.ds(start, WIN), cols])
                o_ref[rows, cols] = _pair_merge(o, even).astype(BF16)

    nbuf = TILES_PER_STEP * npairs
    return _call(
        body, name=name, grid=(D // lanes, B, nt),
        in_specs=[q_spec, k_spec, v_spec, bias_spec], out_specs=q_spec,
        out_shape=jax.ShapeDtypeStruct((T, D), BF16),
        scratch_shapes=[pltpu.VMEM((S + PAD, lanes), BF16), pltpu.VMEM((S + PAD, lanes), BF16),
                        pltpu.VMEM((nbuf, 2 * Q_TILE, WIN), F32), pltpu.VMEM((nbuf, 2 * Q_TILE, WIN), BF16)],
        args=(q, kv, kv, bias), rider=rider)


def _attn_backward(q, kv, bias, do, S, *, name, rider=None):
    T, D = q.shape
    B = T // S
    H = D // HEAD_DIM
    lanes = min(BWD_HEADS_PER_STEP * HEAD_DIM, D)
    nt, q_spec, k_spec, v_spec, bias_spec = _attn_specs(B, S, D, lanes)
    scale = HEAD_DIM ** -0.5

    npairs = lanes // (2 * HEAD_DIM)

    def body(q_ref, k_ref, v_ref, b_ref, do_ref, dq_ref, dk_ref, dv_ref, db_ref, kp_ref, vp_ref, dka_ref, dva_ref,
             s_ref, dp_ref, p_ref, ds_ref):
        b, i = pl.program_id(1), pl.program_id(2)

        @pl.when((b == 0) & (i == 0))
        def _():
            db_ref[...] = jnp.zeros_like(db_ref)

        @pl.when(i == 0)
        def _():
            _fill_padded(kp_ref, k_ref)
            _fill_padded(vp_ref, v_ref)
            dka_ref[...] = jnp.zeros_like(dka_ref)
            dva_ref[...] = jnp.zeros_like(dva_ref)

        even = lax.broadcasted_iota(jnp.int32, (1, 2 * HEAD_DIM), 1) < HEAD_DIM
        pair_cols = [slice(pp * 2 * HEAD_DIM, (pp + 1) * 2 * HEAD_DIM) for pp in range(npairs)]
        for t in range(TILES_PER_STEP):
            tile = i * TILES_PER_STEP + t
            start = pl.multiple_of(tile * Q_TILE, Q_TILE)
            rows = slice(t * Q_TILE, (t + 1) * Q_TILE)
            valid = lax.broadcasted_iota(jnp.int32, (STRIP, WIN), 1) >= PAD - tile * Q_TILE
            base = t * npairs
            for pp, cols in enumerate(pair_cols):
                s_ref[base + pp] = _dot_nt(_pair_stack(q_ref[rows, cols], even), kp_ref[pl.ds(start, WIN), cols])
                dp_ref[base + pp] = _dot_nt(_pair_stack(do_ref[rows, cols], even), vp_ref[pl.ds(start, WIN), cols])
            for pp in range(npairs):
                for r in range(0, 2 * Q_TILE, STRIP):
                    hb, hr = divmod(r, Q_TILE)
                    p = _strip_probs(s_ref, b_ref, pp, r, valid, base=base)
                    dp = dp_ref[base + pp, r:r + STRIP, :]
                    ds = p * (dp - jnp.sum(p * dp, axis=-1, keepdims=True))
                    db_ref[2 * pp + hb, hr:hr + STRIP, :] += ds
                    p_ref[base + pp, r:r + STRIP, :] = p.astype(BF16)
                    ds_ref[base + pp, r:r + STRIP, :] = ds.astype(BF16)
            for pp, cols in enumerate(pair_cols):
                dsb = ds_ref[base + pp]
                dq = _pair_merge(_dot(dsb, kp_ref[pl.ds(start, WIN), cols]), even) * scale
                dq_ref[rows, cols] = dq.astype(BF16)
                dka_ref[pl.ds(start, WIN), cols] += _dot_tn(dsb, _pair_stack(q_ref[rows, cols], even))
                dva_ref[pl.ds(start, WIN), cols] += _dot_tn(p_ref[base + pp], _pair_stack(do_ref[rows, cols], even))

        @pl.when(i == nt - 1)
        def _():
            dk_ref[...] = dka_ref[PAD:, :].astype(BF16)
            dv_ref[...] = dva_ref[PAD:, :].astype(BF16)

    dkv_shape = jax.ShapeDtypeStruct((T, D), BF16)
    nbuf = TILES_PER_STEP * npairs
    return _call(
        body, name=name, grid=(D // lanes, B, nt),
        in_specs=[q_spec, k_spec, v_spec, bias_spec, q_spec],
        out_specs=[q_spec, k_spec, k_spec, bias_spec],
        out_shape=[jax.ShapeDtypeStruct((T, D), BF16), dkv_shape, dkv_shape,
                   jax.ShapeDtypeStruct((H, Q_TILE, WIN), F32)],
        scratch_shapes=[pltpu.VMEM((S + PAD, lanes), BF16), pltpu.VMEM((S + PAD, lanes), BF16),
                        pltpu.VMEM((S + PAD, lanes), F32), pltpu.VMEM((S + PAD, lanes), F32),
                        pltpu.VMEM((nbuf, 2 * Q_TILE, WIN), F32), pltpu.VMEM((nbuf, 2 * Q_TILE, WIN), F32),
                        pltpu.VMEM((nbuf, 2 * Q_TILE, WIN), BF16), pltpu.VMEM((nbuf, 2 * Q_TILE, WIN), BF16)],
        args=(q, kv, kv, bias, do), rider=rider)


def _loss_head(h, g, target, name):
    T, D = h.shape
    tm = _tile(T, MATMUL_TILE)

    def body(h_ref, g_ref, t_ref, dh_ref, st_ref):
        @pl.when(pl.program_id(0) == 0)
        def _():
            st_ref[...] = jnp.zeros_like(st_ref)

        xh, r = _rms(h_ref[...], None)
        err = xh * g_ref[...] - t_ref[...]
        st_ref[1:2, :] += 0.5 * jnp.sum(jnp.mean(err * err, axis=-1, keepdims=True), axis=0, keepdims=True)
        dy = err * (1.0 / D)
        st_ref[0:1, :] += jnp.sum(dy * xh, axis=0, keepdims=True)
        dn = dy * g_ref[...]
        dh_ref[...] = r * (dn - xh * jnp.mean(dn * xh, axis=-1, keepdims=True))

    row = pl.BlockSpec((tm, D), lambda i: (i, 0))
    return pl.pallas_call(
        body, name=name, grid=(T // tm,),
        in_specs=[row, pl.BlockSpec((1, D), lambda i: (0, 0)), row],
        out_specs=[row, pl.BlockSpec((8, D), lambda i: (0, 0))],
        out_shape=[jax.ShapeDtypeStruct((T, D), F32), jax.ShapeDtypeStruct((8, D), F32)],
        compiler_params=_params(("arbitrary",)),
    )(h, g, target)


def _sum_devices(arrs, name):
    n = len(arrs)

    def body(*refs):
        for a in range(n):
            s = refs[a][0].astype(F32)
            for k in range(1, N_DEV):
                s = s + refs[a][k].astype(F32)
            refs[n + a][...] = s

    vm = pl.BlockSpec(memory_space=pltpu.VMEM)
    return pl.pallas_call(
        body, name=name, out_shape=[jax.ShapeDtypeStruct(a.shape[1:], F32) for a in arrs],
        in_specs=[vm] * n, out_specs=[vm] * n, compiler_params=_params(),
    )(*arrs)


def _adamw_math(w, g, m, v):
    m = ADAM_B1 * m + (1.0 - ADAM_B1) * g
    v = ADAM_B2 * v + (1.0 - ADAM_B2) * (g * g)
    m_hat = m / (1.0 - ADAM_B1 ** ADAM_STEP)
    v_hat = v / (1.0 - ADAM_B2 ** ADAM_STEP)
    delta = -ADAM_LR * (m_hat / (jnp.sqrt(v_hat) + ADAM_EPS) + ADAM_WD * w)
    return delta, m, v


def _adamw_small(items, name):
    n = len(items)

    def body(*refs):
        for a in range(n):
            w, m, v, g = (refs[4 * a + k][...] for k in range(4))
            d, m, v = _adamw_math(w, g, m, v)
            refs[4 * n + 3 * a][...] = d
            refs[4 * n + 3 * a + 1][...] = m
            refs[4 * n + 3 * a + 2][...] = v

    vm = pl.BlockSpec(memory_space=pltpu.VMEM)
    flat = [t for it in items for t in it]
    outs = pl.pallas_call(
        body, name=name,
        out_shape=[jax.ShapeDtypeStruct(it[0].shape, F32) for it in items for _ in range(3)],
        in_specs=[vm] * (4 * n), out_specs=[vm] * (3 * n), compiler_params=_params(),
    )(*flat)
    return [tuple(outs[3 * a:3 * a + 3]) for a in range(n)]


def _adamw_big(w, m, v, owns, landeds, name, rider=None):
    L, R, C = w.shape
    tr = _tile(R, 512)
    nr = R // tr
    counts = [len(ls) for ls in landeds]

    def body(*refs):
        w_ref, m_ref, v_ref = refs[:3]
        g_ref, d_ref, mo_ref, vo_ref = refs[-4:]
        layer = pl.program_id(0)
        at = 3
        for j in range(L):
            own_ref, l_refs = refs[at], refs[at + 1:at + 1 + counts[j]]
            at += 1 + counts[j]

            @pl.when(layer == j)
            def _(own_ref=own_ref, l_refs=l_refs):
                g = own_ref[...]
                for l_ref in l_refs:
                    for k in range(l_ref.shape[0]):
                        g = g + l_ref[k].astype(F32)
                d, mn, vn = _adamw_math(w_ref[0], g, m_ref[0], v_ref[0])
                g_ref[0] = g
                d_ref[0] = d
                mo_ref[0] = mn
                vo_ref[0] = vn

    def pinned(j):
        return lambda l, i: jnp.where(l == j, i, jnp.where(l < j, 0, nr - 1))

    row = pl.BlockSpec((1, tr, C), lambda l, i: (l, i, 0))
    in_specs, args = [row, row, row], [w, m, v]
    for j in range(L):
        in_specs.append(pl.BlockSpec((tr, C), lambda l, i, p=pinned(j): (p(l, i), 0)))
        args.append(owns[j])
        for arr in landeds[j]:
            in_specs.append(pl.BlockSpec((arr.shape[0], tr, C), lambda l, i, p=pinned(j): (0, p(l, i), 0)))
            args.append(arr)
    return _call(body, name=name, grid=(L, nr), in_specs=in_specs, out_specs=[row] * 4,
                 out_shape=[jax.ShapeDtypeStruct((L, R, C), F32)] * 4, args=args, rider=rider)


def kernel(x, a_norm_g, a_w_in, a_v_norm_g, a_w_s, a_b_s, a_w_out, kv_norm_g, w_kv, b_norm_g, b_w_q, b_rel_bias, b_w_o, f_norm_g, f_w_in, f_conv_w, f_conv_b, f_w_down, final_norm_g, loss_target, m_a_norm_g, m_a_w_in, m_a_v_norm_g, m_a_w_s, m_a_b_s, m_a_w_out, m_kv_norm_g, m_w_kv, m_b_norm_g, m_b_w_q, m_b_rel_bias, m_b_w_o, m_f_norm_g, m_f_w_in, m_f_conv_w, m_f_conv_b, m_f_w_down, m_final_norm_g, v_a_norm_g, v_a_w_in, v_a_v_norm_g, v_a_w_s, v_a_b_s, v_a_w_out, v_kv_norm_g, v_w_kv, v_b_norm_g, v_b_w_q, v_b_rel_bias, v_b_w_o, v_f_norm_g, v_f_w_in, v_f_conv_w, v_f_conv_b, v_f_w_down, v_final_norm_g):
    B, S, D = x.shape
    T = B * S
    G = a_w_s.shape[1]
    H = D // HEAD_DIM
    F = f_w_in.shape[2]
    L = f_w_in.shape[0]
    dn = D // N_DEV
    xi, yi, ci = lax.axis_index("x"), lax.axis_index("y"), lax.axis_index("c")
    me = 4 * xi + 2 * yi + ci
    pos = jnp.stack([ci, 2 * xi + yi]).astype(jnp.int32)

    cast = lambda t: t.astype(BF16)
    gather = lambda *ts: _gather_rider(list(ts))
    rel = jnp.pad(b_rel_bias[0], ((0, 0), (0, REL_PAD - b_rel_bias.shape[2])))
    bias, (wa_in, norms_sh, conv_w0, conv_w1) = _bias_build(rel, "bias_build", rider=gather(
        cast(a_w_in[0]), jnp.concatenate([a_norm_g, a_v_norm_g], axis=0), f_conv_w[0], f_conv_w[1]))
    ga = jnp.transpose(norms_sh, (1, 0, 2)).reshape(2, D)
    g_a, g_av = ga[0:1], ga[1:2]

    x2 = x.reshape(T, D)
    tgt = loss_target.reshape(T, D)
    pc = jnp.arange(GMLP_BLOCK) // CHUNK
    mask = (pc[:, None] >= pc[None, :]).astype(F32)
    ws = (a_w_s[0] * mask[None]).astype(BF16)
    bst = jnp.transpose(a_b_s[0])
    four = lambda t: t.reshape((4, 2) + t.shape[1:])

    w_in0_sh = cast(f_w_in[0])
    (z, n_a), (wa_out, w_in0_top) = _norm_matmul(x2, g_a, wa_in, flat=True, nbk=4, name="gmlp_in",
                                                 rider=gather(cast(a_w_out[0]), w_in0_sh[:D // 2]))
    wa_out = wa_out.reshape(D, D)
    (gated, h1), (w_in0_bottom, wf_down0) = _gmlp_forward(z, ws, bst, g_av, wa_out, x2, name="gmlp_mix",
                                                          rider=gather(w_in0_sh[D // 2:], cast(f_w_down[0])))
    w_in0 = jnp.concatenate([w_in0_top, w_in0_bottom], axis=1)
    cw0, cb0, wd0 = conv_w0, f_conv_b[0].reshape(8, 1, F), wf_down0.reshape(4, F, D)
    (au0, ag0, cu0, cg0, act0, n_f0, h2), (wkv, wq, w_in1) = _ffn_fused_forward(
        h1, f_norm_g[0:1], w_in0, cw0, cb0, wd0, S, name="ffn0_fwd",
        rider=gather(cast(w_kv), cast(b_w_q[0]), cast(f_w_in[1])))
    wq = wq.reshape(D, D)
    kv, n_kv = _norm_matmul(h2, kv_norm_g.reshape(1, D), wkv, flat=True, nbk=4, name="kv_proj")
    q, n_q = _norm_matmul(h2, b_norm_g, wq.reshape(1, D, D), flat=True, nbk=1, name="q_proj", scale=HEAD_DIM ** -0.5)
    o, (wo, wf_down1) = _attn_forward(q, kv, bias, S, name="attn", rider=gather(cast(b_w_o[0]), cast(f_w_down[1])))
    wo = wo.reshape(D, D)
    cw1, cb1, wd1 = conv_w1, f_conv_b[1].reshape(8, 1, F), wf_down1.reshape(4, F, D)
    h3 = _matmul_residual(o, wo, h2, "attn_out")
    au1, ag1, cu1, cg1, act1, n_f1, h4 = _ffn_fused_forward(h3, f_norm_g[1:2], w_in1, cw1, cb1, wd1, S, name="ffn1_fwd")

    sums, from_chips = {}, {}

    def sibling_sums(names, parts, landed):
        for nm, p, l in zip(names, parts, landed):
            sums[nm] = _sibling_sum(p, l, pos, "grad_sibling_sum_" + nm)

    def chip_rider(*names):
        return _chip_rider([sums[nm][1] for nm in names])

    dh4, st_final = _loss_head(h4, final_norm_g.reshape(1, D), tgt, "loss_head")
    g_wd1 = _wgrad_rows(act1, dh4, flat=False, tk=F, name="ffn1_dwdown")
    parts = [four(g_wd1.reshape(8, F // 2, D))]
    (dau1, dag1, st_conv1, dh3, st_f1), landed = _ffn_fused_backward(
        dh4, cu1, cg1, au1, ag1, cw1, wd1, w_in1, h3, f_norm_g[1:2], S, name="ffn1_bwd", rider=_sibling_rider(parts))
    sibling_sums(["wd1"], parts, landed)
    g_win1, (from_chips["wd1"],) = _wgrad_cols(n_f1, (dau1, dag1), flat=False, nb=8, nbk=2, name="ffn1_dwin",
                                               rider=chip_rider("wd1"))
    d_o = _matmul_nt(dh3, wo.reshape(1, D, D), flat=True, nbk=1, name="attn_out_dx")
    parts = [four(g_win1)]
    g_wo, landed = _wgrad_rows(o, dh3, flat=True, tk=_tile(D, 512), name="attn_out_dw", rider=_sibling_rider(parts))
    sibling_sums(["win1"], parts, landed)
    (dq, dk, dv, dbias), (from_chips["win1"],) = _attn_backward(
        q, kv, bias, d_o, S, name="attn_bwd", rider=chip_rider("win1"))
    g_rel = _bias_reduce(dbias, "bias_reduce")
    g_wq = _wgrad_cols(n_q, dq, flat=True, nb=1, nbk=1, name="q_dw")
    dh2, st_b = _matmul_nt(dq, wq.reshape(1, D, D), flat=True, nbk=1, name="q_dx", norm=(h2, b_norm_g, dh3))
    dkv = jnp.concatenate([dk, dv], axis=-1)
    g_wkv = _wgrad_cols(n_kv, dkv, flat=True, nb=8, nbk=4, name="kv_dw")
    parts = [four(g_wo.reshape(8, dn, D)), four(g_wq.reshape(8, dn, D)), four(g_wkv)]
    (dh2, st_kv), landed = _matmul_nt(dkv, wkv, flat=True, nbk=4, name="kv_dx",
                                      norm=(h2, kv_norm_g.reshape(1, D), dh2), rider=_sibling_rider(parts))
    sibling_sums(["wo", "wq", "wkv"], parts, landed)
    g_wd0, (from_chips["wo"], from_chips["wq"], from_chips["wkv"]) = _wgrad_rows(
        act0, dh2, flat=False, tk=F, name="ffn0_dwdown", rider=chip_rider("wo", "wq", "wkv"))
    parts = [four(g_wd0.reshape(8, F // 2, D))]
    (dau0, dag0, st_conv0, dh1, st_f0), landed = _ffn_fused_backward(
        dh2, cu0, cg0, au0, ag0, cw0, wd0, w_in0, h1, f_norm_g[0:1], S, name="ffn0_bwd", rider=_sibling_rider(parts))
    sibling_sums(["wd0"], parts, landed)
    g_win0, (ce,) = _wgrad_cols(n_f0, (dau0, dag0), flat=False, nb=8, nbk=2, name="ffn0_dwin",
                                rider=chip_rider("wd0"))
    from_chips["wd0"] = [ce]
    dgated = _matmul_nt(dh1, wa_out.reshape(1, D, D), flat=True, nbk=1, name="gmlp_out_dx")
    parts = [four(g_win0)]
    g_wa_out, landed = _wgrad_rows(gated, dh1, flat=True, tk=_tile(D, 512), name="gmlp_out_dw",
                                   rider=_sibling_rider(parts))
    sibling_sums(["win0"], parts, landed)
    parts = [four(g_wa_out.reshape(8, dn, D))]
    (dz, g_ws, g_bst, st_av), (ce_win0_a, landed) = _gmlp_backward(
        z, dgated, ws, bst, g_av, mask, name="gmlp_bwd",
        rider=_join_riders([_chip_rider([sums["win0"][1]], ks=(1, 2)), _sibling_rider(parts)]))
    sibling_sums(["wa_out"], parts, [landed])
    g_wa_in, (ce_win0_b, ce) = _wgrad_cols(n_a, dz, flat=True, nb=8, nbk=4, name="gmlp_in_dw", rider=_join_riders(
        [_chip_rider([sums["win0"][1]], ks=(3,)), chip_rider("wa_out")]))
    from_chips["win0"], from_chips["wa_out"] = [ce_win0_a, ce_win0_b], [ce]
    vec = jnp.concatenate([st_av[0:1], st_kv[0:1], st_b[0:1], st_f0[0:1], st_f1[0:1], st_final[0:3]], axis=0)
    parts = [four(g_wa_in)]
    (grad_x, st_a), got = _matmul_nt(dz, wa_in, flat=True, nbk=4, name="gmlp_in_dx", norm=(x2, g_a, dh1),
                                     rider=_join_riders([_sibling_rider(parts), gather(
                                         vec, cast(g_ws), cast(g_bst), cast(g_rel), cast(st_conv0), cast(st_conv1))]))
    sibling_sums(["wa_in"], parts, got[0:1])
    small = got[1:]

    def big_update(names, w, m, v, rider=None):
        shape = w.shape
        r = lambda t: t.reshape((len(names), -1, shape[-1]))
        as_list = lambda t: t if isinstance(t, list) else [t]
        outs = _adamw_big(r(w), r(m), r(v), [sums[nm][0] for nm in names],
                          [as_list(from_chips[nm]) for nm in names], "adamw_" + names[0], rider=rider)
        outs, got = (outs, None) if rider is None else outs
        return [t.reshape(shape) for t in outs], got

    u_f_w_in, (ce, st_a) = big_update(["win0", "win1"], f_w_in, m_f_w_in, v_f_w_in,
                                      rider=_join_riders([chip_rider("wa_in"), gather(st_a)]))
    from_chips["wa_in"] = [ce]
    u_f_w_down, _ = big_update(["wd0", "wd1"], f_w_down, m_f_w_down, v_f_w_down)
    u_a_w_in, _ = big_update(["wa_in"], a_w_in, m_a_w_in, v_a_w_in)
    u_w_kv, _ = big_update(["wkv"], w_kv, m_w_kv, v_w_kv)
    u_a_w_out, _ = big_update(["wa_out"], a_w_out, m_a_w_out, v_a_w_out)
    u_b_w_q, _ = big_update(["wq"], b_w_q, m_b_w_q, v_b_w_q)
    u_b_w_o, _ = big_update(["wo"], b_w_o, m_b_w_o, v_b_w_o)

    vec, g_ws, g_bst, g_rel, st_conv0, st_conv1, st_a = _sum_devices(list(small) + [st_a], "sum_small_grads")
    vec = jnp.concatenate([st_a[0:1], vec[0:7]], axis=0)
    loss = vec[7, 0]
    g_a_norm = lax.dynamic_slice_in_dim(vec[0:1], me * dn, dn, axis=1)
    g_av_norm = lax.dynamic_slice_in_dim(vec[1:2], me * dn, dn, axis=1)
    st_conv = jnp.stack([st_conv0, st_conv1])
    g_conv_w = lax.dynamic_index_in_dim(st_conv, me, axis=1, keepdims=False)[:, 0:3]
    g_conv_b = st_conv[:, :, 3, :].reshape(L, 8 * F)
    small_items = [
        (a_norm_g, m_a_norm_g, v_a_norm_g, g_a_norm),
        (a_v_norm_g, m_a_v_norm_g, v_a_v_norm_g, g_av_norm),
        (a_w_s, m_a_w_s, v_a_w_s, g_ws[None]),
        (a_b_s, m_a_b_s, v_a_b_s, jnp.transpose(g_bst)[None]),
        (kv_norm_g.reshape(1, D), m_kv_norm_g.reshape(1, D), v_kv_norm_g.reshape(1, D), vec[2:3]),
        (b_norm_g, m_b_norm_g, v_b_norm_g, vec[3:4]),
        (b_rel_bias, m_b_rel_bias, v_b_rel_bias, g_rel[None, :, :b_rel_bias.shape[2]]),
        (f_norm_g, m_f_norm_g, v_f_norm_g, vec[4:6]),
        (f_conv_w, m_f_conv_w, v_f_conv_w, g_conv_w),
        (f_conv_b, m_f_conv_b, v_f_conv_b, g_conv_b),
        (final_norm_g.reshape(1, D), m_final_norm_g.reshape(1, D), v_final_norm_g.reshape(1, D), vec[6:7]),
    ]
    small_out = _adamw_small(small_items, "adamw_small")
    (u_a_norm, u_av_norm, u_ws, u_bs, u_kvn, u_bn, u_rel, u_fn, u_cw, u_cb, u_fin) = [
        (it[3],) + so for it, so in zip(small_items, small_out)]
    vecD = lambda u: tuple(t.reshape(D) for t in u)
    u_kvn, u_fin = vecD(u_kvn), vecD(u_fin)

    order = [u_a_norm, u_a_w_in, u_av_norm, u_ws, u_bs, u_a_w_out, u_kvn, u_w_kv, u_bn, u_b_w_q, u_rel, u_b_w_o,
             u_fn, u_f_w_in, u_cw, u_cb, u_f_w_down, u_fin]
    outs = [loss, grad_x.reshape(B, S, D)]
    for k in range(4):
        outs += [u[k] for u in order]
    return tuple(outs)
```

```python
import functools

import jax
import jax.numpy as jnp
from jax import lax
from jax.experimental import pallas as pl
from jax.experimental.pallas import tpu as pltpu

F32 = jnp.float32
BF16 = jnp.bfloat16
MESH = pl.DeviceIdType.MESH

N_DEV = 8
EPS = 1e-6
NEG_INF = -1e30
CHUNK = 64
LEFT_CHUNKS = 8
REL_CLIP = 128
HEAD_DIM = 64
GMLP_BLOCK = 128
Q_TILE = 2 * CHUNK
PAD = LEFT_CHUNKS * CHUNK
WIN = PAD + Q_TILE
SKEW = WIN + Q_TILE
REL_PAD = 384
FWD_HEADS_PER_STEP = 8
BWD_HEADS_PER_STEP = 4
TILES_PER_STEP = 4
STRIP = 32
ROWS = 32
ADAM_LR, ADAM_B1, ADAM_B2, ADAM_EPS, ADAM_WD, ADAM_STEP = 0.001, 0.9, 0.999, 1e-08, 0.01, 10
VMEM_LIMIT = 60 * 1024 * 1024
TOKEN_TILE = 512
MATMUL_TILE = 1024
FFN_BLOCKS_PER_STEP = 2


def _params(sem=None):
    return pltpu.CompilerParams(dimension_semantics=sem, vmem_limit_bytes=VMEM_LIMIT)


def _tile(n, pref):
    if n <= pref:
        return n
    for t in range(pref - pref % 8, 7, -8):
        if n % t == 0:
            return t
    return n


def _gelu(x):
    return 0.5 * x * (1.0 + jnp.tanh(0.7978845608028654 * (x + 0.044715 * x * x * x)))


def _gelu_grad(x):
    t = jnp.tanh(0.7978845608028654 * (x + 0.044715 * x * x * x))
    return 0.5 * (1.0 + t) + 0.5 * x * (1.0 - t * t) * 0.7978845608028654 * (1.0 + 3 * 0.044715 * x * x)


def _sigmoid(x):
    return 1.0 / (1.0 + jnp.exp(-x))


def _dot(a, b):
    return jnp.dot(a, b, preferred_element_type=F32)


def _dot_nt(a, b):
    return lax.dot_general(a, b, (((1,), (1,)), ((), ())), preferred_element_type=F32)


def _dot_tn(a, b):
    return lax.dot_general(a, b, (((0,), (0,)), ((), ())), preferred_element_type=F32)


def _split3(x):
    hi = x.astype(BF16)
    r1 = x - hi.astype(F32)
    mid = r1.astype(BF16)
    lo = (r1 - mid.astype(F32)).astype(BF16)
    return hi, mid, lo


def _mesh_pos():
    return lax.axis_index("x"), lax.axis_index("y"), lax.axis_index("c")


class _Rider:
    def __init__(self, arrs, out_shapes, sems, start, finish):
        self.arrs, self.out_shapes, self.sems, self.start, self.finish = arrs, out_shapes, sems, start, finish


def _gather_rider(arrs):
    n = len(arrs)

    def tools(ins, outs, sems):
        send_sems, recv_sems, local_sems = sems
        x, y, c = _mesh_pos()
        me, sibling = (x, y, c), (x, y, 1 - c)
        chips = [(1 - x, y), (x, 1 - y), (1 - x, 1 - y)]

        def slot(a, block):
            px, py, pc = block
            return outs[a].at[4 * px + 2 * py + pc]

        def copy(a, k, block, to, src=None):
            dst = slot(a, block)
            return pltpu.make_async_remote_copy(
                src_ref=dst if src is None else src, dst_ref=dst,
                send_sem=send_sems.at[a, k], recv_sem=recv_sems.at[a, k], device_id=to, device_id_type=MESH)

        def first(a):
            cps = [copy(a, 0, me, sibling, src=ins[a])]
            return cps + [copy(a, 1 + j, me, (*chip, c), src=ins[a]) for j, chip in enumerate(chips)]

        def mine(a):
            return pltpu.make_async_copy(ins[a], slot(a, me), local_sems.at[a])

        return me, sibling, chips, c, copy, first, mine

    def start(ins, outs, sems):
        _, _, _, _, _, first, mine = tools(ins, outs, sems)
        for a in range(n):
            mine(a).start()
            for cp in first(a):
                cp.start()

    def finish(ins, outs, sems):
        me, sibling, chips, c, copy, first, mine = tools(ins, outs, sems)
        passed = []
        for j, chip in enumerate(chips):
            for a in range(n):
                copy(a, 1 + j, (*chip, c), me).wait_recv()
                fwd = copy(a, 4 + j, (*chip, c), sibling)
                fwd.start()
                passed.append(fwd)
        for a in range(n):
            copy(a, 0, sibling, me).wait_recv()
            for j, chip in enumerate(chips):
                copy(a, 4 + j, (*chip, 1 - c), me).wait_recv()
        for a in range(n):
            for cp in first(a):
                cp.wait_send()
        for cp in passed:
            cp.wait_send()
        for a in range(n):
            mine(a).wait()

    return _Rider(list(arrs), [jax.ShapeDtypeStruct((N_DEV,) + a.shape, a.dtype) for a in arrs],
                  [pltpu.SemaphoreType.DMA((n, 7)), pltpu.SemaphoreType.DMA((n, 7)), pltpu.SemaphoreType.DMA((n,))],
                  start, finish)


def _sibling_rider(arrs):
    n = len(arrs)

    def copies(ins, outs, sems):
        x, y, c = _mesh_pos()
        return [pltpu.make_async_remote_copy(
            src_ref=ins[a].at[:, pl.ds(1 - c, 1)], dst_ref=outs[a],
            send_sem=sems[0].at[a], recv_sem=sems[1].at[a], device_id=(x, y, 1 - c), device_id_type=MESH)
            for a in range(n)]

    def start(ins, outs, sems):
        for cp in copies(ins, outs, sems):
            cp.start()

    def finish(ins, outs, sems):
        for cp in copies(ins, outs, sems):
            cp.wait()

    return _Rider(list(arrs), [jax.ShapeDtypeStruct((4, 1) + a.shape[2:], a.dtype) for a in arrs],
                  [pltpu.SemaphoreType.DMA((n,)), pltpu.SemaphoreType.DMA((n,))], start, finish)


def _chip_rider(arrs, ks=(1, 2, 3)):
    n = len(arrs)

    def copies(ins, outs, sems):
        x, y, c = _mesh_pos()
        cps = []
        for a in range(n):
            for s, k in enumerate(ks):
                px = x if k < 2 else 1 - x
                py = y if k == 2 else 1 - y
                cps.append(pltpu.make_async_remote_copy(
                    src_ref=ins[a].at[2 * px + py], dst_ref=outs[a].at[s],
                    send_sem=sems[0].at[a, s], recv_sem=sems[1].at[a, s],
                    device_id=(px, py, c), device_id_type=MESH))
        return cps

    def start(ins, outs, sems):
        for cp in copies(ins, outs, sems):
            cp.start()

    def finish(ins, outs, sems):
        for cp in copies(ins, outs, sems):
            cp.wait()

    return _Rider(list(arrs), [jax.ShapeDtypeStruct((len(ks),) + a.shape[1:], a.dtype) for a in arrs],
                  [pltpu.SemaphoreType.DMA((n, len(ks))), pltpu.SemaphoreType.DMA((n, len(ks)))], start, finish)


def _join_riders(riders):
    def split(seq, counts):
        out, at = [], 0
        for k in counts:
            out.append(seq[at:at + k])
            at += k
        return out

    n_in = [len(r.arrs) for r in riders]
    n_out = [len(r.out_shapes) for r in riders]
    n_sem = [len(r.sems) for r in riders]

    def run(which):
        def fn(ins, outs, sems):
            for r, i, o, s in zip(riders, split(ins, n_in), split(outs, n_out), split(sems, n_sem)):
                getattr(r, which)(i, o, s)
        return fn

    return _Rider([a for r in riders for a in r.arrs], [o for r in riders for o in r.out_shapes],
                  [s for r in riders for s in r.sems], run("start"), run("finish"))


def _run_rider(rider, name):
    n_in, n_out = len(rider.arrs), len(rider.out_shapes)

    def body(*refs):
        ins, outs, sems = refs[:n_in], refs[n_in:n_in + n_out], refs[n_in + n_out:]
        rider.start(ins, outs, sems)
        rider.finish(ins, outs, sems)

    any_spec = pl.BlockSpec(memory_space=pl.ANY)
    return pl.pallas_call(
        body, name=name, out_shape=list(rider.out_shapes), in_specs=[any_spec] * n_in, out_specs=[any_spec] * n_out,
        scratch_shapes=list(rider.sems),
    )(*rider.arrs)


def _call(body, *, name, grid, in_specs, out_specs, out_shape, args, scratch_shapes=(), rider=None):
    params = _params(("arbitrary",) * len(grid))
    if rider is None:
        return pl.pallas_call(body, name=name, grid=grid, in_specs=in_specs, out_specs=out_specs, out_shape=out_shape,
                              scratch_shapes=list(scratch_shapes), compiler_params=params)(*args)
    single = not isinstance(out_shape, (list, tuple))
    outs = [out_shape] if single else list(out_shape)
    ospecs = [out_specs] if single else list(out_specs)
    n_in, n_out, n_scr = len(in_specs), len(outs), len(scratch_shapes)
    r_in, r_out = len(rider.arrs), len(rider.out_shapes)

    def hosted(*refs):
        refs = list(refs)
        ins, rins = refs[:n_in], refs[n_in:n_in + r_in]
        refs = refs[n_in + r_in:]
        houts, routs = refs[:n_out], refs[n_out:n_out + r_out]
        refs = refs[n_out + r_out:]
        scr, rsems = refs[:n_scr], refs[n_scr:]
        ids = [pl.program_id(a) for a in range(len(grid))]
        first = functools.reduce(lambda p, q: p & q, [i == 0 for i in ids])
        last = functools.reduce(lambda p, q: p & q, [i == g - 1 for i, g in zip(ids, grid)])

        @pl.when(first)
        def _():
            rider.start(rins, routs, rsems)

        body(*ins, *houts, *scr)

        @pl.when(last)
        def _():
            rider.finish(rins, routs, rsems)

    any_spec = pl.BlockSpec(memory_space=pl.ANY)
    res = pl.pallas_call(
        hosted, name=name, grid=grid, in_specs=list(in_specs) + [any_spec] * r_in,
        out_specs=ospecs + [any_spec] * r_out, out_shape=outs + list(rider.out_shapes),
        scratch_shapes=list(scratch_shapes) + list(rider.sems), compiler_params=params,
    )(*args, *rider.arrs)
    return (res[0] if single else list(res[:n_out])), list(res[n_out:])


def _sibling_sum(part, landed, pos, name):
    _, _, rows, cols = part.shape
    tr = _tile(rows, 512)

    def body(pos_ref, p_ref, l_ref, own_ref, all_ref):
        s = p_ref[0, 0] + l_ref[0, 0]
        all_ref[0] = s.astype(BF16)

        @pl.when(pl.program_id(1) == pos_ref[1])
        def _():
            own_ref[...] = s

    return pl.pallas_call(
        body, name=name,
        grid_spec=pltpu.PrefetchScalarGridSpec(
            num_scalar_prefetch=1, grid=(rows // tr, 4),
            in_specs=[pl.BlockSpec((1, 1, tr, cols), lambda i, k, pos: (k, pos[0], i, 0)),
                      pl.BlockSpec((1, 1, tr, cols), lambda i, k, pos: (k, 0, i, 0))],
            out_specs=[pl.BlockSpec((tr, cols), lambda i, k, pos: (i, 0)),
                       pl.BlockSpec((1, tr, cols), lambda i, k, pos: (k, i, 0))]),
        out_shape=[jax.ShapeDtypeStruct((rows, cols), F32), jax.ShapeDtypeStruct((4, rows, cols), BF16)],
        compiler_params=_params(("arbitrary", "arbitrary")),
    )(pos, part, landed)


def _rms(x, g):
    r = lax.rsqrt(jnp.mean(x * x, axis=-1, keepdims=True) + EPS)
    return x * r, r


def _norm_matmul(h, g, w, *, flat, nbk, name, scale=1.0, rider=None):
    T, D = h.shape
    nb, _, bn = w.shape
    tm = _tile(T, MATMUL_TILE)

    def body(h_ref, g_ref, w_ref, o_ref, n_ref):
        @pl.when(pl.program_id(1) == 0)
        def _():
            xh, _ = _rms(h_ref[...], None)
            n_ref[...] = (xh * g_ref[...]).astype(BF16)

        n = n_ref[...]
        for k in range(nbk):
            r = _dot(n, w_ref[k])
            r = (r if scale == 1.0 else r * scale).astype(BF16)
            if flat:
                o_ref[:, k * bn:(k + 1) * bn] = r
            else:
                o_ref[k] = r

    if flat:
        out_shape = jax.ShapeDtypeStruct((T, nb * bn), BF16)
        out_spec = pl.BlockSpec((tm, nbk * bn), lambda i, j: (i, j))
    else:
        out_shape = jax.ShapeDtypeStruct((nb, T, bn), BF16)
        out_spec = pl.BlockSpec((nbk, tm, bn), lambda i, j: (j, i, 0))
    return _call(
        body, name=name, grid=(T // tm, nb // nbk),
        in_specs=[pl.BlockSpec((tm, D), lambda i, j: (i, 0)),
                  pl.BlockSpec((1, D), lambda i, j: (0, 0)),
                  pl.BlockSpec((nbk, D, bn), lambda i, j: (j, 0, 0))],
        out_specs=[out_spec, pl.BlockSpec((tm, D), lambda i, j: (i, 0))],
        out_shape=[out_shape, jax.ShapeDtypeStruct((T, D), BF16)],
        args=(h, g, w), rider=rider)


def _matmul_nt(dy, w, *, flat, nbk, name, norm=None, out_dtype=BF16, rider=None):
    nb, R, bn = w.shape
    T = dy.shape[0] if flat else dy.shape[1]
    tm = _tile(T, MATMUL_TILE)
    nj = nb // nbk

    def body(*refs):
        if norm is None:
            dy_ref, w_ref, o_ref, acc_ref = refs
        else:
            dy_ref, w_ref, h_ref, g_ref, dres_ref, o_ref, dg_ref, acc_ref = refs
        i, j = pl.program_id(0), pl.program_id(1)

        @pl.when(j == 0)
        def _():
            acc_ref[...] = jnp.zeros_like(acc_ref)

        acc = acc_ref[...]
        for k in range(nbk):
            d = dy_ref[:, k * bn:(k + 1) * bn] if flat else dy_ref[k]
            acc = acc + _dot_nt(d.astype(BF16), w_ref[k])
        acc_ref[...] = acc

        @pl.when(j == nj - 1)
        def _():
            if norm is None:
                o_ref[...] = acc.astype(out_dtype)
            else:
                xh, r = _rms(h_ref[...], None)

                @pl.when(i == 0)
                def _():
                    dg_ref[...] = jnp.zeros_like(dg_ref)

                dg_ref[0:1, :] += jnp.sum(acc * xh, axis=0, keepdims=True)
                dn = acc * g_ref[...]
                o_ref[...] = dres_ref[...] + r * (dn - xh * jnp.mean(dn * xh, axis=-1, keepdims=True))

    if flat:
        dy_spec = pl.BlockSpec((tm, nbk * bn), lambda i, j: (i, j))
    else:
        dy_spec = pl.BlockSpec((nbk, tm, bn), lambda i, j: (j, i, 0))
    w_spec = pl.BlockSpec((nbk, R, bn), lambda i, j: (j, 0, 0))
    row_spec = pl.BlockSpec((tm, R), lambda i, j: (i, 0))
    if norm is None:
        in_specs, args = [dy_spec, w_spec], (dy, w)
        out_specs = row_spec
        out_shape = jax.ShapeDtypeStruct((T, R), out_dtype)
    else:
        in_specs = [dy_spec, w_spec, row_spec, pl.BlockSpec((1, R), lambda i, j: (0, 0)), row_spec]
        args = (dy, w) + tuple(norm)
        out_specs = [row_spec, pl.BlockSpec((8, R), lambda i, j: (0, 0))]
        out_shape = [jax.ShapeDtypeStruct((T, R), F32), jax.ShapeDtypeStruct((8, R), F32)]
    return _call(
        body, name=name, grid=(T // tm, nj), in_specs=in_specs, out_specs=out_specs, out_shape=out_shape,
        scratch_shapes=[pltpu.VMEM((tm, R), F32)], args=args, rider=rider)


def _wgrad_cols(n, dy, *, flat, nb, nbk, name, rider=None):
    T, D = n.shape
    halves = isinstance(dy, tuple)
    bn = dy.shape[1] // nb if flat else (dy[0] if halves else dy).shape[2]
    tt = _tile(T, MATMUL_TILE)
    nt = T // tt
    nj = nb // nbk

    def body(*refs):
        n_ref, dy_refs, (o_ref, acc_ref) = refs[0], refs[1:-2], refs[-2:]
        j, t = pl.program_id(0), pl.program_id(1)

        @pl.when(t == 0)
        def _():
            acc_ref[...] = jnp.zeros_like(acc_ref)

        def accumulate(dy_ref):
            nv = n_ref[...]
            for k in range(nbk):
                d = dy_ref[:, k * bn:(k + 1) * bn] if flat else dy_ref[k]
                acc_ref[k] += _dot_tn(nv, d)

        if halves:
            pl.when(j < nj // 2)(lambda: accumulate(dy_refs[0]))
            pl.when(j >= nj // 2)(lambda: accumulate(dy_refs[1]))
        else:
            accumulate(dy_refs[0])

        @pl.when(t == nt - 1)
        def _():
            o_ref[...] = acc_ref[...]

    if flat:
        dy_specs, dys = [pl.BlockSpec((tt, nbk * bn), lambda j, t: (t, j))], [dy]
    elif halves:
        first = pl.BlockSpec((nbk, tt, bn), lambda j, t: (jnp.minimum(j, nj // 2 - 1),
                                                          jnp.where(j < nj // 2, t, nt - 1), 0))
        second = pl.BlockSpec((nbk, tt, bn), lambda j, t: (jnp.maximum(j - nj // 2, 0),
                                                           jnp.where(j >= nj // 2, t, 0), 0))
        dy_specs, dys = [first, second], list(dy)
    else:
        dy_specs, dys = [pl.BlockSpec((nbk, tt, bn), lambda j, t: (j, t, 0))], [dy]
    return _call(
        body, name=name, grid=(nj, nt),
        in_specs=[pl.BlockSpec((tt, D), lambda j, t: (t, 0))] + dy_specs,
        out_specs=pl.BlockSpec((nbk, D, bn), lambda j, t: (j, 0, 0)),
        out_shape=jax.ShapeDtypeStruct((nb, D, bn), F32),
        scratch_shapes=[pltpu.VMEM((nbk, D, bn), F32)], args=[n] + dys, rider=rider)


def _wgrad_rows(xa, dh, *, flat, tk, name, rider=None):
    T, D = dh.shape
    nk = xa.shape[1] // tk if flat else xa.shape[0]
    tt = _tile(T, MATMUL_TILE)
    nt = T // tt

    def body(x_ref, dh_ref, o_ref, acc_ref):
        t = pl.program_id(1)

        @pl.when(t == 0)
        def _():
            acc_ref[...] = jnp.zeros_like(acc_ref)

        xv = x_ref[...] if flat else x_ref[0]
        acc_ref[...] += _dot_tn(xv, dh_ref[...].astype(BF16))

        @pl.when(t == nt - 1)
        def _():
            o_ref[...] = acc_ref[...]

    x_spec = pl.BlockSpec((tt, tk), lambda j, t: (t, j)) if flat else pl.BlockSpec((1, tt, tk), lambda j, t: (j, t, 0))
    return _call(
        body, name=name, grid=(nk, nt),
        in_specs=[x_spec, pl.BlockSpec((tt, D), lambda j, t: (t, 0))],
        out_specs=pl.BlockSpec((tk, D), lambda j, t: (j, 0)),
        out_shape=jax.ShapeDtypeStruct((nk * tk, D), F32),
        scratch_shapes=[pltpu.VMEM((tk, D), F32)], args=(xa, dh), rider=rider)


def _matmul_residual(xa, w, res, name):
    T, K = xa.shape
    D = w.shape[1]
    tm = _tile(T, MATMUL_TILE)

    def body(x_ref, w_ref, r_ref, o_ref):
        o_ref[...] = r_ref[...] + _dot(x_ref[...], w_ref[...])

    return pl.pallas_call(
        body, name=name, grid=(T // tm,),
        in_specs=[pl.BlockSpec((tm, K), lambda i: (i, 0)), pl.BlockSpec((K, D), lambda i: (0, 0)),
                  pl.BlockSpec((tm, D), lambda i: (i, 0))],
        out_specs=pl.BlockSpec((tm, D), lambda i: (i, 0)),
        out_shape=jax.ShapeDtypeStruct((T, D), F32),
        compiler_params=_params(("arbitrary",)),
    )(xa, w, res)


def _gmlp_gate(z, ws, bst, gv, G, gd):
    D = G * gd
    u = _gelu(z[:, :D].astype(F32))
    v = _gelu(z[:, D:].astype(F32))
    vh, r = _rms(v, None)
    vn = (vh * gv).astype(BF16)
    return u, v, vh, r, vn


def _gmlp_forward(z, ws, bst, gv, w_out, x, *, name, rider=None):
    T, D2 = z.shape
    D = D2 // 2
    G = ws.shape[0]
    gd = D // G
    tb = _tile(T, 256)
    nblk = tb // GMLP_BLOCK

    def body(z_ref, ws_ref, b_ref, gv_ref, wo_ref, x_ref, gated_ref, h_ref):
        u, _, _, _, vn = _gmlp_gate(z_ref[...], None, None, gv_ref[...], G, gd)
        for n in range(nblk):
            rows = slice(n * GMLP_BLOCK, (n + 1) * GMLP_BLOCK)
            for gi in range(G):
                cols = slice(gi * gd, (gi + 1) * gd)
                s = _dot(ws_ref[gi], vn[rows, cols]) + b_ref[:, gi:gi + 1]
                gated_ref[rows, cols] = (u[rows, cols] * s).astype(BF16)
        h_ref[...] = x_ref[...] + _dot(gated_ref[...], wo_ref[...])

    return _call(
        body, name=name, grid=(T // tb,),
        in_specs=[pl.BlockSpec((tb, D2), lambda i: (i, 0)), pl.BlockSpec(ws.shape, lambda i: (0, 0, 0)),
                  pl.BlockSpec(bst.shape, lambda i: (0, 0)), pl.BlockSpec((1, D), lambda i: (0, 0)),
                  pl.BlockSpec((D, D), lambda i: (0, 0)), pl.BlockSpec((tb, D), lambda i: (i, 0))],
        out_specs=[pl.BlockSpec((tb, D), lambda i: (i, 0)), pl.BlockSpec((tb, D), lambda i: (i, 0))],
        out_shape=[jax.ShapeDtypeStruct((T, D), BF16), jax.ShapeDtypeStruct((T, D), F32)],
        args=(z, ws, bst, gv, w_out, x), rider=rider)


def _gmlp_backward(z, dgated, ws, bst, gv, mask, *, name, rider=None):
    T, D2 = z.shape
    D = D2 // 2
    G = ws.shape[0]
    gd = D // G
    tb = _tile(T, 256)
    nblk = tb // GMLP_BLOCK

    def body(z_ref, dg_ref, ws_ref, b_ref, gv_ref, mask_ref, dz_ref, dws_ref, db_ref, dgv_ref, dvn_ref):
        @pl.when(pl.program_id(0) == 0)
        def _():
            dws_ref[...] = jnp.zeros_like(dws_ref)
            db_ref[...] = jnp.zeros_like(db_ref)
            dgv_ref[...] = jnp.zeros_like(dgv_ref)

        zf = z_ref[...]
        u, v, vh, r, vn = _gmlp_gate(zf, None, None, gv_ref[...], G, gd)
        dg = dg_ref[...].astype(F32)
        for n in range(nblk):
            rows = slice(n * GMLP_BLOCK, (n + 1) * GMLP_BLOCK)
            for gi in range(G):
                cols = slice(gi * gd, (gi + 1) * gd)
                vblk = vn[rows, cols]
                s = _dot(ws_ref[gi], vblk) + b_ref[:, gi:gi + 1]
                dgb = dg[rows, cols]
                ds = dgb * u[rows, cols]
                dsb = ds.astype(BF16)
                dz_ref[rows, cols] = (dgb * s * _gelu_grad(zf[rows, cols].astype(F32))).astype(BF16)
                dvn_ref[rows, cols] = _dot_tn(ws_ref[gi], dsb)
                dws_ref[gi] += _dot_nt(dsb, vblk) * mask_ref[...]
                db_ref[:, gi:gi + 1] += jnp.sum(ds, axis=1, keepdims=True)
        dvn = dvn_ref[...]
        dgv_ref[0:1, :] += jnp.sum(dvn * vh, axis=0, keepdims=True)
        dn = dvn * gv_ref[...]
        dv = r * (dn - vh * jnp.mean(dn * vh, axis=-1, keepdims=True))
        dz_ref[:, D:] = (dv * _gelu_grad(zf[:, D:].astype(F32))).astype(BF16)

    return _call(
        body, name=name, grid=(T // tb,),
        in_specs=[pl.BlockSpec((tb, D2), lambda i: (i, 0)), pl.BlockSpec((tb, D), lambda i: (i, 0)),
                  pl.BlockSpec(ws.shape, lambda i: (0, 0, 0)), pl.BlockSpec(bst.shape, lambda i: (0, 0)),
                  pl.BlockSpec((1, D), lambda i: (0, 0)), pl.BlockSpec(mask.shape, lambda i: (0, 0))],
        out_specs=[pl.BlockSpec((tb, D2), lambda i: (i, 0)), pl.BlockSpec(ws.shape, lambda i: (0, 0, 0)),
                   pl.BlockSpec(bst.shape, lambda i: (0, 0)), pl.BlockSpec((8, D), lambda i: (0, 0))],
        out_shape=[jax.ShapeDtypeStruct((T, D2), BF16), jax.ShapeDtypeStruct(ws.shape, F32),
                   jax.ShapeDtypeStruct(bst.shape, F32), jax.ShapeDtypeStruct((8, D), F32)],
        scratch_shapes=[pltpu.VMEM((tb, D), F32)], args=(z, dgated, ws, bst, gv, mask), rider=rider)


def _shift_rows(x, k):
    return pltpu.roll(x, k % x.shape[0], axis=0)


def _conv3(ext, cw):
    return (cw[0:1] * _shift_rows(ext, 2)[8:] + cw[1:2] * _shift_rows(ext, 1)[8:] + cw[2:3] * ext[8:])


def _ffn_forward(a, cw, cb, wd, h, seq, *, name, rider=None):
    _, T, F = a.shape
    D = h.shape[1]
    tm = _tile(seq, TOKEN_TILE)
    hb = tm // 16

    def body(a_ref, ap_ref, cw_ref, cb_ref, wd_ref, h_ref, act_ref, c_ref, o_ref, acc_ref):
        i, j = pl.program_id(0), pl.program_id(1)
        keep = ((i * tm) % seq != 0).astype(F32)

        def conv(b):
            ext = jnp.concatenate([ap_ref[b, 8:16].astype(F32) * keep, a_ref[b].astype(F32)], axis=0)
            return _conv3(ext, cw_ref[b]) + cb_ref[b]

        up, gate = conv(j), conv(j + 4)
        c_ref[j] = up.astype(BF16)
        c_ref[j + 4] = gate.astype(BF16)
        act = (gate * _sigmoid(gate) * up).astype(BF16)
        act_ref[0] = act

        @pl.when(j == 0)
        def _():
            acc_ref[...] = h_ref[...]

        acc_ref[...] += _dot(act, wd_ref[0])

        @pl.when(j == 3)
        def _():
            o_ref[...] = acc_ref[...]

    return _call(
        body, name=name, grid=(T // tm, 4),
        in_specs=[pl.BlockSpec((8, tm, F), lambda i, j: (0, i, 0)),
                  pl.BlockSpec((8, 16, F), lambda i, j: (0, jnp.maximum(i * hb - 1, 0), 0)),
                  pl.BlockSpec((8, 3, F), lambda i, j: (0, 0, 0)), pl.BlockSpec((8, 1, F), lambda i, j: (0, 0, 0)),
                  pl.BlockSpec((1, F, D), lambda i, j: (j, 0, 0)), pl.BlockSpec((tm, D), lambda i, j: (i, 0))],
        out_specs=[pl.BlockSpec((1, tm, F), lambda i, j: (j, i, 0)), pl.BlockSpec((8, tm, F), lambda i, j: (0, i, 0)),
                   pl.BlockSpec((tm, D), lambda i, j: (i, 0))],
        out_shape=[jax.ShapeDtypeStruct((4, T, F), BF16), jax.ShapeDtypeStruct((8, T, F), BF16),
                   jax.ShapeDtypeStruct((T, D), F32)],
        scratch_shapes=[pltpu.VMEM((tm, D), F32)], args=(a, a, cw, cb, wd, h), rider=rider)


def _ffn_backward(dh, c, a, cw, wd, seq, *, name, rider=None):
    _, T, F = a.shape
    D = dh.shape[1]
    tm = _tile(seq, TOKEN_TILE)
    hb = tm // 16
    nt = T // tm

    def body(dh_ref, dhn_ref, cu_ref, cg_ref, cun_ref, cgn_ref, au_ref, ag_ref, cw_ref, wd_ref, da_ref, st_ref):
        i, j = pl.program_id(0), pl.program_id(1)
        keep_next = (((i + 1) * tm) % seq != 0).astype(F32)

        @pl.when((i == 0) & (j == 0))
        def _():
            st_ref[...] = jnp.zeros_like(st_ref)

        dhe = jnp.concatenate([dh_ref[...], dhn_ref[...] * keep_next], axis=0).astype(BF16)
        dact = _dot_nt(dhe, wd_ref[0])
        up = jnp.concatenate([cu_ref[0].astype(F32), cun_ref[0, 0:8].astype(F32)], axis=0)
        gate = jnp.concatenate([cg_ref[0].astype(F32), cgn_ref[0, 0:8].astype(F32)], axis=0)
        sg = _sigmoid(gate)
        gs = gate * sg
        d_up = dact * gs
        d_gate = dact * up * (sg + gs * (1.0 - sg))

        def finish(b, a_ref, dc):
            w = cw_ref[b]
            dm, u1, u2 = dc[:tm], _shift_rows(dc, -1)[:tm], _shift_rows(dc, -2)[:tm]
            da_ref[b] = (w[2:3] * dm + w[1:2] * u1 + w[0:1] * u2).astype(BF16)
            av = a_ref[0].astype(F32)
            st_ref[b, 0:1, :] += jnp.sum(u2 * av, axis=0, keepdims=True)
            st_ref[b, 1:2, :] += jnp.sum(u1 * av, axis=0, keepdims=True)
            st_ref[b, 2:3, :] += jnp.sum(dm * av, axis=0, keepdims=True)
            st_ref[b, 3:4, :] += jnp.sum(dm, axis=0, keepdims=True)

        finish(j, au_ref, d_up)
        finish(j + 4, ag_ref, d_gate)

    nxt = lambda i: jnp.minimum((i + 1) * hb, T // 16 - 1)
    return _call(
        body, name=name, grid=(nt, 4),
        in_specs=[pl.BlockSpec((tm, D), lambda i, j: (i, 0)),
                  pl.BlockSpec((8, D), lambda i, j: (jnp.minimum((i + 1) * (tm // 8), T // 8 - 1), 0)),
                  pl.BlockSpec((1, tm, F), lambda i, j: (j, i, 0)), pl.BlockSpec((1, tm, F), lambda i, j: (j + 4, i, 0)),
                  pl.BlockSpec((1, 16, F), lambda i, j: (j, nxt(i), 0)),
                  pl.BlockSpec((1, 16, F), lambda i, j: (j + 4, nxt(i), 0)),
                  pl.BlockSpec((1, tm, F), lambda i, j: (j, i, 0)), pl.BlockSpec((1, tm, F), lambda i, j: (j + 4, i, 0)),
                  pl.BlockSpec((8, 3, F), lambda i, j: (0, 0, 0)),
                  pl.BlockSpec((1, F, D), lambda i, j: (j, 0, 0))],
        out_specs=[pl.BlockSpec((8, tm, F), lambda i, j: (0, i, 0)), pl.BlockSpec((8, 8, F), lambda i, j: (0, 0, 0))],
        out_shape=[jax.ShapeDtypeStruct((8, T, F), BF16), jax.ShapeDtypeStruct((8, 8, F), F32)],
        args=(dh, dh, c, c, c, c, a, a, cw, wd), rider=rider)


def _ffn_fused_forward(h, g, w_in, cw, cb, wd, seq, *, name, rider=None):
    T, D = h.shape
    F = w_in.shape[2]
    tm = _tile(seq, TOKEN_TILE // 2)
    bps = 4
    nj = 4 // bps

    def body(h_ref, g_ref, wu_ref, wg_ref, cw_ref, cb_ref, wd_ref,
             au_ref, ag_ref, cu_ref, cg_ref, act_ref, n_ref, o_ref, acc_ref, carry_ref, *work_refs):
        eu_refs, eg_refs, stage_refs = work_refs[:bps], work_refs[bps:2 * bps], work_refs[2 * bps:]
        i, j = pl.program_id(0), pl.program_id(1)
        keep = ((i * tm) % seq != 0).astype(F32)

        @pl.when((i == 0) & (j == 0))
        def _():
            carry_ref[...] = jnp.zeros_like(carry_ref)

        @pl.when(j == 0)
        def _():
            xh, _ = _rms(h_ref[...], None)
            n_ref[...] = (xh * g_ref[...]).astype(BF16)
            acc_ref[...] = h_ref[...]

        n = n_ref[...]

        def project(b, k, w_ref, a_ref, ext_ref):
            a = _dot(n, w_ref[k]).astype(BF16)
            a_ref[k] = a
            ext_ref[0:8, :] = carry_ref[b] * keep
            ext_ref[8:, :] = a.astype(F32)
            carry_ref[b] = ext_ref[tm:tm + 8, :]

        def conv(b, ext_ref, r):
            x, w = ext_ref[r:r + ROWS + 8, :], cw_ref[b]
            return (w[0:1] * _shift_rows(x, 2) + w[1:2] * _shift_rows(x, 1) + w[2:3] * x)[8:] + cb_ref[b]

        for k in range(bps):
            project(j * bps + k, k, wu_ref, au_ref, eu_refs[k])
            project(j * bps + k + 4, k, wg_ref, ag_ref, eg_refs[k])
        outs = []
        for k in range(bps):
            for r in range(0, tm, ROWS):
                up, gate = conv(j * bps + k, eu_refs[k], r), conv(j * bps + k + 4, eg_refs[k], r)
                cu_ref[k, r:r + ROWS, :] = up.astype(BF16)
                cg_ref[k, r:r + ROWS, :] = gate.astype(BF16)
                stage_refs[k][r:r + ROWS, :] = (gate * _sigmoid(gate) * up).astype(BF16)
            act = stage_refs[k][...]
            act_ref[k] = act
            outs.append(_dot(act, wd_ref[k]))
        acc_ref[...] += functools.reduce(lambda p, q: p + q, outs)

        @pl.when(j == nj - 1)
        def _():
            o_ref[...] = acc_ref[...]

    blk = pl.BlockSpec((bps, tm, F), lambda i, j: (j, i, 0))
    row = pl.BlockSpec((tm, D), lambda i, j: (i, 0))
    half = jax.ShapeDtypeStruct((4, T, F), BF16)
    work = [pltpu.VMEM((tm + 8, F), F32)] * (2 * bps) + [pltpu.VMEM((tm, F), BF16)] * bps
    return _call(
        body, name=name, grid=(T // tm, nj),
        in_specs=[row, pl.BlockSpec((1, D), lambda i, j: (0, 0)),
                  pl.BlockSpec((bps, D, F), lambda i, j: (j, 0, 0), pipeline_mode=pl.Buffered(1)),
                  pl.BlockSpec((bps, D, F), lambda i, j: (j + nj, 0, 0), pipeline_mode=pl.Buffered(1)),
                  pl.BlockSpec((8, 3, F), lambda i, j: (0, 0, 0)), pl.BlockSpec((8, 1, F), lambda i, j: (0, 0, 0)),
                  pl.BlockSpec((bps, F, D), lambda i, j: (j, 0, 0), pipeline_mode=pl.Buffered(1))],
        out_specs=[blk, blk, blk, blk, blk, row, row],
        out_shape=[half, half, half, half, half, jax.ShapeDtypeStruct((T, D), BF16), jax.ShapeDtypeStruct((T, D), F32)],
        scratch_shapes=[pltpu.VMEM((tm, D), F32), pltpu.VMEM((8, 8, F), F32)] + work,
        args=(h, g, w_in, w_in, cw, cb, wd), rider=rider)


def _ffn_fused_backward(dh, cu, cg, au, ag, cw, wd, w_in, h, g, seq, *, name, rider=None):
    T, D = dh.shape
    F = wd.shape[1]
    tm = _tile(seq, TOKEN_TILE // 2)
    hb = tm // 16
    nt = T // tm
    bps = FFN_BLOCKS_PER_STEP
    nj = 4 // bps

    def body(dh_ref, dhn_ref, cu_ref, cg_ref, cun_ref, cgn_ref, au_ref, ag_ref, cw_ref, wd_ref, wu_ref, wg_ref,
             h_ref, g_ref, dau_ref, dag_ref, st_ref, o_ref, dg_ref, acc_ref, dact_ref, du_ref, dgt_ref):
        j, i = pl.program_id(0), pl.program_id(1)
        keep_next = (((i + 1) * tm) % seq != 0).astype(F32)
        tile_rows = pl.ds(pl.multiple_of(i * tm, tm), tm)

        @pl.when((i == 0) & (j == 0))
        def _():
            st_ref[...] = jnp.zeros_like(st_ref)
            dg_ref[...] = jnp.zeros_like(dg_ref)

        dhe = jnp.concatenate([dh_ref[...], dhn_ref[...] * keep_next], axis=0).astype(BF16)
        for k in range(bps):
            dact_ref[k] = _dot_nt(dhe, wd_ref[k])

        def conv_grads(k):
            for r in range(0, tm + 8, ROWS):
                if r < tm:
                    rows = slice(r, r + ROWS)
                    up, gate = cu_ref[k, rows, :].astype(F32), cg_ref[k, rows, :].astype(F32)
                else:
                    rows = slice(tm, tm + 8)
                    up, gate = cun_ref[k, 0:8, :].astype(F32), cgn_ref[k, 0:8, :].astype(F32)
                dact = dact_ref[k, rows, :]
                sg = _sigmoid(gate)
                gs = gate * sg
                du_ref[k, rows, :] = dact * gs
                dgt_ref[k, rows, :] = dact * up * (sg + gs * (1.0 - sg))

        def finish(b, k, a_ref, w_ref, dc_ref, da_ref):
            w = cw_ref[b]
            sums = [jnp.zeros((8, F), F32) for _ in range(4)]
            fold = lambda t: jnp.sum(t.reshape(ROWS // 8, 8, F), axis=0)
            for r in range(0, tm, ROWS):
                dc = dc_ref[k, r:r + ROWS + 8, :]
                dm, u1, u2 = dc[:ROWS], _shift_rows(dc, -1)[:ROWS], _shift_rows(dc, -2)[:ROWS]
                da_ref[k, r:r + ROWS, :] = (w[2:3] * dm + w[1:2] * u1 + w[0:1] * u2).astype(BF16)
                av = a_ref[k, r:r + ROWS, :].astype(F32)
                for s, t in enumerate((u2 * av, u1 * av, dm * av, dm)):
                    sums[s] = sums[s] + fold(t)
            for s in range(4):
                st_ref[b, s:s + 1, :] += jnp.sum(sums[s], axis=0, keepdims=True)
            return _dot_nt(da_ref[k], w_ref[k])

        dn_parts = []
        for k in range(bps):
            conv_grads(k)
            dn_parts.append(finish(j * bps + k, k, au_ref, wu_ref, du_ref, dau_ref))
            dn_parts.append(finish(j * bps + k + 4, k, ag_ref, wg_ref, dgt_ref, dag_ref))
        dn_part = functools.reduce(lambda p, q: p + q, dn_parts)

        @pl.when(j == 0)
        def _():
            acc_ref[tile_rows, :] = dn_part

        @pl.when(j > 0)
        def _():
            acc_ref[tile_rows, :] += dn_part

        @pl.when(j == nj - 1)
        def _():
            acc = acc_ref[tile_rows, :]
            xh, r = _rms(h_ref[...], None)
            dg_ref[0:1, :] += jnp.sum(acc * xh, axis=0, keepdims=True)
            dn = acc * g_ref[...]
            o_ref[...] = dh_ref[...] + r * (dn - xh * jnp.mean(dn * xh, axis=-1, keepdims=True))

    last = lambda j, i: jnp.where(j == nj - 1, i, 0)
    nxt = lambda i: jnp.minimum((i + 1) * hb, T // 16 - 1)
    blk = pl.BlockSpec((bps, tm, F), lambda j, i: (j, i, 0))
    halo = pl.BlockSpec((bps, 16, F), lambda j, i: (j, nxt(i), 0))
    row = pl.BlockSpec((tm, D), lambda j, i: (i, 0))
    work = pltpu.VMEM((bps, tm + 8, F), F32)
    half = jax.ShapeDtypeStruct((4, T, F), BF16)
    return _call(
        body, name=name, grid=(nj, nt),
        in_specs=[row, pl.BlockSpec((8, D), lambda j, i: (jnp.minimum((i + 1) * (tm // 8), T // 8 - 1), 0)),
                  blk, blk, halo, halo, blk, blk,
                  pl.BlockSpec((8, 3, F), lambda j, i: (0, 0, 0)), pl.BlockSpec((bps, F, D), lambda j, i: (j, 0, 0)),
                  pl.BlockSpec((bps, D, F), lambda j, i: (j, 0, 0)),
                  pl.BlockSpec((bps, D, F), lambda j, i: (j + nj, 0, 0)),
                  pl.BlockSpec((tm, D), lambda j, i: (last(j, i), 0)), pl.BlockSpec((1, D), lambda j, i: (0, 0))],
        out_specs=[blk, blk, pl.BlockSpec((8, 8, F), lambda j, i: (0, 0, 0)),
                   pl.BlockSpec((tm, D), lambda j, i: (last(j, i), 0)), pl.BlockSpec((8, D), lambda j, i: (0, 0))],
        out_shape=[half, half, jax.ShapeDtypeStruct((8, 8, F), F32),
                   jax.ShapeDtypeStruct((T, D), F32), jax.ShapeDtypeStruct((8, D), F32)],
        scratch_shapes=[pltpu.VMEM((T, D), F32), work, work, work],
        args=(dh, dh, cu, cg, cu, cg, au, ag, cw, wd, w_in, w_in, h, g), rider=rider)


def _rel_onehot():
    r = lax.broadcasted_iota(jnp.int32, (REL_PAD, SKEW), 0)
    n = lax.broadcasted_iota(jnp.int32, (REL_PAD, SKEW), 1)
    off = jnp.where(n >= WIN, n - SKEW, n)
    idx = jnp.minimum(PAD - off, REL_CLIP) + REL_CLIP
    return (r == idx).astype(BF16)


def _skew(x, sign):
    row = lax.broadcasted_iota(jnp.int32, x.shape, 0)
    for b in range(7):
        x = jnp.where((row >> b) & 1 == 1, pltpu.roll(x, (sign * (1 << b)) % SKEW, axis=1), x)
    return x


def _bias_build(rel, name, rider=None):
    H = rel.shape[0]

    def body(rel_ref, o_ref):
        oh = _rel_onehot()
        hi, mid, lo = _split3(rel_ref[...])
        base = _dot(hi, oh) + _dot(mid, oh) + _dot(lo, oh)
        mine = lax.broadcasted_iota(jnp.int32, (H, 1), 0) == pl.program_id(0)
        row = jnp.sum(jnp.where(mine, base, 0.0), axis=0, keepdims=True)
        q = lax.broadcasted_iota(jnp.int32, (Q_TILE, WIN), 0)
        k = lax.broadcasted_iota(jnp.int32, (Q_TILE, WIN), 1)
        ok = ((q < CHUNK) & (k < WIN - CHUNK)) | ((q >= CHUNK) & (k >= CHUNK))
        t = _skew(jnp.broadcast_to(row, (Q_TILE, SKEW)), 1)
        o_ref[0] = jnp.where(ok, t[:, :WIN], NEG_INF)

    return _call(
        body, name=name, grid=(H,), in_specs=[pl.BlockSpec((H, REL_PAD), lambda h: (0, 0))],
        out_specs=pl.BlockSpec((1, Q_TILE, WIN), lambda h: (h, 0, 0)),
        out_shape=jax.ShapeDtypeStruct((H, Q_TILE, WIN), F32), args=(rel,), rider=rider)


def _bias_reduce(dbias, name):
    H = dbias.shape[0]

    def body(d_ref, o_ref, e_ref):
        oh = _rel_onehot()
        for hd in range(H):
            x = jnp.concatenate([d_ref[hd], jnp.zeros((Q_TILE, SKEW - WIN), F32)], axis=1)
            e_ref[hd:hd + 1, :] = jnp.sum(_skew(x, -1), axis=0, keepdims=True)
        hi, mid, lo = _split3(e_ref[...])
        o_ref[...] = _dot_nt(hi, oh) + _dot_nt(mid, oh) + _dot_nt(lo, oh)

    return pl.pallas_call(
        body, name=name, out_shape=jax.ShapeDtypeStruct((H, REL_PAD), F32),
        in_specs=[pl.BlockSpec(memory_space=pltpu.VMEM)], out_specs=pl.BlockSpec(memory_space=pltpu.VMEM),
        scratch_shapes=[pltpu.VMEM((H, SKEW), F32)],
        compiler_params=_params(),
    )(dbias)


def _pair_stack(xp, even):
    z = jnp.zeros_like(xp)
    return jnp.concatenate([jnp.where(even, xp, z), jnp.where(even, z, xp)], axis=0)


def _pair_merge(y, even):
    return jnp.where(even, y[:Q_TILE], y[Q_TILE:])


def _strip_probs(s_ref, b_ref, pp, r, valid, base=0):
    hb, hr = divmod(r, Q_TILE)
    s = s_ref[base + pp, r:r + STRIP, :] + b_ref[2 * pp + hb, hr:hr + STRIP, :]
    s = jnp.where(valid, s, NEG_INF)
    e = jnp.exp(s - jnp.max(s, axis=-1, keepdims=True))
    return e * (1.0 / jnp.sum(e, axis=-1, keepdims=True))


def _fill_padded(dst_ref, src_ref):
    dst_ref[0:PAD, :] = jnp.zeros((PAD, dst_ref.shape[1]), dst_ref.dtype)
    dst_ref[PAD:, :] = src_ref[...]


def _attn_specs(B, S, D, lanes):
    nt = S // (TILES_PER_STEP * Q_TILE)
    q_spec = pl.BlockSpec((TILES_PER_STEP * Q_TILE, lanes), lambda g, b, i: (b * nt + i, g))
    k_spec = pl.BlockSpec((S, lanes), lambda g, b, i: (b, g))
    v_spec = pl.BlockSpec((S, lanes), lambda g, b, i: (b, D // lanes + g))
    bias_spec = pl.BlockSpec((lanes // HEAD_DIM, Q_TILE, WIN), lambda g, b, i: (g, 0, 0))
    return nt, q_spec, k_spec, v_spec, bias_spec


def _attn_forward(q, kv, bias, S, *, name, rider=None):
    T, D = q.shape
    B = T // S
    lanes = min(FWD_HEADS_PER_STEP * HEAD_DIM, D)
    nt, q_spec, k_spec, v_spec, bias_spec = _attn_specs(B, S, D, lanes)

    npairs = lanes // (2 * HEAD_DIM)

    def body(q_ref, k_ref, v_ref, b_ref, o_ref, kp_ref, vp_ref, s_ref, p_ref):
        i = pl.program_id(2)

        @pl.when(i == 0)
        def _():
            _fill_padded(kp_ref, k_ref)
            _fill_padded(vp_ref, v_ref)

        even = lax.broadcasted_iota(jnp.int32, (1, 2 * HEAD_DIM), 1) < HEAD_DIM
        pair_cols = [slice(pp * 2 * HEAD_DIM, (pp + 1) * 2 * HEAD_DIM) for pp in range(npairs)]
        for t in range(TILES_PER_STEP):
            tile = i * TILES_PER_STEP + t
            start = pl.multiple_of(tile * Q_TILE, Q_TILE)
            rows = slice(t * Q_TILE, (t + 1) * Q_TILE)
            valid = lax.broadcasted_iota(jnp.int32, (STRIP, WIN), 1) >= PAD - tile * Q_TILE
            for pp, cols in enumerate(pair_cols):
                s_ref[t * npairs + pp] = _dot_nt(_pair_stack(q_ref[rows, cols], even), kp_ref[pl.ds(start, WIN), cols])
            for pp in range(npairs):
                for r in range(0, 2 * Q_TILE, STRIP):
                    p = _strip_probs(s_ref, b_ref, pp, r, valid, base=t * npairs)
                    p_ref[t * npairs + pp, r:r + STRIP, :] = p.astype(BF16)
            for pp, cols in enumerate(pair_cols):
                o = _dot(p_ref[t * npairs + pp], vp_ref[pl.ds(start, WIN), cols])
                o_ref[rows, cols] = _pair_merge(o, even).astype(BF16)

    nbuf = TILES_PER_STEP * npairs
    return _call(
        body, name=name, grid=(D // lanes, B, nt),
        in_specs=[q_spec, k_spec, v_spec, bias_spec], out_specs=q_spec,
        out_shape=jax.ShapeDtypeStruct((T, D), BF16),
        scratch_shapes=[pltpu.VMEM((S + PAD, lanes), BF16), pltpu.VMEM((S + PAD, lanes), BF16),
                        pltpu.VMEM((nbuf, 2 * Q_TILE, WIN), F32), pltpu.VMEM((nbuf, 2 * Q_TILE, WIN), BF16)],
        args=(q, kv, kv, bias), rider=rider)


def _attn_backward(q, kv, bias, do, S, *, name, rider=None):
    T, D = q.shape
    B = T // S
    H = D // HEAD_DIM
    lanes = min(BWD_HEADS_PER_STEP * HEAD_DIM, D)
    nt, q_spec, k_spec, v_spec, bias_spec = _attn_specs(B, S, D, lanes)
    scale = HEAD_DIM ** -0.5

    npairs = lanes // (2 * HEAD_DIM)

    def body(q_ref, k_ref, v_ref, b_ref, do_ref, dq_ref, dk_ref, dv_ref, db_ref, kp_ref, vp_ref, dka_ref, dva_ref,
             s_ref, dp_ref, p_ref, ds_ref):
        b, i = pl.program_id(1), pl.program_id(2)

        @pl.when((b == 0) & (i == 0))
        def _():
            db_ref[...] = jnp.zeros_like(db_ref)

        @pl.when(i == 0)
        def _():
            _fill_padded(kp_ref, k_ref)
            _fill_padded(vp_ref, v_ref)
            dka_ref[...] = jnp.zeros_like(dka_ref)
            dva_ref[...] = jnp.zeros_like(dva_ref)

        even = lax.broadcasted_iota(jnp.int32, (1, 2 * HEAD_DIM), 1) < HEAD_DIM
        pair_cols = [slice(pp * 2 * HEAD_DIM, (pp + 1) * 2 * HEAD_DIM) for pp in range(npairs)]
        for t in range(TILES_PER_STEP):
            tile = i * TILES_PER_STEP + t
            start = pl.multiple_of(tile * Q_TILE, Q_TILE)
            rows = slice(t * Q_TILE, (t + 1) * Q_TILE)
            valid = lax.broadcasted_iota(jnp.int32, (STRIP, WIN), 1) >= PAD - tile * Q_TILE
            base = t * npairs
            for pp, cols in enumerate(pair_cols):
                s_ref[base + pp] = _dot_nt(_pair_stack(q_ref[rows, cols], even), kp_ref[pl.ds(start, WIN), cols])
                dp_ref[base + pp] = _dot_nt(_pair_stack(do_ref[rows, cols], even), vp_ref[pl.ds(start, WIN), cols])
            for pp in range(npairs):
                for r in range(0, 2 * Q_TILE, STRIP):
                    hb, hr = divmod(r, Q_TILE)
                    p = _strip_probs(s_ref, b_ref, pp, r, valid, base=base)
                    dp = dp_ref[base + pp, r:r + STRIP, :]
                    ds = p * (dp - jnp.sum(p * dp, axis=-1, keepdims=True))
                    db_ref[2 * pp + hb, hr:hr + STRIP, :] += ds
                    p_ref[base + pp, r:r + STRIP, :] = p.astype(BF16)
                    ds_ref[base + pp, r:r + STRIP, :] = ds.astype(BF16)
            for pp, cols in enumerate(pair_cols):
                dsb = ds_ref[base + pp]
                dq = _pair_merge(_dot(dsb, kp_ref[pl.ds(start, WIN), cols]), even) * scale
                dq_ref[rows, cols] = dq.astype(BF16)
                dka_ref[pl.ds(start, WIN), cols] += _dot_tn(dsb, _pair_stack(q_ref[rows, cols], even))
                dva_ref[pl.ds(start, WIN), cols] += _dot_tn(p_ref[base + pp], _pair_stack(do_ref[rows, cols], even))

        @pl.when(i == nt - 1)
        def _():
            dk_ref[...] = dka_ref[PAD:, :].astype(BF16)
            dv_ref[...] = dva_ref[PAD:, :].astype(BF16)

    dkv_shape = jax.ShapeDtypeStruct((T, D), BF16)
    nbuf = TILES_PER_STEP * npairs
    return _call(
        body, name=name, grid=(D // lanes, B, nt),
        in_specs=[q_spec, k_spec, v_spec, bias_spec, q_spec],
        out_specs=[q_spec, k_spec, k_spec, bias_spec],
        out_shape=[jax.ShapeDtypeStruct((T, D), BF16), dkv_shape, dkv_shape,
                   jax.ShapeDtypeStruct((H, Q_TILE, WIN), F32)],
        scratch_shapes=[pltpu.VMEM((S + PAD, lanes), BF16), pltpu.VMEM((S + PAD, lanes), BF16),
                        pltpu.VMEM((S + PAD, lanes), F32), pltpu.VMEM((S + PAD, lanes), F32),
                        pltpu.VMEM((nbuf, 2 * Q_TILE, WIN), F32), pltpu.VMEM((nbuf, 2 * Q_TILE, WIN), F32),
                        pltpu.VMEM((nbuf, 2 * Q_TILE, WIN), BF16), pltpu.VMEM((nbuf, 2 * Q_TILE, WIN), BF16)],
        args=(q, kv, kv, bias, do), rider=rider)


def _loss_head(h, g, target, name):
    T, D = h.shape
    tm = _tile(T, MATMUL_TILE)

    def body(h_ref, g_ref, t_ref, dh_ref, st_ref):
        @pl.when(pl.program_id(0) == 0)
        def _():
            st_ref[...] = jnp.zeros_like(st_ref)

        xh, r = _rms(h_ref[...], None)
        err = xh * g_ref[...] - t_ref[...]
        st_ref[1:2, :] += 0.5 * jnp.sum(jnp.mean(err * err, axis=-1, keepdims=True), axis=0, keepdims=True)
        dy = err * (1.0 / D)
        st_ref[0:1, :] += jnp.sum(dy * xh, axis=0, keepdims=True)
        dn = dy * g_ref[...]
        dh_ref[...] = r * (dn - xh * jnp.mean(dn * xh, axis=-1, keepdims=True))

    row = pl.BlockSpec((tm, D), lambda i: (i, 0))
    return pl.pallas_call(
        body, name=name, grid=(T // tm,),
        in_specs=[row, pl.BlockSpec((1, D), lambda i: (0, 0)), row],
        out_specs=[row, pl.BlockSpec((8, D), lambda i: (0, 0))],
        out_shape=[jax.ShapeDtypeStruct((T, D), F32), jax.ShapeDtypeStruct((8, D), F32)],
        compiler_params=_params(("arbitrary",)),
    )(h, g, target)


def _sum_devices(arrs, name):
    n = len(arrs)

    def body(*refs):
        for a in range(n):
            s = refs[a][0].astype(F32)
            for k in range(1, N_DEV):
                s = s + refs[a][k].astype(F32)
            refs[n + a][...] = s

    vm = pl.BlockSpec(memory_space=pltpu.VMEM)
    return pl.pallas_call(
        body, name=name, out_shape=[jax.ShapeDtypeStruct(a.shape[1:], F32) for a in arrs],
        in_specs=[vm] * n, out_specs=[vm] * n, compiler_params=_params(),
    )(*arrs)


def _adamw_math(w, g, m, v):
    m = ADAM_B1 * m + (1.0 - ADAM_B1) * g
    v = ADAM_B2 * v + (1.0 - ADAM_B2) * (g * g)
    m_hat = m / (1.0 - ADAM_B1 ** ADAM_STEP)
    v_hat = v / (1.0 - ADAM_B2 ** ADAM_STEP)
    delta = -ADAM_LR * (m_hat / (jnp.sqrt(v_hat) + ADAM_EPS) + ADAM_WD * w)
    return delta, m, v


def _adamw_small(items, name):
    n = len(items)

    def body(*refs):
        for a in range(n):
            w, m, v, g = (refs[4 * a + k][...] for k in range(4))
            d, m, v = _adamw_math(w, g, m, v)
            refs[4 * n + 3 * a][...] = d
            refs[4 * n + 3 * a + 1][...] = m
            refs[4 * n + 3 * a + 2][...] = v

    vm = pl.BlockSpec(memory_space=pltpu.VMEM)
    flat = [t for it in items for t in it]
    outs = pl.pallas_call(
        body, name=name,
        out_shape=[jax.ShapeDtypeStruct(it[0].shape, F32) for it in items for _ in range(3)],
        in_specs=[vm] * (4 * n), out_specs=[vm] * (3 * n), compiler_params=_params(),
    )(*flat)
    return [tuple(outs[3 * a:3 * a + 3]) for a in range(n)]


def _adamw_big(w, m, v, owns, landeds, name, rider=None):
    L, R, C = w.shape
    tr = _tile(R, 512)
    nr = R // tr
    counts = [len(ls) for ls in landeds]

    def body(*refs):
        w_ref, m_ref, v_ref = refs[:3]
        g_ref, d_ref, mo_ref, vo_ref = refs[-4:]
        layer = pl.program_id(0)
        at = 3
        for j in range(L):
            own_ref, l_refs = refs[at], refs[at + 1:at + 1 + counts[j]]
            at += 1 + counts[j]

            @pl.when(layer == j)
            def _(own_ref=own_ref, l_refs=l_refs):
                g = own_ref[...]
                for l_ref in l_refs:
                    for k in range(l_ref.shape[0]):
                        g = g + l_ref[k].astype(F32)
                d, mn, vn = _adamw_math(w_ref[0], g, m_ref[0], v_ref[0])
                g_ref[0] = g
                d_ref[0] = d
                mo_ref[0] = mn
                vo_ref[0] = vn

    def pinned(j):
        return lambda l, i: jnp.where(l == j, i, jnp.where(l < j, 0, nr - 1))

    row = pl.BlockSpec((1, tr, C), lambda l, i: (l, i, 0))
    in_specs, args = [row, row, row], [w, m, v]
    for j in range(L):
        in_specs.append(pl.BlockSpec((tr, C), lambda l, i, p=pinned(j): (p(l, i), 0)))
        args.append(owns[j])
        for arr in landeds[j]:
            in_specs.append(pl.BlockSpec((arr.shape[0], tr, C), lambda l, i, p=pinned(j): (0, p(l, i), 0)))
            args.append(arr)
    return _call(body, name=name, grid=(L, nr), in_specs=in_specs, out_specs=[row] * 4,
                 out_shape=[jax.ShapeDtypeStruct((L, R, C), F32)] * 4, args=args, rider=rider)


def kernel(x, a_norm_g, a_w_in, a_v_norm_g, a_w_s, a_b_s, a_w_out, kv_norm_g, w_kv, b_norm_g, b_w_q, b_rel_bias, b_w_o, f_norm_g, f_w_in, f_conv_w, f_conv_b, f_w_down, final_norm_g, loss_target, m_a_norm_g, m_a_w_in, m_a_v_norm_g, m_a_w_s, m_a_b_s, m_a_w_out, m_kv_norm_g, m_w_kv, m_b_norm_g, m_b_w_q, m_b_rel_bias, m_b_w_o, m_f_norm_g, m_f_w_in, m_f_conv_w, m_f_conv_b, m_f_w_down, m_final_norm_g, v_a_norm_g, v_a_w_in, v_a_v_norm_g, v_a_w_s, v_a_b_s, v_a_w_out, v_kv_norm_g, v_w_kv, v_b_norm_g, v_b_w_q, v_b_rel_bias, v_b_w_o, v_f_norm_g, v_f_w_in, v_f_conv_w, v_f_conv_b, v_f_w_down, v_final_norm_g):
    B, S, D = x.shape
    T = B * S
    G = a_w_s.shape[1]
    H = D // HEAD_DIM
    F = f_w_in.shape[2]
    L = f_w_in.shape[0]
    dn = D // N_DEV
    xi, yi, ci = lax.axis_index("x"), lax.axis_index("y"), lax.axis_index("c")
    me = 4 * xi + 2 * yi + ci
    pos = jnp.stack([ci, 2 * xi + yi]).astype(jnp.int32)

    cast = lambda t: t.astype(BF16)
    gather = lambda *ts: _gather_rider(list(ts))
    rel = jnp.pad(b_rel_bias[0], ((0, 0), (0, REL_PAD - b_rel_bias.shape[2])))
    bias, (wa_in, norms_sh, conv_w0, conv_w1) = _bias_build(rel, "bias_build", rider=gather(
        cast(a_w_in[0]), jnp.concatenate([a_norm_g, a_v_norm_g], axis=0), f_conv_w[0], f_conv_w[1]))
    ga = jnp.transpose(norms_sh, (1, 0, 2)).reshape(2, D)
    g_a, g_av = ga[0:1], ga[1:2]

    x2 = x.reshape(T, D)
    tgt = loss_target.reshape(T, D)
    pc = jnp.arange(GMLP_BLOCK) // CHUNK
    mask = (pc[:, None] >= pc[None, :]).astype(F32)
    ws = (a_w_s[0] * mask[None]).astype(BF16)
    bst = jnp.transpose(a_b_s[0])
    four = lambda t: t.reshape((4, 2) + t.shape[1:])

    w_in0_sh = cast(f_w_in[0])
    (z, n_a), (wa_out, w_in0_top) = _norm_matmul(x2, g_a, wa_in, flat=True, nbk=4, name="gmlp_in",
                                                 rider=gather(cast(a_w_out[0]), w_in0_sh[:D // 2]))
    wa_out = wa_out.reshape(D, D)
    (gated, h1), (w_in0_bottom, wf_down0) = _gmlp_forward(z, ws, bst, g_av, wa_out, x2, name="gmlp_mix",
                                                          rider=gather(w_in0_sh[D // 2:], cast(f_w_down[0])))
    w_in0 = jnp.concatenate([w_in0_top, w_in0_bottom], axis=1)
    cw0, cb0, wd0 = conv_w0, f_conv_b[0].reshape(8, 1, F), wf_down0.reshape(4, F, D)
    (au0, ag0, cu0, cg0, act0, n_f0, h2), (wkv, wq, w_in1) = _ffn_fused_forward(
        h1, f_norm_g[0:1], w_in0, cw0, cb0, wd0, S, name="ffn0_fwd",
        rider=gather(cast(w_kv), cast(b_w_q[0]), cast(f_w_in[1])))
    wq = wq.reshape(D, D)
    kv, n_kv = _norm_matmul(h2, kv_norm_g.reshape(1, D), wkv, flat=True, nbk=4, name="kv_proj")
    q, n_q = _norm_matmul(h2, b_norm_g, wq.reshape(1, D, D), flat=True, nbk=1, name="q_proj", scale=HEAD_DIM ** -0.5)
    o, (wo, wf_down1) = _attn_forward(q, kv, bias, S, name="attn", rider=gather(cast(b_w_o[0]), cast(f_w_down[1])))
    wo = wo.reshape(D, D)
    cw1, cb1, wd1 = conv_w1, f_conv_b[1].reshape(8, 1, F), wf_down1.reshape(4, F, D)
    h3 = _matmul_residual(o, wo, h2, "attn_out")
    au1, ag1, cu1, cg1, act1, n_f1, h4 = _ffn_fused_forward(h3, f_norm_g[1:2], w_in1, cw1, cb1, wd1, S, name="ffn1_fwd")

    sums, from_chips = {}, {}

    def sibling_sums(names, parts, landed):
        for nm, p, l in zip(names, parts, landed):
            sums[nm] = _sibling_sum(p, l, pos, "grad_sibling_sum_" + nm)

    def chip_rider(*names):
        return _chip_rider([sums[nm][1] for nm in names])

    dh4, st_final = _loss_head(h4, final_norm_g.reshape(1, D), tgt, "loss_head")
    g_wd1 = _wgrad_rows(act1, dh4, flat=False, tk=F, name="ffn1_dwdown")
    parts = [four(g_wd1.reshape(8, F // 2, D))]
    (dau1, dag1, st_conv1, dh3, st_f1), landed = _ffn_fused_backward(
        dh4, cu1, cg1, au1, ag1, cw1, wd1, w_in1, h3, f_norm_g[1:2], S, name="ffn1_bwd", rider=_sibling_rider(parts))
    sibling_sums(["wd1"], parts, landed)
    g_win1, (from_chips["wd1"],) = _wgrad_cols(n_f1, (dau1, dag1), flat=False, nb=8, nbk=2, name="ffn1_dwin",
                                               rider=chip_rider("wd1"))
    d_o = _matmul_nt(dh3, wo.reshape(1, D, D), flat=True, nbk=1, name="attn_out_dx")
    parts = [four(g_win1)]
    g_wo, landed = _wgrad_rows(o, dh3, flat=True, tk=_tile(D, 512), name="attn_out_dw", rider=_sibling_rider(parts))
    sibling_sums(["win1"], parts, landed)
    (dq, dk, dv, dbias), (from_chips["win1"],) = _attn_backward(
        q, kv, bias, d_o, S, name="attn_bwd", rider=chip_rider("win1"))
    g_rel = _bias_reduce(dbias, "bias_reduce")
    g_wq = _wgrad_cols(n_q, dq, flat=True, nb=1, nbk=1, name="q_dw")
    dh2, st_b = _matmul_nt(dq, wq.reshape(1, D, D), flat=True, nbk=1, name="q_dx", norm=(h2, b_norm_g, dh3))
    dkv = jnp.concatenate([dk, dv], axis=-1)
    g_wkv = _wgrad_cols(n_kv, dkv, flat=True, nb=8, nbk=4, name="kv_dw")
    parts = [four(g_wo.reshape(8, dn, D)), four(g_wq.reshape(8, dn, D)), four(g_wkv)]
    (dh2, st_kv), landed = _matmul_nt(dkv, wkv, flat=True, nbk=4, name="kv_dx",
                                      norm=(h2, kv_norm_g.reshape(1, D), dh2), rider=_sibling_rider(parts))
    sibling_sums(["wo", "wq", "wkv"], parts, landed)
    g_wd0, (from_chips["wo"], from_chips["wq"], from_chips["wkv"]) = _wgrad_rows(
        act0, dh2, flat=False, tk=F, name="ffn0_dwdown", rider=chip_rider("wo", "wq", "wkv"))
    parts = [four(g_wd0.reshape(8, F // 2, D))]
    (dau0, dag0, st_conv0, dh1, st_f0), landed = _ffn_fused_backward(
        dh2, cu0, cg0, au0, ag0, cw0, wd0, w_in0, h1, f_norm_g[0:1], S, name="ffn0_bwd", rider=_sibling_rider(parts))
    sibling_sums(["wd0"], parts, landed)
    g_win0, (ce,) = _wgrad_cols(n_f0, (dau0, dag0), flat=False, nb=8, nbk=2, name="ffn0_dwin",
                                rider=chip_rider("wd0"))
    from_chips["wd0"] = [ce]
    dgated = _matmul_nt(dh1, wa_out.reshape(1, D, D), flat=True, nbk=1, name="gmlp_out_dx")
    parts = [four(g_win0)]
    g_wa_out, landed = _wgrad_rows(gated, dh1, flat=True, tk=_tile(D, 512), name="gmlp_out_dw",
                                   rider=_sibling_rider(parts))
    sibling_sums(["win0"], parts, landed)
    parts = [four(g_wa_out.reshape(8, dn, D))]
    (dz, g_ws, g_bst, st_av), (ce_win0_a, landed) = _gmlp_backward(
        z, dgated, ws, bst, g_av, mask, name="gmlp_bwd",
        rider=_join_riders([_chip_rider([sums["win0"][1]], ks=(1, 2)), _sibling_rider(parts)]))
    sibling_sums(["wa_out"], parts, [landed])
    g_wa_in, (ce_win0_b, ce) = _wgrad_cols(n_a, dz, flat=True, nb=8, nbk=4, name="gmlp_in_dw", rider=_join_riders(
        [_chip_rider([sums["win0"][1]], ks=(3,)), chip_rider("wa_out")]))
    from_chips["win0"], from_chips["wa_out"] = [ce_win0_a, ce_win0_b], [ce]
    vec = jnp.concatenate([st_av[0:1], st_kv[0:1], st_b[0:1], st_f0[0:1], st_f1[0:1], st_final[0:3]], axis=0)
    parts = [four(g_wa_in)]
    (grad_x, st_a), got = _matmul_nt(dz, wa_in, flat=True, nbk=4, name="gmlp_in_dx", norm=(x2, g_a, dh1),
                                     rider=_join_riders([_sibling_rider(parts), gather(
                                         vec, cast(g_ws), cast(g_bst), cast(g_rel), cast(st_conv0), cast(st_conv1))]))
    sibling_sums(["wa_in"], parts, got[0:1])
    small = got[1:]

    def big_update(names, w, m, v, rider=None):
        shape = w.shape
        r = lambda t: t.reshape((len(names), -1, shape[-1]))
        as_list = lambda t: t if isinstance(t, list) else [t]
        outs = _adamw_big(r(w), r(m), r(v), [sums[nm][0] for nm in names],
                          [as_list(from_chips[nm]) for nm in names], "adamw_" + names[0], rider=rider)
        outs, got = (outs, None) if rider is None else outs
        return [t.reshape(shape) for t in outs], got

    u_f_w_in, (ce, st_a) = big_update(["win0", "win1"], f_w_in, m_f_w_in, v_f_w_in,
                                      rider=_join_riders([chip_rider("wa_in"), gather(st_a)]))
    from_chips["wa_in"] = [ce]
    u_f_w_down, _ = big_update(["wd0", "wd1"], f_w_down, m_f_w_down, v_f_w_down)
    u_a_w_in, _ = big_update(["wa_in"], a_w_in, m_a_w_in, v_a_w_in)
    u_w_kv, _ = big_update(["wkv"], w_kv, m_w_kv, v_w_kv)
    u_a_w_out, _ = big_update(["wa_out"], a_w_out, m_a_w_out, v_a_w_out)
    u_b_w_q, _ = big_update(["wq"], b_w_q, m_b_w_q, v_b_w_q)
    u_b_w_o, _ = big_update(["wo"], b_w_o, m_b_w_o, v_b_w_o)

    vec, g_ws, g_bst, g_rel, st_conv0, st_conv1, st_a = _sum_devices(list(small) + [st_a], "sum_small_grads")
    vec = jnp.concatenate([st_a[0:1], vec[0:7]], axis=0)
    loss = vec[7, 0]
    g_a_norm = lax.dynamic_slice_in_dim(vec[0:1], me * dn, dn, axis=1)
    g_av_norm = lax.dynamic_slice_in_dim(vec[1:2], me * dn, dn, axis=1)
    st_conv = jnp.stack([st_conv0, st_conv1])
    g_conv_w = lax.dynamic_index_in_dim(st_conv, me, axis=1, keepdims=False)[:, 0:3]
    g_conv_b = st_conv[:, :, 3, :].reshape(L, 8 * F)
    small_items = [
        (a_norm_g, m_a_norm_g, v_a_norm_g, g_a_norm),
        (a_v_norm_g, m_a_v_norm_g, v_a_v_norm_g, g_av_norm),
        (a_w_s, m_a_w_s, v_a_w_s, g_ws[None]),
        (a_b_s, m_a_b_s, v_a_b_s, jnp.transpose(g_bst)[None]),
        (kv_norm_g.reshape(1, D), m_kv_norm_g.reshape(1, D), v_kv_norm_g.reshape(1, D), vec[2:3]),
        (b_norm_g, m_b_norm_g, v_b_norm_g, vec[3:4]),
        (b_rel_bias, m_b_rel_bias, v_b_rel_bias, g_rel[None, :, :b_rel_bias.shape[2]]),
        (f_norm_g, m_f_norm_g, v_f_norm_g, vec[4:6]),
        (f_conv_w, m_f_conv_w, v_f_conv_w, g_conv_w),
        (f_conv_b, m_f_conv_b, v_f_conv_b, g_conv_b),
        (final_norm_g.reshape(1, D), m_final_norm_g.reshape(1, D), v_final_norm_g.reshape(1, D), vec[6:7]),
    ]
    small_out = _adamw_small(small_items, "adamw_small")
    (u_a_norm, u_av_norm, u_ws, u_bs, u_kvn, u_bn, u_rel, u_fn, u_cw, u_cb, u_fin) = [
        (it[3],) + so for it, so in zip(small_items, small_out)]
    vecD = lambda u: tuple(t.reshape(D) for t in u)
    u_kvn, u_fin = vecD(u_kvn), vecD(u_fin)

    order = [u_a_norm, u_a_w_in, u_av_norm, u_ws, u_bs, u_a_w_out, u_kvn, u_w_kv, u_bn, u_b_w_q, u_rel, u_b_w_o,
             u_fn, u_f_w_in, u_cw, u_cb, u_f_w_down, u_fin]
    outs = [loss, grad_x.reshape(B, S, D)]
    for k in range(4):
        outs += [u[k] for u in order]
    return tuple(outs)
```

```python
import functools

import jax
import jax.numpy as jnp
from jax import lax
from jax.experimental import pallas as pl
from jax.experimental.pallas import tpu as pltpu

F32 = jnp.float32
BF16 = jnp.bfloat16
MESH = pl.DeviceIdType.MESH

N_DEV = 8
EPS = 1e-6
NEG_INF = -1e30
CHUNK = 64
LEFT_CHUNKS = 8
REL_CLIP = 128
HEAD_DIM = 64
GMLP_BLOCK = 128
Q_TILE = 2 * CHUNK
PAD = LEFT_CHUNKS * CHUNK
WIN = PAD + Q_TILE
SKEW = WIN + Q_TILE
REL_PAD = 384
FWD_HEADS_PER_STEP = 8
BWD_HEADS_PER_STEP = 4
TILES_PER_STEP = 4
STRIP = 32
ROWS = 32
ADAM_LR, ADAM_B1, ADAM_B2, ADAM_EPS, ADAM_WD, ADAM_STEP = 0.001, 0.9, 0.999, 1e-08, 0.01, 10
VMEM_LIMIT = 60 * 1024 * 1024
TOKEN_TILE = 512
MATMUL_TILE = 1024
FFN_BLOCKS_PER_STEP = 4


def _params(sem=None):
    return pltpu.CompilerParams(dimension_semantics=sem, vmem_limit_bytes=VMEM_LIMIT)


def _tile(n, pref):
    if n <= pref:
        return n
    for t in range(pref - pref % 8, 7, -8):
        if n % t == 0:
            return t
    return n


def _gelu(x):
    return 0.5 * x * (1.0 + jnp.tanh(0.7978845608028654 * (x + 0.044715 * x * x * x)))


def _gelu_grad(x):
    t = jnp.tanh(0.7978845608028654 * (x + 0.044715 * x * x * x))
    return 0.5 * (1.0 + t) + 0.5 * x * (1.0 - t * t) * 0.7978845608028654 * (1.0 + 3 * 0.044715 * x * x)


def _sigmoid(x):
    return 1.0 / (1.0 + jnp.exp(-x))


def _dot(a, b):
    return jnp.dot(a, b, preferred_element_type=F32)


def _dot_nt(a, b):
    return lax.dot_general(a, b, (((1,), (1,)), ((), ())), preferred_element_type=F32)


def _dot_tn(a, b):
    return lax.dot_general(a, b, (((0,), (0,)), ((), ())), preferred_element_type=F32)


def _split3(x):
    hi = x.astype(BF16)
    r1 = x - hi.astype(F32)
    mid = r1.astype(BF16)
    lo = (r1 - mid.astype(F32)).astype(BF16)
    return hi, mid, lo


def _mesh_pos():
    return lax.axis_index("x"), lax.axis_index("y"), lax.axis_index("c")


class _Rider:
    def __init__(self, arrs, out_shapes, sems, start, finish):
        self.arrs, self.out_shapes, self.sems, self.start, self.finish = arrs, out_shapes, sems, start, finish


def _gather_rider(arrs):
    n = len(arrs)

    def tools(ins, outs, sems):
        send_sems, recv_sems, local_sems = sems
        x, y, c = _mesh_pos()
        me, sibling = (x, y, c), (x, y, 1 - c)
        chips = [(1 - x, y), (x, 1 - y), (1 - x, 1 - y)]

        def slot(a, block):
            px, py, pc = block
            return outs[a].at[4 * px + 2 * py + pc]

        def copy(a, k, block, to, src=None):
            dst = slot(a, block)
            return pltpu.make_async_remote_copy(
                src_ref=dst if src is None else src, dst_ref=dst,
                send_sem=send_sems.at[a, k], recv_sem=recv_sems.at[a, k], device_id=to, device_id_type=MESH)

        def first(a):
            cps = [copy(a, 0, me, sibling, src=ins[a])]
            return cps + [copy(a, 1 + j, me, (*chip, c), src=ins[a]) for j, chip in enumerate(chips)]

        def mine(a):
            return pltpu.make_async_copy(ins[a], slot(a, me), local_sems.at[a])

        return me, sibling, chips, c, copy, first, mine

    def start(ins, outs, sems):
        _, _, _, _, _, first, mine = tools(ins, outs, sems)
        for a in range(n):
            mine(a).start()
            for cp in first(a):
                cp.start()

    def finish(ins, outs, sems):
        me, sibling, chips, c, copy, first, mine = tools(ins, outs, sems)
        passed = []
        for j, chip in enumerate(chips):
            for a in range(n):
                copy(a, 1 + j, (*chip, c), me).wait_recv()
                fwd = copy(a, 4 + j, (*chip, c), sibling)
                fwd.start()
                passed.append(fwd)
        for a in range(n):
            copy(a, 0, sibling, me).wait_recv()
            for j, chip in enumerate(chips):
                copy(a, 4 + j, (*chip, 1 - c), me).wait_recv()
        for a in range(n):
            for cp in first(a):
                cp.wait_send()
        for cp in passed:
            cp.wait_send()
        for a in range(n):
            mine(a).wait()

    return _Rider(list(arrs), [jax.ShapeDtypeStruct((N_DEV,) + a.shape, a.dtype) for a in arrs],
                  [pltpu.SemaphoreType.DMA((n, 7)), pltpu.SemaphoreType.DMA((n, 7)), pltpu.SemaphoreType.DMA((n,))],
                  start, finish)


def _sibling_rider(arrs):
    n = len(arrs)

    def copies(ins, outs, sems):
        x, y, c = _mesh_pos()
        return [pltpu.make_async_remote_copy(
            src_ref=ins[a].at[:, pl.ds(1 - c, 1)], dst_ref=outs[a],
            send_sem=sems[0].at[a], recv_sem=sems[1].at[a], device_id=(x, y, 1 - c), device_id_type=MESH)
            for a in range(n)]

    def start(ins, outs, sems):
        for cp in copies(ins, outs, sems):
            cp.start()

    def finish(ins, outs, sems):
        for cp in copies(ins, outs, sems):
            cp.wait()

    return _Rider(list(arrs), [jax.ShapeDtypeStruct((4, 1) + a.shape[2:], a.dtype) for a in arrs],
                  [pltpu.SemaphoreType.DMA((n,)), pltpu.SemaphoreType.DMA((n,))], start, finish)


def _chip_rider(arrs, ks=(1, 2, 3)):
    n = len(arrs)

    def copies(ins, outs, sems):
        x, y, c = _mesh_pos()
        cps = []
        for a in range(n):
            for s, k in enumerate(ks):
                px = x if k < 2 else 1 - x
                py = y if k == 2 else 1 - y
                cps.append(pltpu.make_async_remote_copy(
                    src_ref=ins[a].at[2 * px + py], dst_ref=outs[a].at[s],
                    send_sem=sems[0].at[a, s], recv_sem=sems[1].at[a, s],
                    device_id=(px, py, c), device_id_type=MESH))
        return cps

    def start(ins, outs, sems):
        for cp in copies(ins, outs, sems):
            cp.start()

    def finish(ins, outs, sems):
        for cp in copies(ins, outs, sems):
            cp.wait()

    return _Rider(list(arrs), [jax.ShapeDtypeStruct((len(ks),) + a.shape[1:], a.dtype) for a in arrs],
                  [pltpu.SemaphoreType.DMA((n, len(ks))), pltpu.SemaphoreType.DMA((n, len(ks)))], start, finish)


def _join_riders(riders):
    def split(seq, counts):
        out, at = [], 0
        for k in counts:
            out.append(seq[at:at + k])
            at += k
        return out

    n_in = [len(r.arrs) for r in riders]
    n_out = [len(r.out_shapes) for r in riders]
    n_sem = [len(r.sems) for r in riders]

    def run(which):
        def fn(ins, outs, sems):
            for r, i, o, s in zip(riders, split(ins, n_in), split(outs, n_out), split(sems, n_sem)):
                getattr(r, which)(i, o, s)
        return fn

    return _Rider([a for r in riders for a in r.arrs], [o for r in riders for o in r.out_shapes],
                  [s for r in riders for s in r.sems], run("start"), run("finish"))


def _run_rider(rider, name):
    n_in, n_out = len(rider.arrs), len(rider.out_shapes)

    def body(*refs):
        ins, outs, sems = refs[:n_in], refs[n_in:n_in + n_out], refs[n_in + n_out:]
        rider.start(ins, outs, sems)
        rider.finish(ins, outs, sems)

    any_spec = pl.BlockSpec(memory_space=pl.ANY)
    return pl.pallas_call(
        body, name=name, out_shape=list(rider.out_shapes), in_specs=[any_spec] * n_in, out_specs=[any_spec] * n_out,
        scratch_shapes=list(rider.sems),
    )(*rider.arrs)


def _call(body, *, name, grid, in_specs, out_specs, out_shape, args, scratch_shapes=(), rider=None):
    params = _params(("arbitrary",) * len(grid))
    if rider is None:
        return pl.pallas_call(body, name=name, grid=grid, in_specs=in_specs, out_specs=out_specs, out_shape=out_shape,
                              scratch_shapes=list(scratch_shapes), compiler_params=params)(*args)
    single = not isinstance(out_shape, (list, tuple))
    outs = [out_shape] if single else list(out_shape)
    ospecs = [out_specs] if single else list(out_specs)
    n_in, n_out, n_scr = len(in_specs), len(outs), len(scratch_shapes)
    r_in, r_out = len(rider.arrs), len(rider.out_shapes)

    def hosted(*refs):
        refs = list(refs)
        ins, rins = refs[:n_in], refs[n_in:n_in + r_in]
        refs = refs[n_in + r_in:]
        houts, routs = refs[:n_out], refs[n_out:n_out + r_out]
        refs = refs[n_out + r_out:]
        scr, rsems = refs[:n_scr], refs[n_scr:]
        ids = [pl.program_id(a) for a in range(len(grid))]
        first = functools.reduce(lambda p, q: p & q, [i == 0 for i in ids])
        last = functools.reduce(lambda p, q: p & q, [i == g - 1 for i, g in zip(ids, grid)])

        @pl.when(first)
        def _():
            rider.start(rins, routs, rsems)

        body(*ins, *houts, *scr)

        @pl.when(last)
        def _():
            rider.finish(rins, routs, rsems)

    any_spec = pl.BlockSpec(memory_space=pl.ANY)
    res = pl.pallas_call(
        hosted, name=name, grid=grid, in_specs=list(in_specs) + [any_spec] * r_in,
        out_specs=ospecs + [any_spec] * r_out, out_shape=outs + list(rider.out_shapes),
        scratch_shapes=list(scratch_shapes) + list(rider.sems), compiler_params=params,
    )(*args, *rider.arrs)
    return (res[0] if single else list(res[:n_out])), list(res[n_out:])


def _sibling_sum(part, landed, pos, name):
    _, _, rows, cols = part.shape
    tr = _tile(rows, 512)

    def body(pos_ref, p_ref, l_ref, own_ref, all_ref):
        s = p_ref[0, 0] + l_ref[0, 0]
        all_ref[0] = s.astype(BF16)

        @pl.when(pl.program_id(1) == pos_ref[1])
        def _():
            own_ref[...] = s

    return pl.pallas_call(
        body, name=name,
        grid_spec=pltpu.PrefetchScalarGridSpec(
            num_scalar_prefetch=1, grid=(rows // tr, 4),
            in_specs=[pl.BlockSpec((1, 1, tr, cols), lambda i, k, pos: (k, pos[0], i, 0)),
                      pl.BlockSpec((1, 1, tr, cols), lambda i, k, pos: (k, 0, i, 0))],
            out_specs=[pl.BlockSpec((tr, cols), lambda i, k, pos: (i, 0)),
                       pl.BlockSpec((1, tr, cols), lambda i, k, pos: (k, i, 0))]),
        out_shape=[jax.ShapeDtypeStruct((rows, cols), F32), jax.ShapeDtypeStruct((4, rows, cols), BF16)],
        compiler_params=_params(("arbitrary", "arbitrary")),
    )(pos, part, landed)


def _rms(x, g):
    r = lax.rsqrt(jnp.mean(x * x, axis=-1, keepdims=True) + EPS)
    return x * r, r


def _norm_matmul(h, g, w, *, flat, nbk, name, scale=1.0, rider=None):
    T, D = h.shape
    nb, _, bn = w.shape
    tm = _tile(T, MATMUL_TILE)

    def body(h_ref, g_ref, w_ref, o_ref, n_ref):
        @pl.when(pl.program_id(1) == 0)
        def _():
            xh, _ = _rms(h_ref[...], None)
            n_ref[...] = (xh * g_ref[...]).astype(BF16)

        n = n_ref[...]
        for k in range(nbk):
            r = _dot(n, w_ref[k])
            r = (r if scale == 1.0 else r * scale).astype(BF16)
            if flat:
                o_ref[:, k * bn:(k + 1) * bn] = r
            else:
                o_ref[k] = r

    if flat:
        out_shape = jax.ShapeDtypeStruct((T, nb * bn), BF16)
        out_spec = pl.BlockSpec((tm, nbk * bn), lambda i, j: (i, j))
    else:
        out_shape = jax.ShapeDtypeStruct((nb, T, bn), BF16)
        out_spec = pl.BlockSpec((nbk, tm, bn), lambda i, j: (j, i, 0))
    return _call(
        body, name=name, grid=(T // tm, nb // nbk),
        in_specs=[pl.BlockSpec((tm, D), lambda i, j: (i, 0)),
                  pl.BlockSpec((1, D), lambda i, j: (0, 0)),
                  pl.BlockSpec((nbk, D, bn), lambda i, j: (j, 0, 0))],
        out_specs=[out_spec, pl.BlockSpec((tm, D), lambda i, j: (i, 0))],
        out_shape=[out_shape, jax.ShapeDtypeStruct((T, D), BF16)],
        args=(h, g, w), rider=rider)


def _matmul_nt(dy, w, *, flat, nbk, name, norm=None, out_dtype=BF16, rider=None):
    nb, R, bn = w.shape
    T = dy.shape[0] if flat else dy.shape[1]
    tm = _tile(T, MATMUL_TILE)
    nj = nb // nbk

    def body(*refs):
        if norm is None:
            dy_ref, w_ref, o_ref, acc_ref = refs
        else:
            dy_ref, w_ref, h_ref, g_ref, dres_ref, o_ref, dg_ref, acc_ref = refs
        i, j = pl.program_id(0), pl.program_id(1)

        @pl.when(j == 0)
        def _():
            acc_ref[...] = jnp.zeros_like(acc_ref)

        acc = acc_ref[...]
        for k in range(nbk):
            d = dy_ref[:, k * bn:(k + 1) * bn] if flat else dy_ref[k]
            acc = acc + _dot_nt(d.astype(BF16), w_ref[k])
        acc_ref[...] = acc

        @pl.when(j == nj - 1)
        def _():
            if norm is None:
                o_ref[...] = acc.astype(out_dtype)
            else:
                xh, r = _rms(h_ref[...], None)

                @pl.when(i == 0)
                def _():
                    dg_ref[...] = jnp.zeros_like(dg_ref)

                dg_ref[0:1, :] += jnp.sum(acc * xh, axis=0, keepdims=True)
                dn = acc * g_ref[...]
                o_ref[...] = dres_ref[...] + r * (dn - xh * jnp.mean(dn * xh, axis=-1, keepdims=True))

    if flat:
        dy_spec = pl.BlockSpec((tm, nbk * bn), lambda i, j: (i, j))
    else:
        dy_spec = pl.BlockSpec((nbk, tm, bn), lambda i, j: (j, i, 0))
    w_spec = pl.BlockSpec((nbk, R, bn), lambda i, j: (j, 0, 0))
    row_spec = pl.BlockSpec((tm, R), lambda i, j: (i, 0))
    if norm is None:
        in_specs, args = [dy_spec, w_spec], (dy, w)
        out_specs = row_spec
        out_shape = jax.ShapeDtypeStruct((T, R), out_dtype)
    else:
        in_specs = [dy_spec, w_spec, row_spec, pl.BlockSpec((1, R), lambda i, j: (0, 0)), row_spec]
        args = (dy, w) + tuple(norm)
        out_specs = [row_spec, pl.BlockSpec((8, R), lambda i, j: (0, 0))]
        out_shape = [jax.ShapeDtypeStruct((T, R), F32), jax.ShapeDtypeStruct((8, R), F32)]
    return _call(
        body, name=name, grid=(T // tm, nj), in_specs=in_specs, out_specs=out_specs, out_shape=out_shape,
        scratch_shapes=[pltpu.VMEM((tm, R), F32)], args=args, rider=rider)


def _wgrad_cols(n, dy, *, flat, nb, nbk, name, rider=None):
    T, D = n.shape
    halves = isinstance(dy, tuple)
    bn = dy.shape[1] // nb if flat else (dy[0] if halves else dy).shape[2]
    tt = _tile(T, MATMUL_TILE)
    nt = T // tt
    nj = nb // nbk

    def body(*refs):
        n_ref, dy_refs, (o_ref, acc_ref) = refs[0], refs[1:-2], refs[-2:]
        j, t = pl.program_id(0), pl.program_id(1)

        @pl.when(t == 0)
        def _():
            acc_ref[...] = jnp.zeros_like(acc_ref)

        def accumulate(dy_ref):
            nv = n_ref[...]
            for k in range(nbk):
                d = dy_ref[:, k * bn:(k + 1) * bn] if flat else dy_ref[k]
                acc_ref[k] += _dot_tn(nv, d)

        if halves:
            pl.when(j < nj // 2)(lambda: accumulate(dy_refs[0]))
            pl.when(j >= nj // 2)(lambda: accumulate(dy_refs[1]))
        else:
            accumulate(dy_refs[0])

        @pl.when(t == nt - 1)
        def _():
            o_ref[...] = acc_ref[...]

    if flat:
        dy_specs, dys = [pl.BlockSpec((tt, nbk * bn), lambda j, t: (t, j))], [dy]
    elif halves:
        first = pl.BlockSpec((nbk, tt, bn), lambda j, t: (jnp.minimum(j, nj // 2 - 1),
                                                          jnp.where(j < nj // 2, t, nt - 1), 0))
        second = pl.BlockSpec((nbk, tt, bn), lambda j, t: (jnp.maximum(j - nj // 2, 0),
                                                           jnp.where(j >= nj // 2, t, 0), 0))
        dy_specs, dys = [first, second], list(dy)
    else:
        dy_specs, dys = [pl.BlockSpec((nbk, tt, bn), lambda j, t: (j, t, 0))], [dy]
    return _call(
        body, name=name, grid=(nj, nt),
        in_specs=[pl.BlockSpec((tt, D), lambda j, t: (t, 0))] + dy_specs,
        out_specs=pl.BlockSpec((nbk, D, bn), lambda j, t: (j, 0, 0)),
        out_shape=jax.ShapeDtypeStruct((nb, D, bn), F32),
        scratch_shapes=[pltpu.VMEM((nbk, D, bn), F32)], args=[n] + dys, rider=rider)


def _wgrad_rows(xa, dh, *, flat, tk, name, rider=None):
    T, D = dh.shape
    nk = xa.shape[1] // tk if flat else xa.shape[0]
    tt = _tile(T, MATMUL_TILE)
    nt = T // tt

    def body(x_ref, dh_ref, o_ref, acc_ref):
        t = pl.program_id(1)

        @pl.when(t == 0)
        def _():
            acc_ref[...] = jnp.zeros_like(acc_ref)

        xv = x_ref[...] if flat else x_ref[0]
        acc_ref[...] += _dot_tn(xv, dh_ref[...].astype(BF16))

        @pl.when(t == nt - 1)
        def _():
            o_ref[...] = acc_ref[...]

    x_spec = pl.BlockSpec((tt, tk), lambda j, t: (t, j)) if flat else pl.BlockSpec((1, tt, tk), lambda j, t: (j, t, 0))
    return _call(
        body, name=name, grid=(nk, nt),
        in_specs=[x_spec, pl.BlockSpec((tt, D), lambda j, t: (t, 0))],
        out_specs=pl.BlockSpec((tk, D), lambda j, t: (j, 0)),
        out_shape=jax.ShapeDtypeStruct((nk * tk, D), F32),
        scratch_shapes=[pltpu.VMEM((tk, D), F32)], args=(xa, dh), rider=rider)


def _matmul_residual(xa, w, res, name):
    T, K = xa.shape
    D = w.shape[1]
    tm = _tile(T, MATMUL_TILE)

    def body(x_ref, w_ref, r_ref, o_ref):
        o_ref[...] = r_ref[...] + _dot(x_ref[...], w_ref[...])

    return pl.pallas_call(
        body, name=name, grid=(T // tm,),
        in_specs=[pl.BlockSpec((tm, K), lambda i: (i, 0)), pl.BlockSpec((K, D), lambda i: (0, 0)),
                  pl.BlockSpec((tm, D), lambda i: (i, 0))],
        out_specs=pl.BlockSpec((tm, D), lambda i: (i, 0)),
        out_shape=jax.ShapeDtypeStruct((T, D), F32),
        compiler_params=_params(("arbitrary",)),
    )(xa, w, res)


def _gmlp_gate(z, ws, bst, gv, G, gd):
    D = G * gd
    u = _gelu(z[:, :D].astype(F32))
    v = _gelu(z[:, D:].astype(F32))
    vh, r = _rms(v, None)
    vn = (vh * gv).astype(BF16)
    return u, v, vh, r, vn


def _gmlp_forward(z, ws, bst, gv, w_out, x, *, name, rider=None):
    T, D2 = z.shape
    D = D2 // 2
    G = ws.shape[0]
    gd = D // G
    tb = _tile(T, 256)
    nblk = tb // GMLP_BLOCK

    def body(z_ref, ws_ref, b_ref, gv_ref, wo_ref, x_ref, gated_ref, h_ref):
        u, _, _, _, vn = _gmlp_gate(z_ref[...], None, None, gv_ref[...], G, gd)
        for n in range(nblk):
            rows = slice(n * GMLP_BLOCK, (n + 1) * GMLP_BLOCK)
            for gi in range(G):
                cols = slice(gi * gd, (gi + 1) * gd)
                s = _dot(ws_ref[gi], vn[rows, cols]) + b_ref[:, gi:gi + 1]
                gated_ref[rows, cols] = (u[rows, cols] * s).astype(BF16)
        h_ref[...] = x_ref[...] + _dot(gated_ref[...], wo_ref[...])

    return _call(
        body, name=name, grid=(T // tb,),
        in_specs=[pl.BlockSpec((tb, D2), lambda i: (i, 0)), pl.BlockSpec(ws.shape, lambda i: (0, 0, 0)),
                  pl.BlockSpec(bst.shape, lambda i: (0, 0)), pl.BlockSpec((1, D), lambda i: (0, 0)),
                  pl.BlockSpec((D, D), lambda i: (0, 0)), pl.BlockSpec((tb, D), lambda i: (i, 0))],
        out_specs=[pl.BlockSpec((tb, D), lambda i: (i, 0)), pl.BlockSpec((tb, D), lambda i: (i, 0))],
        out_shape=[jax.ShapeDtypeStruct((T, D), BF16), jax.ShapeDtypeStruct((T, D), F32)],
        args=(z, ws, bst, gv, w_out, x), rider=rider)


def _gmlp_backward(z, dgated, ws, bst, gv, mask, *, name, rider=None):
    T, D2 = z.shape
    D = D2 // 2
    G = ws.shape[0]
    gd = D // G
    tb = _tile(T, 256)
    nblk = tb // GMLP_BLOCK

    def body(z_ref, dg_ref, ws_ref, b_ref, gv_ref, mask_ref, dz_ref, dws_ref, db_ref, dgv_ref, dvn_ref):
        @pl.when(pl.program_id(0) == 0)
        def _():
            dws_ref[...] = jnp.zeros_like(dws_ref)
            db_ref[...] = jnp.zeros_like(db_ref)
            dgv_ref[...] = jnp.zeros_like(dgv_ref)

        zf = z_ref[...]
        u, v, vh, r, vn = _gmlp_gate(zf, None, None, gv_ref[...], G, gd)
        dg = dg_ref[...].astype(F32)
        for n in range(nblk):
            rows = slice(n * GMLP_BLOCK, (n + 1) * GMLP_BLOCK)
            for gi in range(G):
                cols = slice(gi * gd, (gi + 1) * gd)
                vblk = vn[rows, cols]
                s = _dot(ws_ref[gi], vblk) + b_ref[:, gi:gi + 1]
                dgb = dg[rows, cols]
                ds = dgb * u[rows, cols]
                dsb = ds.astype(BF16)
                dz_ref[rows, cols] = (dgb * s * _gelu_grad(zf[rows, cols].astype(F32))).astype(BF16)
                dvn_ref[rows, cols] = _dot_tn(ws_ref[gi], dsb)
                dws_ref[gi] += _dot_nt(dsb, vblk) * mask_ref[...]
                db_ref[:, gi:gi + 1] += jnp.sum(ds, axis=1, keepdims=True)
        dvn = dvn_ref[...]
        dgv_ref[0:1, :] += jnp.sum(dvn * vh, axis=0, keepdims=True)
        dn = dvn * gv_ref[...]
        dv = r * (dn - vh * jnp.mean(dn * vh, axis=-1, keepdims=True))
        dz_ref[:, D:] = (dv * _gelu_grad(zf[:, D:].astype(F32))).astype(BF16)

    return _call(
        body, name=name, grid=(T // tb,),
        in_specs=[pl.BlockSpec((tb, D2), lambda i: (i, 0)), pl.BlockSpec((tb, D), lambda i: (i, 0)),
                  pl.BlockSpec(ws.shape, lambda i: (0, 0, 0)), pl.BlockSpec(bst.shape, lambda i: (0, 0)),
                  pl.BlockSpec((1, D), lambda i: (0, 0)), pl.BlockSpec(mask.shape, lambda i: (0, 0))],
        out_specs=[pl.BlockSpec((tb, D2), lambda i: (i, 0)), pl.BlockSpec(ws.shape, lambda i: (0, 0, 0)),
                   pl.BlockSpec(bst.shape, lambda i: (0, 0)), pl.BlockSpec((8, D), lambda i: (0, 0))],
        out_shape=[jax.ShapeDtypeStruct((T, D2), BF16), jax.ShapeDtypeStruct(ws.shape, F32),
                   jax.ShapeDtypeStruct(bst.shape, F32), jax.ShapeDtypeStruct((8, D), F32)],
        scratch_shapes=[pltpu.VMEM((tb, D), F32)], args=(z, dgated, ws, bst, gv, mask), rider=rider)


def _shift_rows(x, k):
    return pltpu.roll(x, k % x.shape[0], axis=0)


def _conv3(ext, cw):
    return (cw[0:1] * _shift_rows(ext, 2)[8:] + cw[1:2] * _shift_rows(ext, 1)[8:] + cw[2:3] * ext[8:])


def _ffn_forward(a, cw, cb, wd, h, seq, *, name, rider=None):
    _, T, F = a.shape
    D = h.shape[1]
    tm = _tile(seq, TOKEN_TILE)
    hb = tm // 16

    def body(a_ref, ap_ref, cw_ref, cb_ref, wd_ref, h_ref, act_ref, c_ref, o_ref, acc_ref):
        i, j = pl.program_id(0), pl.program_id(1)
        keep = ((i * tm) % seq != 0).astype(F32)

        def conv(b):
            ext = jnp.concatenate([ap_ref[b, 8:16].astype(F32) * keep, a_ref[b].astype(F32)], axis=0)
            return _conv3(ext, cw_ref[b]) + cb_ref[b]

        up, gate = conv(j), conv(j + 4)
        c_ref[j] = up.astype(BF16)
        c_ref[j + 4] = gate.astype(BF16)
        act = (gate * _sigmoid(gate) * up).astype(BF16)
        act_ref[0] = act

        @pl.when(j == 0)
        def _():
            acc_ref[...] = h_ref[...]

        acc_ref[...] += _dot(act, wd_ref[0])

        @pl.when(j == 3)
        def _():
            o_ref[...] = acc_ref[...]

    return _call(
        body, name=name, grid=(T // tm, 4),
        in_specs=[pl.BlockSpec((8, tm, F), lambda i, j: (0, i, 0)),
                  pl.BlockSpec((8, 16, F), lambda i, j: (0, jnp.maximum(i * hb - 1, 0), 0)),
                  pl.BlockSpec((8, 3, F), lambda i, j: (0, 0, 0)), pl.BlockSpec((8, 1, F), lambda i, j: (0, 0, 0)),
                  pl.BlockSpec((1, F, D), lambda i, j: (j, 0, 0)), pl.BlockSpec((tm, D), lambda i, j: (i, 0))],
        out_specs=[pl.BlockSpec((1, tm, F), lambda i, j: (j, i, 0)), pl.BlockSpec((8, tm, F), lambda i, j: (0, i, 0)),
                   pl.BlockSpec((tm, D), lambda i, j: (i, 0))],
        out_shape=[jax.ShapeDtypeStruct((4, T, F), BF16), jax.ShapeDtypeStruct((8, T, F), BF16),
                   jax.ShapeDtypeStruct((T, D), F32)],
        scratch_shapes=[pltpu.VMEM((tm, D), F32)], args=(a, a, cw, cb, wd, h), rider=rider)


def _ffn_backward(dh, c, a, cw, wd, seq, *, name, rider=None):
    _, T, F = a.shape
    D = dh.shape[1]
    tm = _tile(seq, TOKEN_TILE)
    hb = tm // 16
    nt = T // tm

    def body(dh_ref, dhn_ref, cu_ref, cg_ref, cun_ref, cgn_ref, au_ref, ag_ref, cw_ref, wd_ref, da_ref, st_ref):
        i, j = pl.program_id(0), pl.program_id(1)
        keep_next = (((i + 1) * tm) % seq != 0).astype(F32)

        @pl.when((i == 0) & (j == 0))
        def _():
            st_ref[...] = jnp.zeros_like(st_ref)

        dhe = jnp.concatenate([dh_ref[...], dhn_ref[...] * keep_next], axis=0).astype(BF16)
        dact = _dot_nt(dhe, wd_ref[0])
        up = jnp.concatenate([cu_ref[0].astype(F32), cun_ref[0, 0:8].astype(F32)], axis=0)
        gate = jnp.concatenate([cg_ref[0].astype(F32), cgn_ref[0, 0:8].astype(F32)], axis=0)
        sg = _sigmoid(gate)
        gs = gate * sg
        d_up = dact * gs
        d_gate = dact * up * (sg + gs * (1.0 - sg))

        def finish(b, a_ref, dc):
            w = cw_ref[b]
            dm, u1, u2 = dc[:tm], _shift_rows(dc, -1)[:tm], _shift_rows(dc, -2)[:tm]
            da_ref[b] = (w[2:3] * dm + w[1:2] * u1 + w[0:1] * u2).astype(BF16)
            av = a_ref[0].astype(F32)
            st_ref[b, 0:1, :] += jnp.sum(u2 * av, axis=0, keepdims=True)
            st_ref[b, 1:2, :] += jnp.sum(u1 * av, axis=0, keepdims=True)
            st_ref[b, 2:3, :] += jnp.sum(dm * av, axis=0, keepdims=True)
            st_ref[b, 3:4, :] += jnp.sum(dm, axis=0, keepdims=True)

        finish(j, au_ref, d_up)
        finish(j + 4, ag_ref, d_gate)

    nxt = lambda i: jnp.minimum((i + 1) * hb, T // 16 - 1)
    return _call(
        body, name=name, grid=(nt, 4),
        in_specs=[pl.BlockSpec((tm, D), lambda i, j: (i, 0)),
                  pl.BlockSpec((8, D), lambda i, j: (jnp.minimum((i + 1) * (tm // 8), T // 8 - 1), 0)),
                  pl.BlockSpec((1, tm, F), lambda i, j: (j, i, 0)), pl.BlockSpec((1, tm, F), lambda i, j: (j + 4, i, 0)),
                  pl.BlockSpec((1, 16, F), lambda i, j: (j, nxt(i), 0)),
                  pl.BlockSpec((1, 16, F), lambda i, j: (j + 4, nxt(i), 0)),
                  pl.BlockSpec((1, tm, F), lambda i, j: (j, i, 0)), pl.BlockSpec((1, tm, F), lambda i, j: (j + 4, i, 0)),
                  pl.BlockSpec((8, 3, F), lambda i, j: (0, 0, 0)),
                  pl.BlockSpec((1, F, D), lambda i, j: (j, 0, 0))],
        out_specs=[pl.BlockSpec((8, tm, F), lambda i, j: (0, i, 0)), pl.BlockSpec((8, 8, F), lambda i, j: (0, 0, 0))],
        out_shape=[jax.ShapeDtypeStruct((8, T, F), BF16), jax.ShapeDtypeStruct((8, 8, F), F32)],
        args=(dh, dh, c, c, c, c, a, a, cw, wd), rider=rider)


def _ffn_fused_forward(h, g, w_in, cw, cb, wd, seq, *, name, rider=None):
    T, D = h.shape
    F = w_in.shape[2]
    tm = _tile(seq, TOKEN_TILE // 2)
    bps = 4
    nj = 4 // bps

    def body(h_ref, g_ref, wu_ref, wg_ref, cw_ref, cb_ref, wd_ref,
             au_ref, ag_ref, cu_ref, cg_ref, act_ref, n_ref, o_ref, acc_ref, carry_ref, *work_refs):
        eu_refs, eg_refs, stage_refs = work_refs[:bps], work_refs[bps:2 * bps], work_refs[2 * bps:]
        i, j = pl.program_id(0), pl.program_id(1)
        keep = ((i * tm) % seq != 0).astype(F32)

        @pl.when((i == 0) & (j == 0))
        def _():
            carry_ref[...] = jnp.zeros_like(carry_ref)

        @pl.when(j == 0)
        def _():
            xh, _ = _rms(h_ref[...], None)
            n_ref[...] = (xh * g_ref[...]).astype(BF16)
            acc_ref[...] = h_ref[...]

        n = n_ref[...]

        def project(b, k, w_ref, a_ref, ext_ref):
            a = _dot(n, w_ref[k]).astype(BF16)
            a_ref[k] = a
            ext_ref[0:8, :] = carry_ref[b] * keep
            ext_ref[8:, :] = a.astype(F32)
            carry_ref[b] = ext_ref[tm:tm + 8, :]

        def conv(b, ext_ref, r):
            x, w = ext_ref[r:r + ROWS + 8, :], cw_ref[b]
            return (w[0:1] * _shift_rows(x, 2) + w[1:2] * _shift_rows(x, 1) + w[2:3] * x)[8:] + cb_ref[b]

        for k in range(bps):
            project(j * bps + k, k, wu_ref, au_ref, eu_refs[k])
            project(j * bps + k + 4, k, wg_ref, ag_ref, eg_refs[k])
        outs = []
        for k in range(bps):
            for r in range(0, tm, ROWS):
                up, gate = conv(j * bps + k, eu_refs[k], r), conv(j * bps + k + 4, eg_refs[k], r)
                cu_ref[k, r:r + ROWS, :] = up.astype(BF16)
                cg_ref[k, r:r + ROWS, :] = gate.astype(BF16)
                stage_refs[k][r:r + ROWS, :] = (gate * _sigmoid(gate) * up).astype(BF16)
            act = stage_refs[k][...]
            act_ref[k] = act
            outs.append(_dot(act, wd_ref[k]))
        acc_ref[...] += functools.reduce(lambda p, q: p + q, outs)

        @pl.when(j == nj - 1)
        def _():
            o_ref[...] = acc_ref[...]

    blk = pl.BlockSpec((bps, tm, F), lambda i, j: (j, i, 0))
    row = pl.BlockSpec((tm, D), lambda i, j: (i, 0))
    half = jax.ShapeDtypeStruct((4, T, F), BF16)
    work = [pltpu.VMEM((tm + 8, F), F32)] * (2 * bps) + [pltpu.VMEM((tm, F), BF16)] * bps
    return _call(
        body, name=name, grid=(T // tm, nj),
        in_specs=[row, pl.BlockSpec((1, D), lambda i, j: (0, 0)),
                  pl.BlockSpec((bps, D, F), lambda i, j: (j, 0, 0), pipeline_mode=pl.Buffered(1)),
                  pl.BlockSpec((bps, D, F), lambda i, j: (j + nj, 0, 0), pipeline_mode=pl.Buffered(1)),
                  pl.BlockSpec((8, 3, F), lambda i, j: (0, 0, 0)), pl.BlockSpec((8, 1, F), lambda i, j: (0, 0, 0)),
                  pl.BlockSpec((bps, F, D), lambda i, j: (j, 0, 0), pipeline_mode=pl.Buffered(1))],
        out_specs=[blk, blk, blk, blk, blk, row, row],
        out_shape=[half, half, half, half, half, jax.ShapeDtypeStruct((T, D), BF16), jax.ShapeDtypeStruct((T, D), F32)],
        scratch_shapes=[pltpu.VMEM((tm, D), F32), pltpu.VMEM((8, 8, F), F32)] + work,
        args=(h, g, w_in, w_in, cw, cb, wd), rider=rider)


def _ffn_fused_backward(dh, cu, cg, au, ag, cw, wd, w_in, h, g, seq, *, name, rider=None):
    T, D = dh.shape
    F = wd.shape[1]
    tm = _tile(seq, TOKEN_TILE // 2)
    hb = tm // 16
    nt = T // tm
    bps = FFN_BLOCKS_PER_STEP
    nj = 4 // bps

    def body(dh_ref, dhn_ref, cu_ref, cg_ref, cun_ref, cgn_ref, au_ref, ag_ref, cw_ref, wd_ref, wu_ref, wg_ref,
             h_ref, g_ref, dau_ref, dag_ref, st_ref, o_ref, dg_ref, acc_ref, dact_ref, du_ref, dgt_ref):
        j, i = pl.program_id(0), pl.program_id(1)
        keep_next = (((i + 1) * tm) % seq != 0).astype(F32)
        tile_rows = pl.ds(pl.multiple_of(i * tm, tm), tm) if nj > 1 else slice(0, tm)

        @pl.when((i == 0) & (j == 0))
        def _():
            st_ref[...] = jnp.zeros_like(st_ref)
            dg_ref[...] = jnp.zeros_like(dg_ref)

        dhe = jnp.concatenate([dh_ref[...], dhn_ref[...] * keep_next], axis=0).astype(BF16)
        for k in range(bps):
            dact_ref[k] = _dot_nt(dhe, wd_ref[k])

        def conv_grads(k):
            for r in range(0, tm + 8, ROWS):
                if r < tm:
                    rows = slice(r, r + ROWS)
                    up, gate = cu_ref[k, rows, :].astype(F32), cg_ref[k, rows, :].astype(F32)
                else:
                    rows = slice(tm, tm + 8)
                    up, gate = cun_ref[k, 0:8, :].astype(F32), cgn_ref[k, 0:8, :].astype(F32)
                dact = dact_ref[k, rows, :]
                sg = _sigmoid(gate)
                gs = gate * sg
                du_ref[k, rows, :] = dact * gs
                dgt_ref[k, rows, :] = dact * up * (sg + gs * (1.0 - sg))

        def finish(b, k, a_ref, w_ref, dc_ref, da_ref):
            w = cw_ref[b]
            sums = [jnp.zeros((8, F), F32) for _ in range(4)]
            fold = lambda t: jnp.sum(t.reshape(ROWS // 8, 8, F), axis=0)
            for r in range(0, tm, ROWS):
                dc = dc_ref[k, r:r + ROWS + 8, :]
                dm, u1, u2 = dc[:ROWS], _shift_rows(dc, -1)[:ROWS], _shift_rows(dc, -2)[:ROWS]
                da_ref[k, r:r + ROWS, :] = (w[2:3] * dm + w[1:2] * u1 + w[0:1] * u2).astype(BF16)
                av = a_ref[k, r:r + ROWS, :].astype(F32)
                for s, t in enumerate((u2 * av, u1 * av, dm * av, dm)):
                    sums[s] = sums[s] + fold(t)
            for s in range(4):
                st_ref[b, s:s + 1, :] += jnp.sum(sums[s], axis=0, keepdims=True)
            return _dot_nt(da_ref[k], w_ref[k])

        dn_parts = []
        for k in range(bps):
            conv_grads(k)
            dn_parts.append(finish(j * bps + k, k, au_ref, wu_ref, du_ref, dau_ref))
            dn_parts.append(finish(j * bps + k + 4, k, ag_ref, wg_ref, dgt_ref, dag_ref))
        dn_part = functools.reduce(lambda p, q: p + q, dn_parts)

        @pl.when(j == 0)
        def _():
            acc_ref[tile_rows, :] = dn_part

        @pl.when(j > 0)
        def _():
            acc_ref[tile_rows, :] += dn_part

        @pl.when(j == nj - 1)
        def _():
            acc = acc_ref[tile_rows, :]
            xh, r = _rms(h_ref[...], None)
            dg_ref[0:1, :] += jnp.sum(acc * xh, axis=0, keepdims=True)
            dn = acc * g_ref[...]
            o_ref[...] = dh_ref[...] + r * (dn - xh * jnp.mean(dn * xh, axis=-1, keepdims=True))

    last = lambda j, i: jnp.where(j == nj - 1, i, 0)
    nxt = lambda i: jnp.minimum((i + 1) * hb, T // 16 - 1)
    blk = pl.BlockSpec((bps, tm, F), lambda j, i: (j, i, 0))
    halo = pl.BlockSpec((bps, 16, F), lambda j, i: (j, nxt(i), 0))
    row = pl.BlockSpec((tm, D), lambda j, i: (i, 0))
    work = pltpu.VMEM((bps, tm + 8, F), F32)
    half = jax.ShapeDtypeStruct((4, T, F), BF16)
    once = pl.Buffered(1) if nj == 1 else None
    return _call(
        body, name=name, grid=(nj, nt),
        in_specs=[row, pl.BlockSpec((8, D), lambda j, i: (jnp.minimum((i + 1) * (tm // 8), T // 8 - 1), 0)),
                  blk, blk, halo, halo, blk, blk,
                  pl.BlockSpec((8, 3, F), lambda j, i: (0, 0, 0)),
                  pl.BlockSpec((bps, F, D), lambda j, i: (j, 0, 0), pipeline_mode=once),
                  pl.BlockSpec((bps, D, F), lambda j, i: (j, 0, 0), pipeline_mode=once),
                  pl.BlockSpec((bps, D, F), lambda j, i: (j + nj, 0, 0), pipeline_mode=once),
                  pl.BlockSpec((tm, D), lambda j, i: (last(j, i), 0)), pl.BlockSpec((1, D), lambda j, i: (0, 0))],
        out_specs=[blk, blk, pl.BlockSpec((8, 8, F), lambda j, i: (0, 0, 0)),
                   pl.BlockSpec((tm, D), lambda j, i: (last(j, i), 0)), pl.BlockSpec((8, D), lambda j, i: (0, 0))],
        out_shape=[half, half, jax.ShapeDtypeStruct((8, 8, F), F32),
                   jax.ShapeDtypeStruct((T, D), F32), jax.ShapeDtypeStruct((8, D), F32)],
        scratch_shapes=[pltpu.VMEM((T if nj > 1 else tm, D), F32), work, work, work],
        args=(dh, dh, cu, cg, cu, cg, au, ag, cw, wd, w_in, w_in, h, g), rider=rider)


def _rel_onehot():
    r = lax.broadcasted_iota(jnp.int32, (REL_PAD, SKEW), 0)
    n = lax.broadcasted_iota(jnp.int32, (REL_PAD, SKEW), 1)
    off = jnp.where(n >= WIN, n - SKEW, n)
    idx = jnp.minimum(PAD - off, REL_CLIP) + REL_CLIP
    return (r == idx).astype(BF16)


def _skew(x, sign):
    row = lax.broadcasted_iota(jnp.int32, x.shape, 0)
    for b in range(7):
        x = jnp.where((row >> b) & 1 == 1, pltpu.roll(x, (sign * (1 << b)) % SKEW, axis=1), x)
    return x


def _bias_build(rel, name, rider=None):
    H = rel.shape[0]

    def body(rel_ref, o_ref):
        oh = _rel_onehot()
        hi, mid, lo = _split3(rel_ref[...])
        base = _dot(hi, oh) + _dot(mid, oh) + _dot(lo, oh)
        mine = lax.broadcasted_iota(jnp.int32, (H, 1), 0) == pl.program_id(0)
        row = jnp.sum(jnp.where(mine, base, 0.0), axis=0, keepdims=True)
        q = lax.broadcasted_iota(jnp.int32, (Q_TILE, WIN), 0)
        k = lax.broadcasted_iota(jnp.int32, (Q_TILE, WIN), 1)
        ok = ((q < CHUNK) & (k < WIN - CHUNK)) | ((q >= CHUNK) & (k >= CHUNK))
        t = _skew(jnp.broadcast_to(row, (Q_TILE, SKEW)), 1)
        o_ref[0] = jnp.where(ok, t[:, :WIN], NEG_INF)

    return _call(
        body, name=name, grid=(H,), in_specs=[pl.BlockSpec((H, REL_PAD), lambda h: (0, 0))],
        out_specs=pl.BlockSpec((1, Q_TILE, WIN), lambda h: (h, 0, 0)),
        out_shape=jax.ShapeDtypeStruct((H, Q_TILE, WIN), F32), args=(rel,), rider=rider)


def _bias_reduce(dbias, name):
    H = dbias.shape[0]

    def body(d_ref, o_ref, e_ref):
        oh = _rel_onehot()
        for hd in range(H):
            x = jnp.concatenate([d_ref[hd], jnp.zeros((Q_TILE, SKEW - WIN), F32)], axis=1)
            e_ref[hd:hd + 1, :] = jnp.sum(_skew(x, -1), axis=0, keepdims=True)
        hi, mid, lo = _split3(e_ref[...])
        o_ref[...] = _dot_nt(hi, oh) + _dot_nt(mid, oh) + _dot_nt(lo, oh)

    return pl.pallas_call(
        body, name=name, out_shape=jax.ShapeDtypeStruct((H, REL_PAD), F32),
        in_specs=[pl.BlockSpec(memory_space=pltpu.VMEM)], out_specs=pl.BlockSpec(memory_space=pltpu.VMEM),
        scratch_shapes=[pltpu.VMEM((H, SKEW), F32)],
        compiler_params=_params(),
    )(dbias)


def _pair_stack(xp, even):
    z = jnp.zeros_like(xp)
    return jnp.concatenate([jnp.where(even, xp, z), jnp.where(even, z, xp)], axis=0)


def _pair_merge(y, even):
    return jnp.where(even, y[:Q_TILE], y[Q_TILE:])


def _strip_probs(s_ref, b_ref, pp, r, valid, base=0):
    hb, hr = divmod(r, Q_TILE)
    s = s_ref[base + pp, r:r + STRIP, :] + b_ref[2 * pp + hb, hr:hr + STRIP, :]
    s = jnp.where(valid, s, NEG_INF)
    e = jnp.exp(s - jnp.max(s, axis=-1, keepdims=True))
    return e * (1.0 / jnp.sum(e, axis=-1, keepdims=True))


def _fill_padded(dst_ref, src_ref):
    dst_ref[0:PAD, :] = jnp.zeros((PAD, dst_ref.shape[1]), dst_ref.dtype)
    dst_ref[PAD:, :] = src_ref[...]


def _attn_specs(B, S, D, lanes):
    nt = S // (TILES_PER_STEP * Q_TILE)
    q_spec = pl.BlockSpec((TILES_PER_STEP * Q_TILE, lanes), lambda g, b, i: (b * nt + i, g))
    k_spec = pl.BlockSpec((S, lanes), lambda g, b, i: (b, g))
    v_spec = pl.BlockSpec((S, lanes), lambda g, b, i: (b, D // lanes + g))
    bias_spec = pl.BlockSpec((lanes // HEAD_DIM, Q_TILE, WIN), lambda g, b, i: (g, 0, 0))
    return nt, q_spec, k_spec, v_spec, bias_spec


def _attn_forward(q, kv, bias, S, *, name, rider=None):
    T, D = q.shape
    B = T // S
    lanes = min(FWD_HEADS_PER_STEP * HEAD_DIM, D)
    nt, q_spec, k_spec, v_spec, bias_spec = _attn_specs(B, S, D, lanes)

    npairs = lanes // (2 * HEAD_DIM)

    def body(q_ref, k_ref, v_ref, b_ref, o_ref, kp_ref, vp_ref, s_ref, p_ref):
        i = pl.program_id(2)

        @pl.when(i == 0)
        def _():
            _fill_padded(kp_ref, k_ref)
            _fill_padded(vp_ref, v_ref)

        even = lax.broadcasted_iota(jnp.int32, (1, 2 * HEAD_DIM), 1) < HEAD_DIM
        pair_cols = [slice(pp * 2 * HEAD_DIM, (pp + 1) * 2 * HEAD_DIM) for pp in range(npairs)]
        for t in range(TILES_PER_STEP):
            tile = i * TILES_PER_STEP + t
            start = pl.multiple_of(tile * Q_TILE, Q_TILE)
            rows = slice(t * Q_TILE, (t + 1) * Q_TILE)
            valid = lax.broadcasted_iota(jnp.int32, (STRIP, WIN), 1) >= PAD - tile * Q_TILE
            for pp, cols in enumerate(pair_cols):
                s_ref[t * npairs + pp] = _dot_nt(_pair_stack(q_ref[rows, cols], even), kp_ref[pl.ds(start, WIN), cols])
            for pp in range(npairs):
                for r in range(0, 2 * Q_TILE, STRIP):
                    p = _strip_probs(s_ref, b_ref, pp, r, valid, base=t * npairs)
                    p_ref[t * npairs + pp, r:r + STRIP, :] = p.astype(BF16)
            for pp, cols in enumerate(pair_cols):
                o = _dot(p_ref[t * npairs + pp], vp_ref[pl.ds(start, WIN), cols])
                o_ref[rows, cols] = _pair_merge(o, even).astype(BF16)

    nbuf = TILES_PER_STEP * npairs
    return _call(
        body, name=name, grid=(D // lanes, B, nt),
        in_specs=[q_spec, k_spec, v_spec, bias_spec], out_specs=q_spec,
        out_shape=jax.ShapeDtypeStruct((T, D), BF16),
        scratch_shapes=[pltpu.VMEM((S + PAD, lanes), BF16), pltpu.VMEM((S + PAD, lanes), BF16),
                        pltpu.VMEM((nbuf, 2 * Q_TILE, WIN), F32), pltpu.VMEM((nbuf, 2 * Q_TILE, WIN), BF16)],
        args=(q, kv, kv, bias), rider=rider)


def _attn_backward(q, kv, bias, do, S, *, name, rider=None):
    T, D = q.shape
    B = T // S
    H = D // HEAD_DIM
    lanes = min(BWD_HEADS_PER_STEP * HEAD_DIM, D)
    nt, q_spec, k_spec, v_spec, bias_spec = _attn_specs(B, S, D, lanes)
    scale = HEAD_DIM ** -0.5

    npairs = lanes // (2 * HEAD_DIM)

    def body(q_ref, k_ref, v_ref, b_ref, do_ref, dq_ref, dk_ref, dv_ref, db_ref, kp_ref, vp_ref, dka_ref, dva_ref,
             s_ref, dp_ref, p_ref, ds_ref):
        b, i = pl.program_id(1), pl.program_id(2)

        @pl.when((b == 0) & (i == 0))
        def _():
            db_ref[...] = jnp.zeros_like(db_ref)

        @pl.when(i == 0)
        def _():
            _fill_padded(kp_ref, k_ref)
            _fill_padded(vp_ref, v_ref)
            dka_ref[...] = jnp.zeros_like(dka_ref)
            dva_ref[...] = jnp.zeros_like(dva_ref)

        even = lax.broadcasted_iota(jnp.int32, (1, 2 * HEAD_DIM), 1) < HEAD_DIM
        pair_cols = [slice(pp * 2 * HEAD_DIM, (pp + 1) * 2 * HEAD_DIM) for pp in range(npairs)]
        for t in range(TILES_PER_STEP):
            tile = i * TILES_PER_STEP + t
            start = pl.multiple_of(tile * Q_TILE, Q_TILE)
            rows = slice(t * Q_TILE, (t + 1) * Q_TILE)
            valid = lax.broadcasted_iota(jnp.int32, (STRIP, WIN), 1) >= PAD - tile * Q_TILE
            base = t * npairs
            for pp, cols in enumerate(pair_cols):
                s_ref[base + pp] = _dot_nt(_pair_stack(q_ref[rows, cols], even), kp_ref[pl.ds(start, WIN), cols])
                dp_ref[base + pp] = _dot_nt(_pair_stack(do_ref[rows, cols], even), vp_ref[pl.ds(start, WIN), cols])
            for pp in range(npairs):
                for r in range(0, 2 * Q_TILE, STRIP):
                    hb, hr = divmod(r, Q_TILE)
                    p = _strip_probs(s_ref, b_ref, pp, r, valid, base=base)
                    dp = dp_ref[base + pp, r:r + STRIP, :]
                    ds = p * (dp - jnp.sum(p * dp, axis=-1, keepdims=True))
                    db_ref[2 * pp + hb, hr:hr + STRIP, :] += ds
                    p_ref[base + pp, r:r + STRIP, :] = p.astype(BF16)
                    ds_ref[base + pp, r:r + STRIP, :] = ds.astype(BF16)
            for pp, cols in enumerate(pair_cols):
                dsb = ds_ref[base + pp]
                dq = _pair_merge(_dot(dsb, kp_ref[pl.ds(start, WIN), cols]), even) * scale
                dq_ref[rows, cols] = dq.astype(BF16)
                dka_ref[pl.ds(start, WIN), cols] += _dot_tn(dsb, _pair_stack(q_ref[rows, cols], even))
                dva_ref[pl.ds(start, WIN), cols] += _dot_tn(p_ref[base + pp], _pair_stack(do_ref[rows, cols], even))

        @pl.when(i == nt - 1)
        def _():
            dk_ref[...] = dka_ref[PAD:, :].astype(BF16)
            dv_ref[...] = dva_ref[PAD:, :].astype(BF16)

    dkv_shape = jax.ShapeDtypeStruct((T, D), BF16)
    nbuf = TILES_PER_STEP * npairs
    return _call(
        body, name=name, grid=(D // lanes, B, nt),
        in_specs=[q_spec, k_spec, v_spec, bias_spec, q_spec],
        out_specs=[q_spec, k_spec, k_spec, bias_spec],
        out_shape=[jax.ShapeDtypeStruct((T, D), BF16), dkv_shape, dkv_shape,
                   jax.ShapeDtypeStruct((H, Q_TILE, WIN), F32)],
        scratch_shapes=[pltpu.VMEM((S + PAD, lanes), BF16), pltpu.VMEM((S + PAD, lanes), BF16),
                        pltpu.VMEM((S + PAD, lanes), F32), pltpu.VMEM((S + PAD, lanes), F32),
                        pltpu.VMEM((nbuf, 2 * Q_TILE, WIN), F32), pltpu.VMEM((nbuf, 2 * Q_TILE, WIN), F32),
                        pltpu.VMEM((nbuf, 2 * Q_TILE, WIN), BF16), pltpu.VMEM((nbuf, 2 * Q_TILE, WIN), BF16)],
        args=(q, kv, kv, bias, do), rider=rider)


def _loss_head(h, g, target, name):
    T, D = h.shape
    tm = _tile(T, MATMUL_TILE)

    def body(h_ref, g_ref, t_ref, dh_ref, st_ref):
        @pl.when(pl.program_id(0) == 0)
        def _():
            st_ref[...] = jnp.zeros_like(st_ref)

        xh, r = _rms(h_ref[...], None)
        err = xh * g_ref[...] - t_ref[...]
        st_ref[1:2, :] += 0.5 * jnp.sum(jnp.mean(err * err, axis=-1, keepdims=True), axis=0, keepdims=True)
        dy = err * (1.0 / D)
        st_ref[0:1, :] += jnp.sum(dy * xh, axis=0, keepdims=True)
        dn = dy * g_ref[...]
        dh_ref[...] = r * (dn - xh * jnp.mean(dn * xh, axis=-1, keepdims=True))

    row = pl.BlockSpec((tm, D), lambda i: (i, 0))
    return pl.pallas_call(
        body, name=name, grid=(T // tm,),
        in_specs=[row, pl.BlockSpec((1, D), lambda i: (0, 0)), row],
        out_specs=[row, pl.BlockSpec((8, D), lambda i: (0, 0))],
        out_shape=[jax.ShapeDtypeStruct((T, D), F32), jax.ShapeDtypeStruct((8, D), F32)],
        compiler_params=_params(("arbitrary",)),
    )(h, g, target)


def _sum_devices(arrs, name):
    n = len(arrs)

    def body(*refs):
        for a in range(n):
            s = refs[a][0].astype(F32)
            for k in range(1, N_DEV):
                s = s + refs[a][k].astype(F32)
            refs[n + a][...] = s

    vm = pl.BlockSpec(memory_space=pltpu.VMEM)
    return pl.pallas_call(
        body, name=name, out_shape=[jax.ShapeDtypeStruct(a.shape[1:], F32) for a in arrs],
        in_specs=[vm] * n, out_specs=[vm] * n, compiler_params=_params(),
    )(*arrs)


def _adamw_math(w, g, m, v):
    m = ADAM_B1 * m + (1.0 - ADAM_B1) * g
    v = ADAM_B2 * v + (1.0 - ADAM_B2) * (g * g)
    m_hat = m / (1.0 - ADAM_B1 ** ADAM_STEP)
    v_hat = v / (1.0 - ADAM_B2 ** ADAM_STEP)
    delta = -ADAM_LR * (m_hat / (jnp.sqrt(v_hat) + ADAM_EPS) + ADAM_WD * w)
    return delta, m, v


def _adamw_small(items, name):
    n = len(items)

    def body(*refs):
        for a in range(n):
            w, m, v, g = (refs[4 * a + k][...] for k in range(4))
            d, m, v = _adamw_math(w, g, m, v)
            refs[4 * n + 3 * a][...] = d
            refs[4 * n + 3 * a + 1][...] = m
            refs[4 * n + 3 * a + 2][...] = v

    vm = pl.BlockSpec(memory_space=pltpu.VMEM)
    flat = [t for it in items for t in it]
    outs = pl.pallas_call(
        body, name=name,
        out_shape=[jax.ShapeDtypeStruct(it[0].shape, F32) for it in items for _ in range(3)],
        in_specs=[vm] * (4 * n), out_specs=[vm] * (3 * n), compiler_params=_params(),
    )(*flat)
    return [tuple(outs[3 * a:3 * a + 3]) for a in range(n)]


def _adamw_big(w, m, v, owns, landeds, name, rider=None):
    L, R, C = w.shape
    tr = _tile(R, 512)
    nr = R // tr
    counts = [len(ls) for ls in landeds]

    def body(*refs):
        w_ref, m_ref, v_ref = refs[:3]
        g_ref, d_ref, mo_ref, vo_ref = refs[-4:]
        layer = pl.program_id(0)
        at = 3
        for j in range(L):
            own_ref, l_refs = refs[at], refs[at + 1:at + 1 + counts[j]]
            at += 1 + counts[j]

            @pl.when(layer == j)
            def _(own_ref=own_ref, l_refs=l_refs):
                g = own_ref[...]
                for l_ref in l_refs:
                    for k in range(l_ref.shape[0]):
                        g = g + l_ref[k].astype(F32)
                d, mn, vn = _adamw_math(w_ref[0], g, m_ref[0], v_ref[0])
                g_ref[0] = g
                d_ref[0] = d
                mo_ref[0] = mn
                vo_ref[0] = vn

    def pinned(j):
        return lambda l, i: jnp.where(l == j, i, jnp.where(l < j, 0, nr - 1))

    row = pl.BlockSpec((1, tr, C), lambda l, i: (l, i, 0))
    in_specs, args = [row, row, row], [w, m, v]
    for j in range(L):
        in_specs.append(pl.BlockSpec((tr, C), lambda l, i, p=pinned(j): (p(l, i), 0)))
        args.append(owns[j])
        for arr in landeds[j]:
            in_specs.append(pl.BlockSpec((arr.shape[0], tr, C), lambda l, i, p=pinned(j): (0, p(l, i), 0)))
            args.append(arr)
    return _call(body, name=name, grid=(L, nr), in_specs=in_specs, out_specs=[row] * 4,
                 out_shape=[jax.ShapeDtypeStruct((L, R, C), F32)] * 4, args=args, rider=rider)


def kernel(x, a_norm_g, a_w_in, a_v_norm_g, a_w_s, a_b_s, a_w_out, kv_norm_g, w_kv, b_norm_g, b_w_q, b_rel_bias, b_w_o, f_norm_g, f_w_in, f_conv_w, f_conv_b, f_w_down, final_norm_g, loss_target, m_a_norm_g, m_a_w_in, m_a_v_norm_g, m_a_w_s, m_a_b_s, m_a_w_out, m_kv_norm_g, m_w_kv, m_b_norm_g, m_b_w_q, m_b_rel_bias, m_b_w_o, m_f_norm_g, m_f_w_in, m_f_conv_w, m_f_conv_b, m_f_w_down, m_final_norm_g, v_a_norm_g, v_a_w_in, v_a_v_norm_g, v_a_w_s, v_a_b_s, v_a_w_out, v_kv_norm_g, v_w_kv, v_b_norm_g, v_b_w_q, v_b_rel_bias, v_b_w_o, v_f_norm_g, v_f_w_in, v_f_conv_w, v_f_conv_b, v_f_w_down, v_final_norm_g):
    B, S, D = x.shape
    T = B * S
    G = a_w_s.shape[1]
    H = D // HEAD_DIM
    F = f_w_in.shape[2]
    L = f_w_in.shape[0]
    dn = D // N_DEV
    xi, yi, ci = lax.axis_index("x"), lax.axis_index("y"), lax.axis_index("c")
    me = 4 * xi + 2 * yi + ci
    pos = jnp.stack([ci, 2 * xi + yi]).astype(jnp.int32)

    cast = lambda t: t.astype(BF16)
    gather = lambda *ts: _gather_rider(list(ts))
    rel = jnp.pad(b_rel_bias[0], ((0, 0), (0, REL_PAD - b_rel_bias.shape[2])))
    bias, (wa_in, norms_sh, conv_w0, conv_w1) = _bias_build(rel, "bias_build", rider=gather(
        cast(a_w_in[0]), jnp.concatenate([a_norm_g, a_v_norm_g], axis=0), f_conv_w[0], f_conv_w[1]))
    ga = jnp.transpose(norms_sh, (1, 0, 2)).reshape(2, D)
    g_a, g_av = ga[0:1], ga[1:2]

    x2 = x.reshape(T, D)
    tgt = loss_target.reshape(T, D)
    pc = jnp.arange(GMLP_BLOCK) // CHUNK
    mask = (pc[:, None] >= pc[None, :]).astype(F32)
    ws = (a_w_s[0] * mask[None]).astype(BF16)
    bst = jnp.transpose(a_b_s[0])
    four = lambda t: t.reshape((4, 2) + t.shape[1:])

    w_in0_sh = cast(f_w_in[0])
    (z, n_a), (wa_out, w_in0_top) = _norm_matmul(x2, g_a, wa_in, flat=True, nbk=4, name="gmlp_in",
                                                 rider=gather(cast(a_w_out[0]), w_in0_sh[:D // 2]))
    wa_out = wa_out.reshape(D, D)
    (gated, h1), (w_in0_bottom, wf_down0) = _gmlp_forward(z, ws, bst, g_av, wa_out, x2, name="gmlp_mix",
                                                          rider=gather(w_in0_sh[D // 2:], cast(f_w_down[0])))
    w_in0 = jnp.concatenate([w_in0_top, w_in0_bottom], axis=1)
    cw0, cb0, wd0 = conv_w0, f_conv_b[0].reshape(8, 1, F), wf_down0.reshape(4, F, D)
    (au0, ag0, cu0, cg0, act0, n_f0, h2), (wkv, wq, w_in1) = _ffn_fused_forward(
        h1, f_norm_g[0:1], w_in0, cw0, cb0, wd0, S, name="ffn0_fwd",
        rider=gather(cast(w_kv), cast(b_w_q[0]), cast(f_w_in[1])))
    wq = wq.reshape(D, D)
    kv, n_kv = _norm_matmul(h2, kv_norm_g.reshape(1, D), wkv, flat=True, nbk=4, name="kv_proj")
    q, n_q = _norm_matmul(h2, b_norm_g, wq.reshape(1, D, D), flat=True, nbk=1, name="q_proj", scale=HEAD_DIM ** -0.5)
    o, (wo, wf_down1) = _attn_forward(q, kv, bias, S, name="attn", rider=gather(cast(b_w_o[0]), cast(f_w_down[1])))
    wo = wo.reshape(D, D)
    cw1, cb1, wd1 = conv_w1, f_conv_b[1].reshape(8, 1, F), wf_down1.reshape(4, F, D)
    h3 = _matmul_residual(o, wo, h2, "attn_out")
    au1, ag1, cu1, cg1, act1, n_f1, h4 = _ffn_fused_forward(h3, f_norm_g[1:2], w_in1, cw1, cb1, wd1, S, name="ffn1_fwd")

    sums, from_chips = {}, {}

    def sibling_sums(names, parts, landed):
        for nm, p, l in zip(names, parts, landed):
            sums[nm] = _sibling_sum(p, l, pos, "grad_sibling_sum_" + nm)

    def chip_rider(*names):
        return _chip_rider([sums[nm][1] for nm in names])

    dh4, st_final = _loss_head(h4, final_norm_g.reshape(1, D), tgt, "loss_head")
    g_wd1 = _wgrad_rows(act1, dh4, flat=False, tk=F, name="ffn1_dwdown")
    parts = [four(g_wd1.reshape(8, F // 2, D))]
    (dau1, dag1, st_conv1, dh3, st_f1), landed = _ffn_fused_backward(
        dh4, cu1, cg1, au1, ag1, cw1, wd1, w_in1, h3, f_norm_g[1:2], S, name="ffn1_bwd", rider=_sibling_rider(parts))
    sibling_sums(["wd1"], parts, landed)
    g_win1, (from_chips["wd1"],) = _wgrad_cols(n_f1, (dau1, dag1), flat=False, nb=8, nbk=2, name="ffn1_dwin",
                                               rider=chip_rider("wd1"))
    d_o = _matmul_nt(dh3, wo.reshape(1, D, D), flat=True, nbk=1, name="attn_out_dx")
    parts = [four(g_win1)]
    g_wo, landed = _wgrad_rows(o, dh3, flat=True, tk=_tile(D, 512), name="attn_out_dw", rider=_sibling_rider(parts))
    sibling_sums(["win1"], parts, landed)
    (dq, dk, dv, dbias), (from_chips["win1"],) = _attn_backward(
        q, kv, bias, d_o, S, name="attn_bwd", rider=chip_rider("win1"))
    g_rel = _bias_reduce(dbias, "bias_reduce")
    g_wq = _wgrad_cols(n_q, dq, flat=True, nb=1, nbk=1, name="q_dw")
    dh2, st_b = _matmul_nt(dq, wq.reshape(1, D, D), flat=True, nbk=1, name="q_dx", norm=(h2, b_norm_g, dh3))
    dkv = jnp.concatenate([dk, dv], axis=-1)
    g_wkv = _wgrad_cols(n_kv, dkv, flat=True, nb=8, nbk=4, name="kv_dw")
    parts = [four(g_wo.reshape(8, dn, D)), four(g_wq.reshape(8, dn, D)), four(g_wkv)]
    (dh2, st_kv), landed = _matmul_nt(dkv, wkv, flat=True, nbk=4, name="kv_dx",
                                      norm=(h2, kv_norm_g.reshape(1, D), dh2), rider=_sibling_rider(parts))
    sibling_sums(["wo", "wq", "wkv"], parts, landed)
    g_wd0, (from_chips["wo"], from_chips["wq"], from_chips["wkv"]) = _wgrad_rows(
        act0, dh2, flat=False, tk=F, name="ffn0_dwdown", rider=chip_rider("wo", "wq", "wkv"))
    parts = [four(g_wd0.reshape(8, F // 2, D))]
    (dau0, dag0, st_conv0, dh1, st_f0), landed = _ffn_fused_backward(
        dh2, cu0, cg0, au0, ag0, cw0, wd0, w_in0, h1, f_norm_g[0:1], S, name="ffn0_bwd", rider=_sibling_rider(parts))
    sibling_sums(["wd0"], parts, landed)
    g_win0, (ce,) = _wgrad_cols(n_f0, (dau0, dag0), flat=False, nb=8, nbk=2, name="ffn0_dwin",
                                rider=chip_rider("wd0"))
    from_chips["wd0"] = [ce]
    dgated = _matmul_nt(dh1, wa_out.reshape(1, D, D), flat=True, nbk=1, name="gmlp_out_dx")
    parts = [four(g_win0)]
    g_wa_out, landed = _wgrad_rows(gated, dh1, flat=True, tk=_tile(D, 512), name="gmlp_out_dw",
                                   rider=_sibling_rider(parts))
    sibling_sums(["win0"], parts, landed)
    parts = [four(g_wa_out.reshape(8, dn, D))]
    (dz, g_ws, g_bst, st_av), (ce_win0_a, landed) = _gmlp_backward(
        z, dgated, ws, bst, g_av, mask, name="gmlp_bwd",
        rider=_join_riders([_chip_rider([sums["win0"][1]], ks=(1, 2)), _sibling_rider(parts)]))
    sibling_sums(["wa_out"], parts, [landed])
    g_wa_in, (ce_win0_b, ce) = _wgrad_cols(n_a, dz, flat=True, nb=8, nbk=4, name="gmlp_in_dw", rider=_join_riders(
        [_chip_rider([sums["win0"][1]], ks=(3,)), chip_rider("wa_out")]))
    from_chips["win0"], from_chips["wa_out"] = [ce_win0_a, ce_win0_b], [ce]
    vec = jnp.concatenate([st_av[0:1], st_kv[0:1], st_b[0:1], st_f0[0:1], st_f1[0:1], st_final[0:3]], axis=0)
    parts = [four(g_wa_in)]
    (grad_x, st_a), got = _matmul_nt(dz, wa_in, flat=True, nbk=4, name="gmlp_in_dx", norm=(x2, g_a, dh1),
                                     rider=_join_riders([_sibling_rider(parts), gather(
                                         vec, cast(g_ws), cast(g_bst), cast(g_rel), cast(st_conv0), cast(st_conv1))]))
    sibling_sums(["wa_in"], parts, got[0:1])
    small = got[1:]

    def big_update(names, w, m, v, rider=None):
        shape = w.shape
        r = lambda t: t.reshape((len(names), -1, shape[-1]))
        as_list = lambda t: t if isinstance(t, list) else [t]
        outs = _adamw_big(r(w), r(m), r(v), [sums[nm][0] for nm in names],
                          [as_list(from_chips[nm]) for nm in names], "adamw_" + names[0], rider=rider)
        outs, got = (outs, None) if rider is None else outs
        return [t.reshape(shape) for t in outs], got

    u_f_w_in, (ce, st_a) = big_update(["win0", "win1"], f_w_in, m_f_w_in, v_f_w_in,
                                      rider=_join_riders([chip_rider("wa_in"), gather(st_a)]))
    from_chips["wa_in"] = [ce]
    u_f_w_down, _ = big_update(["wd0", "wd1"], f_w_down, m_f_w_down, v_f_w_down)
    u_a_w_in, _ = big_update(["wa_in"], a_w_in, m_a_w_in, v_a_w_in)
    u_w_kv, _ = big_update(["wkv"], w_kv, m_w_kv, v_w_kv)
    u_a_w_out, _ = big_update(["wa_out"], a_w_out, m_a_w_out, v_a_w_out)
    u_b_w_q, _ = big_update(["wq"], b_w_q, m_b_w_q, v_b_w_q)
    u_b_w_o, _ = big_update(["wo"], b_w_o, m_b_w_o, v_b_w_o)

    vec, g_ws, g_bst, g_rel, st_conv0, st_conv1, st_a = _sum_devices(list(small) + [st_a], "sum_small_grads")
    vec = jnp.concatenate([st_a[0:1], vec[0:7]], axis=0)
    loss = vec[7, 0]
    g_a_norm = lax.dynamic_slice_in_dim(vec[0:1], me * dn, dn, axis=1)
    g_av_norm = lax.dynamic_slice_in_dim(vec[1:2], me * dn, dn, axis=1)
    st_conv = jnp.stack([st_conv0, st_conv1])
    g_conv_w = lax.dynamic_index_in_dim(st_conv, me, axis=1, keepdims=False)[:, 0:3]
    g_conv_b = st_conv[:, :, 3, :].reshape(L, 8 * F)
    small_items = [
        (a_norm_g, m_a_norm_g, v_a_norm_g, g_a_norm),
        (a_v_norm_g, m_a_v_norm_g, v_a_v_norm_g, g_av_norm),
        (a_w_s, m_a_w_s, v_a_w_s, g_ws[None]),
        (a_b_s, m_a_b_s, v_a_b_s, jnp.transpose(g_bst)[None]),
        (kv_norm_g.reshape(1, D), m_kv_norm_g.reshape(1, D), v_kv_norm_g.reshape(1, D), vec[2:3]),
        (b_norm_g, m_b_norm_g, v_b_norm_g, vec[3:4]),
        (b_rel_bias, m_b_rel_bias, v_b_rel_bias, g_rel[None, :, :b_rel_bias.shape[2]]),
        (f_norm_g, m_f_norm_g, v_f_norm_g, vec[4:6]),
        (f_conv_w, m_f_conv_w, v_f_conv_w, g_conv_w),
        (f_conv_b, m_f_conv_b, v_f_conv_b, g_conv_b),
        (final_norm_g.reshape(1, D), m_final_norm_g.reshape(1, D), v_final_norm_g.reshape(1, D), vec[6:7]),
    ]
    small_out = _adamw_small(small_items, "adamw_small")
    (u_a_norm, u_av_norm, u_ws, u_bs, u_kvn, u_bn, u_rel, u_fn, u_cw, u_cb, u_fin) = [
        (it[3],) + so for it, so in zip(small_items, small_out)]
    vecD = lambda u: tuple(t.reshape(D) for t in u)
    u_kvn, u_fin = vecD(u_kvn), vecD(u_fin)

    order = [u_a_norm, u_a_w_in, u_av_norm, u_ws, u_bs, u_a_w_out, u_kvn, u_w_kv, u_bn, u_b_w_q, u_rel, u_b_w_o,
             u_fn, u_f_w_in, u_cw, u_cb, u_f_w_down, u_fin]
    outs = [loss, grad_x.reshape(B, S, D)]
    for k in range(4):
        outs += [u[k] for u in order]
    return tuple(outs)
```

```python
import functools

import jax
import jax.numpy as jnp
from jax import lax
from jax.experimental import pallas as pl
from jax.experimental.pallas import tpu as pltpu

F32 = jnp.float32
BF16 = jnp.bfloat16
MESH = pl.DeviceIdType.MESH

N_DEV = 8
EPS = 1e-6
NEG_INF = -1e30
CHUNK = 64
LEFT_CHUNKS = 8
REL_CLIP = 128
HEAD_DIM = 64
GMLP_BLOCK = 128
Q_TILE = 2 * CHUNK
PAD = LEFT_CHUNKS * CHUNK
WIN = PAD + Q_TILE
SKEW = WIN + Q_TILE
REL_PAD = 384
FWD_HEADS_PER_STEP = 8
BWD_HEADS_PER_STEP = 4
TILES_PER_STEP = 4
STRIP = 32
ROWS = 32
ADAM_LR, ADAM_B1, ADAM_B2, ADAM_EPS, ADAM_WD, ADAM_STEP = 0.001, 0.9, 0.999, 1e-08, 0.01, 10
VMEM_LIMIT = 60 * 1024 * 1024
TOKEN_TILE = 512
MATMUL_TILE = 1024
FFN_BLOCKS_PER_STEP = 4


def _params(sem=None):
    return pltpu.CompilerParams(dimension_semantics=sem, vmem_limit_bytes=VMEM_LIMIT)


def _tile(n, pref):
    if n <= pref:
        return n
    for t in range(pref - pref % 8, 7, -8):
        if n % t == 0:
            return t
    return n


def _gelu(x):
    return 0.5 * x * (1.0 + jnp.tanh(0.7978845608028654 * (x + 0.044715 * x * x * x)))


def _gelu_grad(x):
    t = jnp.tanh(0.7978845608028654 * (x + 0.044715 * x * x * x))
    return 0.5 * (1.0 + t) + 0.5 * x * (1.0 - t * t) * 0.7978845608028654 * (1.0 + 3 * 0.044715 * x * x)


def _sigmoid(x):
    return 1.0 / (1.0 + jnp.exp(-x))


def _dot(a, b):
    return jnp.dot(a, b, preferred_element_type=F32)


def _dot_nt(a, b):
    return lax.dot_general(a, b, (((1,), (1,)), ((), ())), preferred_element_type=F32)


def _dot_tn(a, b):
    return lax.dot_general(a, b, (((0,), (0,)), ((), ())), preferred_element_type=F32)


def _split3(x):
    hi = x.astype(BF16)
    r1 = x - hi.astype(F32)
    mid = r1.astype(BF16)
    lo = (r1 - mid.astype(F32)).astype(BF16)
    return hi, mid, lo


def _mesh_pos():
    return lax.axis_index("x"), lax.axis_index("y"), lax.axis_index("c")


class _Rider:
    def __init__(self, arrs, out_shapes, sems, start, finish):
        self.arrs, self.out_shapes, self.sems, self.start, self.finish = arrs, out_shapes, sems, start, finish


def _gather_rider(arrs):
    n = len(arrs)

    def tools(ins, outs, sems):
        send_sems, recv_sems, local_sems = sems
        x, y, c = _mesh_pos()
        me, sibling = (x, y, c), (x, y, 1 - c)
        chips = [(1 - x, y), (x, 1 - y), (1 - x, 1 - y)]

        def slot(a, block):
            px, py, pc = block
            return outs[a].at[4 * px + 2 * py + pc]

        def copy(a, k, block, to, src=None):
            dst = slot(a, block)
            return pltpu.make_async_remote_copy(
                src_ref=dst if src is None else src, dst_ref=dst,
                send_sem=send_sems.at[a, k], recv_sem=recv_sems.at[a, k], device_id=to, device_id_type=MESH)

        def first(a):
            cps = [copy(a, 0, me, sibling, src=ins[a])]
            return cps + [copy(a, 1 + j, me, (*chip, c), src=ins[a]) for j, chip in enumerate(chips)]

        def mine(a):
            return pltpu.make_async_copy(ins[a], slot(a, me), local_sems.at[a])

        return me, sibling, chips, c, copy, first, mine

    def start(ins, outs, sems):
        _, _, _, _, _, first, mine = tools(ins, outs, sems)
        for a in range(n):
            mine(a).start()
            for cp in first(a):
                cp.start()

    def finish(ins, outs, sems):
        me, sibling, chips, c, copy, first, mine = tools(ins, outs, sems)
        passed = []
        for j, chip in enumerate(chips):
            for a in range(n):
                copy(a, 1 + j, (*chip, c), me).wait_recv()
                fwd = copy(a, 4 + j, (*chip, c), sibling)
                fwd.start()
                passed.append(fwd)
        for a in range(n):
            copy(a, 0, sibling, me).wait_recv()
            for j, chip in enumerate(chips):
                copy(a, 4 + j, (*chip, 1 - c), me).wait_recv()
        for a in range(n):
            for cp in first(a):
                cp.wait_send()
        for cp in passed:
            cp.wait_send()
        for a in range(n):
            mine(a).wait()

    return _Rider(list(arrs), [jax.ShapeDtypeStruct((N_DEV,) + a.shape, a.dtype) for a in arrs],
                  [pltpu.SemaphoreType.DMA((n, 7)), pltpu.SemaphoreType.DMA((n, 7)), pltpu.SemaphoreType.DMA((n,))],
                  start, finish)


def _sibling_rider(arrs):
    n = len(arrs)

    def copies(ins, outs, sems):
        x, y, c = _mesh_pos()
        return [pltpu.make_async_remote_copy(
            src_ref=ins[a].at[:, pl.ds(1 - c, 1)], dst_ref=outs[a],
            send_sem=sems[0].at[a], recv_sem=sems[1].at[a], device_id=(x, y, 1 - c), device_id_type=MESH)
            for a in range(n)]

    def start(ins, outs, sems):
        for cp in copies(ins, outs, sems):
            cp.start()

    def finish(ins, outs, sems):
        for cp in copies(ins, outs, sems):
            cp.wait()

    return _Rider(list(arrs), [jax.ShapeDtypeStruct((4, 1) + a.shape[2:], a.dtype) for a in arrs],
                  [pltpu.SemaphoreType.DMA((n,)), pltpu.SemaphoreType.DMA((n,))], start, finish)


def _chip_rider(arrs, ks=(1, 2, 3)):
    n = len(arrs)

    def copies(ins, outs, sems):
        x, y, c = _mesh_pos()
        cps = []
        for a in range(n):
            for s, k in enumerate(ks):
                px = x if k < 2 else 1 - x
                py = y if k == 2 else 1 - y
                cps.append(pltpu.make_async_remote_copy(
                    src_ref=ins[a].at[2 * px + py], dst_ref=outs[a].at[s],
                    send_sem=sems[0].at[a, s], recv_sem=sems[1].at[a, s],
                    device_id=(px, py, c), device_id_type=MESH))
        return cps

    def start(ins, outs, sems):
        for cp in copies(ins, outs, sems):
            cp.start()

    def finish(ins, outs, sems):
        for cp in copies(ins, outs, sems):
            cp.wait()

    return _Rider(list(arrs), [jax.ShapeDtypeStruct((len(ks),) + a.shape[1:], a.dtype) for a in arrs],
                  [pltpu.SemaphoreType.DMA((n, len(ks))), pltpu.SemaphoreType.DMA((n, len(ks)))], start, finish)


def _join_riders(riders):
    def split(seq, counts):
        out, at = [], 0
        for k in counts:
            out.append(seq[at:at + k])
            at += k
        return out

    n_in = [len(r.arrs) for r in riders]
    n_out = [len(r.out_shapes) for r in riders]
    n_sem = [len(r.sems) for r in riders]

    def run(which):
        def fn(ins, outs, sems):
            for r, i, o, s in zip(riders, split(ins, n_in), split(outs, n_out), split(sems, n_sem)):
                getattr(r, which)(i, o, s)
        return fn

    return _Rider([a for r in riders for a in r.arrs], [o for r in riders for o in r.out_shapes],
                  [s for r in riders for s in r.sems], run("start"), run("finish"))


def _run_rider(rider, name):
    n_in, n_out = len(rider.arrs), len(rider.out_shapes)

    def body(*refs):
        ins, outs, sems = refs[:n_in], refs[n_in:n_in + n_out], refs[n_in + n_out:]
        rider.start(ins, outs, sems)
        rider.finish(ins, outs, sems)

    any_spec = pl.BlockSpec(memory_space=pl.ANY)
    return pl.pallas_call(
        body, name=name, out_shape=list(rider.out_shapes), in_specs=[any_spec] * n_in, out_specs=[any_spec] * n_out,
        scratch_shapes=list(rider.sems),
    )(*rider.arrs)


def _call(body, *, name, grid, in_specs, out_specs, out_shape, args, scratch_shapes=(), rider=None):
    params = _params(("arbitrary",) * len(grid))
    if rider is None:
        return pl.pallas_call(body, name=name, grid=grid, in_specs=in_specs, out_specs=out_specs, out_shape=out_shape,
                              scratch_shapes=list(scratch_shapes), compiler_params=params)(*args)
    single = not isinstance(out_shape, (list, tuple))
    outs = [out_shape] if single else list(out_shape)
    ospecs = [out_specs] if single else list(out_specs)
    n_in, n_out, n_scr = len(in_specs), len(outs), len(scratch_shapes)
    r_in, r_out = len(rider.arrs), len(rider.out_shapes)

    def hosted(*refs):
        refs = list(refs)
        ins, rins = refs[:n_in], refs[n_in:n_in + r_in]
        refs = refs[n_in + r_in:]
        houts, routs = refs[:n_out], refs[n_out:n_out + r_out]
        refs = refs[n_out + r_out:]
        scr, rsems = refs[:n_scr], refs[n_scr:]
        ids = [pl.program_id(a) for a in range(len(grid))]
        first = functools.reduce(lambda p, q: p & q, [i == 0 for i in ids])
        last = functools.reduce(lambda p, q: p & q, [i == g - 1 for i, g in zip(ids, grid)])

        @pl.when(first)
        def _():
            rider.start(rins, routs, rsems)

        body(*ins, *houts, *scr)

        @pl.when(last)
        def _():
            rider.finish(rins, routs, rsems)

    any_spec = pl.BlockSpec(memory_space=pl.ANY)
    res = pl.pallas_call(
        hosted, name=name, grid=grid, in_specs=list(in_specs) + [any_spec] * r_in,
        out_specs=ospecs + [any_spec] * r_out, out_shape=outs + list(rider.out_shapes),
        scratch_shapes=list(scratch_shapes) + list(rider.sems), compiler_params=params,
    )(*args, *rider.arrs)
    return (res[0] if single else list(res[:n_out])), list(res[n_out:])


def _sibling_sum(part, landed, pos, name):
    _, _, rows, cols = part.shape
    tr = _tile(rows, 512)

    def body(pos_ref, p_ref, l_ref, own_ref, all_ref):
        s = p_ref[0, 0] + l_ref[0, 0]
        all_ref[0] = s.astype(BF16)

        @pl.when(pl.program_id(1) == pos_ref[1])
        def _():
            own_ref[...] = s

    return pl.pallas_call(
        body, name=name,
        grid_spec=pltpu.PrefetchScalarGridSpec(
            num_scalar_prefetch=1, grid=(rows // tr, 4),
            in_specs=[pl.BlockSpec((1, 1, tr, cols), lambda i, k, pos: (k, pos[0], i, 0)),
                      pl.BlockSpec((1, 1, tr, cols), lambda i, k, pos: (k, 0, i, 0))],
            out_specs=[pl.BlockSpec((tr, cols), lambda i, k, pos: (i, 0)),
                       pl.BlockSpec((1, tr, cols), lambda i, k, pos: (k, i, 0))]),
        out_shape=[jax.ShapeDtypeStruct((rows, cols), F32), jax.ShapeDtypeStruct((4, rows, cols), BF16)],
        compiler_params=_params(("arbitrary", "arbitrary")),
    )(pos, part, landed)


def _rms(x, g):
    r = lax.rsqrt(jnp.mean(x * x, axis=-1, keepdims=True) + EPS)
    return x * r, r


def _norm_matmul(h, g, w, *, flat, nbk, name, scale=1.0, rider=None):
    T, D = h.shape
    nb, _, bn = w.shape
    tm = _tile(T, MATMUL_TILE)

    def body(h_ref, g_ref, w_ref, o_ref, n_ref):
        @pl.when(pl.program_id(1) == 0)
        def _():
            xh, _ = _rms(h_ref[...], None)
            n_ref[...] = (xh * g_ref[...]).astype(BF16)

        n = n_ref[...]
        for k in range(nbk):
            r = _dot(n, w_ref[k])
            r = (r if scale == 1.0 else r * scale).astype(BF16)
            if flat:
                o_ref[:, k * bn:(k + 1) * bn] = r
            else:
                o_ref[k] = r

    if flat:
        out_shape = jax.ShapeDtypeStruct((T, nb * bn), BF16)
        out_spec = pl.BlockSpec((tm, nbk * bn), lambda i, j: (i, j))
    else:
        out_shape = jax.ShapeDtypeStruct((nb, T, bn), BF16)
        out_spec = pl.BlockSpec((nbk, tm, bn), lambda i, j: (j, i, 0))
    return _call(
        body, name=name, grid=(T // tm, nb // nbk),
        in_specs=[pl.BlockSpec((tm, D), lambda i, j: (i, 0)),
                  pl.BlockSpec((1, D), lambda i, j: (0, 0)),
                  pl.BlockSpec((nbk, D, bn), lambda i, j: (j, 0, 0))],
        out_specs=[out_spec, pl.BlockSpec((tm, D), lambda i, j: (i, 0))],
        out_shape=[out_shape, jax.ShapeDtypeStruct((T, D), BF16)],
        args=(h, g, w), rider=rider)


def _matmul_nt(dy, w, *, flat, nbk, name, norm=None, out_dtype=BF16, rider=None):
    nb, R, bn = w.shape
    T = dy.shape[0] if flat else dy.shape[1]
    tm = _tile(T, MATMUL_TILE)
    nj = nb // nbk

    def body(*refs):
        if norm is None:
            dy_ref, w_ref, o_ref, acc_ref = refs
        else:
            dy_ref, w_ref, h_ref, g_ref, dres_ref, o_ref, dg_ref, acc_ref = refs
        i, j = pl.program_id(0), pl.program_id(1)

        @pl.when(j == 0)
        def _():
            acc_ref[...] = jnp.zeros_like(acc_ref)

        acc = acc_ref[...]
        for k in range(nbk):
            d = dy_ref[:, k * bn:(k + 1) * bn] if flat else dy_ref[k]
            acc = acc + _dot_nt(d.astype(BF16), w_ref[k])
        acc_ref[...] = acc

        @pl.when(j == nj - 1)
        def _():
            if norm is None:
                o_ref[...] = acc.astype(out_dtype)
            else:
                xh, r = _rms(h_ref[...], None)

                @pl.when(i == 0)
                def _():
                    dg_ref[...] = jnp.zeros_like(dg_ref)

                dg_ref[0:1, :] += jnp.sum(acc * xh, axis=0, keepdims=True)
                dn = acc * g_ref[...]
                o_ref[...] = dres_ref[...] + r * (dn - xh * jnp.mean(dn * xh, axis=-1, keepdims=True))

    if flat:
        dy_spec = pl.BlockSpec((tm, nbk * bn), lambda i, j: (i, j))
    else:
        dy_spec = pl.BlockSpec((nbk, tm, bn), lambda i, j: (j, i, 0))
    w_spec = pl.BlockSpec((nbk, R, bn), lambda i, j: (j, 0, 0))
    row_spec = pl.BlockSpec((tm, R), lambda i, j: (i, 0))
    if norm is None:
        in_specs, args = [dy_spec, w_spec], (dy, w)
        out_specs = row_spec
        out_shape = jax.ShapeDtypeStruct((T, R), out_dtype)
    else:
        in_specs = [dy_spec, w_spec, row_spec, pl.BlockSpec((1, R), lambda i, j: (0, 0)), row_spec]
        args = (dy, w) + tuple(norm)
        out_specs = [row_spec, pl.BlockSpec((8, R), lambda i, j: (0, 0))]
        out_shape = [jax.ShapeDtypeStruct((T, R), F32), jax.ShapeDtypeStruct((8, R), F32)]
    return _call(
        body, name=name, grid=(T // tm, nj), in_specs=in_specs, out_specs=out_specs, out_shape=out_shape,
        scratch_shapes=[pltpu.VMEM((tm, R), F32)], args=args, rider=rider)


def _wgrad_cols(n, dy, *, flat, nb, nbk, name, rider=None):
    T, D = n.shape
    halves = isinstance(dy, tuple)
    bn = dy.shape[1] // nb if flat else (dy[0] if halves else dy).shape[2]
    tt = _tile(T, MATMUL_TILE)
    nt = T // tt
    nj = nb // nbk

    def body(*refs):
        n_ref, dy_refs, (o_ref, acc_ref) = refs[0], refs[1:-2], refs[-2:]
        j, t = pl.program_id(0), pl.program_id(1)

        @pl.when(t == 0)
        def _():
            acc_ref[...] = jnp.zeros_like(acc_ref)

        def accumulate(dy_ref):
            nv = n_ref[...]
            for k in range(nbk):
                d = dy_ref[:, k * bn:(k + 1) * bn] if flat else dy_ref[k]
                acc_ref[k] += _dot_tn(nv, d)

        if halves:
            pl.when(j < nj // 2)(lambda: accumulate(dy_refs[0]))
            pl.when(j >= nj // 2)(lambda: accumulate(dy_refs[1]))
        else:
            accumulate(dy_refs[0])

        @pl.when(t == nt - 1)
        def _():
            o_ref[...] = acc_ref[...]

    if flat:
        dy_specs, dys = [pl.BlockSpec((tt, nbk * bn), lambda j, t: (t, j))], [dy]
    elif halves:
        first = pl.BlockSpec((nbk, tt, bn), lambda j, t: (jnp.minimum(j, nj // 2 - 1),
                                                          jnp.where(j < nj // 2, t, nt - 1), 0))
        second = pl.BlockSpec((nbk, tt, bn), lambda j, t: (jnp.maximum(j - nj // 2, 0),
                                                           jnp.where(j >= nj // 2, t, 0), 0))
        dy_specs, dys = [first, second], list(dy)
    else:
        dy_specs, dys = [pl.BlockSpec((nbk, tt, bn), lambda j, t: (j, t, 0))], [dy]
    return _call(
        body, name=name, grid=(nj, nt),
        in_specs=[pl.BlockSpec((tt, D), lambda j, t: (t, 0))] + dy_specs,
        out_specs=pl.BlockSpec((nbk, D, bn), lambda j, t: (j, 0, 0)),
        out_shape=jax.ShapeDtypeStruct((nb, D, bn), F32),
        scratch_shapes=[pltpu.VMEM((nbk, D, bn), F32)], args=[n] + dys, rider=rider)


def _wgrad_rows(xa, dh, *, flat, tk, name, rider=None):
    T, D = dh.shape
    nk = xa.shape[1] // tk if flat else xa.shape[0]
    tt = _tile(T, MATMUL_TILE)
    nt = T // tt

    def body(x_ref, dh_ref, o_ref, acc_ref):
        t = pl.program_id(1)

        @pl.when(t == 0)
        def _():
            acc_ref[...] = jnp.zeros_like(acc_ref)

        xv = x_ref[...] if flat else x_ref[0]
        acc_ref[...] += _dot_tn(xv, dh_ref[...].astype(BF16))

        @pl.when(t == nt - 1)
        def _():
            o_ref[...] = acc_ref[...]

    x_spec = pl.BlockSpec((tt, tk), lambda j, t: (t, j)) if flat else pl.BlockSpec((1, tt, tk), lambda j, t: (j, t, 0))
    return _call(
        body, name=name, grid=(nk, nt),
        in_specs=[x_spec, pl.BlockSpec((tt, D), lambda j, t: (t, 0))],
        out_specs=pl.BlockSpec((tk, D), lambda j, t: (j, 0)),
        out_shape=jax.ShapeDtypeStruct((nk * tk, D), F32),
        scratch_shapes=[pltpu.VMEM((tk, D), F32)], args=(xa, dh), rider=rider)


def _matmul_residual(xa, w, res, name):
    T, K = xa.shape
    D = w.shape[1]
    tm = _tile(T, MATMUL_TILE)

    def body(x_ref, w_ref, r_ref, o_ref):
        o_ref[...] = r_ref[...] + _dot(x_ref[...], w_ref[...])

    return pl.pallas_call(
        body, name=name, grid=(T // tm,),
        in_specs=[pl.BlockSpec((tm, K), lambda i: (i, 0)), pl.BlockSpec((K, D), lambda i: (0, 0)),
                  pl.BlockSpec((tm, D), lambda i: (i, 0))],
        out_specs=pl.BlockSpec((tm, D), lambda i: (i, 0)),
        out_shape=jax.ShapeDtypeStruct((T, D), F32),
        compiler_params=_params(("arbitrary",)),
    )(xa, w, res)


def _gmlp_gate(z, ws, bst, gv, G, gd):
    D = G * gd
    u = _gelu(z[:, :D].astype(F32))
    v = _gelu(z[:, D:].astype(F32))
    vh, r = _rms(v, None)
    vn = (vh * gv).astype(BF16)
    return u, v, vh, r, vn


def _gmlp_forward(z, ws, bst, gv, w_out, x, *, name, rider=None):
    T, D2 = z.shape
    D = D2 // 2
    G = ws.shape[0]
    gd = D // G
    tb = _tile(T, 256)
    nblk = tb // GMLP_BLOCK

    def body(z_ref, ws_ref, b_ref, gv_ref, wo_ref, x_ref, gated_ref, h_ref):
        u, _, _, _, vn = _gmlp_gate(z_ref[...], None, None, gv_ref[...], G, gd)
        for n in range(nblk):
            rows = slice(n * GMLP_BLOCK, (n + 1) * GMLP_BLOCK)
            for gi in range(G):
                cols = slice(gi * gd, (gi + 1) * gd)
                s = _dot(ws_ref[gi], vn[rows, cols]) + b_ref[:, gi:gi + 1]
                gated_ref[rows, cols] = (u[rows, cols] * s).astype(BF16)
        h_ref[...] = x_ref[...] + _dot(gated_ref[...], wo_ref[...])

    return _call(
        body, name=name, grid=(T // tb,),
        in_specs=[pl.BlockSpec((tb, D2), lambda i: (i, 0)), pl.BlockSpec(ws.shape, lambda i: (0, 0, 0)),
                  pl.BlockSpec(bst.shape, lambda i: (0, 0)), pl.BlockSpec((1, D), lambda i: (0, 0)),
                  pl.BlockSpec((D, D), lambda i: (0, 0)), pl.BlockSpec((tb, D), lambda i: (i, 0))],
        out_specs=[pl.BlockSpec((tb, D), lambda i: (i, 0)), pl.BlockSpec((tb, D), lambda i: (i, 0))],
        out_shape=[jax.ShapeDtypeStruct((T, D), BF16), jax.ShapeDtypeStruct((T, D), F32)],
        args=(z, ws, bst, gv, w_out, x), rider=rider)


def _gmlp_backward(z, dgated, ws, bst, gv, mask, *, name, rider=None):
    T, D2 = z.shape
    D = D2 // 2
    G = ws.shape[0]
    gd = D // G
    tb = _tile(T, 256)
    nblk = tb // GMLP_BLOCK

    def body(z_ref, dg_ref, ws_ref, b_ref, gv_ref, mask_ref, dz_ref, dws_ref, db_ref, dgv_ref, dvn_ref):
        @pl.when(pl.program_id(0) == 0)
        def _():
            dws_ref[...] = jnp.zeros_like(dws_ref)
            db_ref[...] = jnp.zeros_like(db_ref)
            dgv_ref[...] = jnp.zeros_like(dgv_ref)

        zf = z_ref[...]
        u, v, vh, r, vn = _gmlp_gate(zf, None, None, gv_ref[...], G, gd)
        dg = dg_ref[...].astype(F32)
        for n in range(nblk):
            rows = slice(n * GMLP_BLOCK, (n + 1) * GMLP_BLOCK)
            for gi in range(G):
                cols = slice(gi * gd, (gi + 1) * gd)
                vblk = vn[rows, cols]
                s = _dot(ws_ref[gi], vblk) + b_ref[:, gi:gi + 1]
                dgb = dg[rows, cols]
                ds = dgb * u[rows, cols]
                dsb = ds.astype(BF16)
                dz_ref[rows, cols] = (dgb * s * _gelu_grad(zf[rows, cols].astype(F32))).astype(BF16)
                dvn_ref[rows, cols] = _dot_tn(ws_ref[gi], dsb)
                dws_ref[gi] += _dot_nt(dsb, vblk) * mask_ref[...]
                db_ref[:, gi:gi + 1] += jnp.sum(ds, axis=1, keepdims=True)
        dvn = dvn_ref[...]
        dgv_ref[0:1, :] += jnp.sum(dvn * vh, axis=0, keepdims=True)
        dn = dvn * gv_ref[...]
        dv = r * (dn - vh * jnp.mean(dn * vh, axis=-1, keepdims=True))
        dz_ref[:, D:] = (dv * _gelu_grad(zf[:, D:].astype(F32))).astype(BF16)

    return _call(
        body, name=name, grid=(T // tb,),
        in_specs=[pl.BlockSpec((tb, D2), lambda i: (i, 0)), pl.BlockSpec((tb, D), lambda i: (i, 0)),
                  pl.BlockSpec(ws.shape, lambda i: (0, 0, 0)), pl.BlockSpec(bst.shape, lambda i: (0, 0)),
                  pl.BlockSpec((1, D), lambda i: (0, 0)), pl.BlockSpec(mask.shape, lambda i: (0, 0))],
        out_specs=[pl.BlockSpec((tb, D2), lambda i: (i, 0)), pl.BlockSpec(ws.shape, lambda i: (0, 0, 0)),
                   pl.BlockSpec(bst.shape, lambda i: (0, 0)), pl.BlockSpec((8, D), lambda i: (0, 0))],
        out_shape=[jax.ShapeDtypeStruct((T, D2), BF16), jax.ShapeDtypeStruct(ws.shape, F32),
                   jax.ShapeDtypeStruct(bst.shape, F32), jax.ShapeDtypeStruct((8, D), F32)],
        scratch_shapes=[pltpu.VMEM((tb, D), F32)], args=(z, dgated, ws, bst, gv, mask), rider=rider)


def _shift_rows(x, k):
    return pltpu.roll(x, k % x.shape[0], axis=0)


def _conv3(ext, cw):
    return (cw[0:1] * _shift_rows(ext, 2)[8:] + cw[1:2] * _shift_rows(ext, 1)[8:] + cw[2:3] * ext[8:])


def _ffn_forward(a, cw, cb, wd, h, seq, *, name, rider=None):
    _, T, F = a.shape
    D = h.shape[1]
    tm = _tile(seq, TOKEN_TILE)
    hb = tm // 16

    def body(a_ref, ap_ref, cw_ref, cb_ref, wd_ref, h_ref, act_ref, c_ref, o_ref, acc_ref):
        i, j = pl.program_id(0), pl.program_id(1)
        keep = ((i * tm) % seq != 0).astype(F32)

        def conv(b):
            ext = jnp.concatenate([ap_ref[b, 8:16].astype(F32) * keep, a_ref[b].astype(F32)], axis=0)
            return _conv3(ext, cw_ref[b]) + cb_ref[b]

        up, gate = conv(j), conv(j + 4)
        c_ref[j] = up.astype(BF16)
        c_ref[j + 4] = gate.astype(BF16)
        act = (gate * _sigmoid(gate) * up).astype(BF16)
        act_ref[0] = act

        @pl.when(j == 0)
        def _():
            acc_ref[...] = h_ref[...]

        acc_ref[...] += _dot(act, wd_ref[0])

        @pl.when(j == 3)
        def _():
            o_ref[...] = acc_ref[...]

    return _call(
        body, name=name, grid=(T // tm, 4),
        in_specs=[pl.BlockSpec((8, tm, F), lambda i, j: (0, i, 0)),
                  pl.BlockSpec((8, 16, F), lambda i, j: (0, jnp.maximum(i * hb - 1, 0), 0)),
                  pl.BlockSpec((8, 3, F), lambda i, j: (0, 0, 0)), pl.BlockSpec((8, 1, F), lambda i, j: (0, 0, 0)),
                  pl.BlockSpec((1, F, D), lambda i, j: (j, 0, 0)), pl.BlockSpec((tm, D), lambda i, j: (i, 0))],
        out_specs=[pl.BlockSpec((1, tm, F), lambda i, j: (j, i, 0)), pl.BlockSpec((8, tm, F), lambda i, j: (0, i, 0)),
                   pl.BlockSpec((tm, D), lambda i, j: (i, 0))],
        out_shape=[jax.ShapeDtypeStruct((4, T, F), BF16), jax.ShapeDtypeStruct((8, T, F), BF16),
                   jax.ShapeDtypeStruct((T, D), F32)],
        scratch_shapes=[pltpu.VMEM((tm, D), F32)], args=(a, a, cw, cb, wd, h), rider=rider)


def _ffn_backward(dh, c, a, cw, wd, seq, *, name, rider=None):
    _, T, F = a.shape
    D = dh.shape[1]
    tm = _tile(seq, TOKEN_TILE)
    hb = tm // 16
    nt = T // tm

    def body(dh_ref, dhn_ref, cu_ref, cg_ref, cun_ref, cgn_ref, au_ref, ag_ref, cw_ref, wd_ref, da_ref, st_ref):
        i, j = pl.program_id(0), pl.program_id(1)
        keep_next = (((i + 1) * tm) % seq != 0).astype(F32)

        @pl.when((i == 0) & (j == 0))
        def _():
            st_ref[...] = jnp.zeros_like(st_ref)

        dhe = jnp.concatenate([dh_ref[...], dhn_ref[...] * keep_next], axis=0).astype(BF16)
        dact = _dot_nt(dhe, wd_ref[0])
        up = jnp.concatenate([cu_ref[0].astype(F32), cun_ref[0, 0:8].astype(F32)], axis=0)
        gate = jnp.concatenate([cg_ref[0].astype(F32), cgn_ref[0, 0:8].astype(F32)], axis=0)
        sg = _sigmoid(gate)
        gs = gate * sg
        d_up = dact * gs
        d_gate = dact * up * (sg + gs * (1.0 - sg))

        def finish(b, a_ref, dc):
            w = cw_ref[b]
            dm, u1, u2 = dc[:tm], _shift_rows(dc, -1)[:tm], _shift_rows(dc, -2)[:tm]
            da_ref[b] = (w[2:3] * dm + w[1:2] * u1 + w[0:1] * u2).astype(BF16)
            av = a_ref[0].astype(F32)
            st_ref[b, 0:1, :] += jnp.sum(u2 * av, axis=0, keepdims=True)
            st_ref[b, 1:2, :] += jnp.sum(u1 * av, axis=0, keepdims=True)
            st_ref[b, 2:3, :] += jnp.sum(dm * av, axis=0, keepdims=True)
            st_ref[b, 3:4, :] += jnp.sum(dm, axis=0, keepdims=True)

        finish(j, au_ref, d_up)
        finish(j + 4, ag_ref, d_gate)

    nxt = lambda i: jnp.minimum((i + 1) * hb, T // 16 - 1)
    return _call(
        body, name=name, grid=(nt, 4),
        in_specs=[pl.BlockSpec((tm, D), lambda i, j: (i, 0)),
                  pl.BlockSpec((8, D), lambda i, j: (jnp.minimum((i + 1) * (tm // 8), T // 8 - 1), 0)),
                  pl.BlockSpec((1, tm, F), lambda i, j: (j, i, 0)), pl.BlockSpec((1, tm, F), lambda i, j: (j + 4, i, 0)),
                  pl.BlockSpec((1, 16, F), lambda i, j: (j, nxt(i), 0)),
                  pl.BlockSpec((1, 16, F), lambda i, j: (j + 4, nxt(i), 0)),
                  pl.BlockSpec((1, tm, F), lambda i, j: (j, i, 0)), pl.BlockSpec((1, tm, F), lambda i, j: (j + 4, i, 0)),
                  pl.BlockSpec((8, 3, F), lambda i, j: (0, 0, 0)),
                  pl.BlockSpec((1, F, D), lambda i, j: (j, 0, 0))],
        out_specs=[pl.BlockSpec((8, tm, F), lambda i, j: (0, i, 0)), pl.BlockSpec((8, 8, F), lambda i, j: (0, 0, 0))],
        out_shape=[jax.ShapeDtypeStruct((8, T, F), BF16), jax.ShapeDtypeStruct((8, 8, F), F32)],
        args=(dh, dh, c, c, c, c, a, a, cw, wd), rider=rider)


def _ffn_fused_forward(h, g, w_in, cw, cb, wd, seq, *, name, rider=None):
    T, D = h.shape
    F = w_in.shape[2]
    tm = _tile(seq, TOKEN_TILE // 2)
    bps = 4
    nj = 4 // bps

    def body(h_ref, g_ref, wu_ref, wg_ref, cw_ref, cb_ref, wd_ref,
             au_ref, ag_ref, cu_ref, cg_ref, act_ref, n_ref, o_ref, acc_ref, carry_ref, *work_refs):
        eu_refs, eg_refs, stage_refs = work_refs[:bps], work_refs[bps:2 * bps], work_refs[2 * bps:]
        i, j = pl.program_id(0), pl.program_id(1)
        keep = ((i * tm) % seq != 0).astype(F32)

        @pl.when((i == 0) & (j == 0))
        def _():
            carry_ref[...] = jnp.zeros_like(carry_ref)

        @pl.when(j == 0)
        def _():
            xh, _ = _rms(h_ref[...], None)
            n_ref[...] = (xh * g_ref[...]).astype(BF16)
            acc_ref[...] = h_ref[...]

        n = n_ref[...]

        def project(b, k, w_ref, a_ref, ext_ref):
            a = _dot(n, w_ref[k]).astype(BF16)
            a_ref[k] = a
            ext_ref[0:8, :] = carry_ref[b] * keep
            ext_ref[8:, :] = a.astype(F32)
            carry_ref[b] = ext_ref[tm:tm + 8, :]

        def conv(b, ext_ref, r):
            x, w = ext_ref[r:r + ROWS + 8, :], cw_ref[b]
            return (w[0:1] * _shift_rows(x, 2) + w[1:2] * _shift_rows(x, 1) + w[2:3] * x)[8:] + cb_ref[b]

        for k in range(bps):
            project(j * bps + k, k, wu_ref, au_ref, eu_refs[k])
            project(j * bps + k + 4, k, wg_ref, ag_ref, eg_refs[k])
        outs = []
        for k in range(bps):
            for r in range(0, tm, ROWS):
                up, gate = conv(j * bps + k, eu_refs[k], r), conv(j * bps + k + 4, eg_refs[k], r)
                cu_ref[k, r:r + ROWS, :] = up.astype(BF16)
                cg_ref[k, r:r + ROWS, :] = gate.astype(BF16)
                stage_refs[k][r:r + ROWS, :] = (gate * _sigmoid(gate) * up).astype(BF16)
            act = stage_refs[k][...]
            act_ref[k] = act
            outs.append(_dot(act, wd_ref[k]))
        acc_ref[...] += functools.reduce(lambda p, q: p + q, outs)

        @pl.when(j == nj - 1)
        def _():
            o_ref[...] = acc_ref[...]

    blk = pl.BlockSpec((bps, tm, F), lambda i, j: (j, i, 0))
    row = pl.BlockSpec((tm, D), lambda i, j: (i, 0))
    half = jax.ShapeDtypeStruct((4, T, F), BF16)
    work = [pltpu.VMEM((tm + 8, F), F32)] * (2 * bps) + [pltpu.VMEM((tm, F), BF16)] * bps
    return _call(
        body, name=name, grid=(T // tm, nj),
        in_specs=[row, pl.BlockSpec((1, D), lambda i, j: (0, 0)),
                  pl.BlockSpec((bps, D, F), lambda i, j: (j, 0, 0), pipeline_mode=pl.Buffered(1)),
                  pl.BlockSpec((bps, D, F), lambda i, j: (j + nj, 0, 0), pipeline_mode=pl.Buffered(1)),
                  pl.BlockSpec((8, 3, F), lambda i, j: (0, 0, 0)), pl.BlockSpec((8, 1, F), lambda i, j: (0, 0, 0)),
                  pl.BlockSpec((bps, F, D), lambda i, j: (j, 0, 0), pipeline_mode=pl.Buffered(1))],
        out_specs=[blk, blk, blk, blk, blk, row, row],
        out_shape=[half, half, half, half, half, jax.ShapeDtypeStruct((T, D), BF16), jax.ShapeDtypeStruct((T, D), F32)],
        scratch_shapes=[pltpu.VMEM((tm, D), F32), pltpu.VMEM((8, 8, F), F32)] + work,
        args=(h, g, w_in, w_in, cw, cb, wd), rider=rider)


def _ffn_fused_backward(dh, cu, cg, au, ag, cw, wd, w_in, h, g, seq, *, name, rider=None):
    T, D = dh.shape
    F = wd.shape[1]
    tm = _tile(seq, TOKEN_TILE // 2)
    hb = tm // 16
    nt = T // tm
    bps = FFN_BLOCKS_PER_STEP
    nj = 4 // bps

    def body(dh_ref, dhn_ref, cu_ref, cg_ref, cun_ref, cgn_ref, au_ref, ag_ref, cw_ref, wd_ref, wu_ref, wg_ref,
             h_ref, g_ref, dau_ref, dag_ref, st_ref, o_ref, dg_ref, acc_ref, dact_ref, du_ref, dgt_ref):
        j, i = pl.program_id(0), pl.program_id(1)
        keep_next = (((i + 1) * tm) % seq != 0).astype(F32)
        tile_rows = pl.ds(pl.multiple_of(i * tm, tm), tm) if nj > 1 else slice(0, tm)

        @pl.when((i == 0) & (j == 0))
        def _():
            st_ref[...] = jnp.zeros_like(st_ref)
            dg_ref[...] = jnp.zeros_like(dg_ref)

        dhe = jnp.concatenate([dh_ref[...], dhn_ref[...] * keep_next], axis=0).astype(BF16)
        for k in range(bps):
            dact_ref[k] = _dot_nt(dhe, wd_ref[k])

        def conv_grads(k):
            for r in range(0, tm + 8, ROWS):
                if r < tm:
                    rows = slice(r, r + ROWS)
                    up, gate = cu_ref[k, rows, :].astype(F32), cg_ref[k, rows, :].astype(F32)
                else:
                    rows = slice(tm, tm + 8)
                    up, gate = cun_ref[k, 0:8, :].astype(F32), cgn_ref[k, 0:8, :].astype(F32)
                dact = dact_ref[k, rows, :]
                sg = _sigmoid(gate)
                gs = gate * sg
                du_ref[k, rows, :] = dact * gs
                dgt_ref[k, rows, :] = dact * up * (sg + gs * (1.0 - sg))

        def finish(b, k, a_ref, w_ref, dc_ref, da_ref):
            w = cw_ref[b]
            sums = [jnp.zeros((8, F), F32) for _ in range(4)]
            fold = lambda t: jnp.sum(t.reshape(ROWS // 8, 8, F), axis=0)
            for r in range(0, tm, ROWS):
                dc = dc_ref[k, r:r + ROWS + 8, :]
                dm, u1, u2 = dc[:ROWS], _shift_rows(dc, -1)[:ROWS], _shift_rows(dc, -2)[:ROWS]
                da_ref[k, r:r + ROWS, :] = (w[2:3] * dm + w[1:2] * u1 + w[0:1] * u2).astype(BF16)
                av = a_ref[k, r:r + ROWS, :].astype(F32)
                for s, t in enumerate((u2 * av, u1 * av, dm * av, dm)):
                    sums[s] = sums[s] + fold(t)
            for s in range(4):
                st_ref[b, s:s + 1, :] += jnp.sum(sums[s], axis=0, keepdims=True)
            return _dot_nt(da_ref[k], w_ref[k])

        dn_parts = []
        for k in range(bps):
            conv_grads(k)
            dn_parts.append(finish(j * bps + k, k, au_ref, wu_ref, du_ref, dau_ref))
            dn_parts.append(finish(j * bps + k + 4, k, ag_ref, wg_ref, dgt_ref, dag_ref))
        dn_part = functools.reduce(lambda p, q: p + q, dn_parts)

        @pl.when(j == 0)
        def _():
            acc_ref[tile_rows, :] = dn_part

        @pl.when(j > 0)
        def _():
            acc_ref[tile_rows, :] += dn_part

        @pl.when(j == nj - 1)
        def _():
            acc = acc_ref[tile_rows, :]
            xh, r = _rms(h_ref[...], None)
            dg_ref[0:1, :] += jnp.sum(acc * xh, axis=0, keepdims=True)
            dn = acc * g_ref[...]
            o_ref[...] = dh_ref[...] + r * (dn - xh * jnp.mean(dn * xh, axis=-1, keepdims=True))

    last = lambda j, i: jnp.where(j == nj - 1, i, 0)
    nxt = lambda i: jnp.minimum((i + 1) * hb, T // 16 - 1)
    blk = pl.BlockSpec((bps, tm, F), lambda j, i: (j, i, 0))
    halo = pl.BlockSpec((bps, 16, F), lambda j, i: (j, nxt(i), 0))
    row = pl.BlockSpec((tm, D), lambda j, i: (i, 0))
    work = pltpu.VMEM((bps, tm + 8, F), F32)
    half = jax.ShapeDtypeStruct((4, T, F), BF16)
    once = pl.Buffered(1) if nj == 1 else None
    return _call(
        body, name=name, grid=(nj, nt),
        in_specs=[row, pl.BlockSpec((8, D), lambda j, i: (jnp.minimum((i + 1) * (tm // 8), T // 8 - 1), 0)),
                  blk, blk, halo, halo, blk, blk,
                  pl.BlockSpec((8, 3, F), lambda j, i: (0, 0, 0)),
                  pl.BlockSpec((bps, F, D), lambda j, i: (j, 0, 0), pipeline_mode=once),
                  pl.BlockSpec((bps, D, F), lambda j, i: (j, 0, 0), pipeline_mode=once),
                  pl.BlockSpec((bps, D, F), lambda j, i: (j + nj, 0, 0), pipeline_mode=once),
                  pl.BlockSpec((tm, D), lambda j, i: (last(j, i), 0)), pl.BlockSpec((1, D), lambda j, i: (0, 0))],
        out_specs=[blk, blk, pl.BlockSpec((8, 8, F), lambda j, i: (0, 0, 0)),
                   pl.BlockSpec((tm, D), lambda j, i: (last(j, i), 0)), pl.BlockSpec((8, D), lambda j, i: (0, 0))],
        out_shape=[half, half, jax.ShapeDtypeStruct((8, 8, F), F32),
                   jax.ShapeDtypeStruct((T, D), F32), jax.ShapeDtypeStruct((8, D), F32)],
        scratch_shapes=[pltpu.VMEM((T if nj > 1 else tm, D), F32), work, work, work],
        args=(dh, dh, cu, cg, cu, cg, au, ag, cw, wd, w_in, w_in, h, g), rider=rider)


def _rel_onehot():
    r = lax.broadcasted_iota(jnp.int32, (REL_PAD, SKEW), 0)
    n = lax.broadcasted_iota(jnp.int32, (REL_PAD, SKEW), 1)
    off = jnp.where(n >= WIN, n - SKEW, n)
    idx = jnp.minimum(PAD - off, REL_CLIP) + REL_CLIP
    return (r == idx).astype(BF16)


def _skew(x, sign):
    row = lax.broadcasted_iota(jnp.int32, x.shape, 0)
    for b in range(7):
        x = jnp.where((row >> b) & 1 == 1, pltpu.roll(x, (sign * (1 << b)) % SKEW, axis=1), x)
    return x


def _bias_build(rel, name, rider=None):
    H = rel.shape[0]

    def body(rel_ref, o_ref):
        oh = _rel_onehot()
        hi, mid, lo = _split3(rel_ref[...])
        base = _dot(hi, oh) + _dot(mid, oh) + _dot(lo, oh)
        mine = lax.broadcasted_iota(jnp.int32, (H, 1), 0) == pl.program_id(0)
        row = jnp.sum(jnp.where(mine, base, 0.0), axis=0, keepdims=True)
        q = lax.broadcasted_iota(jnp.int32, (Q_TILE, WIN), 0)
        k = lax.broadcasted_iota(jnp.int32, (Q_TILE, WIN), 1)
        ok = ((q < CHUNK) & (k < WIN - CHUNK)) | ((q >= CHUNK) & (k >= CHUNK))
        t = _skew(jnp.broadcast_to(row, (Q_TILE, SKEW)), 1)
        o_ref[0] = jnp.where(ok, t[:, :WIN], NEG_INF)

    return _call(
        body, name=name, grid=(H,), in_specs=[pl.BlockSpec((H, REL_PAD), lambda h: (0, 0))],
        out_specs=pl.BlockSpec((1, Q_TILE, WIN), lambda h: (h, 0, 0)),
        out_shape=jax.ShapeDtypeStruct((H, Q_TILE, WIN), F32), args=(rel,), rider=rider)


def _bias_reduce(dbias, name):
    H = dbias.shape[0]

    def body(d_ref, o_ref, e_ref):
        oh = _rel_onehot()
        for hd in range(H):
            x = jnp.concatenate([d_ref[hd], jnp.zeros((Q_TILE, SKEW - WIN), F32)], axis=1)
            e_ref[hd:hd + 1, :] = jnp.sum(_skew(x, -1), axis=0, keepdims=True)
        hi, mid, lo = _split3(e_ref[...])
        o_ref[...] = _dot_nt(hi, oh) + _dot_nt(mid, oh) + _dot_nt(lo, oh)

    return pl.pallas_call(
        body, name=name, out_shape=jax.ShapeDtypeStruct((H, REL_PAD), F32),
        in_specs=[pl.BlockSpec(memory_space=pltpu.VMEM)], out_specs=pl.BlockSpec(memory_space=pltpu.VMEM),
        scratch_shapes=[pltpu.VMEM((H, SKEW), F32)],
        compiler_params=_params(),
    )(dbias)


def _pair_stack(xp, even):
    z = jnp.zeros_like(xp)
    return jnp.concatenate([jnp.where(even, xp, z), jnp.where(even, z, xp)], axis=0)


def _pair_merge(y, even):
    return jnp.where(even, y[:Q_TILE], y[Q_TILE:])


def _strip_probs(s_ref, b_ref, pp, r, valid, base=0):
    hb, hr = divmod(r, Q_TILE)
    s = s_ref[base + pp, r:r + STRIP, :] + b_ref[2 * pp + hb, hr:hr + STRIP, :]
    s = jnp.where(valid, s, NEG_INF)
    e = jnp.exp(s - jnp.max(s, axis=-1, keepdims=True))
    return e * (1.0 / jnp.sum(e, axis=-1, keepdims=True))


def _fill_padded(dst_ref, src_ref):
    dst_ref[0:PAD, :] = jnp.zeros((PAD, dst_ref.shape[1]), dst_ref.dtype)
    dst_ref[PAD:, :] = src_ref[...]


def _attn_specs(B, S, D, lanes):
    nt = S // (TILES_PER_STEP * Q_TILE)
    q_spec = pl.BlockSpec((TILES_PER_STEP * Q_TILE, lanes), lambda g, b, i: (b * nt + i, g))
    k_spec = pl.BlockSpec((S, lanes), lambda g, b, i: (b, g))
    v_spec = pl.BlockSpec((S, lanes), lambda g, b, i: (b, D // lanes + g))
    bias_spec = pl.BlockSpec((lanes // HEAD_DIM, Q_TILE, WIN), lambda g, b, i: (g, 0, 0))
    return nt, q_spec, k_spec, v_spec, bias_spec


def _attn_forward(q, kv, bias, S, *, name, rider=None):
    T, D = q.shape
    B = T // S
    lanes = min(FWD_HEADS_PER_STEP * HEAD_DIM, D)
    nt, q_spec, k_spec, v_spec, bias_spec = _attn_specs(B, S, D, lanes)

    npairs = lanes // (2 * HEAD_DIM)

    def body(q_ref, k_ref, v_ref, b_ref, o_ref, kp_ref, vp_ref, s_ref, p_ref):
        i = pl.program_id(2)

        @pl.when(i == 0)
        def _():
            _fill_padded(kp_ref, k_ref)
            _fill_padded(vp_ref, v_ref)

        even = lax.broadcasted_iota(jnp.int32, (1, 2 * HEAD_DIM), 1) < HEAD_DIM
        pair_cols = [slice(pp * 2 * HEAD_DIM, (pp + 1) * 2 * HEAD_DIM) for pp in range(npairs)]
        for t in range(TILES_PER_STEP):
            tile = i * TILES_PER_STEP + t
            start = pl.multiple_of(tile * Q_TILE, Q_TILE)
            rows = slice(t * Q_TILE, (t + 1) * Q_TILE)
            valid = lax.broadcasted_iota(jnp.int32, (STRIP, WIN), 1) >= PAD - tile * Q_TILE
            for pp, cols in enumerate(pair_cols):
                s_ref[t * npairs + pp] = _dot_nt(_pair_stack(q_ref[rows, cols], even), kp_ref[pl.ds(start, WIN), cols])
            for pp in range(npairs):
                for r in range(0, 2 * Q_TILE, STRIP):
                    p = _strip_probs(s_ref, b_ref, pp, r, valid, base=t * npairs)
                    p_ref[t * npairs + pp, r:r + STRIP, :] = p.astype(BF16)
            for pp, cols in enumerate(pair_cols):
                o = _dot(p_ref[t * npairs + pp], vp_ref[pl.ds(start, WIN), cols])
                o_ref[rows, cols] = _pair_merge(o, even).astype(BF16)

    nbuf = TILES_PER_STEP * npairs
    return _call(
        body, name=name, grid=(D // lanes, B, nt),
        in_specs=[q_spec, k_spec, v_spec, bias_spec], out_specs=q_spec,
        out_shape=jax.ShapeDtypeStruct((T, D), BF16),
        scratch_shapes=[pltpu.VMEM((S + PAD, lanes), BF16), pltpu.VMEM((S + PAD, lanes), BF16),
                        pltpu.VMEM((nbuf, 2 * Q_TILE, WIN), F32), pltpu.VMEM((nbuf, 2 * Q_TILE, WIN), BF16)],
        args=(q, kv, kv, bias), rider=rider)


def _attn_backward(q, kv, bias, do, S, *, name, rider=None):
    T, D = q.shape
    B = T // S
    H = D // HEAD_DIM
    lanes = min(BWD_HEADS_PER_STEP * HEAD_DIM, D)
    nt, q_spec, k_spec, v_spec, bias_spec = _attn_specs(B, S, D, lanes)
    scale = HEAD_DIM ** -0.5

    npairs = lanes // (2 * HEAD_DIM)

    def body(q_ref, k_ref, v_ref, b_ref, do_ref, dq_ref, dk_ref, dv_ref, db_ref, kp_ref, vp_ref, dka_ref, dva_ref,
             s_ref, dp_ref, p_ref, ds_ref):
        b, i = pl.program_id(1), pl.program_id(2)

        @pl.when((b == 0) & (i == 0))
        def _():
            db_ref[...] = jnp.zeros_like(db_ref)

        @pl.when(i == 0)
        def _():
            _fill_padded(kp_ref, k_ref)
            _fill_padded(vp_ref, v_ref)
            dka_ref[...] = jnp.zeros_like(dka_ref)
            dva_ref[...] = jnp.zeros_like(dva_ref)

        even = lax.broadcasted_iota(jnp.int32, (1, 2 * HEAD_DIM), 1) < HEAD_DIM
        pair_cols = [slice(pp * 2 * HEAD_DIM, (pp + 1) * 2 * HEAD_DIM) for pp in range(npairs)]
        for t in range(TILES_PER_STEP):
            tile = i * TILES_PER_STEP + t
            start = pl.multiple_of(tile * Q_TILE, Q_TILE)
            rows = slice(t * Q_TILE, (t + 1) * Q_TILE)
            valid = lax.broadcasted_iota(jnp.int32, (STRIP, WIN), 1) >= PAD - tile * Q_TILE
            base = t * npairs
            for pp, cols in enumerate(pair_cols):
                s_ref[base + pp] = _dot_nt(_pair_stack(q_ref[rows, cols], even), kp_ref[pl.ds(start, WIN), cols])
                dp_ref[base + pp] = _dot_nt(_pair_stack(do_ref[rows, cols], even), vp_ref[pl.ds(start, WIN), cols])
            for pp in range(npairs):
                for r in range(0, 2 * Q_TILE, STRIP):
                    hb, hr = divmod(r, Q_TILE)
                    p = _strip_probs(s_ref, b_ref, pp, r, valid, base=base)
                    dp = dp_ref[base + pp, r:r + STRIP, :]
                    ds = p * (dp - jnp.sum(p * dp, axis=-1, keepdims=True))
                    db_ref[2 * pp + hb, hr:hr + STRIP, :] += ds
                    p_ref[base + pp, r:r + STRIP, :] = p.astype(BF16)
                    ds_ref[base + pp, r:r + STRIP, :] = ds.astype(BF16)
            for pp, cols in enumerate(pair_cols):
                dsb = ds_ref[base + pp]
                dq = _pair_merge(_dot(dsb, kp_ref[pl.ds(start, WIN), cols]), even) * scale
                dq_ref[rows, cols] = dq.astype(BF16)
                dka_ref[pl.ds(start, WIN), cols] += _dot_tn(dsb, _pair_stack(q_ref[rows, cols], even))
                dva_ref[pl.ds(start, WIN), cols] += _dot_tn(p_ref[base + pp], _pair_stack(do_ref[rows, cols], even))

        @pl.when(i == nt - 1)
        def _():
            dk_ref[...] = dka_ref[PAD:, :].astype(BF16)
            dv_ref[...] = dva_ref[PAD:, :].astype(BF16)

    dkv_shape = jax.ShapeDtypeStruct((T, D), BF16)
    nbuf = TILES_PER_STEP * npairs
    return _call(
        body, name=name, grid=(D // lanes, B, nt),
        in_specs=[q_spec, k_spec, v_spec, bias_spec, q_spec],
        out_specs=[q_spec, k_spec, k_spec, bias_spec],
        out_shape=[jax.ShapeDtypeStruct((T, D), BF16), dkv_shape, dkv_shape,
                   jax.ShapeDtypeStruct((H, Q_TILE, WIN), F32)],
        scratch_shapes=[pltpu.VMEM((S + PAD, lanes), BF16), pltpu.VMEM((S + PAD, lanes), BF16),
                        pltpu.VMEM((S + PAD, lanes), F32), pltpu.VMEM((S + PAD, lanes), F32),
                        pltpu.VMEM((nbuf, 2 * Q_TILE, WIN), F32), pltpu.VMEM((nbuf, 2 * Q_TILE, WIN), F32),
                        pltpu.VMEM((nbuf, 2 * Q_TILE, WIN), BF16), pltpu.VMEM((nbuf, 2 * Q_TILE, WIN), BF16)],
        args=(q, kv, kv, bias, do), rider=rider)


def _loss_head(h, g, target, name):
    T, D = h.shape
    tm = _tile(T, MATMUL_TILE)

    def body(h_ref, g_ref, t_ref, dh_ref, st_ref):
        @pl.when(pl.program_id(0) == 0)
        def _():
            st_ref[...] = jnp.zeros_like(st_ref)

        xh, r = _rms(h_ref[...], None)
        err = xh * g_ref[...] - t_ref[...]
        st_ref[1:2, :] += 0.5 * jnp.sum(jnp.mean(err * err, axis=-1, keepdims=True), axis=0, keepdims=True)
        dy = err * (1.0 / D)
        st_ref[0:1, :] += jnp.sum(dy * xh, axis=0, keepdims=True)
        dn = dy * g_ref[...]
        dh_ref[...] = r * (dn - xh * jnp.mean(dn * xh, axis=-1, keepdims=True))

    row = pl.BlockSpec((tm, D), lambda i: (i, 0))
    return pl.pallas_call(
        body, name=name, grid=(T // tm,),
        in_specs=[row, pl.BlockSpec((1, D), lambda i: (0, 0)), row],
        out_specs=[row, pl.BlockSpec((8, D), lambda i: (0, 0))],
        out_shape=[jax.ShapeDtypeStruct((T, D), F32), jax.ShapeDtypeStruct((8, D), F32)],
        compiler_params=_params(("arbitrary",)),
    )(h, g, target)


def _sum_devices(arrs, name):
    n = len(arrs)

    def body(*refs):
        for a in range(n):
            s = refs[a][0].astype(F32)
            for k in range(1, N_DEV):
                s = s + refs[a][k].astype(F32)
            refs[n + a][...] = s

    vm = pl.BlockSpec(memory_space=pltpu.VMEM)
    return pl.pallas_call(
        body, name=name, out_shape=[jax.ShapeDtypeStruct(a.shape[1:], F32) for a in arrs],
        in_specs=[vm] * n, out_specs=[vm] * n, compiler_params=_params(),
    )(*arrs)


def _adamw_math(w, g, m, v):
    m = ADAM_B1 * m + (1.0 - ADAM_B1) * g
    v = ADAM_B2 * v + (1.0 - ADAM_B2) * (g * g)
    m_hat = m / (1.0 - ADAM_B1 ** ADAM_STEP)
    v_hat = v / (1.0 - ADAM_B2 ** ADAM_STEP)
    delta = -ADAM_LR * (m_hat / (jnp.sqrt(v_hat) + ADAM_EPS) + ADAM_WD * w)
    return delta, m, v


def _adamw_small(items, name):
    n = len(items)

    def body(*refs):
        for a in range(n):
            w, m, v, g = (refs[4 * a + k][...] for k in range(4))
            d, m, v = _adamw_math(w, g, m, v)
            refs[4 * n + 3 * a][...] = d
            refs[4 * n + 3 * a + 1][...] = m
            refs[4 * n + 3 * a + 2][...] = v

    vm = pl.BlockSpec(memory_space=pltpu.VMEM)
    flat = [t for it in items for t in it]
    outs = pl.pallas_call(
        body, name=name,
        out_shape=[jax.ShapeDtypeStruct(it[0].shape, F32) for it in items for _ in range(3)],
        in_specs=[vm] * (4 * n), out_specs=[vm] * (3 * n), compiler_params=_params(),
    )(*flat)
    return [tuple(outs[3 * a:3 * a + 3]) for a in range(n)]


def _adamw_big(w, m, v, owns, landeds, name, rider=None):
    L, R, C = w.shape
    tr = _tile(R, 512)
    nr = R // tr
    counts = [len(ls) for ls in landeds]

    def body(*refs):
        w_ref, m_ref, v_ref = refs[:3]
        g_ref, d_ref, mo_ref, vo_ref = refs[-4:]
        layer = pl.program_id(0)
        at = 3
        for j in range(L):
            own_ref, l_refs = refs[at], refs[at + 1:at + 1 + counts[j]]
            at += 1 + counts[j]

            @pl.when(layer == j)
            def _(own_ref=own_ref, l_refs=l_refs):
                g = own_ref[...]
                for l_ref in l_refs:
                    for k in range(l_ref.shape[0]):
                        g = g + l_ref[k].astype(F32)
                d, mn, vn = _adamw_math(w_ref[0], g, m_ref[0], v_ref[0])
                g_ref[0] = g
                d_ref[0] = d
                mo_ref[0] = mn
                vo_ref[0] = vn

    def pinned(j):
        return lambda l, i: jnp.where(l == j, i, jnp.where(l < j, 0, nr - 1))

    row = pl.BlockSpec((1, tr, C), lambda l, i: (l, i, 0))
    in_specs, args = [row, row, row], [w, m, v]
    for j in range(L):
        in_specs.append(pl.BlockSpec((tr, C), lambda l, i, p=pinned(j): (p(l, i), 0)))
        args.append(owns[j])
        for arr in landeds[j]:
            in_specs.append(pl.BlockSpec((arr.shape[0], tr, C), lambda l, i, p=pinned(j): (0, p(l, i), 0)))
            args.append(arr)
    return _call(body, name=name, grid=(L, nr), in_specs=in_specs, out_specs=[row] * 4,
                 out_shape=[jax.ShapeDtypeStruct((L, R, C), F32)] * 4, args=args, rider=rider)


def _adamw_transposed(wt, mt, vt, owns, landeds, name, rider=None):
    L, C, R = wt.shape
    tc = _tile(R, 256)
    nr = R // tc
    counts = [len(ls) for ls in landeds]

    def body(*refs):
        w_ref, m_ref, v_ref = refs[:3]
        g_ref, d_ref, mo_ref, vo_ref = refs[-4:]
        layer = pl.program_id(0)
        eye = (lax.broadcasted_iota(jnp.int32, (tc, tc), 0) == lax.broadcasted_iota(jnp.int32, (tc, tc), 1)).astype(BF16)
        at = 3
        for j in range(L):
            own_ref, l_refs = refs[at], refs[at + 1:at + 1 + counts[j]]
            at += 1 + counts[j]

            @pl.when(layer == j)
            def _(own_ref=own_ref, l_refs=l_refs):
                hi, mid, lo = _split3(own_ref[...])
                g = (_dot_tn(hi, eye) + _dot_tn(mid, eye)) + _dot_tn(lo, eye)
                for l_ref in l_refs:
                    for k in range(l_ref.shape[0]):
                        g = g + _dot_tn(l_ref[k], eye)
                d, mn, vn = _adamw_math(w_ref[0], g, m_ref[0], v_ref[0])
                g_ref[0] = g
                d_ref[0] = d
                mo_ref[0] = mn
                vo_ref[0] = vn

    def pinned(j):
        return lambda l, i: jnp.where(l == j, i, jnp.where(l < j, 0, nr - 1))

    col = pl.BlockSpec((1, C, tc), lambda l, i: (l, 0, i))
    in_specs, args = [col, col, col], [wt, mt, vt]
    for j in range(L):
        in_specs.append(pl.BlockSpec((tc, C), lambda l, i, p=pinned(j): (p(l, i), 0)))
        args.append(owns[j])
        for arr in landeds[j]:
            in_specs.append(pl.BlockSpec((arr.shape[0], tc, C), lambda l, i, p=pinned(j): (0, p(l, i), 0)))
            args.append(arr)
    return _call(body, name=name, grid=(L, nr), in_specs=in_specs, out_specs=[col] * 4,
                 out_shape=[jax.ShapeDtypeStruct((L, C, R), F32)] * 4, args=args, rider=rider)


def kernel(x, a_norm_g, a_w_in, a_v_norm_g, a_w_s, a_b_s, a_w_out, kv_norm_g, w_kv, b_norm_g, b_w_q, b_rel_bias, b_w_o, f_norm_g, f_w_in, f_conv_w, f_conv_b, f_w_down, final_norm_g, loss_target, m_a_norm_g, m_a_w_in, m_a_v_norm_g, m_a_w_s, m_a_b_s, m_a_w_out, m_kv_norm_g, m_w_kv, m_b_norm_g, m_b_w_q, m_b_rel_bias, m_b_w_o, m_f_norm_g, m_f_w_in, m_f_conv_w, m_f_conv_b, m_f_w_down, m_final_norm_g, v_a_norm_g, v_a_w_in, v_a_v_norm_g, v_a_w_s, v_a_b_s, v_a_w_out, v_kv_norm_g, v_w_kv, v_b_norm_g, v_b_w_q, v_b_rel_bias, v_b_w_o, v_f_norm_g, v_f_w_in, v_f_conv_w, v_f_conv_b, v_f_w_down, v_final_norm_g):
    B, S, D = x.shape
    T = B * S
    G = a_w_s.shape[1]
    H = D // HEAD_DIM
    F = f_w_in.shape[2]
    L = f_w_in.shape[0]
    dn = D // N_DEV
    xi, yi, ci = lax.axis_index("x"), lax.axis_index("y"), lax.axis_index("c")
    me = 4 * xi + 2 * yi + ci
    pos = jnp.stack([ci, 2 * xi + yi]).astype(jnp.int32)

    cast = lambda t: t.astype(BF16)
    gather = lambda *ts: _gather_rider(list(ts))
    rel = jnp.pad(b_rel_bias[0], ((0, 0), (0, REL_PAD - b_rel_bias.shape[2])))
    bias, (wa_in, norms_sh, conv_w0, conv_w1) = _bias_build(rel, "bias_build", rider=gather(
        cast(a_w_in[0]), jnp.concatenate([a_norm_g, a_v_norm_g], axis=0), f_conv_w[0], f_conv_w[1]))
    ga = jnp.transpose(norms_sh, (1, 0, 2)).reshape(2, D)
    g_a, g_av = ga[0:1], ga[1:2]

    x2 = x.reshape(T, D)
    tgt = loss_target.reshape(T, D)
    pc = jnp.arange(GMLP_BLOCK) // CHUNK
    mask = (pc[:, None] >= pc[None, :]).astype(F32)
    ws = (a_w_s[0] * mask[None]).astype(BF16)
    bst = jnp.transpose(a_b_s[0])
    four = lambda t: t.reshape((4, 2) + t.shape[1:])

    w_in0_sh = cast(f_w_in[0])
    (z, n_a), (wa_out, w_in0_top) = _norm_matmul(x2, g_a, wa_in, flat=True, nbk=4, name="gmlp_in",
                                                 rider=gather(cast(a_w_out[0]), w_in0_sh[:D // 2]))
    wa_out = wa_out.reshape(D, D)
    (gated, h1), (w_in0_bottom, wf_down0) = _gmlp_forward(z, ws, bst, g_av, wa_out, x2, name="gmlp_mix",
                                                          rider=gather(w_in0_sh[D // 2:], cast(f_w_down[0])))
    w_in0 = jnp.concatenate([w_in0_top, w_in0_bottom], axis=1)
    cw0, cb0, wd0 = conv_w0, f_conv_b[0].reshape(8, 1, F), wf_down0.reshape(4, F, D)
    (au0, ag0, cu0, cg0, act0, n_f0, h2), (wkv, wq, w_in1) = _ffn_fused_forward(
        h1, f_norm_g[0:1], w_in0, cw0, cb0, wd0, S, name="ffn0_fwd",
        rider=gather(cast(w_kv), cast(b_w_q[0]), cast(f_w_in[1])))
    wq = wq.reshape(D, D)
    kv, n_kv = _norm_matmul(h2, kv_norm_g.reshape(1, D), wkv, flat=True, nbk=4, name="kv_proj")
    q, n_q = _norm_matmul(h2, b_norm_g, wq.reshape(1, D, D), flat=True, nbk=1, name="q_proj", scale=HEAD_DIM ** -0.5)
    o, (wo, wf_down1) = _attn_forward(q, kv, bias, S, name="attn", rider=gather(cast(b_w_o[0]), cast(f_w_down[1])))
    wo = wo.reshape(D, D)
    cw1, cb1, wd1 = conv_w1, f_conv_b[1].reshape(8, 1, F), wf_down1.reshape(4, F, D)
    h3 = _matmul_residual(o, wo, h2, "attn_out")
    au1, ag1, cu1, cg1, act1, n_f1, h4 = _ffn_fused_forward(h3, f_norm_g[1:2], w_in1, cw1, cb1, wd1, S, name="ffn1_fwd")

    sums, from_chips = {}, {}

    def sibling_sums(names, parts, landed):
        for nm, p, l in zip(names, parts, landed):
            sums[nm] = _sibling_sum(p, l, pos, "grad_sibling_sum_" + nm)

    def chip_rider(*names):
        return _chip_rider([sums[nm][1] for nm in names])

    dh4, st_final = _loss_head(h4, final_norm_g.reshape(1, D), tgt, "loss_head")
    g_wd1 = _wgrad_rows(act1, dh4, flat=False, tk=F, name="ffn1_dwdown")
    parts = [four(g_wd1.reshape(8, F // 2, D))]
    (dau1, dag1, st_conv1, dh3, st_f1), landed = _ffn_fused_backward(
        dh4, cu1, cg1, au1, ag1, cw1, wd1, w_in1, h3, f_norm_g[1:2], S, name="ffn1_bwd", rider=_sibling_rider(parts))
    sibling_sums(["wd1"], parts, landed)
    g_win1, (from_chips["wd1"],) = _wgrad_cols(n_f1, (dau1, dag1), flat=False, nb=8, nbk=2, name="ffn1_dwin",
                                               rider=chip_rider("wd1"))
    d_o = _matmul_nt(dh3, wo.reshape(1, D, D), flat=True, nbk=1, name="attn_out_dx")
    parts = [four(g_win1)]
    g_wo, landed = _wgrad_rows(o, dh3, flat=True, tk=_tile(D, 512), name="attn_out_dw", rider=_sibling_rider(parts))
    sibling_sums(["win1"], parts, landed)
    (dq, dk, dv, dbias), (from_chips["win1"],) = _attn_backward(
        q, kv, bias, d_o, S, name="attn_bwd", rider=chip_rider("win1"))
    g_rel = _bias_reduce(dbias, "bias_reduce")
    g_wq = _wgrad_cols(n_q, dq, flat=True, nb=1, nbk=1, name="q_dw")
    dh2, st_b = _matmul_nt(dq, wq.reshape(1, D, D), flat=True, nbk=1, name="q_dx", norm=(h2, b_norm_g, dh3))
    dkv = jnp.concatenate([dk, dv], axis=-1)
    g_wkv = _wgrad_cols(n_kv, dkv, flat=True, nb=8, nbk=4, name="kv_dw")
    parts = [four(g_wo.reshape(8, dn, D)), four(g_wq.reshape(8, dn, D)), four(g_wkv)]
    (dh2, st_kv), landed = _matmul_nt(dkv, wkv, flat=True, nbk=4, name="kv_dx",
                                      norm=(h2, kv_norm_g.reshape(1, D), dh2), rider=_sibling_rider(parts))
    sibling_sums(["wo", "wq", "wkv"], parts, landed)
    g_wd0, (from_chips["wo"], from_chips["wq"], from_chips["wkv"]) = _wgrad_rows(
        act0, dh2, flat=False, tk=F, name="ffn0_dwdown", rider=chip_rider("wo", "wq", "wkv"))
    parts = [four(g_wd0.reshape(8, F // 2, D))]
    (dau0, dag0, st_conv0, dh1, st_f0), landed = _ffn_fused_backward(
        dh2, cu0, cg0, au0, ag0, cw0, wd0, w_in0, h1, f_norm_g[0:1], S, name="ffn0_bwd", rider=_sibling_rider(parts))
    sibling_sums(["wd0"], parts, landed)
    g_win0, (ce,) = _wgrad_cols(n_f0, (dau0, dag0), flat=False, nb=8, nbk=2, name="ffn0_dwin",
                                rider=chip_rider("wd0"))
    from_chips["wd0"] = [ce]
    dgated = _matmul_nt(dh1, wa_out.reshape(1, D, D), flat=True, nbk=1, name="gmlp_out_dx")
    parts = [four(g_win0)]
    g_wa_out, landed = _wgrad_rows(gated, dh1, flat=True, tk=_tile(D, 512), name="gmlp_out_dw",
                                   rider=_sibling_rider(parts))
    sibling_sums(["win0"], parts, landed)
    parts = [four(g_wa_out.reshape(8, dn, D))]
    (dz, g_ws, g_bst, st_av), (ce_win0_a, landed) = _gmlp_backward(
        z, dgated, ws, bst, g_av, mask, name="gmlp_bwd",
        rider=_join_riders([_chip_rider([sums["win0"][1]], ks=(1, 2)), _sibling_rider(parts)]))
    sibling_sums(["wa_out"], parts, [landed])
    g_wa_in, (ce_win0_b,) = _wgrad_cols(n_a, dz, flat=True, nb=8, nbk=4, name="gmlp_in_dw",
                                        rider=_chip_rider([sums["win0"][1]], ks=(3,)))
    from_chips["win0"] = [ce_win0_a, ce_win0_b]
    vec = jnp.concatenate([st_av[0:1], st_kv[0:1], st_b[0:1], st_f0[0:1], st_f1[0:1], st_final[0:3]], axis=0)
    parts = [four(g_wa_in)]
    (grad_x, st_a), got = _matmul_nt(dz, wa_in, flat=True, nbk=4, name="gmlp_in_dx", norm=(x2, g_a, dh1),
                                     rider=_join_riders([_sibling_rider(parts), chip_rider("wa_out"), gather(
                                         vec, cast(g_ws), cast(g_bst), cast(g_rel), cast(st_conv0), cast(st_conv1))]))
    sibling_sums(["wa_in"], parts, got[0:1])
    from_chips["wa_out"], small = [got[1]], got[2:]

    def big_update(names, w, m, v, rider=None):
        shape = w.shape
        r = lambda t: t.reshape((len(names), -1, shape[-1]))
        as_list = lambda t: t if isinstance(t, list) else [t]
        outs = _adamw_big(r(w), r(m), r(v), [sums[nm][0] for nm in names],
                          [as_list(from_chips[nm]) for nm in names], "adamw_" + names[0], rider=rider)
        outs, got = (outs, None) if rider is None else outs
        return [t.reshape(shape) for t in outs], got

    u_f_w_down, (ce, st_a) = big_update(["wd0", "wd1"], f_w_down, m_f_w_down, v_f_w_down,
                                        rider=_join_riders([chip_rider("wa_in"), gather(st_a)]))
    from_chips["wa_in"] = [ce]
    names = ["win0", "win1"]
    tr = lambda t: jnp.swapaxes(t, 1, 2)
    outs = _adamw_transposed(
        tr(f_w_in), tr(m_f_w_in), tr(v_f_w_in), [sums[nm][0] for nm in names],
        [t if isinstance(t, list) else [t] for t in (from_chips[nm] for nm in names)], "adamw_win0")
    u_f_w_in = [tr(t) for t in outs]
    u_a_w_in, _ = big_update(["wa_in"], a_w_in, m_a_w_in, v_a_w_in)
    u_w_kv, _ = big_update(["wkv"], w_kv, m_w_kv, v_w_kv)
    u_a_w_out, _ = big_update(["wa_out"], a_w_out, m_a_w_out, v_a_w_out)
    u_b_w_q, _ = big_update(["wq"], b_w_q, m_b_w_q, v_b_w_q)
    u_b_w_o, _ = big_update(["wo"], b_w_o, m_b_w_o, v_b_w_o)

    vec, g_ws, g_bst, g_rel, st_conv0, st_conv1, st_a = _sum_devices(list(small) + [st_a], "sum_small_grads")
    vec = jnp.concatenate([st_a[0:1], vec[0:7]], axis=0)
    loss = vec[7, 0]
    g_a_norm = lax.dynamic_slice_in_dim(vec[0:1], me * dn, dn, axis=1)
    g_av_norm = lax.dynamic_slice_in_dim(vec[1:2], me * dn, dn, axis=1)
    st_conv = jnp.stack([st_conv0, st_conv1])
    g_conv_w = lax.dynamic_index_in_dim(st_conv, me, axis=1, keepdims=False)[:, 0:3]
    g_conv_b = st_conv[:, :, 3, :].reshape(L, 8 * F)
    small_items = [
        (a_norm_g, m_a_norm_g, v_a_norm_g, g_a_norm),
        (a_v_norm_g, m_a_v_norm_g, v_a_v_norm_g, g_av_norm),
        (a_w_s, m_a_w_s, v_a_w_s, g_ws[None]),
        (a_b_s, m_a_b_s, v_a_b_s, jnp.transpose(g_bst)[None]),
        (kv_norm_g.reshape(1, D), m_kv_norm_g.reshape(1, D), v_kv_norm_g.reshape(1, D), vec[2:3]),
        (b_norm_g, m_b_norm_g, v_b_norm_g, vec[3:4]),
        (b_rel_bias, m_b_rel_bias, v_b_rel_bias, g_rel[None, :, :b_rel_bias.shape[2]]),
        (f_norm_g, m_f_norm_g, v_f_norm_g, vec[4:6]),
        (f_conv_w, m_f_conv_w, v_f_conv_w, g_conv_w),
        (f_conv_b, m_f_conv_b, v_f_conv_b, g_conv_b),
        (final_norm_g.reshape(1, D), m_final_norm_g.reshape(1, D), v_final_norm_g.reshape(1, D), vec[6:7]),
    ]
    small_out = _adamw_small(small_items, "adamw_small")
    (u_a_norm, u_av_norm, u_ws, u_bs, u_kvn, u_bn, u_rel, u_fn, u_cw, u_cb, u_fin) = [
        (it[3],) + so for it, so in zip(small_items, small_out)]
    vecD = lambda u: tuple(t.reshape(D) for t in u)
    u_kvn, u_fin = vecD(u_kvn), vecD(u_fin)

    order = [u_a_norm, u_a_w_in, u_av_norm, u_ws, u_bs, u_a_w_out, u_kvn, u_w_kv, u_bn, u_b_w_q, u_rel, u_b_w_o,
             u_fn, u_f_w_in, u_cw, u_cb, u_f_w_down, u_fin]
    outs = [loss, grad_x.reshape(B, S, D)]
    for k in range(4):
        outs += [u[k] for u in order]
    return tuple(outs)
```

```python
import functools

import jax
import jax.numpy as jnp
from jax import lax
from jax.experimental import pallas as pl
from jax.experimental.pallas import tpu as pltpu

F32 = jnp.float32
BF16 = jnp.bfloat16
MESH = pl.DeviceIdType.MESH

N_DEV = 8
EPS = 1e-6
NEG_INF = -1e30
CHUNK = 64
LEFT_CHUNKS = 8
REL_CLIP = 128
HEAD_DIM = 64
GMLP_BLOCK = 128
Q_TILE = 2 * CHUNK
PAD = LEFT_CHUNKS * CHUNK
WIN = PAD + Q_TILE
SKEW = WIN + Q_TILE
REL_PAD = 384
FWD_HEADS_PER_STEP = 8
BWD_HEADS_PER_STEP = 4
TILES_PER_STEP = 4
STRIP = 32
ROWS = 32
ADAM_LR, ADAM_B1, ADAM_B2, ADAM_EPS, ADAM_WD, ADAM_STEP = 0.001, 0.9, 0.999, 1e-08, 0.01, 10
VMEM_LIMIT = 60 * 1024 * 1024
TOKEN_TILE = 512
MATMUL_TILE = 1024
FFN_BLOCKS_PER_STEP = 4


def _params(sem=None):
    return pltpu.CompilerParams(dimension_semantics=sem, vmem_limit_bytes=VMEM_LIMIT)


def _tile(n, pref):
    if n <= pref:
        return n
    for t in range(pref - pref % 8, 7, -8):
        if n % t == 0:
            return t
    return n


def _gelu(x):
    return 0.5 * x * (1.0 + jnp.tanh(0.7978845608028654 * (x + 0.044715 * x * x * x)))


def _gelu_grad(x):
    t = jnp.tanh(0.7978845608028654 * (x + 0.044715 * x * x * x))
    return 0.5 * (1.0 + t) + 0.5 * x * (1.0 - t * t) * 0.7978845608028654 * (1.0 + 3 * 0.044715 * x * x)


def _sigmoid(x):
    return 1.0 / (1.0 + jnp.exp(-x))


def _dot(a, b):
    return jnp.dot(a, b, preferred_element_type=F32)


def _dot_nt(a, b):
    return lax.dot_general(a, b, (((1,), (1,)), ((), ())), preferred_element_type=F32)


def _dot_tn(a, b):
    return lax.dot_general(a, b, (((0,), (0,)), ((), ())), preferred_element_type=F32)


def _split3(x):
    hi = x.astype(BF16)
    r1 = x - hi.astype(F32)
    mid = r1.astype(BF16)
    lo = (r1 - mid.astype(F32)).astype(BF16)
    return hi, mid, lo


def _mesh_pos():
    return lax.axis_index("x"), lax.axis_index("y"), lax.axis_index("c")


class _Rider:
    def __init__(self, arrs, out_shapes, sems, start, finish):
        self.arrs, self.out_shapes, self.sems, self.start, self.finish = arrs, out_shapes, sems, start, finish


def _gather_rider(arrs):
    n = len(arrs)

    def tools(ins, outs, sems):
        send_sems, recv_sems, local_sems = sems
        x, y, c = _mesh_pos()
        me, sibling = (x, y, c), (x, y, 1 - c)
        chips = [(1 - x, y), (x, 1 - y), (1 - x, 1 - y)]

        def slot(a, block):
            px, py, pc = block
            return outs[a].at[4 * px + 2 * py + pc]

        def copy(a, k, block, to, src=None):
            dst = slot(a, block)
            return pltpu.make_async_remote_copy(
                src_ref=dst if src is None else src, dst_ref=dst,
                send_sem=send_sems.at[a, k], recv_sem=recv_sems.at[a, k], device_id=to, device_id_type=MESH)

        def first(a):
            cps = [copy(a, 0, me, sibling, src=ins[a])]
            return cps + [copy(a, 1 + j, me, (*chip, c), src=ins[a]) for j, chip in enumerate(chips)]

        def mine(a):
            return pltpu.make_async_copy(ins[a], slot(a, me), local_sems.at[a])

        return me, sibling, chips, c, copy, first, mine

    def start(ins, outs, sems):
        _, _, _, _, _, first, mine = tools(ins, outs, sems)
        for a in range(n):
            mine(a).start()
            for cp in first(a):
                cp.start()

    def finish(ins, outs, sems):
        me, sibling, chips, c, copy, first, mine = tools(ins, outs, sems)
        passed = []
        for j, chip in enumerate(chips):
            for a in range(n):
                copy(a, 1 + j, (*chip, c), me).wait_recv()
                fwd = copy(a, 4 + j, (*chip, c), sibling)
                fwd.start()
                passed.append(fwd)
        for a in range(n):
            copy(a, 0, sibling, me).wait_recv()
            for j, chip in enumerate(chips):
                copy(a, 4 + j, (*chip, 1 - c), me).wait_recv()
        for a in range(n):
            for cp in first(a):
                cp.wait_send()
        for cp in passed:
            cp.wait_send()
        for a in range(n):
            mine(a).wait()

    return _Rider(list(arrs), [jax.ShapeDtypeStruct((N_DEV,) + a.shape, a.dtype) for a in arrs],
                  [pltpu.SemaphoreType.DMA((n, 7)), pltpu.SemaphoreType.DMA((n, 7)), pltpu.SemaphoreType.DMA((n,))],
                  start, finish)


def _sibling_rider(arrs):
    n = len(arrs)

    def copies(ins, outs, sems):
        x, y, c = _mesh_pos()
        return [pltpu.make_async_remote_copy(
            src_ref=ins[a].at[:, pl.ds(1 - c, 1)], dst_ref=outs[a],
            send_sem=sems[0].at[a], recv_sem=sems[1].at[a], device_id=(x, y, 1 - c), device_id_type=MESH)
            for a in range(n)]

    def start(ins, outs, sems):
        for cp in copies(ins, outs, sems):
            cp.start()

    def finish(ins, outs, sems):
        for cp in copies(ins, outs, sems):
            cp.wait()

    return _Rider(list(arrs), [jax.ShapeDtypeStruct((4, 1) + a.shape[2:], a.dtype) for a in arrs],
                  [pltpu.SemaphoreType.DMA((n,)), pltpu.SemaphoreType.DMA((n,))], start, finish)


def _chip_rider(arrs, ks=(1, 2, 3)):
    n = len(arrs)

    def copies(ins, outs, sems):
        x, y, c = _mesh_pos()
        cps = []
        for a in range(n):
            for s, k in enumerate(ks):
                px = x if k < 2 else 1 - x
                py = y if k == 2 else 1 - y
                cps.append(pltpu.make_async_remote_copy(
                    src_ref=ins[a].at[2 * px + py], dst_ref=outs[a].at[s],
                    send_sem=sems[0].at[a, s], recv_sem=sems[1].at[a, s],
                    device_id=(px, py, c), device_id_type=MESH))
        return cps

    def start(ins, outs, sems):
        for cp in copies(ins, outs, sems):
            cp.start()

    def finish(ins, outs, sems):
        for cp in copies(ins, outs, sems):
            cp.wait()

    return _Rider(list(arrs), [jax.ShapeDtypeStruct((len(ks),) + a.shape[1:], a.dtype) for a in arrs],
                  [pltpu.SemaphoreType.DMA((n, len(ks))), pltpu.SemaphoreType.DMA((n, len(ks)))], start, finish)


def _join_riders(riders):
    def split(seq, counts):
        out, at = [], 0
        for k in counts:
            out.append(seq[at:at + k])
            at += k
        return out

    n_in = [len(r.arrs) for r in riders]
    n_out = [len(r.out_shapes) for r in riders]
    n_sem = [len(r.sems) for r in riders]

    def run(which):
        def fn(ins, outs, sems):
            for r, i, o, s in zip(riders, split(ins, n_in), split(outs, n_out), split(sems, n_sem)):
                getattr(r, which)(i, o, s)
        return fn

    return _Rider([a for r in riders for a in r.arrs], [o for r in riders for o in r.out_shapes],
                  [s for r in riders for s in r.sems], run("start"), run("finish"))


def _run_rider(rider, name):
    n_in, n_out = len(rider.arrs), len(rider.out_shapes)

    def body(*refs):
        ins, outs, sems = refs[:n_in], refs[n_in:n_in + n_out], refs[n_in + n_out:]
        rider.start(ins, outs, sems)
        rider.finish(ins, outs, sems)

    any_spec = pl.BlockSpec(memory_space=pl.ANY)
    return pl.pallas_call(
        body, name=name, out_shape=list(rider.out_shapes), in_specs=[any_spec] * n_in, out_specs=[any_spec] * n_out,
        scratch_shapes=list(rider.sems),
    )(*rider.arrs)


def _call(body, *, name, grid, in_specs, out_specs, out_shape, args, scratch_shapes=(), rider=None):
    params = _params(("arbitrary",) * len(grid))
    if rider is None:
        return pl.pallas_call(body, name=name, grid=grid, in_specs=in_specs, out_specs=out_specs, out_shape=out_shape,
                              scratch_shapes=list(scratch_shapes), compiler_params=params)(*args)
    single = not isinstance(out_shape, (list, tuple))
    outs = [out_shape] if single else list(out_shape)
    ospecs = [out_specs] if single else list(out_specs)
    n_in, n_out, n_scr = len(in_specs), len(outs), len(scratch_shapes)
    r_in, r_out = len(rider.arrs), len(rider.out_shapes)

    def hosted(*refs):
        refs = list(refs)
        ins, rins = refs[:n_in], refs[n_in:n_in + r_in]
        refs = refs[n_in + r_in:]
        houts, routs = refs[:n_out], refs[n_out:n_out + r_out]
        refs = refs[n_out + r_out:]
        scr, rsems = refs[:n_scr], refs[n_scr:]
        ids = [pl.program_id(a) for a in range(len(grid))]
        first = functools.reduce(lambda p, q: p & q, [i == 0 for i in ids])
        last = functools.reduce(lambda p, q: p & q, [i == g - 1 for i, g in zip(ids, grid)])

        @pl.when(first)
        def _():
            rider.start(rins, routs, rsems)

        body(*ins, *houts, *scr)

        @pl.when(last)
        def _():
            rider.finish(rins, routs, rsems)

    any_spec = pl.BlockSpec(memory_space=pl.ANY)
    res = pl.pallas_call(
        hosted, name=name, grid=grid, in_specs=list(in_specs) + [any_spec] * r_in,
        out_specs=ospecs + [any_spec] * r_out, out_shape=outs + list(rider.out_shapes),
        scratch_shapes=list(scratch_shapes) + list(rider.sems), compiler_params=params,
    )(*args, *rider.arrs)
    return (res[0] if single else list(res[:n_out])), list(res[n_out:])


def _sibling_sum(part, landed, pos, name):
    _, _, rows, cols = part.shape
    tr = _tile(rows, 512)

    def body(pos_ref, p_ref, l_ref, own_ref, all_ref):
        s = p_ref[0, 0] + l_ref[0, 0]
        all_ref[0] = s.astype(BF16)

        @pl.when(pl.program_id(1) == pos_ref[1])
        def _():
            own_ref[...] = s

    return pl.pallas_call(
        body, name=name,
        grid_spec=pltpu.PrefetchScalarGridSpec(
            num_scalar_prefetch=1, grid=(rows // tr, 4),
            in_specs=[pl.BlockSpec((1, 1, tr, cols), lambda i, k, pos: (k, pos[0], i, 0)),
                      pl.BlockSpec((1, 1, tr, cols), lambda i, k, pos: (k, 0, i, 0))],
            out_specs=[pl.BlockSpec((tr, cols), lambda i, k, pos: (i, 0)),
                       pl.BlockSpec((1, tr, cols), lambda i, k, pos: (k, i, 0))]),
        out_shape=[jax.ShapeDtypeStruct((rows, cols), F32), jax.ShapeDtypeStruct((4, rows, cols), BF16)],
        compiler_params=_params(("arbitrary", "arbitrary")),
    )(pos, part, landed)


def _rms(x, g):
    r = lax.rsqrt(jnp.mean(x * x, axis=-1, keepdims=True) + EPS)
    return x * r, r


def _norm_matmul(h, g, w, *, flat, nbk, name, scale=1.0, rider=None):
    T, D = h.shape
    nb, _, bn = w.shape
    tm = _tile(T, MATMUL_TILE)

    def body(h_ref, g_ref, w_ref, o_ref, n_ref):
        @pl.when(pl.program_id(1) == 0)
        def _():
            xh, _ = _rms(h_ref[...], None)
            n_ref[...] = (xh * g_ref[...]).astype(BF16)

        n = n_ref[...]
        for k in range(nbk):
            r = _dot(n, w_ref[k])
            r = (r if scale == 1.0 else r * scale).astype(BF16)
            if flat:
                o_ref[:, k * bn:(k + 1) * bn] = r
            else:
                o_ref[k] = r

    if flat:
        out_shape = jax.ShapeDtypeStruct((T, nb * bn), BF16)
        out_spec = pl.BlockSpec((tm, nbk * bn), lambda i, j: (i, j))
    else:
        out_shape = jax.ShapeDtypeStruct((nb, T, bn), BF16)
        out_spec = pl.BlockSpec((nbk, tm, bn), lambda i, j: (j, i, 0))
    return _call(
        body, name=name, grid=(T // tm, nb // nbk),
        in_specs=[pl.BlockSpec((tm, D), lambda i, j: (i, 0)),
                  pl.BlockSpec((1, D), lambda i, j: (0, 0)),
                  pl.BlockSpec((nbk, D, bn), lambda i, j: (j, 0, 0))],
        out_specs=[out_spec, pl.BlockSpec((tm, D), lambda i, j: (i, 0))],
        out_shape=[out_shape, jax.ShapeDtypeStruct((T, D), BF16)],
        args=(h, g, w), rider=rider)


def _matmul_nt(dy, w, *, flat, nbk, name, norm=None, out_dtype=BF16, rider=None):
    nb, R, bn = w.shape
    halves = isinstance(dy, tuple)
    T = (dy[0] if halves else dy).shape[0] if flat else dy.shape[1]
    tm = _tile(T, MATMUL_TILE)
    nj = nb // nbk

    def body(*refs):
        dy_refs, refs = (refs[:2], refs[2:]) if halves else (refs[:1], refs[1:])
        if norm is None:
            w_ref, o_ref, acc_ref = refs
        else:
            w_ref, h_ref, g_ref, dres_ref, o_ref, dg_ref, acc_ref = refs
        i, j = pl.program_id(0), pl.program_id(1)

        @pl.when(j == 0)
        def _():
            acc_ref[...] = jnp.zeros_like(acc_ref)

        def accumulate(dy_ref):
            part = acc_ref[...]
            for k in range(nbk):
                d = dy_ref[:, k * bn:(k + 1) * bn] if flat else dy_ref[k]
                part = part + _dot_nt(d.astype(BF16), w_ref[k])
            acc_ref[...] = part

        if halves:
            pl.when(j < nj // 2)(lambda: accumulate(dy_refs[0]))
            pl.when(j >= nj // 2)(lambda: accumulate(dy_refs[1]))
        else:
            accumulate(dy_refs[0])

        @pl.when(j == nj - 1)
        def _():
            acc = acc_ref[...]
            if norm is None:
                o_ref[...] = acc.astype(out_dtype)
            else:
                xh, r = _rms(h_ref[...], None)

                @pl.when(i == 0)
                def _():
                    dg_ref[...] = jnp.zeros_like(dg_ref)

                dg_ref[0:1, :] += jnp.sum(acc * xh, axis=0, keepdims=True)
                dn = acc * g_ref[...]
                o_ref[...] = dres_ref[...] + r * (dn - xh * jnp.mean(dn * xh, axis=-1, keepdims=True))

    if halves:
        dy_specs = [pl.BlockSpec((tm, nbk * bn), lambda i, j: (i, jnp.minimum(j, nj // 2 - 1))),
                    pl.BlockSpec((tm, nbk * bn), lambda i, j: (i, jnp.maximum(j - nj // 2, 0)))]
        dys = tuple(dy)
    elif flat:
        dy_specs, dys = [pl.BlockSpec((tm, nbk * bn), lambda i, j: (i, j))], (dy,)
    else:
        dy_specs, dys = [pl.BlockSpec((nbk, tm, bn), lambda i, j: (j, i, 0))], (dy,)
    w_spec = pl.BlockSpec((nbk, R, bn), lambda i, j: (j, 0, 0))
    row_spec = pl.BlockSpec((tm, R), lambda i, j: (i, 0))
    if norm is None:
        in_specs, args = dy_specs + [w_spec], dys + (w,)
        out_specs = row_spec
        out_shape = jax.ShapeDtypeStruct((T, R), out_dtype)
    else:
        in_specs = dy_specs + [w_spec, row_spec, pl.BlockSpec((1, R), lambda i, j: (0, 0)), row_spec]
        args = dys + (w,) + tuple(norm)
        out_specs = [row_spec, pl.BlockSpec((8, R), lambda i, j: (0, 0))]
        out_shape = [jax.ShapeDtypeStruct((T, R), F32), jax.ShapeDtypeStruct((8, R), F32)]
    return _call(
        body, name=name, grid=(T // tm, nj), in_specs=in_specs, out_specs=out_specs, out_shape=out_shape,
        scratch_shapes=[pltpu.VMEM((tm, R), F32)], args=args, rider=rider)


def _wgrad_cols(n, dy, *, flat, nb, nbk, name, rider=None):
    T, D = n.shape
    halves = isinstance(dy, tuple)
    bn = (2 * dy[0].shape[1] if halves else dy.shape[1]) // nb if flat else (dy[0] if halves else dy).shape[2]
    tt = _tile(T, MATMUL_TILE)
    nt = T // tt
    nj = nb // nbk

    def body(*refs):
        n_ref, dy_refs, (o_ref, acc_ref) = refs[0], refs[1:-2], refs[-2:]
        j, t = pl.program_id(0), pl.program_id(1)

        @pl.when(t == 0)
        def _():
            acc_ref[...] = jnp.zeros_like(acc_ref)

        def accumulate(dy_ref):
            nv = n_ref[...]
            for k in range(nbk):
                d = dy_ref[:, k * bn:(k + 1) * bn] if flat else dy_ref[k]
                acc_ref[k] += _dot_tn(nv, d)

        if halves:
            pl.when(j < nj // 2)(lambda: accumulate(dy_refs[0]))
            pl.when(j >= nj // 2)(lambda: accumulate(dy_refs[1]))
        else:
            accumulate(dy_refs[0])

        @pl.when(t == nt - 1)
        def _():
            o_ref[...] = acc_ref[...]

    if flat and halves:
        first = pl.BlockSpec((tt, nbk * bn), lambda j, t: (jnp.where(j < nj // 2, t, nt - 1),
                                                           jnp.minimum(j, nj // 2 - 1)))
        second = pl.BlockSpec((tt, nbk * bn), lambda j, t: (jnp.where(j >= nj // 2, t, 0),
                                                            jnp.maximum(j - nj // 2, 0)))
        dy_specs, dys = [first, second], list(dy)
    elif flat:
        dy_specs, dys = [pl.BlockSpec((tt, nbk * bn), lambda j, t: (t, j))], [dy]
    elif halves:
        first = pl.BlockSpec((nbk, tt, bn), lambda j, t: (jnp.minimum(j, nj // 2 - 1),
                                                          jnp.where(j < nj // 2, t, nt - 1), 0))
        second = pl.BlockSpec((nbk, tt, bn), lambda j, t: (jnp.maximum(j - nj // 2, 0),
                                                           jnp.where(j >= nj // 2, t, 0), 0))
        dy_specs, dys = [first, second], list(dy)
    else:
        dy_specs, dys = [pl.BlockSpec((nbk, tt, bn), lambda j, t: (j, t, 0))], [dy]
    return _call(
        body, name=name, grid=(nj, nt),
        in_specs=[pl.BlockSpec((tt, D), lambda j, t: (t, 0))] + dy_specs,
        out_specs=pl.BlockSpec((nbk, D, bn), lambda j, t: (j, 0, 0)),
        out_shape=jax.ShapeDtypeStruct((nb, D, bn), F32),
        scratch_shapes=[pltpu.VMEM((nbk, D, bn), F32)], args=[n] + dys, rider=rider)


def _wgrad_rows(xa, dh, *, flat, tk, name, rider=None):
    T, D = dh.shape
    nk = xa.shape[1] // tk if flat else xa.shape[0]
    tt = _tile(T, MATMUL_TILE)
    nt = T // tt

    def body(x_ref, dh_ref, o_ref, acc_ref):
        t = pl.program_id(1)

        @pl.when(t == 0)
        def _():
            acc_ref[...] = jnp.zeros_like(acc_ref)

        xv = x_ref[...] if flat else x_ref[0]
        acc_ref[...] += _dot_tn(xv, dh_ref[...].astype(BF16))

        @pl.when(t == nt - 1)
        def _():
            o_ref[...] = acc_ref[...]

    x_spec = pl.BlockSpec((tt, tk), lambda j, t: (t, j)) if flat else pl.BlockSpec((1, tt, tk), lambda j, t: (j, t, 0))
    return _call(
        body, name=name, grid=(nk, nt),
        in_specs=[x_spec, pl.BlockSpec((tt, D), lambda j, t: (t, 0))],
        out_specs=pl.BlockSpec((tk, D), lambda j, t: (j, 0)),
        out_shape=jax.ShapeDtypeStruct((nk * tk, D), F32),
        scratch_shapes=[pltpu.VMEM((tk, D), F32)], args=(xa, dh), rider=rider)


def _matmul_residual(xa, w, res, name):
    T, K = xa.shape
    D = w.shape[1]
    tm = _tile(T, MATMUL_TILE)

    def body(x_ref, w_ref, r_ref, o_ref):
        o_ref[...] = r_ref[...] + _dot(x_ref[...], w_ref[...])

    return pl.pallas_call(
        body, name=name, grid=(T // tm,),
        in_specs=[pl.BlockSpec((tm, K), lambda i: (i, 0)), pl.BlockSpec((K, D), lambda i: (0, 0)),
                  pl.BlockSpec((tm, D), lambda i: (i, 0))],
        out_specs=pl.BlockSpec((tm, D), lambda i: (i, 0)),
        out_shape=jax.ShapeDtypeStruct((T, D), F32),
        compiler_params=_params(("arbitrary",)),
    )(xa, w, res)


def _gmlp_gate(z, ws, bst, gv, G, gd):
    D = G * gd
    u = _gelu(z[:, :D].astype(F32))
    v = _gelu(z[:, D:].astype(F32))
    vh, r = _rms(v, None)
    vn = (vh * gv).astype(BF16)
    return u, v, vh, r, vn


def _gmlp_forward(z, ws, bst, gv, w_out, x, *, name, rider=None):
    T, D2 = z.shape
    D = D2 // 2
    G = ws.shape[0]
    gd = D // G
    tb = _tile(T, 256)
    nblk = tb // GMLP_BLOCK

    def body(z_ref, ws_ref, b_ref, gv_ref, wo_ref, x_ref, gated_ref, h_ref):
        u, _, _, _, vn = _gmlp_gate(z_ref[...], None, None, gv_ref[...], G, gd)
        for n in range(nblk):
            rows = slice(n * GMLP_BLOCK, (n + 1) * GMLP_BLOCK)
            for gi in range(G):
                cols = slice(gi * gd, (gi + 1) * gd)
                s = _dot(ws_ref[gi], vn[rows, cols]) + b_ref[:, gi:gi + 1]
                gated_ref[rows, cols] = (u[rows, cols] * s).astype(BF16)
        h_ref[...] = x_ref[...] + _dot(gated_ref[...], wo_ref[...])

    return _call(
        body, name=name, grid=(T // tb,),
        in_specs=[pl.BlockSpec((tb, D2), lambda i: (i, 0)), pl.BlockSpec(ws.shape, lambda i: (0, 0, 0)),
                  pl.BlockSpec(bst.shape, lambda i: (0, 0)), pl.BlockSpec((1, D), lambda i: (0, 0)),
                  pl.BlockSpec((D, D), lambda i: (0, 0)), pl.BlockSpec((tb, D), lambda i: (i, 0))],
        out_specs=[pl.BlockSpec((tb, D), lambda i: (i, 0)), pl.BlockSpec((tb, D), lambda i: (i, 0))],
        out_shape=[jax.ShapeDtypeStruct((T, D), BF16), jax.ShapeDtypeStruct((T, D), F32)],
        args=(z, ws, bst, gv, w_out, x), rider=rider)


def _gmlp_backward(z, dgated, ws, bst, gv, mask, *, name, rider=None):
    T, D2 = z.shape
    D = D2 // 2
    G = ws.shape[0]
    gd = D // G
    tb = _tile(T, 256)
    nblk = tb // GMLP_BLOCK

    def body(z_ref, dg_ref, ws_ref, b_ref, gv_ref, mask_ref, dz_ref, dws_ref, db_ref, dgv_ref, dvn_ref):
        @pl.when(pl.program_id(0) == 0)
        def _():
            dws_ref[...] = jnp.zeros_like(dws_ref)
            db_ref[...] = jnp.zeros_like(db_ref)
            dgv_ref[...] = jnp.zeros_like(dgv_ref)

        zf = z_ref[...]
        u, v, vh, r, vn = _gmlp_gate(zf, None, None, gv_ref[...], G, gd)
        dg = dg_ref[...].astype(F32)
        for n in range(nblk):
            rows = slice(n * GMLP_BLOCK, (n + 1) * GMLP_BLOCK)
            for gi in range(G):
                cols = slice(gi * gd, (gi + 1) * gd)
                vblk = vn[rows, cols]
                s = _dot(ws_ref[gi], vblk) + b_ref[:, gi:gi + 1]
                dgb = dg[rows, cols]
                ds = dgb * u[rows, cols]
                dsb = ds.astype(BF16)
                dz_ref[rows, cols] = (dgb * s * _gelu_grad(zf[rows, cols].astype(F32))).astype(BF16)
                dvn_ref[rows, cols] = _dot_tn(ws_ref[gi], dsb)
                dws_ref[gi] += _dot_nt(dsb, vblk) * mask_ref[...]
                db_ref[:, gi:gi + 1] += jnp.sum(ds, axis=1, keepdims=True)
        dvn = dvn_ref[...]
        dgv_ref[0:1, :] += jnp.sum(dvn * vh, axis=0, keepdims=True)
        dn = dvn * gv_ref[...]
        dv = r * (dn - vh * jnp.mean(dn * vh, axis=-1, keepdims=True))
        dz_ref[:, D:] = (dv * _gelu_grad(zf[:, D:].astype(F32))).astype(BF16)

    return _call(
        body, name=name, grid=(T // tb,),
        in_specs=[pl.BlockSpec((tb, D2), lambda i: (i, 0)), pl.BlockSpec((tb, D), lambda i: (i, 0)),
                  pl.BlockSpec(ws.shape, lambda i: (0, 0, 0)), pl.BlockSpec(bst.shape, lambda i: (0, 0)),
                  pl.BlockSpec((1, D), lambda i: (0, 0)), pl.BlockSpec(mask.shape, lambda i: (0, 0))],
        out_specs=[pl.BlockSpec((tb, D2), lambda i: (i, 0)), pl.BlockSpec(ws.shape, lambda i: (0, 0, 0)),
                   pl.BlockSpec(bst.shape, lambda i: (0, 0)), pl.BlockSpec((8, D), lambda i: (0, 0))],
        out_shape=[jax.ShapeDtypeStruct((T, D2), BF16), jax.ShapeDtypeStruct(ws.shape, F32),
                   jax.ShapeDtypeStruct(bst.shape, F32), jax.ShapeDtypeStruct((8, D), F32)],
        scratch_shapes=[pltpu.VMEM((tb, D), F32)], args=(z, dgated, ws, bst, gv, mask), rider=rider)


def _shift_rows(x, k):
    return pltpu.roll(x, k % x.shape[0], axis=0)


def _conv3(ext, cw):
    return (cw[0:1] * _shift_rows(ext, 2)[8:] + cw[1:2] * _shift_rows(ext, 1)[8:] + cw[2:3] * ext[8:])


def _ffn_forward(a, cw, cb, wd, h, seq, *, name, rider=None):
    _, T, F = a.shape
    D = h.shape[1]
    tm = _tile(seq, TOKEN_TILE)
    hb = tm // 16

    def body(a_ref, ap_ref, cw_ref, cb_ref, wd_ref, h_ref, act_ref, c_ref, o_ref, acc_ref):
        i, j = pl.program_id(0), pl.program_id(1)
        keep = ((i * tm) % seq != 0).astype(F32)

        def conv(b):
            ext = jnp.concatenate([ap_ref[b, 8:16].astype(F32) * keep, a_ref[b].astype(F32)], axis=0)
            return _conv3(ext, cw_ref[b]) + cb_ref[b]

        up, gate = conv(j), conv(j + 4)
        c_ref[j] = up.astype(BF16)
        c_ref[j + 4] = gate.astype(BF16)
        act = (gate * _sigmoid(gate) * up).astype(BF16)
        act_ref[0] = act

        @pl.when(j == 0)
        def _():
            acc_ref[...] = h_ref[...]

        acc_ref[...] += _dot(act, wd_ref[0])

        @pl.when(j == 3)
        def _():
            o_ref[...] = acc_ref[...]

    return _call(
        body, name=name, grid=(T // tm, 4),
        in_specs=[pl.BlockSpec((8, tm, F), lambda i, j: (0, i, 0)),
                  pl.BlockSpec((8, 16, F), lambda i, j: (0, jnp.maximum(i * hb - 1, 0), 0)),
                  pl.BlockSpec((8, 3, F), lambda i, j: (0, 0, 0)), pl.BlockSpec((8, 1, F), lambda i, j: (0, 0, 0)),
                  pl.BlockSpec((1, F, D), lambda i, j: (j, 0, 0)), pl.BlockSpec((tm, D), lambda i, j: (i, 0))],
        out_specs=[pl.BlockSpec((1, tm, F), lambda i, j: (j, i, 0)), pl.BlockSpec((8, tm, F), lambda i, j: (0, i, 0)),
                   pl.BlockSpec((tm, D), lambda i, j: (i, 0))],
        out_shape=[jax.ShapeDtypeStruct((4, T, F), BF16), jax.ShapeDtypeStruct((8, T, F), BF16),
                   jax.ShapeDtypeStruct((T, D), F32)],
        scratch_shapes=[pltpu.VMEM((tm, D), F32)], args=(a, a, cw, cb, wd, h), rider=rider)


def _ffn_backward(dh, c, a, cw, wd, seq, *, name, rider=None):
    _, T, F = a.shape
    D = dh.shape[1]
    tm = _tile(seq, TOKEN_TILE)
    hb = tm // 16
    nt = T // tm

    def body(dh_ref, dhn_ref, cu_ref, cg_ref, cun_ref, cgn_ref, au_ref, ag_ref, cw_ref, wd_ref, da_ref, st_ref):
        i, j = pl.program_id(0), pl.program_id(1)
        keep_next = (((i + 1) * tm) % seq != 0).astype(F32)

        @pl.when((i == 0) & (j == 0))
        def _():
            st_ref[...] = jnp.zeros_like(st_ref)

        dhe = jnp.concatenate([dh_ref[...], dhn_ref[...] * keep_next], axis=0).astype(BF16)
        dact = _dot_nt(dhe, wd_ref[0])
        up = jnp.concatenate([cu_ref[0].astype(F32), cun_ref[0, 0:8].astype(F32)], axis=0)
        gate = jnp.concatenate([cg_ref[0].astype(F32), cgn_ref[0, 0:8].astype(F32)], axis=0)
        sg = _sigmoid(gate)
        gs = gate * sg
        d_up = dact * gs
        d_gate = dact * up * (sg + gs * (1.0 - sg))

        def finish(b, a_ref, dc):
            w = cw_ref[b]
            dm, u1, u2 = dc[:tm], _shift_rows(dc, -1)[:tm], _shift_rows(dc, -2)[:tm]
            da_ref[b] = (w[2:3] * dm + w[1:2] * u1 + w[0:1] * u2).astype(BF16)
            av = a_ref[0].astype(F32)
            st_ref[b, 0:1, :] += jnp.sum(u2 * av, axis=0, keepdims=True)
            st_ref[b, 1:2, :] += jnp.sum(u1 * av, axis=0, keepdims=True)
            st_ref[b, 2:3, :] += jnp.sum(dm * av, axis=0, keepdims=True)
            st_ref[b, 3:4, :] += jnp.sum(dm, axis=0, keepdims=True)

        finish(j, au_ref, d_up)
        finish(j + 4, ag_ref, d_gate)

    nxt = lambda i: jnp.minimum((i + 1) * hb, T // 16 - 1)
    return _call(
        body, name=name, grid=(nt, 4),
        in_specs=[pl.BlockSpec((tm, D), lambda i, j: (i, 0)),
                  pl.BlockSpec((8, D), lambda i, j: (jnp.minimum((i + 1) * (tm // 8), T // 8 - 1), 0)),
                  pl.BlockSpec((1, tm, F), lambda i, j: (j, i, 0)), pl.BlockSpec((1, tm, F), lambda i, j: (j + 4, i, 0)),
                  pl.BlockSpec((1, 16, F), lambda i, j: (j, nxt(i), 0)),
                  pl.BlockSpec((1, 16, F), lambda i, j: (j + 4, nxt(i), 0)),
                  pl.BlockSpec((1, tm, F), lambda i, j: (j, i, 0)), pl.BlockSpec((1, tm, F), lambda i, j: (j + 4, i, 0)),
                  pl.BlockSpec((8, 3, F), lambda i, j: (0, 0, 0)),
                  pl.BlockSpec((1, F, D), lambda i, j: (j, 0, 0))],
        out_specs=[pl.BlockSpec((8, tm, F), lambda i, j: (0, i, 0)), pl.BlockSpec((8, 8, F), lambda i, j: (0, 0, 0))],
        out_shape=[jax.ShapeDtypeStruct((8, T, F), BF16), jax.ShapeDtypeStruct((8, 8, F), F32)],
        args=(dh, dh, c, c, c, c, a, a, cw, wd), rider=rider)


def _ffn_fused_forward(h, g, w_in, cw, cb, wd, seq, *, name, rider=None):
    T, D = h.shape
    F = w_in.shape[2]
    tm = _tile(seq, TOKEN_TILE // 2)
    bps = 4
    nj = 4 // bps

    def body(h_ref, g_ref, wu_ref, wg_ref, cw_ref, cb_ref, wd_ref,
             au_ref, ag_ref, cu_ref, cg_ref, act_ref, n_ref, o_ref, acc_ref, carry_ref, *work_refs):
        eu_refs, eg_refs, stage_refs = work_refs[:bps], work_refs[bps:2 * bps], work_refs[2 * bps:]
        i, j = pl.program_id(0), pl.program_id(1)
        keep = ((i * tm) % seq != 0).astype(F32)

        @pl.when((i == 0) & (j == 0))
        def _():
            carry_ref[...] = jnp.zeros_like(carry_ref)

        @pl.when(j == 0)
        def _():
            xh, _ = _rms(h_ref[...], None)
            n_ref[...] = (xh * g_ref[...]).astype(BF16)
            acc_ref[...] = h_ref[...]

        n = n_ref[...]

        def project(b, k, w_ref, a_ref, ext_ref):
            a = _dot(n, w_ref[k]).astype(BF16)
            a_ref[k] = a
            ext_ref[0:8, :] = carry_ref[b] * keep
            ext_ref[8:, :] = a.astype(F32)
            carry_ref[b] = ext_ref[tm:tm + 8, :]

        def conv(b, ext_ref, r):
            x, w = ext_ref[r:r + ROWS + 8, :], cw_ref[b]
            return (w[0:1] * _shift_rows(x, 2) + w[1:2] * _shift_rows(x, 1) + w[2:3] * x)[8:] + cb_ref[b]

        for k in range(bps):
            project(j * bps + k, k, wu_ref, au_ref, eu_refs[k])
            project(j * bps + k + 4, k, wg_ref, ag_ref, eg_refs[k])
        outs = []
        for k in range(bps):
            for r in range(0, tm, ROWS):
                up, gate = conv(j * bps + k, eu_refs[k], r), conv(j * bps + k + 4, eg_refs[k], r)
                cu_ref[k, r:r + ROWS, :] = up.astype(BF16)
                cg_ref[k, r:r + ROWS, :] = gate.astype(BF16)
                stage_refs[k][r:r + ROWS, :] = (gate * _sigmoid(gate) * up).astype(BF16)
            act = stage_refs[k][...]
            act_ref[k] = act
            outs.append(_dot(act, wd_ref[k]))
        acc_ref[...] += functools.reduce(lambda p, q: p + q, outs)

        @pl.when(j == nj - 1)
        def _():
            o_ref[...] = acc_ref[...]

    blk = pl.BlockSpec((bps, tm, F), lambda i, j: (j, i, 0))
    row = pl.BlockSpec((tm, D), lambda i, j: (i, 0))
    half = jax.ShapeDtypeStruct((4, T, F), BF16)
    work = [pltpu.VMEM((tm + 8, F), F32)] * (2 * bps) + [pltpu.VMEM((tm, F), BF16)] * bps
    return _call(
        body, name=name, grid=(T // tm, nj),
        in_specs=[row, pl.BlockSpec((1, D), lambda i, j: (0, 0)),
                  pl.BlockSpec((bps, D, F), lambda i, j: (j, 0, 0), pipeline_mode=pl.Buffered(1)),
                  pl.BlockSpec((bps, D, F), lambda i, j: (j + nj, 0, 0), pipeline_mode=pl.Buffered(1)),
                  pl.BlockSpec((8, 3, F), lambda i, j: (0, 0, 0)), pl.BlockSpec((8, 1, F), lambda i, j: (0, 0, 0)),
                  pl.BlockSpec((bps, F, D), lambda i, j: (j, 0, 0), pipeline_mode=pl.Buffered(1))],
        out_specs=[blk, blk, blk, blk, blk, row, row],
        out_shape=[half, half, half, half, half, jax.ShapeDtypeStruct((T, D), BF16), jax.ShapeDtypeStruct((T, D), F32)],
        scratch_shapes=[pltpu.VMEM((tm, D), F32), pltpu.VMEM((8, 8, F), F32)] + work,
        args=(h, g, w_in, w_in, cw, cb, wd), rider=rider)


def _ffn_fused_backward(dh, cu, cg, au, ag, cw, wd, w_in, h, g, seq, *, name, rider=None):
    T, D = dh.shape
    F = wd.shape[1]
    tm = _tile(seq, TOKEN_TILE // 2)
    hb = tm // 16
    nt = T // tm
    bps = FFN_BLOCKS_PER_STEP
    nj = 4 // bps

    def body(dh_ref, dhn_ref, cu_ref, cg_ref, cun_ref, cgn_ref, au_ref, ag_ref, cw_ref, wd_ref, wu_ref, wg_ref,
             h_ref, g_ref, dau_ref, dag_ref, st_ref, o_ref, dg_ref, acc_ref, dact_ref, du_ref, dgt_ref):
        j, i = pl.program_id(0), pl.program_id(1)
        keep_next = (((i + 1) * tm) % seq != 0).astype(F32)
        tile_rows = pl.ds(pl.multiple_of(i * tm, tm), tm) if nj > 1 else slice(0, tm)

        @pl.when((i == 0) & (j == 0))
        def _():
            st_ref[...] = jnp.zeros_like(st_ref)
            dg_ref[...] = jnp.zeros_like(dg_ref)

        dhe = jnp.concatenate([dh_ref[...], dhn_ref[...] * keep_next], axis=0).astype(BF16)
        for k in range(bps):
            dact_ref[k] = _dot_nt(dhe, wd_ref[k])

        def conv_grads(k):
            for r in range(0, tm + 8, ROWS):
                if r < tm:
                    rows = slice(r, r + ROWS)
                    up, gate = cu_ref[k, rows, :].astype(F32), cg_ref[k, rows, :].astype(F32)
                else:
                    rows = slice(tm, tm + 8)
                    up, gate = cun_ref[k, 0:8, :].astype(F32), cgn_ref[k, 0:8, :].astype(F32)
                dact = dact_ref[k, rows, :]
                sg = _sigmoid(gate)
                gs = gate * sg
                du_ref[k, rows, :] = dact * gs
                dgt_ref[k, rows, :] = dact * up * (sg + gs * (1.0 - sg))

        def finish(b, k, a_ref, w_ref, dc_ref, da_ref):
            w = cw_ref[b]
            sums = [jnp.zeros((8, F), F32) for _ in range(4)]
            fold = lambda t: jnp.sum(t.reshape(ROWS // 8, 8, F), axis=0)
            for r in range(0, tm, ROWS):
                dc = dc_ref[k, r:r + ROWS + 8, :]
                dm, u1, u2 = dc[:ROWS], _shift_rows(dc, -1)[:ROWS], _shift_rows(dc, -2)[:ROWS]
                da_ref[k, r:r + ROWS, :] = (w[2:3] * dm + w[1:2] * u1 + w[0:1] * u2).astype(BF16)
                av = a_ref[k, r:r + ROWS, :].astype(F32)
                for s, t in enumerate((u2 * av, u1 * av, dm * av, dm)):
                    sums[s] = sums[s] + fold(t)
            for s in range(4):
                st_ref[b, s:s + 1, :] += jnp.sum(sums[s], axis=0, keepdims=True)
            return _dot_nt(da_ref[k], w_ref[k])

        dn_parts = []
        for k in range(bps):
            conv_grads(k)
            dn_parts.append(finish(j * bps + k, k, au_ref, wu_ref, du_ref, dau_ref))
            dn_parts.append(finish(j * bps + k + 4, k, ag_ref, wg_ref, dgt_ref, dag_ref))
        dn_part = functools.reduce(lambda p, q: p + q, dn_parts)

        @pl.when(j == 0)
        def _():
            acc_ref[tile_rows, :] = dn_part

        @pl.when(j > 0)
        def _():
            acc_ref[tile_rows, :] += dn_part

        @pl.when(j == nj - 1)
        def _():
            acc = acc_ref[tile_rows, :]
            xh, r = _rms(h_ref[...], None)
            dg_ref[0:1, :] += jnp.sum(acc * xh, axis=0, keepdims=True)
            dn = acc * g_ref[...]
            o_ref[...] = dh_ref[...] + r * (dn - xh * jnp.mean(dn * xh, axis=-1, keepdims=True))

    last = lambda j, i: jnp.where(j == nj - 1, i, 0)
    nxt = lambda i: jnp.minimum((i + 1) * hb, T // 16 - 1)
    blk = pl.BlockSpec((bps, tm, F), lambda j, i: (j, i, 0))
    halo = pl.BlockSpec((bps, 16, F), lambda j, i: (j, nxt(i), 0))
    row = pl.BlockSpec((tm, D), lambda j, i: (i, 0))
    work = pltpu.VMEM((bps, tm + 8, F), F32)
    half = jax.ShapeDtypeStruct((4, T, F), BF16)
    once = pl.Buffered(1) if nj == 1 else None
    return _call(
        body, name=name, grid=(nj, nt),
        in_specs=[row, pl.BlockSpec((8, D), lambda j, i: (jnp.minimum((i + 1) * (tm // 8), T // 8 - 1), 0)),
                  blk, blk, halo, halo, blk, blk,
                  pl.BlockSpec((8, 3, F), lambda j, i: (0, 0, 0)),
                  pl.BlockSpec((bps, F, D), lambda j, i: (j, 0, 0), pipeline_mode=once),
                  pl.BlockSpec((bps, D, F), lambda j, i: (j, 0, 0), pipeline_mode=once),
                  pl.BlockSpec((bps, D, F), lambda j, i: (j + nj, 0, 0), pipeline_mode=once),
                  pl.BlockSpec((tm, D), lambda j, i: (last(j, i), 0)), pl.BlockSpec((1, D), lambda j, i: (0, 0))],
        out_specs=[blk, blk, pl.BlockSpec((8, 8, F), lambda j, i: (0, 0, 0)),
                   pl.BlockSpec((tm, D), lambda j, i: (last(j, i), 0)), pl.BlockSpec((8, D), lambda j, i: (0, 0))],
        out_shape=[half, half, jax.ShapeDtypeStruct((8, 8, F), F32),
                   jax.ShapeDtypeStruct((T, D), F32), jax.ShapeDtypeStruct((8, D), F32)],
        scratch_shapes=[pltpu.VMEM((T if nj > 1 else tm, D), F32), work, work, work],
        args=(dh, dh, cu, cg, cu, cg, au, ag, cw, wd, w_in, w_in, h, g), rider=rider)


def _rel_onehot():
    r = lax.broadcasted_iota(jnp.int32, (REL_PAD, SKEW), 0)
    n = lax.broadcasted_iota(jnp.int32, (REL_PAD, SKEW), 1)
    off = jnp.where(n >= WIN, n - SKEW, n)
    idx = jnp.minimum(PAD - off, REL_CLIP) + REL_CLIP
    return (r == idx).astype(BF16)


def _skew(x, sign):
    row = lax.broadcasted_iota(jnp.int32, x.shape, 0)
    for b in range(7):
        x = jnp.where((row >> b) & 1 == 1, pltpu.roll(x, (sign * (1 << b)) % SKEW, axis=1), x)
    return x


def _bias_build(rel, name, rider=None):
    H = rel.shape[0]

    def body(rel_ref, o_ref):
        oh = _rel_onehot()
        hi, mid, lo = _split3(rel_ref[...])
        base = _dot(hi, oh) + _dot(mid, oh) + _dot(lo, oh)
        mine = lax.broadcasted_iota(jnp.int32, (H, 1), 0) == pl.program_id(0)
        row = jnp.sum(jnp.where(mine, base, 0.0), axis=0, keepdims=True)
        q = lax.broadcasted_iota(jnp.int32, (Q_TILE, WIN), 0)
        k = lax.broadcasted_iota(jnp.int32, (Q_TILE, WIN), 1)
        ok = ((q < CHUNK) & (k < WIN - CHUNK)) | ((q >= CHUNK) & (k >= CHUNK))
        t = _skew(jnp.broadcast_to(row, (Q_TILE, SKEW)), 1)
        o_ref[0] = jnp.where(ok, t[:, :WIN], NEG_INF)

    return _call(
        body, name=name, grid=(H,), in_specs=[pl.BlockSpec((H, REL_PAD), lambda h: (0, 0))],
        out_specs=pl.BlockSpec((1, Q_TILE, WIN), lambda h: (h, 0, 0)),
        out_shape=jax.ShapeDtypeStruct((H, Q_TILE, WIN), F32), args=(rel,), rider=rider)


def _bias_reduce(dbias, name):
    H = dbias.shape[0]

    def body(d_ref, o_ref, e_ref):
        oh = _rel_onehot()
        for hd in range(H):
            x = jnp.concatenate([d_ref[hd], jnp.zeros((Q_TILE, SKEW - WIN), F32)], axis=1)
            e_ref[hd:hd + 1, :] = jnp.sum(_skew(x, -1), axis=0, keepdims=True)
        hi, mid, lo = _split3(e_ref[...])
        o_ref[...] = _dot_nt(hi, oh) + _dot_nt(mid, oh) + _dot_nt(lo, oh)

    return pl.pallas_call(
        body, name=name, out_shape=jax.ShapeDtypeStruct((H, REL_PAD), F32),
        in_specs=[pl.BlockSpec(memory_space=pltpu.VMEM)], out_specs=pl.BlockSpec(memory_space=pltpu.VMEM),
        scratch_shapes=[pltpu.VMEM((H, SKEW), F32)],
        compiler_params=_params(),
    )(dbias)


def _pair_stack(xp, even):
    z = jnp.zeros_like(xp)
    return jnp.concatenate([jnp.where(even, xp, z), jnp.where(even, z, xp)], axis=0)


def _pair_merge(y, even):
    return jnp.where(even, y[:Q_TILE], y[Q_TILE:])


def _strip_probs(s_ref, b_ref, pp, r, valid, base=0):
    hb, hr = divmod(r, Q_TILE)
    s = s_ref[base + pp, r:r + STRIP, :] + b_ref[2 * pp + hb, hr:hr + STRIP, :]
    s = jnp.where(valid, s, NEG_INF)
    e = jnp.exp(s - jnp.max(s, axis=-1, keepdims=True))
    return e * (1.0 / jnp.sum(e, axis=-1, keepdims=True))


def _fill_padded(dst_ref, src_ref):
    dst_ref[0:PAD, :] = jnp.zeros((PAD, dst_ref.shape[1]), dst_ref.dtype)
    dst_ref[PAD:, :] = src_ref[...]


def _attn_specs(B, S, D, lanes):
    nt = S // (TILES_PER_STEP * Q_TILE)
    q_spec = pl.BlockSpec((TILES_PER_STEP * Q_TILE, lanes), lambda g, b, i: (b * nt + i, g))
    k_spec = pl.BlockSpec((S, lanes), lambda g, b, i: (b, g))
    v_spec = pl.BlockSpec((S, lanes), lambda g, b, i: (b, D // lanes + g))
    bias_spec = pl.BlockSpec((lanes // HEAD_DIM, Q_TILE, WIN), lambda g, b, i: (g, 0, 0))
    return nt, q_spec, k_spec, v_spec, bias_spec


def _attn_forward(q, kv, bias, S, *, name, rider=None):
    T, D = q.shape
    B = T // S
    lanes = min(FWD_HEADS_PER_STEP * HEAD_DIM, D)
    nt, q_spec, k_spec, v_spec, bias_spec = _attn_specs(B, S, D, lanes)

    npairs = lanes // (2 * HEAD_DIM)

    def body(q_ref, k_ref, v_ref, b_ref, o_ref, kp_ref, vp_ref, s_ref, p_ref):
        i = pl.program_id(2)

        @pl.when(i == 0)
        def _():
            _fill_padded(kp_ref, k_ref)
            _fill_padded(vp_ref, v_ref)

        even = lax.broadcasted_iota(jnp.int32, (1, 2 * HEAD_DIM), 1) < HEAD_DIM
        pair_cols = [slice(pp * 2 * HEAD_DIM, (pp + 1) * 2 * HEAD_DIM) for pp in range(npairs)]
        for t in range(TILES_PER_STEP):
            tile = i * TILES_PER_STEP + t
            start = pl.multiple_of(tile * Q_TILE, Q_TILE)
            rows = slice(t * Q_TILE, (t + 1) * Q_TILE)
            valid = lax.broadcasted_iota(jnp.int32, (STRIP, WIN), 1) >= PAD - tile * Q_TILE
            for pp, cols in enumerate(pair_cols):
                s_ref[t * npairs + pp] = _dot_nt(_pair_stack(q_ref[rows, cols], even), kp_ref[pl.ds(start, WIN), cols])
            for pp in range(npairs):
                for r in range(0, 2 * Q_TILE, STRIP):
                    p = _strip_probs(s_ref, b_ref, pp, r, valid, base=t * npairs)
                    p_ref[t * npairs + pp, r:r + STRIP, :] = p.astype(BF16)
            for pp, cols in enumerate(pair_cols):
                o = _dot(p_ref[t * npairs + pp], vp_ref[pl.ds(start, WIN), cols])
                o_ref[rows, cols] = _pair_merge(o, even).astype(BF16)

    nbuf = TILES_PER_STEP * npairs
    return _call(
        body, name=name, grid=(D // lanes, B, nt),
        in_specs=[q_spec, k_spec, v_spec, bias_spec], out_specs=q_spec,
        out_shape=jax.ShapeDtypeStruct((T, D), BF16),
        scratch_shapes=[pltpu.VMEM((S + PAD, lanes), BF16), pltpu.VMEM((S + PAD, lanes), BF16),
                        pltpu.VMEM((nbuf, 2 * Q_TILE, WIN), F32), pltpu.VMEM((nbuf, 2 * Q_TILE, WIN), BF16)],
        args=(q, kv, kv, bias), rider=rider)


def _attn_backward(q, kv, bias, do, S, *, name, rider=None):
    T, D = q.shape
    B = T // S
    H = D // HEAD_DIM
    lanes = min(BWD_HEADS_PER_STEP * HEAD_DIM, D)
    nt, q_spec, k_spec, v_spec, bias_spec = _attn_specs(B, S, D, lanes)
    scale = HEAD_DIM ** -0.5

    npairs = lanes // (2 * HEAD_DIM)

    def body(q_ref, k_ref, v_ref, b_ref, do_ref, dq_ref, dk_ref, dv_ref, db_ref, kp_ref, vp_ref, dka_ref, dva_ref,
             s_ref, dp_ref, p_ref, ds_ref):
        b, i = pl.program_id(1), pl.program_id(2)

        @pl.when((b == 0) & (i == 0))
        def _():
            db_ref[...] = jnp.zeros_like(db_ref)

        @pl.when(i == 0)
        def _():
            _fill_padded(kp_ref, k_ref)
            _fill_padded(vp_ref, v_ref)
            dka_ref[...] = jnp.zeros_like(dka_ref)
            dva_ref[...] = jnp.zeros_like(dva_ref)

        even = lax.broadcasted_iota(jnp.int32, (1, 2 * HEAD_DIM), 1) < HEAD_DIM
        pair_cols = [slice(pp * 2 * HEAD_DIM, (pp + 1) * 2 * HEAD_DIM) for pp in range(npairs)]
        for t in range(TILES_PER_STEP):
            tile = i * TILES_PER_STEP + t
            start = pl.multiple_of(tile * Q_TILE, Q_TILE)
            rows = slice(t * Q_TILE, (t + 1) * Q_TILE)
            valid = lax.broadcasted_iota(jnp.int32, (STRIP, WIN), 1) >= PAD - tile * Q_TILE
            base = t * npairs
            for pp, cols in enumerate(pair_cols):
                s_ref[base + pp] = _dot_nt(_pair_stack(q_ref[rows, cols], even), kp_ref[pl.ds(start, WIN), cols])
                dp_ref[base + pp] = _dot_nt(_pair_stack(do_ref[rows, cols], even), vp_ref[pl.ds(start, WIN), cols])
            for pp in range(npairs):
                for r in range(0, 2 * Q_TILE, STRIP):
                    hb, hr = divmod(r, Q_TILE)
                    p = _strip_probs(s_ref, b_ref, pp, r, valid, base=base)
                    dp = dp_ref[base + pp, r:r + STRIP, :]
                    ds = p * (dp - jnp.sum(p * dp, axis=-1, keepdims=True))
                    db_ref[2 * pp + hb, hr:hr + STRIP, :] += ds
                    p_ref[base + pp, r:r + STRIP, :] = p.astype(BF16)
                    ds_ref[base + pp, r:r + STRIP, :] = ds.astype(BF16)
            for pp, cols in enumerate(pair_cols):
                dsb = ds_ref[base + pp]
                dq = _pair_merge(_dot(dsb, kp_ref[pl.ds(start, WIN), cols]), even) * scale
                dq_ref[rows, cols] = dq.astype(BF16)
                dka_ref[pl.ds(start, WIN), cols] += _dot_tn(dsb, _pair_stack(q_ref[rows, cols], even))
                dva_ref[pl.ds(start, WIN), cols] += _dot_tn(p_ref[base + pp], _pair_stack(do_ref[rows, cols], even))

        @pl.when(i == nt - 1)
        def _():
            dk_ref[...] = dka_ref[PAD:, :].astype(BF16)
            dv_ref[...] = dva_ref[PAD:, :].astype(BF16)

    dkv_shape = jax.ShapeDtypeStruct((T, D), BF16)
    nbuf = TILES_PER_STEP * npairs
    return _call(
        body, name=name, grid=(D // lanes, B, nt),
        in_specs=[q_spec, k_spec, v_spec, bias_spec, q_spec],
        out_specs=[q_spec, k_spec, k_spec, bias_spec],
        out_shape=[jax.ShapeDtypeStruct((T, D), BF16), dkv_shape, dkv_shape,
                   jax.ShapeDtypeStruct((H, Q_TILE, WIN), F32)],
        scratch_shapes=[pltpu.VMEM((S + PAD, lanes), BF16), pltpu.VMEM((S + PAD, lanes), BF16),
                        pltpu.VMEM((S + PAD, lanes), F32), pltpu.VMEM((S + PAD, lanes), F32),
                        pltpu.VMEM((nbuf, 2 * Q_TILE, WIN), F32), pltpu.VMEM((nbuf, 2 * Q_TILE, WIN), F32),
                        pltpu.VMEM((nbuf, 2 * Q_TILE, WIN), BF16), pltpu.VMEM((nbuf, 2 * Q_TILE, WIN), BF16)],
        args=(q, kv, kv, bias, do), rider=rider)


def _loss_head(h, g, target, name):
    T, D = h.shape
    tm = _tile(T, MATMUL_TILE)

    def body(h_ref, g_ref, t_ref, dh_ref, st_ref):
        @pl.when(pl.program_id(0) == 0)
        def _():
            st_ref[...] = jnp.zeros_like(st_ref)

        xh, r = _rms(h_ref[...], None)
        err = xh * g_ref[...] - t_ref[...]
        st_ref[1:2, :] += 0.5 * jnp.sum(jnp.mean(err * err, axis=-1, keepdims=True), axis=0, keepdims=True)
        dy = err * (1.0 / D)
        st_ref[0:1, :] += jnp.sum(dy * xh, axis=0, keepdims=True)
        dn = dy * g_ref[...]
        dh_ref[...] = r * (dn - xh * jnp.mean(dn * xh, axis=-1, keepdims=True))

    row = pl.BlockSpec((tm, D), lambda i: (i, 0))
    return pl.pallas_call(
        body, name=name, grid=(T // tm,),
        in_specs=[row, pl.BlockSpec((1, D), lambda i: (0, 0)), row],
        out_specs=[row, pl.BlockSpec((8, D), lambda i: (0, 0))],
        out_shape=[jax.ShapeDtypeStruct((T, D), F32), jax.ShapeDtypeStruct((8, D), F32)],
        compiler_params=_params(("arbitrary",)),
    )(h, g, target)


def _sum_devices(arrs, name):
    n = len(arrs)

    def body(*refs):
        for a in range(n):
            s = refs[a][0].astype(F32)
            for k in range(1, N_DEV):
                s = s + refs[a][k].astype(F32)
            refs[n + a][...] = s

    vm = pl.BlockSpec(memory_space=pltpu.VMEM)
    return pl.pallas_call(
        body, name=name, out_shape=[jax.ShapeDtypeStruct(a.shape[1:], F32) for a in arrs],
        in_specs=[vm] * n, out_specs=[vm] * n, compiler_params=_params(),
    )(*arrs)


def _adamw_math(w, g, m, v):
    m = ADAM_B1 * m + (1.0 - ADAM_B1) * g
    v = ADAM_B2 * v + (1.0 - ADAM_B2) * (g * g)
    m_hat = m / (1.0 - ADAM_B1 ** ADAM_STEP)
    v_hat = v / (1.0 - ADAM_B2 ** ADAM_STEP)
    delta = -ADAM_LR * (m_hat / (jnp.sqrt(v_hat) + ADAM_EPS) + ADAM_WD * w)
    return delta, m, v


def _adamw_small(items, name):
    n = len(items)

    def body(*refs):
        for a in range(n):
            w, m, v, g = (refs[4 * a + k][...] for k in range(4))
            d, m, v = _adamw_math(w, g, m, v)
            refs[4 * n + 3 * a][...] = d
            refs[4 * n + 3 * a + 1][...] = m
            refs[4 * n + 3 * a + 2][...] = v

    vm = pl.BlockSpec(memory_space=pltpu.VMEM)
    flat = [t for it in items for t in it]
    outs = pl.pallas_call(
        body, name=name,
        out_shape=[jax.ShapeDtypeStruct(it[0].shape, F32) for it in items for _ in range(3)],
        in_specs=[vm] * (4 * n), out_specs=[vm] * (3 * n), compiler_params=_params(),
    )(*flat)
    return [tuple(outs[3 * a:3 * a + 3]) for a in range(n)]


def _adamw_big(w, m, v, owns, landeds, name, rider=None):
    L, R, C = w.shape
    tr = _tile(R, 512)
    nr = R // tr
    counts = [len(ls) for ls in landeds]

    def body(*refs):
        w_ref, m_ref, v_ref = refs[:3]
        g_ref, d_ref, mo_ref, vo_ref = refs[-4:]
        layer = pl.program_id(0)
        at = 3
        for j in range(L):
            own_ref, l_refs = refs[at], refs[at + 1:at + 1 + counts[j]]
            at += 1 + counts[j]

            @pl.when(layer == j)
            def _(own_ref=own_ref, l_refs=l_refs):
                g = own_ref[...]
                for l_ref in l_refs:
                    for k in range(l_ref.shape[0]):
                        g = g + l_ref[k].astype(F32)
                d, mn, vn = _adamw_math(w_ref[0], g, m_ref[0], v_ref[0])
                g_ref[0] = g
                d_ref[0] = d
                mo_ref[0] = mn
                vo_ref[0] = vn

    def pinned(j):
        return lambda l, i: jnp.where(l == j, i, jnp.where(l < j, 0, nr - 1))

    row = pl.BlockSpec((1, tr, C), lambda l, i: (l, i, 0))
    in_specs, args = [row, row, row], [w, m, v]
    for j in range(L):
        in_specs.append(pl.BlockSpec((tr, C), lambda l, i, p=pinned(j): (p(l, i), 0)))
        args.append(owns[j])
        for arr in landeds[j]:
            in_specs.append(pl.BlockSpec((arr.shape[0], tr, C), lambda l, i, p=pinned(j): (0, p(l, i), 0)))
            args.append(arr)
    return _call(body, name=name, grid=(L, nr), in_specs=in_specs, out_specs=[row] * 4,
                 out_shape=[jax.ShapeDtypeStruct((L, R, C), F32)] * 4, args=args, rider=rider)


def _adamw_transposed(wt, mt, vt, owns, landeds, name, rider=None):
    L, C, R = wt.shape
    tc = _tile(R, 256)
    nr = R // tc
    counts = [len(ls) for ls in landeds]

    def body(*refs):
        w_ref, m_ref, v_ref = refs[:3]
        g_ref, d_ref, mo_ref, vo_ref = refs[-4:]
        layer = pl.program_id(0)
        eye = (lax.broadcasted_iota(jnp.int32, (tc, tc), 0) == lax.broadcasted_iota(jnp.int32, (tc, tc), 1)).astype(BF16)
        at = 3
        for j in range(L):
            own_ref, l_refs = refs[at], refs[at + 1:at + 1 + counts[j]]
            at += 1 + counts[j]

            @pl.when(layer == j)
            def _(own_ref=own_ref, l_refs=l_refs):
                hi, mid, lo = _split3(own_ref[...])
                g = (_dot_tn(hi, eye) + _dot_tn(mid, eye)) + _dot_tn(lo, eye)
                for l_ref in l_refs:
                    for k in range(l_ref.shape[0]):
                        g = g + _dot_tn(l_ref[k], eye)
                d, mn, vn = _adamw_math(w_ref[0], g, m_ref[0], v_ref[0])
                g_ref[0] = g
                d_ref[0] = d
                mo_ref[0] = mn
                vo_ref[0] = vn

    def pinned(j):
        return lambda l, i: jnp.where(l == j, i, jnp.where(l < j, 0, nr - 1))

    col = pl.BlockSpec((1, C, tc), lambda l, i: (l, 0, i))
    in_specs, args = [col, col, col], [wt, mt, vt]
    for j in range(L):
        in_specs.append(pl.BlockSpec((tc, C), lambda l, i, p=pinned(j): (p(l, i), 0)))
        args.append(owns[j])
        for arr in landeds[j]:
            in_specs.append(pl.BlockSpec((arr.shape[0], tc, C), lambda l, i, p=pinned(j): (0, p(l, i), 0)))
            args.append(arr)
    return _call(body, name=name, grid=(L, nr), in_specs=in_specs, out_specs=[col] * 4,
                 out_shape=[jax.ShapeDtypeStruct((L, C, R), F32)] * 4, args=args, rider=rider)


def kernel(x, a_norm_g, a_w_in, a_v_norm_g, a_w_s, a_b_s, a_w_out, kv_norm_g, w_kv, b_norm_g, b_w_q, b_rel_bias, b_w_o, f_norm_g, f_w_in, f_conv_w, f_conv_b, f_w_down, final_norm_g, loss_target, m_a_norm_g, m_a_w_in, m_a_v_norm_g, m_a_w_s, m_a_b_s, m_a_w_out, m_kv_norm_g, m_w_kv, m_b_norm_g, m_b_w_q, m_b_rel_bias, m_b_w_o, m_f_norm_g, m_f_w_in, m_f_conv_w, m_f_conv_b, m_f_w_down, m_final_norm_g, v_a_norm_g, v_a_w_in, v_a_v_norm_g, v_a_w_s, v_a_b_s, v_a_w_out, v_kv_norm_g, v_w_kv, v_b_norm_g, v_b_w_q, v_b_rel_bias, v_b_w_o, v_f_norm_g, v_f_w_in, v_f_conv_w, v_f_conv_b, v_f_w_down, v_final_norm_g):
    B, S, D = x.shape
    T = B * S
    G = a_w_s.shape[1]
    H = D // HEAD_DIM
    F = f_w_in.shape[2]
    L = f_w_in.shape[0]
    dn = D // N_DEV
    xi, yi, ci = lax.axis_index("x"), lax.axis_index("y"), lax.axis_index("c")
    me = 4 * xi + 2 * yi + ci
    pos = jnp.stack([ci, 2 * xi + yi]).astype(jnp.int32)

    cast = lambda t: t.astype(BF16)
    gather = lambda *ts: _gather_rider(list(ts))
    rel = jnp.pad(b_rel_bias[0], ((0, 0), (0, REL_PAD - b_rel_bias.shape[2])))
    bias, (wa_in, norms_sh, conv_w0, conv_w1) = _bias_build(rel, "bias_build", rider=gather(
        cast(a_w_in[0]), jnp.concatenate([a_norm_g, a_v_norm_g], axis=0), f_conv_w[0], f_conv_w[1]))
    ga = jnp.transpose(norms_sh, (1, 0, 2)).reshape(2, D)
    g_a, g_av = ga[0:1], ga[1:2]

    x2 = x.reshape(T, D)
    tgt = loss_target.reshape(T, D)
    pc = jnp.arange(GMLP_BLOCK) // CHUNK
    mask = (pc[:, None] >= pc[None, :]).astype(F32)
    ws = (a_w_s[0] * mask[None]).astype(BF16)
    bst = jnp.transpose(a_b_s[0])
    four = lambda t: t.reshape((4, 2) + t.shape[1:])

    (z, n_a), (wa_out, w_in0) = _norm_matmul(x2, g_a, wa_in, flat=True, nbk=4, name="gmlp_in",
                                             rider=gather(cast(a_w_out[0]), cast(f_w_in[0])))
    wa_out = wa_out.reshape(D, D)
    (gated, h1), (wf_down0,) = _gmlp_forward(z, ws, bst, g_av, wa_out, x2, name="gmlp_mix",
                                             rider=gather(cast(f_w_down[0])))
    cw0, cb0, wd0 = conv_w0, f_conv_b[0].reshape(8, 1, F), wf_down0.reshape(4, F, D)
    (au0, ag0, cu0, cg0, act0, n_f0, h2), (wkv, wq, w_in1) = _ffn_fused_forward(
        h1, f_norm_g[0:1], w_in0, cw0, cb0, wd0, S, name="ffn0_fwd",
        rider=gather(cast(w_kv), cast(b_w_q[0]), cast(f_w_in[1])))
    wq = wq.reshape(D, D)
    kv, n_kv = _norm_matmul(h2, kv_norm_g.reshape(1, D), wkv, flat=True, nbk=4, name="kv_proj")
    q, n_q = _norm_matmul(h2, b_norm_g, wq.reshape(1, D, D), flat=True, nbk=1, name="q_proj", scale=HEAD_DIM ** -0.5)
    o, (wo, wf_down1) = _attn_forward(q, kv, bias, S, name="attn", rider=gather(cast(b_w_o[0]), cast(f_w_down[1])))
    wo = wo.reshape(D, D)
    cw1, cb1, wd1 = conv_w1, f_conv_b[1].reshape(8, 1, F), wf_down1.reshape(4, F, D)
    h3 = _matmul_residual(o, wo, h2, "attn_out")
    au1, ag1, cu1, cg1, act1, n_f1, h4 = _ffn_fused_forward(h3, f_norm_g[1:2], w_in1, cw1, cb1, wd1, S, name="ffn1_fwd")

    sums, from_chips = {}, {}

    def sibling_sums(names, parts, landed):
        for nm, p, l in zip(names, parts, landed):
            sums[nm] = _sibling_sum(p, l, pos, "grad_sibling_sum_" + nm)

    def chip_rider(*names):
        return _chip_rider([sums[nm][1] for nm in names])

    dh4, st_final = _loss_head(h4, final_norm_g.reshape(1, D), tgt, "loss_head")
    g_wd1 = _wgrad_rows(act1, dh4, flat=False, tk=F, name="ffn1_dwdown")
    parts = [four(g_wd1.reshape(8, F // 2, D))]
    (dau1, dag1, st_conv1, dh3, st_f1), landed = _ffn_fused_backward(
        dh4, cu1, cg1, au1, ag1, cw1, wd1, w_in1, h3, f_norm_g[1:2], S, name="ffn1_bwd", rider=_sibling_rider(parts))
    sibling_sums(["wd1"], parts, landed)
    g_win1, (from_chips["wd1"],) = _wgrad_cols(n_f1, (dau1, dag1), flat=False, nb=8, nbk=2, name="ffn1_dwin",
                                               rider=chip_rider("wd1"))
    d_o = _matmul_nt(dh3, wo.reshape(1, D, D), flat=True, nbk=1, name="attn_out_dx")
    parts = [four(g_win1)]
    g_wo, landed = _wgrad_rows(o, dh3, flat=True, tk=_tile(D, 512), name="attn_out_dw", rider=_sibling_rider(parts))
    sibling_sums(["win1"], parts, landed)
    (dq, dk, dv, dbias), (from_chips["win1"],) = _attn_backward(
        q, kv, bias, d_o, S, name="attn_bwd", rider=chip_rider("win1"))
    g_rel = _bias_reduce(dbias, "bias_reduce")
    g_wq = _wgrad_cols(n_q, dq, flat=True, nb=1, nbk=1, name="q_dw")
    dh2, st_b = _matmul_nt(dq, wq.reshape(1, D, D), flat=True, nbk=1, name="q_dx", norm=(h2, b_norm_g, dh3))
    g_wkv = _wgrad_cols(n_kv, (dk, dv), flat=True, nb=8, nbk=4, name="kv_dw")
    parts = [four(g_wo.reshape(8, dn, D)), four(g_wq.reshape(8, dn, D)), four(g_wkv)]
    (dh2, st_kv), landed = _matmul_nt((dk, dv), wkv, flat=True, nbk=4, name="kv_dx",
                                      norm=(h2, kv_norm_g.reshape(1, D), dh2), rider=_sibling_rider(parts))
    sibling_sums(["wo", "wq", "wkv"], parts, landed)
    g_wd0, (from_chips["wo"], from_chips["wq"], from_chips["wkv"]) = _wgrad_rows(
        act0, dh2, flat=False, tk=F, name="ffn0_dwdown", rider=chip_rider("wo", "wq", "wkv"))
    parts = [four(g_wd0.reshape(8, F // 2, D))]
    (dau0, dag0, st_conv0, dh1, st_f0), landed = _ffn_fused_backward(
        dh2, cu0, cg0, au0, ag0, cw0, wd0, w_in0, h1, f_norm_g[0:1], S, name="ffn0_bwd", rider=_sibling_rider(parts))
    sibling_sums(["wd0"], parts, landed)
    g_win0, (ce,) = _wgrad_cols(n_f0, (dau0, dag0), flat=False, nb=8, nbk=2, name="ffn0_dwin",
                                rider=chip_rider("wd0"))
    from_chips["wd0"] = [ce]
    dgated = _matmul_nt(dh1, wa_out.reshape(1, D, D), flat=True, nbk=1, name="gmlp_out_dx")
    parts = [four(g_win0)]
    g_wa_out, landed = _wgrad_rows(gated, dh1, flat=True, tk=_tile(D, 512), name="gmlp_out_dw",
                                   rider=_sibling_rider(parts))
    sibling_sums(["win0"], parts, landed)
    parts = [four(g_wa_out.reshape(8, dn, D))]
    (dz, g_ws, g_bst, st_av), (ce_win0_a, landed) = _gmlp_backward(
        z, dgated, ws, bst, g_av, mask, name="gmlp_bwd",
        rider=_join_riders([_chip_rider([sums["win0"][1]], ks=(1, 2)), _sibling_rider(parts)]))
    sibling_sums(["wa_out"], parts, [landed])
    g_wa_in, (ce_win0_b,) = _wgrad_cols(n_a, dz, flat=True, nb=8, nbk=4, name="gmlp_in_dw",
                                        rider=_chip_rider([sums["win0"][1]], ks=(3,)))
    from_chips["win0"] = [ce_win0_a, ce_win0_b]
    vec = jnp.concatenate([st_av[0:1], st_kv[0:1], st_b[0:1], st_f0[0:1], st_f1[0:1], st_final[0:3]], axis=0)
    parts = [four(g_wa_in)]
    (grad_x, st_a), got = _matmul_nt(dz, wa_in, flat=True, nbk=4, name="gmlp_in_dx", norm=(x2, g_a, dh1),
                                     rider=_join_riders([_sibling_rider(parts), chip_rider("wa_out"), gather(
                                         vec, cast(g_ws), cast(g_bst), cast(g_rel), cast(st_conv0), cast(st_conv1))]))
    sibling_sums(["wa_in"], parts, got[0:1])
    from_chips["wa_out"], small = [got[1]], got[2:]

    def big_update(names, w, m, v, rider=None):
        shape = w.shape
        r = lambda t: t.reshape((len(names), -1, shape[-1]))
        as_list = lambda t: t if isinstance(t, list) else [t]
        outs = _adamw_big(r(w), r(m), r(v), [sums[nm][0] for nm in names],
                          [as_list(from_chips[nm]) for nm in names], "adamw_" + names[0], rider=rider)
        outs, got = (outs, None) if rider is None else outs
        return [t.reshape(shape) for t in outs], got

    u_f_w_down, (ce, st_a) = big_update(["wd0", "wd1"], f_w_down, m_f_w_down, v_f_w_down,
                                        rider=_join_riders([chip_rider("wa_in"), gather(st_a)]))
    from_chips["wa_in"] = [ce]
    names = ["win0", "win1"]
    tr = lambda t: jnp.swapaxes(t, 1, 2)
    outs = _adamw_transposed(
        tr(f_w_in), tr(m_f_w_in), tr(v_f_w_in), [sums[nm][0] for nm in names],
        [t if isinstance(t, list) else [t] for t in (from_chips[nm] for nm in names)], "adamw_win0")
    u_f_w_in = [tr(t) for t in outs]
    u_a_w_in, _ = big_update(["wa_in"], a_w_in, m_a_w_in, v_a_w_in)
    u_w_kv, _ = big_update(["wkv"], w_kv, m_w_kv, v_w_kv)
    u_a_w_out, _ = big_update(["wa_out"], a_w_out, m_a_w_out, v_a_w_out)
    u_b_w_q, _ = big_update(["wq"], b_w_q, m_b_w_q, v_b_w_q)
    u_b_w_o, _ = big_update(["wo"], b_w_o, m_b_w_o, v_b_w_o)

    vec, g_ws, g_bst, g_rel, st_conv0, st_conv1, st_a = _sum_devices(list(small) + [st_a], "sum_small_grads")
    vec = jnp.concatenate([st_a[0:1], vec[0:7]], axis=0)
    loss = vec[7, 0]
    g_a_norm = lax.dynamic_slice_in_dim(vec[0:1], me * dn, dn, axis=1)
    g_av_norm = lax.dynamic_slice_in_dim(vec[1:2], me * dn, dn, axis=1)
    st_conv = jnp.stack([st_conv0, st_conv1])
    g_conv_w = lax.dynamic_index_in_dim(st_conv, me, axis=1, keepdims=False)[:, 0:3]
    g_conv_b = st_conv[:, :, 3, :].reshape(L, 8 * F)
    small_items = [
        (a_norm_g, m_a_norm_g, v_a_norm_g, g_a_norm),
        (a_v_norm_g, m_a_v_norm_g, v_a_v_norm_g, g_av_norm),
        (a_w_s, m_a_w_s, v_a_w_s, g_ws[None]),
        (a_b_s, m_a_b_s, v_a_b_s, jnp.transpose(g_bst)[None]),
        (kv_norm_g.reshape(1, D), m_kv_norm_g.reshape(1, D), v_kv_norm_g.reshape(1, D), vec[2:3]),
        (b_norm_g, m_b_norm_g, v_b_norm_g, vec[3:4]),
        (b_rel_bias, m_b_rel_bias, v_b_rel_bias, g_rel[None, :, :b_rel_bias.shape[2]]),
        (f_norm_g, m_f_norm_g, v_f_norm_g, vec[4:6]),
        (f_conv_w, m_f_conv_w, v_f_conv_w, g_conv_w),
        (f_conv_b, m_f_conv_b, v_f_conv_b, g_conv_b),
        (final_norm_g.reshape(1, D), m_final_norm_g.reshape(1, D), v_final_norm_g.reshape(1, D), vec[6:7]),
    ]
    small_out = _adamw_small(small_items, "adamw_small")
    (u_a_norm, u_av_norm, u_ws, u_bs, u_kvn, u_bn, u_rel, u_fn, u_cw, u_cb, u_fin) = [
        (it[3],) + so for it, so in zip(small_items, small_out)]
    vecD = lambda u: tuple(t.reshape(D) for t in u)
    u_kvn, u_fin = vecD(u_kvn), vecD(u_fin)

    order = [u_a_norm, u_a_w_in, u_av_norm, u_ws, u_bs, u_a_w_out, u_kvn, u_w_kv, u_bn, u_b_w_q, u_rel, u_b_w_o,
             u_fn, u_f_w_in, u_cw, u_cb, u_f_w_down, u_fin]
    outs = [loss, grad_x.reshape(B, S, D)]
    for k in range(4):
        outs += [u[k] for u in order]
    return tuple(outs)
```

```python
import functools

import jax
import jax.numpy as jnp
from jax import lax
from jax.experimental import pallas as pl
from jax.experimental.pallas import tpu as pltpu

F32 = jnp.float32
BF16 = jnp.bfloat16
MESH = pl.DeviceIdType.MESH

N_DEV = 8
EPS = 1e-6
NEG_INF = -1e30
CHUNK = 64
LEFT_CHUNKS = 8
REL_CLIP = 128
HEAD_DIM = 64
GMLP_BLOCK = 128
Q_TILE = 2 * CHUNK
PAD = LEFT_CHUNKS * CHUNK
WIN = PAD + Q_TILE
SKEW = WIN + Q_TILE
REL_PAD = 384
FWD_HEADS_PER_STEP = 8
BWD_HEADS_PER_STEP = 4
TILES_PER_STEP = 4
STRIP = 32
ROWS = 32
ADAM_LR, ADAM_B1, ADAM_B2, ADAM_EPS, ADAM_WD, ADAM_STEP = 0.001, 0.9, 0.999, 1e-08, 0.01, 10
VMEM_LIMIT = 60 * 1024 * 1024
TOKEN_TILE = 512
MATMUL_TILE = 1024
WGRAD_TILE = 2048
FFN_BLOCKS_PER_STEP = 4


def _params(sem=None):
    return pltpu.CompilerParams(dimension_semantics=sem, vmem_limit_bytes=VMEM_LIMIT)


def _tile(n, pref):
    if n <= pref:
        return n
    for t in range(pref - pref % 8, 7, -8):
        if n % t == 0:
            return t
    return n


def _gelu(x):
    return 0.5 * x * (1.0 + jnp.tanh(0.7978845608028654 * (x + 0.044715 * x * x * x)))


def _gelu_grad(x):
    t = jnp.tanh(0.7978845608028654 * (x + 0.044715 * x * x * x))
    return 0.5 * (1.0 + t) + 0.5 * x * (1.0 - t * t) * 0.7978845608028654 * (1.0 + 3 * 0.044715 * x * x)


def _sigmoid(x):
    return 1.0 / (1.0 + jnp.exp(-x))


def _dot(a, b):
    return jnp.dot(a, b, preferred_element_type=F32)


def _dot_nt(a, b):
    return lax.dot_general(a, b, (((1,), (1,)), ((), ())), preferred_element_type=F32)


def _dot_tn(a, b):
    return lax.dot_general(a, b, (((0,), (0,)), ((), ())), preferred_element_type=F32)


def _split3(x):
    hi = x.astype(BF16)
    r1 = x - hi.astype(F32)
    mid = r1.astype(BF16)
    lo = (r1 - mid.astype(F32)).astype(BF16)
    return hi, mid, lo


def _mesh_pos():
    return lax.axis_index("x"), lax.axis_index("y"), lax.axis_index("c")


class _Rider:
    def __init__(self, arrs, out_shapes, sems, start, finish):
        self.arrs, self.out_shapes, self.sems, self.start, self.finish = arrs, out_shapes, sems, start, finish


def _gather_rider(arrs):
    n = len(arrs)

    def tools(ins, outs, sems):
        send_sems, recv_sems, local_sems = sems
        x, y, c = _mesh_pos()
        me, sibling = (x, y, c), (x, y, 1 - c)
        chips = [(1 - x, y), (x, 1 - y), (1 - x, 1 - y)]

        def slot(a, block):
            px, py, pc = block
            return outs[a].at[4 * px + 2 * py + pc]

        def copy(a, k, block, to, src=None):
            dst = slot(a, block)
            return pltpu.make_async_remote_copy(
                src_ref=dst if src is None else src, dst_ref=dst,
                send_sem=send_sems.at[a, k], recv_sem=recv_sems.at[a, k], device_id=to, device_id_type=MESH)

        def first(a):
            cps = [copy(a, 0, me, sibling, src=ins[a])]
            return cps + [copy(a, 1 + j, me, (*chip, c), src=ins[a]) for j, chip in enumerate(chips)]

        def mine(a):
            return pltpu.make_async_copy(ins[a], slot(a, me), local_sems.at[a])

        return me, sibling, chips, c, copy, first, mine

    def start(ins, outs, sems):
        _, _, _, _, _, first, mine = tools(ins, outs, sems)
        for a in range(n):
            mine(a).start()
            for cp in first(a):
                cp.start()

    def finish(ins, outs, sems):
        me, sibling, chips, c, copy, first, mine = tools(ins, outs, sems)
        passed = []
        for j, chip in enumerate(chips):
            for a in range(n):
                copy(a, 1 + j, (*chip, c), me).wait_recv()
                fwd = copy(a, 4 + j, (*chip, c), sibling)
                fwd.start()
                passed.append(fwd)
        for a in range(n):
            copy(a, 0, sibling, me).wait_recv()
            for j, chip in enumerate(chips):
                copy(a, 4 + j, (*chip, 1 - c), me).wait_recv()
        for a in range(n):
            for cp in first(a):
                cp.wait_send()
        for cp in passed:
            cp.wait_send()
        for a in range(n):
            mine(a).wait()

    return _Rider(list(arrs), [jax.ShapeDtypeStruct((N_DEV,) + a.shape, a.dtype) for a in arrs],
                  [pltpu.SemaphoreType.DMA((n, 7)), pltpu.SemaphoreType.DMA((n, 7)), pltpu.SemaphoreType.DMA((n,))],
                  start, finish)


def _sibling_rider(arrs):
    n = len(arrs)

    def copies(ins, outs, sems):
        x, y, c = _mesh_pos()
        return [pltpu.make_async_remote_copy(
            src_ref=ins[a].at[:, pl.ds(1 - c, 1)], dst_ref=outs[a],
            send_sem=sems[0].at[a], recv_sem=sems[1].at[a], device_id=(x, y, 1 - c), device_id_type=MESH)
            for a in range(n)]

    def start(ins, outs, sems):
        for cp in copies(ins, outs, sems):
            cp.start()

    def finish(ins, outs, sems):
        for cp in copies(ins, outs, sems):
            cp.wait()

    return _Rider(list(arrs), [jax.ShapeDtypeStruct((4, 1) + a.shape[2:], a.dtype) for a in arrs],
                  [pltpu.SemaphoreType.DMA((n,)), pltpu.SemaphoreType.DMA((n,))], start, finish)


def _chip_rider(arrs, ks=(1, 2, 3)):
    n = len(arrs)

    def copies(ins, outs, sems):
        x, y, c = _mesh_pos()
        cps = []
        for a in range(n):
            for s, k in enumerate(ks):
                px = x if k < 2 else 1 - x
                py = y if k == 2 else 1 - y
                cps.append(pltpu.make_async_remote_copy(
                    src_ref=ins[a].at[2 * px + py], dst_ref=outs[a].at[s],
                    send_sem=sems[0].at[a, s], recv_sem=sems[1].at[a, s],
                    device_id=(px, py, c), device_id_type=MESH))
        return cps

    def start(ins, outs, sems):
        for cp in copies(ins, outs, sems):
            cp.start()

    def finish(ins, outs, sems):
        for cp in copies(ins, outs, sems):
            cp.wait()

    return _Rider(list(arrs), [jax.ShapeDtypeStruct((len(ks),) + a.shape[1:], a.dtype) for a in arrs],
                  [pltpu.SemaphoreType.DMA((n, len(ks))), pltpu.SemaphoreType.DMA((n, len(ks)))], start, finish)


def _join_riders(riders):
    def split(seq, counts):
        out, at = [], 0
        for k in counts:
            out.append(seq[at:at + k])
            at += k
        return out

    n_in = [len(r.arrs) for r in riders]
    n_out = [len(r.out_shapes) for r in riders]
    n_sem = [len(r.sems) for r in riders]

    def run(which):
        def fn(ins, outs, sems):
            for r, i, o, s in zip(riders, split(ins, n_in), split(outs, n_out), split(sems, n_sem)):
                getattr(r, which)(i, o, s)
        return fn

    return _Rider([a for r in riders for a in r.arrs], [o for r in riders for o in r.out_shapes],
                  [s for r in riders for s in r.sems], run("start"), run("finish"))


def _run_rider(rider, name):
    n_in, n_out = len(rider.arrs), len(rider.out_shapes)

    def body(*refs):
        ins, outs, sems = refs[:n_in], refs[n_in:n_in + n_out], refs[n_in + n_out:]
        rider.start(ins, outs, sems)
        rider.finish(ins, outs, sems)

    any_spec = pl.BlockSpec(memory_space=pl.ANY)
    return pl.pallas_call(
        body, name=name, out_shape=list(rider.out_shapes), in_specs=[any_spec] * n_in, out_specs=[any_spec] * n_out,
        scratch_shapes=list(rider.sems),
    )(*rider.arrs)


def _call(body, *, name, grid, in_specs, out_specs, out_shape, args, scratch_shapes=(), rider=None):
    params = _params(("arbitrary",) * len(grid))
    if rider is None:
        return pl.pallas_call(body, name=name, grid=grid, in_specs=in_specs, out_specs=out_specs, out_shape=out_shape,
                              scratch_shapes=list(scratch_shapes), compiler_params=params)(*args)
    single = not isinstance(out_shape, (list, tuple))
    outs = [out_shape] if single else list(out_shape)
    ospecs = [out_specs] if single else list(out_specs)
    n_in, n_out, n_scr = len(in_specs), len(outs), len(scratch_shapes)
    r_in, r_out = len(rider.arrs), len(rider.out_shapes)

    def hosted(*refs):
        refs = list(refs)
        ins, rins = refs[:n_in], refs[n_in:n_in + r_in]
        refs = refs[n_in + r_in:]
        houts, routs = refs[:n_out], refs[n_out:n_out + r_out]
        refs = refs[n_out + r_out:]
        scr, rsems = refs[:n_scr], refs[n_scr:]
        ids = [pl.program_id(a) for a in range(len(grid))]
        first = functools.reduce(lambda p, q: p & q, [i == 0 for i in ids])
        last = functools.reduce(lambda p, q: p & q, [i == g - 1 for i, g in zip(ids, grid)])

        @pl.when(first)
        def _():
            rider.start(rins, routs, rsems)

        body(*ins, *houts, *scr)

        @pl.when(last)
        def _():
            rider.finish(rins, routs, rsems)

    any_spec = pl.BlockSpec(memory_space=pl.ANY)
    res = pl.pallas_call(
        hosted, name=name, grid=grid, in_specs=list(in_specs) + [any_spec] * r_in,
        out_specs=ospecs + [any_spec] * r_out, out_shape=outs + list(rider.out_shapes),
        scratch_shapes=list(scratch_shapes) + list(rider.sems), compiler_params=params,
    )(*args, *rider.arrs)
    return (res[0] if single else list(res[:n_out])), list(res[n_out:])


def _sibling_sum(part, landed, pos, name):
    _, _, rows, cols = part.shape
    tr = _tile(rows, 512)

    def body(pos_ref, p_ref, l_ref, own_ref, all_ref):
        s = p_ref[0, 0] + l_ref[0, 0]
        all_ref[0] = s.astype(BF16)

        @pl.when(pl.program_id(1) == pos_ref[1])
        def _():
            own_ref[...] = s

    return pl.pallas_call(
        body, name=name,
        grid_spec=pltpu.PrefetchScalarGridSpec(
            num_scalar_prefetch=1, grid=(rows // tr, 4),
            in_specs=[pl.BlockSpec((1, 1, tr, cols), lambda i, k, pos: (k, pos[0], i, 0)),
                      pl.BlockSpec((1, 1, tr, cols), lambda i, k, pos: (k, 0, i, 0))],
            out_specs=[pl.BlockSpec((tr, cols), lambda i, k, pos: (i, 0)),
                       pl.BlockSpec((1, tr, cols), lambda i, k, pos: (k, i, 0))]),
        out_shape=[jax.ShapeDtypeStruct((rows, cols), F32), jax.ShapeDtypeStruct((4, rows, cols), BF16)],
        compiler_params=_params(("arbitrary", "arbitrary")),
    )(pos, part, landed)


def _rms(x, g):
    r = lax.rsqrt(jnp.mean(x * x, axis=-1, keepdims=True) + EPS)
    return x * r, r


def _norm_matmul(h, g, w, *, flat, nbk, name, scale=1.0, rider=None):
    T, D = h.shape
    nb, _, bn = w.shape
    tm = _tile(T, MATMUL_TILE)

    def body(h_ref, g_ref, w_ref, o_ref, n_ref):
        @pl.when(pl.program_id(1) == 0)
        def _():
            xh, _ = _rms(h_ref[...], None)
            n_ref[...] = (xh * g_ref[...]).astype(BF16)

        n = n_ref[...]
        for k in range(nbk):
            r = _dot(n, w_ref[k])
            r = (r if scale == 1.0 else r * scale).astype(BF16)
            if flat:
                o_ref[:, k * bn:(k + 1) * bn] = r
            else:
                o_ref[k] = r

    if flat:
        out_shape = jax.ShapeDtypeStruct((T, nb * bn), BF16)
        out_spec = pl.BlockSpec((tm, nbk * bn), lambda i, j: (i, j))
    else:
        out_shape = jax.ShapeDtypeStruct((nb, T, bn), BF16)
        out_spec = pl.BlockSpec((nbk, tm, bn), lambda i, j: (j, i, 0))
    return _call(
        body, name=name, grid=(T // tm, nb // nbk),
        in_specs=[pl.BlockSpec((tm, D), lambda i, j: (i, 0)),
                  pl.BlockSpec((1, D), lambda i, j: (0, 0)),
                  pl.BlockSpec((nbk, D, bn), lambda i, j: (j, 0, 0))],
        out_specs=[out_spec, pl.BlockSpec((tm, D), lambda i, j: (i, 0))],
        out_shape=[out_shape, jax.ShapeDtypeStruct((T, D), BF16)],
        args=(h, g, w), rider=rider)


def _matmul_nt(dy, w, *, flat, nbk, name, norm=None, out_dtype=BF16, rider=None):
    nb, R, bn = w.shape
    halves = isinstance(dy, tuple)
    T = (dy[0] if halves else dy).shape[0] if flat else dy.shape[1]
    tm = _tile(T, MATMUL_TILE)
    nj = nb // nbk

    def body(*refs):
        dy_refs, refs = (refs[:2], refs[2:]) if halves else (refs[:1], refs[1:])
        if norm is None:
            w_ref, o_ref, acc_ref = refs
        else:
            w_ref, h_ref, g_ref, dres_ref, o_ref, dg_ref, acc_ref = refs
        i, j = pl.program_id(0), pl.program_id(1)

        @pl.when(j == 0)
        def _():
            acc_ref[...] = jnp.zeros_like(acc_ref)

        def accumulate(dy_ref):
            part = acc_ref[...]
            for k in range(nbk):
                d = dy_ref[:, k * bn:(k + 1) * bn] if flat else dy_ref[k]
                part = part + _dot_nt(d.astype(BF16), w_ref[k])
            acc_ref[...] = part

        if halves:
            pl.when(j < nj // 2)(lambda: accumulate(dy_refs[0]))
            pl.when(j >= nj // 2)(lambda: accumulate(dy_refs[1]))
        else:
            accumulate(dy_refs[0])

        @pl.when(j == nj - 1)
        def _():
            acc = acc_ref[...]
            if norm is None:
                o_ref[...] = acc.astype(out_dtype)
            else:
                xh, r = _rms(h_ref[...], None)

                @pl.when(i == 0)
                def _():
                    dg_ref[...] = jnp.zeros_like(dg_ref)

                dg_ref[0:1, :] += jnp.sum(acc * xh, axis=0, keepdims=True)
                dn = acc * g_ref[...]
                o_ref[...] = dres_ref[...] + r * (dn - xh * jnp.mean(dn * xh, axis=-1, keepdims=True))

    if halves:
        dy_specs = [pl.BlockSpec((tm, nbk * bn), lambda i, j: (i, jnp.minimum(j, nj // 2 - 1))),
                    pl.BlockSpec((tm, nbk * bn), lambda i, j: (i, jnp.maximum(j - nj // 2, 0)))]
        dys = tuple(dy)
    elif flat:
        dy_specs, dys = [pl.BlockSpec((tm, nbk * bn), lambda i, j: (i, j))], (dy,)
    else:
        dy_specs, dys = [pl.BlockSpec((nbk, tm, bn), lambda i, j: (j, i, 0))], (dy,)
    w_spec = pl.BlockSpec((nbk, R, bn), lambda i, j: (j, 0, 0))
    row_spec = pl.BlockSpec((tm, R), lambda i, j: (i, 0))
    if norm is None:
        in_specs, args = dy_specs + [w_spec], dys + (w,)
        out_specs = row_spec
        out_shape = jax.ShapeDtypeStruct((T, R), out_dtype)
    else:
        in_specs = dy_specs + [w_spec, row_spec, pl.BlockSpec((1, R), lambda i, j: (0, 0)), row_spec]
        args = dys + (w,) + tuple(norm)
        out_specs = [row_spec, pl.BlockSpec((8, R), lambda i, j: (0, 0))]
        out_shape = [jax.ShapeDtypeStruct((T, R), F32), jax.ShapeDtypeStruct((8, R), F32)]
    return _call(
        body, name=name, grid=(T // tm, nj), in_specs=in_specs, out_specs=out_specs, out_shape=out_shape,
        scratch_shapes=[pltpu.VMEM((tm, R), F32)], args=args, rider=rider)


def _wgrad_cols(n, dy, *, flat, nb, nbk, name, rider=None):
    T, D = n.shape
    halves = isinstance(dy, tuple)
    bn = (2 * dy[0].shape[1] if halves else dy.shape[1]) // nb if flat else (dy[0] if halves else dy).shape[2]
    tt = _tile(T, WGRAD_TILE)
    nt = T // tt
    nj = nb // nbk

    def body(*refs):
        n_ref, dy_refs, (o_ref, acc_ref) = refs[0], refs[1:-2], refs[-2:]
        j, t = pl.program_id(0), pl.program_id(1)

        @pl.when(t == 0)
        def _():
            acc_ref[...] = jnp.zeros_like(acc_ref)

        def accumulate(dy_ref):
            nv = n_ref[...]
            for k in range(nbk):
                d = dy_ref[:, k * bn:(k + 1) * bn] if flat else dy_ref[k]
                acc_ref[k] += _dot_tn(nv, d)

        if halves:
            pl.when(j < nj // 2)(lambda: accumulate(dy_refs[0]))
            pl.when(j >= nj // 2)(lambda: accumulate(dy_refs[1]))
        else:
            accumulate(dy_refs[0])

        @pl.when(t == nt - 1)
        def _():
            o_ref[...] = acc_ref[...]

    if flat and halves:
        first = pl.BlockSpec((tt, nbk * bn), lambda j, t: (jnp.where(j < nj // 2, t, nt - 1),
                                                           jnp.minimum(j, nj // 2 - 1)))
        second = pl.BlockSpec((tt, nbk * bn), lambda j, t: (jnp.where(j >= nj // 2, t, 0),
                                                            jnp.maximum(j - nj // 2, 0)))
        dy_specs, dys = [first, second], list(dy)
    elif flat:
        dy_specs, dys = [pl.BlockSpec((tt, nbk * bn), lambda j, t: (t, j))], [dy]
    elif halves:
        first = pl.BlockSpec((nbk, tt, bn), lambda j, t: (jnp.minimum(j, nj // 2 - 1),
                                                          jnp.where(j < nj // 2, t, nt - 1), 0))
        second = pl.BlockSpec((nbk, tt, bn), lambda j, t: (jnp.maximum(j - nj // 2, 0),
                                                           jnp.where(j >= nj // 2, t, 0), 0))
        dy_specs, dys = [first, second], list(dy)
    else:
        dy_specs, dys = [pl.BlockSpec((nbk, tt, bn), lambda j, t: (j, t, 0))], [dy]
    return _call(
        body, name=name, grid=(nj, nt),
        in_specs=[pl.BlockSpec((tt, D), lambda j, t: (t, 0))] + dy_specs,
        out_specs=pl.BlockSpec((nbk, D, bn), lambda j, t: (j, 0, 0)),
        out_shape=jax.ShapeDtypeStruct((nb, D, bn), F32),
        scratch_shapes=[pltpu.VMEM((nbk, D, bn), F32)], args=[n] + dys, rider=rider)


def _wgrad_rows(xa, dh, *, flat, tk, name, rider=None):
    T, D = dh.shape
    nk = xa.shape[1] // tk if flat else xa.shape[0]
    tt = _tile(T, WGRAD_TILE)
    nt = T // tt

    def body(x_ref, dh_ref, o_ref, acc_ref):
        t = pl.program_id(1)

        @pl.when(t == 0)
        def _():
            acc_ref[...] = jnp.zeros_like(acc_ref)

        xv = x_ref[...] if flat else x_ref[0]
        acc_ref[...] += _dot_tn(xv, dh_ref[...].astype(BF16))

        @pl.when(t == nt - 1)
        def _():
            o_ref[...] = acc_ref[...]

    x_spec = pl.BlockSpec((tt, tk), lambda j, t: (t, j)) if flat else pl.BlockSpec((1, tt, tk), lambda j, t: (j, t, 0))
    return _call(
        body, name=name, grid=(nk, nt),
        in_specs=[x_spec, pl.BlockSpec((tt, D), lambda j, t: (t, 0))],
        out_specs=pl.BlockSpec((tk, D), lambda j, t: (j, 0)),
        out_shape=jax.ShapeDtypeStruct((nk * tk, D), F32),
        scratch_shapes=[pltpu.VMEM((tk, D), F32)], args=(xa, dh), rider=rider)


def _matmul_residual(xa, w, res, name):
    T, K = xa.shape
    D = w.shape[1]
    tm = _tile(T, MATMUL_TILE)

    def body(x_ref, w_ref, r_ref, o_ref):
        o_ref[...] = r_ref[...] + _dot(x_ref[...], w_ref[...])

    return pl.pallas_call(
        body, name=name, grid=(T // tm,),
        in_specs=[pl.BlockSpec((tm, K), lambda i: (i, 0)), pl.BlockSpec((K, D), lambda i: (0, 0)),
                  pl.BlockSpec((tm, D), lambda i: (i, 0))],
        out_specs=pl.BlockSpec((tm, D), lambda i: (i, 0)),
        out_shape=jax.ShapeDtypeStruct((T, D), F32),
        compiler_params=_params(("arbitrary",)),
    )(xa, w, res)


def _gmlp_gate(z, ws, bst, gv, G, gd):
    D = G * gd
    u = _gelu(z[:, :D].astype(F32))
    v = _gelu(z[:, D:].astype(F32))
    vh, r = _rms(v, None)
    vn = (vh * gv).astype(BF16)
    return u, v, vh, r, vn


def _gmlp_forward(z, ws, bst, gv, w_out, x, *, name, rider=None):
    T, D2 = z.shape
    D = D2 // 2
    G = ws.shape[0]
    gd = D // G
    tb = _tile(T, 256)
    nblk = tb // GMLP_BLOCK

    def body(z_ref, ws_ref, b_ref, gv_ref, wo_ref, x_ref, gated_ref, h_ref):
        u, _, _, _, vn = _gmlp_gate(z_ref[...], None, None, gv_ref[...], G, gd)
        for n in range(nblk):
            rows = slice(n * GMLP_BLOCK, (n + 1) * GMLP_BLOCK)
            for gi in range(G):
                cols = slice(gi * gd, (gi + 1) * gd)
                s = _dot(ws_ref[gi], vn[rows, cols]) + b_ref[:, gi:gi + 1]
                gated_ref[rows, cols] = (u[rows, cols] * s).astype(BF16)
        h_ref[...] = x_ref[...] + _dot(gated_ref[...], wo_ref[...])

    return _call(
        body, name=name, grid=(T // tb,),
        in_specs=[pl.BlockSpec((tb, D2), lambda i: (i, 0)), pl.BlockSpec(ws.shape, lambda i: (0, 0, 0)),
                  pl.BlockSpec(bst.shape, lambda i: (0, 0)), pl.BlockSpec((1, D), lambda i: (0, 0)),
                  pl.BlockSpec((D, D), lambda i: (0, 0)), pl.BlockSpec((tb, D), lambda i: (i, 0))],
        out_specs=[pl.BlockSpec((tb, D), lambda i: (i, 0)), pl.BlockSpec((tb, D), lambda i: (i, 0))],
        out_shape=[jax.ShapeDtypeStruct((T, D), BF16), jax.ShapeDtypeStruct((T, D), F32)],
        args=(z, ws, bst, gv, w_out, x), rider=rider)


def _gmlp_backward(z, dgated, ws, bst, gv, mask, *, name, rider=None):
    T, D2 = z.shape
    D = D2 // 2
    G = ws.shape[0]
    gd = D // G
    tb = _tile(T, 256)
    nblk = tb // GMLP_BLOCK

    def body(z_ref, dg_ref, ws_ref, b_ref, gv_ref, mask_ref, dz_ref, dws_ref, db_ref, dgv_ref, dvn_ref):
        @pl.when(pl.program_id(0) == 0)
        def _():
            dws_ref[...] = jnp.zeros_like(dws_ref)
            db_ref[...] = jnp.zeros_like(db_ref)
            dgv_ref[...] = jnp.zeros_like(dgv_ref)

        zf = z_ref[...]
        u, v, vh, r, vn = _gmlp_gate(zf, None, None, gv_ref[...], G, gd)
        dg = dg_ref[...].astype(F32)
        for n in range(nblk):
            rows = slice(n * GMLP_BLOCK, (n + 1) * GMLP_BLOCK)
            for gi in range(G):
                cols = slice(gi * gd, (gi + 1) * gd)
                vblk = vn[rows, cols]
                s = _dot(ws_ref[gi], vblk) + b_ref[:, gi:gi + 1]
                dgb = dg[rows, cols]
                ds = dgb * u[rows, cols]
                dsb = ds.astype(BF16)
                dz_ref[rows, cols] = (dgb * s * _gelu_grad(zf[rows, cols].astype(F32))).astype(BF16)
                dvn_ref[rows, cols] = _dot_tn(ws_ref[gi], dsb)
                dws_ref[gi] += _dot_nt(dsb, vblk) * mask_ref[...]
                db_ref[:, gi:gi + 1] += jnp.sum(ds, axis=1, keepdims=True)
        dvn = dvn_ref[...]
        dgv_ref[0:1, :] += jnp.sum(dvn * vh, axis=0, keepdims=True)
        dn = dvn * gv_ref[...]
        dv = r * (dn - vh * jnp.mean(dn * vh, axis=-1, keepdims=True))
        dz_ref[:, D:] = (dv * _gelu_grad(zf[:, D:].astype(F32))).astype(BF16)

    return _call(
        body, name=name, grid=(T // tb,),
        in_specs=[pl.BlockSpec((tb, D2), lambda i: (i, 0)), pl.BlockSpec((tb, D), lambda i: (i, 0)),
                  pl.BlockSpec(ws.shape, lambda i: (0, 0, 0)), pl.BlockSpec(bst.shape, lambda i: (0, 0)),
                  pl.BlockSpec((1, D), lambda i: (0, 0)), pl.BlockSpec(mask.shape, lambda i: (0, 0))],
        out_specs=[pl.BlockSpec((tb, D2), lambda i: (i, 0)), pl.BlockSpec(ws.shape, lambda i: (0, 0, 0)),
                   pl.BlockSpec(bst.shape, lambda i: (0, 0)), pl.BlockSpec((8, D), lambda i: (0, 0))],
        out_shape=[jax.ShapeDtypeStruct((T, D2), BF16), jax.ShapeDtypeStruct(ws.shape, F32),
                   jax.ShapeDtypeStruct(bst.shape, F32), jax.ShapeDtypeStruct((8, D), F32)],
        scratch_shapes=[pltpu.VMEM((tb, D), F32)], args=(z, dgated, ws, bst, gv, mask), rider=rider)


def _shift_rows(x, k):
    return pltpu.roll(x, k % x.shape[0], axis=0)


def _conv3(ext, cw):
    return (cw[0:1] * _shift_rows(ext, 2)[8:] + cw[1:2] * _shift_rows(ext, 1)[8:] + cw[2:3] * ext[8:])


def _ffn_forward(a, cw, cb, wd, h, seq, *, name, rider=None):
    _, T, F = a.shape
    D = h.shape[1]
    tm = _tile(seq, TOKEN_TILE)
    hb = tm // 16

    def body(a_ref, ap_ref, cw_ref, cb_ref, wd_ref, h_ref, act_ref, c_ref, o_ref, acc_ref):
        i, j = pl.program_id(0), pl.program_id(1)
        keep = ((i * tm) % seq != 0).astype(F32)

        def conv(b):
            ext = jnp.concatenate([ap_ref[b, 8:16].astype(F32) * keep, a_ref[b].astype(F32)], axis=0)
            return _conv3(ext, cw_ref[b]) + cb_ref[b]

        up, gate = conv(j), conv(j + 4)
        c_ref[j] = up.astype(BF16)
        c_ref[j + 4] = gate.astype(BF16)
        act = (gate * _sigmoid(gate) * up).astype(BF16)
        act_ref[0] = act

        @pl.when(j == 0)
        def _():
            acc_ref[...] = h_ref[...]

        acc_ref[...] += _dot(act, wd_ref[0])

        @pl.when(j == 3)
        def _():
            o_ref[...] = acc_ref[...]

    return _call(
        body, name=name, grid=(T // tm, 4),
        in_specs=[pl.BlockSpec((8, tm, F), lambda i, j: (0, i, 0)),
                  pl.BlockSpec((8, 16, F), lambda i, j: (0, jnp.maximum(i * hb - 1, 0), 0)),
                  pl.BlockSpec((8, 3, F), lambda i, j: (0, 0, 0)), pl.BlockSpec((8, 1, F), lambda i, j: (0, 0, 0)),
                  pl.BlockSpec((1, F, D), lambda i, j: (j, 0, 0)), pl.BlockSpec((tm, D), lambda i, j: (i, 0))],
        out_specs=[pl.BlockSpec((1, tm, F), lambda i, j: (j, i, 0)), pl.BlockSpec((8, tm, F), lambda i, j: (0, i, 0)),
                   pl.BlockSpec((tm, D), lambda i, j: (i, 0))],
        out_shape=[jax.ShapeDtypeStruct((4, T, F), BF16), jax.ShapeDtypeStruct((8, T, F), BF16),
                   jax.ShapeDtypeStruct((T, D), F32)],
        scratch_shapes=[pltpu.VMEM((tm, D), F32)], args=(a, a, cw, cb, wd, h), rider=rider)


def _ffn_backward(dh, c, a, cw, wd, seq, *, name, rider=None):
    _, T, F = a.shape
    D = dh.shape[1]
    tm = _tile(seq, TOKEN_TILE)
    hb = tm // 16
    nt = T // tm

    def body(dh_ref, dhn_ref, cu_ref, cg_ref, cun_ref, cgn_ref, au_ref, ag_ref, cw_ref, wd_ref, da_ref, st_ref):
        i, j = pl.program_id(0), pl.program_id(1)
        keep_next = (((i + 1) * tm) % seq != 0).astype(F32)

        @pl.when((i == 0) & (j == 0))
        def _():
            st_ref[...] = jnp.zeros_like(st_ref)

        dhe = jnp.concatenate([dh_ref[...], dhn_ref[...] * keep_next], axis=0).astype(BF16)
        dact = _dot_nt(dhe, wd_ref[0])
        up = jnp.concatenate([cu_ref[0].astype(F32), cun_ref[0, 0:8].astype(F32)], axis=0)
        gate = jnp.concatenate([cg_ref[0].astype(F32), cgn_ref[0, 0:8].astype(F32)], axis=0)
        sg = _sigmoid(gate)
        gs = gate * sg
        d_up = dact * gs
        d_gate = dact * up * (sg + gs * (1.0 - sg))

        def finish(b, a_ref, dc):
            w = cw_ref[b]
            dm, u1, u2 = dc[:tm], _shift_rows(dc, -1)[:tm], _shift_rows(dc, -2)[:tm]
            da_ref[b] = (w[2:3] * dm + w[1:2] * u1 + w[0:1] * u2).astype(BF16)
            av = a_ref[0].astype(F32)
            st_ref[b, 0:1, :] += jnp.sum(u2 * av, axis=0, keepdims=True)
            st_ref[b, 1:2, :] += jnp.sum(u1 * av, axis=0, keepdims=True)
            st_ref[b, 2:3, :] += jnp.sum(dm * av, axis=0, keepdims=True)
            st_ref[b, 3:4, :] += jnp.sum(dm, axis=0, keepdims=True)

        finish(j, au_ref, d_up)
        finish(j + 4, ag_ref, d_gate)

    nxt = lambda i: jnp.minimum((i + 1) * hb, T // 16 - 1)
    return _call(
        body, name=name, grid=(nt, 4),
        in_specs=[pl.BlockSpec((tm, D), lambda i, j: (i, 0)),
                  pl.BlockSpec((8, D), lambda i, j: (jnp.minimum((i + 1) * (tm // 8), T // 8 - 1), 0)),
                  pl.BlockSpec((1, tm, F), lambda i, j: (j, i, 0)), pl.BlockSpec((1, tm, F), lambda i, j: (j + 4, i, 0)),
                  pl.BlockSpec((1, 16, F), lambda i, j: (j, nxt(i), 0)),
                  pl.BlockSpec((1, 16, F), lambda i, j: (j + 4, nxt(i), 0)),
                  pl.BlockSpec((1, tm, F), lambda i, j: (j, i, 0)), pl.BlockSpec((1, tm, F), lambda i, j: (j + 4, i, 0)),
                  pl.BlockSpec((8, 3, F), lambda i, j: (0, 0, 0)),
                  pl.BlockSpec((1, F, D), lambda i, j: (j, 0, 0))],
        out_specs=[pl.BlockSpec((8, tm, F), lambda i, j: (0, i, 0)), pl.BlockSpec((8, 8, F), lambda i, j: (0, 0, 0))],
        out_shape=[jax.ShapeDtypeStruct((8, T, F), BF16), jax.ShapeDtypeStruct((8, 8, F), F32)],
        args=(dh, dh, c, c, c, c, a, a, cw, wd), rider=rider)


def _ffn_fused_forward(h, g, w_in, cw, cb, wd, seq, *, name, rider=None):
    T, D = h.shape
    F = w_in.shape[2]
    tm = _tile(seq, TOKEN_TILE // 2)
    bps = 4
    nj = 4 // bps

    def body(h_ref, g_ref, wu_ref, wg_ref, cw_ref, cb_ref, wd_ref,
             au_ref, ag_ref, cu_ref, cg_ref, act_ref, n_ref, o_ref, acc_ref, carry_ref, *work_refs):
        eu_refs, eg_refs, stage_refs = work_refs[:bps], work_refs[bps:2 * bps], work_refs[2 * bps:]
        i, j = pl.program_id(0), pl.program_id(1)
        keep = ((i * tm) % seq != 0).astype(F32)

        @pl.when((i == 0) & (j == 0))
        def _():
            carry_ref[...] = jnp.zeros_like(carry_ref)

        @pl.when(j == 0)
        def _():
            xh, _ = _rms(h_ref[...], None)
            n_ref[...] = (xh * g_ref[...]).astype(BF16)
            acc_ref[...] = h_ref[...]

        n = n_ref[...]

        def project(b, k, w_ref, a_ref, ext_ref):
            a = _dot(n, w_ref[k]).astype(BF16)
            a_ref[k] = a
            ext_ref[0:8, :] = carry_ref[b] * keep
            ext_ref[8:, :] = a.astype(F32)
            carry_ref[b] = ext_ref[tm:tm + 8, :]

        def conv(b, ext_ref, r):
            x, w = ext_ref[r:r + ROWS + 8, :], cw_ref[b]
            return (w[0:1] * _shift_rows(x, 2) + w[1:2] * _shift_rows(x, 1) + w[2:3] * x)[8:] + cb_ref[b]

        for k in range(bps):
            project(j * bps + k, k, wu_ref, au_ref, eu_refs[k])
            project(j * bps + k + 4, k, wg_ref, ag_ref, eg_refs[k])
        outs = []
        for k in range(bps):
            for r in range(0, tm, ROWS):
                up, gate = conv(j * bps + k, eu_refs[k], r), conv(j * bps + k + 4, eg_refs[k], r)
                cu_ref[k, r:r + ROWS, :] = up.astype(BF16)
                cg_ref[k, r:r + ROWS, :] = gate.astype(BF16)
                stage_refs[k][r:r + ROWS, :] = (gate * _sigmoid(gate) * up).astype(BF16)
            act = stage_refs[k][...]
            act_ref[k] = act
            outs.append(_dot(act, wd_ref[k]))
        acc_ref[...] += functools.reduce(lambda p, q: p + q, outs)

        @pl.when(j == nj - 1)
        def _():
            o_ref[...] = acc_ref[...]

    blk = pl.BlockSpec((bps, tm, F), lambda i, j: (j, i, 0))
    row = pl.BlockSpec((tm, D), lambda i, j: (i, 0))
    half = jax.ShapeDtypeStruct((4, T, F), BF16)
    work = [pltpu.VMEM((tm + 8, F), F32)] * (2 * bps) + [pltpu.VMEM((tm, F), BF16)] * bps
    return _call(
        body, name=name, grid=(T // tm, nj),
        in_specs=[row, pl.BlockSpec((1, D), lambda i, j: (0, 0)),
                  pl.BlockSpec((bps, D, F), lambda i, j: (j, 0, 0), pipeline_mode=pl.Buffered(1)),
                  pl.BlockSpec((bps, D, F), lambda i, j: (j + nj, 0, 0), pipeline_mode=pl.Buffered(1)),
                  pl.BlockSpec((8, 3, F), lambda i, j: (0, 0, 0)), pl.BlockSpec((8, 1, F), lambda i, j: (0, 0, 0)),
                  pl.BlockSpec((bps, F, D), lambda i, j: (j, 0, 0), pipeline_mode=pl.Buffered(1))],
        out_specs=[blk, blk, blk, blk, blk, row, row],
        out_shape=[half, half, half, half, half, jax.ShapeDtypeStruct((T, D), BF16), jax.ShapeDtypeStruct((T, D), F32)],
        scratch_shapes=[pltpu.VMEM((tm, D), F32), pltpu.VMEM((8, 8, F), F32)] + work,
        args=(h, g, w_in, w_in, cw, cb, wd), rider=rider)


def _ffn_fused_backward(dh, cu, cg, au, ag, cw, wd, w_in, h, g, seq, *, name, rider=None):
    T, D = dh.shape
    F = wd.shape[1]
    tm = _tile(seq, TOKEN_TILE // 2)
    hb = tm // 16
    nt = T // tm
    bps = FFN_BLOCKS_PER_STEP
    nj = 4 // bps

    def body(dh_ref, dhn_ref, cu_ref, cg_ref, cun_ref, cgn_ref, au_ref, ag_ref, cw_ref, wd_ref, wu_ref, wg_ref,
             h_ref, g_ref, dau_ref, dag_ref, st_ref, o_ref, dg_ref, acc_ref, dact_ref, du_ref, dgt_ref):
        j, i = pl.program_id(0), pl.program_id(1)
        keep_next = (((i + 1) * tm) % seq != 0).astype(F32)
        tile_rows = pl.ds(pl.multiple_of(i * tm, tm), tm) if nj > 1 else slice(0, tm)

        @pl.when((i == 0) & (j == 0))
        def _():
            st_ref[...] = jnp.zeros_like(st_ref)
            dg_ref[...] = jnp.zeros_like(dg_ref)

        dhe = jnp.concatenate([dh_ref[...], dhn_ref[...] * keep_next], axis=0).astype(BF16)
        for k in range(bps):
            dact_ref[k] = _dot_nt(dhe, wd_ref[k])

        def conv_grads(k):
            for r in range(0, tm + 8, ROWS):
                if r < tm:
                    rows = slice(r, r + ROWS)
                    up, gate = cu_ref[k, rows, :].astype(F32), cg_ref[k, rows, :].astype(F32)
                else:
                    rows = slice(tm, tm + 8)
                    up, gate = cun_ref[k, 0:8, :].astype(F32), cgn_ref[k, 0:8, :].astype(F32)
                dact = dact_ref[k, rows, :]
                sg = _sigmoid(gate)
                gs = gate * sg
                du_ref[k, rows, :] = dact * gs
                dgt_ref[k, rows, :] = dact * up * (sg + gs * (1.0 - sg))

        def finish(b, k, a_ref, w_ref, dc_ref, da_ref):
            w = cw_ref[b]
            sums = [jnp.zeros((8, F), F32) for _ in range(4)]
            fold = lambda t: jnp.sum(t.reshape(ROWS // 8, 8, F), axis=0)
            for r in range(0, tm, ROWS):
                dc = dc_ref[k, r:r + ROWS + 8, :]
                dm, u1, u2 = dc[:ROWS], _shift_rows(dc, -1)[:ROWS], _shift_rows(dc, -2)[:ROWS]
                da_ref[k, r:r + ROWS, :] = (w[2:3] * dm + w[1:2] * u1 + w[0:1] * u2).astype(BF16)
                av = a_ref[k, r:r + ROWS, :].astype(F32)
                for s, t in enumerate((u2 * av, u1 * av, dm * av, dm)):
                    sums[s] = sums[s] + fold(t)
            for s in range(4):
                st_ref[b, s:s + 1, :] += jnp.sum(sums[s], axis=0, keepdims=True)
            return _dot_nt(da_ref[k], w_ref[k])

        dn_parts = []
        for k in range(bps):
            conv_grads(k)
            dn_parts.append(finish(j * bps + k, k, au_ref, wu_ref, du_ref, dau_ref))
            dn_parts.append(finish(j * bps + k + 4, k, ag_ref, wg_ref, dgt_ref, dag_ref))
        dn_part = functools.reduce(lambda p, q: p + q, dn_parts)

        @pl.when(j == 0)
        def _():
            acc_ref[tile_rows, :] = dn_part

        @pl.when(j > 0)
        def _():
            acc_ref[tile_rows, :] += dn_part

        @pl.when(j == nj - 1)
        def _():
            acc = acc_ref[tile_rows, :]
            xh, r = _rms(h_ref[...], None)
            dg_ref[0:1, :] += jnp.sum(acc * xh, axis=0, keepdims=True)
            dn = acc * g_ref[...]
            o_ref[...] = dh_ref[...] + r * (dn - xh * jnp.mean(dn * xh, axis=-1, keepdims=True))

    last = lambda j, i: jnp.where(j == nj - 1, i, 0)
    nxt = lambda i: jnp.minimum((i + 1) * hb, T // 16 - 1)
    blk = pl.BlockSpec((bps, tm, F), lambda j, i: (j, i, 0))
    halo = pl.BlockSpec((bps, 16, F), lambda j, i: (j, nxt(i), 0))
    row = pl.BlockSpec((tm, D), lambda j, i: (i, 0))
    work = pltpu.VMEM((bps, tm + 8, F), F32)
    half = jax.ShapeDtypeStruct((4, T, F), BF16)
    once = pl.Buffered(1) if nj == 1 else None
    return _call(
        body, name=name, grid=(nj, nt),
        in_specs=[row, pl.BlockSpec((8, D), lambda j, i: (jnp.minimum((i + 1) * (tm // 8), T // 8 - 1), 0)),
                  blk, blk, halo, halo, blk, blk,
                  pl.BlockSpec((8, 3, F), lambda j, i: (0, 0, 0)),
                  pl.BlockSpec((bps, F, D), lambda j, i: (j, 0, 0), pipeline_mode=once),
                  pl.BlockSpec((bps, D, F), lambda j, i: (j, 0, 0), pipeline_mode=once),
                  pl.BlockSpec((bps, D, F), lambda j, i: (j + nj, 0, 0), pipeline_mode=once),
                  pl.BlockSpec((tm, D), lambda j, i: (last(j, i), 0)), pl.BlockSpec((1, D), lambda j, i: (0, 0))],
        out_specs=[blk, blk, pl.BlockSpec((8, 8, F), lambda j, i: (0, 0, 0)),
                   pl.BlockSpec((tm, D), lambda j, i: (last(j, i), 0)), pl.BlockSpec((8, D), lambda j, i: (0, 0))],
        out_shape=[half, half, jax.ShapeDtypeStruct((8, 8, F), F32),
                   jax.ShapeDtypeStruct((T, D), F32), jax.ShapeDtypeStruct((8, D), F32)],
        scratch_shapes=[pltpu.VMEM((T if nj > 1 else tm, D), F32), work, work, work],
        args=(dh, dh, cu, cg, cu, cg, au, ag, cw, wd, w_in, w_in, h, g), rider=rider)


def _rel_onehot():
    r = lax.broadcasted_iota(jnp.int32, (REL_PAD, SKEW), 0)
    n = lax.broadcasted_iota(jnp.int32, (REL_PAD, SKEW), 1)
    off = jnp.where(n >= WIN, n - SKEW, n)
    idx = jnp.minimum(PAD - off, REL_CLIP) + REL_CLIP
    return (r == idx).astype(BF16)


def _skew(x, sign):
    row = lax.broadcasted_iota(jnp.int32, x.shape, 0)
    for b in range(7):
        x = jnp.where((row >> b) & 1 == 1, pltpu.roll(x, (sign * (1 << b)) % SKEW, axis=1), x)
    return x


def _bias_build(rel, name, rider=None):
    H = rel.shape[0]

    def body(rel_ref, o_ref):
        oh = _rel_onehot()
        hi, mid, lo = _split3(rel_ref[...])
        base = _dot(hi, oh) + _dot(mid, oh) + _dot(lo, oh)
        mine = lax.broadcasted_iota(jnp.int32, (H, 1), 0) == pl.program_id(0)
        row = jnp.sum(jnp.where(mine, base, 0.0), axis=0, keepdims=True)
        q = lax.broadcasted_iota(jnp.int32, (Q_TILE, WIN), 0)
        k = lax.broadcasted_iota(jnp.int32, (Q_TILE, WIN), 1)
        ok = ((q < CHUNK) & (k < WIN - CHUNK)) | ((q >= CHUNK) & (k >= CHUNK))
        t = _skew(jnp.broadcast_to(row, (Q_TILE, SKEW)), 1)
        o_ref[0] = jnp.where(ok, t[:, :WIN], NEG_INF)

    return _call(
        body, name=name, grid=(H,), in_specs=[pl.BlockSpec((H, REL_PAD), lambda h: (0, 0))],
        out_specs=pl.BlockSpec((1, Q_TILE, WIN), lambda h: (h, 0, 0)),
        out_shape=jax.ShapeDtypeStruct((H, Q_TILE, WIN), F32), args=(rel,), rider=rider)


def _bias_reduce(dbias, name):
    H = dbias.shape[0]

    def body(d_ref, o_ref, e_ref):
        oh = _rel_onehot()
        for hd in range(H):
            x = jnp.concatenate([d_ref[hd], jnp.zeros((Q_TILE, SKEW - WIN), F32)], axis=1)
            e_ref[hd:hd + 1, :] = jnp.sum(_skew(x, -1), axis=0, keepdims=True)
        hi, mid, lo = _split3(e_ref[...])
        o_ref[...] = _dot_nt(hi, oh) + _dot_nt(mid, oh) + _dot_nt(lo, oh)

    return pl.pallas_call(
        body, name=name, out_shape=jax.ShapeDtypeStruct((H, REL_PAD), F32),
        in_specs=[pl.BlockSpec(memory_space=pltpu.VMEM)], out_specs=pl.BlockSpec(memory_space=pltpu.VMEM),
        scratch_shapes=[pltpu.VMEM((H, SKEW), F32)],
        compiler_params=_params(),
    )(dbias)


def _pair_stack(xp, even):
    z = jnp.zeros_like(xp)
    return jnp.concatenate([jnp.where(even, xp, z), jnp.where(even, z, xp)], axis=0)


def _pair_merge(y, even):
    return jnp.where(even, y[:Q_TILE], y[Q_TILE:])


def _strip_probs(s_ref, b_ref, pp, r, valid, base=0):
    hb, hr = divmod(r, Q_TILE)
    s = s_ref[base + pp, r:r + STRIP, :] + b_ref[2 * pp + hb, hr:hr + STRIP, :]
    s = jnp.where(valid, s, NEG_INF)
    e = jnp.exp(s - jnp.max(s, axis=-1, keepdims=True))
    return e * (1.0 / jnp.sum(e, axis=-1, keepdims=True))


def _fill_padded(dst_ref, src_ref):
    dst_ref[0:PAD, :] = jnp.zeros((PAD, dst_ref.shape[1]), dst_ref.dtype)
    dst_ref[PAD:, :] = src_ref[...]


def _attn_specs(B, S, D, lanes):
    nt = S // (TILES_PER_STEP * Q_TILE)
    q_spec = pl.BlockSpec((TILES_PER_STEP * Q_TILE, lanes), lambda g, b, i: (b * nt + i, g))
    k_spec = pl.BlockSpec((S, lanes), lambda g, b, i: (b, g))
    v_spec = pl.BlockSpec((S, lanes), lambda g, b, i: (b, D // lanes + g))
    bias_spec = pl.BlockSpec((lanes // HEAD_DIM, Q_TILE, WIN), lambda g, b, i: (g, 0, 0))
    return nt, q_spec, k_spec, v_spec, bias_spec


def _attn_forward(q, kv, bias, S, *, name, rider=None):
    T, D = q.shape
    B = T // S
    lanes = min(FWD_HEADS_PER_STEP * HEAD_DIM, D)
    nt, q_spec, k_spec, v_spec, bias_spec = _attn_specs(B, S, D, lanes)

    npairs = lanes // (2 * HEAD_DIM)

    def body(q_ref, k_ref, v_ref, b_ref, o_ref, kp_ref, vp_ref, s_ref, p_ref):
        i = pl.program_id(2)

        @pl.when(i == 0)
        def _():
            _fill_padded(kp_ref, k_ref)
            _fill_padded(vp_ref, v_ref)

        even = lax.broadcasted_iota(jnp.int32, (1, 2 * HEAD_DIM), 1) < HEAD_DIM
        pair_cols = [slice(pp * 2 * HEAD_DIM, (pp + 1) * 2 * HEAD_DIM) for pp in range(npairs)]
        for t in range(TILES_PER_STEP):
            tile = i * TILES_PER_STEP + t
            start = pl.multiple_of(tile * Q_TILE, Q_TILE)
            rows = slice(t * Q_TILE, (t + 1) * Q_TILE)
            valid = lax.broadcasted_iota(jnp.int32, (STRIP, WIN), 1) >= PAD - tile * Q_TILE
            for pp, cols in enumerate(pair_cols):
                s_ref[t * npairs + pp] = _dot_nt(_pair_stack(q_ref[rows, cols], even), kp_ref[pl.ds(start, WIN), cols])
            for pp in range(npairs):
                for r in range(0, 2 * Q_TILE, STRIP):
                    p = _strip_probs(s_ref, b_ref, pp, r, valid, base=t * npairs)
                    p_ref[t * npairs + pp, r:r + STRIP, :] = p.astype(BF16)
            for pp, cols in enumerate(pair_cols):
                o = _dot(p_ref[t * npairs + pp], vp_ref[pl.ds(start, WIN), cols])
                o_ref[rows, cols] = _pair_merge(o, even).astype(BF16)

    nbuf = TILES_PER_STEP * npairs
    return _call(
        body, name=name, grid=(D // lanes, B, nt),
        in_specs=[q_spec, k_spec, v_spec, bias_spec], out_specs=q_spec,
        out_shape=jax.ShapeDtypeStruct((T, D), BF16),
        scratch_shapes=[pltpu.VMEM((S + PAD, lanes), BF16), pltpu.VMEM((S + PAD, lanes), BF16),
                        pltpu.VMEM((nbuf, 2 * Q_TILE, WIN), F32), pltpu.VMEM((nbuf, 2 * Q_TILE, WIN), BF16)],
        args=(q, kv, kv, bias), rider=rider)


def _attn_backward(q, kv, bias, do, S, *, name, rider=None):
    T, D = q.shape
    B = T // S
    H = D // HEAD_DIM
    lanes = min(BWD_HEADS_PER_STEP * HEAD_DIM, D)
    nt, q_spec, k_spec, v_spec, bias_spec = _attn_specs(B, S, D, lanes)
    scale = HEAD_DIM ** -0.5

    npairs = lanes // (2 * HEAD_DIM)

    def body(q_ref, k_ref, v_ref, b_ref, do_ref, dq_ref, dk_ref, dv_ref, db_ref, kp_ref, vp_ref, dka_ref, dva_ref,
             s_ref, dp_ref, p_ref, ds_ref):
        b, i = pl.program_id(1), pl.program_id(2)

        @pl.when((b == 0) & (i == 0))
        def _():
            db_ref[...] = jnp.zeros_like(db_ref)

        @pl.when(i == 0)
        def _():
            _fill_padded(kp_ref, k_ref)
            _fill_padded(vp_ref, v_ref)
            dka_ref[...] = jnp.zeros_like(dka_ref)
            dva_ref[...] = jnp.zeros_like(dva_ref)

        even = lax.broadcasted_iota(jnp.int32, (1, 2 * HEAD_DIM), 1) < HEAD_DIM
        pair_cols = [slice(pp * 2 * HEAD_DIM, (pp + 1) * 2 * HEAD_DIM) for pp in range(npairs)]
        for t in range(TILES_PER_STEP):
            tile = i * TILES_PER_STEP + t
            start = pl.multiple_of(tile * Q_TILE, Q_TILE)
            rows = slice(t * Q_TILE, (t + 1) * Q_TILE)
            valid = lax.broadcasted_iota(jnp.int32, (STRIP, WIN), 1) >= PAD - tile * Q_TILE
            base = t * npairs
            for pp, cols in enumerate(pair_cols):
                s_ref[base + pp] = _dot_nt(_pair_stack(q_ref[rows, cols], even), kp_ref[pl.ds(start, WIN), cols])
                dp_ref[base + pp] = _dot_nt(_pair_stack(do_ref[rows, cols], even), vp_ref[pl.ds(start, WIN), cols])
            for pp in range(npairs):
                for r in range(0, 2 * Q_TILE, STRIP):
                    hb, hr = divmod(r, Q_TILE)
                    p = _strip_probs(s_ref, b_ref, pp, r, valid, base=base)
                    dp = dp_ref[base + pp, r:r + STRIP, :]
                    ds = p * (dp - jnp.sum(p * dp, axis=-1, keepdims=True))
                    db_ref[2 * pp + hb, hr:hr + STRIP, :] += ds
                    p_ref[base + pp, r:r + STRIP, :] = p.astype(BF16)
                    ds_ref[base + pp, r:r + STRIP, :] = ds.astype(BF16)
            for pp, cols in enumerate(pair_cols):
                dsb = ds_ref[base + pp]
                dq = _pair_merge(_dot(dsb, kp_ref[pl.ds(start, WIN), cols]), even) * scale
                dq_ref[rows, cols] = dq.astype(BF16)
                dka_ref[pl.ds(start, WIN), cols] += _dot_tn(dsb, _pair_stack(q_ref[rows, cols], even))
                dva_ref[pl.ds(start, WIN), cols] += _dot_tn(p_ref[base + pp], _pair_stack(do_ref[rows, cols], even))

        @pl.when(i == nt - 1)
        def _():
            dk_ref[...] = dka_ref[PAD:, :].astype(BF16)
            dv_ref[...] = dva_ref[PAD:, :].astype(BF16)

    dkv_shape = jax.ShapeDtypeStruct((T, D), BF16)
    nbuf = TILES_PER_STEP * npairs
    return _call(
        body, name=name, grid=(D // lanes, B, nt),
        in_specs=[q_spec, k_spec, v_spec, bias_spec, q_spec],
        out_specs=[q_spec, k_spec, k_spec, bias_spec],
        out_shape=[jax.ShapeDtypeStruct((T, D), BF16), dkv_shape, dkv_shape,
                   jax.ShapeDtypeStruct((H, Q_TILE, WIN), F32)],
        scratch_shapes=[pltpu.VMEM((S + PAD, lanes), BF16), pltpu.VMEM((S + PAD, lanes), BF16),
                        pltpu.VMEM((S + PAD, lanes), F32), pltpu.VMEM((S + PAD, lanes), F32),
                        pltpu.VMEM((nbuf, 2 * Q_TILE, WIN), F32), pltpu.VMEM((nbuf, 2 * Q_TILE, WIN), F32),
                        pltpu.VMEM((nbuf, 2 * Q_TILE, WIN), BF16), pltpu.VMEM((nbuf, 2 * Q_TILE, WIN), BF16)],
        args=(q, kv, kv, bias, do), rider=rider)


def _loss_head(h, g, target, name):
    T, D = h.shape
    tm = _tile(T, MATMUL_TILE)

    def body(h_ref, g_ref, t_ref, dh_ref, st_ref):
        @pl.when(pl.program_id(0) == 0)
        def _():
            st_ref[...] = jnp.zeros_like(st_ref)

        xh, r = _rms(h_ref[...], None)
        err = xh * g_ref[...] - t_ref[...]
        st_ref[1:2, :] += 0.5 * jnp.sum(jnp.mean(err * err, axis=-1, keepdims=True), axis=0, keepdims=True)
        dy = err * (1.0 / D)
        st_ref[0:1, :] += jnp.sum(dy * xh, axis=0, keepdims=True)
        dn = dy * g_ref[...]
        dh_ref[...] = r * (dn - xh * jnp.mean(dn * xh, axis=-1, keepdims=True))

    row = pl.BlockSpec((tm, D), lambda i: (i, 0))
    return pl.pallas_call(
        body, name=name, grid=(T // tm,),
        in_specs=[row, pl.BlockSpec((1, D), lambda i: (0, 0)), row],
        out_specs=[row, pl.BlockSpec((8, D), lambda i: (0, 0))],
        out_shape=[jax.ShapeDtypeStruct((T, D), F32), jax.ShapeDtypeStruct((8, D), F32)],
        compiler_params=_params(("arbitrary",)),
    )(h, g, target)


def _sum_devices(arrs, name):
    n = len(arrs)

    def body(*refs):
        for a in range(n):
            s = refs[a][0].astype(F32)
            for k in range(1, N_DEV):
                s = s + refs[a][k].astype(F32)
            refs[n + a][...] = s

    vm = pl.BlockSpec(memory_space=pltpu.VMEM)
    return pl.pallas_call(
        body, name=name, out_shape=[jax.ShapeDtypeStruct(a.shape[1:], F32) for a in arrs],
        in_specs=[vm] * n, out_specs=[vm] * n, compiler_params=_params(),
    )(*arrs)


def _adamw_math(w, g, m, v):
    m = ADAM_B1 * m + (1.0 - ADAM_B1) * g
    v = ADAM_B2 * v + (1.0 - ADAM_B2) * (g * g)
    m_hat = m / (1.0 - ADAM_B1 ** ADAM_STEP)
    v_hat = v / (1.0 - ADAM_B2 ** ADAM_STEP)
    delta = -ADAM_LR * (m_hat / (jnp.sqrt(v_hat) + ADAM_EPS) + ADAM_WD * w)
    return delta, m, v


def _adamw_small(items, name):
    n = len(items)

    def body(*refs):
        for a in range(n):
            w, m, v, g = (refs[4 * a + k][...] for k in range(4))
            d, m, v = _adamw_math(w, g, m, v)
            refs[4 * n + 3 * a][...] = d
            refs[4 * n + 3 * a + 1][...] = m
            refs[4 * n + 3 * a + 2][...] = v

    vm = pl.BlockSpec(memory_space=pltpu.VMEM)
    flat = [t for it in items for t in it]
    outs = pl.pallas_call(
        body, name=name,
        out_shape=[jax.ShapeDtypeStruct(it[0].shape, F32) for it in items for _ in range(3)],
        in_specs=[vm] * (4 * n), out_specs=[vm] * (3 * n), compiler_params=_params(),
    )(*flat)
    return [tuple(outs[3 * a:3 * a + 3]) for a in range(n)]


def _adamw_big(w, m, v, owns, landeds, name, rider=None):
    L, R, C = w.shape
    tr = _tile(R, 512)
    nr = R // tr
    counts = [len(ls) for ls in landeds]

    def body(*refs):
        w_ref, m_ref, v_ref = refs[:3]
        g_ref, d_ref, mo_ref, vo_ref = refs[-4:]
        layer = pl.program_id(0)
        at = 3
        for j in range(L):
            own_ref, l_refs = refs[at], refs[at + 1:at + 1 + counts[j]]
            at += 1 + counts[j]

            @pl.when(layer == j)
            def _(own_ref=own_ref, l_refs=l_refs):
                g = own_ref[...]
                for l_ref in l_refs:
                    for k in range(l_ref.shape[0]):
                        g = g + l_ref[k].astype(F32)
                d, mn, vn = _adamw_math(w_ref[0], g, m_ref[0], v_ref[0])
                g_ref[0] = g
                d_ref[0] = d
                mo_ref[0] = mn
                vo_ref[0] = vn

    def pinned(j):
        return lambda l, i: jnp.where(l == j, i, jnp.where(l < j, 0, nr - 1))

    row = pl.BlockSpec((1, tr, C), lambda l, i: (l, i, 0))
    in_specs, args = [row, row, row], [w, m, v]
    for j in range(L):
        in_specs.append(pl.BlockSpec((tr, C), lambda l, i, p=pinned(j): (p(l, i), 0)))
        args.append(owns[j])
        for arr in landeds[j]:
            in_specs.append(pl.BlockSpec((arr.shape[0], tr, C), lambda l, i, p=pinned(j): (0, p(l, i), 0)))
            args.append(arr)
    return _call(body, name=name, grid=(L, nr), in_specs=in_specs, out_specs=[row] * 4,
                 out_shape=[jax.ShapeDtypeStruct((L, R, C), F32)] * 4, args=args, rider=rider)


def _adamw_transposed(wt, mt, vt, owns, landeds, name, rider=None):
    L, C, R = wt.shape
    tc = _tile(R, 256)
    nr = R // tc
    counts = [len(ls) for ls in landeds]

    def body(*refs):
        w_ref, m_ref, v_ref = refs[:3]
        g_ref, d_ref, mo_ref, vo_ref = refs[-4:]
        layer = pl.program_id(0)
        eye = (lax.broadcasted_iota(jnp.int32, (tc, tc), 0) == lax.broadcasted_iota(jnp.int32, (tc, tc), 1)).astype(BF16)
        at = 3
        for j in range(L):
            own_ref, l_refs = refs[at], refs[at + 1:at + 1 + counts[j]]
            at += 1 + counts[j]

            @pl.when(layer == j)
            def _(own_ref=own_ref, l_refs=l_refs):
                hi, mid, lo = _split3(own_ref[...])
                g = (_dot_tn(hi, eye) + _dot_tn(mid, eye)) + _dot_tn(lo, eye)
                for l_ref in l_refs:
                    for k in range(l_ref.shape[0]):
                        g = g + _dot_tn(l_ref[k], eye)
                d, mn, vn = _adamw_math(w_ref[0], g, m_ref[0], v_ref[0])
                g_ref[0] = g
                d_ref[0] = d
                mo_ref[0] = mn
                vo_ref[0] = vn

    def pinned(j):
        return lambda l, i: jnp.where(l == j, i, jnp.where(l < j, 0, nr - 1))

    col = pl.BlockSpec((1, C, tc), lambda l, i: (l, 0, i))
    in_specs, args = [col, col, col], [wt, mt, vt]
    for j in range(L):
        in_specs.append(pl.BlockSpec((tc, C), lambda l, i, p=pinned(j): (p(l, i), 0)))
        args.append(owns[j])
        for arr in landeds[j]:
            in_specs.append(pl.BlockSpec((arr.shape[0], tc, C), lambda l, i, p=pinned(j): (0, p(l, i), 0)))
            args.append(arr)
    return _call(body, name=name, grid=(L, nr), in_specs=in_specs, out_specs=[col] * 4,
                 out_shape=[jax.ShapeDtypeStruct((L, C, R), F32)] * 4, args=args, rider=rider)


def kernel(x, a_norm_g, a_w_in, a_v_norm_g, a_w_s, a_b_s, a_w_out, kv_norm_g, w_kv, b_norm_g, b_w_q, b_rel_bias, b_w_o, f_norm_g, f_w_in, f_conv_w, f_conv_b, f_w_down, final_norm_g, loss_target, m_a_norm_g, m_a_w_in, m_a_v_norm_g, m_a_w_s, m_a_b_s, m_a_w_out, m_kv_norm_g, m_w_kv, m_b_norm_g, m_b_w_q, m_b_rel_bias, m_b_w_o, m_f_norm_g, m_f_w_in, m_f_conv_w, m_f_conv_b, m_f_w_down, m_final_norm_g, v_a_norm_g, v_a_w_in, v_a_v_norm_g, v_a_w_s, v_a_b_s, v_a_w_out, v_kv_norm_g, v_w_kv, v_b_norm_g, v_b_w_q, v_b_rel_bias, v_b_w_o, v_f_norm_g, v_f_w_in, v_f_conv_w, v_f_conv_b, v_f_w_down, v_final_norm_g):
    B, S, D = x.shape
    T = B * S
    G = a_w_s.shape[1]
    H = D // HEAD_DIM
    F = f_w_in.shape[2]
    L = f_w_in.shape[0]
    dn = D // N_DEV
    xi, yi, ci = lax.axis_index("x"), lax.axis_index("y"), lax.axis_index("c")
    me = 4 * xi + 2 * yi + ci
    pos = jnp.stack([ci, 2 * xi + yi]).astype(jnp.int32)

    cast = lambda t: t.astype(BF16)
    gather = lambda *ts: _gather_rider(list(ts))
    rel = jnp.pad(b_rel_bias[0], ((0, 0), (0, REL_PAD - b_rel_bias.shape[2])))
    bias, (wa_in, norms_sh, conv_w0, conv_w1) = _bias_build(rel, "bias_build", rider=gather(
        cast(a_w_in[0]), jnp.concatenate([a_norm_g, a_v_norm_g], axis=0), f_conv_w[0], f_conv_w[1]))
    ga = jnp.transpose(norms_sh, (1, 0, 2)).reshape(2, D)
    g_a, g_av = ga[0:1], ga[1:2]

    x2 = x.reshape(T, D)
    tgt = loss_target.reshape(T, D)
    pc = jnp.arange(GMLP_BLOCK) // CHUNK
    mask = (pc[:, None] >= pc[None, :]).astype(F32)
    ws = (a_w_s[0] * mask[None]).astype(BF16)
    bst = jnp.transpose(a_b_s[0])
    four = lambda t: t.reshape((4, 2) + t.shape[1:])

    (z, n_a), (wa_out, w_in0) = _norm_matmul(x2, g_a, wa_in, flat=True, nbk=4, name="gmlp_in",
                                             rider=gather(cast(a_w_out[0]), cast(f_w_in[0])))
    wa_out = wa_out.reshape(D, D)
    (gated, h1), (wf_down0,) = _gmlp_forward(z, ws, bst, g_av, wa_out, x2, name="gmlp_mix",
                                             rider=gather(cast(f_w_down[0])))
    cw0, cb0, wd0 = conv_w0, f_conv_b[0].reshape(8, 1, F), wf_down0.reshape(4, F, D)
    (au0, ag0, cu0, cg0, act0, n_f0, h2), (wkv, wq, w_in1) = _ffn_fused_forward(
        h1, f_norm_g[0:1], w_in0, cw0, cb0, wd0, S, name="ffn0_fwd",
        rider=gather(cast(w_kv), cast(b_w_q[0]), cast(f_w_in[1])))
    wq = wq.reshape(D, D)
    kv, n_kv = _norm_matmul(h2, kv_norm_g.reshape(1, D), wkv, flat=True, nbk=4, name="kv_proj")
    q, n_q = _norm_matmul(h2, b_norm_g, wq.reshape(1, D, D), flat=True, nbk=1, name="q_proj", scale=HEAD_DIM ** -0.5)
    o, (wo, wf_down1) = _attn_forward(q, kv, bias, S, name="attn", rider=gather(cast(b_w_o[0]), cast(f_w_down[1])))
    wo = wo.reshape(D, D)
    cw1, cb1, wd1 = conv_w1, f_conv_b[1].reshape(8, 1, F), wf_down1.reshape(4, F, D)
    h3 = _matmul_residual(o, wo, h2, "attn_out")
    au1, ag1, cu1, cg1, act1, n_f1, h4 = _ffn_fused_forward(h3, f_norm_g[1:2], w_in1, cw1, cb1, wd1, S, name="ffn1_fwd")

    sums, from_chips = {}, {}

    def sibling_sums(names, parts, landed):
        for nm, p, l in zip(names, parts, landed):
            sums[nm] = _sibling_sum(p, l, pos, "grad_sibling_sum_" + nm)

    def chip_rider(*names):
        return _chip_rider([sums[nm][1] for nm in names])

    dh4, st_final = _loss_head(h4, final_norm_g.reshape(1, D), tgt, "loss_head")
    g_wd1 = _wgrad_rows(act1, dh4, flat=False, tk=F, name="ffn1_dwdown")
    parts = [four(g_wd1.reshape(8, F // 2, D))]
    (dau1, dag1, st_conv1, dh3, st_f1), landed = _ffn_fused_backward(
        dh4, cu1, cg1, au1, ag1, cw1, wd1, w_in1, h3, f_norm_g[1:2], S, name="ffn1_bwd", rider=_sibling_rider(parts))
    sibling_sums(["wd1"], parts, landed)
    g_win1, (from_chips["wd1"],) = _wgrad_cols(n_f1, (dau1, dag1), flat=False, nb=8, nbk=2, name="ffn1_dwin",
                                               rider=chip_rider("wd1"))
    d_o = _matmul_nt(dh3, wo.reshape(1, D, D), flat=True, nbk=1, name="attn_out_dx")
    parts = [four(g_win1)]
    g_wo, landed = _wgrad_rows(o, dh3, flat=True, tk=_tile(D, 512), name="attn_out_dw", rider=_sibling_rider(parts))
    sibling_sums(["win1"], parts, landed)
    (dq, dk, dv, dbias), (from_chips["win1"],) = _attn_backward(
        q, kv, bias, d_o, S, name="attn_bwd", rider=chip_rider("win1"))
    g_rel = _bias_reduce(dbias, "bias_reduce")
    g_wq = _wgrad_cols(n_q, dq, flat=True, nb=1, nbk=1, name="q_dw")
    dh2, st_b = _matmul_nt(dq, wq.reshape(1, D, D), flat=True, nbk=1, name="q_dx", norm=(h2, b_norm_g, dh3))
    g_wkv = _wgrad_cols(n_kv, (dk, dv), flat=True, nb=8, nbk=4, name="kv_dw")
    parts = [four(g_wo.reshape(8, dn, D)), four(g_wq.reshape(8, dn, D)), four(g_wkv)]
    (dh2, st_kv), landed = _matmul_nt((dk, dv), wkv, flat=True, nbk=4, name="kv_dx",
                                      norm=(h2, kv_norm_g.reshape(1, D), dh2), rider=_sibling_rider(parts))
    sibling_sums(["wo", "wq", "wkv"], parts, landed)
    g_wd0, (from_chips["wo"], from_chips["wq"], from_chips["wkv"]) = _wgrad_rows(
        act0, dh2, flat=False, tk=F, name="ffn0_dwdown", rider=chip_rider("wo", "wq", "wkv"))
    parts = [four(g_wd0.reshape(8, F // 2, D))]
    (dau0, dag0, st_conv0, dh1, st_f0), landed = _ffn_fused_backward(
        dh2, cu0, cg0, au0, ag0, cw0, wd0, w_in0, h1, f_norm_g[0:1], S, name="ffn0_bwd", rider=_sibling_rider(parts))
    sibling_sums(["wd0"], parts, landed)
    g_win0, (ce,) = _wgrad_cols(n_f0, (dau0, dag0), flat=False, nb=8, nbk=2, name="ffn0_dwin",
                                rider=chip_rider("wd0"))
    from_chips["wd0"] = [ce]
    dgated = _matmul_nt(dh1, wa_out.reshape(1, D, D), flat=True, nbk=1, name="gmlp_out_dx")
    parts = [four(g_win0)]
    g_wa_out, landed = _wgrad_rows(gated, dh1, flat=True, tk=_tile(D, 512), name="gmlp_out_dw",
                                   rider=_sibling_rider(parts))
    sibling_sums(["win0"], parts, landed)
    parts = [four(g_wa_out.reshape(8, dn, D))]
    (dz, g_ws, g_bst, st_av), (ce_win0_a, landed) = _gmlp_backward(
        z, dgated, ws, bst, g_av, mask, name="gmlp_bwd",
        rider=_join_riders([_chip_rider([sums["win0"][1]], ks=(1, 2)), _sibling_rider(parts)]))
    sibling_sums(["wa_out"], parts, [landed])
    g_wa_in, (ce_win0_b,) = _wgrad_cols(n_a, dz, flat=True, nb=8, nbk=4, name="gmlp_in_dw",
                                        rider=_chip_rider([sums["win0"][1]], ks=(3,)))
    from_chips["win0"] = [ce_win0_a, ce_win0_b]
    vec = jnp.concatenate([st_av[0:1], st_kv[0:1], st_b[0:1], st_f0[0:1], st_f1[0:1], st_final[0:3]], axis=0)
    parts = [four(g_wa_in)]
    (grad_x, st_a), got = _matmul_nt(dz, wa_in, flat=True, nbk=4, name="gmlp_in_dx", norm=(x2, g_a, dh1),
                                     rider=_join_riders([_sibling_rider(parts), chip_rider("wa_out"), gather(
                                         vec, cast(g_ws), cast(g_bst), cast(g_rel), cast(st_conv0), cast(st_conv1))]))
    sibling_sums(["wa_in"], parts, got[0:1])
    from_chips["wa_out"], small = [got[1]], got[2:]

    def big_update(names, w, m, v, rider=None):
        shape = w.shape
        r = lambda t: t.reshape((len(names), -1, shape[-1]))
        as_list = lambda t: t if isinstance(t, list) else [t]
        outs = _adamw_big(r(w), r(m), r(v), [sums[nm][0] for nm in names],
                          [as_list(from_chips[nm]) for nm in names], "adamw_" + names[0], rider=rider)
        outs, got = (outs, None) if rider is None else outs
        return [t.reshape(shape) for t in outs], got

    u_f_w_down, (ce, st_a) = big_update(["wd0", "wd1"], f_w_down, m_f_w_down, v_f_w_down,
                                        rider=_join_riders([chip_rider("wa_in"), gather(st_a)]))
    from_chips["wa_in"] = [ce]
    names = ["win0", "win1"]
    tr = lambda t: jnp.swapaxes(t, 1, 2)
    outs = _adamw_transposed(
        tr(f_w_in), tr(m_f_w_in), tr(v_f_w_in), [sums[nm][0] for nm in names],
        [t if isinstance(t, list) else [t] for t in (from_chips[nm] for nm in names)], "adamw_win0")
    u_f_w_in = [tr(t) for t in outs]
    u_a_w_in, _ = big_update(["wa_in"], a_w_in, m_a_w_in, v_a_w_in)
    u_w_kv, _ = big_update(["wkv"], w_kv, m_w_kv, v_w_kv)
    u_a_w_out, _ = big_update(["wa_out"], a_w_out, m_a_w_out, v_a_w_out)
    u_b_w_q, _ = big_update(["wq"], b_w_q, m_b_w_q, v_b_w_q)
    u_b_w_o, _ = big_update(["wo"], b_w_o, m_b_w_o, v_b_w_o)

    vec, g_ws, g_bst, g_rel, st_conv0, st_conv1, st_a = _sum_devices(list(small) + [st_a], "sum_small_grads")
    vec = jnp.concatenate([st_a[0:1], vec[0:7]], axis=0)
    loss = vec[7, 0]
    g_a_norm = lax.dynamic_slice_in_dim(vec[0:1], me * dn, dn, axis=1)
    g_av_norm = lax.dynamic_slice_in_dim(vec[1:2], me * dn, dn, axis=1)
    st_conv = jnp.stack([st_conv0, st_conv1])
    g_conv_w = lax.dynamic_index_in_dim(st_conv, me, axis=1, keepdims=False)[:, 0:3]
    g_conv_b = st_conv[:, :, 3, :].reshape(L, 8 * F)
    small_items = [
        (a_norm_g, m_a_norm_g, v_a_norm_g, g_a_norm),
        (a_v_norm_g, m_a_v_norm_g, v_a_v_norm_g, g_av_norm),
        (a_w_s, m_a_w_s, v_a_w_s, g_ws[None]),
        (a_b_s, m_a_b_s, v_a_b_s, jnp.transpose(g_bst)[None]),
        (kv_norm_g.reshape(1, D), m_kv_norm_g.reshape(1, D), v_kv_norm_g.reshape(1, D), vec[2:3]),
        (b_norm_g, m_b_norm_g, v_b_norm_g, vec[3:4]),
        (b_rel_bias, m_b_rel_bias, v_b_rel_bias, g_rel[None, :, :b_rel_bias.shape[2]]),
        (f_norm_g, m_f_norm_g, v_f_norm_g, vec[4:6]),
        (f_conv_w, m_f_conv_w, v_f_conv_w, g_conv_w),
        (f_conv_b, m_f_conv_b, v_f_conv_b, g_conv_b),
        (final_norm_g.reshape(1, D), m_final_norm_g.reshape(1, D), v_final_norm_g.reshape(1, D), vec[6:7]),
    ]
    small_out = _adamw_small(small_items, "adamw_small")
    (u_a_norm, u_av_norm, u_ws, u_bs, u_kvn, u_bn, u_rel, u_fn, u_cw, u_cb, u_fin) = [
        (it[3],) + so for it, so in zip(small_items, small_out)]
    vecD = lambda u: tuple(t.reshape(D) for t in u)
    u_kvn, u_fin = vecD(u_kvn), vecD(u_fin)

    order = [u_a_norm, u_a_w_in, u_av_norm, u_ws, u_bs, u_a_w_out, u_kvn, u_w_kv, u_bn, u_b_w_q, u_rel, u_b_w_o,
             u_fn, u_f_w_in, u_cw, u_cb, u_f_w_down, u_fin]
    outs = [loss, grad_x.reshape(B, S, D)]
    for k in range(4):
        outs += [u[k] for u in order]
    return tuple(outs)
```

```python
import functools

import jax
import jax.numpy as jnp
from jax import lax
from jax.experimental import pallas as pl
from jax.experimental.pallas import tpu as pltpu

F32 = jnp.float32
BF16 = jnp.bfloat16
MESH = pl.DeviceIdType.MESH

N_DEV = 8
EPS = 1e-6
NEG_INF = -1e30
CHUNK = 64
LEFT_CHUNKS = 8
REL_CLIP = 128
HEAD_DIM = 64
GMLP_BLOCK = 128
Q_TILE = 2 * CHUNK
PAD = LEFT_CHUNKS * CHUNK
WIN = PAD + Q_TILE
SKEW = WIN + Q_TILE
REL_PAD = 384
FWD_HEADS_PER_STEP = 8
BWD_HEADS_PER_STEP = 4
TILES_PER_STEP = 4
STRIP = 32
ROWS = 32
ADAM_LR, ADAM_B1, ADAM_B2, ADAM_EPS, ADAM_WD, ADAM_STEP = 0.001, 0.9, 0.999, 1e-08, 0.01, 10
VMEM_LIMIT = 60 * 1024 * 1024
TOKEN_TILE = 512
MATMUL_TILE = 1024
WGRAD_TILE = 2048
FFN_BLOCKS_PER_STEP = 4


def _params(sem=None):
    return pltpu.CompilerParams(dimension_semantics=sem, vmem_limit_bytes=VMEM_LIMIT)


def _tile(n, pref):
    if n <= pref:
        return n
    for t in range(pref - pref % 8, 7, -8):
        if n % t == 0:
            return t
    return n


def _gelu(x):
    return 0.5 * x * (1.0 + jnp.tanh(0.7978845608028654 * (x + 0.044715 * x * x * x)))


def _gelu_grad(x):
    t = jnp.tanh(0.7978845608028654 * (x + 0.044715 * x * x * x))
    return 0.5 * (1.0 + t) + 0.5 * x * (1.0 - t * t) * 0.7978845608028654 * (1.0 + 3 * 0.044715 * x * x)


def _sigmoid(x):
    return 1.0 / (1.0 + jnp.exp(-x))


def _dot(a, b):
    return jnp.dot(a, b, preferred_element_type=F32)


def _dot_nt(a, b):
    return lax.dot_general(a, b, (((1,), (1,)), ((), ())), preferred_element_type=F32)


def _dot_tn(a, b):
    return lax.dot_general(a, b, (((0,), (0,)), ((), ())), preferred_element_type=F32)


def _split3(x):
    hi = x.astype(BF16)
    r1 = x - hi.astype(F32)
    mid = r1.astype(BF16)
    lo = (r1 - mid.astype(F32)).astype(BF16)
    return hi, mid, lo


def _mesh_pos():
    return lax.axis_index("x"), lax.axis_index("y"), lax.axis_index("c")


class _Rider:
    def __init__(self, arrs, out_shapes, sems, start, finish):
        self.arrs, self.out_shapes, self.sems, self.start, self.finish = arrs, out_shapes, sems, start, finish


def _gather_rider(arrs):
    n = len(arrs)

    def tools(ins, outs, sems):
        send_sems, recv_sems, local_sems = sems
        x, y, c = _mesh_pos()
        me, sibling = (x, y, c), (x, y, 1 - c)
        chips = [(1 - x, y), (x, 1 - y), (1 - x, 1 - y)]

        def slot(a, block):
            px, py, pc = block
            return outs[a].at[4 * px + 2 * py + pc]

        def copy(a, k, block, to, src=None):
            dst = slot(a, block)
            return pltpu.make_async_remote_copy(
                src_ref=dst if src is None else src, dst_ref=dst,
                send_sem=send_sems.at[a, k], recv_sem=recv_sems.at[a, k], device_id=to, device_id_type=MESH)

        def first(a):
            cps = [copy(a, 0, me, sibling, src=ins[a])]
            return cps + [copy(a, 1 + j, me, (*chip, c), src=ins[a]) for j, chip in enumerate(chips)]

        def mine(a):
            return pltpu.make_async_copy(ins[a], slot(a, me), local_sems.at[a])

        return me, sibling, chips, c, copy, first, mine

    def start(ins, outs, sems):
        _, _, _, _, _, first, mine = tools(ins, outs, sems)
        for a in range(n):
            mine(a).start()
            for cp in first(a):
                cp.start()

    def finish(ins, outs, sems):
        me, sibling, chips, c, copy, first, mine = tools(ins, outs, sems)
        passed = []
        for j, chip in enumerate(chips):
            for a in range(n):
                copy(a, 1 + j, (*chip, c), me).wait_recv()
                fwd = copy(a, 4 + j, (*chip, c), sibling)
                fwd.start()
                passed.append(fwd)
        for a in range(n):
            copy(a, 0, sibling, me).wait_recv()
            for j, chip in enumerate(chips):
                copy(a, 4 + j, (*chip, 1 - c), me).wait_recv()
        for a in range(n):
            for cp in first(a):
                cp.wait_send()
        for cp in passed:
            cp.wait_send()
        for a in range(n):
            mine(a).wait()

    return _Rider(list(arrs), [jax.ShapeDtypeStruct((N_DEV,) + a.shape, a.dtype) for a in arrs],
                  [pltpu.SemaphoreType.DMA((n, 7)), pltpu.SemaphoreType.DMA((n, 7)), pltpu.SemaphoreType.DMA((n,))],
                  start, finish)


def _sibling_rider(arrs):
    n = len(arrs)

    def copies(ins, outs, sems):
        x, y, c = _mesh_pos()
        return [pltpu.make_async_remote_copy(
            src_ref=ins[a].at[:, pl.ds(1 - c, 1)], dst_ref=outs[a],
            send_sem=sems[0].at[a], recv_sem=sems[1].at[a], device_id=(x, y, 1 - c), device_id_type=MESH)
            for a in range(n)]

    def start(ins, outs, sems):
        for cp in copies(ins, outs, sems):
            cp.start()

    def finish(ins, outs, sems):
        for cp in copies(ins, outs, sems):
            cp.wait()

    return _Rider(list(arrs), [jax.ShapeDtypeStruct((4, 1) + a.shape[2:], a.dtype) for a in arrs],
                  [pltpu.SemaphoreType.DMA((n,)), pltpu.SemaphoreType.DMA((n,))], start, finish)


def _chip_rider(arrs, ks=(1, 2, 3)):
    n = len(arrs)

    def copies(ins, outs, sems):
        x, y, c = _mesh_pos()
        cps = []
        for a in range(n):
            for s, k in enumerate(ks):
                px = x if k < 2 else 1 - x
                py = y if k == 2 else 1 - y
                cps.append(pltpu.make_async_remote_copy(
                    src_ref=ins[a].at[2 * px + py], dst_ref=outs[a].at[s],
                    send_sem=sems[0].at[a, s], recv_sem=sems[1].at[a, s],
                    device_id=(px, py, c), device_id_type=MESH))
        return cps

    def start(ins, outs, sems):
        for cp in copies(ins, outs, sems):
            cp.start()

    def finish(ins, outs, sems):
        for cp in copies(ins, outs, sems):
            cp.wait()

    return _Rider(list(arrs), [jax.ShapeDtypeStruct((len(ks),) + a.shape[1:], a.dtype) for a in arrs],
                  [pltpu.SemaphoreType.DMA((n, len(ks))), pltpu.SemaphoreType.DMA((n, len(ks)))], start, finish)


def _join_riders(riders):
    def split(seq, counts):
        out, at = [], 0
        for k in counts:
            out.append(seq[at:at + k])
            at += k
        return out

    n_in = [len(r.arrs) for r in riders]
    n_out = [len(r.out_shapes) for r in riders]
    n_sem = [len(r.sems) for r in riders]

    def run(which):
        def fn(ins, outs, sems):
            for r, i, o, s in zip(riders, split(ins, n_in), split(outs, n_out), split(sems, n_sem)):
                getattr(r, which)(i, o, s)
        return fn

    return _Rider([a for r in riders for a in r.arrs], [o for r in riders for o in r.out_shapes],
                  [s for r in riders for s in r.sems], run("start"), run("finish"))


def _run_rider(rider, name):
    n_in, n_out = len(rider.arrs), len(rider.out_shapes)

    def body(*refs):
        ins, outs, sems = refs[:n_in], refs[n_in:n_in + n_out], refs[n_in + n_out:]
        rider.start(ins, outs, sems)
        rider.finish(ins, outs, sems)

    any_spec = pl.BlockSpec(memory_space=pl.ANY)
    return pl.pallas_call(
        body, name=name, out_shape=list(rider.out_shapes), in_specs=[any_spec] * n_in, out_specs=[any_spec] * n_out,
        scratch_shapes=list(rider.sems),
    )(*rider.arrs)


def _call(body, *, name, grid, in_specs, out_specs, out_shape, args, scratch_shapes=(), rider=None):
    params = _params(("arbitrary",) * len(grid))
    if rider is None:
        return pl.pallas_call(body, name=name, grid=grid, in_specs=in_specs, out_specs=out_specs, out_shape=out_shape,
                              scratch_shapes=list(scratch_shapes), compiler_params=params)(*args)
    single = not isinstance(out_shape, (list, tuple))
    outs = [out_shape] if single else list(out_shape)
    ospecs = [out_specs] if single else list(out_specs)
    n_in, n_out, n_scr = len(in_specs), len(outs), len(scratch_shapes)
    r_in, r_out = len(rider.arrs), len(rider.out_shapes)

    def hosted(*refs):
        refs = list(refs)
        ins, rins = refs[:n_in], refs[n_in:n_in + r_in]
        refs = refs[n_in + r_in:]
        houts, routs = refs[:n_out], refs[n_out:n_out + r_out]
        refs = refs[n_out + r_out:]
        scr, rsems = refs[:n_scr], refs[n_scr:]
        ids = [pl.program_id(a) for a in range(len(grid))]
        first = functools.reduce(lambda p, q: p & q, [i == 0 for i in ids])
        last = functools.reduce(lambda p, q: p & q, [i == g - 1 for i, g in zip(ids, grid)])

        @pl.when(first)
        def _():
            rider.start(rins, routs, rsems)

        body(*ins, *houts, *scr)

        @pl.when(last)
        def _():
            rider.finish(rins, routs, rsems)

    any_spec = pl.BlockSpec(memory_space=pl.ANY)
    res = pl.pallas_call(
        hosted, name=name, grid=grid, in_specs=list(in_specs) + [any_spec] * r_in,
        out_specs=ospecs + [any_spec] * r_out, out_shape=outs + list(rider.out_shapes),
        scratch_shapes=list(scratch_shapes) + list(rider.sems), compiler_params=params,
    )(*args, *rider.arrs)
    return (res[0] if single else list(res[:n_out])), list(res[n_out:])


def _sibling_sum(part, landed, pos, name):
    _, _, rows, cols = part.shape
    tr = _tile(rows, 512)

    def body(pos_ref, p_ref, l_ref, own_ref, all_ref):
        s = p_ref[0, 0] + l_ref[0, 0]
        all_ref[0] = s.astype(BF16)

        @pl.when(pl.program_id(1) == pos_ref[1])
        def _():
            own_ref[...] = s

    return pl.pallas_call(
        body, name=name,
        grid_spec=pltpu.PrefetchScalarGridSpec(
            num_scalar_prefetch=1, grid=(rows // tr, 4),
            in_specs=[pl.BlockSpec((1, 1, tr, cols), lambda i, k, pos: (k, pos[0], i, 0)),
                      pl.BlockSpec((1, 1, tr, cols), lambda i, k, pos: (k, 0, i, 0))],
            out_specs=[pl.BlockSpec((tr, cols), lambda i, k, pos: (i, 0)),
                       pl.BlockSpec((1, tr, cols), lambda i, k, pos: (k, i, 0))]),
        out_shape=[jax.ShapeDtypeStruct((rows, cols), F32), jax.ShapeDtypeStruct((4, rows, cols), BF16)],
        compiler_params=_params(("arbitrary", "arbitrary")),
    )(pos, part, landed)


def _rms(x, g):
    r = lax.rsqrt(jnp.mean(x * x, axis=-1, keepdims=True) + EPS)
    return x * r, r


def _norm_matmul(h, g, w, *, flat, nbk, name, scale=1.0, rider=None):
    T, D = h.shape
    nb, _, bn = w.shape
    tm = _tile(T, MATMUL_TILE)

    def body(h_ref, g_ref, w_ref, o_ref, n_ref):
        @pl.when(pl.program_id(1) == 0)
        def _():
            xh, _ = _rms(h_ref[...], None)
            n_ref[...] = (xh * g_ref[...]).astype(BF16)

        n = n_ref[...]
        for k in range(nbk):
            r = _dot(n, w_ref[k])
            r = (r if scale == 1.0 else r * scale).astype(BF16)
            if flat:
                o_ref[:, k * bn:(k + 1) * bn] = r
            else:
                o_ref[k] = r

    if flat:
        out_shape = jax.ShapeDtypeStruct((T, nb * bn), BF16)
        out_spec = pl.BlockSpec((tm, nbk * bn), lambda i, j: (i, j))
    else:
        out_shape = jax.ShapeDtypeStruct((nb, T, bn), BF16)
        out_spec = pl.BlockSpec((nbk, tm, bn), lambda i, j: (j, i, 0))
    return _call(
        body, name=name, grid=(T // tm, nb // nbk),
        in_specs=[pl.BlockSpec((tm, D), lambda i, j: (i, 0)),
                  pl.BlockSpec((1, D), lambda i, j: (0, 0)),
                  pl.BlockSpec((nbk, D, bn), lambda i, j: (j, 0, 0))],
        out_specs=[out_spec, pl.BlockSpec((tm, D), lambda i, j: (i, 0))],
        out_shape=[out_shape, jax.ShapeDtypeStruct((T, D), BF16)],
        args=(h, g, w), rider=rider)


def _matmul_nt(dy, w, *, flat, nbk, name, norm=None, out_dtype=BF16, rider=None):
    nb, R, bn = w.shape
    halves = isinstance(dy, tuple)
    T = (dy[0] if halves else dy).shape[0] if flat else dy.shape[1]
    tm = _tile(T, MATMUL_TILE)
    nj = nb // nbk

    def body(*refs):
        dy_refs, refs = (refs[:2], refs[2:]) if halves else (refs[:1], refs[1:])
        if norm is None:
            w_ref, o_ref, acc_ref = refs
        else:
            w_ref, h_ref, g_ref, dres_ref, o_ref, dg_ref, acc_ref = refs
        i, j = pl.program_id(0), pl.program_id(1)

        @pl.when(j == 0)
        def _():
            acc_ref[...] = jnp.zeros_like(acc_ref)

        def accumulate(dy_ref):
            part = acc_ref[...]
            for k in range(nbk):
                d = dy_ref[:, k * bn:(k + 1) * bn] if flat else dy_ref[k]
                part = part + _dot_nt(d.astype(BF16), w_ref[k])
            acc_ref[...] = part

        if halves:
            pl.when(j < nj // 2)(lambda: accumulate(dy_refs[0]))
            pl.when(j >= nj // 2)(lambda: accumulate(dy_refs[1]))
        else:
            accumulate(dy_refs[0])

        @pl.when(j == nj - 1)
        def _():
            acc = acc_ref[...]
            if norm is None:
                o_ref[...] = acc.astype(out_dtype)
            else:
                xh, r = _rms(h_ref[...], None)

                @pl.when(i == 0)
                def _():
                    dg_ref[...] = jnp.zeros_like(dg_ref)

                dg_ref[0:1, :] += jnp.sum(acc * xh, axis=0, keepdims=True)
                dn = acc * g_ref[...]
                o_ref[...] = dres_ref[...] + r * (dn - xh * jnp.mean(dn * xh, axis=-1, keepdims=True))

    if halves:
        dy_specs = [pl.BlockSpec((tm, nbk * bn), lambda i, j: (i, jnp.minimum(j, nj // 2 - 1))),
                    pl.BlockSpec((tm, nbk * bn), lambda i, j: (i, jnp.maximum(j - nj // 2, 0)))]
        dys = tuple(dy)
    elif flat:
        dy_specs, dys = [pl.BlockSpec((tm, nbk * bn), lambda i, j: (i, j))], (dy,)
    else:
        dy_specs, dys = [pl.BlockSpec((nbk, tm, bn), lambda i, j: (j, i, 0))], (dy,)
    w_spec = pl.BlockSpec((nbk, R, bn), lambda i, j: (j, 0, 0))
    row_spec = pl.BlockSpec((tm, R), lambda i, j: (i, 0))
    if norm is None:
        in_specs, args = dy_specs + [w_spec], dys + (w,)
        out_specs = row_spec
        out_shape = jax.ShapeDtypeStruct((T, R), out_dtype)
    else:
        in_specs = dy_specs + [w_spec, row_spec, pl.BlockSpec((1, R), lambda i, j: (0, 0)), row_spec]
        args = dys + (w,) + tuple(norm)
        out_specs = [row_spec, pl.BlockSpec((8, R), lambda i, j: (0, 0))]
        out_shape = [jax.ShapeDtypeStruct((T, R), F32), jax.ShapeDtypeStruct((8, R), F32)]
    return _call(
        body, name=name, grid=(T // tm, nj), in_specs=in_specs, out_specs=out_specs, out_shape=out_shape,
        scratch_shapes=[pltpu.VMEM((tm, R), F32)], args=args, rider=rider)


def _wgrad_cols(n, dy, *, flat, nb, nbk, name, rider=None):
    T, D = n.shape
    halves = isinstance(dy, tuple)
    bn = (2 * dy[0].shape[1] if halves else dy.shape[1]) // nb if flat else (dy[0] if halves else dy).shape[2]
    tt = _tile(T, WGRAD_TILE)
    nt = T // tt
    nj = nb // nbk

    def body(*refs):
        n_ref, dy_refs, (o_ref, acc_ref) = refs[0], refs[1:-2], refs[-2:]
        j, t = pl.program_id(0), pl.program_id(1)

        @pl.when(t == 0)
        def _():
            acc_ref[...] = jnp.zeros_like(acc_ref)

        def accumulate(dy_ref):
            nv = n_ref[...]
            for k in range(nbk):
                d = dy_ref[:, k * bn:(k + 1) * bn] if flat else dy_ref[k]
                acc_ref[k] += _dot_tn(nv, d)

        if halves:
            pl.when(j < nj // 2)(lambda: accumulate(dy_refs[0]))
            pl.when(j >= nj // 2)(lambda: accumulate(dy_refs[1]))
        else:
            accumulate(dy_refs[0])

        @pl.when(t == nt - 1)
        def _():
            o_ref[...] = acc_ref[...]

    if flat and halves:
        first = pl.BlockSpec((tt, nbk * bn), lambda j, t: (jnp.where(j < nj // 2, t, nt - 1),
                                                           jnp.minimum(j, nj // 2 - 1)))
        second = pl.BlockSpec((tt, nbk * bn), lambda j, t: (jnp.where(j >= nj // 2, t, 0),
                                                            jnp.maximum(j - nj // 2, 0)))
        dy_specs, dys = [first, second], list(dy)
    elif flat:
        dy_specs, dys = [pl.BlockSpec((tt, nbk * bn), lambda j, t: (t, j))], [dy]
    elif halves:
        first = pl.BlockSpec((nbk, tt, bn), lambda j, t: (jnp.minimum(j, nj // 2 - 1),
                                                          jnp.where(j < nj // 2, t, nt - 1), 0))
        second = pl.BlockSpec((nbk, tt, bn), lambda j, t: (jnp.maximum(j - nj // 2, 0),
                                                           jnp.where(j >= nj // 2, t, 0), 0))
        dy_specs, dys = [first, second], list(dy)
    else:
        dy_specs, dys = [pl.BlockSpec((nbk, tt, bn), lambda j, t: (j, t, 0))], [dy]
    return _call(
        body, name=name, grid=(nj, nt),
        in_specs=[pl.BlockSpec((tt, D), lambda j, t: (t, 0))] + dy_specs,
        out_specs=pl.BlockSpec((nbk, D, bn), lambda j, t: (j, 0, 0)),
        out_shape=jax.ShapeDtypeStruct((nb, D, bn), F32),
        scratch_shapes=[pltpu.VMEM((nbk, D, bn), F32)], args=[n] + dys, rider=rider)


def _wgrad_rows(xa, dh, *, flat, tk, name, rider=None):
    T, D = dh.shape
    nk = xa.shape[1] // tk if flat else xa.shape[0]
    tt = _tile(T, WGRAD_TILE)
    nt = T // tt

    def body(x_ref, dh_ref, o_ref, acc_ref):
        t = pl.program_id(1)

        @pl.when(t == 0)
        def _():
            acc_ref[...] = jnp.zeros_like(acc_ref)

        xv = x_ref[...] if flat else x_ref[0]
        acc_ref[...] += _dot_tn(xv, dh_ref[...].astype(BF16))

        @pl.when(t == nt - 1)
        def _():
            o_ref[...] = acc_ref[...]

    x_spec = pl.BlockSpec((tt, tk), lambda j, t: (t, j)) if flat else pl.BlockSpec((1, tt, tk), lambda j, t: (j, t, 0))
    return _call(
        body, name=name, grid=(nk, nt),
        in_specs=[x_spec, pl.BlockSpec((tt, D), lambda j, t: (t, 0))],
        out_specs=pl.BlockSpec((tk, D), lambda j, t: (j, 0)),
        out_shape=jax.ShapeDtypeStruct((nk * tk, D), F32),
        scratch_shapes=[pltpu.VMEM((tk, D), F32)], args=(xa, dh), rider=rider)


def _matmul_residual(xa, w, res, name):
    T, K = xa.shape
    D = w.shape[1]
    tm = _tile(T, MATMUL_TILE)

    def body(x_ref, w_ref, r_ref, o_ref):
        o_ref[...] = r_ref[...] + _dot(x_ref[...], w_ref[...])

    return pl.pallas_call(
        body, name=name, grid=(T // tm,),
        in_specs=[pl.BlockSpec((tm, K), lambda i: (i, 0)), pl.BlockSpec((K, D), lambda i: (0, 0)),
                  pl.BlockSpec((tm, D), lambda i: (i, 0))],
        out_specs=pl.BlockSpec((tm, D), lambda i: (i, 0)),
        out_shape=jax.ShapeDtypeStruct((T, D), F32),
        compiler_params=_params(("arbitrary",)),
    )(xa, w, res)


def _gmlp_gate(z, ws, bst, gv, G, gd):
    D = G * gd
    u = _gelu(z[:, :D].astype(F32))
    v = _gelu(z[:, D:].astype(F32))
    vh, r = _rms(v, None)
    vn = (vh * gv).astype(BF16)
    return u, v, vh, r, vn


def _gmlp_forward(z, ws, bst, gv, w_out, x, *, name, rider=None):
    T, D2 = z.shape
    D = D2 // 2
    G = ws.shape[0]
    gd = D // G
    tb = _tile(T, 256)
    nblk = tb // GMLP_BLOCK

    def body(z_ref, ws_ref, b_ref, gv_ref, wo_ref, x_ref, gated_ref, h_ref):
        u, _, _, _, vn = _gmlp_gate(z_ref[...], None, None, gv_ref[...], G, gd)
        for n in range(nblk):
            rows = slice(n * GMLP_BLOCK, (n + 1) * GMLP_BLOCK)
            for gi in range(G):
                cols = slice(gi * gd, (gi + 1) * gd)
                s = _dot(ws_ref[gi], vn[rows, cols]) + b_ref[:, gi:gi + 1]
                gated_ref[rows, cols] = (u[rows, cols] * s).astype(BF16)
        h_ref[...] = x_ref[...] + _dot(gated_ref[...], wo_ref[...])

    return _call(
        body, name=name, grid=(T // tb,),
        in_specs=[pl.BlockSpec((tb, D2), lambda i: (i, 0)), pl.BlockSpec(ws.shape, lambda i: (0, 0, 0)),
                  pl.BlockSpec(bst.shape, lambda i: (0, 0)), pl.BlockSpec((1, D), lambda i: (0, 0)),
                  pl.BlockSpec((D, D), lambda i: (0, 0)), pl.BlockSpec((tb, D), lambda i: (i, 0))],
        out_specs=[pl.BlockSpec((tb, D), lambda i: (i, 0)), pl.BlockSpec((tb, D), lambda i: (i, 0))],
        out_shape=[jax.ShapeDtypeStruct((T, D), BF16), jax.ShapeDtypeStruct((T, D), F32)],
        args=(z, ws, bst, gv, w_out, x), rider=rider)


def _gmlp_backward(z, dgated, ws, bst, gv, mask, *, name, rider=None):
    T, D2 = z.shape
    D = D2 // 2
    G = ws.shape[0]
    gd = D // G
    tb = _tile(T, 256)
    nblk = tb // GMLP_BLOCK

    def body(z_ref, dg_ref, ws_ref, b_ref, gv_ref, mask_ref, dz_ref, dws_ref, db_ref, dgv_ref, dvn_ref):
        @pl.when(pl.program_id(0) == 0)
        def _():
            dws_ref[...] = jnp.zeros_like(dws_ref)
            db_ref[...] = jnp.zeros_like(db_ref)
            dgv_ref[...] = jnp.zeros_like(dgv_ref)

        zf = z_ref[...]
        u, v, vh, r, vn = _gmlp_gate(zf, None, None, gv_ref[...], G, gd)
        dg = dg_ref[...].astype(F32)
        for n in range(nblk):
            rows = slice(n * GMLP_BLOCK, (n + 1) * GMLP_BLOCK)
            for gi in range(G):
                cols = slice(gi * gd, (gi + 1) * gd)
                vblk = vn[rows, cols]
                s = _dot(ws_ref[gi], vblk) + b_ref[:, gi:gi + 1]
                dgb = dg[rows, cols]
                ds = dgb * u[rows, cols]
                dsb = ds.astype(BF16)
                dz_ref[rows, cols] = (dgb * s * _gelu_grad(zf[rows, cols].astype(F32))).astype(BF16)
                dvn_ref[rows, cols] = _dot_tn(ws_ref[gi], dsb)
                dws_ref[gi] += _dot_nt(dsb, vblk) * mask_ref[...]
                db_ref[:, gi:gi + 1] += jnp.sum(ds, axis=1, keepdims=True)
        dvn = dvn_ref[...]
        dgv_ref[0:1, :] += jnp.sum(dvn * vh, axis=0, keepdims=True)
        dn = dvn * gv_ref[...]
        dv = r * (dn - vh * jnp.mean(dn * vh, axis=-1, keepdims=True))
        dz_ref[:, D:] = (dv * _gelu_grad(zf[:, D:].astype(F32))).astype(BF16)

    return _call(
        body, name=name, grid=(T // tb,),
        in_specs=[pl.BlockSpec((tb, D2), lambda i: (i, 0)), pl.BlockSpec((tb, D), lambda i: (i, 0)),
                  pl.BlockSpec(ws.shape, lambda i: (0, 0, 0)), pl.BlockSpec(bst.shape, lambda i: (0, 0)),
                  pl.BlockSpec((1, D), lambda i: (0, 0)), pl.BlockSpec(mask.shape, lambda i: (0, 0))],
        out_specs=[pl.BlockSpec((tb, D2), lambda i: (i, 0)), pl.BlockSpec(ws.shape, lambda i: (0, 0, 0)),
                   pl.BlockSpec(bst.shape, lambda i: (0, 0)), pl.BlockSpec((8, D), lambda i: (0, 0))],
        out_shape=[jax.ShapeDtypeStruct((T, D2), BF16), jax.ShapeDtypeStruct(ws.shape, F32),
                   jax.ShapeDtypeStruct(bst.shape, F32), jax.ShapeDtypeStruct((8, D), F32)],
        scratch_shapes=[pltpu.VMEM((tb, D), F32)], args=(z, dgated, ws, bst, gv, mask), rider=rider)


def _shift_rows(x, k):
    return pltpu.roll(x, k % x.shape[0], axis=0)


def _ffn_fused_forward(h, g, w_in, cw, cb, wd, seq, *, name, rider=None, loss=None):
    T, D = h.shape
    F = w_in.shape[2]
    tm = _tile(seq, TOKEN_TILE // 2)
    bps = 4
    nj = 4 // bps

    def body(*refs):
        refs = list(refs)
        (h_ref, g_ref, wu_ref, wg_ref, cw_ref, cb_ref, wd_ref), refs = refs[:7], refs[7:]
        if loss is not None:
            (gf_ref, t_ref), refs = refs[:2], refs[2:]
        (au_ref, ag_ref, cu_ref, cg_ref, act_ref, n_ref, o_ref), refs = refs[:7], refs[7:]
        if loss is not None:
            st_ref, refs = refs[0], refs[1:]
        (acc_ref, carry_ref), work_refs = refs[:2], refs[2:]
        eu_refs, eg_refs, stage_refs = work_refs[:bps], work_refs[bps:2 * bps], work_refs[2 * bps:]
        i, j = pl.program_id(0), pl.program_id(1)
        keep = ((i * tm) % seq != 0).astype(F32)

        @pl.when((i == 0) & (j == 0))
        def _():
            carry_ref[...] = jnp.zeros_like(carry_ref)
            if loss is not None:
                st_ref[...] = jnp.zeros_like(st_ref)

        @pl.when(j == 0)
        def _():
            xh, _ = _rms(h_ref[...], None)
            n_ref[...] = (xh * g_ref[...]).astype(BF16)
            acc_ref[...] = h_ref[...]

        n = n_ref[...]

        def project(b, k, w_ref, a_ref, ext_ref):
            a = _dot(n, w_ref[k]).astype(BF16)
            a_ref[k] = a
            ext_ref[0:8, :] = carry_ref[b] * keep
            ext_ref[8:, :] = a.astype(F32)
            carry_ref[b] = ext_ref[tm:tm + 8, :]

        def conv(b, ext_ref, r):
            x, w = ext_ref[r:r + ROWS + 8, :], cw_ref[b]
            return (w[0:1] * _shift_rows(x, 2) + w[1:2] * _shift_rows(x, 1) + w[2:3] * x)[8:] + cb_ref[b]

        for k in range(bps):
            project(j * bps + k, k, wu_ref, au_ref, eu_refs[k])
            project(j * bps + k + 4, k, wg_ref, ag_ref, eg_refs[k])
        outs = []
        for k in range(bps):
            for r in range(0, tm, ROWS):
                up, gate = conv(j * bps + k, eu_refs[k], r), conv(j * bps + k + 4, eg_refs[k], r)
                cu_ref[k, r:r + ROWS, :] = up.astype(BF16)
                cg_ref[k, r:r + ROWS, :] = gate.astype(BF16)
                stage_refs[k][r:r + ROWS, :] = (gate * _sigmoid(gate) * up).astype(BF16)
            act = stage_refs[k][...]
            act_ref[k] = act
            outs.append(_dot(act, wd_ref[k]))
        acc_ref[...] += functools.reduce(lambda p, q: p + q, outs)

        @pl.when(j == nj - 1)
        def _():
            if loss is None:
                o_ref[...] = acc_ref[...]
            else:
                xh, r = _rms(acc_ref[...], None)
                err = xh * gf_ref[...] - t_ref[...]
                st_ref[1:2, :] += 0.5 * jnp.sum(jnp.mean(err * err, axis=-1, keepdims=True), axis=0, keepdims=True)
                dy = err * (1.0 / D)
                st_ref[0:1, :] += jnp.sum(dy * xh, axis=0, keepdims=True)
                dn = dy * gf_ref[...]
                o_ref[...] = r * (dn - xh * jnp.mean(dn * xh, axis=-1, keepdims=True))

    blk = pl.BlockSpec((bps, tm, F), lambda i, j: (j, i, 0))
    row = pl.BlockSpec((tm, D), lambda i, j: (i, 0))
    vec = pl.BlockSpec((1, D), lambda i, j: (0, 0))
    half = jax.ShapeDtypeStruct((4, T, F), BF16)
    work = [pltpu.VMEM((tm + 8, F), F32)] * (2 * bps) + [pltpu.VMEM((tm, F), BF16)] * bps
    tail = loss is not None
    return _call(
        body, name=name, grid=(T // tm, nj),
        in_specs=[row, vec,
                  pl.BlockSpec((bps, D, F), lambda i, j: (j, 0, 0), pipeline_mode=pl.Buffered(1)),
                  pl.BlockSpec((bps, D, F), lambda i, j: (j + nj, 0, 0), pipeline_mode=pl.Buffered(1)),
                  pl.BlockSpec((8, 3, F), lambda i, j: (0, 0, 0)), pl.BlockSpec((8, 1, F), lambda i, j: (0, 0, 0)),
                  pl.BlockSpec((bps, F, D), lambda i, j: (j, 0, 0), pipeline_mode=pl.Buffered(1))] + [vec, row] * tail,
        out_specs=[blk, blk, blk, blk, blk, row, row] + [pl.BlockSpec((8, D), lambda i, j: (0, 0))] * tail,
        out_shape=[half, half, half, half, half, jax.ShapeDtypeStruct((T, D), BF16), jax.ShapeDtypeStruct((T, D), F32)]
        + [jax.ShapeDtypeStruct((8, D), F32)] * tail,
        scratch_shapes=[pltpu.VMEM((tm, D), F32), pltpu.VMEM((8, 8, F), F32)] + work,
        args=(h, g, w_in, w_in, cw, cb, wd) + (tuple(loss) if tail else ()), rider=rider)


def _ffn_fused_backward(dh, cu, cg, au, ag, cw, wd, w_in, h, g, seq, *, name, rider=None):
    T, D = dh.shape
    F = wd.shape[1]
    tm = _tile(seq, TOKEN_TILE // 2)
    hb = tm // 16
    nt = T // tm
    bps = FFN_BLOCKS_PER_STEP
    nj = 4 // bps

    def body(dh_ref, dhn_ref, cu_ref, cg_ref, cun_ref, cgn_ref, au_ref, ag_ref, cw_ref, wd_ref, wu_ref, wg_ref,
             h_ref, g_ref, dau_ref, dag_ref, st_ref, o_ref, dg_ref, acc_ref, dact_ref, du_ref, dgt_ref):
        j, i = pl.program_id(0), pl.program_id(1)
        keep_next = (((i + 1) * tm) % seq != 0).astype(F32)
        tile_rows = pl.ds(pl.multiple_of(i * tm, tm), tm) if nj > 1 else slice(0, tm)

        @pl.when((i == 0) & (j == 0))
        def _():
            st_ref[...] = jnp.zeros_like(st_ref)
            dg_ref[...] = jnp.zeros_like(dg_ref)

        dhe = jnp.concatenate([dh_ref[...], dhn_ref[...] * keep_next], axis=0).astype(BF16)
        for k in range(bps):
            dact_ref[k] = _dot_nt(dhe, wd_ref[k])

        def conv_grads(k):
            for r in range(0, tm + 8, ROWS):
                if r < tm:
                    rows = slice(r, r + ROWS)
                    up, gate = cu_ref[k, rows, :].astype(F32), cg_ref[k, rows, :].astype(F32)
                else:
                    rows = slice(tm, tm + 8)
                    up, gate = cun_ref[k, 0:8, :].astype(F32), cgn_ref[k, 0:8, :].astype(F32)
                dact = dact_ref[k, rows, :]
                sg = _sigmoid(gate)
                gs = gate * sg
                du_ref[k, rows, :] = dact * gs
                dgt_ref[k, rows, :] = dact * up * (sg + gs * (1.0 - sg))

        def finish(b, k, a_ref, w_ref, dc_ref, da_ref):
            w = cw_ref[b]
            sums = [jnp.zeros((8, F), F32) for _ in range(4)]
            fold = lambda t: jnp.sum(t.reshape(ROWS // 8, 8, F), axis=0)
            for r in range(0, tm, ROWS):
                dc = dc_ref[k, r:r + ROWS + 8, :]
                dm, u1, u2 = dc[:ROWS], _shift_rows(dc, -1)[:ROWS], _shift_rows(dc, -2)[:ROWS]
                da_ref[k, r:r + ROWS, :] = (w[2:3] * dm + w[1:2] * u1 + w[0:1] * u2).astype(BF16)
                av = a_ref[k, r:r + ROWS, :].astype(F32)
                for s, t in enumerate((u2 * av, u1 * av, dm * av, dm)):
                    sums[s] = sums[s] + fold(t)
            for s in range(4):
                st_ref[b, s:s + 1, :] += jnp.sum(sums[s], axis=0, keepdims=True)
            return _dot_nt(da_ref[k], w_ref[k])

        dn_parts = []
        for k in range(bps):
            conv_grads(k)
            dn_parts.append(finish(j * bps + k, k, au_ref, wu_ref, du_ref, dau_ref))
            dn_parts.append(finish(j * bps + k + 4, k, ag_ref, wg_ref, dgt_ref, dag_ref))
        dn_part = functools.reduce(lambda p, q: p + q, dn_parts)

        @pl.when(j == 0)
        def _():
            acc_ref[tile_rows, :] = dn_part

        @pl.when(j > 0)
        def _():
            acc_ref[tile_rows, :] += dn_part

        @pl.when(j == nj - 1)
        def _():
            acc = acc_ref[tile_rows, :]
            xh, r = _rms(h_ref[...], None)
            dg_ref[0:1, :] += jnp.sum(acc * xh, axis=0, keepdims=True)
            dn = acc * g_ref[...]
            o_ref[...] = dh_ref[...] + r * (dn - xh * jnp.mean(dn * xh, axis=-1, keepdims=True))

    last = lambda j, i: jnp.where(j == nj - 1, i, 0)
    nxt = lambda i: jnp.minimum((i + 1) * hb, T // 16 - 1)
    blk = pl.BlockSpec((bps, tm, F), lambda j, i: (j, i, 0))
    halo = pl.BlockSpec((bps, 16, F), lambda j, i: (j, nxt(i), 0))
    row = pl.BlockSpec((tm, D), lambda j, i: (i, 0))
    work = pltpu.VMEM((bps, tm + 8, F), F32)
    half = jax.ShapeDtypeStruct((4, T, F), BF16)
    once = pl.Buffered(1) if nj == 1 else None
    return _call(
        body, name=name, grid=(nj, nt),
        in_specs=[row, pl.BlockSpec((8, D), lambda j, i: (jnp.minimum((i + 1) * (tm // 8), T // 8 - 1), 0)),
                  blk, blk, halo, halo, blk, blk,
                  pl.BlockSpec((8, 3, F), lambda j, i: (0, 0, 0)),
                  pl.BlockSpec((bps, F, D), lambda j, i: (j, 0, 0), pipeline_mode=once),
                  pl.BlockSpec((bps, D, F), lambda j, i: (j, 0, 0), pipeline_mode=once),
                  pl.BlockSpec((bps, D, F), lambda j, i: (j + nj, 0, 0), pipeline_mode=once),
                  pl.BlockSpec((tm, D), lambda j, i: (last(j, i), 0)), pl.BlockSpec((1, D), lambda j, i: (0, 0))],
        out_specs=[blk, blk, pl.BlockSpec((8, 8, F), lambda j, i: (0, 0, 0)),
                   pl.BlockSpec((tm, D), lambda j, i: (last(j, i), 0)), pl.BlockSpec((8, D), lambda j, i: (0, 0))],
        out_shape=[half, half, jax.ShapeDtypeStruct((8, 8, F), F32),
                   jax.ShapeDtypeStruct((T, D), F32), jax.ShapeDtypeStruct((8, D), F32)],
        scratch_shapes=[pltpu.VMEM((T if nj > 1 else tm, D), F32), work, work, work],
        args=(dh, dh, cu, cg, cu, cg, au, ag, cw, wd, w_in, w_in, h, g), rider=rider)


def _rel_onehot():
    r = lax.broadcasted_iota(jnp.int32, (REL_PAD, SKEW), 0)
    n = lax.broadcasted_iota(jnp.int32, (REL_PAD, SKEW), 1)
    off = jnp.where(n >= WIN, n - SKEW, n)
    idx = jnp.minimum(PAD - off, REL_CLIP) + REL_CLIP
    return (r == idx).astype(BF16)


def _skew(x, sign):
    row = lax.broadcasted_iota(jnp.int32, x.shape, 0)
    for b in range(7):
        x = jnp.where((row >> b) & 1 == 1, pltpu.roll(x, (sign * (1 << b)) % SKEW, axis=1), x)
    return x


def _bias_build(rel, name, rider=None):
    H = rel.shape[0]

    def body(rel_ref, o_ref):
        oh = _rel_onehot()
        hi, mid, lo = _split3(rel_ref[...])
        base = _dot(hi, oh) + _dot(mid, oh) + _dot(lo, oh)
        mine = lax.broadcasted_iota(jnp.int32, (H, 1), 0) == pl.program_id(0)
        row = jnp.sum(jnp.where(mine, base, 0.0), axis=0, keepdims=True)
        q = lax.broadcasted_iota(jnp.int32, (Q_TILE, WIN), 0)
        k = lax.broadcasted_iota(jnp.int32, (Q_TILE, WIN), 1)
        ok = ((q < CHUNK) & (k < WIN - CHUNK)) | ((q >= CHUNK) & (k >= CHUNK))
        t = _skew(jnp.broadcast_to(row, (Q_TILE, SKEW)), 1)
        o_ref[0] = jnp.where(ok, t[:, :WIN], NEG_INF)

    return _call(
        body, name=name, grid=(H,), in_specs=[pl.BlockSpec((H, REL_PAD), lambda h: (0, 0))],
        out_specs=pl.BlockSpec((1, Q_TILE, WIN), lambda h: (h, 0, 0)),
        out_shape=jax.ShapeDtypeStruct((H, Q_TILE, WIN), F32), args=(rel,), rider=rider)


def _bias_reduce(dbias, name):
    H = dbias.shape[0]

    def body(d_ref, o_ref, e_ref):
        oh = _rel_onehot()
        for hd in range(H):
            x = jnp.concatenate([d_ref[hd], jnp.zeros((Q_TILE, SKEW - WIN), F32)], axis=1)
            e_ref[hd:hd + 1, :] = jnp.sum(_skew(x, -1), axis=0, keepdims=True)
        hi, mid, lo = _split3(e_ref[...])
        o_ref[...] = _dot_nt(hi, oh) + _dot_nt(mid, oh) + _dot_nt(lo, oh)

    return pl.pallas_call(
        body, name=name, out_shape=jax.ShapeDtypeStruct((H, REL_PAD), F32),
        in_specs=[pl.BlockSpec(memory_space=pltpu.VMEM)], out_specs=pl.BlockSpec(memory_space=pltpu.VMEM),
        scratch_shapes=[pltpu.VMEM((H, SKEW), F32)],
        compiler_params=_params(),
    )(dbias)


def _pair_stack(xp, even):
    z = jnp.zeros_like(xp)
    return jnp.concatenate([jnp.where(even, xp, z), jnp.where(even, z, xp)], axis=0)


def _pair_merge(y, even):
    return jnp.where(even, y[:Q_TILE], y[Q_TILE:])


def _strip_probs(s_ref, b_ref, pp, r, valid, base=0):
    hb, hr = divmod(r, Q_TILE)
    s = s_ref[base + pp, r:r + STRIP, :] + b_ref[2 * pp + hb, hr:hr + STRIP, :]
    s = jnp.where(valid, s, NEG_INF)
    e = jnp.exp(s - jnp.max(s, axis=-1, keepdims=True))
    return e * (1.0 / jnp.sum(e, axis=-1, keepdims=True))


def _fill_padded(dst_ref, src_ref):
    dst_ref[0:PAD, :] = jnp.zeros((PAD, dst_ref.shape[1]), dst_ref.dtype)
    dst_ref[PAD:, :] = src_ref[...]


def _attn_specs(B, S, D, lanes):
    nt = S // (TILES_PER_STEP * Q_TILE)
    q_spec = pl.BlockSpec((TILES_PER_STEP * Q_TILE, lanes), lambda g, b, i: (b * nt + i, g))
    k_spec = pl.BlockSpec((S, lanes), lambda g, b, i: (b, g))
    v_spec = pl.BlockSpec((S, lanes), lambda g, b, i: (b, D // lanes + g))
    bias_spec = pl.BlockSpec((lanes // HEAD_DIM, Q_TILE, WIN), lambda g, b, i: (g, 0, 0))
    return nt, q_spec, k_spec, v_spec, bias_spec


def _attn_forward(q, kv, bias, S, *, name, rider=None):
    T, D = q.shape
    B = T // S
    lanes = min(FWD_HEADS_PER_STEP * HEAD_DIM, D)
    nt, q_spec, k_spec, v_spec, bias_spec = _attn_specs(B, S, D, lanes)

    npairs = lanes // (2 * HEAD_DIM)

    def body(q_ref, k_ref, v_ref, b_ref, o_ref, kp_ref, vp_ref, s_ref, p_ref):
        i = pl.program_id(2)

        @pl.when(i == 0)
        def _():
            _fill_padded(kp_ref, k_ref)
            _fill_padded(vp_ref, v_ref)

        even = lax.broadcasted_iota(jnp.int32, (1, 2 * HEAD_DIM), 1) < HEAD_DIM
        pair_cols = [slice(pp * 2 * HEAD_DIM, (pp + 1) * 2 * HEAD_DIM) for pp in range(npairs)]
        for t in range(TILES_PER_STEP):
            tile = i * TILES_PER_STEP + t
            start = pl.multiple_of(tile * Q_TILE, Q_TILE)
            rows = slice(t * Q_TILE, (t + 1) * Q_TILE)
            valid = lax.broadcasted_iota(jnp.int32, (STRIP, WIN), 1) >= PAD - tile * Q_TILE
            for pp, cols in enumerate(pair_cols):
                s_ref[t * npairs + pp] = _dot_nt(_pair_stack(q_ref[rows, cols], even), kp_ref[pl.ds(start, WIN), cols])
            for pp in range(npairs):
                for r in range(0, 2 * Q_TILE, STRIP):
                    p = _strip_probs(s_ref, b_ref, pp, r, valid, base=t * npairs)
                    p_ref[t * npairs + pp, r:r + STRIP, :] = p.astype(BF16)
            for pp, cols in enumerate(pair_cols):
                o = _dot(p_ref[t * npairs + pp], vp_ref[pl.ds(start, WIN), cols])
                o_ref[rows, cols] = _pair_merge(o, even).astype(BF16)

    nbuf = TILES_PER_STEP * npairs
    return _call(
        body, name=name, grid=(D // lanes, B, nt),
        in_specs=[q_spec, k_spec, v_spec, bias_spec], out_specs=q_spec,
        out_shape=jax.ShapeDtypeStruct((T, D), BF16),
        scratch_shapes=[pltpu.VMEM((S + PAD, lanes), BF16), pltpu.VMEM((S + PAD, lanes), BF16),
                        pltpu.VMEM((nbuf, 2 * Q_TILE, WIN), F32), pltpu.VMEM((nbuf, 2 * Q_TILE, WIN), BF16)],
        args=(q, kv, kv, bias), rider=rider)


def _attn_backward(q, kv, bias, do, S, *, name, rider=None):
    T, D = q.shape
    B = T // S
    H = D // HEAD_DIM
    lanes = min(BWD_HEADS_PER_STEP * HEAD_DIM, D)
    nt, q_spec, k_spec, v_spec, bias_spec = _attn_specs(B, S, D, lanes)
    scale = HEAD_DIM ** -0.5

    npairs = lanes // (2 * HEAD_DIM)

    def body(q_ref, k_ref, v_ref, b_ref, do_ref, dq_ref, dk_ref, dv_ref, db_ref, kp_ref, vp_ref, dka_ref, dva_ref,
             s_ref, dp_ref, p_ref, ds_ref):
        b, i = pl.program_id(1), pl.program_id(2)

        @pl.when((b == 0) & (i == 0))
        def _():
            db_ref[...] = jnp.zeros_like(db_ref)

        @pl.when(i == 0)
        def _():
            _fill_padded(kp_ref, k_ref)
            _fill_padded(vp_ref, v_ref)
            dka_ref[...] = jnp.zeros_like(dka_ref)
            dva_ref[...] = jnp.zeros_like(dva_ref)

        even = lax.broadcasted_iota(jnp.int32, (1, 2 * HEAD_DIM), 1) < HEAD_DIM
        pair_cols = [slice(pp * 2 * HEAD_DIM, (pp + 1) * 2 * HEAD_DIM) for pp in range(npairs)]
        for t in range(TILES_PER_STEP):
            tile = i * TILES_PER_STEP + t
            start = pl.multiple_of(tile * Q_TILE, Q_TILE)
            rows = slice(t * Q_TILE, (t + 1) * Q_TILE)
            valid = lax.broadcasted_iota(jnp.int32, (STRIP, WIN), 1) >= PAD - tile * Q_TILE
            base = t * npairs
            for pp, cols in enumerate(pair_cols):
                s_ref[base + pp] = _dot_nt(_pair_stack(q_ref[rows, cols], even), kp_ref[pl.ds(start, WIN), cols])
                dp_ref[base + pp] = _dot_nt(_pair_stack(do_ref[rows, cols], even), vp_ref[pl.ds(start, WIN), cols])
            for pp in range(npairs):
                for r in range(0, 2 * Q_TILE, STRIP):
                    hb, hr = divmod(r, Q_TILE)
                    p = _strip_probs(s_ref, b_ref, pp, r, valid, base=base)
                    dp = dp_ref[base + pp, r:r + STRIP, :]
                    ds = p * (dp - jnp.sum(p * dp, axis=-1, keepdims=True))
                    db_ref[2 * pp + hb, hr:hr + STRIP, :] += ds
                    p_ref[base + pp, r:r + STRIP, :] = p.astype(BF16)
                    ds_ref[base + pp, r:r + STRIP, :] = ds.astype(BF16)
            for pp, cols in enumerate(pair_cols):
                dsb = ds_ref[base + pp]
                dq = _pair_merge(_dot(dsb, kp_ref[pl.ds(start, WIN), cols]), even) * scale
                dq_ref[rows, cols] = dq.astype(BF16)
                dka_ref[pl.ds(start, WIN), cols] += _dot_tn(dsb, _pair_stack(q_ref[rows, cols], even))
                dva_ref[pl.ds(start, WIN), cols] += _dot_tn(p_ref[base + pp], _pair_stack(do_ref[rows, cols], even))

        @pl.when(i == nt - 1)
        def _():
            dk_ref[...] = dka_ref[PAD:, :].astype(BF16)
            dv_ref[...] = dva_ref[PAD:, :].astype(BF16)

    dkv_shape = jax.ShapeDtypeStruct((T, D), BF16)
    nbuf = TILES_PER_STEP * npairs
    return _call(
        body, name=name, grid=(D // lanes, B, nt),
        in_specs=[q_spec, k_spec, v_spec, bias_spec, q_spec],
        out_specs=[q_spec, k_spec, k_spec, bias_spec],
        out_shape=[jax.ShapeDtypeStruct((T, D), BF16), dkv_shape, dkv_shape,
                   jax.ShapeDtypeStruct((H, Q_TILE, WIN), F32)],
        scratch_shapes=[pltpu.VMEM((S + PAD, lanes), BF16), pltpu.VMEM((S + PAD, lanes), BF16),
                        pltpu.VMEM((S + PAD, lanes), F32), pltpu.VMEM((S + PAD, lanes), F32),
                        pltpu.VMEM((nbuf, 2 * Q_TILE, WIN), F32), pltpu.VMEM((nbuf, 2 * Q_TILE, WIN), F32),
                        pltpu.VMEM((nbuf, 2 * Q_TILE, WIN), BF16), pltpu.VMEM((nbuf, 2 * Q_TILE, WIN), BF16)],
        args=(q, kv, kv, bias, do), rider=rider)


def _loss_head(h, g, target, name):
    T, D = h.shape
    tm = _tile(T, MATMUL_TILE)

    def body(h_ref, g_ref, t_ref, dh_ref, st_ref):
        @pl.when(pl.program_id(0) == 0)
        def _():
            st_ref[...] = jnp.zeros_like(st_ref)

        xh, r = _rms(h_ref[...], None)
        err = xh * g_ref[...] - t_ref[...]
        st_ref[1:2, :] += 0.5 * jnp.sum(jnp.mean(err * err, axis=-1, keepdims=True), axis=0, keepdims=True)
        dy = err * (1.0 / D)
        st_ref[0:1, :] += jnp.sum(dy * xh, axis=0, keepdims=True)
        dn = dy * g_ref[...]
        dh_ref[...] = r * (dn - xh * jnp.mean(dn * xh, axis=-1, keepdims=True))

    row = pl.BlockSpec((tm, D), lambda i: (i, 0))
    return pl.pallas_call(
        body, name=name, grid=(T // tm,),
        in_specs=[row, pl.BlockSpec((1, D), lambda i: (0, 0)), row],
        out_specs=[row, pl.BlockSpec((8, D), lambda i: (0, 0))],
        out_shape=[jax.ShapeDtypeStruct((T, D), F32), jax.ShapeDtypeStruct((8, D), F32)],
        compiler_params=_params(("arbitrary",)),
    )(h, g, target)


def _sum_devices(arrs, name):
    n = len(arrs)

    def body(*refs):
        for a in range(n):
            s = refs[a][0].astype(F32)
            for k in range(1, N_DEV):
                s = s + refs[a][k].astype(F32)
            refs[n + a][...] = s

    vm = pl.BlockSpec(memory_space=pltpu.VMEM)
    return pl.pallas_call(
        body, name=name, out_shape=[jax.ShapeDtypeStruct(a.shape[1:], F32) for a in arrs],
        in_specs=[vm] * n, out_specs=[vm] * n, compiler_params=_params(),
    )(*arrs)


def _adamw_math(w, g, m, v):
    m = ADAM_B1 * m + (1.0 - ADAM_B1) * g
    v = ADAM_B2 * v + (1.0 - ADAM_B2) * (g * g)
    m_hat = m / (1.0 - ADAM_B1 ** ADAM_STEP)
    v_hat = v / (1.0 - ADAM_B2 ** ADAM_STEP)
    delta = -ADAM_LR * (m_hat / (jnp.sqrt(v_hat) + ADAM_EPS) + ADAM_WD * w)
    return delta, m, v


def _adamw_small(items, name):
    n = len(items)

    def body(*refs):
        for a in range(n):
            w, m, v, g = (refs[4 * a + k][...] for k in range(4))
            d, m, v = _adamw_math(w, g, m, v)
            refs[4 * n + 3 * a][...] = d
            refs[4 * n + 3 * a + 1][...] = m
            refs[4 * n + 3 * a + 2][...] = v

    vm = pl.BlockSpec(memory_space=pltpu.VMEM)
    flat = [t for it in items for t in it]
    outs = pl.pallas_call(
        body, name=name,
        out_shape=[jax.ShapeDtypeStruct(it[0].shape, F32) for it in items for _ in range(3)],
        in_specs=[vm] * (4 * n), out_specs=[vm] * (3 * n), compiler_params=_params(),
    )(*flat)
    return [tuple(outs[3 * a:3 * a + 3]) for a in range(n)]


def _adamw_big(w, m, v, owns, landeds, name, rider=None):
    L, R, C = w.shape
    tr = _tile(R, 512)
    nr = R // tr
    counts = [len(ls) for ls in landeds]

    def body(*refs):
        w_ref, m_ref, v_ref = refs[:3]
        g_ref, d_ref, mo_ref, vo_ref = refs[-4:]
        layer = pl.program_id(0)
        at = 3
        for j in range(L):
            own_ref, l_refs = refs[at], refs[at + 1:at + 1 + counts[j]]
            at += 1 + counts[j]

            @pl.when(layer == j)
            def _(own_ref=own_ref, l_refs=l_refs):
                g = own_ref[...]
                for l_ref in l_refs:
                    for k in range(l_ref.shape[0]):
                        g = g + l_ref[k].astype(F32)
                d, mn, vn = _adamw_math(w_ref[0], g, m_ref[0], v_ref[0])
                g_ref[0] = g
                d_ref[0] = d
                mo_ref[0] = mn
                vo_ref[0] = vn

    def pinned(j):
        return lambda l, i: jnp.where(l == j, i, jnp.where(l < j, 0, nr - 1))

    row = pl.BlockSpec((1, tr, C), lambda l, i: (l, i, 0))
    in_specs, args = [row, row, row], [w, m, v]
    for j in range(L):
        in_specs.append(pl.BlockSpec((tr, C), lambda l, i, p=pinned(j): (p(l, i), 0)))
        args.append(owns[j])
        for arr in landeds[j]:
            in_specs.append(pl.BlockSpec((arr.shape[0], tr, C), lambda l, i, p=pinned(j): (0, p(l, i), 0)))
            args.append(arr)
    return _call(body, name=name, grid=(L, nr), in_specs=in_specs, out_specs=[row] * 4,
                 out_shape=[jax.ShapeDtypeStruct((L, R, C), F32)] * 4, args=args, rider=rider)


def _adamw_transposed(wt, mt, vt, owns, landeds, name, rider=None):
    L, C, R = wt.shape
    tc = _tile(R, 256)
    nr = R // tc
    counts = [len(ls) for ls in landeds]

    def body(*refs):
        w_ref, m_ref, v_ref = refs[:3]
        g_ref, d_ref, mo_ref, vo_ref = refs[-4:]
        layer = pl.program_id(0)
        eye = (lax.broadcasted_iota(jnp.int32, (tc, tc), 0) == lax.broadcasted_iota(jnp.int32, (tc, tc), 1)).astype(BF16)
        at = 3
        for j in range(L):
            own_ref, l_refs = refs[at], refs[at + 1:at + 1 + counts[j]]
            at += 1 + counts[j]

            @pl.when(layer == j)
            def _(own_ref=own_ref, l_refs=l_refs):
                hi, mid, lo = _split3(own_ref[...])
                g = (_dot_tn(hi, eye) + _dot_tn(mid, eye)) + _dot_tn(lo, eye)
                for l_ref in l_refs:
                    for k in range(l_ref.shape[0]):
                        g = g + _dot_tn(l_ref[k], eye)
                d, mn, vn = _adamw_math(w_ref[0], g, m_ref[0], v_ref[0])
                g_ref[0] = g
                d_ref[0] = d
                mo_ref[0] = mn
                vo_ref[0] = vn

    def pinned(j):
        return lambda l, i: jnp.where(l == j, i, jnp.where(l < j, 0, nr - 1))

    col = pl.BlockSpec((1, C, tc), lambda l, i: (l, 0, i))
    in_specs, args = [col, col, col], [wt, mt, vt]
    for j in range(L):
        in_specs.append(pl.BlockSpec((tc, C), lambda l, i, p=pinned(j): (p(l, i), 0)))
        args.append(owns[j])
        for arr in landeds[j]:
            in_specs.append(pl.BlockSpec((arr.shape[0], tc, C), lambda l, i, p=pinned(j): (0, p(l, i), 0)))
            args.append(arr)
    return _call(body, name=name, grid=(L, nr), in_specs=in_specs, out_specs=[col] * 4,
                 out_shape=[jax.ShapeDtypeStruct((L, C, R), F32)] * 4, args=args, rider=rider)


def kernel(x, a_norm_g, a_w_in, a_v_norm_g, a_w_s, a_b_s, a_w_out, kv_norm_g, w_kv, b_norm_g, b_w_q, b_rel_bias, b_w_o, f_norm_g, f_w_in, f_conv_w, f_conv_b, f_w_down, final_norm_g, loss_target, m_a_norm_g, m_a_w_in, m_a_v_norm_g, m_a_w_s, m_a_b_s, m_a_w_out, m_kv_norm_g, m_w_kv, m_b_norm_g, m_b_w_q, m_b_rel_bias, m_b_w_o, m_f_norm_g, m_f_w_in, m_f_conv_w, m_f_conv_b, m_f_w_down, m_final_norm_g, v_a_norm_g, v_a_w_in, v_a_v_norm_g, v_a_w_s, v_a_b_s, v_a_w_out, v_kv_norm_g, v_w_kv, v_b_norm_g, v_b_w_q, v_b_rel_bias, v_b_w_o, v_f_norm_g, v_f_w_in, v_f_conv_w, v_f_conv_b, v_f_w_down, v_final_norm_g):
    B, S, D = x.shape
    T = B * S
    G = a_w_s.shape[1]
    H = D // HEAD_DIM
    F = f_w_in.shape[2]
    L = f_w_in.shape[0]
    dn = D // N_DEV
    xi, yi, ci = lax.axis_index("x"), lax.axis_index("y"), lax.axis_index("c")
    me = 4 * xi + 2 * yi + ci
    pos = jnp.stack([ci, 2 * xi + yi]).astype(jnp.int32)

    cast = lambda t: t.astype(BF16)
    gather = lambda *ts: _gather_rider(list(ts))
    rel = jnp.pad(b_rel_bias[0], ((0, 0), (0, REL_PAD - b_rel_bias.shape[2])))
    bias, (wa_in, norms_sh, conv_w0, conv_w1) = _bias_build(rel, "bias_build", rider=gather(
        cast(a_w_in[0]), jnp.concatenate([a_norm_g, a_v_norm_g], axis=0), f_conv_w[0], f_conv_w[1]))
    ga = jnp.transpose(norms_sh, (1, 0, 2)).reshape(2, D)
    g_a, g_av = ga[0:1], ga[1:2]

    x2 = x.reshape(T, D)
    tgt = loss_target.reshape(T, D)
    pc = jnp.arange(GMLP_BLOCK) // CHUNK
    mask = (pc[:, None] >= pc[None, :]).astype(F32)
    ws = (a_w_s[0] * mask[None]).astype(BF16)
    bst = jnp.transpose(a_b_s[0])
    four = lambda t: t.reshape((4, 2) + t.shape[1:])

    (z, n_a), (wa_out, w_in0) = _norm_matmul(x2, g_a, wa_in, flat=True, nbk=4, name="gmlp_in",
                                             rider=gather(cast(a_w_out[0]), cast(f_w_in[0])))
    wa_out = wa_out.reshape(D, D)
    (gated, h1), (wf_down0,) = _gmlp_forward(z, ws, bst, g_av, wa_out, x2, name="gmlp_mix",
                                             rider=gather(cast(f_w_down[0])))
    cw0, cb0, wd0 = conv_w0, f_conv_b[0].reshape(8, 1, F), wf_down0.reshape(4, F, D)
    (au0, ag0, cu0, cg0, act0, n_f0, h2), (wkv, wq, w_in1) = _ffn_fused_forward(
        h1, f_norm_g[0:1], w_in0, cw0, cb0, wd0, S, name="ffn0_fwd",
        rider=gather(cast(w_kv), cast(b_w_q[0]), cast(f_w_in[1])))
    wq = wq.reshape(D, D)
    kv, n_kv = _norm_matmul(h2, kv_norm_g.reshape(1, D), wkv, flat=True, nbk=4, name="kv_proj")
    q, n_q = _norm_matmul(h2, b_norm_g, wq.reshape(1, D, D), flat=True, nbk=1, name="q_proj", scale=HEAD_DIM ** -0.5)
    o, (wo, wf_down1) = _attn_forward(q, kv, bias, S, name="attn", rider=gather(cast(b_w_o[0]), cast(f_w_down[1])))
    wo = wo.reshape(D, D)
    cw1, cb1, wd1 = conv_w1, f_conv_b[1].reshape(8, 1, F), wf_down1.reshape(4, F, D)
    h3 = _matmul_residual(o, wo, h2, "attn_out")
    au1, ag1, cu1, cg1, act1, n_f1, dh4, st_final = _ffn_fused_forward(
        h3, f_norm_g[1:2], w_in1, cw1, cb1, wd1, S, name="ffn1_fwd", loss=(final_norm_g.reshape(1, D), tgt))

    sums, from_chips = {}, {}

    def sibling_sums(names, parts, landed):
        for nm, p, l in zip(names, parts, landed):
            sums[nm] = _sibling_sum(p, l, pos, "grad_sibling_sum_" + nm)

    def chip_rider(*names):
        return _chip_rider([sums[nm][1] for nm in names])

    g_wd1 = _wgrad_rows(act1, dh4, flat=False, tk=F, name="ffn1_dwdown")
    parts = [four(g_wd1.reshape(8, F // 2, D))]
    (dau1, dag1, st_conv1, dh3, st_f1), landed = _ffn_fused_backward(
        dh4, cu1, cg1, au1, ag1, cw1, wd1, w_in1, h3, f_norm_g[1:2], S, name="ffn1_bwd", rider=_sibling_rider(parts))
    sibling_sums(["wd1"], parts, landed)
    g_win1, (from_chips["wd1"],) = _wgrad_cols(n_f1, (dau1, dag1), flat=False, nb=8, nbk=2, name="ffn1_dwin",
                                               rider=chip_rider("wd1"))
    d_o = _matmul_nt(dh3, wo.reshape(1, D, D), flat=True, nbk=1, name="attn_out_dx")
    parts = [four(g_win1)]
    g_wo, landed = _wgrad_rows(o, dh3, flat=True, tk=_tile(D, 512), name="attn_out_dw", rider=_sibling_rider(parts))
    sibling_sums(["win1"], parts, landed)
    (dq, dk, dv, dbias), (from_chips["win1"],) = _attn_backward(
        q, kv, bias, d_o, S, name="attn_bwd", rider=chip_rider("win1"))
    g_rel = _bias_reduce(dbias, "bias_reduce")
    g_wq = _wgrad_cols(n_q, dq, flat=True, nb=1, nbk=1, name="q_dw")
    dh2, st_b = _matmul_nt(dq, wq.reshape(1, D, D), flat=True, nbk=1, name="q_dx", norm=(h2, b_norm_g, dh3))
    g_wkv = _wgrad_cols(n_kv, (dk, dv), flat=True, nb=8, nbk=4, name="kv_dw")
    parts = [four(g_wo.reshape(8, dn, D)), four(g_wq.reshape(8, dn, D)), four(g_wkv)]
    (dh2, st_kv), landed = _matmul_nt((dk, dv), wkv, flat=True, nbk=4, name="kv_dx",
                                      norm=(h2, kv_norm_g.reshape(1, D), dh2), rider=_sibling_rider(parts))
    sibling_sums(["wo", "wq", "wkv"], parts, landed)
    g_wd0, (from_chips["wo"], from_chips["wq"], from_chips["wkv"]) = _wgrad_rows(
        act0, dh2, flat=False, tk=F, name="ffn0_dwdown", rider=chip_rider("wo", "wq", "wkv"))
    parts = [four(g_wd0.reshape(8, F // 2, D))]
    (dau0, dag0, st_conv0, dh1, st_f0), landed = _ffn_fused_backward(
        dh2, cu0, cg0, au0, ag0, cw0, wd0, w_in0, h1, f_norm_g[0:1], S, name="ffn0_bwd", rider=_sibling_rider(parts))
    sibling_sums(["wd0"], parts, landed)
    g_win0, (ce,) = _wgrad_cols(n_f0, (dau0, dag0), flat=False, nb=8, nbk=2, name="ffn0_dwin",
                                rider=chip_rider("wd0"))
    from_chips["wd0"] = [ce]
    dgated = _matmul_nt(dh1, wa_out.reshape(1, D, D), flat=True, nbk=1, name="gmlp_out_dx")
    parts = [four(g_win0)]
    g_wa_out, landed = _wgrad_rows(gated, dh1, flat=True, tk=_tile(D, 512), name="gmlp_out_dw",
                                   rider=_sibling_rider(parts))
    sibling_sums(["win0"], parts, landed)
    parts = [four(g_wa_out.reshape(8, dn, D))]
    (dz, g_ws, g_bst, st_av), (ce_win0_a, landed) = _gmlp_backward(
        z, dgated, ws, bst, g_av, mask, name="gmlp_bwd",
        rider=_join_riders([_chip_rider([sums["win0"][1]], ks=(1, 2)), _sibling_rider(parts)]))
    sibling_sums(["wa_out"], parts, [landed])
    g_wa_in, (ce_win0_b,) = _wgrad_cols(n_a, dz, flat=True, nb=8, nbk=4, name="gmlp_in_dw",
                                        rider=_chip_rider([sums["win0"][1]], ks=(3,)))
    from_chips["win0"] = [ce_win0_a, ce_win0_b]
    vec = jnp.concatenate([st_av[0:1], st_kv[0:1], st_b[0:1], st_f0[0:1], st_f1[0:1], st_final[0:3]], axis=0)
    parts = [four(g_wa_in)]
    (grad_x, st_a), got = _matmul_nt(dz, wa_in, flat=True, nbk=4, name="gmlp_in_dx", norm=(x2, g_a, dh1),
                                     rider=_join_riders([_sibling_rider(parts), chip_rider("wa_out"), gather(
                                         vec, cast(g_ws), cast(g_bst), cast(g_rel), cast(st_conv0), cast(st_conv1))]))
    sibling_sums(["wa_in"], parts, got[0:1])
    from_chips["wa_out"], small = [got[1]], got[2:]

    def big_update(names, w, m, v, rider=None):
        shape = w.shape
        r = lambda t: t.reshape((len(names), -1, shape[-1]))
        as_list = lambda t: t if isinstance(t, list) else [t]
        outs = _adamw_big(r(w), r(m), r(v), [sums[nm][0] for nm in names],
                          [as_list(from_chips[nm]) for nm in names], "adamw_" + names[0], rider=rider)
        outs, got = (outs, None) if rider is None else outs
        return [t.reshape(shape) for t in outs], got

    u_f_w_down, (ce, st_a) = big_update(["wd0", "wd1"], f_w_down, m_f_w_down, v_f_w_down,
                                        rider=_join_riders([chip_rider("wa_in"), gather(st_a)]))
    from_chips["wa_in"] = [ce]
    names = ["win0", "win1"]
    tr = lambda t: jnp.swapaxes(t, 1, 2)
    outs = _adamw_transposed(
        tr(f_w_in), tr(m_f_w_in), tr(v_f_w_in), [sums[nm][0] for nm in names],
        [t if isinstance(t, list) else [t] for t in (from_chips[nm] for nm in names)], "adamw_win0")
    u_f_w_in = [tr(t) for t in outs]
    u_a_w_in, _ = big_update(["wa_in"], a_w_in, m_a_w_in, v_a_w_in)
    u_w_kv, _ = big_update(["wkv"], w_kv, m_w_kv, v_w_kv)
    u_a_w_out, _ = big_update(["wa_out"], a_w_out, m_a_w_out, v_a_w_out)
    u_b_w_q, _ = big_update(["wq"], b_w_q, m_b_w_q, v_b_w_q)
    u_b_w_o, _ = big_update(["wo"], b_w_o, m_b_w_o, v_b_w_o)

    vec, g_ws, g_bst, g_rel, st_conv0, st_conv1, st_a = _sum_devices(list(small) + [st_a], "sum_small_grads")
    vec = jnp.concatenate([st_a[0:1], vec[0:7]], axis=0)
    loss = vec[7, 0]
    g_a_norm = lax.dynamic_slice_in_dim(vec[0:1], me * dn, dn, axis=1)
    g_av_norm = lax.dynamic_slice_in_dim(vec[1:2], me * dn, dn, axis=1)
    st_conv = jnp.stack([st_conv0, st_conv1])
    g_conv_w = lax.dynamic_index_in_dim(st_conv, me, axis=1, keepdims=False)[:, 0:3]
    g_conv_b = st_conv[:, :, 3, :].reshape(L, 8 * F)
    small_items = [
        (a_norm_g, m_a_norm_g, v_a_norm_g, g_a_norm),
        (a_v_norm_g, m_a_v_norm_g, v_a_v_norm_g, g_av_norm),
        (a_w_s, m_a_w_s, v_a_w_s, g_ws[None]),
        (a_b_s, m_a_b_s, v_a_b_s, jnp.transpose(g_bst)[None]),
        (kv_norm_g.reshape(1, D), m_kv_norm_g.reshape(1, D), v_kv_norm_g.reshape(1, D), vec[2:3]),
        (b_norm_g, m_b_norm_g, v_b_norm_g, vec[3:4]),
        (b_rel_bias, m_b_rel_bias, v_b_rel_bias, g_rel[None, :, :b_rel_bias.shape[2]]),
        (f_norm_g, m_f_norm_g, v_f_norm_g, vec[4:6]),
        (f_conv_w, m_f_conv_w, v_f_conv_w, g_conv_w),
        (f_conv_b, m_f_conv_b, v_f_conv_b, g_conv_b),
        (final_norm_g.reshape(1, D), m_final_norm_g.reshape(1, D), v_final_norm_g.reshape(1, D), vec[6:7]),
    ]
    small_out = _adamw_small(small_items, "adamw_small")
    (u_a_norm, u_av_norm, u_ws, u_bs, u_kvn, u_bn, u_rel, u_fn, u_cw, u_cb, u_fin) = [
        (it[3],) + so for it, so in zip(small_items, small_out)]
    vecD = lambda u: tuple(t.reshape(D) for t in u)
    u_kvn, u_fin = vecD(u_kvn), vecD(u_fin)

    order = [u_a_norm, u_a_w_in, u_av_norm, u_ws, u_bs, u_a_w_out, u_kvn, u_w_kv, u_bn, u_b_w_q, u_rel, u_b_w_o,
             u_fn, u_f_w_in, u_cw, u_cb, u_f_w_down, u_fin]
    outs = [loss, grad_x.reshape(B, S, D)]
    for k in range(4):
        outs += [u[k] for u in order]
    return tuple(outs)
```
